```python
import math
import jax
import jax.numpy as jnp
from jax import lax
import numpy as np

D_MODEL = 1024
BATCH = 1
SEQ = 16384
DEPTH = 4

CTX_LEN = 256
GRID_W = 64
EPS = 1e-6
ROPE_BASE = 10000.0

RET_HEADS = 4
RET_DK = 128
RET_DV = 256
RET_CHUNK = 128
RET_QK = RET_HEADS * RET_DK
RET_V = RET_HEADS * RET_DV

HY_WIDTH = 512
HY_ORDER = 2
HY_BANDS = 16
HY_EMB = 2 * HY_BANDS + 1
HY_FFN = 64
HY_DECAY_TARGET = 1e-2
HY_FAST_DECAY = 0.3
HY_SLOW_DECAY = 1.5

AB_SPLITS = (RET_QK, 2 * RET_QK, 2 * RET_QK + RET_V, 2 * RET_QK + 2 * RET_V)
AB_IN = 2 * RET_QK + 2 * RET_V + (HY_ORDER + 1) * HY_WIDTH
AB_CAT = RET_V + HY_WIDTH

DA_HEADS = 8
DA_HEAD_DIM = 64
DA_WIDTH = DA_HEADS * 2 * DA_HEAD_DIM
Q_BLOCK = 128

N_EXPERTS = 64
TOP_K = 8
N_GROUPS = 8
TOPK_GROUPS = 4
EXPERT_DIM = 256
SHARED_DIM = 256
ROUTED_SCALE = 2.5
MOE_BLOCK = 128

kernel_name = "hybrid_retention_hyena_diffattn_moe_dit"

F32 = jnp.float32


def rmsnorm(x, g, eps=EPS):
    xf = x.astype(F32)
    y = xf * lax.rsqrt(jnp.mean(xf * xf, axis=-1, keepdims=True) + eps)
    return (y * g.astype(F32)).astype(x.dtype)


def head_norm(r, eps=EPS):
    mu = jnp.mean(r, axis=-1, keepdims=True)
    var = jnp.mean(jnp.square(r - mu), axis=-1, keepdims=True)
    return (r - mu) * lax.rsqrt(var + eps)


def swiglu(x, wg, wu, wd):
    return (jax.nn.silu(x @ wg) * (x @ wu)) @ wd


def axial_rope(rows, head_dim):
    n_freq = head_dim // 4
    inv = ROPE_BASE ** (-jnp.arange(n_freq, dtype=F32) / n_freq)
    row = jnp.repeat(jnp.arange(rows, dtype=F32), GRID_W)
    col = jnp.tile(jnp.arange(GRID_W, dtype=F32), rows)
    ang = jnp.concatenate([row[:, None] * inv, col[:, None] * inv], axis=-1)
    return jnp.cos(ang), jnp.sin(ang)


def apply_rope(x, cos, sin):
    shape = (1, x.shape[1]) + (1,) * (x.ndim - 3) + (cos.shape[-1],)
    cos = cos.reshape(shape).astype(x.dtype)
    sin = sin.reshape(shape).astype(x.dtype)
    x1, x2 = jnp.split(x, 2, axis=-1)
    return jnp.concatenate([x1 * cos - x2 * sin, x2 * cos + x1 * sin], axis=-1)


def retention_scan(q, k, v, log_g, s0):
    B, L, H, dk = q.shape
    dv = v.shape[-1]
    n = L // RET_CHUNK
    qc = q.reshape(B, n, RET_CHUNK, H, dk)
    kc = k.reshape(B, n, RET_CHUNK, H, dk)
    vc = v.reshape(B, n, RET_CHUNK, H, dv)
    i = jnp.arange(RET_CHUNK, dtype=F32)
    rel = i[:, None] - i[None, :]
    dec_in = jnp.where(rel[..., None] >= 0, jnp.exp(jnp.maximum(rel, 0.0)[..., None] * log_g), 0.0)
    s = jnp.einsum('bnihd,bnjhd->bnhij', qc, kc) * dec_in.transpose(2, 0, 1)
    inner = jnp.einsum('bnhij,bnjhe->bnihe', s, vc)
    q_dec = jnp.exp((i + 1.0)[:, None] * log_g)
    k_dec = jnp.exp((RET_CHUNK - 1.0 - i)[:, None] * log_g)
    upd = jnp.einsum('bnjhd,bnjhe->nbhde', kc * k_dec[:, :, None], vc)
    g_chunk = jnp.exp(RET_CHUNK * log_g)[None, :, None, None]

    def step(state, u):
        return g_chunk * state + u, state

    s_final, s_prev = lax.scan(step, s0, upd)
    cross = jnp.einsum('bnihd,nbhde->bnihe', qc * q_dec[:, :, None], s_prev)
    return (inner + cross).reshape(B, L, H, dv), s_final


def bidir_retention(qc, kc, vc, qx, kx, vx, log_g):
    B = qc.shape[0]
    z = jnp.zeros((B, RET_HEADS, RET_DK, RET_DV), F32)
    flip = lambda a: a[:, ::-1]
    yc_f, sc_f = retention_scan(qc, kc, vc, log_g[0], z)
    yx_f, _ = retention_scan(qx, kx, vx, log_g[0], sc_f)
    yc_b, sc_b = retention_scan(flip(qc), flip(kc), flip(vc), log_g[1], z)
    yx_b, _ = retention_scan(flip(qx), flip(kx), flip(vx), log_g[1], sc_b)
    return yc_f + flip(yc_b), yx_f + flip(yx_b)


def short_conv(u, w, b):
    up = jnp.pad(u, ((0, 0), (1, 1), (0, 0)))
    return up[:, :-2] * w[0] + up[:, 1:-1] * w[1] + up[:, 2:] * w[2] + b


def hyena_filters(L, w1, b1, f1, w2, b2, f2, w3):
    t = jnp.arange(L, dtype=F32)
    t_norm = t / max(L - 1, 1)
    bands = jnp.linspace(1e-4, HY_BANDS - 1, HY_BANDS, dtype=F32)
    ang = (2.0 * math.pi / L) * t[:, None] * bands[None, :]
    z = jnp.concatenate([t_norm[:, None], jnp.cos(ang), -jnp.sin(ang)], axis=-1)
    h = jnp.sin(f1 * (z @ w1 + b1))
    h = jnp.sin(f2 * (h @ w2 + b2))
    h = (h @ w3).astype(F32)
    deltas = jnp.abs(jnp.linspace(math.log(HY_DECAY_TARGET) / HY_SLOW_DECAY,
                                  math.log(HY_DECAY_TARGET) / HY_FAST_DECAY, HY_WIDTH, dtype=F32))
    window = jnp.exp(-t_norm[:, None] * deltas[None, :])
    return h.reshape(L, HY_ORDER, 2, HY_WIDTH) * window[:, None, None, :]


def bidir_long_conv(u, h_fwd, h_bwd, skip):
    L, C = h_fwd.shape
    hc = jnp.concatenate([h_fwd, jnp.zeros((1, C), F32), h_bwd[1:][::-1]], axis=0)
    uf = u.astype(F32)
    U = jnp.fft.rfft(uf, n=2 * L, axis=1)
    Hf = jnp.fft.rfft(hc, axis=0)
    y = jnp.fft.irfft(U * Hf[None], n=2 * L, axis=1)[:, :L]
    return (y + uf * skip.astype(F32)).astype(u.dtype)


def hyena(u3, conv_w, conv_b, filt, skip):
    u3 = short_conv(u3, conv_w, conv_b)
    v, x1, x2 = jnp.split(u3, 3, axis=-1)
    y = x1 * bidir_long_conv(v, filt[:, 0, 0], filt[:, 0, 1], skip[0])
    y = x2 * bidir_long_conv(y, filt[:, 1, 0], filt[:, 1, 1], skip[1])
    return y


def mixer_ab(hc, hx, w_in, w_out, decay_logit, conv_w, conv_b, w1, b1, f1, w2, b2, f2, w3, skip, rope, with_ctx):
    qc, kc, vc, gc, uc = jnp.split(hc @ w_in, AB_SPLITS, axis=-1)
    qx, kx, vx, gx, ux = jnp.split(hx @ w_in, AB_SPLITS, axis=-1)

    def heads(q, k, v):
        B, L, _ = q.shape
        return (q.reshape(B, L, RET_HEADS, RET_DK).astype(F32),
                k.reshape(B, L, RET_HEADS, RET_DK).astype(F32) * RET_DK ** -0.5,
                v.reshape(B, L, RET_HEADS, RET_DV).astype(F32))

    qc, kc, vc = heads(qc, kc, vc)
    qx, kx, vx = heads(qx, kx, vx)
    cos, sin = rope
    qx, kx = apply_rope(qx, cos, sin), apply_rope(kx, cos, sin)
    log_g = jax.nn.log_sigmoid(decay_logit.astype(F32))
    rc, rx = bidir_retention(qc, kc, vc, qx, kx, vx, log_g)

    def merge(r, g, u):
        B, L = r.shape[:2]
        ret = head_norm(r).reshape(B, L, RET_V).astype(g.dtype) * jax.nn.silu(g)
        hy = hyena(u, conv_w, conv_b, hyena_filters(L, w1, b1, f1, w2, b2, f2, w3), skip)
        return jnp.concatenate([ret, hy], axis=-1) @ w_out

    yx = merge(rx, gx, ux)
    yc = merge(rc, gc, uc) if with_ctx else None
    return yc, yx


def mixer_da(hc, hx, w_in, w_out, lam, subln, lambda_init, rope, with_ctx):
    def heads(p):
        B, L, _ = p.shape
        q, k, v = jnp.split(p, 3, axis=-1)
        return (q.reshape(B, L, DA_HEADS, 2, DA_HEAD_DIM), k.reshape(B, L, DA_HEADS, 2, DA_HEAD_DIM),
                v.reshape(B, L, DA_HEADS, 2 * DA_HEAD_DIM))

    qc, kc, vc = heads(hc @ w_in)
    qx, kx, vx = heads(hx @ w_in)
    cos, sin = rope
    qx, kx = apply_rope(qx, cos, sin), apply_rope(kx, cos, sin)
    lam_f = lam.astype(F32)
    lam_full = jnp.exp(jnp.sum(lam_f[0] * lam_f[1])) - jnp.exp(jnp.sum(lam_f[2] * lam_f[3])) + lambda_init
    k_all = jnp.concatenate([kc, kx], axis=1)
    v_all = jnp.concatenate([vc, vx], axis=1)

    def attend(q, k, v):
        s = jnp.einsum('bqhcd,bkhcd->bhcqk', q, k).astype(F32) * DA_HEAD_DIM ** -0.5
        p = jax.nn.softmax(s, axis=-1)
        a = p[:, :, 0] - lam_full * p[:, :, 1]
        return jnp.einsum('bhqk,bkhe->bqhe', a.astype(v.dtype), v)

    def finish(o):
        B, L = o.shape[:2]
        o = rmsnorm(o, subln, eps=1e-5) * (1.0 - lambda_init)
        return o.reshape(B, L, DA_WIDTH) @ w_out

    B, Lx = qx.shape[:2]
    nb = Lx // Q_BLOCK
    qb = qx.reshape(B, nb, Q_BLOCK, DA_HEADS, 2, DA_HEAD_DIM).transpose(1, 0, 2, 3, 4, 5)
    ox = lax.map(lambda qq: attend(qq, k_all, v_all), qb)
    ox = ox.transpose(1, 0, 2, 3, 4).reshape(B, Lx, DA_HEADS, 2 * DA_HEAD_DIM)
    yx = finish(ox)
    yc = finish(attend(qc, kc, vc)) if with_ctx else None
    return yc, yx


def route(xf, w_r, bias):
    N = xf.shape[0]
    scores = jax.nn.sigmoid((xf @ w_r).astype(F32))
    choice = scores + bias.astype(F32)
    grp = choice.reshape(N, N_GROUPS, N_EXPERTS // N_GROUPS)
    gscore = jnp.sum(lax.top_k(grp, 2)[0], axis=-1)
    _, gidx = lax.top_k(gscore, TOPK_GROUPS)
    gmask = jnp.any(gidx[:, :, None] == jnp.arange(N_GROUPS)[None, None, :], axis=1)
    emask = jnp.repeat(gmask, N_EXPERTS // N_GROUPS, axis=1)
    _, eidx = lax.top_k(jnp.where(emask, choice, -jnp.inf), TOP_K)
    w = jnp.take_along_axis(scores, eidx, axis=-1)
    w = w / (jnp.sum(w, axis=-1, keepdims=True) + 1e-20) * ROUTED_SCALE
    return eidx, w


def routed_experts(xf, eidx, w, wg, wu, wd):
    N, D = xf.shape
    A = N * TOP_K
    e_flat = eidx.reshape(A)
    tok_flat = jnp.repeat(jnp.arange(N, dtype=jnp.int32), TOP_K)
    w_flat = w.reshape(A)
    order = jnp.argsort(e_flat)
    e_sorted = e_flat[order]
    counts = jnp.bincount(e_flat, length=N_EXPERTS)
    padded = (counts + MOE_BLOCK - 1) // MOE_BLOCK * MOE_BLOCK
    pad_end = jnp.cumsum(padded)
    pad_start = pad_end - padded
    start = jnp.cumsum(counts) - counts
    dest = pad_start[e_sorted] + jnp.arange(A, dtype=jnp.int32) - start[e_sorted]
    n_blocks = -(-(A + N_EXPERTS * (MOE_BLOCK - 1)) // MOE_BLOCK)
    P = n_blocks * MOE_BLOCK
    slot_tok = jnp.full((P,), N, jnp.int32).at[dest].set(tok_flat[order])
    slot_w = jnp.zeros((P,), F32).at[dest].set(w_flat[order])
    block_expert = jnp.minimum(jnp.searchsorted(pad_end, jnp.arange(n_blocks, dtype=jnp.int32) * MOE_BLOCK,
                                                side='right'), N_EXPERTS - 1)
    x_pad = jnp.concatenate([xf, jnp.zeros((1, D), xf.dtype)], axis=0)

    def step(acc, blk):
        toks, ws, e = blk
        yb = swiglu(x_pad[toks], wg[e], wu[e], wd[e]) * ws[:, None].astype(xf.dtype)
        return acc.at[toks].add(yb), None

    acc, _ = lax.scan(step, jnp.zeros((N + 1, D), xf.dtype),
                      (slot_tok.reshape(n_blocks, MOE_BLOCK), slot_w.reshape(n_blocks, MOE_BLOCK), block_expert))
    return acc[:N]


def moe(h, w_r, b_r, wg, wu, wd, swg, swu, swd):
    eidx, w = route(h, w_r, b_r)
    return swiglu(h, swg, swu, swd) + routed_experts(h, eidx, w, wg, wu, wd)


def setup_inputs(seed: int = 0) -> dict:
    key = jax.random.key(seed)
    ks = iter(jax.random.split(key, 40))
    nrm = lambda shape, scale: jax.random.normal(next(ks), shape, jnp.float32) * scale
    D = D_MODEL
    ne, no = (DEPTH + 1) // 2, DEPTH // 2
    ret_base = jnp.log(2.0 ** (5.0 + 2.0 * jnp.arange(RET_HEADS, dtype=jnp.float32)) - 1.0)
    return {
        "x": nrm((BATCH, SEQ, D), 1.0),
        "c": nrm((BATCH, D), 1.0),
        "ctx": nrm((BATCH, CTX_LEN, D), 1.0),
        "c_ctx": nrm((D,), 1.0),
        "w_ada": nrm((DEPTH, D, 6 * D), 0.5 * D ** -0.5),
        "b_ada": nrm((DEPTH, 6 * D), 0.02),
        "norm_mix": 1.0 + nrm((DEPTH, D), 0.05),
        "norm_ffn": 1.0 + nrm((DEPTH, D), 0.05),
        "ab_w_in": nrm((ne, D, AB_IN), D ** -0.5),
        "ab_w_out": nrm((ne, AB_CAT, D), AB_CAT ** -0.5),
        "ret_decay_logit": ret_base + nrm((ne, 2, RET_HEADS), 0.1),
        "hy_conv_w": nrm((ne, 3, (HY_ORDER + 1) * HY_WIDTH), 0.5),
        "hy_conv_b": nrm((ne, (HY_ORDER + 1) * HY_WIDTH), 0.02),
        "hy_w1": nrm((ne, HY_EMB, HY_FFN), HY_EMB ** -0.5),
        "hy_b1": nrm((ne, HY_FFN), 0.1),
        "hy_freq1": 1.0 + nrm((ne, HY_FFN), 0.1),
        "hy_w2": nrm((ne, HY_FFN, HY_FFN), HY_FFN ** -0.5),
        "hy_b2": nrm((ne, HY_FFN), 0.1),
        "hy_freq2": 1.0 + nrm((ne, HY_FFN), 0.1),
        "hy_w3": nrm((ne, HY_FFN, HY_ORDER * 2 * HY_WIDTH), 0.005),
        "hy_skip": nrm((ne, HY_ORDER, HY_WIDTH), 0.5),
        "da_w_in": nrm((no, D, 3 * DA_WIDTH), D ** -0.5),
        "da_w_out": nrm((no, DA_WIDTH, D), DA_WIDTH ** -0.5),
        "da_lambda": nrm((no, 4, DA_HEAD_DIM), 0.1),
        "da_subln": 1.0 + nrm((no, 2 * DA_HEAD_DIM), 0.05),
        "router_w": nrm((DEPTH, D, N_EXPERTS), D ** -0.5),
        "router_b": nrm((DEPTH, N_EXPERTS), 0.01),
        "exp_w_gate": nrm((DEPTH, N_EXPERTS, D, EXPERT_DIM), D ** -0.5),
        "exp_w_up": nrm((DEPTH, N_EXPERTS, D, EXPERT_DIM), D ** -0.5),
        "exp_w_down": nrm((DEPTH, N_EXPERTS, EXPERT_DIM, D), EXPERT_DIM ** -0.5),
        "sh_w_gate": nrm((DEPTH, D, SHARED_DIM), D ** -0.5),
        "sh_w_up": nrm((DEPTH, D, SHARED_DIM), D ** -0.5),
        "sh_w_down": nrm((DEPTH, SHARED_DIM, D), SHARED_DIM ** -0.5),
        "norm_final": 1.0 + nrm((D,), 0.05),
    }


def reference(x, c, ctx, c_ctx, w_ada, b_ada, norm_mix, norm_ffn, ab_w_in, ab_w_out, ret_decay_logit,
              hy_conv_w, hy_conv_b, hy_w1, hy_b1, hy_freq1, hy_w2, hy_b2, hy_freq2, hy_w3, hy_skip,
              da_w_in, da_w_out, da_lambda, da_subln, router_w, router_b, exp_w_gate, exp_w_up, exp_w_down,
              sh_w_gate, sh_w_up, sh_w_down, norm_final):
    B, L, D = x.shape
    Lc = ctx.shape[1]
    rows = L // GRID_W
    rope_ret = axial_rope(rows, RET_DK)
    rope_da = axial_rope(rows, DA_HEAD_DIM)
    for i in range(DEPTH):
        with_ctx = i < DEPTH - 1
        j = i // 2
        mod_x = jax.nn.silu(c) @ w_ada[i] + b_ada[i]
        mod_c = jax.nn.silu(c_ctx) @ w_ada[i] + b_ada[i]
        sx = jnp.split(mod_x[:, None, :], 6, axis=-1)
        sc = jnp.split(mod_c, 6, axis=-1)
        hx = rmsnorm(x, norm_mix[i]) * (1.0 + sx[1]) + sx[0]
        hc = rmsnorm(ctx, norm_mix[i]) * (1.0 + sc[1]) + sc[0]
        if i % 2 == 0:
            yc, yx = mixer_ab(hc, hx, ab_w_in[j], ab_w_out[j], ret_decay_logit[j], hy_conv_w[j], hy_conv_b[j],
                              hy_w1[j], hy_b1[j], hy_freq1[j], hy_w2[j], hy_b2[j], hy_freq2[j], hy_w3[j], hy_skip[j],
                              rope_ret, with_ctx)
        else:
            lambda_init = 0.8 - 0.6 * math.exp(-0.3 * i)
            yc, yx = mixer_da(hc, hx, da_w_in[j], da_w_out[j], da_lambda[j], da_subln[j], lambda_init,
                              rope_da, with_ctx)
        x = x + sx[2] * yx
        hx = rmsnorm(x, norm_ffn[i]) * (1.0 + sx[4]) + sx[3]
        moe_w = (router_w[i], router_b[i], exp_w_gate[i], exp_w_up[i], exp_w_down[i],
                 sh_w_gate[i], sh_w_up[i], sh_w_down[i])
        if with_ctx:
            ctx = ctx + sc[2] * yc
            hc = rmsnorm(ctx, norm_ffn[i]) * (1.0 + sc[4]) + sc[3]
            h_all = jnp.concatenate([hc.reshape(B * Lc, D), hx.reshape(B * L, D)], axis=0)
            y_all = moe(h_all, *moe_w)
            ctx = ctx + sc[5] * y_all[:B * Lc].reshape(B, Lc, D)
            x = x + sx[5] * y_all[B * Lc:].reshape(B, L, D)
        else:
            x = x + sx[5] * moe(hx.reshape(B * L, D), *moe_w).reshape(B, L, D)
    return rmsnorm(x, norm_final)
```

```python
import functools
import math

import jax
import jax.numpy as jnp
from jax import lax
from jax.experimental import pallas as pl
from jax.experimental.pallas import tpu as pltpu

F32 = jnp.float32
BF16 = jnp.bfloat16
HIGHEST = lax.Precision.HIGHEST

D_MODEL = 1024
DEPTH = 4
GRID_W = 64
EPS = 1e-6
ROPE_BASE = 10000.0

RET_HEADS = 4
RET_DK = 128
RET_DV = 256
RET_CHUNK = 128
RET_QK = RET_HEADS * RET_DK
RET_V = RET_HEADS * RET_DV

HY_WIDTH = 512
HY_ORDER = 2
HY_BANDS = 16
HY_EMB = 2 * HY_BANDS + 1
HY_FFN = 64
HY_DECAY_TARGET = 1e-2
HY_FAST_DECAY = 0.3
HY_SLOW_DECAY = 1.5
HY_ZCOLS = 64
HY_VALID_COL = HY_EMB
FFT_N2 = 128

AB_IN = 2 * RET_QK + 2 * RET_V + (HY_ORDER + 1) * HY_WIDTH
AB_CAT = RET_V + HY_WIDTH

DA_HEADS = 8
DA_HEAD_DIM = 64
DA_WIDTH = DA_HEADS * 2 * DA_HEAD_DIM

N_EXPERTS = 64
TOP_K = 8
N_GROUPS = 8
TOPK_GROUPS = 4
GROUP_SIZE = N_EXPERTS // N_GROUPS
EXPERT_DIM = 256
ROUTED_SCALE = 2.5

LANE = 128
SUBLANE = 8
ROW_TILE = 256
MAX_TOKEN_TILE = 1280
VMEM_LIMIT = 48 * 1024 * 1024
NEG_BIG = -1e30


def _params(sem):
    return pltpu.CompilerParams(dimension_semantics=sem, vmem_limit_bytes=VMEM_LIMIT)


def _token_tile(n):
    best = ROW_TILE
    t = ROW_TILE
    while t <= min(n, MAX_TOKEN_TILE):
        if n % t == 0:
            best = t
        t += ROW_TILE
    return best


def _mm_kernel(*refs, n_pairs, has_epi):
    acc = None
    for p in range(n_pairs):
        a = refs[2 * p][...].astype(BF16)
        b = refs[2 * p + 1][...].astype(BF16)
        d = jnp.dot(a, b, preferred_element_type=F32)
        acc = d if acc is None else acc + d
    idx = 2 * n_pairs
    if has_epi:
        acc = refs[idx][...] * (acc + refs[idx + 1][...] * refs[idx + 2][...])
        idx += 3
    o_ref = refs[idx]
    o_ref[...] = acc.astype(o_ref.dtype)


def mm(pairs, out_dtype, tm, tn, epi=None, name="mm"):
    m = pairs[0][0].shape[0]
    n = pairs[0][1].shape[1]
    assert m % tm == 0 and n % tn == 0
    in_specs, args = [], []
    for a, b in pairs:
        k = a.shape[1]
        in_specs += [pl.BlockSpec((tm, k), lambda i, j: (i, 0)), pl.BlockSpec((k, tn), lambda i, j: (0, j))]
        args += [a, b]
    if epi is not None:
        in_specs += [pl.BlockSpec((tm, tn), lambda i, j: (i, j)), pl.BlockSpec((1, tn), lambda i, j: (0, j)),
                     pl.BlockSpec((tm, tn), lambda i, j: (i, j))]
        args += list(epi)
    return pl.pallas_call(
        functools.partial(_mm_kernel, n_pairs=len(pairs), has_epi=epi is not None),
        grid=(m // tm, n // tn),
        in_specs=in_specs,
        out_specs=pl.BlockSpec((tm, tn), lambda i, j: (i, j)),
        out_shape=jax.ShapeDtypeStruct((m, n), out_dtype),
        compiler_params=_params(("parallel", "parallel")),
        name=name,
    )(*args)


def _adaln_kernel(cv_ref, w_ref, b_ref, o_ref):
    cv = cv_ref[...]
    s = cv * jax.nn.sigmoid(cv)
    o_ref[0] = jnp.dot(s, w_ref[0], precision=HIGHEST, preferred_element_type=F32) + b_ref[0]


def adaln(cv, w_ada, b_ada):
    depth, d, n = w_ada.shape
    tn = 1536
    return pl.pallas_call(
        _adaln_kernel,
        grid=(depth, n // tn),
        in_specs=[pl.BlockSpec((SUBLANE, d), lambda l, j: (0, 0)),
                  pl.BlockSpec((1, d, tn), lambda l, j: (l, 0, j)),
                  pl.BlockSpec((1, 1, tn), lambda l, j: (l, 0, j))],
        out_specs=pl.BlockSpec((1, SUBLANE, tn), lambda l, j: (l, 0, j)),
        out_shape=jax.ShapeDtypeStruct((depth, SUBLANE, n), F32),
        compiler_params=_params(("parallel", "parallel")),
        name="adaln",
    )(cv, w_ada, b_ada.reshape(depth, 1, n))


def _norm_mod_kernel(*refs, has_delta, gate_idx, shift_idx, scale_idx, has_router, write_xs):
    it = iter(refs)
    xs_ref = next(it)
    x = xs_ref[...]
    if has_delta:
        delta_ref = next(it)
        gmods_ref = next(it)
        x = x + gmods_ref[0, gate_idx:gate_idx + 1, :] * delta_ref[...]
    mods_ref = next(it) if shift_idx is not None else None
    g_ref = next(it)
    wr_ref = next(it) if has_router else None
    if write_xs:
        next(it)[...] = x
    h_ref = next(it)
    y = x * lax.rsqrt(jnp.mean(x * x, axis=-1, keepdims=True) + EPS) * g_ref[...]
    if shift_idx is not None:
        y = y * (1.0 + mods_ref[0, scale_idx:scale_idx + 1, :]) + mods_ref[0, shift_idx:shift_idx + 1, :]
    h_ref[...] = y.astype(h_ref.dtype)
    if has_router:
        lg_ref = next(it)
        lg_ref[...] = lax.dot_general(wr_ref[...], y, (((1,), (1,)), ((), ())),
                                      precision=HIGHEST, preferred_element_type=F32)


def norm_mod(xs, g, *, n_rows, ctx_tile, delta=None, gate_mods=None, gate_idx=None, mods=None, shift_idx=None,
             scale_idx=None, router_wt=None, out_dtype=BF16):
    d = xs.shape[1]
    n_tiles = n_rows // ROW_TILE
    row = pl.BlockSpec((ROW_TILE, d), lambda i: (i, 0))
    mod_spec = pl.BlockSpec((1, 6, d), lambda i: (jnp.where(i == ctx_tile, 0, 1), 0, 0))
    in_specs, args = [row], [xs]
    has_delta = delta is not None
    if has_delta:
        in_specs += [row, mod_spec]
        args += [delta, gate_mods]
    if shift_idx is not None:
        in_specs.append(mod_spec)
        args.append(mods)
    in_specs.append(pl.BlockSpec((1, d), lambda i: (0, 0)))
    args.append(g.reshape(1, d))
    has_router = router_wt is not None
    if has_router:
        in_specs.append(pl.BlockSpec(router_wt.shape, lambda i: (0, 0)))
        args.append(router_wt)
    out_specs, out_shape = [], []
    if has_delta:
        out_specs.append(row)
        out_shape.append(jax.ShapeDtypeStruct((n_rows, d), F32))
    out_specs.append(row)
    out_shape.append(jax.ShapeDtypeStruct((n_rows, d), out_dtype))
    if has_router:
        out_specs.append(pl.BlockSpec((N_EXPERTS, ROW_TILE), lambda i: (0, i)))
        out_shape.append(jax.ShapeDtypeStruct((N_EXPERTS, n_rows), F32))
    return pl.pallas_call(
        functools.partial(_norm_mod_kernel, has_delta=has_delta, gate_idx=gate_idx, shift_idx=shift_idx,
                          scale_idx=scale_idx, has_router=has_router, write_xs=has_delta),
        grid=(n_tiles,),
        in_specs=in_specs,
        out_specs=out_specs,
        out_shape=out_shape,
        compiler_params=_params(("parallel",)),
        name="norm_mod",
    )(*args)


def _rope_kernel(p_ref, cos_ref, sin_ref, o_ref, *, n_rot_blocks, head_dim, scales):
    cos = cos_ref[...]
    sin = sin_ref[...]
    for b in range(len(scales)):
        x = p_ref[:, b * LANE:(b + 1) * LANE]
        if b < n_rot_blocks:
            if head_dim == LANE:
                rot = pltpu.roll(x, LANE // 2, 1)
            else:
                lane = lax.broadcasted_iota(jnp.int32, x.shape, 1)
                first_half = (lane % head_dim) < head_dim // 2
                rot = jnp.where(first_half, pltpu.roll(x, LANE - head_dim // 2, 1), pltpu.roll(x, head_dim // 2, 1))
            x = x * cos + rot * sin
        if scales[b] != 1.0:
            x = x * scales[b]
        o_ref[:, b * LANE:(b + 1) * LANE] = x.astype(o_ref.dtype)


def rope_cast(p, cos, sin, *, width, n_rot_blocks, head_dim, scales, out_dtype):
    n_rows = p.shape[0]
    return pl.pallas_call(
        functools.partial(_rope_kernel, n_rot_blocks=n_rot_blocks, head_dim=head_dim, scales=scales),
        grid=(n_rows // ROW_TILE,),
        in_specs=[pl.BlockSpec((ROW_TILE, width), lambda i: (i, 0)),
                  pl.BlockSpec((ROW_TILE, LANE), lambda i: (i, 0)),
                  pl.BlockSpec((ROW_TILE, LANE), lambda i: (i, 0))],
        out_specs=pl.BlockSpec((ROW_TILE, width), lambda i: (i, 0)),
        out_shape=jax.ShapeDtypeStruct((n_rows, width), out_dtype),
        compiler_params=_params(("parallel",)),
        name="rope_cast",
    )(p, cos, sin)


def rope_tables(seq, ctx_len, head_dim):
    n_freq = head_dim // 4
    inv = ROPE_BASE ** (-jnp.arange(n_freq, dtype=F32) / n_freq)
    rows = seq // GRID_W
    row = jnp.repeat(jnp.arange(rows, dtype=F32), GRID_W)
    col = jnp.tile(jnp.arange(GRID_W, dtype=F32), rows)
    ang = jnp.concatenate([row[:, None] * inv, col[:, None] * inv], axis=-1)
    cos, sin = jnp.cos(ang), jnp.sin(ang)
    cos = jnp.concatenate([cos, cos], axis=-1)
    sin = jnp.concatenate([-sin, sin], axis=-1)
    reps = LANE // head_dim
    cos, sin = jnp.tile(cos, (1, reps)), jnp.tile(sin, (1, reps))
    cos = jnp.concatenate([cos, jnp.ones((ctx_len, LANE), F32)], axis=0)
    sin = jnp.concatenate([sin, jnp.zeros((ctx_len, LANE), F32)], axis=0)
    return cos, sin


def _ret_kernel(lg_ref, gc_ref, q_ref, k_ref, v_ref, *rest, reverse):
    if reverse:
        yf_ref, gate_ref, o_ref, s_ref = rest
    else:
        o_ref, s_ref = rest
    c = RET_CHUNK

    @pl.when(pl.program_id(0) == 0)
    def _():
        s_ref[...] = jnp.zeros_like(s_ref)

    ii = lax.broadcasted_iota(jnp.int32, (c, c), 0)
    jj = lax.broadcasted_iota(jnp.int32, (c, c), 1)
    rel = ((jj - ii) if reverse else (ii - jj)).astype(F32)
    pos = lax.broadcasted_iota(jnp.int32, (c, 1), 0).astype(F32)
    for h in range(RET_HEADS):
        lg = lg_ref[h]
        dec = jnp.where(rel >= 0, jnp.exp(jnp.maximum(rel, 0.0) * lg), 0.0)
        if reverse:
            q_dec = jnp.exp((c - pos) * lg)
            k_dec = jnp.exp(pos * lg)
        else:
            q_dec = jnp.exp((pos + 1.0) * lg)
            k_dec = jnp.exp((c - 1.0 - pos) * lg)
        q = q_ref[:, h * RET_DK:(h + 1) * RET_DK]
        k = k_ref[:, h * RET_DK:(h + 1) * RET_DK]
        v = v_ref[:, h * RET_DV:(h + 1) * RET_DV].astype(BF16)
        s = lax.dot_general(q.astype(BF16), k.astype(BF16), (((1,), (1,)), ((), ())),
                            preferred_element_type=F32) * dec
        state = s_ref[h]
        y = jnp.dot(s.astype(BF16), v, preferred_element_type=F32)
        y = y + jnp.dot((q * q_dec).astype(BF16), state.astype(BF16), preferred_element_type=F32)
        upd = lax.dot_general((k * k_dec).astype(BF16), v, (((0,), (0,)), ((), ())), preferred_element_type=F32)
        s_ref[h] = gc_ref[h] * state + upd
        if reverse:
            r = y + yf_ref[:, h * RET_DV:(h + 1) * RET_DV]
            mu = jnp.mean(r, axis=-1, keepdims=True)
            rc = r - mu
            var = jnp.mean(rc * rc, axis=-1, keepdims=True)
            g = gate_ref[:, h * RET_DV:(h + 1) * RET_DV]
            o_ref[:, h * RET_DV:(h + 1) * RET_DV] = (rc * lax.rsqrt(var + EPS) * (g * jax.nn.sigmoid(g))).astype(
                o_ref.dtype)
        else:
            o_ref[:, h * RET_DV:(h + 1) * RET_DV] = y


def retention(qkv, p, log_g, g_chunk, *, seq):
    n_rows = qkv.shape[0]
    n_chunks = n_rows // RET_CHUNK
    n_x = seq // RET_CHUNK
    smem = pl.BlockSpec(memory_space=pltpu.SMEM)

    def run(reverse, extra):
        if reverse:
            idx = lambda t: n_chunks - 1 - t
        else:
            idx = lambda t: (t + n_x) % n_chunks
        in_specs = [smem, smem,
                    pl.BlockSpec((RET_CHUNK, RET_QK), lambda t: (idx(t), 0)),
                    pl.BlockSpec((RET_CHUNK, RET_QK), lambda t: (idx(t), 1)),
                    pl.BlockSpec((RET_CHUNK, RET_V), lambda t: (idx(t), 1))]
        args = [log_g[1 if reverse else 0], g_chunk[1 if reverse else 0], qkv, qkv, qkv]
        if reverse:
            in_specs += [pl.BlockSpec((RET_CHUNK, RET_V), lambda t: (idx(t), 0)),
                         pl.BlockSpec((RET_CHUNK, RET_V), lambda t: (idx(t), 2))]
            args += list(extra)
        return pl.pallas_call(
            functools.partial(_ret_kernel, reverse=reverse),
            grid=(n_chunks,),
            in_specs=in_specs,
            out_specs=pl.BlockSpec((RET_CHUNK, RET_V), lambda t: (idx(t), 0)),
            out_shape=jax.ShapeDtypeStruct((n_rows, RET_V), BF16 if reverse else F32),
            scratch_shapes=[pltpu.VMEM((RET_HEADS, RET_DK, RET_DV), F32)],
            compiler_params=_params(("arbitrary",)),
            name="retention_bwd" if reverse else "retention_fwd",
        )(*args)

    y_fwd = run(False, None)
    return run(True, (y_fwd, p))


def _shortconv_kernel(cur_ref, prev_ref, next_ref, w_ref, b_ref, v_ref, x1_ref, x2_ref, *, x_tiles):
    i = pl.program_id(0)
    cur = cur_ref[...]
    rows = cur.shape[0]
    row = lax.broadcasted_iota(jnp.int32, (rows, 1), 0)
    has_prev = jnp.where((i == 0) | (i == x_tiles), 0.0, 1.0)
    has_next = jnp.where((i == x_tiles - 1) | (i == x_tiles), 0.0, 1.0)
    up = jnp.where(row == 0, prev_ref[SUBLANE - 1:SUBLANE, :] * has_prev, pltpu.roll(cur, 1, 0))
    dn = jnp.where(row == rows - 1, next_ref[0:1, :] * has_next, pltpu.roll(cur, rows - 1, 0))
    y = up * w_ref[0:1, :] + cur * w_ref[1:2, :] + dn * w_ref[2:3, :] + b_ref[...]
    v_ref[...] = y[:, :HY_WIDTH]
    x1_ref[...] = y[:, HY_WIDTH:2 * HY_WIDTH]
    x2_ref[...] = y[:, 2 * HY_WIDTH:]


def shortconv(p, w, b, *, seq):
    n_rows = p.shape[0]
    width = 3 * HY_WIDTH
    col = p.shape[1] // width - 1
    per = ROW_TILE // SUBLANE
    last = n_rows // SUBLANE - 1
    out = jax.ShapeDtypeStruct((n_rows, HY_WIDTH), F32)
    ospec = pl.BlockSpec((ROW_TILE, HY_WIDTH), lambda i: (i, 0))
    return pl.pallas_call(
        functools.partial(_shortconv_kernel, x_tiles=seq // ROW_TILE),
        grid=(n_rows // ROW_TILE,),
        in_specs=[pl.BlockSpec((ROW_TILE, width), lambda i: (i, col)),
                  pl.BlockSpec((SUBLANE, width), lambda i: (jnp.maximum(i * per - 1, 0), col)),
                  pl.BlockSpec((SUBLANE, width), lambda i: (jnp.minimum((i + 1) * per, last), col)),
                  pl.BlockSpec((3, width), lambda i: (0, 0)),
                  pl.BlockSpec((1, width), lambda i: (0, 0))],
        out_specs=[ospec, ospec, ospec],
        out_shape=[out, out, out],
        compiler_params=_params(("parallel",)),
        name="shortconv",
    )(p, p, p, w, b.reshape(1, width))


def _filt_kernel(z_ref, w1_ref, b1_ref, f1_ref, w2_ref, b2_ref, f2_ref, w3_ref, dl_ref, o_ref):
    z = z_ref[...]
    h = jnp.sin(f1_ref[...] * (jnp.dot(z, w1_ref[...], precision=HIGHEST, preferred_element_type=F32) + b1_ref[...]))
    h = jnp.sin(f2_ref[...] * (jnp.dot(h, w2_ref[...], precision=HIGHEST, preferred_element_type=F32) + b2_ref[...]))
    hf = jnp.dot(h, w3_ref[...], precision=HIGHEST, preferred_element_type=F32)
    window = jnp.exp(-z[:, 0:1] * dl_ref[...])
    o_ref[0] = hf * window * z[:, HY_VALID_COL:HY_VALID_COL + 1]


def hyena_filter_taps(length, w1, b1, f1, w2, b2, f2, w3):
    z = _filter_positions(length)
    w1p = jnp.zeros((HY_ZCOLS, HY_FFN), F32).at[:HY_EMB].set(w1)
    deltas = jnp.abs(jnp.linspace(math.log(HY_DECAY_TARGET) / HY_SLOW_DECAY,
                                  math.log(HY_DECAY_TARGET) / HY_FAST_DECAY, HY_WIDTH, dtype=F32)).reshape(1, HY_WIDTH)
    tm = min(length, 512)
    half_tiles = length // tm
    vec = lambda a: a.reshape(1, HY_FFN)
    small = lambda shape: pl.BlockSpec(shape, lambda i, o: (0, 0))
    return pl.pallas_call(
        _filt_kernel,
        grid=(2 * half_tiles, HY_ORDER),
        in_specs=[pl.BlockSpec((tm, HY_ZCOLS), lambda i, o: (i, 0)),
                  small((HY_ZCOLS, HY_FFN)), small((1, HY_FFN)), small((1, HY_FFN)),
                  small((HY_FFN, HY_FFN)), small((1, HY_FFN)), small((1, HY_FFN)),
                  pl.BlockSpec((HY_FFN, HY_WIDTH), lambda i, o: (0, 2 * o + jnp.where(i >= half_tiles, 1, 0))),
                  small((1, HY_WIDTH))],
        out_specs=pl.BlockSpec((1, tm, HY_WIDTH), lambda i, o: (o, i, 0)),
        out_shape=jax.ShapeDtypeStruct((HY_ORDER, 2 * length, HY_WIDTH), F32),
        compiler_params=_params(("parallel", "parallel")),
        name="hyena_filter",
    )(z, w1p, vec(b1), vec(f1), w2, vec(b2), vec(f2), w3, deltas)


def _filter_positions(length):
    t = jnp.concatenate([jnp.arange(length, dtype=F32), float(length) - jnp.arange(length, dtype=F32)])
    valid = jnp.ones((2 * length,), F32).at[length].set(0.0)
    t_norm = t / max(length - 1, 1)
    bands = jnp.linspace(1e-4, HY_BANDS - 1, HY_BANDS, dtype=F32)
    ang = (2.0 * math.pi / length) * t[:, None] * bands[None, :]
    z = jnp.concatenate([t_norm[:, None], jnp.cos(ang), -jnp.sin(ang), valid[:, None]], axis=-1)
    return jnp.pad(z, ((0, 0), (0, HY_ZCOLS - z.shape[1])))


def _angles(num, den):
    return (2.0 * math.pi / den) * (num % den).astype(F32)


def dft_tables_two_stage(m):
    n2 = FFT_N2
    n1 = m // n2
    half = n1 // 2
    kp = -(-(half + 1) // SUBLANE) * SUBLANE
    k1 = jnp.arange(kp, dtype=jnp.int32)
    live = (k1 <= half)
    a1 = _angles(k1[:, None] * jnp.arange(n1, dtype=jnp.int32)[None, :], n1)
    f1 = jnp.concatenate([jnp.where(live[:, None], jnp.cos(a1), 0.0), jnp.where(live[:, None], -jnp.sin(a1), 0.0)], 0)
    wgt = jnp.where((k1 == 0) | (k1 == half), 1.0, 2.0) * live / m
    a1h = a1[:, :half].T
    cinv = jnp.concatenate([jnp.cos(a1h) * wgt[None, :], -jnp.sin(a1h) * wgt[None, :]], axis=1)
    k = k1[:, None, None] + n1 * jnp.arange(n2, dtype=jnp.int32)[None, :, None]
    th = _angles(k * jnp.arange(n2, dtype=jnp.int32)[None, None, :], m)
    c = jnp.where(live[:, None, None], jnp.cos(th), 0.0)
    s = jnp.where(live[:, None, None], jnp.sin(th), 0.0)
    g_fwd = jnp.concatenate([jnp.concatenate([c, s], 2), jnp.concatenate([-s, c], 2)], 1)
    ct, st = jnp.swapaxes(c, 1, 2), jnp.swapaxes(s, 1, 2)
    g_inv = jnp.concatenate([jnp.concatenate([ct, -st], 2), jnp.concatenate([st, ct], 2)], 1)
    return dict(n1=n1, kp=kp, f1=f1.astype(BF16), f1_half=f1[:, :half].astype(BF16), cinv=cinv.astype(BF16),
                g_fwd=g_fwd.astype(BF16), g_inv=g_inv.astype(BF16))


def dft_tables_one_stage(m):
    half = m // 2
    kp = -(-(half + 1) // SUBLANE) * SUBLANE
    k = jnp.arange(kp, dtype=jnp.int32)
    live = (k <= half)
    a = _angles(k[:, None] * jnp.arange(m, dtype=jnp.int32)[None, :], m)
    f = jnp.concatenate([jnp.where(live[:, None], jnp.cos(a), 0.0), jnp.where(live[:, None], -jnp.sin(a), 0.0)], 0)
    wgt = jnp.where((k == 0) | (k == half), 1.0, 2.0) * live / m
    ah = a[:, :half].T
    cinv = jnp.concatenate([jnp.cos(ah) * wgt[None, :], -jnp.sin(ah) * wgt[None, :]], axis=1)
    return dict(kp=kp, f=f.astype(BF16), f_half=f[:, :half].astype(BF16), cinv=cinv.astype(BF16))


def _bmm_kernel(*refs, kb, in_part_major, out_part_major, has_h):
    if has_h:
        g_ref, a_ref, h_ref, o_ref = refs
    else:
        g_ref, a_ref, o_ref = refs
    n2 = FFT_N2
    for b in range(kb):
        if in_part_major:
            ar, ai = a_ref[0, b], a_ref[1, b]
        else:
            ar, ai = a_ref[b, 0], a_ref[b, 1]
        if has_h:
            hr, hi = h_ref[b, 0], h_ref[b, 1]
            ar, ai = ar * hr - ai * hi, ar * hi + ai * hr
        xin = jnp.concatenate([ar, ai], axis=0).astype(BF16)
        y = jnp.dot(g_ref[b], xin, preferred_element_type=F32)
        if out_part_major:
            o_ref[0, b] = y[:n2]
            o_ref[1, b] = y[n2:]
        else:
            o_ref[b, 0] = y[:n2]
            o_ref[b, 1] = y[n2:]


def bmm_k1(g, a, h=None, *, in_part_major, out_part_major):
    kp = g.shape[0]
    n2 = FFT_N2
    c = a.shape[-1]
    kb, tc = SUBLANE, 256
    pm = lambda: pl.BlockSpec((2, kb, n2, tc), lambda i, j: (0, i, 0, j))
    km = lambda: pl.BlockSpec((kb, 2, n2, tc), lambda i, j: (i, 0, 0, j))
    in_specs = [pl.BlockSpec((kb, 2 * n2, 2 * n2), lambda i, j: (i, 0, 0)), pm() if in_part_major else km()]
    args = [g, a]
    if h is not None:
        in_specs.append(km())
        args.append(h)
    return pl.pallas_call(
        functools.partial(_bmm_kernel, kb=kb, in_part_major=in_part_major, out_part_major=out_part_major,
                          has_h=h is not None),
        grid=(kp // kb, c // tc),
        in_specs=in_specs,
        out_specs=pm() if out_part_major else km(),
        out_shape=jax.ShapeDtypeStruct((2, kp, n2, c) if out_part_major else (kp, 2, n2, c), F32),
        compiler_params=_params(("parallel", "parallel")),
        name="dft_inner",
    )(*args)


def _cmul_kernel(x_ref, h_ref, o_ref):
    xr, xi, hr, hi = x_ref[0], x_ref[1], h_ref[0], h_ref[1]
    o_ref[0] = xr * hr - xi * hi
    o_ref[1] = xr * hi + xi * hr


def cmul(x, h):
    spec = pl.BlockSpec(x.shape, lambda i: (0, 0, 0))
    return pl.pallas_call(_cmul_kernel, grid=(1,), in_specs=[spec, spec], out_specs=spec,
                          out_shape=jax.ShapeDtypeStruct(x.shape, F32), compiler_params=_params(("arbitrary",)),
                          name="spectrum_product")(x, h)


def long_conv_two_stage(tabs, taps, v, x1, x2, skip):
    length, c = v.shape
    n2, n1, kp = FFT_N2, tabs["n1"], tabs["kp"]
    cols = n2 * c
    tn = 2048

    def fwd(seq2d, full):
        a = mm([(tabs["f1"] if full else tabs["f1_half"], seq2d)], F32, 2 * kp, tn, name="dft_outer")
        return a.reshape(2, kp, n2, c)

    spectra = [bmm_k1(tabs["g_fwd"], fwd(taps[o].reshape(n1, cols), True), in_part_major=True, out_part_major=False)
               for o in range(HY_ORDER)]
    u = v
    for o, gate in enumerate((x1, x2)):
        xf = bmm_k1(tabs["g_fwd"], fwd(u.reshape(n1 // 2, cols), False), in_part_major=True, out_part_major=False)
        bt = bmm_k1(tabs["g_inv"], xf, spectra[o], in_part_major=False, out_part_major=True)
        skip_row = jnp.tile(skip[o].reshape(1, c), (1, n2))
        u = mm([(tabs["cinv"], bt.reshape(2 * kp, cols))], F32, n1 // 2, tn,
               epi=(gate.reshape(n1 // 2, cols), skip_row, u.reshape(n1 // 2, cols)), name="idft_outer_gate")
        u = u.reshape(length, c)
    return u


def long_conv_one_stage(tabs, taps, v, x1, x2, skip):
    length, c = v.shape
    kp = tabs["kp"]
    u = v
    for o, gate in enumerate((x1, x2)):
        hs = mm([(tabs["f"], taps[o])], F32, 2 * kp, c, name="ctx_dft").reshape(2, kp, c)
        xs = mm([(tabs["f_half"], u)], F32, 2 * kp, c, name="ctx_dft").reshape(2, kp, c)
        ys = cmul(xs, hs).reshape(2 * kp, c)
        u = mm([(tabs["cinv"], ys)], F32, length, c, epi=(gate, skip[o].reshape(1, c), u), name="ctx_idft_gate")
    return u


def _flash_kernel(lam_ref, q_ref, k_ref, v_ref, sub_ref, o_ref, m_ref, l_ref, acc_ref, *, ctx_tile, ctx_len,
                  out_scale):
    i = pl.program_id(1)
    j = pl.program_id(2)
    last = pl.num_programs(2) - 1
    d = DA_HEAD_DIM

    @pl.when(j == 0)
    def _():
        m_ref[...] = jnp.full_like(m_ref, NEG_BIG)
        l_ref[...] = jnp.zeros_like(l_ref)
        acc_ref[...] = jnp.zeros_like(acc_ref)

    def step(masked):
        v = v_ref[...]
        for c in range(2):
            q = q_ref[:, c * d:(c + 1) * d]
            k = k_ref[:, c * d:(c + 1) * d]
            s = lax.dot_general(q, k, (((1,), (1,)), ((), ())), preferred_element_type=F32)
            if masked:
                col = lax.broadcasted_iota(jnp.int32, s.shape, 1)
                s = jnp.where(col >= s.shape[1] - ctx_len, s, NEG_BIG)
            m_old = m_ref[c]
            m_new = jnp.maximum(m_old, jnp.max(s, axis=-1, keepdims=True))
            alpha = jnp.exp(m_old - m_new)
            pr = jnp.exp(s - m_new)
            l_ref[c] = alpha * l_ref[c] + jnp.sum(pr, axis=-1, keepdims=True)
            acc_ref[c] = alpha * acc_ref[c] + jnp.dot(pr.astype(BF16), v, preferred_element_type=F32)
            m_ref[c] = m_new

    @pl.when(i != ctx_tile)
    def _():
        step(False)

    @pl.when((i == ctx_tile) & (j == last))
    def _():
        step(True)

    @pl.when(j == last)
    def _():
        o = acc_ref[0] / l_ref[0] - lam_ref[0] * (acc_ref[1] / l_ref[1])
        o = o * lax.rsqrt(jnp.mean(o * o, axis=-1, keepdims=True) + 1e-5) * sub_ref[...]
        o_ref[...] = (o * out_scale).astype(o_ref.dtype)


def diff_attention(qkv, lam_full, subln, *, seq, ctx_len, lambda_init):
    n_rows = qkv.shape[0]
    tq = ROW_TILE
    tk = _token_tile(n_rows)
    hw = 2 * DA_HEAD_DIM
    return pl.pallas_call(
        functools.partial(_flash_kernel, ctx_tile=seq // tq, ctx_len=ctx_len, out_scale=1.0 - lambda_init),
        grid=(DA_HEADS, n_rows // tq, n_rows // tk),
        in_specs=[pl.BlockSpec(memory_space=pltpu.SMEM),
                  pl.BlockSpec((tq, hw), lambda h, i, j: (i, h)),
                  pl.BlockSpec((tk, hw), lambda h, i, j: (j, DA_HEADS + h)),
                  pl.BlockSpec((tk, hw), lambda h, i, j: (j, 2 * DA_HEADS + h)),
                  pl.BlockSpec((1, hw), lambda h, i, j: (0, 0))],
        out_specs=pl.BlockSpec((tq, hw), lambda h, i, j: (i, h)),
        out_shape=jax.ShapeDtypeStruct((n_rows, DA_WIDTH), BF16),
        scratch_shapes=[pltpu.VMEM((2, tq, 1), F32), pltpu.VMEM((2, tq, 1), F32), pltpu.VMEM((2, tq, hw), F32)],
        compiler_params=_params(("parallel", "parallel", "arbitrary")),
        name="diff_attention",
    )(lam_full.reshape(1), qkv, qkv, qkv, subln.reshape(1, hw))


def _route_kernel(lg_ref, b_ref, o_ref):
    t = lg_ref.shape[1]
    scores = jax.nn.sigmoid(lg_ref[...])
    choice = (scores + b_ref[...]).reshape(N_GROUPS, GROUP_SIZE, t)
    s3 = scores.reshape(N_GROUPS, GROUP_SIZE, t)
    member = lax.broadcasted_iota(jnp.int32, choice.shape, 1)
    group = lax.broadcasted_iota(jnp.int32, (N_GROUPS, 1, t), 0)
    expert = lax.broadcasted_iota(jnp.int32, choice.shape, 0) * GROUP_SIZE + member
    neg_inf = -jnp.inf
    m1 = jnp.max(choice, axis=1, keepdims=True)
    first = jnp.min(jnp.where(choice == m1, member, GROUP_SIZE), axis=1, keepdims=True)
    m2 = jnp.max(jnp.where(member == first, neg_inf, choice), axis=1, keepdims=True)
    gscore = m1 + m2
    gsel = jnp.zeros(gscore.shape, F32)
    for _ in range(TOPK_GROUPS):
        m = jnp.max(gscore, axis=0, keepdims=True)
        f = jnp.min(jnp.where(gscore == m, group, N_GROUPS), axis=0, keepdims=True)
        hit = group == f
        gsel = jnp.where(hit, 1.0, gsel)
        gscore = jnp.where(hit, neg_inf, gscore)
    cand = jnp.where(gsel > 0.0, choice, neg_inf)
    esel = jnp.zeros(choice.shape, F32)
    for _ in range(TOP_K):
        m = jnp.max(jnp.max(cand, axis=1, keepdims=True), axis=0, keepdims=True)
        f = jnp.min(jnp.min(jnp.where(cand == m, expert, N_EXPERTS), axis=1, keepdims=True), axis=0, keepdims=True)
        hit = expert == f
        esel = jnp.where(hit, 1.0, esel)
        cand = jnp.where(hit, neg_inf, cand)
    w = s3 * esel
    denom = jnp.sum(jnp.sum(w, axis=1, keepdims=True), axis=0, keepdims=True) + 1e-20
    o_ref[...] = (w / denom * ROUTED_SCALE).reshape(N_EXPERTS, t)


def route(logits_t, bias):
    n = logits_t.shape[1]
    t = _token_tile(n)
    return pl.pallas_call(
        _route_kernel,
        grid=(n // t,),
        in_specs=[pl.BlockSpec((N_EXPERTS, t), lambda i: (0, i)), pl.BlockSpec((N_EXPERTS, 1), lambda i: (0, 0))],
        out_specs=pl.BlockSpec((N_EXPERTS, t), lambda i: (0, i)),
        out_shape=jax.ShapeDtypeStruct((N_EXPERTS, n), F32),
        compiler_params=_params(("parallel",)),
        name="route",
    )(logits_t, bias.reshape(N_EXPERTS, 1))


def _moe_kernel(h_ref, gt_ref, wg_ref, wu_ref, wd_ref, swg_ref, swu_ref, swd_ref, o_ref, g_ref):
    e = pl.program_id(1)
    h = h_ref[...]

    def ffn(wg, wu, wd, row_scale):
        a = jnp.dot(h, wg, preferred_element_type=F32)
        a = a * jax.nn.sigmoid(a) * jnp.dot(h, wu, preferred_element_type=F32)
        if row_scale is not None:
            a = a * row_scale
        return jnp.dot(a.astype(BF16), wd, preferred_element_type=F32)

    @pl.when(e == 0)
    def _():
        g_ref[...] = gt_ref[...].T
        o_ref[...] = ffn(swg_ref[...], swu_ref[...], swd_ref[...], None)

    g = g_ref[...]
    lane = lax.broadcasted_iota(jnp.int32, g.shape, 1)
    gcol = jnp.sum(jnp.where(lane == e, g, 0.0), axis=1, keepdims=True)
    o_ref[...] += ffn(wg_ref[0], wu_ref[0], wd_ref[0], gcol)


def moe(h, gates_t, wg, wu, wd, swg, swu, swd):
    n, d = h.shape
    t = _token_tile(n)
    f = wg.shape[-1]
    return pl.pallas_call(
        _moe_kernel,
        grid=(n // t, N_EXPERTS),
        in_specs=[pl.BlockSpec((t, d), lambda i, e: (i, 0)),
                  pl.BlockSpec((N_EXPERTS, t), lambda i, e: (0, i)),
                  pl.BlockSpec((1, d, f), lambda i, e: (e, 0, 0)),
                  pl.BlockSpec((1, d, f), lambda i, e: (e, 0, 0)),
                  pl.BlockSpec((1, f, d), lambda i, e: (e, 0, 0)),
                  pl.BlockSpec((d, f), lambda i, e: (0, 0)),
                  pl.BlockSpec((d, f), lambda i, e: (0, 0)),
                  pl.BlockSpec((f, d), lambda i, e: (0, 0))],
        out_specs=pl.BlockSpec((t, d), lambda i, e: (i, 0)),
        out_shape=jax.ShapeDtypeStruct((n, d), F32),
        scratch_shapes=[pltpu.VMEM((t, N_EXPERTS), F32)],
        compiler_params=_params(("parallel", "arbitrary")),
        name="moe",
    )(h, gates_t, wg, wu, wd, swg, swu, swd)


def mixer_ab(h, w_in, w_out, decay_logit, conv_w, conv_b, w1, b1, f1, w2, b2, f2, w3, skip, rope, dft, *, seq,
             ctx_len):
    n_rows = h.shape[0]
    tm = _token_tile(n_rows)
    p = mm([(h, w_in.astype(BF16))], F32, tm, 512, name="ab_in_proj")
    qkv_w = 2 * RET_QK + RET_V
    n_qk = 2 * RET_QK // LANE
    scales = (1.0,) * (RET_QK // LANE) + (RET_DK ** -0.5,) * (RET_QK // LANE) + (1.0,) * (RET_V // LANE)
    qkv = rope_cast(p, rope[0], rope[1], width=qkv_w, n_rot_blocks=n_qk, head_dim=RET_DK, scales=scales,
                    out_dtype=F32)
    log_g = jax.nn.log_sigmoid(decay_logit.astype(F32))
    ret = retention(qkv, p, log_g, jnp.exp(RET_CHUNK * log_g), seq=seq)
    v, x1, x2 = shortconv(p, conv_w, conv_b, seq=seq)
    filt = (w1, b1, f1, w2, b2, f2, w3)
    hy_x = long_conv_two_stage(dft["x"], hyena_filter_taps(seq, *filt), v[:seq], x1[:seq], x2[:seq], skip)
    hy_c = long_conv_one_stage(dft["c"], hyena_filter_taps(ctx_len, *filt), v[seq:], x1[seq:], x2[seq:], skip)
    hy = jnp.concatenate([hy_x, hy_c], axis=0)
    w_out = w_out.astype(BF16)
    return mm([(ret, w_out[:RET_V]), (hy, w_out[RET_V:])], F32, tm, 512, name="ab_out_proj")


def mixer_da(h, w_in, w_out, lam, subln, lambda_init, rope, *, seq, ctx_len):
    n_rows = h.shape[0]
    tm = _token_tile(n_rows)
    p = mm([(h, w_in.astype(BF16))], F32, tm, 512, name="da_in_proj")
    n_qk = 2 * DA_WIDTH // LANE
    scales = (DA_HEAD_DIM ** -0.5,) * (DA_WIDTH // LANE) + (1.0,) * (2 * DA_WIDTH // LANE)
    qkv = rope_cast(p, rope[0], rope[1], width=3 * DA_WIDTH, n_rot_blocks=n_qk, head_dim=DA_HEAD_DIM, scales=scales,
                    out_dtype=BF16)
    lam_f = lam.astype(F32)
    lam_full = jnp.exp(jnp.sum(lam_f[0] * lam_f[1])) - jnp.exp(jnp.sum(lam_f[2] * lam_f[3])) + lambda_init
    o = diff_attention(qkv, lam_full, subln, seq=seq, ctx_len=ctx_len, lambda_init=lambda_init)
    return mm([(o, w_out.astype(BF16))], F32, tm, 512, name="da_out_proj")


def kernel(x, c, ctx, c_ctx, w_ada, b_ada, norm_mix, norm_ffn, ab_w_in, ab_w_out, ret_decay_logit, hy_conv_w, hy_conv_b, hy_w1, hy_b1, hy_freq1, hy_w2, hy_b2, hy_freq2, hy_w3, hy_skip, da_w_in, da_w_out, da_lambda, da_subln, router_w, router_b, exp_w_gate, exp_w_up, exp_w_down, sh_w_gate, sh_w_up, sh_w_down, norm_final):
    batch, seq, d = x.shape
    ctx_len = ctx.shape[1]
    assert batch == 1 and seq % ROW_TILE == 0 and ctx_len == ROW_TILE
    depth = w_ada.shape[0]
    n_rows = seq + ctx_len
    ctx_tile = seq // ROW_TILE

    xs = jnp.concatenate([x[0], ctx[0]], axis=0)
    cv = jnp.zeros((SUBLANE, d), F32).at[0].set(c_ctx).at[1].set(c[0])
    mods = adaln(cv, w_ada, b_ada)[:, :2].reshape(depth, 2, 6, d)

    rope_ret = rope_tables(seq, ctx_len, RET_DK)
    rope_da = rope_tables(seq, ctx_len, DA_HEAD_DIM)
    dft = dict(x=dft_tables_two_stage(2 * seq), c=dft_tables_one_stage(2 * ctx_len))
    common = dict(n_rows=n_rows, ctx_tile=ctx_tile)

    delta, gate_mods = None, None
    for i in range(depth):
        j = i // 2
        if delta is None:
            (h,) = norm_mod(xs, norm_mix[i], mods=mods[i], shift_idx=0, scale_idx=1, **common)
        else:
            xs, h = norm_mod(xs, norm_mix[i], delta=delta, gate_mods=gate_mods, gate_idx=5, mods=mods[i], shift_idx=0,
                             scale_idx=1, **common)
        if i % 2 == 0:
            y = mixer_ab(h, ab_w_in[j], ab_w_out[j], ret_decay_logit[j], hy_conv_w[j], hy_conv_b[j], hy_w1[j],
                         hy_b1[j], hy_freq1[j], hy_w2[j], hy_b2[j], hy_freq2[j], hy_w3[j], hy_skip[j], rope_ret, dft,
                         seq=seq, ctx_len=ctx_len)
        else:
            lambda_init = 0.8 - 0.6 * math.exp(-0.3 * i)
            y = mixer_da(h, da_w_in[j], da_w_out[j], da_lambda[j], da_subln[j], lambda_init, rope_da, seq=seq,
                         ctx_len=ctx_len)
        xs, h, logits_t = norm_mod(xs, norm_ffn[i], delta=y, gate_mods=mods[i], gate_idx=2, mods=mods[i], shift_idx=3,
                                   scale_idx=4, router_wt=router_w[i].T, **common)
        gates_t = route(logits_t, router_b[i])
        delta = moe(h, gates_t, exp_w_gate[i].astype(BF16), exp_w_up[i].astype(BF16), exp_w_down[i].astype(BF16),
                    sh_w_gate[i].astype(BF16), sh_w_up[i].astype(BF16), sh_w_down[i].astype(BF16))
        gate_mods = mods[i]
    _, out = norm_mod(xs, norm_final, n_rows=seq, ctx_tile=ctx_tile, delta=delta, gate_mods=gate_mods, gate_idx=5,
                      out_dtype=F32)
    return out[None]
```

```python
import functools
import math

import jax
import jax.numpy as jnp
from jax import lax
from jax.experimental import pallas as pl
from jax.experimental.pallas import tpu as pltpu

F32 = jnp.float32
BF16 = jnp.bfloat16
HIGHEST = lax.Precision.HIGHEST

D_MODEL = 1024
DEPTH = 4
GRID_W = 64
EPS = 1e-6
ROPE_BASE = 10000.0

RET_HEADS = 4
RET_DK = 128
RET_DV = 256
RET_CHUNK = 128
RET_QK = RET_HEADS * RET_DK
RET_V = RET_HEADS * RET_DV

HY_WIDTH = 512
HY_ORDER = 2
HY_BANDS = 16
HY_EMB = 2 * HY_BANDS + 1
HY_FFN = 64
HY_DECAY_TARGET = 1e-2
HY_FAST_DECAY = 0.3
HY_SLOW_DECAY = 1.5
HY_ZCOLS = 64
HY_VALID_COL = HY_EMB
FFT_N2 = 128

AB_IN = 2 * RET_QK + 2 * RET_V + (HY_ORDER + 1) * HY_WIDTH
AB_CAT = RET_V + HY_WIDTH

DA_HEADS = 8
DA_HEAD_DIM = 64
DA_WIDTH = DA_HEADS * 2 * DA_HEAD_DIM

N_EXPERTS = 64
TOP_K = 8
N_GROUPS = 8
TOPK_GROUPS = 4
GROUP_SIZE = N_EXPERTS // N_GROUPS
EXPERT_DIM = 256
ROUTED_SCALE = 2.5

LANE = 128
SUBLANE = 8
ROW_TILE = 256
MAX_TOKEN_TILE = 1280
VMEM_LIMIT = 48 * 1024 * 1024
FLASH_VMEM_LIMIT = 56 * 1024 * 1024
NEG_BIG = -1e30
LOG2_E = 1.4426950408889634


def _params(sem):
    return pltpu.CompilerParams(dimension_semantics=sem, vmem_limit_bytes=VMEM_LIMIT)


def _token_tile(n):
    best = ROW_TILE
    t = ROW_TILE
    while t <= min(n, MAX_TOKEN_TILE):
        if n % t == 0:
            best = t
        t += ROW_TILE
    return best


def _mm_kernel(*refs, n_pairs, has_epi):
    acc = None
    for p in range(n_pairs):
        a = refs[2 * p][...].astype(BF16)
        b = refs[2 * p + 1][...].astype(BF16)
        d = jnp.dot(a, b, preferred_element_type=F32)
        acc = d if acc is None else acc + d
    idx = 2 * n_pairs
    if has_epi:
        acc = refs[idx][...] * (acc + refs[idx + 1][...] * refs[idx + 2][...])
        idx += 3
    o_ref = refs[idx]
    o_ref[...] = acc.astype(o_ref.dtype)


def mm(pairs, out_dtype, tm, tn, epi=None, name="mm"):
    m = pairs[0][0].shape[0]
    n = pairs[0][1].shape[1]
    assert m % tm == 0 and n % tn == 0
    in_specs, args = [], []
    for a, b in pairs:
        k = a.shape[1]
        in_specs += [pl.BlockSpec((tm, k), lambda i, j: (i, 0)), pl.BlockSpec((k, tn), lambda i, j: (0, j))]
        args += [a, b]
    if epi is not None:
        in_specs += [pl.BlockSpec((tm, tn), lambda i, j: (i, j)), pl.BlockSpec((1, tn), lambda i, j: (0, j)),
                     pl.BlockSpec((tm, tn), lambda i, j: (i, j))]
        args += list(epi)
    return pl.pallas_call(
        functools.partial(_mm_kernel, n_pairs=len(pairs), has_epi=epi is not None),
        grid=(m // tm, n // tn),
        in_specs=in_specs,
        out_specs=pl.BlockSpec((tm, tn), lambda i, j: (i, j)),
        out_shape=jax.ShapeDtypeStruct((m, n), out_dtype),
        compiler_params=_params(("parallel", "parallel")),
        name=name,
    )(*args)


def _adaln_kernel(cv_ref, w_ref, b_ref, o_ref):
    cv = cv_ref[...]
    s = cv * jax.nn.sigmoid(cv)
    o_ref[0] = jnp.dot(s, w_ref[0], precision=HIGHEST, preferred_element_type=F32) + b_ref[0]


def adaln(cv, w_ada, b_ada):
    depth, d, n = w_ada.shape
    tn = 1536
    return pl.pallas_call(
        _adaln_kernel,
        grid=(depth, n // tn),
        in_specs=[pl.BlockSpec((SUBLANE, d), lambda l, j: (0, 0)),
                  pl.BlockSpec((1, d, tn), lambda l, j: (l, 0, j)),
                  pl.BlockSpec((1, 1, tn), lambda l, j: (l, 0, j))],
        out_specs=pl.BlockSpec((1, SUBLANE, tn), lambda l, j: (l, 0, j)),
        out_shape=jax.ShapeDtypeStruct((depth, SUBLANE, n), F32),
        compiler_params=_params(("parallel", "parallel")),
        name="adaln",
    )(cv, w_ada, b_ada.reshape(depth, 1, n))


def _norm_mod_kernel(*refs, has_delta, gate_idx, shift_idx, scale_idx, has_router, write_xs):
    it = iter(refs)
    xs_ref = next(it)
    x = xs_ref[...]
    if has_delta:
        delta_ref = next(it)
        gmods_ref = next(it)
        x = x + gmods_ref[0, gate_idx:gate_idx + 1, :] * delta_ref[...]
    mods_ref = next(it) if shift_idx is not None else None
    g_ref = next(it)
    wr_ref = next(it) if has_router else None
    if write_xs:
        next(it)[...] = x
    h_ref = next(it)
    y = x * lax.rsqrt(jnp.mean(x * x, axis=-1, keepdims=True) + EPS) * g_ref[...]
    if shift_idx is not None:
        y = y * (1.0 + mods_ref[0, scale_idx:scale_idx + 1, :]) + mods_ref[0, shift_idx:shift_idx + 1, :]
    h_ref[...] = y.astype(h_ref.dtype)
    if has_router:
        lg_ref = next(it)
        lg_ref[...] = lax.dot_general(wr_ref[...], y, (((1,), (1,)), ((), ())),
                                      precision=HIGHEST, preferred_element_type=F32)


def norm_mod(xs, g, *, n_rows, ctx_tile, delta=None, gate_mods=None, gate_idx=None, mods=None, shift_idx=None,
             scale_idx=None, router_wt=None, out_dtype=BF16):
    d = xs.shape[1]
    n_tiles = n_rows // ROW_TILE
    row = pl.BlockSpec((ROW_TILE, d), lambda i: (i, 0))
    mod_spec = pl.BlockSpec((1, 6, d), lambda i: (jnp.where(i == ctx_tile, 0, 1), 0, 0))
    in_specs, args = [row], [xs]
    has_delta = delta is not None
    if has_delta:
        in_specs += [row, mod_spec]
        args += [delta, gate_mods]
    if shift_idx is not None:
        in_specs.append(mod_spec)
        args.append(mods)
    in_specs.append(pl.BlockSpec((1, d), lambda i: (0, 0)))
    args.append(g.reshape(1, d))
    has_router = router_wt is not None
    if has_router:
        in_specs.append(pl.BlockSpec(router_wt.shape, lambda i: (0, 0)))
        args.append(router_wt)
    out_specs, out_shape = [], []
    if has_delta:
        out_specs.append(row)
        out_shape.append(jax.ShapeDtypeStruct((n_rows, d), F32))
    out_specs.append(row)
    out_shape.append(jax.ShapeDtypeStruct((n_rows, d), out_dtype))
    if has_router:
        out_specs.append(pl.BlockSpec((N_EXPERTS, ROW_TILE), lambda i: (0, i)))
        out_shape.append(jax.ShapeDtypeStruct((N_EXPERTS, n_rows), F32))
    return pl.pallas_call(
        functools.partial(_norm_mod_kernel, has_delta=has_delta, gate_idx=gate_idx, shift_idx=shift_idx,
                          scale_idx=scale_idx, has_router=has_router, write_xs=has_delta),
        grid=(n_tiles,),
        in_specs=in_specs,
        out_specs=out_specs,
        out_shape=out_shape,
        compiler_params=_params(("parallel",)),
        name="norm_mod",
    )(*args)


def _rope_kernel(p_ref, cos_ref, sin_ref, o_ref, *, n_rot_blocks, head_dim, scales):
    cos = cos_ref[...]
    sin = sin_ref[...]
    for b in range(len(scales)):
        x = p_ref[:, b * LANE:(b + 1) * LANE]
        if b < n_rot_blocks:
            if head_dim == LANE:
                rot = pltpu.roll(x, LANE // 2, 1)
            else:
                lane = lax.broadcasted_iota(jnp.int32, x.shape, 1)
                first_half = (lane % head_dim) < head_dim // 2
                rot = jnp.where(first_half, pltpu.roll(x, LANE - head_dim // 2, 1), pltpu.roll(x, head_dim // 2, 1))
            x = x * cos + rot * sin
        if scales[b] != 1.0:
            x = x * scales[b]
        o_ref[:, b * LANE:(b + 1) * LANE] = x.astype(o_ref.dtype)


def rope_cast(p, cos, sin, *, width, n_rot_blocks, head_dim, scales, out_dtype):
    n_rows = p.shape[0]
    return pl.pallas_call(
        functools.partial(_rope_kernel, n_rot_blocks=n_rot_blocks, head_dim=head_dim, scales=scales),
        grid=(n_rows // ROW_TILE,),
        in_specs=[pl.BlockSpec((ROW_TILE, width), lambda i: (i, 0)),
                  pl.BlockSpec((ROW_TILE, LANE), lambda i: (i, 0)),
                  pl.BlockSpec((ROW_TILE, LANE), lambda i: (i, 0))],
        out_specs=pl.BlockSpec((ROW_TILE, width), lambda i: (i, 0)),
        out_shape=jax.ShapeDtypeStruct((n_rows, width), out_dtype),
        compiler_params=_params(("parallel",)),
        name="rope_cast",
    )(p, cos, sin)


def rope_tables(seq, ctx_len, head_dim):
    n_freq = head_dim // 4
    inv = ROPE_BASE ** (-jnp.arange(n_freq, dtype=F32) / n_freq)
    rows = seq // GRID_W
    row = jnp.repeat(jnp.arange(rows, dtype=F32), GRID_W)
    col = jnp.tile(jnp.arange(GRID_W, dtype=F32), rows)
    ang = jnp.concatenate([row[:, None] * inv, col[:, None] * inv], axis=-1)
    cos, sin = jnp.cos(ang), jnp.sin(ang)
    cos = jnp.concatenate([cos, cos], axis=-1)
    sin = jnp.concatenate([-sin, sin], axis=-1)
    reps = LANE // head_dim
    cos, sin = jnp.tile(cos, (1, reps)), jnp.tile(sin, (1, reps))
    cos = jnp.concatenate([cos, jnp.ones((ctx_len, LANE), F32)], axis=0)
    sin = jnp.concatenate([sin, jnp.zeros((ctx_len, LANE), F32)], axis=0)
    return cos, sin


def _ret_kernel(lg_ref, gc_ref, q_ref, k_ref, v_ref, *rest, reverse):
    if reverse:
        yf_ref, gate_ref, o_ref, s_ref = rest
    else:
        o_ref, s_ref = rest
    c = RET_CHUNK

    @pl.when(pl.program_id(0) == 0)
    def _():
        s_ref[...] = jnp.zeros_like(s_ref)

    ii = lax.broadcasted_iota(jnp.int32, (c, c), 0)
    jj = lax.broadcasted_iota(jnp.int32, (c, c), 1)
    rel = ((jj - ii) if reverse else (ii - jj)).astype(F32)
    pos = lax.broadcasted_iota(jnp.int32, (c, 1), 0).astype(F32)
    for h in range(RET_HEADS):
        lg = lg_ref[h]
        dec = jnp.where(rel >= 0, jnp.exp(jnp.maximum(rel, 0.0) * lg), 0.0)
        if reverse:
            q_dec = jnp.exp((c - pos) * lg)
            k_dec = jnp.exp(pos * lg)
        else:
            q_dec = jnp.exp((pos + 1.0) * lg)
            k_dec = jnp.exp((c - 1.0 - pos) * lg)
        q = q_ref[:, h * RET_DK:(h + 1) * RET_DK]
        k = k_ref[:, h * RET_DK:(h + 1) * RET_DK]
        v = v_ref[:, h * RET_DV:(h + 1) * RET_DV].astype(BF16)
        s = lax.dot_general(q.astype(BF16), k.astype(BF16), (((1,), (1,)), ((), ())),
                            preferred_element_type=F32) * dec
        state = s_ref[h]
        y = jnp.dot(s.astype(BF16), v, preferred_element_type=F32)
        y = y + jnp.dot((q * q_dec).astype(BF16), state.astype(BF16), preferred_element_type=F32)
        upd = lax.dot_general((k * k_dec).astype(BF16), v, (((0,), (0,)), ((), ())), preferred_element_type=F32)
        s_ref[h] = gc_ref[h] * state + upd
        if reverse:
            r = y + yf_ref[:, h * RET_DV:(h + 1) * RET_DV]
            mu = jnp.mean(r, axis=-1, keepdims=True)
            rc = r - mu
            var = jnp.mean(rc * rc, axis=-1, keepdims=True)
            g = gate_ref[:, h * RET_DV:(h + 1) * RET_DV]
            o_ref[:, h * RET_DV:(h + 1) * RET_DV] = (rc * lax.rsqrt(var + EPS) * (g * jax.nn.sigmoid(g))).astype(
                o_ref.dtype)
        else:
            o_ref[:, h * RET_DV:(h + 1) * RET_DV] = y


def retention(qkv, p, log_g, g_chunk, *, seq):
    n_rows = qkv.shape[0]
    n_chunks = n_rows // RET_CHUNK
    n_x = seq // RET_CHUNK
    smem = pl.BlockSpec(memory_space=pltpu.SMEM)

    def run(reverse, extra):
        if reverse:
            idx = lambda t: n_chunks - 1 - t
        else:
            idx = lambda t: (t + n_x) % n_chunks
        in_specs = [smem, smem,
                    pl.BlockSpec((RET_CHUNK, RET_QK), lambda t: (idx(t), 0)),
                    pl.BlockSpec((RET_CHUNK, RET_QK), lambda t: (idx(t), 1)),
                    pl.BlockSpec((RET_CHUNK, RET_V), lambda t: (idx(t), 1))]
        args = [log_g[1 if reverse else 0], g_chunk[1 if reverse else 0], qkv, qkv, qkv]
        if reverse:
            in_specs += [pl.BlockSpec((RET_CHUNK, RET_V), lambda t: (idx(t), 0)),
                         pl.BlockSpec((RET_CHUNK, RET_V), lambda t: (idx(t), 2))]
            args += list(extra)
        return pl.pallas_call(
            functools.partial(_ret_kernel, reverse=reverse),
            grid=(n_chunks,),
            in_specs=in_specs,
            out_specs=pl.BlockSpec((RET_CHUNK, RET_V), lambda t: (idx(t), 0)),
            out_shape=jax.ShapeDtypeStruct((n_rows, RET_V), BF16 if reverse else F32),
            scratch_shapes=[pltpu.VMEM((RET_HEADS, RET_DK, RET_DV), F32)],
            compiler_params=_params(("arbitrary",)),
            name="retention_bwd" if reverse else "retention_fwd",
        )(*args)

    y_fwd = run(False, None)
    return run(True, (y_fwd, p))


def _shortconv_kernel(cur_ref, prev_ref, next_ref, w_ref, b_ref, v_ref, x1_ref, x2_ref, *, x_tiles):
    i = pl.program_id(0)
    cur = cur_ref[...]
    rows = cur.shape[0]
    row = lax.broadcasted_iota(jnp.int32, (rows, 1), 0)
    has_prev = jnp.where((i == 0) | (i == x_tiles), 0.0, 1.0)
    has_next = jnp.where((i == x_tiles - 1) | (i == x_tiles), 0.0, 1.0)
    up = jnp.where(row == 0, prev_ref[SUBLANE - 1:SUBLANE, :] * has_prev, pltpu.roll(cur, 1, 0))
    dn = jnp.where(row == rows - 1, next_ref[0:1, :] * has_next, pltpu.roll(cur, rows - 1, 0))
    y = up * w_ref[0:1, :] + cur * w_ref[1:2, :] + dn * w_ref[2:3, :] + b_ref[...]
    v_ref[...] = y[:, :HY_WIDTH]
    x1_ref[...] = y[:, HY_WIDTH:2 * HY_WIDTH]
    x2_ref[...] = y[:, 2 * HY_WIDTH:]


def shortconv(p, w, b, *, seq):
    n_rows = p.shape[0]
    width = 3 * HY_WIDTH
    col = p.shape[1] // width - 1
    per = ROW_TILE // SUBLANE
    last = n_rows // SUBLANE - 1
    out = jax.ShapeDtypeStruct((n_rows, HY_WIDTH), F32)
    ospec = pl.BlockSpec((ROW_TILE, HY_WIDTH), lambda i: (i, 0))
    return pl.pallas_call(
        functools.partial(_shortconv_kernel, x_tiles=seq // ROW_TILE),
        grid=(n_rows // ROW_TILE,),
        in_specs=[pl.BlockSpec((ROW_TILE, width), lambda i: (i, col)),
                  pl.BlockSpec((SUBLANE, width), lambda i: (jnp.maximum(i * per - 1, 0), col)),
                  pl.BlockSpec((SUBLANE, width), lambda i: (jnp.minimum((i + 1) * per, last), col)),
                  pl.BlockSpec((3, width), lambda i: (0, 0)),
                  pl.BlockSpec((1, width), lambda i: (0, 0))],
        out_specs=[ospec, ospec, ospec],
        out_shape=[out, out, out],
        compiler_params=_params(("parallel",)),
        name="shortconv",
    )(p, p, p, w, b.reshape(1, width))


def _filt_kernel(z_ref, w1_ref, b1_ref, f1_ref, w2_ref, b2_ref, f2_ref, w3a_ref, w3b_ref, dl_ref, o_ref):
    z = z_ref[...]
    h = jnp.sin(f1_ref[...] * (jnp.dot(z, w1_ref[...], precision=HIGHEST, preferred_element_type=F32) + b1_ref[...]))
    h = jnp.sin(f2_ref[...] * (jnp.dot(h, w2_ref[...], precision=HIGHEST, preferred_element_type=F32) + b2_ref[...]))
    window = jnp.exp(-z[:, 0:1] * dl_ref[...]) * z[:, HY_VALID_COL:HY_VALID_COL + 1]
    for o, w3_ref in enumerate((w3a_ref, w3b_ref)):
        o_ref[o] = jnp.dot(h, w3_ref[...], precision=HIGHEST, preferred_element_type=F32) * window


def hyena_filter_taps(length, w1, b1, f1, w2, b2, f2, w3):
    z = _filter_positions(length)
    w1p = jnp.zeros((HY_ZCOLS, HY_FFN), F32).at[:HY_EMB].set(w1)
    deltas = jnp.abs(jnp.linspace(math.log(HY_DECAY_TARGET) / HY_SLOW_DECAY,
                                  math.log(HY_DECAY_TARGET) / HY_FAST_DECAY, HY_WIDTH, dtype=F32)).reshape(1, HY_WIDTH)
    tm = min(length, 512)
    half_tiles = length // tm
    vec = lambda a: a.reshape(1, HY_FFN)
    small = lambda shape: pl.BlockSpec(shape, lambda i: (0, 0))
    w3_spec = lambda o: pl.BlockSpec((HY_FFN, HY_WIDTH), lambda i: (0, 2 * o + jnp.where(i >= half_tiles, 1, 0)))
    assert HY_ORDER == 2
    return pl.pallas_call(
        _filt_kernel,
        grid=(2 * half_tiles,),
        in_specs=[pl.BlockSpec((tm, HY_ZCOLS), lambda i: (i, 0)),
                  small((HY_ZCOLS, HY_FFN)), small((1, HY_FFN)), small((1, HY_FFN)),
                  small((HY_FFN, HY_FFN)), small((1, HY_FFN)), small((1, HY_FFN)),
                  w3_spec(0), w3_spec(1), small((1, HY_WIDTH))],
        out_specs=pl.BlockSpec((HY_ORDER, tm, HY_WIDTH), lambda i: (0, i, 0)),
        out_shape=jax.ShapeDtypeStruct((HY_ORDER, 2 * length, HY_WIDTH), F32),
        compiler_params=_params(("parallel",)),
        name="hyena_filter",
    )(z, w1p, vec(b1), vec(f1), w2, vec(b2), vec(f2), w3, w3, deltas)


def _filter_positions(length):
    t = jnp.concatenate([jnp.arange(length, dtype=F32), float(length) - jnp.arange(length, dtype=F32)])
    valid = jnp.ones((2 * length,), F32).at[length].set(0.0)
    t_norm = t / max(length - 1, 1)
    bands = jnp.linspace(1e-4, HY_BANDS - 1, HY_BANDS, dtype=F32)
    ang = (2.0 * math.pi / length) * t[:, None] * bands[None, :]
    z = jnp.concatenate([t_norm[:, None], jnp.cos(ang), -jnp.sin(ang), valid[:, None]], axis=-1)
    return jnp.pad(z, ((0, 0), (0, HY_ZCOLS - z.shape[1])))


def _angles(num, den):
    return (2.0 * math.pi / den) * (num % den).astype(F32)


def dft_tables_two_stage(m):
    n2 = FFT_N2
    n1 = m // n2
    half = n1 // 2
    kp = -(-(half + 1) // SUBLANE) * SUBLANE
    k1 = jnp.arange(kp, dtype=jnp.int32)
    live = (k1 <= half)
    a1 = _angles(k1[:, None] * jnp.arange(n1, dtype=jnp.int32)[None, :], n1)
    f1 = jnp.concatenate([jnp.where(live[:, None], jnp.cos(a1), 0.0), jnp.where(live[:, None], -jnp.sin(a1), 0.0)], 0)
    wgt = jnp.where((k1 == 0) | (k1 == half), 1.0, 2.0) * live / m
    a1h = a1[:, :half].T
    cinv = jnp.concatenate([jnp.cos(a1h) * wgt[None, :], -jnp.sin(a1h) * wgt[None, :]], axis=1)
    k = k1[:, None, None] + n1 * jnp.arange(n2, dtype=jnp.int32)[None, :, None]
    th = _angles(k * jnp.arange(n2, dtype=jnp.int32)[None, None, :], m)
    c = jnp.where(live[:, None, None], jnp.cos(th), 0.0)
    s = jnp.where(live[:, None, None], jnp.sin(th), 0.0)
    g_fwd = jnp.concatenate([jnp.concatenate([c, s], 2), jnp.concatenate([-s, c], 2)], 1)
    ct, st = jnp.swapaxes(c, 1, 2), jnp.swapaxes(s, 1, 2)
    g_inv = jnp.concatenate([jnp.concatenate([ct, -st], 2), jnp.concatenate([st, ct], 2)], 1)
    return dict(n1=n1, kp=kp, f1=f1.astype(BF16), f1_half=f1[:, :half].astype(BF16), cinv=cinv.astype(BF16),
                g_fwd=g_fwd.astype(BF16), g_inv=g_inv.astype(BF16))


def dft_tables_one_stage(m):
    half = m // 2
    kp = -(-(half + 1) // SUBLANE) * SUBLANE
    k = jnp.arange(kp, dtype=jnp.int32)
    live = (k <= half)
    a = _angles(k[:, None] * jnp.arange(m, dtype=jnp.int32)[None, :], m)
    f = jnp.concatenate([jnp.where(live[:, None], jnp.cos(a), 0.0), jnp.where(live[:, None], -jnp.sin(a), 0.0)], 0)
    wgt = jnp.where((k == 0) | (k == half), 1.0, 2.0) * live / m
    ah = a[:, :half].T
    cinv = jnp.concatenate([jnp.cos(ah) * wgt[None, :], -jnp.sin(ah) * wgt[None, :]], axis=1)
    return dict(kp=kp, f=f.astype(BF16), f_half=f[:, :half].astype(BF16), cinv=cinv.astype(BF16))


def _bmm_kernel(*refs, kb, in_part_major, out_part_major, has_h):
    if has_h:
        g_ref, a_ref, h_ref, o_ref = refs
    else:
        g_ref, a_ref, o_ref = refs
    n2 = FFT_N2
    for b in range(kb):
        if in_part_major:
            ar, ai = a_ref[0, b], a_ref[1, b]
        else:
            ar, ai = a_ref[b, 0], a_ref[b, 1]
        if has_h:
            hr, hi = h_ref[b, 0], h_ref[b, 1]
            ar, ai = ar * hr - ai * hi, ar * hi + ai * hr
        xin = jnp.concatenate([ar, ai], axis=0).astype(BF16)
        y = jnp.dot(g_ref[b], xin, preferred_element_type=F32)
        if out_part_major:
            o_ref[0, b] = y[:n2]
            o_ref[1, b] = y[n2:]
        else:
            o_ref[b, 0] = y[:n2]
            o_ref[b, 1] = y[n2:]


def bmm_k1(g, a, h=None, *, in_part_major, out_part_major):
    kp = g.shape[0]
    n2 = FFT_N2
    c = a.shape[-1]
    kb, tc = SUBLANE, 256
    pm = lambda: pl.BlockSpec((2, kb, n2, tc), lambda i, j: (0, i, 0, j))
    km = lambda: pl.BlockSpec((kb, 2, n2, tc), lambda i, j: (i, 0, 0, j))
    in_specs = [pl.BlockSpec((kb, 2 * n2, 2 * n2), lambda i, j: (i, 0, 0)), pm() if in_part_major else km()]
    args = [g, a]
    if h is not None:
        in_specs.append(km())
        args.append(h)
    return pl.pallas_call(
        functools.partial(_bmm_kernel, kb=kb, in_part_major=in_part_major, out_part_major=out_part_major,
                          has_h=h is not None),
        grid=(kp // kb, c // tc),
        in_specs=in_specs,
        out_specs=pm() if out_part_major else km(),
        out_shape=jax.ShapeDtypeStruct((2, kp, n2, c) if out_part_major else (kp, 2, n2, c), F32),
        compiler_params=_params(("parallel", "parallel")),
        name="dft_inner",
    )(*args)


def _cmul_kernel(x_ref, h_ref, o_ref):
    xr, xi, hr, hi = x_ref[0], x_ref[1], h_ref[0], h_ref[1]
    o_ref[0] = xr * hr - xi * hi
    o_ref[1] = xr * hi + xi * hr


def cmul(x, h):
    spec = pl.BlockSpec(x.shape, lambda i: (0, 0, 0))
    return pl.pallas_call(_cmul_kernel, grid=(1,), in_specs=[spec, spec], out_specs=spec,
                          out_shape=jax.ShapeDtypeStruct(x.shape, F32), compiler_params=_params(("arbitrary",)),
                          name="spectrum_product")(x, h)


def long_conv_two_stage(tabs, taps, v, x1, x2, skip):
    length, c = v.shape
    n2, n1, kp = FFT_N2, tabs["n1"], tabs["kp"]
    cols = n2 * c
    tn = 2048

    def fwd(seq2d, full):
        a = mm([(tabs["f1"] if full else tabs["f1_half"], seq2d)], F32, 2 * kp, tn, name="dft_outer")
        return a.reshape(2, kp, n2, c)

    spectra = [bmm_k1(tabs["g_fwd"], fwd(taps[o].reshape(n1, cols), True), in_part_major=True, out_part_major=False)
               for o in range(HY_ORDER)]
    u = v
    for o, gate in enumerate((x1, x2)):
        xf = bmm_k1(tabs["g_fwd"], fwd(u.reshape(n1 // 2, cols), False), in_part_major=True, out_part_major=False)
        bt = bmm_k1(tabs["g_inv"], xf, spectra[o], in_part_major=False, out_part_major=True)
        skip_row = jnp.tile(skip[o].reshape(1, c), (1, n2))
        u = mm([(tabs["cinv"], bt.reshape(2 * kp, cols))], F32, n1 // 2, tn,
               epi=(gate.reshape(n1 // 2, cols), skip_row, u.reshape(n1 // 2, cols)), name="idft_outer_gate")
        u = u.reshape(length, c)
    return u


def long_conv_one_stage(tabs, taps, v, x1, x2, skip):
    length, c = v.shape
    kp = tabs["kp"]
    u = v
    for o, gate in enumerate((x1, x2)):
        hs = mm([(tabs["f"], taps[o])], F32, 2 * kp, c, name="ctx_dft").reshape(2, kp, c)
        xs = mm([(tabs["f_half"], u)], F32, 2 * kp, c, name="ctx_dft").reshape(2, kp, c)
        ys = cmul(xs, hs).reshape(2 * kp, c)
        u = mm([(tabs["cinv"], ys)], F32, length, c, epi=(gate, skip[o].reshape(1, c), u), name="ctx_idft_gate")
    return u


def _flash_kernel(lam_ref, qt_ref, k_ref, vt_ref, sub_ref, o_ref, m_ref, acc_ref, *, kv, seq, ctx_len, out_scale):
    i = pl.program_id(1)
    last_q = pl.num_programs(1) - 1
    tq = qt_ref.shape[1]
    d = DA_HEAD_DIM
    dv = 2 * DA_HEAD_DIM
    n_chunks = k_ref.shape[0] // kv
    m_ref[...] = jnp.full_like(m_ref, NEG_BIG)
    acc_ref[...] = jnp.zeros_like(acc_ref)

    def run(masked):
        def body(kc, carry):
            off = pl.multiple_of(kc * kv, kv)
            kblk = k_ref[pl.ds(off, kv), :]
            vt = vt_ref[:, pl.ds(off, kv)]
            if masked:
                key = off + lax.broadcasted_iota(jnp.int32, (kv, 1), 0)
                lane = lax.broadcasted_iota(jnp.int32, (1, tq), 1)
                bias = jnp.where(key < seq, NEG_BIG, 0.0) * jnp.where(lane >= tq - ctx_len, 1.0, 0.0)
            for c in range(2):
                s = jnp.dot(kblk[:, c * d:(c + 1) * d], qt_ref[c * d:(c + 1) * d, :], preferred_element_type=F32)
                if masked:
                    s = s + bias
                m_old = m_ref[c]
                m_new = jnp.maximum(m_old, jnp.max(s, axis=0, keepdims=True))
                alpha = jnp.exp2(m_old - m_new)
                pr = jnp.exp2(s - m_new).astype(BF16)
                acc_ref[c] = alpha * acc_ref[c] + jnp.dot(vt, pr, preferred_element_type=F32)
                m_ref[c] = m_new
            return carry

        lax.fori_loop(0, n_chunks, body, 0)

    @pl.when(i != last_q)
    def _():
        run(False)

    @pl.when(i == last_q)
    def _():
        run(True)

    a0 = acc_ref[0, :dv, :] / acc_ref[0, dv:dv + 1, :]
    a1 = acc_ref[1, :dv, :] / acc_ref[1, dv:dv + 1, :]
    o = (a0 - lam_ref[0] * a1).T
    o = o * lax.rsqrt(jnp.mean(o * o, axis=-1, keepdims=True) + 1e-5) * sub_ref[...]
    o_ref[...] = (o * out_scale).astype(o_ref.dtype)


def diff_attention(qkv, lam_full, subln, *, seq, ctx_len, lambda_init):
    n_rows = qkv.shape[0]
    tq = _token_tile(n_rows)
    hw = 2 * DA_HEAD_DIM
    ones_rows = 2 * SUBLANE
    qt = qkv[:, :DA_WIDTH].T
    vt = qkv[:, 2 * DA_WIDTH:].T.reshape(DA_HEADS, hw, n_rows)
    vt = jnp.concatenate([vt, jnp.ones((DA_HEADS, ones_rows, n_rows), BF16)], axis=1)
    vt = vt.reshape(DA_HEADS * (hw + ones_rows), n_rows)
    return pl.pallas_call(
        functools.partial(_flash_kernel, kv=tq, seq=seq, ctx_len=ctx_len, out_scale=1.0 - lambda_init),
        grid=(DA_HEADS, n_rows // tq),
        in_specs=[pl.BlockSpec(memory_space=pltpu.SMEM),
                  pl.BlockSpec((hw, tq), lambda h, i: (h, i)),
                  pl.BlockSpec((n_rows, hw), lambda h, i: (0, DA_HEADS + h)),
                  pl.BlockSpec((hw + ones_rows, n_rows), lambda h, i: (h, 0)),
                  pl.BlockSpec((1, hw), lambda h, i: (0, 0))],
        out_specs=pl.BlockSpec((tq, hw), lambda h, i: (i, h)),
        out_shape=jax.ShapeDtypeStruct((n_rows, DA_WIDTH), BF16),
        scratch_shapes=[pltpu.VMEM((2, 1, tq), F32), pltpu.VMEM((2, hw + ones_rows, tq), F32)],
        compiler_params=pltpu.CompilerParams(dimension_semantics=("parallel", "parallel"),
                                             vmem_limit_bytes=FLASH_VMEM_LIMIT),
        name="diff_attention",
    )(lam_full.reshape(1), qt, qkv, vt, subln.reshape(1, hw))


def _route_kernel(lg_ref, b_ref, o_ref):
    t = lg_ref.shape[1]
    scores = jax.nn.sigmoid(lg_ref[...])
    choice = (scores + b_ref[...]).reshape(N_GROUPS, GROUP_SIZE, t)
    s3 = scores.reshape(N_GROUPS, GROUP_SIZE, t)
    member = lax.broadcasted_iota(jnp.int32, choice.shape, 1)
    group = lax.broadcasted_iota(jnp.int32, (N_GROUPS, 1, t), 0)
    expert = lax.broadcasted_iota(jnp.int32, choice.shape, 0) * GROUP_SIZE + member
    neg_inf = -jnp.inf
    m1 = jnp.max(choice, axis=1, keepdims=True)
    first = jnp.min(jnp.where(choice == m1, member, GROUP_SIZE), axis=1, keepdims=True)
    m2 = jnp.max(jnp.where(member == first, neg_inf, choice), axis=1, keepdims=True)
    gscore = m1 + m2
    gsel = jnp.zeros(gscore.shape, F32)
    for _ in range(TOPK_GROUPS):
        m = jnp.max(gscore, axis=0, keepdims=True)
        f = jnp.min(jnp.where(gscore == m, group, N_GROUPS), axis=0, keepdims=True)
        hit = group == f
        gsel = jnp.where(hit, 1.0, gsel)
        gscore = jnp.where(hit, neg_inf, gscore)
    cand = jnp.where(gsel > 0.0, choice, neg_inf)
    esel = jnp.zeros(choice.shape, F32)
    for _ in range(TOP_K):
        m = jnp.max(jnp.max(cand, axis=1, keepdims=True), axis=0, keepdims=True)
        f = jnp.min(jnp.min(jnp.where(cand == m, expert, N_EXPERTS), axis=1, keepdims=True), axis=0, keepdims=True)
        hit = expert == f
        esel = jnp.where(hit, 1.0, esel)
        cand = jnp.where(hit, neg_inf, cand)
    w = s3 * esel
    denom = jnp.sum(jnp.sum(w, axis=1, keepdims=True), axis=0, keepdims=True) + 1e-20
    o_ref[...] = (w / denom * ROUTED_SCALE).reshape(N_EXPERTS, t)


def route(logits_t, bias):
    n = logits_t.shape[1]
    t = _token_tile(n)
    return pl.pallas_call(
        _route_kernel,
        grid=(n // t,),
        in_specs=[pl.BlockSpec((N_EXPERTS, t), lambda i: (0, i)), pl.BlockSpec((N_EXPERTS, 1), lambda i: (0, 0))],
        out_specs=pl.BlockSpec((N_EXPERTS, t), lambda i: (0, i)),
        out_shape=jax.ShapeDtypeStruct((N_EXPERTS, n), F32),
        compiler_params=_params(("parallel",)),
        name="route",
    )(logits_t, bias.reshape(N_EXPERTS, 1))


def _moe_kernel(h_ref, gt_ref, wg_ref, wu_ref, wd_ref, swg_ref, swu_ref, swd_ref, o_ref, g_ref):
    e = pl.program_id(1)
    h = h_ref[...]

    def ffn(wg, wu, wd, row_scale):
        a = jnp.dot(h, wg, preferred_element_type=F32)
        a = a * jax.nn.sigmoid(a) * jnp.dot(h, wu, preferred_element_type=F32)
        if row_scale is not None:
            a = a * row_scale
        return jnp.dot(a.astype(BF16), wd, preferred_element_type=F32)

    @pl.when(e == 0)
    def _():
        g_ref[...] = gt_ref[...].T
        o_ref[...] = ffn(swg_ref[...], swu_ref[...], swd_ref[...], None)

    g = g_ref[...]
    lane = lax.broadcasted_iota(jnp.int32, g.shape, 1)
    gcol = jnp.sum(jnp.where(lane == e, g, 0.0), axis=1, keepdims=True)
    o_ref[...] += ffn(wg_ref[0], wu_ref[0], wd_ref[0], gcol)


def moe(h, gates_t, wg, wu, wd, swg, swu, swd):
    n, d = h.shape
    t = _token_tile(n)
    f = wg.shape[-1]
    return pl.pallas_call(
        _moe_kernel,
        grid=(n // t, N_EXPERTS),
        in_specs=[pl.BlockSpec((t, d), lambda i, e: (i, 0)),
                  pl.BlockSpec((N_EXPERTS, t), lambda i, e: (0, i)),
                  pl.BlockSpec((1, d, f), lambda i, e: (e, 0, 0)),
                  pl.BlockSpec((1, d, f), lambda i, e: (e, 0, 0)),
                  pl.BlockSpec((1, f, d), lambda i, e: (e, 0, 0)),
                  pl.BlockSpec((d, f), lambda i, e: (0, 0)),
                  pl.BlockSpec((d, f), lambda i, e: (0, 0)),
                  pl.BlockSpec((f, d), lambda i, e: (0, 0))],
        out_specs=pl.BlockSpec((t, d), lambda i, e: (i, 0)),
        out_shape=jax.ShapeDtypeStruct((n, d), F32),
        scratch_shapes=[pltpu.VMEM((t, N_EXPERTS), F32)],
        compiler_params=_params(("parallel", "arbitrary")),
        name="moe",
    )(h, gates_t, wg, wu, wd, swg, swu, swd)


def mixer_ab(h, w_in, w_out, decay_logit, conv_w, conv_b, w1, b1, f1, w2, b2, f2, w3, skip, rope, dft, *, seq,
             ctx_len):
    n_rows = h.shape[0]
    tm = _token_tile(n_rows)
    p = mm([(h, w_in.astype(BF16))], F32, tm, 512, name="ab_in_proj")
    qkv_w = 2 * RET_QK + RET_V
    n_qk = 2 * RET_QK // LANE
    scales = (1.0,) * (RET_QK // LANE) + (RET_DK ** -0.5,) * (RET_QK // LANE) + (1.0,) * (RET_V // LANE)
    qkv = rope_cast(p, rope[0], rope[1], width=qkv_w, n_rot_blocks=n_qk, head_dim=RET_DK, scales=scales,
                    out_dtype=F32)
    log_g = jax.nn.log_sigmoid(decay_logit.astype(F32))
    ret = retention(qkv, p, log_g, jnp.exp(RET_CHUNK * log_g), seq=seq)
    v, x1, x2 = shortconv(p, conv_w, conv_b, seq=seq)
    filt = (w1, b1, f1, w2, b2, f2, w3)
    hy_x = long_conv_two_stage(dft["x"], hyena_filter_taps(seq, *filt), v[:seq], x1[:seq], x2[:seq], skip)
    hy_c = long_conv_one_stage(dft["c"], hyena_filter_taps(ctx_len, *filt), v[seq:], x1[seq:], x2[seq:], skip)
    hy = jnp.concatenate([hy_x, hy_c], axis=0)
    w_out = w_out.astype(BF16)
    return mm([(ret, w_out[:RET_V]), (hy, w_out[RET_V:])], F32, tm, 512, name="ab_out_proj")


def mixer_da(h, w_in, w_out, lam, subln, lambda_init, rope, *, seq, ctx_len):
    n_rows = h.shape[0]
    tm = _token_tile(n_rows)
    p = mm([(h, w_in.astype(BF16))], F32, tm, 512, name="da_in_proj")
    n_qk = 2 * DA_WIDTH // LANE
    scales = (LOG2_E * DA_HEAD_DIM ** -0.5,) * (DA_WIDTH // LANE) + (1.0,) * (2 * DA_WIDTH // LANE)
    qkv = rope_cast(p, rope[0], rope[1], width=3 * DA_WIDTH, n_rot_blocks=n_qk, head_dim=DA_HEAD_DIM, scales=scales,
                    out_dtype=BF16)
    lam_f = lam.astype(F32)
    lam_full = jnp.exp(jnp.sum(lam_f[0] * lam_f[1])) - jnp.exp(jnp.sum(lam_f[2] * lam_f[3])) + lambda_init
    o = diff_attention(qkv, lam_full, subln, seq=seq, ctx_len=ctx_len, lambda_init=lambda_init)
    return mm([(o, w_out.astype(BF16))], F32, tm, 512, name="da_out_proj")


def kernel(x, c, ctx, c_ctx, w_ada, b_ada, norm_mix, norm_ffn, ab_w_in, ab_w_out, ret_decay_logit, hy_conv_w, hy_conv_b, hy_w1, hy_b1, hy_freq1, hy_w2, hy_b2, hy_freq2, hy_w3, hy_skip, da_w_in, da_w_out, da_lambda, da_subln, router_w, router_b, exp_w_gate, exp_w_up, exp_w_down, sh_w_gate, sh_w_up, sh_w_down, norm_final):
    batch, seq, d = x.shape
    ctx_len = ctx.shape[1]
    assert batch == 1 and seq % ROW_TILE == 0 and ctx_len == ROW_TILE
    depth = w_ada.shape[0]
    n_rows = seq + ctx_len
    ctx_tile = seq // ROW_TILE

    xs = jnp.concatenate([x[0], ctx[0]], axis=0)
    cv = jnp.zeros((SUBLANE, d), F32).at[0].set(c_ctx).at[1].set(c[0])
    mods = adaln(cv, w_ada, b_ada)[:, :2].reshape(depth, 2, 6, d)

    rope_ret = rope_tables(seq, ctx_len, RET_DK)
    rope_da = rope_tables(seq, ctx_len, DA_HEAD_DIM)
    dft = dict(x=dft_tables_two_stage(2 * seq), c=dft_tables_one_stage(2 * ctx_len))
    common = dict(n_rows=n_rows, ctx_tile=ctx_tile)

    delta, gate_mods = None, None
    for i in range(depth):
        j = i // 2
        if delta is None:
            (h,) = norm_mod(xs, norm_mix[i], mods=mods[i], shift_idx=0, scale_idx=1, **common)
        else:
            xs, h = norm_mod(xs, norm_mix[i], delta=delta, gate_mods=gate_mods, gate_idx=5, mods=mods[i], shift_idx=0,
                             scale_idx=1, **common)
        if i % 2 == 0:
            y = mixer_ab(h, ab_w_in[j], ab_w_out[j], ret_decay_logit[j], hy_conv_w[j], hy_conv_b[j], hy_w1[j],
                         hy_b1[j], hy_freq1[j], hy_w2[j], hy_b2[j], hy_freq2[j], hy_w3[j], hy_skip[j], rope_ret, dft,
                         seq=seq, ctx_len=ctx_len)
        else:
            lambda_init = 0.8 - 0.6 * math.exp(-0.3 * i)
            y = mixer_da(h, da_w_in[j], da_w_out[j], da_lambda[j], da_subln[j], lambda_init, rope_da, seq=seq,
                         ctx_len=ctx_len)
        xs, h, logits_t = norm_mod(xs, norm_ffn[i], delta=y, gate_mods=mods[i], gate_idx=2, mods=mods[i], shift_idx=3,
                                   scale_idx=4, router_wt=router_w[i].T, **common)
        gates_t = route(logits_t, router_b[i])
        delta = moe(h, gates_t, exp_w_gate[i].astype(BF16), exp_w_up[i].astype(BF16), exp_w_down[i].astype(BF16),
                    sh_w_gate[i].astype(BF16), sh_w_up[i].astype(BF16), sh_w_down[i].astype(BF16))
        gate_mods = mods[i]
    _, out = norm_mod(xs, norm_final, n_rows=seq, ctx_tile=ctx_tile, delta=delta, gate_mods=gate_mods, gate_idx=5,
                      out_dtype=F32)
    return out[None]
```

```python
import functools
import math

import jax
import jax.numpy as jnp
from jax import lax
from jax.experimental import pallas as pl
from jax.experimental.pallas import tpu as pltpu
from jax.experimental.pallas import tpu_sc as plsc

F32 = jnp.float32
BF16 = jnp.bfloat16
HIGHEST = lax.Precision.HIGHEST

D_MODEL = 1024
DEPTH = 4
GRID_W = 64
EPS = 1e-6
ROPE_BASE = 10000.0

RET_HEADS = 4
RET_DK = 128
RET_DV = 256
RET_CHUNK = 128
RET_QK = RET_HEADS * RET_DK
RET_V = RET_HEADS * RET_DV

HY_WIDTH = 512
HY_ORDER = 2
HY_BANDS = 16
HY_EMB = 2 * HY_BANDS + 1
HY_FFN = 64
HY_DECAY_TARGET = 1e-2
HY_FAST_DECAY = 0.3
HY_SLOW_DECAY = 1.5
HY_ZCOLS = 64
HY_VALID_COL = HY_EMB
FFT_N2 = 128

AB_IN = 2 * RET_QK + 2 * RET_V + (HY_ORDER + 1) * HY_WIDTH
AB_CAT = RET_V + HY_WIDTH

DA_HEADS = 8
DA_HEAD_DIM = 64
DA_WIDTH = DA_HEADS * 2 * DA_HEAD_DIM

N_EXPERTS = 64
TOP_K = 8
N_GROUPS = 8
TOPK_GROUPS = 4
GROUP_SIZE = N_EXPERTS // N_GROUPS
EXPERT_DIM = 256
ROUTED_SCALE = 2.5
MOE_BLOCK = 256
SC_CORES = 2
SC_SUBCORES = 16
SC_WORKERS = SC_CORES * SC_SUBCORES
SC_WINDOW = 64

LANE = 128
SUBLANE = 8
ROW_TILE = 256
MAX_TOKEN_TILE = 1280
VMEM_LIMIT = 48 * 1024 * 1024
FLASH_VMEM_LIMIT = 56 * 1024 * 1024
NEG_BIG = -1e30
LOG2_E = 1.4426950408889634


def _params(sem):
    return pltpu.CompilerParams(dimension_semantics=sem, vmem_limit_bytes=VMEM_LIMIT)


def _token_tile(n):
    best = ROW_TILE
    t = ROW_TILE
    while t <= min(n, MAX_TOKEN_TILE):
        if n % t == 0:
            best = t
        t += ROW_TILE
    return best


def _mm_kernel(*refs, n_pairs, has_epi):
    acc = None
    for p in range(n_pairs):
        a = refs[2 * p][...].astype(BF16)
        b = refs[2 * p + 1][...].astype(BF16)
        d = jnp.dot(a, b, preferred_element_type=F32)
        acc = d if acc is None else acc + d
    idx = 2 * n_pairs
    if has_epi:
        acc = refs[idx][...] * (acc + refs[idx + 1][...] * refs[idx + 2][...])
        idx += 3
    o_ref = refs[idx]
    o_ref[...] = acc.astype(o_ref.dtype)


def mm(pairs, out_dtype, tm, tn, epi=None, name="mm"):
    m = pairs[0][0].shape[0]
    n = pairs[0][1].shape[1]
    assert m % tm == 0 and n % tn == 0
    in_specs, args = [], []
    for a, b in pairs:
        k = a.shape[1]
        in_specs += [pl.BlockSpec((tm, k), lambda i, j: (i, 0)), pl.BlockSpec((k, tn), lambda i, j: (0, j))]
        args += [a, b]
    if epi is not None:
        in_specs += [pl.BlockSpec((tm, tn), lambda i, j: (i, j)), pl.BlockSpec((1, tn), lambda i, j: (0, j)),
                     pl.BlockSpec((tm, tn), lambda i, j: (i, j))]
        args += list(epi)
    return pl.pallas_call(
        functools.partial(_mm_kernel, n_pairs=len(pairs), has_epi=epi is not None),
        grid=(m // tm, n // tn),
        in_specs=in_specs,
        out_specs=pl.BlockSpec((tm, tn), lambda i, j: (i, j)),
        out_shape=jax.ShapeDtypeStruct((m, n), out_dtype),
        compiler_params=_params(("parallel", "parallel")),
        name=name,
    )(*args)


def _adaln_kernel(cv_ref, w_ref, b_ref, o_ref):
    cv = cv_ref[...]
    s = cv * jax.nn.sigmoid(cv)
    o_ref[0] = jnp.dot(s, w_ref[0], precision=HIGHEST, preferred_element_type=F32) + b_ref[0]


def adaln(cv, w_ada, b_ada):
    depth, d, n = w_ada.shape
    tn = 1536
    return pl.pallas_call(
        _adaln_kernel,
        grid=(depth, n // tn),
        in_specs=[pl.BlockSpec((SUBLANE, d), lambda l, j: (0, 0)),
                  pl.BlockSpec((1, d, tn), lambda l, j: (l, 0, j)),
                  pl.BlockSpec((1, 1, tn), lambda l, j: (l, 0, j))],
        out_specs=pl.BlockSpec((1, SUBLANE, tn), lambda l, j: (l, 0, j)),
        out_shape=jax.ShapeDtypeStruct((depth, SUBLANE, n), F32),
        compiler_params=_params(("parallel", "parallel")),
        name="adaln",
    )(cv, w_ada, b_ada.reshape(depth, 1, n))


def _norm_mod_kernel(*refs, has_delta, gate_idx, shift_idx, scale_idx, has_router, write_xs):
    it = iter(refs)
    xs_ref = next(it)
    x = xs_ref[...]
    if has_delta:
        delta_ref = next(it)
        gmods_ref = next(it)
        x = x + gmods_ref[0, gate_idx:gate_idx + 1, :] * delta_ref[...]
    mods_ref = next(it) if shift_idx is not None else None
    g_ref = next(it)
    wr_ref = next(it) if has_router else None
    if write_xs:
        next(it)[...] = x
    h_ref = next(it)
    y = x * lax.rsqrt(jnp.mean(x * x, axis=-1, keepdims=True) + EPS) * g_ref[...]
    if shift_idx is not None:
        y = y * (1.0 + mods_ref[0, scale_idx:scale_idx + 1, :]) + mods_ref[0, shift_idx:shift_idx + 1, :]
    h_ref[...] = y.astype(h_ref.dtype)
    if has_router:
        lg_ref = next(it)
        lg_ref[...] = lax.dot_general(wr_ref[...], y, (((1,), (1,)), ((), ())),
                                      precision=HIGHEST, preferred_element_type=F32)


def norm_mod(xs, g, *, n_rows, ctx_tile, delta=None, gate_mods=None, gate_idx=None, mods=None, shift_idx=None,
             scale_idx=None, router_wt=None, out_dtype=BF16):
    d = xs.shape[1]
    n_tiles = n_rows // ROW_TILE
    row = pl.BlockSpec((ROW_TILE, d), lambda i: (i, 0))
    mod_spec = pl.BlockSpec((1, 6, d), lambda i: (jnp.where(i == ctx_tile, 0, 1), 0, 0))
    in_specs, args = [row], [xs]
    has_delta = delta is not None
    if has_delta:
        in_specs += [row, mod_spec]
        args += [delta, gate_mods]
    if shift_idx is not None:
        in_specs.append(mod_spec)
        args.append(mods)
    in_specs.append(pl.BlockSpec((1, d), lambda i: (0, 0)))
    args.append(g.reshape(1, d))
    has_router = router_wt is not None
    if has_router:
        in_specs.append(pl.BlockSpec(router_wt.shape, lambda i: (0, 0)))
        args.append(router_wt)
    out_specs, out_shape = [], []
    if has_delta:
        out_specs.append(row)
        out_shape.append(jax.ShapeDtypeStruct((n_rows, d), F32))
    out_specs.append(row)
    out_shape.append(jax.ShapeDtypeStruct((n_rows, d), out_dtype))
    if has_router:
        out_specs.append(pl.BlockSpec((N_EXPERTS, ROW_TILE), lambda i: (0, i)))
        out_shape.append(jax.ShapeDtypeStruct((N_EXPERTS, n_rows), F32))
    return pl.pallas_call(
        functools.partial(_norm_mod_kernel, has_delta=has_delta, gate_idx=gate_idx, shift_idx=shift_idx,
                          scale_idx=scale_idx, has_router=has_router, write_xs=has_delta),
        grid=(n_tiles,),
        in_specs=in_specs,
        out_specs=out_specs,
        out_shape=out_shape,
        compiler_params=_params(("parallel",)),
        name="norm_mod",
    )(*args)


def _rope_kernel(p_ref, cos_ref, sin_ref, o_ref, *, n_rot_blocks, head_dim, scales):
    cos = cos_ref[...]
    sin = sin_ref[...]
    for b in range(len(scales)):
        x = p_ref[:, b * LANE:(b + 1) * LANE]
        if b < n_rot_blocks:
            if head_dim == LANE:
                rot = pltpu.roll(x, LANE // 2, 1)
            else:
                lane = lax.broadcasted_iota(jnp.int32, x.shape, 1)
                first_half = (lane % head_dim) < head_dim // 2
                rot = jnp.where(first_half, pltpu.roll(x, LANE - head_dim // 2, 1), pltpu.roll(x, head_dim // 2, 1))
            x = x * cos + rot * sin
        if scales[b] != 1.0:
            x = x * scales[b]
        o_ref[:, b * LANE:(b + 1) * LANE] = x.astype(o_ref.dtype)


def rope_cast(p, cos, sin, *, width, n_rot_blocks, head_dim, scales, out_dtype):
    n_rows = p.shape[0]
    return pl.pallas_call(
        functools.partial(_rope_kernel, n_rot_blocks=n_rot_blocks, head_dim=head_dim, scales=scales),
        grid=(n_rows // ROW_TILE,),
        in_specs=[pl.BlockSpec((ROW_TILE, width), lambda i: (i, 0)),
                  pl.BlockSpec((ROW_TILE, LANE), lambda i: (i, 0)),
                  pl.BlockSpec((ROW_TILE, LANE), lambda i: (i, 0))],
        out_specs=pl.BlockSpec((ROW_TILE, width), lambda i: (i, 0)),
        out_shape=jax.ShapeDtypeStruct((n_rows, width), out_dtype),
        compiler_params=_params(("parallel",)),
        name="rope_cast",
    )(p, cos, sin)


def rope_tables(seq, ctx_len, head_dim):
    n_freq = head_dim // 4
    inv = ROPE_BASE ** (-jnp.arange(n_freq, dtype=F32) / n_freq)
    rows = seq // GRID_W
    row = jnp.repeat(jnp.arange(rows, dtype=F32), GRID_W)
    col = jnp.tile(jnp.arange(GRID_W, dtype=F32), rows)
    ang = jnp.concatenate([row[:, None] * inv, col[:, None] * inv], axis=-1)
    cos, sin = jnp.cos(ang), jnp.sin(ang)
    cos = jnp.concatenate([cos, cos], axis=-1)
    sin = jnp.concatenate([-sin, sin], axis=-1)
    reps = LANE // head_dim
    cos, sin = jnp.tile(cos, (1, reps)), jnp.tile(sin, (1, reps))
    cos = jnp.concatenate([cos, jnp.ones((ctx_len, LANE), F32)], axis=0)
    sin = jnp.concatenate([sin, jnp.zeros((ctx_len, LANE), F32)], axis=0)
    return cos, sin


def _ret_kernel(lg_ref, gc_ref, q_ref, k_ref, v_ref, *rest, reverse):
    if reverse:
        yf_ref, gate_ref, o_ref, s_ref = rest
    else:
        o_ref, s_ref = rest
    c = RET_CHUNK

    @pl.when(pl.program_id(0) == 0)
    def _():
        s_ref[...] = jnp.zeros_like(s_ref)

    ii = lax.broadcasted_iota(jnp.int32, (c, c), 0)
    jj = lax.broadcasted_iota(jnp.int32, (c, c), 1)
    rel = ((jj - ii) if reverse else (ii - jj)).astype(F32)
    pos = lax.broadcasted_iota(jnp.int32, (c, 1), 0).astype(F32)
    for h in range(RET_HEADS):
        lg = lg_ref[h]
        dec = jnp.where(rel >= 0, jnp.exp(jnp.maximum(rel, 0.0) * lg), 0.0)
        if reverse:
            q_dec = jnp.exp((c - pos) * lg)
            k_dec = jnp.exp(pos * lg)
        else:
            q_dec = jnp.exp((pos + 1.0) * lg)
            k_dec = jnp.exp((c - 1.0 - pos) * lg)
        q = q_ref[:, h * RET_DK:(h + 1) * RET_DK]
        k = k_ref[:, h * RET_DK:(h + 1) * RET_DK]
        v = v_ref[:, h * RET_DV:(h + 1) * RET_DV].astype(BF16)
        s = lax.dot_general(q.astype(BF16), k.astype(BF16), (((1,), (1,)), ((), ())),
                            preferred_element_type=F32) * dec
        state = s_ref[h]
        y = jnp.dot(s.astype(BF16), v, preferred_element_type=F32)
        y = y + jnp.dot((q * q_dec).astype(BF16), state.astype(BF16), preferred_element_type=F32)
        upd = lax.dot_general((k * k_dec).astype(BF16), v, (((0,), (0,)), ((), ())), preferred_element_type=F32)
        s_ref[h] = gc_ref[h] * state + upd
        if reverse:
            r = y + yf_ref[:, h * RET_DV:(h + 1) * RET_DV]
            mu = jnp.mean(r, axis=-1, keepdims=True)
            rc = r - mu
            var = jnp.mean(rc * rc, axis=-1, keepdims=True)
            g = gate_ref[:, h * RET_DV:(h + 1) * RET_DV]
            o_ref[:, h * RET_DV:(h + 1) * RET_DV] = (rc * lax.rsqrt(var + EPS) * (g * jax.nn.sigmoid(g))).astype(
                o_ref.dtype)
        else:
            o_ref[:, h * RET_DV:(h + 1) * RET_DV] = y


def retention(qkv, p, log_g, g_chunk, *, seq):
    n_rows = qkv.shape[0]
    n_chunks = n_rows // RET_CHUNK
    n_x = seq // RET_CHUNK
    smem = pl.BlockSpec(memory_space=pltpu.SMEM)

    def run(reverse, extra):
        if reverse:
            idx = lambda t: n_chunks - 1 - t
        else:
            idx = lambda t: (t + n_x) % n_chunks
        in_specs = [smem, smem,
                    pl.BlockSpec((RET_CHUNK, RET_QK), lambda t: (idx(t), 0)),
                    pl.BlockSpec((RET_CHUNK, RET_QK), lambda t: (idx(t), 1)),
                    pl.BlockSpec((RET_CHUNK, RET_V), lambda t: (idx(t), 1))]
        args = [log_g[1 if reverse else 0], g_chunk[1 if reverse else 0], qkv, qkv, qkv]
        if reverse:
            in_specs += [pl.BlockSpec((RET_CHUNK, RET_V), lambda t: (idx(t), 0)),
                         pl.BlockSpec((RET_CHUNK, RET_V), lambda t: (idx(t), 2))]
            args += list(extra)
        return pl.pallas_call(
            functools.partial(_ret_kernel, reverse=reverse),
            grid=(n_chunks,),
            in_specs=in_specs,
            out_specs=pl.BlockSpec((RET_CHUNK, RET_V), lambda t: (idx(t), 0)),
            out_shape=jax.ShapeDtypeStruct((n_rows, RET_V), BF16 if reverse else F32),
            scratch_shapes=[pltpu.VMEM((RET_HEADS, RET_DK, RET_DV), F32)],
            compiler_params=_params(("arbitrary",)),
            name="retention_bwd" if reverse else "retention_fwd",
        )(*args)

    y_fwd = run(False, None)
    return run(True, (y_fwd, p))


def _shortconv_kernel(cur_ref, prev_ref, next_ref, w_ref, b_ref, v_ref, x1_ref, x2_ref, *, x_tiles):
    i = pl.program_id(0)
    cur = cur_ref[...]
    rows = cur.shape[0]
    row = lax.broadcasted_iota(jnp.int32, (rows, 1), 0)
    has_prev = jnp.where((i == 0) | (i == x_tiles), 0.0, 1.0)
    has_next = jnp.where((i == x_tiles - 1) | (i == x_tiles), 0.0, 1.0)
    up = jnp.where(row == 0, prev_ref[SUBLANE - 1:SUBLANE, :] * has_prev, pltpu.roll(cur, 1, 0))
    dn = jnp.where(row == rows - 1, next_ref[0:1, :] * has_next, pltpu.roll(cur, rows - 1, 0))
    y = up * w_ref[0:1, :] + cur * w_ref[1:2, :] + dn * w_ref[2:3, :] + b_ref[...]
    v_ref[...] = y[:, :HY_WIDTH]
    x1_ref[...] = y[:, HY_WIDTH:2 * HY_WIDTH]
    x2_ref[...] = y[:, 2 * HY_WIDTH:]


def shortconv(p, w, b, *, seq):
    n_rows = p.shape[0]
    width = 3 * HY_WIDTH
    col = p.shape[1] // width - 1
    per = ROW_TILE // SUBLANE
    last = n_rows // SUBLANE - 1
    out = jax.ShapeDtypeStruct((n_rows, HY_WIDTH), F32)
    ospec = pl.BlockSpec((ROW_TILE, HY_WIDTH), lambda i: (i, 0))
    return pl.pallas_call(
        functools.partial(_shortconv_kernel, x_tiles=seq // ROW_TILE),
        grid=(n_rows // ROW_TILE,),
        in_specs=[pl.BlockSpec((ROW_TILE, width), lambda i: (i, col)),
                  pl.BlockSpec((SUBLANE, width), lambda i: (jnp.maximum(i * per - 1, 0), col)),
                  pl.BlockSpec((SUBLANE, width), lambda i: (jnp.minimum((i + 1) * per, last), col)),
                  pl.BlockSpec((3, width), lambda i: (0, 0)),
                  pl.BlockSpec((1, width), lambda i: (0, 0))],
        out_specs=[ospec, ospec, ospec],
        out_shape=[out, out, out],
        compiler_params=_params(("parallel",)),
        name="shortconv",
    )(p, p, p, w, b.reshape(1, width))


def _filt_kernel(z_ref, w1_ref, b1_ref, f1_ref, w2_ref, b2_ref, f2_ref, w3a_ref, w3b_ref, dl_ref, o_ref):
    z = z_ref[...]
    h = jnp.sin(f1_ref[...] * (jnp.dot(z, w1_ref[...], precision=HIGHEST, preferred_element_type=F32) + b1_ref[...]))
    h = jnp.sin(f2_ref[...] * (jnp.dot(h, w2_ref[...], precision=HIGHEST, preferred_element_type=F32) + b2_ref[...]))
    window = jnp.exp(-z[:, 0:1] * dl_ref[...]) * z[:, HY_VALID_COL:HY_VALID_COL + 1]
    for o, w3_ref in enumerate((w3a_ref, w3b_ref)):
        o_ref[o] = jnp.dot(h, w3_ref[...], precision=HIGHEST, preferred_element_type=F32) * window


def hyena_filter_taps(length, w1, b1, f1, w2, b2, f2, w3):
    z = _filter_positions(length)
    w1p = jnp.zeros((HY_ZCOLS, HY_FFN), F32).at[:HY_EMB].set(w1)
    deltas = jnp.abs(jnp.linspace(math.log(HY_DECAY_TARGET) / HY_SLOW_DECAY,
                                  math.log(HY_DECAY_TARGET) / HY_FAST_DECAY, HY_WIDTH, dtype=F32)).reshape(1, HY_WIDTH)
    tm = min(length, 512)
    half_tiles = length // tm
    vec = lambda a: a.reshape(1, HY_FFN)
    small = lambda shape: pl.BlockSpec(shape, lambda i: (0, 0))
    w3_spec = lambda o: pl.BlockSpec((HY_FFN, HY_WIDTH), lambda i: (0, 2 * o + jnp.where(i >= half_tiles, 1, 0)))
    assert HY_ORDER == 2
    return pl.pallas_call(
        _filt_kernel,
        grid=(2 * half_tiles,),
        in_specs=[pl.BlockSpec((tm, HY_ZCOLS), lambda i: (i, 0)),
                  small((HY_ZCOLS, HY_FFN)), small((1, HY_FFN)), small((1, HY_FFN)),
                  small((HY_FFN, HY_FFN)), small((1, HY_FFN)), small((1, HY_FFN)),
                  w3_spec(0), w3_spec(1), small((1, HY_WIDTH))],
        out_specs=pl.BlockSpec((HY_ORDER, tm, HY_WIDTH), lambda i: (0, i, 0)),
        out_shape=jax.ShapeDtypeStruct((HY_ORDER, 2 * length, HY_WIDTH), F32),
        compiler_params=_params(("parallel",)),
        name="hyena_filter",
    )(z, w1p, vec(b1), vec(f1), w2, vec(b2), vec(f2), w3, w3, deltas)


def _filter_positions(length):
    t = jnp.concatenate([jnp.arange(length, dtype=F32), float(length) - jnp.arange(length, dtype=F32)])
    valid = jnp.ones((2 * length,), F32).at[length].set(0.0)
    t_norm = t / max(length - 1, 1)
    bands = jnp.linspace(1e-4, HY_BANDS - 1, HY_BANDS, dtype=F32)
    ang = (2.0 * math.pi / length) * t[:, None] * bands[None, :]
    z = jnp.concatenate([t_norm[:, None], jnp.cos(ang), -jnp.sin(ang), valid[:, None]], axis=-1)
    return jnp.pad(z, ((0, 0), (0, HY_ZCOLS - z.shape[1])))


def _angles(num, den):
    return (2.0 * math.pi / den) * (num % den).astype(F32)


def dft_tables_two_stage(m):
    n2 = FFT_N2
    n1 = m // n2
    half = n1 // 2
    kp = -(-(half + 1) // SUBLANE) * SUBLANE
    k1 = jnp.arange(kp, dtype=jnp.int32)
    live = (k1 <= half)
    a1 = _angles(k1[:, None] * jnp.arange(n1, dtype=jnp.int32)[None, :], n1)
    f1 = jnp.concatenate([jnp.where(live[:, None], jnp.cos(a1), 0.0), jnp.where(live[:, None], -jnp.sin(a1), 0.0)], 0)
    wgt = jnp.where((k1 == 0) | (k1 == half), 1.0, 2.0) * live / m
    a1h = a1[:, :half].T
    cinv = jnp.concatenate([jnp.cos(a1h) * wgt[None, :], -jnp.sin(a1h) * wgt[None, :]], axis=1)
    k = k1[:, None, None] + n1 * jnp.arange(n2, dtype=jnp.int32)[None, :, None]
    th = _angles(k * jnp.arange(n2, dtype=jnp.int32)[None, None, :], m)
    c = jnp.where(live[:, None, None], jnp.cos(th), 0.0)
    s = jnp.where(live[:, None, None], jnp.sin(th), 0.0)
    g_fwd = jnp.concatenate([jnp.concatenate([c, s], 2), jnp.concatenate([-s, c], 2)], 1)
    ct, st = jnp.swapaxes(c, 1, 2), jnp.swapaxes(s, 1, 2)
    g_inv = jnp.concatenate([jnp.concatenate([ct, -st], 2), jnp.concatenate([st, ct], 2)], 1)
    return dict(n1=n1, kp=kp, f1=f1.astype(BF16), f1_half=f1[:, :half].astype(BF16), cinv=cinv.astype(BF16),
                g_fwd=g_fwd.astype(BF16), g_inv=g_inv.astype(BF16))


def dft_tables_one_stage(m):
    half = m // 2
    kp = -(-(half + 1) // SUBLANE) * SUBLANE
    k = jnp.arange(kp, dtype=jnp.int32)
    live = (k <= half)
    a = _angles(k[:, None] * jnp.arange(m, dtype=jnp.int32)[None, :], m)
    f = jnp.concatenate([jnp.where(live[:, None], jnp.cos(a), 0.0), jnp.where(live[:, None], -jnp.sin(a), 0.0)], 0)
    wgt = jnp.where((k == 0) | (k == half), 1.0, 2.0) * live / m
    ah = a[:, :half].T
    cinv = jnp.concatenate([jnp.cos(ah) * wgt[None, :], -jnp.sin(ah) * wgt[None, :]], axis=1)
    return dict(kp=kp, f=f.astype(BF16), f_half=f[:, :half].astype(BF16), cinv=cinv.astype(BF16))


def _bmm_kernel(*refs, kb, in_part_major, out_part_major, has_h):
    if has_h:
        g_ref, a_ref, h_ref, o_ref = refs
    else:
        g_ref, a_ref, o_ref = refs
    n2 = FFT_N2
    for b in range(kb):
        if in_part_major:
            ar, ai = a_ref[0, b], a_ref[1, b]
        else:
            ar, ai = a_ref[b, 0], a_ref[b, 1]
        if has_h:
            hr, hi = h_ref[b, 0], h_ref[b, 1]
            ar, ai = ar * hr - ai * hi, ar * hi + ai * hr
        xin = jnp.concatenate([ar, ai], axis=0).astype(BF16)
        y = jnp.dot(g_ref[b], xin, preferred_element_type=F32)
        if out_part_major:
            o_ref[0, b] = y[:n2]
            o_ref[1, b] = y[n2:]
        else:
            o_ref[b, 0] = y[:n2]
            o_ref[b, 1] = y[n2:]


def bmm_k1(g, a, h=None, *, in_part_major, out_part_major):
    kp = g.shape[0]
    n2 = FFT_N2
    c = a.shape[-1]
    kb, tc = SUBLANE, 256
    pm = lambda: pl.BlockSpec((2, kb, n2, tc), lambda i, j: (0, i, 0, j))
    km = lambda: pl.BlockSpec((kb, 2, n2, tc), lambda i, j: (i, 0, 0, j))
    in_specs = [pl.BlockSpec((kb, 2 * n2, 2 * n2), lambda i, j: (i, 0, 0)), pm() if in_part_major else km()]
    args = [g, a]
    if h is not None:
        in_specs.append(km())
        args.append(h)
    return pl.pallas_call(
        functools.partial(_bmm_kernel, kb=kb, in_part_major=in_part_major, out_part_major=out_part_major,
                          has_h=h is not None),
        grid=(kp // kb, c // tc),
        in_specs=in_specs,
        out_specs=pm() if out_part_major else km(),
        out_shape=jax.ShapeDtypeStruct((2, kp, n2, c) if out_part_major else (kp, 2, n2, c), F32),
        compiler_params=_params(("parallel", "parallel")),
        name="dft_inner",
    )(*args)


def _cmul_kernel(x_ref, h_ref, o_ref):
    xr, xi, hr, hi = x_ref[0], x_ref[1], h_ref[0], h_ref[1]
    o_ref[0] = xr * hr - xi * hi
    o_ref[1] = xr * hi + xi * hr


def cmul(x, h):
    spec = pl.BlockSpec(x.shape, lambda i: (0, 0, 0))
    return pl.pallas_call(_cmul_kernel, grid=(1,), in_specs=[spec, spec], out_specs=spec,
                          out_shape=jax.ShapeDtypeStruct(x.shape, F32), compiler_params=_params(("arbitrary",)),
                          name="spectrum_product")(x, h)


def long_conv_two_stage(tabs, taps, v, x1, x2, skip):
    length, c = v.shape
    n2, n1, kp = FFT_N2, tabs["n1"], tabs["kp"]
    cols = n2 * c
    tn = 2048

    def fwd(seq2d, full):
        a = mm([(tabs["f1"] if full else tabs["f1_half"], seq2d)], F32, 2 * kp, tn, name="dft_outer")
        return a.reshape(2, kp, n2, c)

    spectra = [bmm_k1(tabs["g_fwd"], fwd(taps[o].reshape(n1, cols), True), in_part_major=True, out_part_major=False)
               for o in range(HY_ORDER)]
    u = v
    for o, gate in enumerate((x1, x2)):
        xf = bmm_k1(tabs["g_fwd"], fwd(u.reshape(n1 // 2, cols), False), in_part_major=True, out_part_major=False)
        bt = bmm_k1(tabs["g_inv"], xf, spectra[o], in_part_major=False, out_part_major=True)
        skip_row = jnp.tile(skip[o].reshape(1, c), (1, n2))
        u = mm([(tabs["cinv"], bt.reshape(2 * kp, cols))], F32, n1 // 2, tn,
               epi=(gate.reshape(n1 // 2, cols), skip_row, u.reshape(n1 // 2, cols)), name="idft_outer_gate")
        u = u.reshape(length, c)
    return u


def long_conv_one_stage(tabs, taps, v, x1, x2, skip):
    length, c = v.shape
    kp = tabs["kp"]
    u = v
    for o, gate in enumerate((x1, x2)):
        hs = mm([(tabs["f"], taps[o])], F32, 2 * kp, c, name="ctx_dft").reshape(2, kp, c)
        xs = mm([(tabs["f_half"], u)], F32, 2 * kp, c, name="ctx_dft").reshape(2, kp, c)
        ys = cmul(xs, hs).reshape(2 * kp, c)
        u = mm([(tabs["cinv"], ys)], F32, length, c, epi=(gate, skip[o].reshape(1, c), u), name="ctx_idft_gate")
    return u


def _flash_kernel(lam_ref, qt_ref, k_ref, vt_ref, sub_ref, o_ref, m_ref, acc_ref, *, kv, seq, ctx_len, out_scale):
    i = pl.program_id(1)
    last_q = pl.num_programs(1) - 1
    tq = qt_ref.shape[1]
    d = DA_HEAD_DIM
    dv = 2 * DA_HEAD_DIM
    n_chunks = k_ref.shape[0] // kv
    m_ref[...] = jnp.full_like(m_ref, NEG_BIG)
    acc_ref[...] = jnp.zeros_like(acc_ref)

    def run(masked):
        def body(kc, carry):
            off = pl.multiple_of(kc * kv, kv)
            kblk = k_ref[pl.ds(off, kv), :]
            vt = vt_ref[:, pl.ds(off, kv)]
            if masked:
                key = off + lax.broadcasted_iota(jnp.int32, (kv, 1), 0)
                lane = lax.broadcasted_iota(jnp.int32, (1, tq), 1)
                bias = jnp.where(key < seq, NEG_BIG, 0.0) * jnp.where(lane >= tq - ctx_len, 1.0, 0.0)
            for c in range(2):
                s = jnp.dot(kblk[:, c * d:(c + 1) * d], qt_ref[c * d:(c + 1) * d, :], preferred_element_type=F32)
                if masked:
                    s = s + bias
                m_old = m_ref[c]
                m_new = jnp.maximum(m_old, jnp.max(s, axis=0, keepdims=True))
                alpha = jnp.exp2(m_old - m_new)
                pr = jnp.exp2(s - m_new).astype(BF16)
                acc_ref[c] = alpha * acc_ref[c] + jnp.dot(vt, pr, preferred_element_type=F32)
                m_ref[c] = m_new
            return carry

        lax.fori_loop(0, n_chunks, body, 0)

    @pl.when(i != last_q)
    def _():
        run(False)

    @pl.when(i == last_q)
    def _():
        run(True)

    a0 = acc_ref[0, :dv, :] / acc_ref[0, dv:dv + 1, :]
    a1 = acc_ref[1, :dv, :] / acc_ref[1, dv:dv + 1, :]
    o = (a0 - lam_ref[0] * a1).T
    o = o * lax.rsqrt(jnp.mean(o * o, axis=-1, keepdims=True) + 1e-5) * sub_ref[...]
    o_ref[...] = (o * out_scale).astype(o_ref.dtype)


def diff_attention(qkv, lam_full, subln, *, seq, ctx_len, lambda_init):
    n_rows = qkv.shape[0]
    tq = _token_tile(n_rows)
    hw = 2 * DA_HEAD_DIM
    ones_rows = 2 * SUBLANE
    qt = qkv[:, :DA_WIDTH].T
    vt = qkv[:, 2 * DA_WIDTH:].T.reshape(DA_HEADS, hw, n_rows)
    vt = jnp.concatenate([vt, jnp.ones((DA_HEADS, ones_rows, n_rows), BF16)], axis=1)
    vt = vt.reshape(DA_HEADS * (hw + ones_rows), n_rows)
    return pl.pallas_call(
        functools.partial(_flash_kernel, kv=tq, seq=seq, ctx_len=ctx_len, out_scale=1.0 - lambda_init),
        grid=(DA_HEADS, n_rows // tq),
        in_specs=[pl.BlockSpec(memory_space=pltpu.SMEM),
                  pl.BlockSpec((hw, tq), lambda h, i: (h, i)),
                  pl.BlockSpec((n_rows, hw), lambda h, i: (0, DA_HEADS + h)),
                  pl.BlockSpec((hw + ones_rows, n_rows), lambda h, i: (h, 0)),
                  pl.BlockSpec((1, hw), lambda h, i: (0, 0))],
        out_specs=pl.BlockSpec((tq, hw), lambda h, i: (i, h)),
        out_shape=jax.ShapeDtypeStruct((n_rows, DA_WIDTH), BF16),
        scratch_shapes=[pltpu.VMEM((2, 1, tq), F32), pltpu.VMEM((2, hw + ones_rows, tq), F32)],
        compiler_params=pltpu.CompilerParams(dimension_semantics=("parallel", "parallel"),
                                             vmem_limit_bytes=FLASH_VMEM_LIMIT),
        name="diff_attention",
    )(lam_full.reshape(1), qt, qkv, vt, subln.reshape(1, hw))


def _route_kernel(lg_ref, b_ref, tri_ref, eidx_ref, w_ref, rank_ref, cnt_ref, carry_ref):
    t = lg_ref.shape[1]

    @pl.when(pl.program_id(0) == 0)
    def _():
        carry_ref[...] = jnp.zeros_like(carry_ref)

    scores = jax.nn.sigmoid(lg_ref[...])
    choice = (scores + b_ref[...]).reshape(N_GROUPS, GROUP_SIZE, t)
    s3 = scores.reshape(N_GROUPS, GROUP_SIZE, t)
    member = lax.broadcasted_iota(jnp.int32, choice.shape, 1)
    group = lax.broadcasted_iota(jnp.int32, (N_GROUPS, 1, t), 0)
    expert = lax.broadcasted_iota(jnp.int32, choice.shape, 0) * GROUP_SIZE + member
    neg_inf = -jnp.inf
    m1 = jnp.max(choice, axis=1, keepdims=True)
    first = jnp.min(jnp.where(choice == m1, member, GROUP_SIZE), axis=1, keepdims=True)
    m2 = jnp.max(jnp.where(member == first, neg_inf, choice), axis=1, keepdims=True)
    gscore = m1 + m2
    gsel = jnp.zeros(gscore.shape, F32)
    for _ in range(TOPK_GROUPS):
        m = jnp.max(gscore, axis=0, keepdims=True)
        f = jnp.min(jnp.where(gscore == m, group, N_GROUPS), axis=0, keepdims=True)
        hit = group == f
        gsel = jnp.where(hit, 1.0, gsel)
        gscore = jnp.where(hit, neg_inf, gscore)
    cand = jnp.where(gsel > 0.0, choice, neg_inf)
    esel = jnp.zeros(choice.shape, F32)
    picks = []
    for _ in range(TOP_K):
        m = jnp.max(jnp.max(cand, axis=1, keepdims=True), axis=0, keepdims=True)
        f = jnp.min(jnp.min(jnp.where(cand == m, expert, N_EXPERTS), axis=1, keepdims=True), axis=0, keepdims=True)
        hit = expert == f
        esel = jnp.where(hit, 1.0, esel)
        cand = jnp.where(hit, neg_inf, cand)
        picks.append(f)
    w = s3 * esel
    denom = jnp.sum(jnp.sum(w, axis=1, keepdims=True), axis=0, keepdims=True) + 1e-20
    w = w / denom * ROUTED_SCALE
    sel = esel.reshape(N_EXPERTS, t)
    before = jnp.dot(sel.astype(BF16), tri_ref[...], preferred_element_type=F32) + carry_ref[...]
    before = before.reshape(N_GROUPS, GROUP_SIZE, t)
    pick = lambda a, hit: jnp.sum(jnp.sum(jnp.where(hit, a, 0.0), axis=1, keepdims=True), axis=0).reshape(1, t)
    for k, f in enumerate(picks):
        hit = expert == f
        eidx_ref[k:k + 1, :] = f.reshape(1, t)
        w_ref[k:k + 1, :] = pick(w, hit)
        rank_ref[k:k + 1, :] = pick(before, hit).astype(jnp.int32)
    carry_ref[...] += jnp.sum(sel, axis=1, keepdims=True)
    cnt_ref[...] = carry_ref[...]


def route(logits_t, bias):
    n = logits_t.shape[1]
    t = _token_tile(n)
    tri = (jnp.arange(t)[:, None] < jnp.arange(t)[None, :]).astype(BF16)
    tok = lambda dt: jax.ShapeDtypeStruct((TOP_K, n), dt)
    tok_spec = pl.BlockSpec((TOP_K, t), lambda i: (0, i))
    return pl.pallas_call(
        _route_kernel,
        grid=(n // t,),
        in_specs=[pl.BlockSpec((N_EXPERTS, t), lambda i: (0, i)), pl.BlockSpec((N_EXPERTS, 1), lambda i: (0, 0)),
                  pl.BlockSpec((t, t), lambda i: (0, 0))],
        out_specs=[tok_spec, tok_spec, tok_spec, pl.BlockSpec((N_EXPERTS, 1), lambda i: (0, 0))],
        out_shape=[tok(jnp.int32), tok(F32), tok(jnp.int32), jax.ShapeDtypeStruct((N_EXPERTS, 1), F32)],
        scratch_shapes=[pltpu.VMEM((N_EXPERTS, 1), F32)],
        compiler_params=_params(("arbitrary",)),
        name="route",
    )(logits_t, bias.reshape(N_EXPERTS, 1), tri)


def _slot_kernel(start_ref, eidx_ref, rank_ref, dest_ref):
    e = eidx_ref[...]
    d = rank_ref[...]
    for x in range(N_EXPERTS):
        d = d + jnp.where(e == x, start_ref[x], 0)
    dest_ref[...] = d


def slot_index(pad_start, eidx, rank):
    n = eidx.shape[1]
    t = _token_tile(n)
    spec = pl.BlockSpec((TOP_K, t), lambda i: (0, i))
    return pl.pallas_call(
        _slot_kernel,
        grid=(n // t,),
        in_specs=[pl.BlockSpec(memory_space=pltpu.SMEM), spec, spec],
        out_specs=spec,
        out_shape=jax.ShapeDtypeStruct((TOP_K, n), jnp.int32),
        compiler_params=_params(("parallel",)),
        name="slot_index",
    )(pad_start, eidx, rank)


def _sc_worker():
    return lax.axis_index("s") * SC_CORES + lax.axis_index("c")


def sc_dispatch(h, dest3, n_slots):
    n, d = h.shape
    n_win = n // SC_WINDOW
    mesh = plsc.VectorSubcoreMesh(core_axis_name="c", subcore_axis_name="s")

    @functools.partial(
        pl.kernel, mesh=mesh, out_type=jax.ShapeDtypeStruct((n_slots, d), F32),
        scratch_types=[pltpu.VMEM((TOP_K, SC_WINDOW), jnp.int32), pltpu.VMEM((SC_WINDOW, d), F32),
                       pltpu.SemaphoreType.DMA])
    def k(h_hbm, dest_hbm, out_hbm, idx_v, rows_v, sem):
        wid = _sc_worker()

        @pl.loop(0, -(-n_win // SC_WORKERS))
        def _(it):
            w = it * SC_WORKERS + wid

            @pl.when(w < n_win)
            def _():
                pltpu.sync_copy(dest_hbm.at[w], idx_v)
                pltpu.sync_copy(h_hbm.at[pl.ds(w * SC_WINDOW, SC_WINDOW)], rows_v)
                copies = [pltpu.async_copy(rows_v, out_hbm.at[idx_v.at[j]], sem) for j in range(TOP_K)]
                for c in copies:
                    c.wait()

    return k(h, dest3)


def sc_combine_gather(y, dest3):
    d = y.shape[1]
    n_win = dest3.shape[0]
    n = n_win * SC_WINDOW
    mesh = plsc.VectorSubcoreMesh(core_axis_name="c", subcore_axis_name="s")

    @functools.partial(
        pl.kernel, mesh=mesh, out_type=jax.ShapeDtypeStruct((TOP_K, n, d), F32),
        scratch_types=[pltpu.VMEM((TOP_K, SC_WINDOW), jnp.int32), pltpu.VMEM((SC_WINDOW, d), F32),
                       pltpu.SemaphoreType.DMA])
    def k(y_hbm, dest_hbm, out_hbm, idx_v, rows_v, sem):
        wid = _sc_worker()

        @pl.loop(0, -(-n_win // SC_WORKERS))
        def _(it):
            w = it * SC_WORKERS + wid

            @pl.when(w < n_win)
            def _():
                pltpu.sync_copy(dest_hbm.at[w], idx_v)
                for j in range(TOP_K):
                    pltpu.async_copy(y_hbm.at[idx_v.at[j]], rows_v, sem).wait()
                    pltpu.sync_copy(rows_v, out_hbm.at[j, pl.ds(w * SC_WINDOW, SC_WINDOW)])

    return k(y, dest3)


def _expert_ffn_kernel(be_ref, bv_ref, x_ref, wg_ref, wu_ref, wd_ref, o_ref, wg_s, wu_s, wd_s):
    b = pl.program_id(0)
    valid = bv_ref[b]
    new_expert = (b == 0) | (be_ref[b] != be_ref[jnp.maximum(b - 1, 0)])

    @pl.when(new_expert)
    def _():
        wg_s[...] = wg_ref[0].astype(BF16)
        wu_s[...] = wu_ref[0].astype(BF16)
        wd_s[...] = wd_ref[0].astype(BF16)

    @pl.when(valid > 0)
    def _():
        row = lax.broadcasted_iota(jnp.int32, (x_ref.shape[0], 1), 0)
        x = jnp.where(row < valid, x_ref[...], 0.0).astype(BF16)
        a = jnp.dot(x, wg_s[...], preferred_element_type=F32)
        a = a * jax.nn.sigmoid(a) * jnp.dot(x, wu_s[...], preferred_element_type=F32)
        o_ref[...] = jnp.dot(a.astype(BF16), wd_s[...], preferred_element_type=F32)


def expert_ffn(xg, block_expert, block_valid, wg, wu, wd):
    n_slots, d = xg.shape
    f = wg.shape[-1]
    grid_spec = pltpu.PrefetchScalarGridSpec(
        num_scalar_prefetch=2,
        grid=(n_slots // MOE_BLOCK,),
        in_specs=[pl.BlockSpec((MOE_BLOCK, d), lambda b, be, bv: (b, 0)),
                  pl.BlockSpec((1, d, f), lambda b, be, bv: (be[b], 0, 0)),
                  pl.BlockSpec((1, d, f), lambda b, be, bv: (be[b], 0, 0)),
                  pl.BlockSpec((1, f, d), lambda b, be, bv: (be[b], 0, 0))],
        out_specs=pl.BlockSpec((MOE_BLOCK, d), lambda b, be, bv: (b, 0)),
        scratch_shapes=[pltpu.VMEM((d, f), BF16), pltpu.VMEM((d, f), BF16), pltpu.VMEM((f, d), BF16)],
    )
    return pl.pallas_call(
        _expert_ffn_kernel,
        grid_spec=grid_spec,
        out_shape=jax.ShapeDtypeStruct((n_slots, d), F32),
        compiler_params=_params(("arbitrary",)),
        name="expert_ffn",
    )(block_expert, block_valid, xg, wg, wu, wd)


def _combine_kernel(yg_ref, w_ref, h_ref, swg_ref, swu_ref, swd_ref, o_ref):
    h = h_ref[...].astype(BF16)
    a = jnp.dot(h, swg_ref[...], preferred_element_type=F32)
    a = a * jax.nn.sigmoid(a) * jnp.dot(h, swu_ref[...], preferred_element_type=F32)
    acc = jnp.dot(a.astype(BF16), swd_ref[...], preferred_element_type=F32)
    wt = w_ref[...].T
    for k in range(TOP_K):
        acc = acc + wt[:, k:k + 1] * yg_ref[k]
    o_ref[...] = acc


def combine(yg, w, h, swg, swu, swd):
    n, d = h.shape
    f = swg.shape[-1]
    tm = ROW_TILE
    return pl.pallas_call(
        _combine_kernel,
        grid=(n // tm,),
        in_specs=[pl.BlockSpec((TOP_K, tm, d), lambda i: (0, i, 0)),
                  pl.BlockSpec((TOP_K, tm), lambda i: (0, i)),
                  pl.BlockSpec((tm, d), lambda i: (i, 0)),
                  pl.BlockSpec((d, f), lambda i: (0, 0)),
                  pl.BlockSpec((d, f), lambda i: (0, 0)),
                  pl.BlockSpec((f, d), lambda i: (0, 0))],
        out_specs=pl.BlockSpec((tm, d), lambda i: (i, 0)),
        out_shape=jax.ShapeDtypeStruct((n, d), F32),
        compiler_params=_params(("parallel",)),
        name="moe_combine",
    )(yg, w, h, swg, swu, swd)


def moe(h, logits_t, bias, wg, wu, wd, swg, swu, swd):
    n, d = h.shape
    eidx, w, rank, counts = route(logits_t, bias)
    counts = counts.reshape(N_EXPERTS).astype(jnp.int32)
    padded = (counts + MOE_BLOCK - 1) // MOE_BLOCK * MOE_BLOCK
    pad_end = jnp.cumsum(padded)
    pad_start = pad_end - padded
    n_slots = n * TOP_K + N_EXPERTS * MOE_BLOCK
    starts = jnp.arange(n_slots // MOE_BLOCK, dtype=jnp.int32) * MOE_BLOCK
    block_expert = jnp.minimum(jnp.searchsorted(pad_end, starts, side="right"), N_EXPERTS - 1).astype(jnp.int32)
    block_valid = jnp.clip(counts[block_expert] - (starts - pad_start[block_expert]), 0, MOE_BLOCK).astype(jnp.int32)
    dest = slot_index(pad_start.astype(jnp.int32), eidx, rank)
    dest3 = dest.reshape(TOP_K, n // SC_WINDOW, SC_WINDOW).transpose(1, 0, 2)
    xg = sc_dispatch(h, dest3, n_slots)
    y = expert_ffn(xg, block_expert, block_valid, wg, wu, wd)
    yg = sc_combine_gather(y, dest3)
    return combine(yg, w, h, swg.astype(BF16), swu.astype(BF16), swd.astype(BF16))


def mixer_ab(h, w_in, w_out, decay_logit, conv_w, conv_b, w1, b1, f1, w2, b2, f2, w3, skip, rope, dft, *, seq,
             ctx_len):
    n_rows = h.shape[0]
    tm = _token_tile(n_rows)
    p = mm([(h, w_in.astype(BF16))], F32, tm, 512, name="ab_in_proj")
    qkv_w = 2 * RET_QK + RET_V
    n_qk = 2 * RET_QK // LANE
    scales = (1.0,) * (RET_QK // LANE) + (RET_DK ** -0.5,) * (RET_QK // LANE) + (1.0,) * (RET_V // LANE)
    qkv = rope_cast(p, rope[0], rope[1], width=qkv_w, n_rot_blocks=n_qk, head_dim=RET_DK, scales=scales,
                    out_dtype=F32)
    log_g = jax.nn.log_sigmoid(decay_logit.astype(F32))
    ret = retention(qkv, p, log_g, jnp.exp(RET_CHUNK * log_g), seq=seq)
    v, x1, x2 = shortconv(p, conv_w, conv_b, seq=seq)
    filt = (w1, b1, f1, w2, b2, f2, w3)
    hy_x = long_conv_two_stage(dft["x"], hyena_filter_taps(seq, *filt), v[:seq], x1[:seq], x2[:seq], skip)
    hy_c = long_conv_one_stage(dft["c"], hyena_filter_taps(ctx_len, *filt), v[seq:], x1[seq:], x2[seq:], skip)
    hy = jnp.concatenate([hy_x, hy_c], axis=0)
    w_out = w_out.astype(BF16)
    return mm([(ret, w_out[:RET_V]), (hy, w_out[RET_V:])], F32, tm, 512, name="ab_out_proj")


def mixer_da(h, w_in, w_out, lam, subln, lambda_init, rope, *, seq, ctx_len):
    n_rows = h.shape[0]
    tm = _token_tile(n_rows)
    p = mm([(h, w_in.astype(BF16))], F32, tm, 512, name="da_in_proj")
    n_qk = 2 * DA_WIDTH // LANE
    scales = (LOG2_E * DA_HEAD_DIM ** -0.5,) * (DA_WIDTH // LANE) + (1.0,) * (2 * DA_WIDTH // LANE)
    qkv = rope_cast(p, rope[0], rope[1], width=3 * DA_WIDTH, n_rot_blocks=n_qk, head_dim=DA_HEAD_DIM, scales=scales,
                    out_dtype=BF16)
    lam_f = lam.astype(F32)
    lam_full = jnp.exp(jnp.sum(lam_f[0] * lam_f[1])) - jnp.exp(jnp.sum(lam_f[2] * lam_f[3])) + lambda_init
    o = diff_attention(qkv, lam_full, subln, seq=seq, ctx_len=ctx_len, lambda_init=lambda_init)
    return mm([(o, w_out.astype(BF16))], F32, tm, 512, name="da_out_proj")


def kernel(x, c, ctx, c_ctx, w_ada, b_ada, norm_mix, norm_ffn, ab_w_in, ab_w_out, ret_decay_logit, hy_conv_w, hy_conv_b, hy_w1, hy_b1, hy_freq1, hy_w2, hy_b2, hy_freq2, hy_w3, hy_skip, da_w_in, da_w_out, da_lambda, da_subln, router_w, router_b, exp_w_gate, exp_w_up, exp_w_down, sh_w_gate, sh_w_up, sh_w_down, norm_final):
    batch, seq, d = x.shape
    ctx_len = ctx.shape[1]
    assert batch == 1 and seq % ROW_TILE == 0 and ctx_len == ROW_TILE
    depth = w_ada.shape[0]
    n_rows = seq + ctx_len
    ctx_tile = seq // ROW_TILE

    xs = jnp.concatenate([x[0], ctx[0]], axis=0)
    cv = jnp.zeros((SUBLANE, d), F32).at[0].set(c_ctx).at[1].set(c[0])
    mods = adaln(cv, w_ada, b_ada)[:, :2].reshape(depth, 2, 6, d)

    rope_ret = rope_tables(seq, ctx_len, RET_DK)
    rope_da = rope_tables(seq, ctx_len, DA_HEAD_DIM)
    dft = dict(x=dft_tables_two_stage(2 * seq), c=dft_tables_one_stage(2 * ctx_len))
    common = dict(n_rows=n_rows, ctx_tile=ctx_tile)

    delta, gate_mods = None, None
    for i in range(depth):
        j = i // 2
        if delta is None:
            (h,) = norm_mod(xs, norm_mix[i], mods=mods[i], shift_idx=0, scale_idx=1, **common)
        else:
            xs, h = norm_mod(xs, norm_mix[i], delta=delta, gate_mods=gate_mods, gate_idx=5, mods=mods[i], shift_idx=0,
                             scale_idx=1, **common)
        if i % 2 == 0:
            y = mixer_ab(h, ab_w_in[j], ab_w_out[j], ret_decay_logit[j], hy_conv_w[j], hy_conv_b[j], hy_w1[j],
                         hy_b1[j], hy_freq1[j], hy_w2[j], hy_b2[j], hy_freq2[j], hy_w3[j], hy_skip[j], rope_ret, dft,
                         seq=seq, ctx_len=ctx_len)
        else:
            lambda_init = 0.8 - 0.6 * math.exp(-0.3 * i)
            y = mixer_da(h, da_w_in[j], da_w_out[j], da_lambda[j], da_subln[j], lambda_init, rope_da, seq=seq,
                         ctx_len=ctx_len)
        xs, h, logits_t = norm_mod(xs, norm_ffn[i], delta=y, gate_mods=mods[i], gate_idx=2, mods=mods[i], shift_idx=3,
                                   scale_idx=4, router_wt=router_w[i].T, out_dtype=F32, **common)
        delta = moe(h, logits_t, router_b[i], exp_w_gate[i], exp_w_up[i], exp_w_down[i],
                    sh_w_gate[i], sh_w_up[i], sh_w_down[i])
        gate_mods = mods[i]
    _, out = norm_mod(xs, norm_final, n_rows=seq, ctx_tile=ctx_tile, delta=delta, gate_mods=gate_mods, gate_idx=5,
                      out_dtype=F32)
    return out[None]
```

```python
import functools
import math

import jax
import jax.numpy as jnp
from jax import lax
from jax.experimental import pallas as pl
from jax.experimental.pallas import tpu as pltpu
from jax.experimental.pallas import tpu_sc as plsc

F32 = jnp.float32
BF16 = jnp.bfloat16
HIGHEST = lax.Precision.HIGHEST

D_MODEL = 1024
DEPTH = 4
GRID_W = 64
EPS = 1e-6
ROPE_BASE = 10000.0

RET_HEADS = 4
RET_DK = 128
RET_DV = 256
RET_CHUNK = 128
RET_QK = RET_HEADS * RET_DK
RET_V = RET_HEADS * RET_DV

HY_WIDTH = 512
HY_ORDER = 2
HY_BANDS = 16
HY_EMB = 2 * HY_BANDS + 1
HY_FFN = 64
HY_DECAY_TARGET = 1e-2
HY_FAST_DECAY = 0.3
HY_SLOW_DECAY = 1.5
HY_ZCOLS = 64
HY_VALID_COL = HY_EMB
FFT_N2 = 128

AB_IN = 2 * RET_QK + 2 * RET_V + (HY_ORDER + 1) * HY_WIDTH
AB_CAT = RET_V + HY_WIDTH

DA_HEADS = 8
DA_HEAD_DIM = 64
DA_WIDTH = DA_HEADS * 2 * DA_HEAD_DIM

N_EXPERTS = 64
TOP_K = 8
N_GROUPS = 8
TOPK_GROUPS = 4
GROUP_SIZE = N_EXPERTS // N_GROUPS
EXPERT_DIM = 256
ROUTED_SCALE = 2.5
MOE_BLOCK = 256
SC_CORES = 2
SC_SUBCORES = 16
SC_WORKERS = SC_CORES * SC_SUBCORES
SC_WINDOW = 64

LANE = 128
SUBLANE = 8
ROW_TILE = 256
MAX_TOKEN_TILE = 1280
VMEM_LIMIT = 48 * 1024 * 1024
FLASH_VMEM_LIMIT = 56 * 1024 * 1024
NEG_BIG = -1e30
LOG2_E = 1.4426950408889634
FLASH_LAZY_HEADROOM = 60.0


def _params(sem):
    return pltpu.CompilerParams(dimension_semantics=sem, vmem_limit_bytes=VMEM_LIMIT)


def _token_tile(n):
    best = ROW_TILE
    t = ROW_TILE
    while t <= min(n, MAX_TOKEN_TILE):
        if n % t == 0:
            best = t
        t += ROW_TILE
    return best


def _mm_kernel(*refs, n_pairs, has_epi):
    acc = None
    for p in range(n_pairs):
        a = refs[2 * p][...].astype(BF16)
        b = refs[2 * p + 1][...].astype(BF16)
        d = jnp.dot(a, b, preferred_element_type=F32)
        acc = d if acc is None else acc + d
    idx = 2 * n_pairs
    if has_epi:
        acc = refs[idx][...] * (acc + refs[idx + 1][...] * refs[idx + 2][...])
        idx += 3
    o_ref = refs[idx]
    o_ref[...] = acc.astype(o_ref.dtype)


def mm(pairs, out_dtype, tm, tn, epi=None, name="mm"):
    m = pairs[0][0].shape[0]
    n = pairs[0][1].shape[1]
    assert m % tm == 0 and n % tn == 0
    in_specs, args = [], []
    for a, b in pairs:
        k = a.shape[1]
        in_specs += [pl.BlockSpec((tm, k), lambda i, j: (i, 0)), pl.BlockSpec((k, tn), lambda i, j: (0, j))]
        args += [a, b]
    if epi is not None:
        in_specs += [pl.BlockSpec((tm, tn), lambda i, j: (i, j)), pl.BlockSpec((1, tn), lambda i, j: (0, j)),
                     pl.BlockSpec((tm, tn), lambda i, j: (i, j))]
        args += list(epi)
    return pl.pallas_call(
        functools.partial(_mm_kernel, n_pairs=len(pairs), has_epi=epi is not None),
        grid=(m // tm, n // tn),
        in_specs=in_specs,
        out_specs=pl.BlockSpec((tm, tn), lambda i, j: (i, j)),
        out_shape=jax.ShapeDtypeStruct((m, n), out_dtype),
        compiler_params=_params(("parallel", "parallel")),
        name=name,
    )(*args)


def _adaln_kernel(cv_ref, w_ref, b_ref, o_ref):
    cv = cv_ref[...]
    s = cv * jax.nn.sigmoid(cv)
    o_ref[0] = jnp.dot(s, w_ref[0], precision=HIGHEST, preferred_element_type=F32) + b_ref[0]


def adaln(cv, w_ada, b_ada):
    depth, d, n = w_ada.shape
    tn = 1536
    return pl.pallas_call(
        _adaln_kernel,
        grid=(depth, n // tn),
        in_specs=[pl.BlockSpec((SUBLANE, d), lambda l, j: (0, 0)),
                  pl.BlockSpec((1, d, tn), lambda l, j: (l, 0, j)),
                  pl.BlockSpec((1, 1, tn), lambda l, j: (l, 0, j))],
        out_specs=pl.BlockSpec((1, SUBLANE, tn), lambda l, j: (l, 0, j)),
        out_shape=jax.ShapeDtypeStruct((depth, SUBLANE, n), F32),
        compiler_params=_params(("parallel", "parallel")),
        name="adaln",
    )(cv, w_ada, b_ada.reshape(depth, 1, n))


def _norm_mod_kernel(*refs, has_delta, gate_idx, shift_idx, scale_idx, has_router, write_xs):
    it = iter(refs)
    xs_ref = next(it)
    x = xs_ref[...]
    if has_delta:
        delta_ref = next(it)
        gmods_ref = next(it)
        x = x + gmods_ref[0, gate_idx:gate_idx + 1, :] * delta_ref[...]
    mods_ref = next(it) if shift_idx is not None else None
    g_ref = next(it)
    wr_ref = next(it) if has_router else None
    if write_xs:
        next(it)[...] = x
    h_ref = next(it)
    y = x * lax.rsqrt(jnp.mean(x * x, axis=-1, keepdims=True) + EPS) * g_ref[...]
    if shift_idx is not None:
        y = y * (1.0 + mods_ref[0, scale_idx:scale_idx + 1, :]) + mods_ref[0, shift_idx:shift_idx + 1, :]
    h_ref[...] = y.astype(h_ref.dtype)
    if has_router:
        lg_ref = next(it)
        lg_ref[...] = lax.dot_general(wr_ref[...], y, (((1,), (1,)), ((), ())),
                                      precision=HIGHEST, preferred_element_type=F32)
        half = y.shape[1] // 2
        next(it)[...] = _pack_bf16_pair(y[:, :half], y[:, half:])


def _pack_bf16_pair(a, b):
    def rounded(x):
        u = lax.bitcast_convert_type(x, jnp.int32)
        return u + 0x7FFF + (lax.shift_right_logical(u, 16) & 1)
    return lax.shift_right_logical(rounded(a), 16) | (rounded(b) & -65536)


def _unpack_bf16_pair(w):
    return (lax.bitcast_convert_type(lax.shift_left(w, 16), F32),
            lax.bitcast_convert_type(w & -65536, F32))


def norm_mod(xs, g, *, n_rows, ctx_tile, delta=None, gate_mods=None, gate_idx=None, mods=None, shift_idx=None,
             scale_idx=None, router_wt=None, out_dtype=BF16):
    d = xs.shape[1]
    n_tiles = n_rows // ROW_TILE
    row = pl.BlockSpec((ROW_TILE, d), lambda i: (i, 0))
    mod_spec = pl.BlockSpec((1, 6, d), lambda i: (jnp.where(i == ctx_tile, 0, 1), 0, 0))
    in_specs, args = [row], [xs]
    has_delta = delta is not None
    if has_delta:
        in_specs += [row, mod_spec]
        args += [delta, gate_mods]
    if shift_idx is not None:
        in_specs.append(mod_spec)
        args.append(mods)
    in_specs.append(pl.BlockSpec((1, d), lambda i: (0, 0)))
    args.append(g.reshape(1, d))
    has_router = router_wt is not None
    if has_router:
        in_specs.append(pl.BlockSpec(router_wt.shape, lambda i: (0, 0)))
        args.append(router_wt)
    out_specs, out_shape = [], []
    if has_delta:
        out_specs.append(row)
        out_shape.append(jax.ShapeDtypeStruct((n_rows, d), F32))
    out_specs.append(row)
    out_shape.append(jax.ShapeDtypeStruct((n_rows, d), out_dtype))
    if has_router:
        out_specs.append(pl.BlockSpec((N_EXPERTS, ROW_TILE), lambda i: (0, i)))
        out_shape.append(jax.ShapeDtypeStruct((N_EXPERTS, n_rows), F32))
        out_specs.append(pl.BlockSpec((ROW_TILE, d // 2), lambda i: (i, 0)))
        out_shape.append(jax.ShapeDtypeStruct((n_rows, d // 2), jnp.int32))
    return pl.pallas_call(
        functools.partial(_norm_mod_kernel, has_delta=has_delta, gate_idx=gate_idx, shift_idx=shift_idx,
                          scale_idx=scale_idx, has_router=has_router, write_xs=has_delta),
        grid=(n_tiles,),
        in_specs=in_specs,
        out_specs=out_specs,
        out_shape=out_shape,
        compiler_params=_params(("parallel",)),
        name="norm_mod",
    )(*args)


def _rope_kernel(p_ref, cos_ref, sin_ref, o_ref, *, n_rot_blocks, head_dim, scales):
    cos = cos_ref[...]
    sin = sin_ref[...]
    for b in range(len(scales)):
        x = p_ref[:, b * LANE:(b + 1) * LANE]
        if b < n_rot_blocks:
            if head_dim == LANE:
                rot = pltpu.roll(x, LANE // 2, 1)
            else:
                lane = lax.broadcasted_iota(jnp.int32, x.shape, 1)
                first_half = (lane % head_dim) < head_dim // 2
                rot = jnp.where(first_half, pltpu.roll(x, LANE - head_dim // 2, 1), pltpu.roll(x, head_dim // 2, 1))
            x = x * cos + rot * sin
        if scales[b] != 1.0:
            x = x * scales[b]
        o_ref[:, b * LANE:(b + 1) * LANE] = x.astype(o_ref.dtype)


def rope_cast(p, cos, sin, *, width, n_rot_blocks, head_dim, scales, out_dtype):
    n_rows = p.shape[0]
    return pl.pallas_call(
        functools.partial(_rope_kernel, n_rot_blocks=n_rot_blocks, head_dim=head_dim, scales=scales),
        grid=(n_rows // ROW_TILE,),
        in_specs=[pl.BlockSpec((ROW_TILE, width), lambda i: (i, 0)),
                  pl.BlockSpec((ROW_TILE, LANE), lambda i: (i, 0)),
                  pl.BlockSpec((ROW_TILE, LANE), lambda i: (i, 0))],
        out_specs=pl.BlockSpec((ROW_TILE, width), lambda i: (i, 0)),
        out_shape=jax.ShapeDtypeStruct((n_rows, width), out_dtype),
        compiler_params=_params(("parallel",)),
        name="rope_cast",
    )(p, cos, sin)


def rope_tables(seq, ctx_len, head_dim):
    n_freq = head_dim // 4
    inv = ROPE_BASE ** (-jnp.arange(n_freq, dtype=F32) / n_freq)
    rows = seq // GRID_W
    row = jnp.repeat(jnp.arange(rows, dtype=F32), GRID_W)
    col = jnp.tile(jnp.arange(GRID_W, dtype=F32), rows)
    ang = jnp.concatenate([row[:, None] * inv, col[:, None] * inv], axis=-1)
    cos, sin = jnp.cos(ang), jnp.sin(ang)
    cos = jnp.concatenate([cos, cos], axis=-1)
    sin = jnp.concatenate([-sin, sin], axis=-1)
    reps = LANE // head_dim
    cos, sin = jnp.tile(cos, (1, reps)), jnp.tile(sin, (1, reps))
    cos = jnp.concatenate([cos, jnp.ones((ctx_len, LANE), F32)], axis=0)
    sin = jnp.concatenate([sin, jnp.zeros((ctx_len, LANE), F32)], axis=0)
    return cos, sin


def _ret_kernel(lg_ref, gc_ref, q_ref, k_ref, v_ref, *rest, reverse):
    if reverse:
        yf_ref, gate_ref, o_ref, s_ref = rest
    else:
        o_ref, s_ref = rest
    c = RET_CHUNK

    @pl.when(pl.program_id(0) == 0)
    def _():
        s_ref[...] = jnp.zeros_like(s_ref)

    ii = lax.broadcasted_iota(jnp.int32, (c, c), 0)
    jj = lax.broadcasted_iota(jnp.int32, (c, c), 1)
    rel = ((jj - ii) if reverse else (ii - jj)).astype(F32)
    pos = lax.broadcasted_iota(jnp.int32, (c, 1), 0).astype(F32)
    for h in range(RET_HEADS):
        lg = lg_ref[h]
        dec = jnp.where(rel >= 0, jnp.exp(jnp.maximum(rel, 0.0) * lg), 0.0)
        if reverse:
            q_dec = jnp.exp((c - pos) * lg)
            k_dec = jnp.exp(pos * lg)
        else:
            q_dec = jnp.exp((pos + 1.0) * lg)
            k_dec = jnp.exp((c - 1.0 - pos) * lg)
        q = q_ref[:, h * RET_DK:(h + 1) * RET_DK]
        k = k_ref[:, h * RET_DK:(h + 1) * RET_DK]
        v = v_ref[:, h * RET_DV:(h + 1) * RET_DV].astype(BF16)
        s = lax.dot_general(q.astype(BF16), k.astype(BF16), (((1,), (1,)), ((), ())),
                            preferred_element_type=F32) * dec
        state = s_ref[h]
        y = jnp.dot(s.astype(BF16), v, preferred_element_type=F32)
        y = y + jnp.dot((q * q_dec).astype(BF16), state.astype(BF16), preferred_element_type=F32)
        upd = lax.dot_general((k * k_dec).astype(BF16), v, (((0,), (0,)), ((), ())), preferred_element_type=F32)
        s_ref[h] = gc_ref[h] * state + upd
        if reverse:
            r = y + yf_ref[:, h * RET_DV:(h + 1) * RET_DV]
            mu = jnp.mean(r, axis=-1, keepdims=True)
            rc = r - mu
            var = jnp.mean(rc * rc, axis=-1, keepdims=True)
            g = gate_ref[:, h * RET_DV:(h + 1) * RET_DV]
            o_ref[:, h * RET_DV:(h + 1) * RET_DV] = (rc * lax.rsqrt(var + EPS) * (g * jax.nn.sigmoid(g))).astype(
                o_ref.dtype)
        else:
            o_ref[:, h * RET_DV:(h + 1) * RET_DV] = y


def retention(qkv, p, log_g, g_chunk, *, seq):
    n_rows = qkv.shape[0]
    n_chunks = n_rows // RET_CHUNK
    n_x = seq // RET_CHUNK
    smem = pl.BlockSpec(memory_space=pltpu.SMEM)

    def run(reverse, extra):
        if reverse:
            idx = lambda t: n_chunks - 1 - t
        else:
            idx = lambda t: (t + n_x) % n_chunks
        in_specs = [smem, smem,
                    pl.BlockSpec((RET_CHUNK, RET_QK), lambda t: (idx(t), 0)),
                    pl.BlockSpec((RET_CHUNK, RET_QK), lambda t: (idx(t), 1)),
                    pl.BlockSpec((RET_CHUNK, RET_V), lambda t: (idx(t), 1))]
        args = [log_g[1 if reverse else 0], g_chunk[1 if reverse else 0], qkv, qkv, qkv]
        if reverse:
            in_specs += [pl.BlockSpec((RET_CHUNK, RET_V), lambda t: (idx(t), 0)),
                         pl.BlockSpec((RET_CHUNK, RET_V), lambda t: (idx(t), 2))]
            args += list(extra)
        return pl.pallas_call(
            functools.partial(_ret_kernel, reverse=reverse),
            grid=(n_chunks,),
            in_specs=in_specs,
            out_specs=pl.BlockSpec((RET_CHUNK, RET_V), lambda t: (idx(t), 0)),
            out_shape=jax.ShapeDtypeStruct((n_rows, RET_V), BF16 if reverse else F32),
            scratch_shapes=[pltpu.VMEM((RET_HEADS, RET_DK, RET_DV), F32)],
            compiler_params=_params(("arbitrary",)),
            name="retention_bwd" if reverse else "retention_fwd",
        )(*args)

    y_fwd = run(False, None)
    return run(True, (y_fwd, p))


def _shortconv_kernel(cur_ref, prev_ref, next_ref, w_ref, b_ref, v_ref, x1_ref, x2_ref, *, x_tiles):
    i = pl.program_id(0)
    cur = cur_ref[...]
    rows = cur.shape[0]
    row = lax.broadcasted_iota(jnp.int32, (rows, 1), 0)
    has_prev = jnp.where((i == 0) | (i == x_tiles), 0.0, 1.0)
    has_next = jnp.where((i == x_tiles - 1) | (i == x_tiles), 0.0, 1.0)
    up = jnp.where(row == 0, prev_ref[SUBLANE - 1:SUBLANE, :] * has_prev, pltpu.roll(cur, 1, 0))
    dn = jnp.where(row == rows - 1, next_ref[0:1, :] * has_next, pltpu.roll(cur, rows - 1, 0))
    y = up * w_ref[0:1, :] + cur * w_ref[1:2, :] + dn * w_ref[2:3, :] + b_ref[...]
    v_ref[...] = y[:, :HY_WIDTH]
    x1_ref[...] = y[:, HY_WIDTH:2 * HY_WIDTH]
    x2_ref[...] = y[:, 2 * HY_WIDTH:]


def shortconv(p, w, b, *, seq):
    n_rows = p.shape[0]
    width = 3 * HY_WIDTH
    col = p.shape[1] // width - 1
    per = ROW_TILE // SUBLANE
    last = n_rows // SUBLANE - 1
    out = jax.ShapeDtypeStruct((n_rows, HY_WIDTH), F32)
    ospec = pl.BlockSpec((ROW_TILE, HY_WIDTH), lambda i: (i, 0))
    return pl.pallas_call(
        functools.partial(_shortconv_kernel, x_tiles=seq // ROW_TILE),
        grid=(n_rows // ROW_TILE,),
        in_specs=[pl.BlockSpec((ROW_TILE, width), lambda i: (i, col)),
                  pl.BlockSpec((SUBLANE, width), lambda i: (jnp.maximum(i * per - 1, 0), col)),
                  pl.BlockSpec((SUBLANE, width), lambda i: (jnp.minimum((i + 1) * per, last), col)),
                  pl.BlockSpec((3, width), lambda i: (0, 0)),
                  pl.BlockSpec((1, width), lambda i: (0, 0))],
        out_specs=[ospec, ospec, ospec],
        out_shape=[out, out, out],
        compiler_params=_params(("parallel",)),
        name="shortconv",
    )(p, p, p, w, b.reshape(1, width))


def _filt_kernel(z_ref, w1_ref, b1_ref, f1_ref, w2_ref, b2_ref, f2_ref, w3a_ref, w3b_ref, dl_ref, o_ref):
    z = z_ref[...]
    h = jnp.sin(f1_ref[...] * (jnp.dot(z, w1_ref[...], precision=HIGHEST, preferred_element_type=F32) + b1_ref[...]))
    h = jnp.sin(f2_ref[...] * (jnp.dot(h, w2_ref[...], precision=HIGHEST, preferred_element_type=F32) + b2_ref[...]))
    window = jnp.exp(-z[:, 0:1] * dl_ref[...]) * z[:, HY_VALID_COL:HY_VALID_COL + 1]
    for o, w3_ref in enumerate((w3a_ref, w3b_ref)):
        o_ref[o] = jnp.dot(h, w3_ref[...], precision=HIGHEST, preferred_element_type=F32) * window


def hyena_filter_taps(length, w1, b1, f1, w2, b2, f2, w3):
    z = _filter_positions(length)
    w1p = jnp.zeros((HY_ZCOLS, HY_FFN), F32).at[:HY_EMB].set(w1)
    deltas = jnp.abs(jnp.linspace(math.log(HY_DECAY_TARGET) / HY_SLOW_DECAY,
                                  math.log(HY_DECAY_TARGET) / HY_FAST_DECAY, HY_WIDTH, dtype=F32)).reshape(1, HY_WIDTH)
    tm = min(length, 512)
    half_tiles = length // tm
    vec = lambda a: a.reshape(1, HY_FFN)
    small = lambda shape: pl.BlockSpec(shape, lambda i: (0, 0))
    w3_spec = lambda o: pl.BlockSpec((HY_FFN, HY_WIDTH), lambda i: (0, 2 * o + jnp.where(i >= half_tiles, 1, 0)))
    assert HY_ORDER == 2
    return pl.pallas_call(
        _filt_kernel,
        grid=(2 * half_tiles,),
        in_specs=[pl.BlockSpec((tm, HY_ZCOLS), lambda i: (i, 0)),
                  small((HY_ZCOLS, HY_FFN)), small((1, HY_FFN)), small((1, HY_FFN)),
                  small((HY_FFN, HY_FFN)), small((1, HY_FFN)), small((1, HY_FFN)),
                  w3_spec(0), w3_spec(1), small((1, HY_WIDTH))],
        out_specs=pl.BlockSpec((HY_ORDER, tm, HY_WIDTH), lambda i: (0, i, 0)),
        out_shape=jax.ShapeDtypeStruct((HY_ORDER, 2 * length, HY_WIDTH), F32),
        compiler_params=_params(("parallel",)),
        name="hyena_filter",
    )(z, w1p, vec(b1), vec(f1), w2, vec(b2), vec(f2), w3, w3, deltas)


def _filter_positions(length):
    t = jnp.concatenate([jnp.arange(length, dtype=F32), float(length) - jnp.arange(length, dtype=F32)])
    valid = jnp.ones((2 * length,), F32).at[length].set(0.0)
    t_norm = t / max(length - 1, 1)
    bands = jnp.linspace(1e-4, HY_BANDS - 1, HY_BANDS, dtype=F32)
    ang = (2.0 * math.pi / length) * t[:, None] * bands[None, :]
    z = jnp.concatenate([t_norm[:, None], jnp.cos(ang), -jnp.sin(ang), valid[:, None]], axis=-1)
    return jnp.pad(z, ((0, 0), (0, HY_ZCOLS - z.shape[1])))


def _angles(num, den):
    return (2.0 * math.pi / den) * (num % den).astype(F32)


def dft_tables_two_stage(m):
    n2 = FFT_N2
    n1 = m // n2
    half = n1 // 2
    kp = -(-(half + 1) // SUBLANE) * SUBLANE
    k1 = jnp.arange(kp, dtype=jnp.int32)
    live = (k1 <= half)
    a1 = _angles(k1[:, None] * jnp.arange(n1, dtype=jnp.int32)[None, :], n1)
    f1 = jnp.concatenate([jnp.where(live[:, None], jnp.cos(a1), 0.0), jnp.where(live[:, None], -jnp.sin(a1), 0.0)], 0)
    wgt = jnp.where((k1 == 0) | (k1 == half), 1.0, 2.0) * live / m
    a1h = a1[:, :half].T
    cinv = jnp.concatenate([jnp.cos(a1h) * wgt[None, :], -jnp.sin(a1h) * wgt[None, :]], axis=1)
    k = k1[:, None, None] + n1 * jnp.arange(n2, dtype=jnp.int32)[None, :, None]
    th = _angles(k * jnp.arange(n2, dtype=jnp.int32)[None, None, :], m)
    c = jnp.where(live[:, None, None], jnp.cos(th), 0.0)
    s = jnp.where(live[:, None, None], jnp.sin(th), 0.0)
    g_fwd = jnp.concatenate([jnp.concatenate([c, s], 2), jnp.concatenate([-s, c], 2)], 1)
    ct, st = jnp.swapaxes(c, 1, 2), jnp.swapaxes(s, 1, 2)
    g_inv = jnp.concatenate([jnp.concatenate([ct, -st], 2), jnp.concatenate([st, ct], 2)], 1)
    return dict(n1=n1, kp=kp, f1=f1.astype(BF16), f1_half=f1[:, :half].astype(BF16), cinv=cinv.astype(BF16),
                g_fwd=g_fwd.astype(BF16), g_inv=g_inv.astype(BF16))


def dft_tables_one_stage(m):
    half = m // 2
    kp = -(-(half + 1) // SUBLANE) * SUBLANE
    k = jnp.arange(kp, dtype=jnp.int32)
    live = (k <= half)
    a = _angles(k[:, None] * jnp.arange(m, dtype=jnp.int32)[None, :], m)
    f = jnp.concatenate([jnp.where(live[:, None], jnp.cos(a), 0.0), jnp.where(live[:, None], -jnp.sin(a), 0.0)], 0)
    wgt = jnp.where((k == 0) | (k == half), 1.0, 2.0) * live / m
    ah = a[:, :half].T
    cinv = jnp.concatenate([jnp.cos(ah) * wgt[None, :], -jnp.sin(ah) * wgt[None, :]], axis=1)
    return dict(kp=kp, f=f.astype(BF16), f_half=f[:, :half].astype(BF16), cinv=cinv.astype(BF16))


def _bmm_kernel(*refs, kb, in_part_major, out_part_major, has_h):
    if has_h:
        g_ref, a_ref, h_ref, o_ref = refs
    else:
        g_ref, a_ref, o_ref = refs
    n2 = FFT_N2
    for b in range(kb):
        if in_part_major:
            ar, ai = a_ref[0, b], a_ref[1, b]
        else:
            ar, ai = a_ref[b, 0], a_ref[b, 1]
        if has_h:
            hr, hi = h_ref[b, 0], h_ref[b, 1]
            ar, ai = ar * hr - ai * hi, ar * hi + ai * hr
        xin = jnp.concatenate([ar, ai], axis=0).astype(BF16)
        y = jnp.dot(g_ref[b], xin, preferred_element_type=F32)
        if out_part_major:
            o_ref[0, b] = y[:n2]
            o_ref[1, b] = y[n2:]
        else:
            o_ref[b, 0] = y[:n2]
            o_ref[b, 1] = y[n2:]


def bmm_k1(g, a, h=None, *, in_part_major, out_part_major):
    kp = g.shape[0]
    n2 = FFT_N2
    c = a.shape[-1]
    kb, tc = SUBLANE, 256
    pm = lambda: pl.BlockSpec((2, kb, n2, tc), lambda i, j: (0, i, 0, j))
    km = lambda: pl.BlockSpec((kb, 2, n2, tc), lambda i, j: (i, 0, 0, j))
    in_specs = [pl.BlockSpec((kb, 2 * n2, 2 * n2), lambda i, j: (i, 0, 0)), pm() if in_part_major else km()]
    args = [g, a]
    if h is not None:
        in_specs.append(km())
        args.append(h)
    return pl.pallas_call(
        functools.partial(_bmm_kernel, kb=kb, in_part_major=in_part_major, out_part_major=out_part_major,
                          has_h=h is not None),
        grid=(kp // kb, c // tc),
        in_specs=in_specs,
        out_specs=pm() if out_part_major else km(),
        out_shape=jax.ShapeDtypeStruct((2, kp, n2, c) if out_part_major else (kp, 2, n2, c), F32),
        compiler_params=_params(("parallel", "parallel")),
        name="dft_inner",
    )(*args)


def _cmul_kernel(x_ref, h_ref, o_ref):
    xr, xi, hr, hi = x_ref[0], x_ref[1], h_ref[0], h_ref[1]
    o_ref[0] = xr * hr - xi * hi
    o_ref[1] = xr * hi + xi * hr


def cmul(x, h):
    spec = pl.BlockSpec(x.shape, lambda i: (0, 0, 0))
    return pl.pallas_call(_cmul_kernel, grid=(1,), in_specs=[spec, spec], out_specs=spec,
                          out_shape=jax.ShapeDtypeStruct(x.shape, F32), compiler_params=_params(("arbitrary",)),
                          name="spectrum_product")(x, h)


def long_conv_two_stage(tabs, taps, v, x1, x2, skip):
    length, c = v.shape
    n2, n1, kp = FFT_N2, tabs["n1"], tabs["kp"]
    cols = n2 * c
    tn = 2048

    def fwd(seq2d, full):
        a = mm([(tabs["f1"] if full else tabs["f1_half"], seq2d)], F32, 2 * kp, tn, name="dft_outer")
        return a.reshape(2, kp, n2, c)

    spectra = [bmm_k1(tabs["g_fwd"], fwd(taps[o].reshape(n1, cols), True), in_part_major=True, out_part_major=False)
               for o in range(HY_ORDER)]
    u = v
    for o, gate in enumerate((x1, x2)):
        xf = bmm_k1(tabs["g_fwd"], fwd(u.reshape(n1 // 2, cols), False), in_part_major=True, out_part_major=False)
        bt = bmm_k1(tabs["g_inv"], xf, spectra[o], in_part_major=False, out_part_major=True)
        skip_row = jnp.tile(skip[o].reshape(1, c), (1, n2))
        u = mm([(tabs["cinv"], bt.reshape(2 * kp, cols))], F32, n1 // 2, tn,
               epi=(gate.reshape(n1 // 2, cols), skip_row, u.reshape(n1 // 2, cols)), name="idft_outer_gate")
        u = u.reshape(length, c)
    return u


def long_conv_one_stage(tabs, taps, v, x1, x2, skip):
    length, c = v.shape
    kp = tabs["kp"]
    u = v
    for o, gate in enumerate((x1, x2)):
        hs = mm([(tabs["f"], taps[o])], F32, 2 * kp, c, name="ctx_dft").reshape(2, kp, c)
        xs = mm([(tabs["f_half"], u)], F32, 2 * kp, c, name="ctx_dft").reshape(2, kp, c)
        ys = cmul(xs, hs).reshape(2 * kp, c)
        u = mm([(tabs["cinv"], ys)], F32, length, c, epi=(gate, skip[o].reshape(1, c), u), name="ctx_idft_gate")
    return u


def _flash_kernel(lam_ref, qt_ref, k_ref, vt_ref, sub_ref, o_ref, m_ref, acc_ref, *, kv, seq, ctx_len, out_scale):
    i = pl.program_id(1)
    last_q = pl.num_programs(1) - 1
    tq = qt_ref.shape[1]
    d = DA_HEAD_DIM
    dv = 2 * DA_HEAD_DIM
    n_chunks = k_ref.shape[0] // kv
    m_ref[...] = jnp.full_like(m_ref, NEG_BIG)
    acc_ref[...] = jnp.zeros_like(acc_ref)

    def scores(kc, c, masked):
        off = kc * kv if isinstance(kc, int) else pl.multiple_of(kc * kv, kv)
        s = jnp.dot(k_ref[pl.ds(off, kv), c * d:(c + 1) * d], qt_ref[c * d:(c + 1) * d, :],
                    preferred_element_type=F32)
        if masked:
            key = off + lax.broadcasted_iota(jnp.int32, (kv, 1), 0)
            lane = lax.broadcasted_iota(jnp.int32, (1, tq), 1)
            s = s + jnp.where(key < seq, NEG_BIG, 0.0) * jnp.where(lane >= tq - ctx_len, 1.0, 0.0)
        return s, vt_ref[:, pl.ds(off, kv)]

    def exact_step(kc, c, masked):
        s, vt = scores(kc, c, masked)
        m_old = m_ref[c]
        m_new = jnp.maximum(m_old, jnp.max(s, axis=0, keepdims=True))
        pr = jnp.exp2(s - m_new).astype(BF16)
        acc_ref[c] = jnp.exp2(m_old - m_new) * acc_ref[c] + jnp.dot(vt, pr, preferred_element_type=F32)
        m_ref[c] = m_new

    def lazy_step(kc, c):
        s, vt = scores(kc, c, False)
        m_old = m_ref[c]
        m_chunk = jnp.max(s, axis=0, keepdims=True)
        pv = jnp.dot(vt, jnp.exp2(s - m_old).astype(BF16), preferred_element_type=F32)
        safe = jnp.max(m_chunk - m_old) <= FLASH_LAZY_HEADROOM

        @pl.when(safe)
        def _():
            m_new = jnp.maximum(m_old, m_chunk)
            acc_ref[c] = jnp.exp2(m_old - m_new) * (acc_ref[c] + pv)
            m_ref[c] = m_new

        @pl.when(jnp.logical_not(safe))
        def _():
            exact_step(kc, c, False)

    @pl.when(i != last_q)
    def _():
        for c in range(2):
            exact_step(0, c, False)

        def body(kc, carry):
            for c in range(2):
                lazy_step(kc, c)
            return carry

        lax.fori_loop(1, n_chunks, body, 0)

    @pl.when(i == last_q)
    def _():
        def body(kc, carry):
            for c in range(2):
                exact_step(kc, c, True)
            return carry

        lax.fori_loop(0, n_chunks, body, 0)

    a0 = acc_ref[0, :dv, :] / acc_ref[0, dv:dv + 1, :]
    a1 = acc_ref[1, :dv, :] / acc_ref[1, dv:dv + 1, :]
    o = (a0 - lam_ref[0] * a1).T
    o = o * lax.rsqrt(jnp.mean(o * o, axis=-1, keepdims=True) + 1e-5) * sub_ref[...]
    o_ref[...] = (o * out_scale).astype(o_ref.dtype)


def diff_attention(qkv, lam_full, subln, *, seq, ctx_len, lambda_init):
    n_rows = qkv.shape[0]
    tq = _token_tile(n_rows)
    hw = 2 * DA_HEAD_DIM
    ones_rows = 2 * SUBLANE
    qt = qkv[:, :DA_WIDTH].T
    vt = qkv[:, 2 * DA_WIDTH:].T.reshape(DA_HEADS, hw, n_rows)
    vt = jnp.concatenate([vt, jnp.ones((DA_HEADS, ones_rows, n_rows), BF16)], axis=1)
    vt = vt.reshape(DA_HEADS * (hw + ones_rows), n_rows)
    return pl.pallas_call(
        functools.partial(_flash_kernel, kv=tq, seq=seq, ctx_len=ctx_len, out_scale=1.0 - lambda_init),
        grid=(DA_HEADS, n_rows // tq),
        in_specs=[pl.BlockSpec(memory_space=pltpu.SMEM),
                  pl.BlockSpec((hw, tq), lambda h, i: (h, i)),
                  pl.BlockSpec((n_rows, hw), lambda h, i: (0, DA_HEADS + h)),
                  pl.BlockSpec((hw + ones_rows, n_rows), lambda h, i: (h, 0)),
                  pl.BlockSpec((1, hw), lambda h, i: (0, 0))],
        out_specs=pl.BlockSpec((tq, hw), lambda h, i: (i, h)),
        out_shape=jax.ShapeDtypeStruct((n_rows, DA_WIDTH), BF16),
        scratch_shapes=[pltpu.VMEM((2, 1, tq), F32), pltpu.VMEM((2, hw + ones_rows, tq), F32)],
        compiler_params=pltpu.CompilerParams(dimension_semantics=("parallel", "parallel"),
                                             vmem_limit_bytes=FLASH_VMEM_LIMIT),
        name="diff_attention",
    )(lam_full.reshape(1), qt, qkv, vt, subln.reshape(1, hw))


def _route_kernel(lg_ref, b_ref, tri_ref, eidx_ref, w_ref, rank_ref, cnt_ref, carry_ref):
    t = lg_ref.shape[1]

    @pl.when(pl.program_id(0) == 0)
    def _():
        carry_ref[...] = jnp.zeros_like(carry_ref)

    scores = jax.nn.sigmoid(lg_ref[...])
    choice = (scores + b_ref[...]).reshape(N_GROUPS, GROUP_SIZE, t)
    s3 = scores.reshape(N_GROUPS, GROUP_SIZE, t)
    member = lax.broadcasted_iota(jnp.int32, choice.shape, 1)
    group = lax.broadcasted_iota(jnp.int32, (N_GROUPS, 1, t), 0)
    expert = lax.broadcasted_iota(jnp.int32, choice.shape, 0) * GROUP_SIZE + member
    neg_inf = -jnp.inf
    m1 = jnp.max(choice, axis=1, keepdims=True)
    first = jnp.min(jnp.where(choice == m1, member, GROUP_SIZE), axis=1, keepdims=True)
    m2 = jnp.max(jnp.where(member == first, neg_inf, choice), axis=1, keepdims=True)
    gscore = m1 + m2
    gsel = jnp.zeros(gscore.shape, F32)
    for _ in range(TOPK_GROUPS):
        m = jnp.max(gscore, axis=0, keepdims=True)
        f = jnp.min(jnp.where(gscore == m, group, N_GROUPS), axis=0, keepdims=True)
        hit = group == f
        gsel = jnp.where(hit, 1.0, gsel)
        gscore = jnp.where(hit, neg_inf, gscore)
    cand = jnp.where(gsel > 0.0, choice, neg_inf)
    esel = jnp.zeros(choice.shape, F32)
    picks = []
    for _ in range(TOP_K):
        m = jnp.max(jnp.max(cand, axis=1, keepdims=True), axis=0, keepdims=True)
        f = jnp.min(jnp.min(jnp.where(cand == m, expert, N_EXPERTS), axis=1, keepdims=True), axis=0, keepdims=True)
        hit = expert == f
        esel = jnp.where(hit, 1.0, esel)
        cand = jnp.where(hit, neg_inf, cand)
        picks.append(f)
    w = s3 * esel
    denom = jnp.sum(jnp.sum(w, axis=1, keepdims=True), axis=0, keepdims=True) + 1e-20
    w = w / denom * ROUTED_SCALE
    sel = esel.reshape(N_EXPERTS, t)
    before = jnp.dot(sel.astype(BF16), tri_ref[...], preferred_element_type=F32) + carry_ref[...]
    before = before.reshape(N_GROUPS, GROUP_SIZE, t)
    pick = lambda a, hit: jnp.sum(jnp.sum(jnp.where(hit, a, 0.0), axis=1, keepdims=True), axis=0).reshape(1, t)
    for k, f in enumerate(picks):
        hit = expert == f
        eidx_ref[k:k + 1, :] = f.reshape(1, t)
        w_ref[k:k + 1, :] = pick(w, hit)
        rank_ref[k:k + 1, :] = pick(before, hit).astype(jnp.int32)
    carry_ref[...] += jnp.sum(sel, axis=1, keepdims=True)
    cnt_ref[...] = carry_ref[...]


def route(logits_t, bias):
    n = logits_t.shape[1]
    t = _token_tile(n)
    tri = (jnp.arange(t)[:, None] < jnp.arange(t)[None, :]).astype(BF16)
    tok = lambda dt: jax.ShapeDtypeStruct((TOP_K, n), dt)
    tok_spec = pl.BlockSpec((TOP_K, t), lambda i: (0, i))
    return pl.pallas_call(
        _route_kernel,
        grid=(n // t,),
        in_specs=[pl.BlockSpec((N_EXPERTS, t), lambda i: (0, i)), pl.BlockSpec((N_EXPERTS, 1), lambda i: (0, 0)),
                  pl.BlockSpec((t, t), lambda i: (0, 0))],
        out_specs=[tok_spec, tok_spec, tok_spec, pl.BlockSpec((N_EXPERTS, 1), lambda i: (0, 0))],
        out_shape=[tok(jnp.int32), tok(F32), tok(jnp.int32), jax.ShapeDtypeStruct((N_EXPERTS, 1), F32)],
        scratch_shapes=[pltpu.VMEM((N_EXPERTS, 1), F32)],
        compiler_params=_params(("arbitrary",)),
        name="route",
    )(logits_t, bias.reshape(N_EXPERTS, 1), tri)


def _slot_kernel(start_ref, eidx_ref, rank_ref, dest_ref):
    e = eidx_ref[...]
    d = rank_ref[...]
    for x in range(N_EXPERTS):
        d = d + jnp.where(e == x, start_ref[x], 0)
    dest_ref[...] = d


def slot_index(pad_start, eidx, rank):
    n = eidx.shape[1]
    t = _token_tile(n)
    spec = pl.BlockSpec((TOP_K, t), lambda i: (0, i))
    return pl.pallas_call(
        _slot_kernel,
        grid=(n // t,),
        in_specs=[pl.BlockSpec(memory_space=pltpu.SMEM), spec, spec],
        out_specs=spec,
        out_shape=jax.ShapeDtypeStruct((TOP_K, n), jnp.int32),
        compiler_params=_params(("parallel",)),
        name="slot_index",
    )(pad_start, eidx, rank)


def _sc_worker():
    return lax.axis_index("s") * SC_CORES + lax.axis_index("c")


def sc_dispatch(h, dest3, n_slots):
    n, d = h.shape
    n_win = n // SC_WINDOW
    mesh = plsc.VectorSubcoreMesh(core_axis_name="c", subcore_axis_name="s")

    @functools.partial(
        pl.kernel, mesh=mesh, out_type=jax.ShapeDtypeStruct((n_slots, d), h.dtype),
        scratch_types=[pltpu.VMEM((TOP_K, SC_WINDOW), jnp.int32), pltpu.VMEM((SC_WINDOW, d), h.dtype),
                       pltpu.SemaphoreType.DMA])
    def k(h_hbm, dest_hbm, out_hbm, idx_v, rows_v, sem):
        wid = _sc_worker()

        @pl.loop(0, -(-n_win // SC_WORKERS))
        def _(it):
            w = it * SC_WORKERS + wid

            @pl.when(w < n_win)
            def _():
                pltpu.sync_copy(dest_hbm.at[w], idx_v)
                pltpu.sync_copy(h_hbm.at[pl.ds(w * SC_WINDOW, SC_WINDOW)], rows_v)
                copies = [pltpu.async_copy(rows_v, out_hbm.at[idx_v.at[j]], sem) for j in range(TOP_K)]
                for c in copies:
                    c.wait()

    return k(h, dest3)


def sc_combine_gather(y, dest3):
    d = y.shape[1]
    n_win = dest3.shape[0]
    n = n_win * SC_WINDOW
    mesh = plsc.VectorSubcoreMesh(core_axis_name="c", subcore_axis_name="s")

    @functools.partial(
        pl.kernel, mesh=mesh, out_type=jax.ShapeDtypeStruct((TOP_K, n, d), y.dtype),
        scratch_types=[pltpu.VMEM((TOP_K, SC_WINDOW), jnp.int32), pltpu.VMEM((SC_WINDOW, d), y.dtype),
                       pltpu.SemaphoreType.DMA])
    def k(y_hbm, dest_hbm, out_hbm, idx_v, rows_v, sem):
        wid = _sc_worker()

        @pl.loop(0, -(-n_win // SC_WORKERS))
        def _(it):
            w = it * SC_WORKERS + wid

            @pl.when(w < n_win)
            def _():
                pltpu.sync_copy(dest_hbm.at[w], idx_v)
                for j in range(TOP_K):
                    pltpu.async_copy(y_hbm.at[idx_v.at[j]], rows_v, sem).wait()
                    pltpu.sync_copy(rows_v, out_hbm.at[j, pl.ds(w * SC_WINDOW, SC_WINDOW)])

    return k(y, dest3)


def _expert_ffn_kernel(be_ref, bv_ref, x_ref, wg_ref, wu_ref, wd_ref, o_ref, wg_s, wu_s, wd_s):
    b = pl.program_id(0)
    valid = bv_ref[b]
    new_expert = (b == 0) | (be_ref[b] != be_ref[jnp.maximum(b - 1, 0)])

    @pl.when(new_expert)
    def _():
        wg_s[...] = wg_ref[0].astype(BF16)
        wu_s[...] = wu_ref[0].astype(BF16)
        wd_s[...] = wd_ref[0].astype(BF16)

    @pl.when(valid > 0)
    def _():
        row = lax.broadcasted_iota(jnp.int32, (x_ref.shape[0], 1), 0)
        lo, hi = _unpack_bf16_pair(jnp.where(row < valid, x_ref[...], 0))
        x = jnp.concatenate([lo.astype(BF16), hi.astype(BF16)], axis=1)
        a = jnp.dot(x, wg_s[...], preferred_element_type=F32)
        a = a * jax.nn.sigmoid(a) * jnp.dot(x, wu_s[...], preferred_element_type=F32)
        y = jnp.dot(a.astype(BF16), wd_s[...], preferred_element_type=F32)
        half = y.shape[1] // 2
        o_ref[...] = _pack_bf16_pair(y[:, :half], y[:, half:])


def expert_ffn(xg, block_expert, block_valid, wg, wu, wd):
    n_slots, dp = xg.shape
    d, f = wg.shape[-2:]
    grid_spec = pltpu.PrefetchScalarGridSpec(
        num_scalar_prefetch=2,
        grid=(n_slots // MOE_BLOCK,),
        in_specs=[pl.BlockSpec((MOE_BLOCK, dp), lambda b, be, bv: (b, 0)),
                  pl.BlockSpec((1, d, f), lambda b, be, bv: (be[b], 0, 0)),
                  pl.BlockSpec((1, d, f), lambda b, be, bv: (be[b], 0, 0)),
                  pl.BlockSpec((1, f, d), lambda b, be, bv: (be[b], 0, 0))],
        out_specs=pl.BlockSpec((MOE_BLOCK, dp), lambda b, be, bv: (b, 0)),
        scratch_shapes=[pltpu.VMEM((d, f), BF16), pltpu.VMEM((d, f), BF16), pltpu.VMEM((f, d), BF16)],
    )
    return pl.pallas_call(
        _expert_ffn_kernel,
        grid_spec=grid_spec,
        out_shape=jax.ShapeDtypeStruct((n_slots, dp), jnp.int32),
        compiler_params=_params(("arbitrary",)),
        name="expert_ffn",
    )(block_expert, block_valid, xg, wg, wu, wd)


def _combine_kernel(yg_ref, w_ref, h_ref, swg_ref, swu_ref, swd_ref, o_ref):
    h = h_ref[...]
    a = jnp.dot(h, swg_ref[...], preferred_element_type=F32)
    a = a * jax.nn.sigmoid(a) * jnp.dot(h, swu_ref[...], preferred_element_type=F32)
    acc = jnp.dot(a.astype(BF16), swd_ref[...], preferred_element_type=F32)
    half = acc.shape[1] // 2
    acc_lo, acc_hi = acc[:, :half], acc[:, half:]
    wt = w_ref[...].T
    for k in range(TOP_K):
        lo, hi = _unpack_bf16_pair(yg_ref[k])
        acc_lo = acc_lo + wt[:, k:k + 1] * lo
        acc_hi = acc_hi + wt[:, k:k + 1] * hi
    o_ref[:, :half] = acc_lo
    o_ref[:, half:] = acc_hi


def combine(yg, w, h, swg, swu, swd):
    n, d = h.shape
    f = swg.shape[-1]
    tm = ROW_TILE
    return pl.pallas_call(
        _combine_kernel,
        grid=(n // tm,),
        in_specs=[pl.BlockSpec((TOP_K, tm, d // 2), lambda i: (0, i, 0)),
                  pl.BlockSpec((TOP_K, tm), lambda i: (0, i)),
                  pl.BlockSpec((tm, d), lambda i: (i, 0)),
                  pl.BlockSpec((d, f), lambda i: (0, 0)),
                  pl.BlockSpec((d, f), lambda i: (0, 0)),
                  pl.BlockSpec((f, d), lambda i: (0, 0))],
        out_specs=pl.BlockSpec((tm, d), lambda i: (i, 0)),
        out_shape=jax.ShapeDtypeStruct((n, d), F32),
        compiler_params=_params(("parallel",)),
        name="moe_combine",
    )(yg, w, h, swg, swu, swd)


def moe(h, h_packed, logits_t, bias, wg, wu, wd, swg, swu, swd):
    n, d = h.shape
    eidx, w, rank, counts = route(logits_t, bias)
    counts = counts.reshape(N_EXPERTS).astype(jnp.int32)
    padded = (counts + MOE_BLOCK - 1) // MOE_BLOCK * MOE_BLOCK
    pad_end = jnp.cumsum(padded)
    pad_start = pad_end - padded
    n_slots = n * TOP_K + N_EXPERTS * MOE_BLOCK
    starts = jnp.arange(n_slots // MOE_BLOCK, dtype=jnp.int32) * MOE_BLOCK
    owner = jnp.sum((pad_end[None, :] <= starts[:, None]).astype(jnp.int32), axis=1)
    block_expert = jnp.minimum(owner, N_EXPERTS - 1)
    member = (block_expert[:, None] == jnp.arange(N_EXPERTS, dtype=jnp.int32)[None, :]).astype(jnp.int32)
    left = jnp.sum(member * (counts + pad_start)[None, :], axis=1) - starts
    block_valid = jnp.clip(left, 0, MOE_BLOCK).astype(jnp.int32)
    dest = slot_index(pad_start.astype(jnp.int32), eidx, rank)
    dest3 = dest.reshape(TOP_K, n // SC_WINDOW, SC_WINDOW).transpose(1, 0, 2)
    xg = sc_dispatch(h_packed, dest3, n_slots)
    y = expert_ffn(xg, block_expert, block_valid, wg, wu, wd)
    yg = sc_combine_gather(y, dest3)
    return combine(yg, w, h, swg.astype(BF16), swu.astype(BF16), swd.astype(BF16))


def mixer_ab(h, w_in, w_out, decay_logit, conv_w, conv_b, w1, b1, f1, w2, b2, f2, w3, skip, rope, dft, *, seq,
             ctx_len):
    n_rows = h.shape[0]
    tm = _token_tile(n_rows)
    p = mm([(h, w_in.astype(BF16))], F32, tm, 512, name="ab_in_proj")
    qkv_w = 2 * RET_QK + RET_V
    n_qk = 2 * RET_QK // LANE
    scales = (1.0,) * (RET_QK // LANE) + (RET_DK ** -0.5,) * (RET_QK // LANE) + (1.0,) * (RET_V // LANE)
    qkv = rope_cast(p, rope[0], rope[1], width=qkv_w, n_rot_blocks=n_qk, head_dim=RET_DK, scales=scales,
                    out_dtype=F32)
    log_g = jax.nn.log_sigmoid(decay_logit.astype(F32))
    ret = retention(qkv, p, log_g, jnp.exp(RET_CHUNK * log_g), seq=seq)
    v, x1, x2 = shortconv(p, conv_w, conv_b, seq=seq)
    filt = (w1, b1, f1, w2, b2, f2, w3)
    hy_x = long_conv_two_stage(dft["x"], hyena_filter_taps(seq, *filt), v[:seq], x1[:seq], x2[:seq], skip)
    hy_c = long_conv_one_stage(dft["c"], hyena_filter_taps(ctx_len, *filt), v[seq:], x1[seq:], x2[seq:], skip)
    hy = jnp.concatenate([hy_x, hy_c], axis=0)
    w_out = w_out.astype(BF16)
    return mm([(ret, w_out[:RET_V]), (hy, w_out[RET_V:])], F32, tm, 512, name="ab_out_proj")


def mixer_da(h, w_in, w_out, lam, subln, lambda_init, rope, *, seq, ctx_len):
    n_rows = h.shape[0]
    tm = _token_tile(n_rows)
    p = mm([(h, w_in.astype(BF16))], F32, tm, 512, name="da_in_proj")
    n_qk = 2 * DA_WIDTH // LANE
    scales = (LOG2_E * DA_HEAD_DIM ** -0.5,) * (DA_WIDTH // LANE) + (1.0,) * (2 * DA_WIDTH // LANE)
    qkv = rope_cast(p, rope[0], rope[1], width=3 * DA_WIDTH, n_rot_blocks=n_qk, head_dim=DA_HEAD_DIM, scales=scales,
                    out_dtype=BF16)
    lam_f = lam.astype(F32)
    lam_full = jnp.exp(jnp.sum(lam_f[0] * lam_f[1])) - jnp.exp(jnp.sum(lam_f[2] * lam_f[3])) + lambda_init
    o = diff_attention(qkv, lam_full, subln, seq=seq, ctx_len=ctx_len, lambda_init=lambda_init)
    return mm([(o, w_out.astype(BF16))], F32, tm, 512, name="da_out_proj")


def kernel(x, c, ctx, c_ctx, w_ada, b_ada, norm_mix, norm_ffn, ab_w_in, ab_w_out, ret_decay_logit, hy_conv_w, hy_conv_b, hy_w1, hy_b1, hy_freq1, hy_w2, hy_b2, hy_freq2, hy_w3, hy_skip, da_w_in, da_w_out, da_lambda, da_subln, router_w, router_b, exp_w_gate, exp_w_up, exp_w_down, sh_w_gate, sh_w_up, sh_w_down, norm_final):
    batch, seq, d = x.shape
    ctx_len = ctx.shape[1]
    assert batch == 1 and seq % ROW_TILE == 0 and ctx_len == ROW_TILE
    depth = w_ada.shape[0]
    n_rows = seq + ctx_len
    ctx_tile = seq // ROW_TILE

    xs = jnp.concatenate([x[0], ctx[0]], axis=0)
    cv = jnp.zeros((SUBLANE, d), F32).at[0].set(c_ctx).at[1].set(c[0])
    mods = adaln(cv, w_ada, b_ada)[:, :2].reshape(depth, 2, 6, d)

    rope_ret = rope_tables(seq, ctx_len, RET_DK)
    rope_da = rope_tables(seq, ctx_len, DA_HEAD_DIM)
    dft = dict(x=dft_tables_two_stage(2 * seq), c=dft_tables_one_stage(2 * ctx_len))
    common = dict(n_rows=n_rows, ctx_tile=ctx_tile)

    delta, gate_mods = None, None
    for i in range(depth):
        j = i // 2
        if delta is None:
            (h,) = norm_mod(xs, norm_mix[i], mods=mods[i], shift_idx=0, scale_idx=1, **common)
        else:
            xs, h = norm_mod(xs, norm_mix[i], delta=delta, gate_mods=gate_mods, gate_idx=5, mods=mods[i], shift_idx=0,
                             scale_idx=1, **common)
        if i % 2 == 0:
            y = mixer_ab(h, ab_w_in[j], ab_w_out[j], ret_decay_logit[j], hy_conv_w[j], hy_conv_b[j], hy_w1[j],
                         hy_b1[j], hy_freq1[j], hy_w2[j], hy_b2[j], hy_freq2[j], hy_w3[j], hy_skip[j], rope_ret, dft,
                         seq=seq, ctx_len=ctx_len)
        else:
            lambda_init = 0.8 - 0.6 * math.exp(-0.3 * i)
            y = mixer_da(h, da_w_in[j], da_w_out[j], da_lambda[j], da_subln[j], lambda_init, rope_da, seq=seq,
                         ctx_len=ctx_len)
        xs, h, logits_t, h_packed = norm_mod(xs, norm_ffn[i], delta=y, gate_mods=mods[i], gate_idx=2, mods=mods[i],
                                             shift_idx=3, scale_idx=4, router_wt=router_w[i].T, **common)
        delta = moe(h, h_packed, logits_t, router_b[i], exp_w_gate[i], exp_w_up[i], exp_w_down[i],
                    sh_w_gate[i], sh_w_up[i], sh_w_down[i])
        gate_mods = mods[i]
    _, out = norm_mod(xs, norm_final, n_rows=seq, ctx_tile=ctx_tile, delta=delta, gate_mods=gate_mods, gate_idx=5,
                      out_dtype=F32)
    return out[None]
```

```python
import functools
import math

import jax
import jax.numpy as jnp
from jax import lax
from jax.experimental import pallas as pl
from jax.experimental.pallas import tpu as pltpu
from jax.experimental.pallas import tpu_sc as plsc

F32 = jnp.float32
BF16 = jnp.bfloat16
HIGHEST = lax.Precision.HIGHEST

D_MODEL = 1024
DEPTH = 4
GRID_W = 64
EPS = 1e-6
ROPE_BASE = 10000.0

RET_HEADS = 4
RET_DK = 128
RET_DV = 256
RET_CHUNK = 128
RET_QK = RET_HEADS * RET_DK
RET_V = RET_HEADS * RET_DV

HY_WIDTH = 512
HY_ORDER = 2
HY_BANDS = 16
HY_EMB = 2 * HY_BANDS + 1
HY_FFN = 64
HY_DECAY_TARGET = 1e-2
HY_FAST_DECAY = 0.3
HY_SLOW_DECAY = 1.5
HY_ZCOLS = 64
HY_VALID_COL = HY_EMB
FFT_N2 = 128

AB_IN = 2 * RET_QK + 2 * RET_V + (HY_ORDER + 1) * HY_WIDTH
AB_CAT = RET_V + HY_WIDTH

DA_HEADS = 8
DA_HEAD_DIM = 64
DA_WIDTH = DA_HEADS * 2 * DA_HEAD_DIM

N_EXPERTS = 64
TOP_K = 8
N_GROUPS = 8
TOPK_GROUPS = 4
GROUP_SIZE = N_EXPERTS // N_GROUPS
EXPERT_DIM = 256
ROUTED_SCALE = 2.5
MOE_BLOCK = 512
MOE_SUB_BLOCKS = 2
SC_CORES = 2
SC_SUBCORES = 16
SC_WORKERS = SC_CORES * SC_SUBCORES
SC_WINDOW = 64

LANE = 128
SUBLANE = 8
ROW_TILE = 256
MAX_TOKEN_TILE = 1280
VMEM_LIMIT = 48 * 1024 * 1024
FLASH_VMEM_LIMIT = 56 * 1024 * 1024
NEG_BIG = -1e30
LOG2_E = 1.4426950408889634
FLASH_ONES_ROWS = 16
FLASH_INIT_KEYS = 16
FLASH_LAZY_HEADROOM = 60.0


def _params(sem):
    return pltpu.CompilerParams(dimension_semantics=sem, vmem_limit_bytes=VMEM_LIMIT)


def _token_tile(n):
    best = ROW_TILE
    t = ROW_TILE
    while t <= min(n, MAX_TOKEN_TILE):
        if n % t == 0:
            best = t
        t += ROW_TILE
    return best


def _mm_kernel(*refs, n_pairs, has_epi):
    acc = None
    for p in range(n_pairs):
        a = refs[2 * p][...].astype(BF16)
        b = refs[2 * p + 1][...].astype(BF16)
        d = jnp.dot(a, b, preferred_element_type=F32)
        acc = d if acc is None else acc + d
    idx = 2 * n_pairs
    if has_epi:
        acc = refs[idx][...] * (acc + refs[idx + 1][...] * refs[idx + 2][...])
        idx += 3
    o_ref = refs[idx]
    o_ref[...] = acc.astype(o_ref.dtype)


def mm(pairs, out_dtype, tm, tn, epi=None, name="mm"):
    m = pairs[0][0].shape[0]
    n = pairs[0][1].shape[1]
    assert m % tm == 0 and n % tn == 0
    in_specs, args = [], []
    for a, b in pairs:
        k = a.shape[1]
        in_specs += [pl.BlockSpec((tm, k), lambda i, j: (i, 0)), pl.BlockSpec((k, tn), lambda i, j: (0, j))]
        args += [a, b]
    if epi is not None:
        in_specs += [pl.BlockSpec((tm, tn), lambda i, j: (i, j)), pl.BlockSpec((1, tn), lambda i, j: (0, j)),
                     pl.BlockSpec((tm, tn), lambda i, j: (i, j))]
        args += list(epi)
    return pl.pallas_call(
        functools.partial(_mm_kernel, n_pairs=len(pairs), has_epi=epi is not None),
        grid=(m // tm, n // tn),
        in_specs=in_specs,
        out_specs=pl.BlockSpec((tm, tn), lambda i, j: (i, j)),
        out_shape=jax.ShapeDtypeStruct((m, n), out_dtype),
        compiler_params=_params(("parallel", "parallel")),
        name=name,
    )(*args)


def _adaln_kernel(cv_ref, w_ref, b_ref, o_ref):
    cv = cv_ref[...]
    s = cv * jax.nn.sigmoid(cv)
    o_ref[0] = jnp.dot(s, w_ref[0], precision=HIGHEST, preferred_element_type=F32) + b_ref[0]


def adaln(cv, w_ada, b_ada):
    depth, d, n = w_ada.shape
    tn = 1536
    return pl.pallas_call(
        _adaln_kernel,
        grid=(depth, n // tn),
        in_specs=[pl.BlockSpec((SUBLANE, d), lambda l, j: (0, 0)),
                  pl.BlockSpec((1, d, tn), lambda l, j: (l, 0, j)),
                  pl.BlockSpec((1, 1, tn), lambda l, j: (l, 0, j))],
        out_specs=pl.BlockSpec((1, SUBLANE, tn), lambda l, j: (l, 0, j)),
        out_shape=jax.ShapeDtypeStruct((depth, SUBLANE, n), F32),
        compiler_params=_params(("parallel", "parallel")),
        name="adaln",
    )(cv, w_ada, b_ada.reshape(depth, 1, n))


def _norm_mod_kernel(*refs, has_delta, gate_idx, shift_idx, scale_idx, has_router, write_xs):
    it = iter(refs)
    xs_ref = next(it)
    x = xs_ref[...]
    if has_delta:
        delta_ref = next(it)
        gmods_ref = next(it)
        x = x + gmods_ref[0, gate_idx:gate_idx + 1, :] * delta_ref[...]
    mods_ref = next(it) if shift_idx is not None else None
    g_ref = next(it)
    wr_ref = next(it) if has_router else None
    if write_xs:
        next(it)[...] = x
    h_ref = next(it)
    y = x * lax.rsqrt(jnp.mean(x * x, axis=-1, keepdims=True) + EPS) * g_ref[...]
    if shift_idx is not None:
        y = y * (1.0 + mods_ref[0, scale_idx:scale_idx + 1, :]) + mods_ref[0, shift_idx:shift_idx + 1, :]
    h_ref[...] = y.astype(h_ref.dtype)
    if has_router:
        lg_ref = next(it)
        lg_ref[...] = lax.dot_general(wr_ref[...], y, (((1,), (1,)), ((), ())),
                                      precision=HIGHEST, preferred_element_type=F32)
        half = y.shape[1] // 2
        next(it)[...] = _pack_bf16_pair(y[:, :half], y[:, half:])


def _pack_bf16_pair(a, b):
    def rounded(x):
        u = lax.bitcast_convert_type(x, jnp.int32)
        return u + 0x7FFF + (lax.shift_right_logical(u, 16) & 1)
    return lax.shift_right_logical(rounded(a), 16) | (rounded(b) & -65536)


def _unpack_bf16_pair(w):
    return (lax.bitcast_convert_type(lax.shift_left(w, 16), F32),
            lax.bitcast_convert_type(w & -65536, F32))


def norm_mod(xs, g, *, n_rows, ctx_tile, delta=None, gate_mods=None, gate_idx=None, mods=None, shift_idx=None,
             scale_idx=None, router_wt=None, out_dtype=BF16):
    d = xs.shape[1]
    n_tiles = n_rows // ROW_TILE
    row = pl.BlockSpec((ROW_TILE, d), lambda i: (i, 0))
    mod_spec = pl.BlockSpec((1, 6, d), lambda i: (jnp.where(i == ctx_tile, 0, 1), 0, 0))
    in_specs, args = [row], [xs]
    has_delta = delta is not None
    if has_delta:
        in_specs += [row, mod_spec]
        args += [delta, gate_mods]
    if shift_idx is not None:
        in_specs.append(mod_spec)
        args.append(mods)
    in_specs.append(pl.BlockSpec((1, d), lambda i: (0, 0)))
    args.append(g.reshape(1, d))
    has_router = router_wt is not None
    if has_router:
        in_specs.append(pl.BlockSpec(router_wt.shape, lambda i: (0, 0)))
        args.append(router_wt)
    out_specs, out_shape = [], []
    if has_delta:
        out_specs.append(row)
        out_shape.append(jax.ShapeDtypeStruct((n_rows, d), F32))
    out_specs.append(row)
    out_shape.append(jax.ShapeDtypeStruct((n_rows, d), out_dtype))
    if has_router:
        out_specs.append(pl.BlockSpec((N_EXPERTS, ROW_TILE), lambda i: (0, i)))
        out_shape.append(jax.ShapeDtypeStruct((N_EXPERTS, n_rows), F32))
        out_specs.append(pl.BlockSpec((ROW_TILE, d // 2), lambda i: (i, 0)))
        out_shape.append(jax.ShapeDtypeStruct((n_rows, d // 2), jnp.int32))
    return pl.pallas_call(
        functools.partial(_norm_mod_kernel, has_delta=has_delta, gate_idx=gate_idx, shift_idx=shift_idx,
                          scale_idx=scale_idx, has_router=has_router, write_xs=has_delta),
        grid=(n_tiles,),
        in_specs=in_specs,
        out_specs=out_specs,
        out_shape=out_shape,
        compiler_params=_params(("parallel",)),
        name="norm_mod",
    )(*args)


def _rope_kernel(p_ref, cos_ref, sin_ref, o_ref, *, n_rot_blocks, head_dim, scales):
    cos = cos_ref[...]
    sin = sin_ref[...]
    for b in range(len(scales)):
        x = p_ref[:, b * LANE:(b + 1) * LANE]
        if b < n_rot_blocks:
            if head_dim == LANE:
                rot = pltpu.roll(x, LANE // 2, 1)
            else:
                lane = lax.broadcasted_iota(jnp.int32, x.shape, 1)
                first_half = (lane % head_dim) < head_dim // 2
                rot = jnp.where(first_half, pltpu.roll(x, LANE - head_dim // 2, 1), pltpu.roll(x, head_dim // 2, 1))
            x = x * cos + rot * sin
        if scales[b] != 1.0:
            x = x * scales[b]
        o_ref[:, b * LANE:(b + 1) * LANE] = x.astype(o_ref.dtype)


def rope_cast(p, cos, sin, *, width, n_rot_blocks, head_dim, scales, out_dtype):
    n_rows = p.shape[0]
    return pl.pallas_call(
        functools.partial(_rope_kernel, n_rot_blocks=n_rot_blocks, head_dim=head_dim, scales=scales),
        grid=(n_rows // ROW_TILE,),
        in_specs=[pl.BlockSpec((ROW_TILE, width), lambda i: (i, 0)),
                  pl.BlockSpec((ROW_TILE, LANE), lambda i: (i, 0)),
                  pl.BlockSpec((ROW_TILE, LANE), lambda i: (i, 0))],
        out_specs=pl.BlockSpec((ROW_TILE, width), lambda i: (i, 0)),
        out_shape=jax.ShapeDtypeStruct((n_rows, width), out_dtype),
        compiler_params=_params(("parallel",)),
        name="rope_cast",
    )(p, cos, sin)


def rope_tables(seq, ctx_len, head_dim):
    n_freq = head_dim // 4
    inv = ROPE_BASE ** (-jnp.arange(n_freq, dtype=F32) / n_freq)
    rows = seq // GRID_W
    row = jnp.repeat(jnp.arange(rows, dtype=F32), GRID_W)
    col = jnp.tile(jnp.arange(GRID_W, dtype=F32), rows)
    ang = jnp.concatenate([row[:, None] * inv, col[:, None] * inv], axis=-1)
    cos, sin = jnp.cos(ang), jnp.sin(ang)
    cos = jnp.concatenate([cos, cos], axis=-1)
    sin = jnp.concatenate([-sin, sin], axis=-1)
    reps = LANE // head_dim
    cos, sin = jnp.tile(cos, (1, reps)), jnp.tile(sin, (1, reps))
    cos = jnp.concatenate([cos, jnp.ones((ctx_len, LANE), F32)], axis=0)
    sin = jnp.concatenate([sin, jnp.zeros((ctx_len, LANE), F32)], axis=0)
    return cos, sin


def _ret_kernel(lg_ref, gc_ref, q_ref, k_ref, v_ref, *rest, reverse):
    if reverse:
        yf_ref, gate_ref, o_ref, s_ref = rest
    else:
        o_ref, s_ref = rest
    c = RET_CHUNK

    @pl.when(pl.program_id(0) == 0)
    def _():
        s_ref[...] = jnp.zeros_like(s_ref)

    ii = lax.broadcasted_iota(jnp.int32, (c, c), 0)
    jj = lax.broadcasted_iota(jnp.int32, (c, c), 1)
    rel = ((jj - ii) if reverse else (ii - jj)).astype(F32)
    pos = lax.broadcasted_iota(jnp.int32, (c, 1), 0).astype(F32)
    for h in range(RET_HEADS):
        lg = lg_ref[h]
        dec = jnp.where(rel >= 0, jnp.exp(jnp.maximum(rel, 0.0) * lg), 0.0)
        if reverse:
            q_dec = jnp.exp((c - pos) * lg)
            k_dec = jnp.exp(pos * lg)
        else:
            q_dec = jnp.exp((pos + 1.0) * lg)
            k_dec = jnp.exp((c - 1.0 - pos) * lg)
        q = q_ref[:, h * RET_DK:(h + 1) * RET_DK]
        k = k_ref[:, h * RET_DK:(h + 1) * RET_DK]
        v = v_ref[:, h * RET_DV:(h + 1) * RET_DV].astype(BF16)
        s = lax.dot_general(q.astype(BF16), k.astype(BF16), (((1,), (1,)), ((), ())),
                            preferred_element_type=F32) * dec
        state = s_ref[h]
        y = jnp.dot(s.astype(BF16), v, preferred_element_type=F32)
        y = y + jnp.dot((q * q_dec).astype(BF16), state.astype(BF16), preferred_element_type=F32)
        upd = lax.dot_general((k * k_dec).astype(BF16), v, (((0,), (0,)), ((), ())), preferred_element_type=F32)
        s_ref[h] = gc_ref[h] * state + upd
        if reverse:
            r = y + yf_ref[:, h * RET_DV:(h + 1) * RET_DV]
            mu = jnp.mean(r, axis=-1, keepdims=True)
            rc = r - mu
            var = jnp.mean(rc * rc, axis=-1, keepdims=True)
            g = gate_ref[:, h * RET_DV:(h + 1) * RET_DV]
            o_ref[:, h * RET_DV:(h + 1) * RET_DV] = (rc * lax.rsqrt(var + EPS) * (g * jax.nn.sigmoid(g))).astype(
                o_ref.dtype)
        else:
            o_ref[:, h * RET_DV:(h + 1) * RET_DV] = y


def retention(qkv, p, log_g, g_chunk, *, seq):
    n_rows = qkv.shape[0]
    n_chunks = n_rows // RET_CHUNK
    n_x = seq // RET_CHUNK
    smem = pl.BlockSpec(memory_space=pltpu.SMEM)

    def run(reverse, extra):
        if reverse:
            idx = lambda t: n_chunks - 1 - t
        else:
            idx = lambda t: (t + n_x) % n_chunks
        in_specs = [smem, smem,
                    pl.BlockSpec((RET_CHUNK, RET_QK), lambda t: (idx(t), 0)),
                    pl.BlockSpec((RET_CHUNK, RET_QK), lambda t: (idx(t), 1)),
                    pl.BlockSpec((RET_CHUNK, RET_V), lambda t: (idx(t), 1))]
        args = [log_g[1 if reverse else 0], g_chunk[1 if reverse else 0], qkv, qkv, qkv]
        if reverse:
            in_specs += [pl.BlockSpec((RET_CHUNK, RET_V), lambda t: (idx(t), 0)),
                         pl.BlockSpec((RET_CHUNK, RET_V), lambda t: (idx(t), 2))]
            args += list(extra)
        return pl.pallas_call(
            functools.partial(_ret_kernel, reverse=reverse),
            grid=(n_chunks,),
            in_specs=in_specs,
            out_specs=pl.BlockSpec((RET_CHUNK, RET_V), lambda t: (idx(t), 0)),
            out_shape=jax.ShapeDtypeStruct((n_rows, RET_V), BF16 if reverse else F32),
            scratch_shapes=[pltpu.VMEM((RET_HEADS, RET_DK, RET_DV), F32)],
            compiler_params=_params(("arbitrary",)),
            name="retention_bwd" if reverse else "retention_fwd",
        )(*args)

    y_fwd = run(False, None)
    return run(True, (y_fwd, p))


def _shortconv_kernel(cur_ref, prev_ref, next_ref, w_ref, b_ref, v_ref, x1_ref, x2_ref, *, x_tiles):
    i = pl.program_id(0)
    cur = cur_ref[...]
    rows = cur.shape[0]
    row = lax.broadcasted_iota(jnp.int32, (rows, 1), 0)
    has_prev = jnp.where((i == 0) | (i == x_tiles), 0.0, 1.0)
    has_next = jnp.where((i == x_tiles - 1) | (i == x_tiles), 0.0, 1.0)
    up = jnp.where(row == 0, prev_ref[SUBLANE - 1:SUBLANE, :] * has_prev, pltpu.roll(cur, 1, 0))
    dn = jnp.where(row == rows - 1, next_ref[0:1, :] * has_next, pltpu.roll(cur, rows - 1, 0))
    y = up * w_ref[0:1, :] + cur * w_ref[1:2, :] + dn * w_ref[2:3, :] + b_ref[...]
    v_ref[...] = y[:, :HY_WIDTH]
    x1_ref[...] = y[:, HY_WIDTH:2 * HY_WIDTH]
    x2_ref[...] = y[:, 2 * HY_WIDTH:]


def shortconv(p, w, b, *, seq):
    n_rows = p.shape[0]
    width = 3 * HY_WIDTH
    col = p.shape[1] // width - 1
    per = ROW_TILE // SUBLANE
    last = n_rows // SUBLANE - 1
    out = jax.ShapeDtypeStruct((n_rows, HY_WIDTH), F32)
    ospec = pl.BlockSpec((ROW_TILE, HY_WIDTH), lambda i: (i, 0))
    return pl.pallas_call(
        functools.partial(_shortconv_kernel, x_tiles=seq // ROW_TILE),
        grid=(n_rows // ROW_TILE,),
        in_specs=[pl.BlockSpec((ROW_TILE, width), lambda i: (i, col)),
                  pl.BlockSpec((SUBLANE, width), lambda i: (jnp.maximum(i * per - 1, 0), col)),
                  pl.BlockSpec((SUBLANE, width), lambda i: (jnp.minimum((i + 1) * per, last), col)),
                  pl.BlockSpec((3, width), lambda i: (0, 0)),
                  pl.BlockSpec((1, width), lambda i: (0, 0))],
        out_specs=[ospec, ospec, ospec],
        out_shape=[out, out, out],
        compiler_params=_params(("parallel",)),
        name="shortconv",
    )(p, p, p, w, b.reshape(1, width))


def _filt_kernel(z_ref, w1_ref, b1_ref, f1_ref, w2_ref, b2_ref, f2_ref, w3a_ref, w3b_ref, dl_ref, o_ref):
    z = z_ref[...]
    h = jnp.sin(f1_ref[...] * (jnp.dot(z, w1_ref[...], precision=HIGHEST, preferred_element_type=F32) + b1_ref[...]))
    h = jnp.sin(f2_ref[...] * (jnp.dot(h, w2_ref[...], precision=HIGHEST, preferred_element_type=F32) + b2_ref[...]))
    window = jnp.exp(-z[:, 0:1] * dl_ref[...]) * z[:, HY_VALID_COL:HY_VALID_COL + 1]
    for o, w3_ref in enumerate((w3a_ref, w3b_ref)):
        o_ref[o] = jnp.dot(h, w3_ref[...], precision=HIGHEST, preferred_element_type=F32) * window


def hyena_filter_taps(length, w1, b1, f1, w2, b2, f2, w3):
    z = _filter_positions(length)
    w1p = jnp.zeros((HY_ZCOLS, HY_FFN), F32).at[:HY_EMB].set(w1)
    deltas = jnp.abs(jnp.linspace(math.log(HY_DECAY_TARGET) / HY_SLOW_DECAY,
                                  math.log(HY_DECAY_TARGET) / HY_FAST_DECAY, HY_WIDTH, dtype=F32)).reshape(1, HY_WIDTH)
    tm = min(length, 512)
    half_tiles = length // tm
    vec = lambda a: a.reshape(1, HY_FFN)
    small = lambda shape: pl.BlockSpec(shape, lambda i: (0, 0))
    w3_spec = lambda o: pl.BlockSpec((HY_FFN, HY_WIDTH), lambda i: (0, 2 * o + jnp.where(i >= half_tiles, 1, 0)))
    assert HY_ORDER == 2
    return pl.pallas_call(
        _filt_kernel,
        grid=(2 * half_tiles,),
        in_specs=[pl.BlockSpec((tm, HY_ZCOLS), lambda i: (i, 0)),
                  small((HY_ZCOLS, HY_FFN)), small((1, HY_FFN)), small((1, HY_FFN)),
                  small((HY_FFN, HY_FFN)), small((1, HY_FFN)), small((1, HY_FFN)),
                  w3_spec(0), w3_spec(1), small((1, HY_WIDTH))],
        out_specs=pl.BlockSpec((HY_ORDER, tm, HY_WIDTH), lambda i: (0, i, 0)),
        out_shape=jax.ShapeDtypeStruct((HY_ORDER, 2 * length, HY_WIDTH), F32),
        compiler_params=_params(("parallel",)),
        name="hyena_filter",
    )(z, w1p, vec(b1), vec(f1), w2, vec(b2), vec(f2), w3, w3, deltas)


def _filter_positions(length):
    t = jnp.concatenate([jnp.arange(length, dtype=F32), float(length) - jnp.arange(length, dtype=F32)])
    valid = jnp.ones((2 * length,), F32).at[length].set(0.0)
    t_norm = t / max(length - 1, 1)
    bands = jnp.linspace(1e-4, HY_BANDS - 1, HY_BANDS, dtype=F32)
    ang = (2.0 * math.pi / length) * t[:, None] * bands[None, :]
    z = jnp.concatenate([t_norm[:, None], jnp.cos(ang), -jnp.sin(ang), valid[:, None]], axis=-1)
    return jnp.pad(z, ((0, 0), (0, HY_ZCOLS - z.shape[1])))


def _angles(num, den):
    return (2.0 * math.pi / den) * (num % den).astype(F32)


def dft_tables_two_stage(m):
    n2 = FFT_N2
    n1 = m // n2
    half = n1 // 2
    kp = -(-(half + 1) // SUBLANE) * SUBLANE
    k1 = jnp.arange(kp, dtype=jnp.int32)
    live = (k1 <= half)
    a1 = _angles(k1[:, None] * jnp.arange(n1, dtype=jnp.int32)[None, :], n1)
    f1 = jnp.concatenate([jnp.where(live[:, None], jnp.cos(a1), 0.0), jnp.where(live[:, None], -jnp.sin(a1), 0.0)], 0)
    wgt = jnp.where((k1 == 0) | (k1 == half), 1.0, 2.0) * live / m
    a1h = a1[:, :half].T
    cinv = jnp.concatenate([jnp.cos(a1h) * wgt[None, :], -jnp.sin(a1h) * wgt[None, :]], axis=1)
    k = k1[:, None, None] + n1 * jnp.arange(n2, dtype=jnp.int32)[None, :, None]
    th = _angles(k * jnp.arange(n2, dtype=jnp.int32)[None, None, :], m)
    c = jnp.where(live[:, None, None], jnp.cos(th), 0.0)
    s = jnp.where(live[:, None, None], jnp.sin(th), 0.0)
    g_fwd = jnp.concatenate([jnp.concatenate([c, s], 2), jnp.concatenate([-s, c], 2)], 1)
    ct, st = jnp.swapaxes(c, 1, 2), jnp.swapaxes(s, 1, 2)
    g_inv = jnp.concatenate([jnp.concatenate([ct, -st], 2), jnp.concatenate([st, ct], 2)], 1)
    return dict(n1=n1, kp=kp, f1=f1.astype(BF16), f1_half=f1[:, :half].astype(BF16), cinv=cinv.astype(BF16),
                g_fwd=g_fwd.astype(BF16), g_inv=g_inv.astype(BF16))


def dft_tables_one_stage(m):
    half = m // 2
    kp = -(-(half + 1) // SUBLANE) * SUBLANE
    k = jnp.arange(kp, dtype=jnp.int32)
    live = (k <= half)
    a = _angles(k[:, None] * jnp.arange(m, dtype=jnp.int32)[None, :], m)
    f = jnp.concatenate([jnp.where(live[:, None], jnp.cos(a), 0.0), jnp.where(live[:, None], -jnp.sin(a), 0.0)], 0)
    wgt = jnp.where((k == 0) | (k == half), 1.0, 2.0) * live / m
    ah = a[:, :half].T
    cinv = jnp.concatenate([jnp.cos(ah) * wgt[None, :], -jnp.sin(ah) * wgt[None, :]], axis=1)
    return dict(kp=kp, f=f.astype(BF16), f_half=f[:, :half].astype(BF16), cinv=cinv.astype(BF16))


def _bmm_kernel(*refs, kb, in_part_major, out_part_major, has_h):
    if has_h:
        g_ref, a_ref, h_ref, o_ref = refs
    else:
        g_ref, a_ref, o_ref = refs
    n2 = FFT_N2
    for b in range(kb):
        if in_part_major:
            ar, ai = a_ref[0, b], a_ref[1, b]
        else:
            ar, ai = a_ref[b, 0], a_ref[b, 1]
        if has_h:
            hr, hi = h_ref[b, 0], h_ref[b, 1]
            ar, ai = ar * hr - ai * hi, ar * hi + ai * hr
        xin = jnp.concatenate([ar, ai], axis=0).astype(BF16)
        y = jnp.dot(g_ref[b], xin, preferred_element_type=F32)
        if out_part_major:
            o_ref[0, b] = y[:n2]
            o_ref[1, b] = y[n2:]
        else:
            o_ref[b, 0] = y[:n2]
            o_ref[b, 1] = y[n2:]


def bmm_k1(g, a, h=None, *, in_part_major, out_part_major):
    kp = g.shape[0]
    n2 = FFT_N2
    c = a.shape[-1]
    kb, tc = SUBLANE, 256
    pm = lambda: pl.BlockSpec((2, kb, n2, tc), lambda i, j: (0, i, 0, j))
    km = lambda: pl.BlockSpec((kb, 2, n2, tc), lambda i, j: (i, 0, 0, j))
    in_specs = [pl.BlockSpec((kb, 2 * n2, 2 * n2), lambda i, j: (i, 0, 0)), pm() if in_part_major else km()]
    args = [g, a]
    if h is not None:
        in_specs.append(km())
        args.append(h)
    return pl.pallas_call(
        functools.partial(_bmm_kernel, kb=kb, in_part_major=in_part_major, out_part_major=out_part_major,
                          has_h=h is not None),
        grid=(kp // kb, c // tc),
        in_specs=in_specs,
        out_specs=pm() if out_part_major else km(),
        out_shape=jax.ShapeDtypeStruct((2, kp, n2, c) if out_part_major else (kp, 2, n2, c), F32),
        compiler_params=_params(("parallel", "parallel")),
        name="dft_inner",
    )(*args)


def _cmul_kernel(x_ref, h_ref, o_ref):
    xr, xi, hr, hi = x_ref[0], x_ref[1], h_ref[0], h_ref[1]
    o_ref[0] = xr * hr - xi * hi
    o_ref[1] = xr * hi + xi * hr


def cmul(x, h):
    spec = pl.BlockSpec(x.shape, lambda i: (0, 0, 0))
    return pl.pallas_call(_cmul_kernel, grid=(1,), in_specs=[spec, spec], out_specs=spec,
                          out_shape=jax.ShapeDtypeStruct(x.shape, F32), compiler_params=_params(("arbitrary",)),
                          name="spectrum_product")(x, h)


def long_conv_two_stage(tabs, taps, v, x1, x2, skip):
    length, c = v.shape
    n2, n1, kp = FFT_N2, tabs["n1"], tabs["kp"]
    cols = n2 * c
    tn = 2048

    def fwd(seq2d, full):
        a = mm([(tabs["f1"] if full else tabs["f1_half"], seq2d)], F32, 2 * kp, tn, name="dft_outer")
        return a.reshape(2, kp, n2, c)

    spectra = [bmm_k1(tabs["g_fwd"], fwd(taps[o].reshape(n1, cols), True), in_part_major=True, out_part_major=False)
               for o in range(HY_ORDER)]
    u = v
    for o, gate in enumerate((x1, x2)):
        xf = bmm_k1(tabs["g_fwd"], fwd(u.reshape(n1 // 2, cols), False), in_part_major=True, out_part_major=False)
        bt = bmm_k1(tabs["g_inv"], xf, spectra[o], in_part_major=False, out_part_major=True)
        skip_row = jnp.tile(skip[o].reshape(1, c), (1, n2))
        u = mm([(tabs["cinv"], bt.reshape(2 * kp, cols))], F32, n1 // 2, tn,
               epi=(gate.reshape(n1 // 2, cols), skip_row, u.reshape(n1 // 2, cols)), name="idft_outer_gate")
        u = u.reshape(length, c)
    return u


def long_conv_one_stage(tabs, taps, v, x1, x2, skip):
    length, c = v.shape
    kp = tabs["kp"]
    u = v
    for o, gate in enumerate((x1, x2)):
        hs = mm([(tabs["f"], taps[o])], F32, 2 * kp, c, name="ctx_dft").reshape(2, kp, c)
        xs = mm([(tabs["f_half"], u)], F32, 2 * kp, c, name="ctx_dft").reshape(2, kp, c)
        ys = cmul(xs, hs).reshape(2 * kp, c)
        u = mm([(tabs["cinv"], ys)], F32, length, c, epi=(gate, skip[o].reshape(1, c), u), name="ctx_idft_gate")
    return u


def _flash_kernel(lam_ref, qt_ref, k_ref, vt_ref, sub_ref, o_ref, m_ref, acc_ref, *, kv, seq, ctx_len, out_scale):
    i = pl.program_id(1)
    last_q = pl.num_programs(1) - 1
    tq = qt_ref.shape[1]
    d = DA_HEAD_DIM
    dv = 2 * DA_HEAD_DIM
    n_chunks = k_ref.shape[0] // kv
    acc_ref[...] = jnp.zeros_like(acc_ref)

    def scores(off, rows, c, masked):
        s = jnp.dot(k_ref[pl.ds(off, rows), c * d:(c + 1) * d], qt_ref[c * d:(c + 1) * d, :],
                    preferred_element_type=F32)
        if masked:
            key = off + lax.broadcasted_iota(jnp.int32, (rows, 1), 0)
            lane = lax.broadcasted_iota(jnp.int32, (1, tq), 1)
            s = s + jnp.where(key < seq, NEG_BIG, 0.0) * jnp.where(lane >= tq - ctx_len, 1.0, 0.0)
        return s

    def exact_step(off, c, masked):
        s = scores(off, kv, c, masked)
        m_old = m_ref[c]
        m_new = jnp.maximum(m_old, jnp.max(s, axis=0, keepdims=True))
        pr = jnp.exp2(s - m_new).astype(BF16)
        acc_ref[c] = jnp.exp2(m_old - m_new) * acc_ref[c] + jnp.dot(vt_ref[:, pl.ds(off, kv)], pr,
                                                                   preferred_element_type=F32)
        m_ref[c] = m_new

    def lazy_step(off, c, masked):
        s = scores(off, kv, c, masked)
        m_old = m_ref[c]
        m_chunk = jnp.max(s, axis=0, keepdims=True)
        pv = jnp.dot(vt_ref[:, pl.ds(off, kv)], jnp.exp2(s - m_old).astype(BF16), preferred_element_type=F32)
        safe = jnp.max(m_chunk - m_old) <= FLASH_LAZY_HEADROOM

        @pl.when(safe)
        def _():
            m_new = jnp.maximum(m_old, m_chunk)
            acc_ref[c] = jnp.exp2(m_old - m_new) * (acc_ref[c] + pv)
            m_ref[c] = m_new

        @pl.when(jnp.logical_not(safe))
        def _():
            exact_step(off, c, masked)

    def run(masked):
        for c in range(2):
            m_ref[c] = jnp.max(scores(0, FLASH_INIT_KEYS, c, masked), axis=0, keepdims=True)

        def body(kc, carry):
            off = pl.multiple_of(kc * kv, kv)
            for c in range(2):
                lazy_step(off, c, masked)
            return carry

        lax.fori_loop(0, n_chunks, body, 0)

    @pl.when(i != last_q)
    def _():
        run(False)

    @pl.when(i == last_q)
    def _():
        run(True)

    a0 = acc_ref[0, :dv, :] / acc_ref[0, dv:dv + 1, :]
    a1 = acc_ref[1, :dv, :] / acc_ref[1, dv:dv + 1, :]
    o = (a0 - lam_ref[0] * a1).T
    o = o * lax.rsqrt(jnp.mean(o * o, axis=-1, keepdims=True) + 1e-5) * sub_ref[...]
    o_ref[...] = (o * out_scale).astype(o_ref.dtype)


def _rope_da_kernel(p_ref, cos_ref, sin_ref, qt_ref, k_ref, vt_ref):
    cos = cos_ref[...]
    sin = sin_ref[...]
    hw = 2 * DA_HEAD_DIM
    lane = lax.broadcasted_iota(jnp.int32, cos.shape, 1)
    first_half = (lane % DA_HEAD_DIM) < DA_HEAD_DIM // 2

    def rotated(b):
        x = p_ref[:, b * LANE:(b + 1) * LANE]
        rot = jnp.where(first_half, pltpu.roll(x, LANE - DA_HEAD_DIM // 2, 1), pltpu.roll(x, DA_HEAD_DIM // 2, 1))
        return x * cos + rot * sin

    ones = jnp.ones((FLASH_ONES_ROWS, cos.shape[0]), BF16)
    for h in range(DA_HEADS):
        qt_ref[h * hw:(h + 1) * hw, :] = (rotated(h) * (LOG2_E * DA_HEAD_DIM ** -0.5)).T.astype(BF16)
        k_ref[:, h * hw:(h + 1) * hw] = rotated(DA_HEADS + h).astype(BF16)
        base = h * (hw + FLASH_ONES_ROWS)
        vt_ref[base:base + hw, :] = p_ref[:, (2 * DA_HEADS + h) * LANE:(2 * DA_HEADS + h + 1) * LANE].T.astype(BF16)
        vt_ref[base + hw:base + hw + FLASH_ONES_ROWS, :] = ones


def rope_da(p, cos, sin):
    n_rows = p.shape[0]
    assert 2 * DA_HEAD_DIM == LANE
    vt_rows = DA_HEADS * (LANE + FLASH_ONES_ROWS)
    return pl.pallas_call(
        _rope_da_kernel,
        grid=(n_rows // ROW_TILE,),
        in_specs=[pl.BlockSpec((ROW_TILE, 3 * DA_WIDTH), lambda i: (i, 0)),
                  pl.BlockSpec((ROW_TILE, LANE), lambda i: (i, 0)),
                  pl.BlockSpec((ROW_TILE, LANE), lambda i: (i, 0))],
        out_specs=[pl.BlockSpec((DA_WIDTH, ROW_TILE), lambda i: (0, i)),
                   pl.BlockSpec((ROW_TILE, DA_WIDTH), lambda i: (i, 0)),
                   pl.BlockSpec((vt_rows, ROW_TILE), lambda i: (0, i))],
        out_shape=[jax.ShapeDtypeStruct((DA_WIDTH, n_rows), BF16),
                   jax.ShapeDtypeStruct((n_rows, DA_WIDTH), BF16),
                   jax.ShapeDtypeStruct((vt_rows, n_rows), BF16)],
        compiler_params=_params(("parallel",)),
        name="rope_da",
    )(p, cos, sin)


def diff_attention(qt, k, vt, lam_full, subln, *, seq, ctx_len, lambda_init):
    n_rows = k.shape[0]
    tq = _token_tile(n_rows)
    hw = 2 * DA_HEAD_DIM
    ones_rows = FLASH_ONES_ROWS
    return pl.pallas_call(
        functools.partial(_flash_kernel, kv=tq, seq=seq, ctx_len=ctx_len, out_scale=1.0 - lambda_init),
        grid=(DA_HEADS, n_rows // tq),
        in_specs=[pl.BlockSpec(memory_space=pltpu.SMEM),
                  pl.BlockSpec((hw, tq), lambda h, i: (h, i)),
                  pl.BlockSpec((n_rows, hw), lambda h, i: (0, h)),
                  pl.BlockSpec((hw + ones_rows, n_rows), lambda h, i: (h, 0)),
                  pl.BlockSpec((1, hw), lambda h, i: (0, 0))],
        out_specs=pl.BlockSpec((tq, hw), lambda h, i: (i, h)),
        out_shape=jax.ShapeDtypeStruct((n_rows, DA_WIDTH), BF16),
        scratch_shapes=[pltpu.VMEM((2, 1, tq), F32), pltpu.VMEM((2, hw + ones_rows, tq), F32)],
        compiler_params=pltpu.CompilerParams(dimension_semantics=("parallel", "parallel"),
                                             vmem_limit_bytes=FLASH_VMEM_LIMIT),
        name="diff_attention",
    )(lam_full.reshape(1), qt, k, vt, subln.reshape(1, hw))


def _route_kernel(lg_ref, b_ref, tri_ref, eidx_ref, w_ref, rank_ref, cnt_ref, carry_ref):
    t = lg_ref.shape[1]

    @pl.when(pl.program_id(0) == 0)
    def _():
        carry_ref[...] = jnp.zeros_like(carry_ref)

    scores = jax.nn.sigmoid(lg_ref[...])
    choice = (scores + b_ref[...]).reshape(N_GROUPS, GROUP_SIZE, t)
    s3 = scores.reshape(N_GROUPS, GROUP_SIZE, t)
    member = lax.broadcasted_iota(jnp.int32, choice.shape, 1)
    group = lax.broadcasted_iota(jnp.int32, (N_GROUPS, 1, t), 0)
    expert = lax.broadcasted_iota(jnp.int32, choice.shape, 0) * GROUP_SIZE + member
    neg_inf = -jnp.inf
    m1 = jnp.max(choice, axis=1, keepdims=True)
    first = jnp.min(jnp.where(choice == m1, member, GROUP_SIZE), axis=1, keepdims=True)
    m2 = jnp.max(jnp.where(member == first, neg_inf, choice), axis=1, keepdims=True)
    gscore = m1 + m2
    gsel = jnp.zeros(gscore.shape, F32)
    for _ in range(TOPK_GROUPS):
        m = jnp.max(gscore, axis=0, keepdims=True)
        f = jnp.min(jnp.where(gscore == m, group, N_GROUPS), axis=0, keepdims=True)
        hit = group == f
        gsel = jnp.where(hit, 1.0, gsel)
        gscore = jnp.where(hit, neg_inf, gscore)
    cand = jnp.where(gsel > 0.0, choice, neg_inf)
    esel = jnp.zeros(choice.shape, F32)
    picks = []
    for _ in range(TOP_K):
        m = jnp.max(jnp.max(cand, axis=1, keepdims=True), axis=0, keepdims=True)
        f = jnp.min(jnp.min(jnp.where(cand == m, expert, N_EXPERTS), axis=1, keepdims=True), axis=0, keepdims=True)
        hit = expert == f
        esel = jnp.where(hit, 1.0, esel)
        cand = jnp.where(hit, neg_inf, cand)
        picks.append(f)
    w = s3 * esel
    denom = jnp.sum(jnp.sum(w, axis=1, keepdims=True), axis=0, keepdims=True) + 1e-20
    w = w / denom * ROUTED_SCALE
    sel = esel.reshape(N_EXPERTS, t)
    before = jnp.dot(sel.astype(BF16), tri_ref[...], preferred_element_type=F32) + carry_ref[...]
    before = before.reshape(N_GROUPS, GROUP_SIZE, t)
    pick = lambda a, hit: jnp.sum(jnp.sum(jnp.where(hit, a, 0.0), axis=1, keepdims=True), axis=0).reshape(1, t)
    for k, f in enumerate(picks):
        hit = expert == f
        eidx_ref[k:k + 1, :] = f.reshape(1, t)
        w_ref[k:k + 1, :] = pick(w, hit)
        rank_ref[k:k + 1, :] = pick(before, hit).astype(jnp.int32)
    carry_ref[...] += jnp.sum(sel, axis=1, keepdims=True)
    cnt_ref[...] = carry_ref[...]


def route(logits_t, bias):
    n = logits_t.shape[1]
    t = _token_tile(n)
    tri = (jnp.arange(t)[:, None] < jnp.arange(t)[None, :]).astype(BF16)
    tok = lambda dt: jax.ShapeDtypeStruct((TOP_K, n), dt)
    tok_spec = pl.BlockSpec((TOP_K, t), lambda i: (0, i))
    return pl.pallas_call(
        _route_kernel,
        grid=(n // t,),
        in_specs=[pl.BlockSpec((N_EXPERTS, t), lambda i: (0, i)), pl.BlockSpec((N_EXPERTS, 1), lambda i: (0, 0)),
                  pl.BlockSpec((t, t), lambda i: (0, 0))],
        out_specs=[tok_spec, tok_spec, tok_spec, pl.BlockSpec((N_EXPERTS, 1), lambda i: (0, 0))],
        out_shape=[tok(jnp.int32), tok(F32), tok(jnp.int32), jax.ShapeDtypeStruct((N_EXPERTS, 1), F32)],
        scratch_shapes=[pltpu.VMEM((N_EXPERTS, 1), F32)],
        compiler_params=_params(("arbitrary",)),
        name="route",
    )(logits_t, bias.reshape(N_EXPERTS, 1), tri)


def _slot_kernel(start_ref, eidx_ref, rank_ref, dest_ref):
    e = eidx_ref[...]
    d = rank_ref[...]
    for x in range(N_EXPERTS):
        d = d + jnp.where(e == x, start_ref[x], 0)
    dest_ref[...] = d


def slot_index(pad_start, eidx, rank):
    n = eidx.shape[1]
    t = _token_tile(n)
    spec = pl.BlockSpec((TOP_K, t), lambda i: (0, i))
    return pl.pallas_call(
        _slot_kernel,
        grid=(n // t,),
        in_specs=[pl.BlockSpec(memory_space=pltpu.SMEM), spec, spec],
        out_specs=spec,
        out_shape=jax.ShapeDtypeStruct((TOP_K, n), jnp.int32),
        compiler_params=_params(("parallel",)),
        name="slot_index",
    )(pad_start, eidx, rank)


def _sc_worker():
    return lax.axis_index("s") * SC_CORES + lax.axis_index("c")


def sc_dispatch(h, dest3, n_slots):
    n, d = h.shape
    n_win = n // SC_WINDOW
    mesh = plsc.VectorSubcoreMesh(core_axis_name="c", subcore_axis_name="s")

    @functools.partial(
        pl.kernel, mesh=mesh, out_type=jax.ShapeDtypeStruct((n_slots, d), h.dtype),
        scratch_types=[pltpu.VMEM((TOP_K, SC_WINDOW), jnp.int32), pltpu.VMEM((SC_WINDOW, d), h.dtype),
                       pltpu.SemaphoreType.DMA])
    def k(h_hbm, dest_hbm, out_hbm, idx_v, rows_v, sem):
        wid = _sc_worker()

        @pl.loop(0, -(-n_win // SC_WORKERS))
        def _(it):
            w = it * SC_WORKERS + wid

            @pl.when(w < n_win)
            def _():
                pltpu.sync_copy(dest_hbm.at[w], idx_v)
                pltpu.sync_copy(h_hbm.at[pl.ds(w * SC_WINDOW, SC_WINDOW)], rows_v)
                copies = [pltpu.async_copy(rows_v, out_hbm.at[idx_v.at[j]], sem) for j in range(TOP_K)]
                for c in copies:
                    c.wait()

    return k(h, dest3)


def sc_combine_gather(y, dest3):
    d = y.shape[1]
    n_win = dest3.shape[0]
    n = n_win * SC_WINDOW
    mesh = plsc.VectorSubcoreMesh(core_axis_name="c", subcore_axis_name="s")

    @functools.partial(
        pl.kernel, mesh=mesh, out_type=jax.ShapeDtypeStruct((TOP_K, n, d), y.dtype),
        scratch_types=[pltpu.VMEM((TOP_K, SC_WINDOW), jnp.int32), pltpu.VMEM((SC_WINDOW, d), y.dtype),
                       pltpu.SemaphoreType.DMA])
    def k(y_hbm, dest_hbm, out_hbm, idx_v, rows_v, sem):
        wid = _sc_worker()

        @pl.loop(0, -(-n_win // SC_WORKERS))
        def _(it):
            w = it * SC_WORKERS + wid

            @pl.when(w < n_win)
            def _():
                pltpu.sync_copy(dest_hbm.at[w], idx_v)
                for j in range(TOP_K):
                    pltpu.async_copy(y_hbm.at[idx_v.at[j]], rows_v, sem).wait()
                    pltpu.sync_copy(rows_v, out_hbm.at[j, pl.ds(w * SC_WINDOW, SC_WINDOW)])

    return k(y, dest3)


def _expert_ffn_kernel(be_ref, bv_ref, x_ref, wg_ref, wu_ref, wd_ref, o_ref, wg_s, wu_s, wd_s):
    b = pl.program_id(0)
    valid = bv_ref[b]
    new_expert = (b == 0) | (be_ref[b] != be_ref[jnp.maximum(b - 1, 0)])

    @pl.when(new_expert)
    def _():
        wg_s[...] = wg_ref[0].astype(BF16)
        wu_s[...] = wu_ref[0].astype(BF16)
        wd_s[...] = wd_ref[0].astype(BF16)

    @pl.when(valid > 0)
    def _():
        sub = x_ref.shape[0] // MOE_SUB_BLOCKS
        row = lax.broadcasted_iota(jnp.int32, (sub, 1), 0)
        for r in range(MOE_SUB_BLOCKS):
            rows = pl.ds(r * sub, sub)
            lo, hi = _unpack_bf16_pair(jnp.where(row + r * sub < valid, x_ref[rows, :], 0))
            x = jnp.concatenate([lo.astype(BF16), hi.astype(BF16)], axis=1)
            a = jnp.dot(x, wg_s[...], preferred_element_type=F32)
            a = a * jax.nn.sigmoid(a) * jnp.dot(x, wu_s[...], preferred_element_type=F32)
            y = jnp.dot(a.astype(BF16), wd_s[...], preferred_element_type=F32)
            half = y.shape[1] // 2
            o_ref[rows, :] = _pack_bf16_pair(y[:, :half], y[:, half:])


def expert_ffn(xg, block_expert, block_valid, wg, wu, wd):
    n_slots, dp = xg.shape
    d, f = wg.shape[-2:]
    grid_spec = pltpu.PrefetchScalarGridSpec(
        num_scalar_prefetch=2,
        grid=(n_slots // MOE_BLOCK,),
        in_specs=[pl.BlockSpec((MOE_BLOCK, dp), lambda b, be, bv: (b, 0)),
                  pl.BlockSpec((1, d, f), lambda b, be, bv: (be[b], 0, 0)),
                  pl.BlockSpec((1, d, f), lambda b, be, bv: (be[b], 0, 0)),
                  pl.BlockSpec((1, f, d), lambda b, be, bv: (be[b], 0, 0))],
        out_specs=pl.BlockSpec((MOE_BLOCK, dp), lambda b, be, bv: (b, 0)),
        scratch_shapes=[pltpu.VMEM((d, f), BF16), pltpu.VMEM((d, f), BF16), pltpu.VMEM((f, d), BF16)],
    )
    return pl.pallas_call(
        _expert_ffn_kernel,
        grid_spec=grid_spec,
        out_shape=jax.ShapeDtypeStruct((n_slots, dp), jnp.int32),
        compiler_params=_params(("arbitrary",)),
        name="expert_ffn",
    )(block_expert, block_valid, xg, wg, wu, wd)


def _combine_kernel(yg_ref, w_ref, h_ref, swg_ref, swu_ref, swd_ref, o_ref):
    h = h_ref[...]
    a = jnp.dot(h, swg_ref[...], preferred_element_type=F32)
    a = a * jax.nn.sigmoid(a) * jnp.dot(h, swu_ref[...], preferred_element_type=F32)
    acc = jnp.dot(a.astype(BF16), swd_ref[...], preferred_element_type=F32)
    half = acc.shape[1] // 2
    acc_lo, acc_hi = acc[:, :half], acc[:, half:]
    wt = w_ref[...].T
    for k in range(TOP_K):
        lo, hi = _unpack_bf16_pair(yg_ref[k])
        acc_lo = acc_lo + wt[:, k:k + 1] * lo
        acc_hi = acc_hi + wt[:, k:k + 1] * hi
    o_ref[:, :half] = acc_lo
    o_ref[:, half:] = acc_hi


def combine(yg, w, h, swg, swu, swd):
    n, d = h.shape
    f = swg.shape[-1]
    tm = ROW_TILE
    return pl.pallas_call(
        _combine_kernel,
        grid=(n // tm,),
        in_specs=[pl.BlockSpec((TOP_K, tm, d // 2), lambda i: (0, i, 0)),
                  pl.BlockSpec((TOP_K, tm), lambda i: (0, i)),
                  pl.BlockSpec((tm, d), lambda i: (i, 0)),
                  pl.BlockSpec((d, f), lambda i: (0, 0)),
                  pl.BlockSpec((d, f), lambda i: (0, 0)),
                  pl.BlockSpec((f, d), lambda i: (0, 0))],
        out_specs=pl.BlockSpec((tm, d), lambda i: (i, 0)),
        out_shape=jax.ShapeDtypeStruct((n, d), F32),
        compiler_params=_params(("parallel",)),
        name="moe_combine",
    )(yg, w, h, swg, swu, swd)


def moe(h, h_packed, logits_t, bias, wg, wu, wd, swg, swu, swd):
    n, d = h.shape
    eidx, w, rank, counts = route(logits_t, bias)
    counts = counts.reshape(N_EXPERTS).astype(jnp.int32)
    padded = (counts + MOE_BLOCK - 1) // MOE_BLOCK * MOE_BLOCK
    pad_end = jnp.cumsum(padded)
    pad_start = pad_end - padded
    n_slots = n * TOP_K + N_EXPERTS * MOE_BLOCK
    starts = jnp.arange(n_slots // MOE_BLOCK, dtype=jnp.int32) * MOE_BLOCK
    owner = jnp.sum((pad_end[None, :] <= starts[:, None]).astype(jnp.int32), axis=1)
    block_expert = jnp.minimum(owner, N_EXPERTS - 1)
    member = (block_expert[:, None] == jnp.arange(N_EXPERTS, dtype=jnp.int32)[None, :]).astype(jnp.int32)
    left = jnp.sum(member * (counts + pad_start)[None, :], axis=1) - starts
    block_valid = jnp.clip(left, 0, MOE_BLOCK).astype(jnp.int32)
    dest = slot_index(pad_start.astype(jnp.int32), eidx, rank)
    dest3 = dest.reshape(TOP_K, n // SC_WINDOW, SC_WINDOW).transpose(1, 0, 2)
    xg = sc_dispatch(h_packed, dest3, n_slots)
    y = expert_ffn(xg, block_expert, block_valid, wg, wu, wd)
    yg = sc_combine_gather(y, dest3)
    return combine(yg, w, h, swg.astype(BF16), swu.astype(BF16), swd.astype(BF16))


def mixer_ab(h, w_in, w_out, decay_logit, conv_w, conv_b, w1, b1, f1, w2, b2, f2, w3, skip, rope, dft, *, seq,
             ctx_len):
    n_rows = h.shape[0]
    tm = _token_tile(n_rows)
    p = mm([(h, w_in.astype(BF16))], F32, tm, 512, name="ab_in_proj")
    qkv_w = 2 * RET_QK + RET_V
    n_qk = 2 * RET_QK // LANE
    scales = (1.0,) * (RET_QK // LANE) + (RET_DK ** -0.5,) * (RET_QK // LANE) + (1.0,) * (RET_V // LANE)
    qkv = rope_cast(p, rope[0], rope[1], width=qkv_w, n_rot_blocks=n_qk, head_dim=RET_DK, scales=scales,
                    out_dtype=F32)
    log_g = jax.nn.log_sigmoid(decay_logit.astype(F32))
    ret = retention(qkv, p, log_g, jnp.exp(RET_CHUNK * log_g), seq=seq)
    v, x1, x2 = shortconv(p, conv_w, conv_b, seq=seq)
    filt = (w1, b1, f1, w2, b2, f2, w3)
    hy_x = long_conv_two_stage(dft["x"], hyena_filter_taps(seq, *filt), v[:seq], x1[:seq], x2[:seq], skip)
    hy_c = long_conv_one_stage(dft["c"], hyena_filter_taps(ctx_len, *filt), v[seq:], x1[seq:], x2[seq:], skip)
    hy = jnp.concatenate([hy_x, hy_c], axis=0)
    w_out = w_out.astype(BF16)
    return mm([(ret, w_out[:RET_V]), (hy, w_out[RET_V:])], F32, tm, 512, name="ab_out_proj")


def mixer_da(h, w_in, w_out, lam, subln, lambda_init, rope, *, seq, ctx_len):
    n_rows = h.shape[0]
    tm = _token_tile(n_rows)
    p = mm([(h, w_in.astype(BF16))], F32, tm, 512, name="da_in_proj")
    qt, k, vt = rope_da(p, rope[0], rope[1])
    lam_f = lam.astype(F32)
    lam_full = jnp.exp(jnp.sum(lam_f[0] * lam_f[1])) - jnp.exp(jnp.sum(lam_f[2] * lam_f[3])) + lambda_init
    o = diff_attention(qt, k, vt, lam_full, subln, seq=seq, ctx_len=ctx_len, lambda_init=lambda_init)
    return mm([(o, w_out.astype(BF16))], F32, tm, 512, name="da_out_proj")


def kernel(x, c, ctx, c_ctx, w_ada, b_ada, norm_mix, norm_ffn, ab_w_in, ab_w_out, ret_decay_logit, hy_conv_w, hy_conv_b, hy_w1, hy_b1, hy_freq1, hy_w2, hy_b2, hy_freq2, hy_w3, hy_skip, da_w_in, da_w_out, da_lambda, da_subln, router_w, router_b, exp_w_gate, exp_w_up, exp_w_down, sh_w_gate, sh_w_up, sh_w_down, norm_final):
    batch, seq, d = x.shape
    ctx_len = ctx.shape[1]
    assert batch == 1 and seq % ROW_TILE == 0 and ctx_len == ROW_TILE
    depth = w_ada.shape[0]
    n_rows = seq + ctx_len
    ctx_tile = seq // ROW_TILE

    xs = jnp.concatenate([x[0], ctx[0]], axis=0)
    cv = jnp.zeros((SUBLANE, d), F32).at[0].set(c_ctx).at[1].set(c[0])
    mods = adaln(cv, w_ada, b_ada)[:, :2].reshape(depth, 2, 6, d)

    rope_ret = rope_tables(seq, ctx_len, RET_DK)
    rope_da = rope_tables(seq, ctx_len, DA_HEAD_DIM)
    dft = dict(x=dft_tables_two_stage(2 * seq), c=dft_tables_one_stage(2 * ctx_len))
    common = dict(n_rows=n_rows, ctx_tile=ctx_tile)

    delta, gate_mods = None, None
    for i in range(depth):
        j = i // 2
        if delta is None:
            (h,) = norm_mod(xs, norm_mix[i], mods=mods[i], shift_idx=0, scale_idx=1, **common)
        else:
            xs, h = norm_mod(xs, norm_mix[i], delta=delta, gate_mods=gate_mods, gate_idx=5, mods=mods[i], shift_idx=0,
                             scale_idx=1, **common)
        if i % 2 == 0:
            y = mixer_ab(h, ab_w_in[j], ab_w_out[j], ret_decay_logit[j], hy_conv_w[j], hy_conv_b[j], hy_w1[j],
                         hy_b1[j], hy_freq1[j], hy_w2[j], hy_b2[j], hy_freq2[j], hy_w3[j], hy_skip[j], rope_ret, dft,
                         seq=seq, ctx_len=ctx_len)
        else:
            lambda_init = 0.8 - 0.6 * math.exp(-0.3 * i)
            y = mixer_da(h, da_w_in[j], da_w_out[j], da_lambda[j], da_subln[j], lambda_init, rope_da, seq=seq,
                         ctx_len=ctx_len)
        xs, h, logits_t, h_packed = norm_mod(xs, norm_ffn[i], delta=y, gate_mods=mods[i], gate_idx=2, mods=mods[i],
                                             shift_idx=3, scale_idx=4, router_wt=router_w[i].T, **common)
        delta = moe(h, h_packed, logits_t, router_b[i], exp_w_gate[i], exp_w_up[i], exp_w_down[i],
                    sh_w_gate[i], sh_w_up[i], sh_w_down[i])
        gate_mods = mods[i]
    _, out = norm_mod(xs, norm_final, n_rows=seq, ctx_tile=ctx_tile, delta=delta, gate_mods=gate_mods, gate_idx=5,
                      out_dtype=F32)
    return out[None]
```

```python
import functools
import math

import jax
import jax.numpy as jnp
from jax import lax
from jax.experimental import pallas as pl
from jax.experimental.pallas import tpu as pltpu
from jax.experimental.pallas import tpu_sc as plsc

F32 = jnp.float32
BF16 = jnp.bfloat16
HIGHEST = lax.Precision.HIGHEST

D_MODEL = 1024
DEPTH = 4
GRID_W = 64
EPS = 1e-6
ROPE_BASE = 10000.0

RET_HEADS = 4
RET_DK = 128
RET_DV = 256
RET_CHUNK = 128
RET_QK = RET_HEADS * RET_DK
RET_V = RET_HEADS * RET_DV

HY_WIDTH = 512
HY_ORDER = 2
HY_BANDS = 16
HY_EMB = 2 * HY_BANDS + 1
HY_FFN = 64
HY_DECAY_TARGET = 1e-2
HY_FAST_DECAY = 0.3
HY_SLOW_DECAY = 1.5
HY_ZCOLS = 64
HY_VALID_COL = HY_EMB
FFT_N2 = 128

AB_IN = 2 * RET_QK + 2 * RET_V + (HY_ORDER + 1) * HY_WIDTH
AB_CAT = RET_V + HY_WIDTH

DA_HEADS = 8
DA_HEAD_DIM = 64
DA_WIDTH = DA_HEADS * 2 * DA_HEAD_DIM

N_EXPERTS = 64
TOP_K = 8
N_GROUPS = 8
TOPK_GROUPS = 4
GROUP_SIZE = N_EXPERTS // N_GROUPS
EXPERT_DIM = 256
ROUTED_SCALE = 2.5
MOE_BLOCK = 512
MOE_SUB_BLOCKS = 2
SC_CORES = 2
SC_SUBCORES = 16
SC_WORKERS = SC_CORES * SC_SUBCORES
SC_WINDOW = 64

LANE = 128
SUBLANE = 8
ROW_TILE = 256
MAX_TOKEN_TILE = 1280
VMEM_LIMIT = 48 * 1024 * 1024
FLASH_VMEM_LIMIT = 56 * 1024 * 1024
NEG_BIG = -1e30
LOG2_E = 1.4426950408889634
FLASH_ONES_ROWS = 16
FLASH_INIT_KEYS = 16
FLASH_LAZY_HEADROOM = 60.0


def _params(sem):
    return pltpu.CompilerParams(dimension_semantics=sem, vmem_limit_bytes=VMEM_LIMIT)


def _token_tile(n):
    best = ROW_TILE
    t = ROW_TILE
    while t <= min(n, MAX_TOKEN_TILE):
        if n % t == 0:
            best = t
        t += ROW_TILE
    return best


def _mm_kernel(*refs, n_pairs, has_epi):
    acc = None
    for p in range(n_pairs):
        a = refs[2 * p][...].astype(BF16)
        b = refs[2 * p + 1][...].astype(BF16)
        d = jnp.dot(a, b, preferred_element_type=F32)
        acc = d if acc is None else acc + d
    idx = 2 * n_pairs
    if has_epi:
        acc = refs[idx][...] * (acc + refs[idx + 1][...] * refs[idx + 2][...])
        idx += 3
    o_ref = refs[idx]
    o_ref[...] = acc.astype(o_ref.dtype)


def mm(pairs, out_dtype, tm, tn, epi=None, name="mm"):
    m = pairs[0][0].shape[0]
    n = pairs[0][1].shape[1]
    assert m % tm == 0 and n % tn == 0
    in_specs, args = [], []
    for a, b in pairs:
        k = a.shape[1]
        in_specs += [pl.BlockSpec((tm, k), lambda i, j: (i, 0)), pl.BlockSpec((k, tn), lambda i, j: (0, j))]
        args += [a, b]
    if epi is not None:
        in_specs += [pl.BlockSpec((tm, tn), lambda i, j: (i, j)), pl.BlockSpec((1, tn), lambda i, j: (0, j)),
                     pl.BlockSpec((tm, tn), lambda i, j: (i, j))]
        args += list(epi)
    return pl.pallas_call(
        functools.partial(_mm_kernel, n_pairs=len(pairs), has_epi=epi is not None),
        grid=(m // tm, n // tn),
        in_specs=in_specs,
        out_specs=pl.BlockSpec((tm, tn), lambda i, j: (i, j)),
        out_shape=jax.ShapeDtypeStruct((m, n), out_dtype),
        compiler_params=_params(("parallel", "parallel")),
        name=name,
    )(*args)


def _adaln_kernel(cv_ref, w_ref, b_ref, o_ref):
    cv = cv_ref[...]
    s = cv * jax.nn.sigmoid(cv)
    o_ref[0] = jnp.dot(s, w_ref[0], precision=HIGHEST, preferred_element_type=F32) + b_ref[0]


def adaln(cv, w_ada, b_ada):
    depth, d, n = w_ada.shape
    tn = 1536
    return pl.pallas_call(
        _adaln_kernel,
        grid=(depth, n // tn),
        in_specs=[pl.BlockSpec((SUBLANE, d), lambda l, j: (0, 0)),
                  pl.BlockSpec((1, d, tn), lambda l, j: (l, 0, j)),
                  pl.BlockSpec((1, 1, tn), lambda l, j: (l, 0, j))],
        out_specs=pl.BlockSpec((1, SUBLANE, tn), lambda l, j: (l, 0, j)),
        out_shape=jax.ShapeDtypeStruct((depth, SUBLANE, n), F32),
        compiler_params=_params(("parallel", "parallel")),
        name="adaln",
    )(cv, w_ada, b_ada.reshape(depth, 1, n))


def _norm_mod_kernel(*refs, has_delta, gate_idx, shift_idx, scale_idx, has_router, write_xs):
    it = iter(refs)
    xs_ref = next(it)
    x = xs_ref[...]
    if has_delta:
        delta_ref = next(it)
        gmods_ref = next(it)
        x = x + gmods_ref[0, gate_idx:gate_idx + 1, :] * delta_ref[...]
    mods_ref = next(it) if shift_idx is not None else None
    g_ref = next(it)
    wr_ref = next(it) if has_router else None
    if write_xs:
        next(it)[...] = x
    h_ref = next(it)
    y = x * lax.rsqrt(jnp.mean(x * x, axis=-1, keepdims=True) + EPS) * g_ref[...]
    if shift_idx is not None:
        y = y * (1.0 + mods_ref[0, scale_idx:scale_idx + 1, :]) + mods_ref[0, shift_idx:shift_idx + 1, :]
    h_ref[...] = y.astype(h_ref.dtype)
    if has_router:
        lg_ref = next(it)
        lg_ref[...] = lax.dot_general(wr_ref[...], y, (((1,), (1,)), ((), ())),
                                      precision=HIGHEST, preferred_element_type=F32)
        half = y.shape[1] // 2
        next(it)[...] = _pack_bf16_pair(y[:, :half], y[:, half:])


def _pack_bf16_pair(a, b):
    def rounded(x):
        u = lax.bitcast_convert_type(x, jnp.int32)
        return u + 0x7FFF + (lax.shift_right_logical(u, 16) & 1)
    return lax.shift_right_logical(rounded(a), 16) | (rounded(b) & -65536)


def _unpack_bf16_pair(w):
    return (lax.bitcast_convert_type(lax.shift_left(w, 16), F32),
            lax.bitcast_convert_type(w & -65536, F32))


def norm_mod(xs, g, *, n_rows, ctx_tile, delta=None, gate_mods=None, gate_idx=None, mods=None, shift_idx=None,
             scale_idx=None, router_wt=None, out_dtype=BF16):
    d = xs.shape[1]
    n_tiles = n_rows // ROW_TILE
    row = pl.BlockSpec((ROW_TILE, d), lambda i: (i, 0))
    mod_spec = pl.BlockSpec((1, 6, d), lambda i: (jnp.where(i == ctx_tile, 0, 1), 0, 0))
    in_specs, args = [row], [xs]
    has_delta = delta is not None
    if has_delta:
        in_specs += [row, mod_spec]
        args += [delta, gate_mods]
    if shift_idx is not None:
        in_specs.append(mod_spec)
        args.append(mods)
    in_specs.append(pl.BlockSpec((1, d), lambda i: (0, 0)))
    args.append(g.reshape(1, d))
    has_router = router_wt is not None
    if has_router:
        in_specs.append(pl.BlockSpec(router_wt.shape, lambda i: (0, 0)))
        args.append(router_wt)
    out_specs, out_shape = [], []
    if has_delta:
        out_specs.append(row)
        out_shape.append(jax.ShapeDtypeStruct((n_rows, d), F32))
    out_specs.append(row)
    out_shape.append(jax.ShapeDtypeStruct((n_rows, d), out_dtype))
    if has_router:
        out_specs.append(pl.BlockSpec((N_EXPERTS, ROW_TILE), lambda i: (0, i)))
        out_shape.append(jax.ShapeDtypeStruct((N_EXPERTS, n_rows), F32))
        out_specs.append(pl.BlockSpec((ROW_TILE, d // 2), lambda i: (i, 0)))
        out_shape.append(jax.ShapeDtypeStruct((n_rows, d // 2), jnp.int32))
    return pl.pallas_call(
        functools.partial(_norm_mod_kernel, has_delta=has_delta, gate_idx=gate_idx, shift_idx=shift_idx,
                          scale_idx=scale_idx, has_router=has_router, write_xs=has_delta),
        grid=(n_tiles,),
        in_specs=in_specs,
        out_specs=out_specs,
        out_shape=out_shape,
        compiler_params=_params(("parallel",)),
        name="norm_mod",
    )(*args)


def _rope_kernel(p_ref, cos_ref, sin_ref, o_ref, *, n_rot_blocks, head_dim, scales):
    cos = cos_ref[...]
    sin = sin_ref[...]
    for b in range(len(scales)):
        x = p_ref[:, b * LANE:(b + 1) * LANE]
        if b < n_rot_blocks:
            if head_dim == LANE:
                rot = pltpu.roll(x, LANE // 2, 1)
            else:
                lane = lax.broadcasted_iota(jnp.int32, x.shape, 1)
                first_half = (lane % head_dim) < head_dim // 2
                rot = jnp.where(first_half, pltpu.roll(x, LANE - head_dim // 2, 1), pltpu.roll(x, head_dim // 2, 1))
            x = x * cos + rot * sin
        if scales[b] != 1.0:
            x = x * scales[b]
        o_ref[:, b * LANE:(b + 1) * LANE] = x.astype(o_ref.dtype)


def rope_cast(p, cos, sin, *, width, n_rot_blocks, head_dim, scales, out_dtype):
    n_rows = p.shape[0]
    return pl.pallas_call(
        functools.partial(_rope_kernel, n_rot_blocks=n_rot_blocks, head_dim=head_dim, scales=scales),
        grid=(n_rows // ROW_TILE,),
        in_specs=[pl.BlockSpec((ROW_TILE, width), lambda i: (i, 0)),
                  pl.BlockSpec((ROW_TILE, LANE), lambda i: (i, 0)),
                  pl.BlockSpec((ROW_TILE, LANE), lambda i: (i, 0))],
        out_specs=pl.BlockSpec((ROW_TILE, width), lambda i: (i, 0)),
        out_shape=jax.ShapeDtypeStruct((n_rows, width), out_dtype),
        compiler_params=_params(("parallel",)),
        name="rope_cast",
    )(p, cos, sin)


def rope_tables(seq, ctx_len, head_dim):
    n_freq = head_dim // 4
    inv = ROPE_BASE ** (-jnp.arange(n_freq, dtype=F32) / n_freq)
    rows = seq // GRID_W
    row = jnp.repeat(jnp.arange(rows, dtype=F32), GRID_W)
    col = jnp.tile(jnp.arange(GRID_W, dtype=F32), rows)
    ang = jnp.concatenate([row[:, None] * inv, col[:, None] * inv], axis=-1)
    cos, sin = jnp.cos(ang), jnp.sin(ang)
    cos = jnp.concatenate([cos, cos], axis=-1)
    sin = jnp.concatenate([-sin, sin], axis=-1)
    reps = LANE // head_dim
    cos, sin = jnp.tile(cos, (1, reps)), jnp.tile(sin, (1, reps))
    cos = jnp.concatenate([cos, jnp.ones((ctx_len, LANE), F32)], axis=0)
    sin = jnp.concatenate([sin, jnp.zeros((ctx_len, LANE), F32)], axis=0)
    return cos, sin


def _ret_kernel(lg_ref, gc_ref, q_ref, k_ref, v_ref, *rest, reverse):
    if reverse:
        yf_ref, gate_ref, o_ref, s_ref = rest
    else:
        o_ref, s_ref = rest
    c = RET_CHUNK

    @pl.when(pl.program_id(0) == 0)
    def _():
        s_ref[...] = jnp.zeros_like(s_ref)

    ii = lax.broadcasted_iota(jnp.int32, (c, c), 0)
    jj = lax.broadcasted_iota(jnp.int32, (c, c), 1)
    rel = ((jj - ii) if reverse else (ii - jj)).astype(F32)
    pos = lax.broadcasted_iota(jnp.int32, (c, 1), 0).astype(F32)
    for h in range(RET_HEADS):
        lg = lg_ref[h]
        dec = jnp.where(rel >= 0, jnp.exp(jnp.maximum(rel, 0.0) * lg), 0.0)
        if reverse:
            q_dec = jnp.exp((c - pos) * lg)
            k_dec = jnp.exp(pos * lg)
        else:
            q_dec = jnp.exp((pos + 1.0) * lg)
            k_dec = jnp.exp((c - 1.0 - pos) * lg)
        q = q_ref[:, h * RET_DK:(h + 1) * RET_DK]
        k = k_ref[:, h * RET_DK:(h + 1) * RET_DK]
        v = v_ref[:, h * RET_DV:(h + 1) * RET_DV].astype(BF16)
        s = lax.dot_general(q.astype(BF16), k.astype(BF16), (((1,), (1,)), ((), ())),
                            preferred_element_type=F32) * dec
        state = s_ref[h]
        y = jnp.dot(s.astype(BF16), v, preferred_element_type=F32)
        y = y + jnp.dot((q * q_dec).astype(BF16), state.astype(BF16), preferred_element_type=F32)
        upd = lax.dot_general((k * k_dec).astype(BF16), v, (((0,), (0,)), ((), ())), preferred_element_type=F32)
        s_ref[h] = gc_ref[h] * state + upd
        if reverse:
            r = y + yf_ref[:, h * RET_DV:(h + 1) * RET_DV]
            mu = jnp.mean(r, axis=-1, keepdims=True)
            rc = r - mu
            var = jnp.mean(rc * rc, axis=-1, keepdims=True)
            g = gate_ref[:, h * RET_DV:(h + 1) * RET_DV]
            o_ref[:, h * RET_DV:(h + 1) * RET_DV] = (rc * lax.rsqrt(var + EPS) * (g * jax.nn.sigmoid(g))).astype(
                o_ref.dtype)
        else:
            o_ref[:, h * RET_DV:(h + 1) * RET_DV] = y


def retention(qkv, p, log_g, g_chunk, *, seq):
    n_rows = qkv.shape[0]
    n_chunks = n_rows // RET_CHUNK
    n_x = seq // RET_CHUNK
    smem = pl.BlockSpec(memory_space=pltpu.SMEM)

    def run(reverse, extra):
        if reverse:
            idx = lambda t: n_chunks - 1 - t
        else:
            idx = lambda t: (t + n_x) % n_chunks
        in_specs = [smem, smem,
                    pl.BlockSpec((RET_CHUNK, RET_QK), lambda t: (idx(t), 0)),
                    pl.BlockSpec((RET_CHUNK, RET_QK), lambda t: (idx(t), 1)),
                    pl.BlockSpec((RET_CHUNK, RET_V), lambda t: (idx(t), 1))]
        args = [log_g[1 if reverse else 0], g_chunk[1 if reverse else 0], qkv, qkv, qkv]
        if reverse:
            in_specs += [pl.BlockSpec((RET_CHUNK, RET_V), lambda t: (idx(t), 0)),
                         pl.BlockSpec((RET_CHUNK, RET_V), lambda t: (idx(t), 2))]
            args += list(extra)
        return pl.pallas_call(
            functools.partial(_ret_kernel, reverse=reverse),
            grid=(n_chunks,),
            in_specs=in_specs,
            out_specs=pl.BlockSpec((RET_CHUNK, RET_V), lambda t: (idx(t), 0)),
            out_shape=jax.ShapeDtypeStruct((n_rows, RET_V), BF16 if reverse else F32),
            scratch_shapes=[pltpu.VMEM((RET_HEADS, RET_DK, RET_DV), F32)],
            compiler_params=_params(("arbitrary",)),
            name="retention_bwd" if reverse else "retention_fwd",
        )(*args)

    y_fwd = run(False, None)
    return run(True, (y_fwd, p))


def _shortconv_kernel(cur_ref, prev_ref, next_ref, w_ref, b_ref, v_ref, x1_ref, x2_ref, *, x_tiles):
    i = pl.program_id(0)
    cur = cur_ref[...]
    rows = cur.shape[0]
    row = lax.broadcasted_iota(jnp.int32, (rows, 1), 0)
    has_prev = jnp.where((i == 0) | (i == x_tiles), 0.0, 1.0)
    has_next = jnp.where((i == x_tiles - 1) | (i == x_tiles), 0.0, 1.0)
    up = jnp.where(row == 0, prev_ref[SUBLANE - 1:SUBLANE, :] * has_prev, pltpu.roll(cur, 1, 0))
    dn = jnp.where(row == rows - 1, next_ref[0:1, :] * has_next, pltpu.roll(cur, rows - 1, 0))
    y = up * w_ref[0:1, :] + cur * w_ref[1:2, :] + dn * w_ref[2:3, :] + b_ref[...]
    v_ref[...] = y[:, :HY_WIDTH]
    x1_ref[...] = y[:, HY_WIDTH:2 * HY_WIDTH]
    x2_ref[...] = y[:, 2 * HY_WIDTH:]


def shortconv(p, w, b, *, seq):
    n_rows = p.shape[0]
    width = 3 * HY_WIDTH
    col = p.shape[1] // width - 1
    per = ROW_TILE // SUBLANE
    last = n_rows // SUBLANE - 1
    out = jax.ShapeDtypeStruct((n_rows, HY_WIDTH), F32)
    ospec = pl.BlockSpec((ROW_TILE, HY_WIDTH), lambda i: (i, 0))
    return pl.pallas_call(
        functools.partial(_shortconv_kernel, x_tiles=seq // ROW_TILE),
        grid=(n_rows // ROW_TILE,),
        in_specs=[pl.BlockSpec((ROW_TILE, width), lambda i: (i, col)),
                  pl.BlockSpec((SUBLANE, width), lambda i: (jnp.maximum(i * per - 1, 0), col)),
                  pl.BlockSpec((SUBLANE, width), lambda i: (jnp.minimum((i + 1) * per, last), col)),
                  pl.BlockSpec((3, width), lambda i: (0, 0)),
                  pl.BlockSpec((1, width), lambda i: (0, 0))],
        out_specs=[ospec, ospec, ospec],
        out_shape=[out, out, out],
        compiler_params=_params(("parallel",)),
        name="shortconv",
    )(p, p, p, w, b.reshape(1, width))


def _filt_kernel(z_ref, w1_ref, b1_ref, f1_ref, w2_ref, b2_ref, f2_ref, w3a_ref, w3b_ref, dl_ref, o_ref):
    z = z_ref[...]
    h = jnp.sin(f1_ref[...] * (jnp.dot(z, w1_ref[...], precision=HIGHEST, preferred_element_type=F32) + b1_ref[...]))
    h = jnp.sin(f2_ref[...] * (jnp.dot(h, w2_ref[...], precision=HIGHEST, preferred_element_type=F32) + b2_ref[...]))
    window = jnp.exp(-z[:, 0:1] * dl_ref[...]) * z[:, HY_VALID_COL:HY_VALID_COL + 1]
    for o, w3_ref in enumerate((w3a_ref, w3b_ref)):
        o_ref[o] = jnp.dot(h, w3_ref[...], precision=HIGHEST, preferred_element_type=F32) * window


def hyena_filter_taps(length, w1, b1, f1, w2, b2, f2, w3):
    z = _filter_positions(length)
    w1p = jnp.zeros((HY_ZCOLS, HY_FFN), F32).at[:HY_EMB].set(w1)
    deltas = jnp.abs(jnp.linspace(math.log(HY_DECAY_TARGET) / HY_SLOW_DECAY,
                                  math.log(HY_DECAY_TARGET) / HY_FAST_DECAY, HY_WIDTH, dtype=F32)).reshape(1, HY_WIDTH)
    tm = min(length, 512)
    half_tiles = length // tm
    vec = lambda a: a.reshape(1, HY_FFN)
    small = lambda shape: pl.BlockSpec(shape, lambda i: (0, 0))
    w3_spec = lambda o: pl.BlockSpec((HY_FFN, HY_WIDTH), lambda i: (0, 2 * o + jnp.where(i >= half_tiles, 1, 0)))
    assert HY_ORDER == 2
    return pl.pallas_call(
        _filt_kernel,
        grid=(2 * half_tiles,),
        in_specs=[pl.BlockSpec((tm, HY_ZCOLS), lambda i: (i, 0)),
                  small((HY_ZCOLS, HY_FFN)), small((1, HY_FFN)), small((1, HY_FFN)),
                  small((HY_FFN, HY_FFN)), small((1, HY_FFN)), small((1, HY_FFN)),
                  w3_spec(0), w3_spec(1), small((1, HY_WIDTH))],
        out_specs=pl.BlockSpec((HY_ORDER, tm, HY_WIDTH), lambda i: (0, i, 0)),
        out_shape=jax.ShapeDtypeStruct((HY_ORDER, 2 * length, HY_WIDTH), F32),
        compiler_params=_params(("parallel",)),
        name="hyena_filter",
    )(z, w1p, vec(b1), vec(f1), w2, vec(b2), vec(f2), w3, w3, deltas)


def _filter_positions(length):
    t = jnp.concatenate([jnp.arange(length, dtype=F32), float(length) - jnp.arange(length, dtype=F32)])
    valid = jnp.ones((2 * length,), F32).at[length].set(0.0)
    t_norm = t / max(length - 1, 1)
    bands = jnp.linspace(1e-4, HY_BANDS - 1, HY_BANDS, dtype=F32)
    ang = (2.0 * math.pi / length) * t[:, None] * bands[None, :]
    z = jnp.concatenate([t_norm[:, None], jnp.cos(ang), -jnp.sin(ang), valid[:, None]], axis=-1)
    return jnp.pad(z, ((0, 0), (0, HY_ZCOLS - z.shape[1])))


def _angles(num, den):
    return (2.0 * math.pi / den) * (num % den).astype(F32)


def dft_tables_two_stage(m):
    n2 = FFT_N2
    n1 = m // n2
    half = n1 // 2
    kp = -(-(half + 1) // SUBLANE) * SUBLANE
    k1 = jnp.arange(kp, dtype=jnp.int32)
    live = (k1 <= half)
    a1 = _angles(k1[:, None] * jnp.arange(n1, dtype=jnp.int32)[None, :], n1)
    f1 = jnp.concatenate([jnp.where(live[:, None], jnp.cos(a1), 0.0), jnp.where(live[:, None], -jnp.sin(a1), 0.0)], 0)
    wgt = jnp.where((k1 == 0) | (k1 == half), 1.0, 2.0) * live / m
    a1h = a1[:, :half].T
    cinv = jnp.concatenate([jnp.cos(a1h) * wgt[None, :], -jnp.sin(a1h) * wgt[None, :]], axis=1)
    k = k1[:, None, None] + n1 * jnp.arange(n2, dtype=jnp.int32)[None, :, None]
    th = _angles(k * jnp.arange(n2, dtype=jnp.int32)[None, None, :], m)
    c = jnp.where(live[:, None, None], jnp.cos(th), 0.0)
    s = jnp.where(live[:, None, None], jnp.sin(th), 0.0)
    g_fwd = jnp.concatenate([jnp.concatenate([c, s], 2), jnp.concatenate([-s, c], 2)], 1)
    ct, st = jnp.swapaxes(c, 1, 2), jnp.swapaxes(s, 1, 2)
    g_inv = jnp.concatenate([jnp.concatenate([ct, -st], 2), jnp.concatenate([st, ct], 2)], 1)
    return dict(n1=n1, kp=kp, f1=f1.astype(BF16), f1_half=f1[:, :half].astype(BF16), cinv=cinv.astype(BF16),
                g_fwd=g_fwd.astype(BF16), g_inv=g_inv.astype(BF16))


def dft_tables_one_stage(m):
    half = m // 2
    kp = -(-(half + 1) // SUBLANE) * SUBLANE
    k = jnp.arange(kp, dtype=jnp.int32)
    live = (k <= half)
    a = _angles(k[:, None] * jnp.arange(m, dtype=jnp.int32)[None, :], m)
    f = jnp.concatenate([jnp.where(live[:, None], jnp.cos(a), 0.0), jnp.where(live[:, None], -jnp.sin(a), 0.0)], 0)
    wgt = jnp.where((k == 0) | (k == half), 1.0, 2.0) * live / m
    ah = a[:, :half].T
    cinv = jnp.concatenate([jnp.cos(ah) * wgt[None, :], -jnp.sin(ah) * wgt[None, :]], axis=1)
    return dict(kp=kp, f=f.astype(BF16), f_half=f[:, :half].astype(BF16), cinv=cinv.astype(BF16))


def _bmm_kernel(*refs, kb, in_part_major, out_part_major, has_h):
    if has_h:
        g_ref, a_ref, h_ref, o_ref = refs
    else:
        g_ref, a_ref, o_ref = refs
    n2 = FFT_N2
    for b in range(kb):
        if in_part_major:
            ar, ai = a_ref[0, b], a_ref[1, b]
        else:
            ar, ai = a_ref[b, 0], a_ref[b, 1]
        if has_h:
            hr, hi = h_ref[b, 0], h_ref[b, 1]
            ar, ai = ar * hr - ai * hi, ar * hi + ai * hr
        xin = jnp.concatenate([ar, ai], axis=0).astype(BF16)
        y = jnp.dot(g_ref[b], xin, preferred_element_type=F32)
        if out_part_major:
            o_ref[0, b] = y[:n2]
            o_ref[1, b] = y[n2:]
        else:
            o_ref[b, 0] = y[:n2]
            o_ref[b, 1] = y[n2:]


def bmm_k1(g, a, h=None, *, in_part_major, out_part_major):
    kp = g.shape[0]
    n2 = FFT_N2
    c = a.shape[-1]
    kb, tc = SUBLANE, 256
    pm = lambda: pl.BlockSpec((2, kb, n2, tc), lambda i, j: (0, i, 0, j))
    km = lambda: pl.BlockSpec((kb, 2, n2, tc), lambda i, j: (i, 0, 0, j))
    in_specs = [pl.BlockSpec((kb, 2 * n2, 2 * n2), lambda i, j: (i, 0, 0)), pm() if in_part_major else km()]
    args = [g, a]
    if h is not None:
        in_specs.append(km())
        args.append(h)
    return pl.pallas_call(
        functools.partial(_bmm_kernel, kb=kb, in_part_major=in_part_major, out_part_major=out_part_major,
                          has_h=h is not None),
        grid=(kp // kb, c // tc),
        in_specs=in_specs,
        out_specs=pm() if out_part_major else km(),
        out_shape=jax.ShapeDtypeStruct((2, kp, n2, c) if out_part_major else (kp, 2, n2, c), F32),
        compiler_params=_params(("parallel", "parallel")),
        name="dft_inner",
    )(*args)


def _cmul_kernel(x_ref, h_ref, o_ref):
    xr, xi, hr, hi = x_ref[0], x_ref[1], h_ref[0], h_ref[1]
    o_ref[0] = xr * hr - xi * hi
    o_ref[1] = xr * hi + xi * hr


def cmul(x, h):
    spec = pl.BlockSpec(x.shape, lambda i: (0, 0, 0))
    return pl.pallas_call(_cmul_kernel, grid=(1,), in_specs=[spec, spec], out_specs=spec,
                          out_shape=jax.ShapeDtypeStruct(x.shape, F32), compiler_params=_params(("arbitrary",)),
                          name="spectrum_product")(x, h)


def long_conv_two_stage(tabs, taps, v, x1, x2, skip):
    length, c = v.shape
    n2, n1, kp = FFT_N2, tabs["n1"], tabs["kp"]
    cols = n2 * c
    tn = 2048

    def fwd(seq2d, full):
        a = mm([(tabs["f1"] if full else tabs["f1_half"], seq2d)], F32, 2 * kp, tn, name="dft_outer")
        return a.reshape(2, kp, n2, c)

    spectra = [bmm_k1(tabs["g_fwd"], fwd(taps[o].reshape(n1, cols), True), in_part_major=True, out_part_major=False)
               for o in range(HY_ORDER)]
    u = v
    for o, gate in enumerate((x1, x2)):
        xf = bmm_k1(tabs["g_fwd"], fwd(u.reshape(n1 // 2, cols), False), in_part_major=True, out_part_major=False)
        bt = bmm_k1(tabs["g_inv"], xf, spectra[o], in_part_major=False, out_part_major=True)
        skip_row = jnp.tile(skip[o].reshape(1, c), (1, n2))
        u = mm([(tabs["cinv"], bt.reshape(2 * kp, cols))], F32, n1 // 2, tn,
               epi=(gate.reshape(n1 // 2, cols), skip_row, u.reshape(n1 // 2, cols)), name="idft_outer_gate")
        u = u.reshape(length, c)
    return u


def long_conv_one_stage(tabs, taps, v, x1, x2, skip):
    length, c = v.shape
    kp = tabs["kp"]
    u = v
    for o, gate in enumerate((x1, x2)):
        hs = mm([(tabs["f"], taps[o])], F32, 2 * kp, c, name="ctx_dft").reshape(2, kp, c)
        xs = mm([(tabs["f_half"], u)], F32, 2 * kp, c, name="ctx_dft").reshape(2, kp, c)
        ys = cmul(xs, hs).reshape(2 * kp, c)
        u = mm([(tabs["cinv"], ys)], F32, length, c, epi=(gate, skip[o].reshape(1, c), u), name="ctx_idft_gate")
    return u


def _flash_kernel(lam_ref, qt_ref, k_ref, vt_ref, sub_ref, o_ref, m_ref, acc_ref, *, kv, seq, ctx_len, out_scale):
    i = pl.program_id(1)
    last_q = pl.num_programs(1) - 1
    tq = qt_ref.shape[1]
    d = DA_HEAD_DIM
    dv = 2 * DA_HEAD_DIM
    n_chunks = k_ref.shape[0] // kv
    acc_ref[...] = jnp.zeros_like(acc_ref)

    def scores(off, rows, c, masked):
        s = jnp.dot(k_ref[pl.ds(off, rows), c * d:(c + 1) * d], qt_ref[c * d:(c + 1) * d, :],
                    preferred_element_type=F32)
        if masked:
            key = off + lax.broadcasted_iota(jnp.int32, (rows, 1), 0)
            lane = lax.broadcasted_iota(jnp.int32, (1, tq), 1)
            s = s + jnp.where(key < seq, NEG_BIG, 0.0) * jnp.where(lane >= tq - ctx_len, 1.0, 0.0)
        return s

    def exact_step(off, c, masked):
        s = scores(off, kv, c, masked)
        m_old = m_ref[c]
        m_new = jnp.maximum(m_old, jnp.max(s, axis=0, keepdims=True))
        pr = jnp.exp2(s - m_new).astype(BF16)
        acc_ref[c] = jnp.exp2(m_old - m_new) * acc_ref[c] + jnp.dot(vt_ref[:, pl.ds(off, kv)], pr,
                                                                   preferred_element_type=F32)
        m_ref[c] = m_new

    def lazy_step(off, c, masked):
        s = scores(off, kv, c, masked)
        m_old = m_ref[c]
        m_chunk = jnp.max(s, axis=0, keepdims=True)
        pv = jnp.dot(vt_ref[:, pl.ds(off, kv)], jnp.exp2(s - m_old).astype(BF16), preferred_element_type=F32)
        safe = jnp.max(m_chunk - m_old) <= FLASH_LAZY_HEADROOM

        @pl.when(safe)
        def _():
            m_new = jnp.maximum(m_old, m_chunk)
            acc_ref[c] = jnp.exp2(m_old - m_new) * (acc_ref[c] + pv)
            m_ref[c] = m_new

        @pl.when(jnp.logical_not(safe))
        def _():
            exact_step(off, c, masked)

    def run(masked):
        for c in range(2):
            m_ref[c] = jnp.max(scores(0, FLASH_INIT_KEYS, c, masked), axis=0, keepdims=True)

        def body(kc, carry):
            off = pl.multiple_of(kc * kv, kv)
            for c in range(2):
                lazy_step(off, c, masked)
            return carry

        lax.fori_loop(0, n_chunks, body, 0)

    @pl.when(i != last_q)
    def _():
        run(False)

    @pl.when(i == last_q)
    def _():
        run(True)

    a0 = acc_ref[0, :dv, :] / acc_ref[0, dv:dv + 1, :]
    a1 = acc_ref[1, :dv, :] / acc_ref[1, dv:dv + 1, :]
    o = (a0 - lam_ref[0] * a1).T
    o = o * lax.rsqrt(jnp.mean(o * o, axis=-1, keepdims=True) + 1e-5) * sub_ref[...]
    o_ref[...] = (o * out_scale).astype(o_ref.dtype)


def _rope_da_kernel(p_ref, cos_ref, sin_ref, qt_ref, k_ref, vt_ref):
    cos = cos_ref[...]
    sin = sin_ref[...]
    hw = 2 * DA_HEAD_DIM
    lane = lax.broadcasted_iota(jnp.int32, cos.shape, 1)
    first_half = (lane % DA_HEAD_DIM) < DA_HEAD_DIM // 2

    def rotated(b):
        x = p_ref[:, b * LANE:(b + 1) * LANE]
        rot = jnp.where(first_half, pltpu.roll(x, LANE - DA_HEAD_DIM // 2, 1), pltpu.roll(x, DA_HEAD_DIM // 2, 1))
        return x * cos + rot * sin

    ones = jnp.ones((FLASH_ONES_ROWS, cos.shape[0]), BF16)
    for h in range(DA_HEADS):
        qt_ref[h * hw:(h + 1) * hw, :] = (rotated(h) * (LOG2_E * DA_HEAD_DIM ** -0.5)).T.astype(BF16)
        k_ref[:, h * hw:(h + 1) * hw] = rotated(DA_HEADS + h).astype(BF16)
        base = h * (hw + FLASH_ONES_ROWS)
        vt_ref[base:base + hw, :] = p_ref[:, (2 * DA_HEADS + h) * LANE:(2 * DA_HEADS + h + 1) * LANE].T.astype(BF16)
        vt_ref[base + hw:base + hw + FLASH_ONES_ROWS, :] = ones


def rope_da(p, cos, sin):
    n_rows = p.shape[0]
    assert 2 * DA_HEAD_DIM == LANE
    vt_rows = DA_HEADS * (LANE + FLASH_ONES_ROWS)
    return pl.pallas_call(
        _rope_da_kernel,
        grid=(n_rows // ROW_TILE,),
        in_specs=[pl.BlockSpec((ROW_TILE, 3 * DA_WIDTH), lambda i: (i, 0)),
                  pl.BlockSpec((ROW_TILE, LANE), lambda i: (i, 0)),
                  pl.BlockSpec((ROW_TILE, LANE), lambda i: (i, 0))],
        out_specs=[pl.BlockSpec((DA_WIDTH, ROW_TILE), lambda i: (0, i)),
                   pl.BlockSpec((ROW_TILE, DA_WIDTH), lambda i: (i, 0)),
                   pl.BlockSpec((vt_rows, ROW_TILE), lambda i: (0, i))],
        out_shape=[jax.ShapeDtypeStruct((DA_WIDTH, n_rows), BF16),
                   jax.ShapeDtypeStruct((n_rows, DA_WIDTH), BF16),
                   jax.ShapeDtypeStruct((vt_rows, n_rows), BF16)],
        compiler_params=_params(("parallel",)),
        name="rope_da",
    )(p, cos, sin)


def diff_attention(qt, k, vt, lam_full, subln, *, seq, ctx_len, lambda_init):
    n_rows = k.shape[0]
    tq = _token_tile(n_rows)
    hw = 2 * DA_HEAD_DIM
    ones_rows = FLASH_ONES_ROWS
    return pl.pallas_call(
        functools.partial(_flash_kernel, kv=tq, seq=seq, ctx_len=ctx_len, out_scale=1.0 - lambda_init),
        grid=(DA_HEADS, n_rows // tq),
        in_specs=[pl.BlockSpec(memory_space=pltpu.SMEM),
                  pl.BlockSpec((hw, tq), lambda h, i: (h, i)),
                  pl.BlockSpec((n_rows, hw), lambda h, i: (0, h)),
                  pl.BlockSpec((hw + ones_rows, n_rows), lambda h, i: (h, 0)),
                  pl.BlockSpec((1, hw), lambda h, i: (0, 0))],
        out_specs=pl.BlockSpec((tq, hw), lambda h, i: (i, h)),
        out_shape=jax.ShapeDtypeStruct((n_rows, DA_WIDTH), BF16),
        scratch_shapes=[pltpu.VMEM((2, 1, tq), F32), pltpu.VMEM((2, hw + ones_rows, tq), F32)],
        compiler_params=pltpu.CompilerParams(dimension_semantics=("parallel", "parallel"),
                                             vmem_limit_bytes=FLASH_VMEM_LIMIT),
        name="diff_attention",
    )(lam_full.reshape(1), qt, k, vt, subln.reshape(1, hw))


def _route_kernel(lg_ref, b_ref, tri_ref, eidx_ref, w_ref, rank_ref, cnt_ref, carry_ref):
    t = lg_ref.shape[1]

    @pl.when(pl.program_id(0) == 0)
    def _():
        carry_ref[...] = jnp.zeros_like(carry_ref)

    scores = jax.nn.sigmoid(lg_ref[...])
    choice = (scores + b_ref[...]).reshape(N_GROUPS, GROUP_SIZE, t)
    s3 = scores.reshape(N_GROUPS, GROUP_SIZE, t)
    member = lax.broadcasted_iota(jnp.int32, choice.shape, 1)
    group = lax.broadcasted_iota(jnp.int32, (N_GROUPS, 1, t), 0)
    expert = lax.broadcasted_iota(jnp.int32, choice.shape, 0) * GROUP_SIZE + member
    neg_inf = -jnp.inf
    m1 = jnp.max(choice, axis=1, keepdims=True)
    first = jnp.min(jnp.where(choice == m1, member, GROUP_SIZE), axis=1, keepdims=True)
    m2 = jnp.max(jnp.where(member == first, neg_inf, choice), axis=1, keepdims=True)
    gscore = m1 + m2
    gsel = jnp.zeros(gscore.shape, F32)
    for _ in range(TOPK_GROUPS):
        m = jnp.max(gscore, axis=0, keepdims=True)
        f = jnp.min(jnp.where(gscore == m, group, N_GROUPS), axis=0, keepdims=True)
        hit = group == f
        gsel = jnp.where(hit, 1.0, gsel)
        gscore = jnp.where(hit, neg_inf, gscore)
    cand = jnp.where(gsel > 0.0, choice, neg_inf)
    esel = jnp.zeros(choice.shape, F32)
    picks = []
    for _ in range(TOP_K):
        m = jnp.max(jnp.max(cand, axis=1, keepdims=True), axis=0, keepdims=True)
        f = jnp.min(jnp.min(jnp.where(cand == m, expert, N_EXPERTS), axis=1, keepdims=True), axis=0, keepdims=True)
        hit = expert == f
        esel = jnp.where(hit, 1.0, esel)
        cand = jnp.where(hit, neg_inf, cand)
        picks.append(f)
    w = s3 * esel
    denom = jnp.sum(jnp.sum(w, axis=1, keepdims=True), axis=0, keepdims=True) + 1e-20
    w = w / denom * ROUTED_SCALE
    sel = esel.reshape(N_EXPERTS, t)
    before = jnp.dot(sel.astype(BF16), tri_ref[...], preferred_element_type=F32) + carry_ref[...]
    before = before.reshape(N_GROUPS, GROUP_SIZE, t)
    pick = lambda a, hit: jnp.sum(jnp.sum(jnp.where(hit, a, 0.0), axis=1, keepdims=True), axis=0).reshape(1, t)
    for k, f in enumerate(picks):
        hit = expert == f
        eidx_ref[k:k + 1, :] = f.reshape(1, t)
        w_ref[k:k + 1, :] = pick(w, hit)
        rank_ref[k:k + 1, :] = pick(before, hit).astype(jnp.int32)
    carry_ref[...] += jnp.sum(sel, axis=1, keepdims=True)
    cnt_ref[...] = carry_ref[...]


def route(logits_t, bias, lo, hi):
    t = _token_tile(logits_t.shape[1])
    n = hi - lo
    tile0 = lo // t
    tri = (jnp.arange(t)[:, None] < jnp.arange(t)[None, :]).astype(BF16)
    tok = lambda dt: jax.ShapeDtypeStruct((TOP_K, n), dt)
    tok_spec = pl.BlockSpec((TOP_K, t), lambda i: (0, i))
    return pl.pallas_call(
        _route_kernel,
        grid=(n // t,),
        in_specs=[pl.BlockSpec((N_EXPERTS, t), lambda i: (0, tile0 + i)),
                  pl.BlockSpec((N_EXPERTS, 1), lambda i: (0, 0)),
                  pl.BlockSpec((t, t), lambda i: (0, 0))],
        out_specs=[tok_spec, tok_spec, tok_spec, pl.BlockSpec((N_EXPERTS, 1), lambda i: (0, 0))],
        out_shape=[tok(jnp.int32), tok(F32), tok(jnp.int32), jax.ShapeDtypeStruct((N_EXPERTS, 1), F32)],
        scratch_shapes=[pltpu.VMEM((N_EXPERTS, 1), F32)],
        compiler_params=_params(("arbitrary",)),
        name="route",
    )(logits_t, bias.reshape(N_EXPERTS, 1), tri)


def _slot_kernel(start_ref, eidx_ref, rank_ref, dest_ref):
    e = eidx_ref[...]
    d = rank_ref[...]
    for x in range(N_EXPERTS):
        d = d + jnp.where(e == x, start_ref[x], 0)
    dest_ref[...] = d


def slot_index(pad_start, eidx, rank):
    n = eidx.shape[1]
    t = _token_tile(n)
    spec = pl.BlockSpec((TOP_K, t), lambda i: (0, i))
    return pl.pallas_call(
        _slot_kernel,
        grid=(n // t,),
        in_specs=[pl.BlockSpec(memory_space=pltpu.SMEM), spec, spec],
        out_specs=spec,
        out_shape=jax.ShapeDtypeStruct((TOP_K, n), jnp.int32),
        compiler_params=_params(("parallel",)),
        name="slot_index",
    )(pad_start, eidx, rank)


def _sc_worker():
    return lax.axis_index("s") * SC_CORES + lax.axis_index("c")


def sc_dispatch(h, dest3, n_slots, row0):
    d = h.shape[1]
    n_win = dest3.shape[0]
    mesh = plsc.VectorSubcoreMesh(core_axis_name="c", subcore_axis_name="s")

    @functools.partial(
        pl.kernel, mesh=mesh, out_type=jax.ShapeDtypeStruct((n_slots, d), h.dtype),
        scratch_types=[pltpu.VMEM((TOP_K, SC_WINDOW), jnp.int32), pltpu.VMEM((SC_WINDOW, d), h.dtype),
                       pltpu.SemaphoreType.DMA])
    def k(h_hbm, dest_hbm, out_hbm, idx_v, rows_v, sem):
        wid = _sc_worker()

        @pl.loop(0, -(-n_win // SC_WORKERS))
        def _(it):
            w = it * SC_WORKERS + wid

            @pl.when(w < n_win)
            def _():
                pltpu.sync_copy(dest_hbm.at[w], idx_v)
                pltpu.sync_copy(h_hbm.at[pl.ds(row0 + w * SC_WINDOW, SC_WINDOW)], rows_v)
                copies = [pltpu.async_copy(rows_v, out_hbm.at[idx_v.at[j]], sem) for j in range(TOP_K)]
                for c in copies:
                    c.wait()

    return k(h, dest3)


def sc_combine_gather(y, dest3):
    d = y.shape[1]
    n_win = dest3.shape[0]
    n = n_win * SC_WINDOW
    mesh = plsc.VectorSubcoreMesh(core_axis_name="c", subcore_axis_name="s")

    @functools.partial(
        pl.kernel, mesh=mesh, out_type=jax.ShapeDtypeStruct((TOP_K, n, d), y.dtype),
        scratch_types=[pltpu.VMEM((TOP_K, SC_WINDOW), jnp.int32), pltpu.VMEM((2, SC_WINDOW, d), y.dtype),
                       pltpu.SemaphoreType.DMA, pltpu.SemaphoreType.DMA,
                       pltpu.SemaphoreType.DMA, pltpu.SemaphoreType.DMA])
    def k(y_hbm, dest_hbm, out_hbm, idx_v, rows_v, gsem0, gsem1, osem0, osem1):
        wid = _sc_worker()
        gsem, osem = (gsem0, gsem1), (osem0, osem1)

        @pl.loop(0, -(-n_win // SC_WORKERS))
        def _(it):
            w = it * SC_WORKERS + wid

            @pl.when(w < n_win)
            def _():
                pltpu.sync_copy(dest_hbm.at[w], idx_v)
                gather = lambda j: pltpu.async_copy(y_hbm.at[idx_v.at[j]], rows_v.at[j % 2], gsem[j % 2])
                g = [None] * TOP_K
                o = [None] * TOP_K
                g[0] = gather(0)
                for j in range(TOP_K):
                    if j + 1 < TOP_K:
                        if j >= 1:
                            o[j - 1].wait()
                        g[j + 1] = gather(j + 1)
                    g[j].wait()
                    o[j] = pltpu.async_copy(rows_v.at[j % 2], out_hbm.at[j, pl.ds(w * SC_WINDOW, SC_WINDOW)],
                                            osem[j % 2])
                o[TOP_K - 2].wait()
                o[TOP_K - 1].wait()

    return k(y, dest3)


def _expert_ffn_kernel(be_ref, bv_ref, x_ref, wg_ref, wu_ref, wd_ref, o_ref, wg_s, wu_s, wd_s):
    b = pl.program_id(0)
    valid = bv_ref[b]
    new_expert = (b == 0) | (be_ref[b] != be_ref[jnp.maximum(b - 1, 0)])

    @pl.when(new_expert)
    def _():
        wg_s[...] = wg_ref[0, 0].astype(BF16)
        wu_s[...] = wu_ref[0, 0].astype(BF16)
        wd_s[...] = wd_ref[0, 0].astype(BF16)

    @pl.when(valid > 0)
    def _():
        sub = x_ref.shape[0] // MOE_SUB_BLOCKS
        row = lax.broadcasted_iota(jnp.int32, (sub, 1), 0)
        for r in range(MOE_SUB_BLOCKS):
            rows = pl.ds(r * sub, sub)
            lo, hi = _unpack_bf16_pair(jnp.where(row + r * sub < valid, x_ref[rows, :], 0))
            x = jnp.concatenate([lo.astype(BF16), hi.astype(BF16)], axis=1)
            a = jnp.dot(x, wg_s[...], preferred_element_type=F32)
            a = a * jax.nn.sigmoid(a) * jnp.dot(x, wu_s[...], preferred_element_type=F32)
            y = jnp.dot(a.astype(BF16), wd_s[...], preferred_element_type=F32)
            half = y.shape[1] // 2
            o_ref[rows, :] = _pack_bf16_pair(y[:, :half], y[:, half:])


def expert_ffn(xg, block_expert, block_valid, wg, wu, wd, layer):
    n_slots, dp = xg.shape
    d, f = wg.shape[-2:]
    grid_spec = pltpu.PrefetchScalarGridSpec(
        num_scalar_prefetch=2,
        grid=(n_slots // MOE_BLOCK,),
        in_specs=[pl.BlockSpec((MOE_BLOCK, dp), lambda b, be, bv: (b, 0)),
                  pl.BlockSpec((1, 1, d, f), lambda b, be, bv: (layer, be[b], 0, 0)),
                  pl.BlockSpec((1, 1, d, f), lambda b, be, bv: (layer, be[b], 0, 0)),
                  pl.BlockSpec((1, 1, f, d), lambda b, be, bv: (layer, be[b], 0, 0))],
        out_specs=pl.BlockSpec((MOE_BLOCK, dp), lambda b, be, bv: (b, 0)),
        scratch_shapes=[pltpu.VMEM((d, f), BF16), pltpu.VMEM((d, f), BF16), pltpu.VMEM((f, d), BF16)],
    )
    return pl.pallas_call(
        _expert_ffn_kernel,
        grid_spec=grid_spec,
        out_shape=jax.ShapeDtypeStruct((n_slots, dp), jnp.int32),
        compiler_params=_params(("arbitrary",)),
        name="expert_ffn",
    )(block_expert, block_valid, xg, wg, wu, wd)


def _combine_kernel(yg_ref, w_ref, h_ref, swg_ref, swu_ref, swd_ref, *rest):
    o_ref = rest[-1]
    h = h_ref[...]
    a = jnp.dot(h, swg_ref[...], preferred_element_type=F32)
    a = a * jax.nn.sigmoid(a) * jnp.dot(h, swu_ref[...], preferred_element_type=F32)
    acc = jnp.dot(a.astype(BF16), swd_ref[...], preferred_element_type=F32)
    half = acc.shape[1] // 2
    acc_lo, acc_hi = acc[:, :half], acc[:, half:]
    wt = w_ref[...].T
    for k in range(TOP_K):
        lo, hi = _unpack_bf16_pair(yg_ref[k])
        acc_lo = acc_lo + wt[:, k:k + 1] * lo
        acc_hi = acc_hi + wt[:, k:k + 1] * hi
    o_ref[:, :half] = acc_lo
    o_ref[:, half:] = acc_hi


def combine(yg, w, h, swg, swu, swd, lo, prev):
    n_all, d = h.shape
    n = w.shape[1]
    f = swg.shape[-1]
    tm = ROW_TILE
    tile0 = lo // tm
    in_specs = [pl.BlockSpec((TOP_K, tm, d // 2), lambda i: (0, i, 0)),
                pl.BlockSpec((TOP_K, tm), lambda i: (0, i)),
                pl.BlockSpec((tm, d), lambda i: (tile0 + i, 0)),
                pl.BlockSpec((d, f), lambda i: (0, 0)),
                pl.BlockSpec((d, f), lambda i: (0, 0)),
                pl.BlockSpec((f, d), lambda i: (0, 0))]
    args = [yg, w, h, swg, swu, swd]
    aliases = {}
    if prev is not None:
        in_specs.append(pl.BlockSpec(memory_space=pl.ANY))
        args.append(prev)
        aliases = {len(args) - 1: 0}
    return pl.pallas_call(
        _combine_kernel,
        grid=(n // tm,),
        in_specs=in_specs,
        out_specs=pl.BlockSpec((tm, d), lambda i: (tile0 + i, 0)),
        out_shape=jax.ShapeDtypeStruct((n_all, d), F32),
        input_output_aliases=aliases,
        compiler_params=_params(("parallel",)),
        name="moe_combine",
    )(*args)


def moe(h, h_packed, logits_t, bias, wg, wu, wd, layer, swg, swu, swd):
    n = h.shape[0]
    t = _token_tile(n)
    cut = (n // t + 1) // 2 * t
    shared = (swg.astype(BF16), swu.astype(BF16), swd.astype(BF16))
    staged = [_moe_experts(h_packed, logits_t, bias, wg, wu, wd, layer, lo, hi) for lo, hi in ((0, cut), (cut, n))]
    out = None
    for (yg, w), lo in zip(staged, (0, cut)):
        out = combine(yg, w, h, *shared, lo, out)
    return out


def _moe_experts(h_packed, logits_t, bias, wg, wu, wd, layer, lo, hi):
    n = hi - lo
    eidx, w, rank, counts = route(logits_t, bias, lo, hi)
    counts = counts.reshape(N_EXPERTS).astype(jnp.int32)
    padded = (counts + MOE_BLOCK - 1) // MOE_BLOCK * MOE_BLOCK
    pad_end = jnp.cumsum(padded)
    pad_start = pad_end - padded
    n_slots = n * TOP_K + N_EXPERTS * MOE_BLOCK
    starts = jnp.arange(n_slots // MOE_BLOCK, dtype=jnp.int32) * MOE_BLOCK
    owner = jnp.sum((pad_end[None, :] <= starts[:, None]).astype(jnp.int32), axis=1)
    block_expert = jnp.minimum(owner, N_EXPERTS - 1)
    member = (block_expert[:, None] == jnp.arange(N_EXPERTS, dtype=jnp.int32)[None, :]).astype(jnp.int32)
    left = jnp.sum(member * (counts + pad_start)[None, :], axis=1) - starts
    block_valid = jnp.clip(left, 0, MOE_BLOCK).astype(jnp.int32)
    dest = slot_index(pad_start.astype(jnp.int32), eidx, rank)
    dest3 = dest.reshape(TOP_K, n // SC_WINDOW, SC_WINDOW).transpose(1, 0, 2)
    xg = sc_dispatch(h_packed, dest3, n_slots, lo)
    y = expert_ffn(xg, block_expert, block_valid, wg, wu, wd, layer)
    return sc_combine_gather(y, dest3), w


def mixer_ab(h, w_in, w_out, decay_logit, conv_w, conv_b, w1, b1, f1, w2, b2, f2, w3, skip, rope, dft, *, seq,
             ctx_len):
    n_rows = h.shape[0]
    tm = _token_tile(n_rows)
    p = mm([(h, w_in.astype(BF16))], F32, tm, 512, name="ab_in_proj")
    qkv_w = 2 * RET_QK + RET_V
    n_qk = 2 * RET_QK // LANE
    scales = (1.0,) * (RET_QK // LANE) + (RET_DK ** -0.5,) * (RET_QK // LANE) + (1.0,) * (RET_V // LANE)
    qkv = rope_cast(p, rope[0], rope[1], width=qkv_w, n_rot_blocks=n_qk, head_dim=RET_DK, scales=scales,
                    out_dtype=F32)
    log_g = jax.nn.log_sigmoid(decay_logit.astype(F32))
    ret = retention(qkv, p, log_g, jnp.exp(RET_CHUNK * log_g), seq=seq)
    v, x1, x2 = shortconv(p, conv_w, conv_b, seq=seq)
    filt = (w1, b1, f1, w2, b2, f2, w3)
    hy_x = long_conv_two_stage(dft["x"], hyena_filter_taps(seq, *filt), v[:seq], x1[:seq], x2[:seq], skip)
    hy_c = long_conv_one_stage(dft["c"], hyena_filter_taps(ctx_len, *filt), v[seq:], x1[seq:], x2[seq:], skip)
    hy = jnp.concatenate([hy_x, hy_c], axis=0)
    w_out = w_out.astype(BF16)
    return mm([(ret, w_out[:RET_V]), (hy, w_out[RET_V:])], F32, tm, 512, name="ab_out_proj")


def mixer_da(h, w_in, w_out, lam, subln, lambda_init, rope, *, seq, ctx_len):
    n_rows = h.shape[0]
    tm = _token_tile(n_rows)
    p = mm([(h, w_in.astype(BF16))], F32, tm, 512, name="da_in_proj")
    qt, k, vt = rope_da(p, rope[0], rope[1])
    lam_f = lam.astype(F32)
    lam_full = jnp.exp(jnp.sum(lam_f[0] * lam_f[1])) - jnp.exp(jnp.sum(lam_f[2] * lam_f[3])) + lambda_init
    o = diff_attention(qt, k, vt, lam_full, subln, seq=seq, ctx_len=ctx_len, lambda_init=lambda_init)
    return mm([(o, w_out.astype(BF16))], F32, tm, 512, name="da_out_proj")


def kernel(x, c, ctx, c_ctx, w_ada, b_ada, norm_mix, norm_ffn, ab_w_in, ab_w_out, ret_decay_logit, hy_conv_w, hy_conv_b, hy_w1, hy_b1, hy_freq1, hy_w2, hy_b2, hy_freq2, hy_w3, hy_skip, da_w_in, da_w_out, da_lambda, da_subln, router_w, router_b, exp_w_gate, exp_w_up, exp_w_down, sh_w_gate, sh_w_up, sh_w_down, norm_final):
    batch, seq, d = x.shape
    ctx_len = ctx.shape[1]
    assert batch == 1 and seq % ROW_TILE == 0 and ctx_len == ROW_TILE
    depth = w_ada.shape[0]
    n_rows = seq + ctx_len
    ctx_tile = seq // ROW_TILE

    xs = jnp.concatenate([x[0], ctx[0]], axis=0)
    cv = jnp.zeros((SUBLANE, d), F32).at[0].set(c_ctx).at[1].set(c[0])
    mods = adaln(cv, w_ada, b_ada)[:, :2].reshape(depth, 2, 6, d)

    rope_ret = rope_tables(seq, ctx_len, RET_DK)
    rope_da = rope_tables(seq, ctx_len, DA_HEAD_DIM)
    dft = dict(x=dft_tables_two_stage(2 * seq), c=dft_tables_one_stage(2 * ctx_len))
    common = dict(n_rows=n_rows, ctx_tile=ctx_tile)

    delta, gate_mods = None, None
    for i in range(depth):
        j = i // 2
        if delta is None:
            (h,) = norm_mod(xs, norm_mix[i], mods=mods[i], shift_idx=0, scale_idx=1, **common)
        else:
            xs, h = norm_mod(xs, norm_mix[i], delta=delta, gate_mods=gate_mods, gate_idx=5, mods=mods[i], shift_idx=0,
                             scale_idx=1, **common)
        if i % 2 == 0:
            y = mixer_ab(h, ab_w_in[j], ab_w_out[j], ret_decay_logit[j], hy_conv_w[j], hy_conv_b[j], hy_w1[j],
                         hy_b1[j], hy_freq1[j], hy_w2[j], hy_b2[j], hy_freq2[j], hy_w3[j], hy_skip[j], rope_ret, dft,
                         seq=seq, ctx_len=ctx_len)
        else:
            lambda_init = 0.8 - 0.6 * math.exp(-0.3 * i)
            y = mixer_da(h, da_w_in[j], da_w_out[j], da_lambda[j], da_subln[j], lambda_init, rope_da, seq=seq,
                         ctx_len=ctx_len)
        xs, h, logits_t, h_packed = norm_mod(xs, norm_ffn[i], delta=y, gate_mods=mods[i], gate_idx=2, mods=mods[i],
                                             shift_idx=3, scale_idx=4, router_wt=router_w[i].T, **common)
        delta = moe(h, h_packed, logits_t, router_b[i], exp_w_gate, exp_w_up, exp_w_down, i,
                    sh_w_gate[i], sh_w_up[i], sh_w_down[i])
        gate_mods = mods[i]
    _, out = norm_mod(xs, norm_final, n_rows=seq, ctx_tile=ctx_tile, delta=delta, gate_mods=gate_mods, gate_idx=5,
                      out_dtype=F32)
    return out[None]
```

```python
import functools
import math

import jax
import jax.numpy as jnp
from jax import lax
from jax.experimental import pallas as pl
from jax.experimental.pallas import tpu as pltpu
from jax.experimental.pallas import tpu_sc as plsc

F32 = jnp.float32
BF16 = jnp.bfloat16
HIGHEST = lax.Precision.HIGHEST

D_MODEL = 1024
DEPTH = 4
GRID_W = 64
EPS = 1e-6
ROPE_BASE = 10000.0

RET_HEADS = 4
RET_DK = 128
RET_DV = 256
RET_CHUNK = 128
RET_QK = RET_HEADS * RET_DK
RET_V = RET_HEADS * RET_DV

HY_WIDTH = 512
HY_ORDER = 2
HY_BANDS = 16
HY_EMB = 2 * HY_BANDS + 1
HY_FFN = 64
HY_DECAY_TARGET = 1e-2
HY_FAST_DECAY = 0.3
HY_SLOW_DECAY = 1.5
HY_ZCOLS = 64
HY_VALID_COL = HY_EMB
FFT_N2 = 128

AB_IN = 2 * RET_QK + 2 * RET_V + (HY_ORDER + 1) * HY_WIDTH
AB_CAT = RET_V + HY_WIDTH

DA_HEADS = 8
DA_HEAD_DIM = 64
DA_WIDTH = DA_HEADS * 2 * DA_HEAD_DIM

N_EXPERTS = 64
TOP_K = 8
N_GROUPS = 8
TOPK_GROUPS = 4
GROUP_SIZE = N_EXPERTS // N_GROUPS
EXPERT_DIM = 256
ROUTED_SCALE = 2.5
MOE_BLOCK = 512
MOE_SUB_BLOCKS = 2
SC_CORES = 2
SC_SUBCORES = 16
SC_WORKERS = SC_CORES * SC_SUBCORES
SC_WINDOW = 64

LANE = 128
SUBLANE = 8
ROW_TILE = 256
MAX_TOKEN_TILE = 1280
VMEM_LIMIT = 48 * 1024 * 1024
FLASH_VMEM_LIMIT = 56 * 1024 * 1024
NEG_BIG = -1e30
LOG2_E = 1.4426950408889634
FLASH_ONES_ROWS = 16
FLASH_INIT_KEYS = 16
FLASH_LAZY_HEADROOM = 60.0


def _params(sem):
    return pltpu.CompilerParams(dimension_semantics=sem, vmem_limit_bytes=VMEM_LIMIT)


def _token_tile(n):
    best = ROW_TILE
    t = ROW_TILE
    while t <= min(n, MAX_TOKEN_TILE):
        if n % t == 0:
            best = t
        t += ROW_TILE
    return best


def _mm_kernel(*refs, n_pairs, has_epi):
    acc = None
    for p in range(n_pairs):
        a = refs[2 * p][...].astype(BF16)
        b = refs[2 * p + 1][...].astype(BF16)
        d = jnp.dot(a, b, preferred_element_type=F32)
        acc = d if acc is None else acc + d
    idx = 2 * n_pairs
    if has_epi:
        acc = refs[idx][...] * (acc + refs[idx + 1][...] * refs[idx + 2][...])
        idx += 3
    o_ref = refs[idx]
    o_ref[...] = acc.astype(o_ref.dtype)


def mm(pairs, out_dtype, tm, tn, epi=None, name="mm"):
    m = pairs[0][0].shape[0]
    n = pairs[0][1].shape[1]
    assert m % tm == 0 and n % tn == 0
    in_specs, args = [], []
    for a, b in pairs:
        k = a.shape[1]
        in_specs += [pl.BlockSpec((tm, k), lambda i, j: (i, 0)), pl.BlockSpec((k, tn), lambda i, j: (0, j))]
        args += [a, b]
    if epi is not None:
        in_specs += [pl.BlockSpec((tm, tn), lambda i, j: (i, j)), pl.BlockSpec((1, tn), lambda i, j: (0, j)),
                     pl.BlockSpec((tm, tn), lambda i, j: (i, j))]
        args += list(epi)
    return pl.pallas_call(
        functools.partial(_mm_kernel, n_pairs=len(pairs), has_epi=epi is not None),
        grid=(m // tm, n // tn),
        in_specs=in_specs,
        out_specs=pl.BlockSpec((tm, tn), lambda i, j: (i, j)),
        out_shape=jax.ShapeDtypeStruct((m, n), out_dtype),
        compiler_params=_params(("parallel", "parallel")),
        name=name,
    )(*args)


def _adaln_kernel(cv_ref, w_ref, b_ref, o_ref):
    cv = cv_ref[...]
    s = cv * jax.nn.sigmoid(cv)
    o_ref[0] = jnp.dot(s, w_ref[0], precision=HIGHEST, preferred_element_type=F32) + b_ref[0]


def adaln(cv, w_ada, b_ada):
    depth, d, n = w_ada.shape
    tn = 1536
    return pl.pallas_call(
        _adaln_kernel,
        grid=(depth, n // tn),
        in_specs=[pl.BlockSpec((SUBLANE, d), lambda l, j: (0, 0)),
                  pl.BlockSpec((1, d, tn), lambda l, j: (l, 0, j)),
                  pl.BlockSpec((1, 1, tn), lambda l, j: (l, 0, j))],
        out_specs=pl.BlockSpec((1, SUBLANE, tn), lambda l, j: (l, 0, j)),
        out_shape=jax.ShapeDtypeStruct((depth, SUBLANE, n), F32),
        compiler_params=_params(("parallel", "parallel")),
        name="adaln",
    )(cv, w_ada, b_ada.reshape(depth, 1, n))


def _norm_mod_kernel(*refs, has_delta, gate_idx, shift_idx, scale_idx, has_router, write_xs):
    it = iter(refs)
    xs_ref = next(it)
    x = xs_ref[...]
    if has_delta:
        delta_ref = next(it)
        gmods_ref = next(it)
        x = x + gmods_ref[0, gate_idx:gate_idx + 1, :] * delta_ref[...]
    mods_ref = next(it) if shift_idx is not None else None
    g_ref = next(it)
    wr_ref = next(it) if has_router else None
    if write_xs:
        next(it)[...] = x
    h_ref = next(it)
    y = x * lax.rsqrt(jnp.mean(x * x, axis=-1, keepdims=True) + EPS) * g_ref[...]
    if shift_idx is not None:
        y = y * (1.0 + mods_ref[0, scale_idx:scale_idx + 1, :]) + mods_ref[0, shift_idx:shift_idx + 1, :]
    h_ref[...] = y.astype(h_ref.dtype)
    if has_router:
        lg_ref = next(it)
        lg_ref[...] = lax.dot_general(wr_ref[...], y, (((1,), (1,)), ((), ())),
                                      precision=HIGHEST, preferred_element_type=F32)
        half = y.shape[1] // 2
        next(it)[...] = _pack_bf16_pair(y[:, :half], y[:, half:])


def _pack_bf16_pair(a, b):
    def rounded(x):
        u = lax.bitcast_convert_type(x, jnp.int32)
        return u + 0x7FFF + (lax.shift_right_logical(u, 16) & 1)
    return lax.shift_right_logical(rounded(a), 16) | (rounded(b) & -65536)


def _unpack_bf16_pair(w):
    return (lax.bitcast_convert_type(lax.shift_left(w, 16), F32),
            lax.bitcast_convert_type(w & -65536, F32))


def norm_mod(xs, g, *, n_rows, ctx_tile, delta=None, gate_mods=None, gate_idx=None, mods=None, shift_idx=None,
             scale_idx=None, router_wt=None, out_dtype=BF16):
    d = xs.shape[1]
    n_tiles = n_rows // ROW_TILE
    row = pl.BlockSpec((ROW_TILE, d), lambda i: (i, 0))
    mod_spec = pl.BlockSpec((1, 6, d), lambda i: (jnp.where(i == ctx_tile, 0, 1), 0, 0))
    in_specs, args = [row], [xs]
    has_delta = delta is not None
    if has_delta:
        in_specs += [row, mod_spec]
        args += [delta, gate_mods]
    if shift_idx is not None:
        in_specs.append(mod_spec)
        args.append(mods)
    in_specs.append(pl.BlockSpec((1, d), lambda i: (0, 0)))
    args.append(g.reshape(1, d))
    has_router = router_wt is not None
    if has_router:
        in_specs.append(pl.BlockSpec(router_wt.shape, lambda i: (0, 0)))
        args.append(router_wt)
    out_specs, out_shape = [], []
    if has_delta:
        out_specs.append(row)
        out_shape.append(jax.ShapeDtypeStruct((n_rows, d), F32))
    out_specs.append(row)
    out_shape.append(jax.ShapeDtypeStruct((n_rows, d), out_dtype))
    if has_router:
        out_specs.append(pl.BlockSpec((N_EXPERTS, ROW_TILE), lambda i: (0, i)))
        out_shape.append(jax.ShapeDtypeStruct((N_EXPERTS, n_rows), F32))
        out_specs.append(pl.BlockSpec((ROW_TILE, d // 2), lambda i: (i, 0)))
        out_shape.append(jax.ShapeDtypeStruct((n_rows, d // 2), jnp.int32))
    return pl.pallas_call(
        functools.partial(_norm_mod_kernel, has_delta=has_delta, gate_idx=gate_idx, shift_idx=shift_idx,
                          scale_idx=scale_idx, has_router=has_router, write_xs=has_delta),
        grid=(n_tiles,),
        in_specs=in_specs,
        out_specs=out_specs,
        out_shape=out_shape,
        compiler_params=_params(("parallel",)),
        name="norm_mod",
    )(*args)


def _rope_kernel(p_ref, cos_ref, sin_ref, o_ref, *, n_rot_blocks, head_dim, scales):
    cos = cos_ref[...]
    sin = sin_ref[...]
    for b in range(len(scales)):
        x = p_ref[:, b * LANE:(b + 1) * LANE]
        if b < n_rot_blocks:
            if head_dim == LANE:
                rot = pltpu.roll(x, LANE // 2, 1)
            else:
                lane = lax.broadcasted_iota(jnp.int32, x.shape, 1)
                first_half = (lane % head_dim) < head_dim // 2
                rot = jnp.where(first_half, pltpu.roll(x, LANE - head_dim // 2, 1), pltpu.roll(x, head_dim // 2, 1))
            x = x * cos + rot * sin
        if scales[b] != 1.0:
            x = x * scales[b]
        o_ref[:, b * LANE:(b + 1) * LANE] = x.astype(o_ref.dtype)


def rope_cast(p, cos, sin, *, width, n_rot_blocks, head_dim, scales, out_dtype):
    n_rows = p.shape[0]
    return pl.pallas_call(
        functools.partial(_rope_kernel, n_rot_blocks=n_rot_blocks, head_dim=head_dim, scales=scales),
        grid=(n_rows // ROW_TILE,),
        in_specs=[pl.BlockSpec((ROW_TILE, width), lambda i: (i, 0)),
                  pl.BlockSpec((ROW_TILE, LANE), lambda i: (i, 0)),
                  pl.BlockSpec((ROW_TILE, LANE), lambda i: (i, 0))],
        out_specs=pl.BlockSpec((ROW_TILE, width), lambda i: (i, 0)),
        out_shape=jax.ShapeDtypeStruct((n_rows, width), out_dtype),
        compiler_params=_params(("parallel",)),
        name="rope_cast",
    )(p, cos, sin)


def rope_tables(seq, ctx_len, head_dim):
    n_freq = head_dim // 4
    inv = ROPE_BASE ** (-jnp.arange(n_freq, dtype=F32) / n_freq)
    rows = seq // GRID_W
    row = jnp.repeat(jnp.arange(rows, dtype=F32), GRID_W)
    col = jnp.tile(jnp.arange(GRID_W, dtype=F32), rows)
    ang = jnp.concatenate([row[:, None] * inv, col[:, None] * inv], axis=-1)
    cos, sin = jnp.cos(ang), jnp.sin(ang)
    cos = jnp.concatenate([cos, cos], axis=-1)
    sin = jnp.concatenate([-sin, sin], axis=-1)
    reps = LANE // head_dim
    cos, sin = jnp.tile(cos, (1, reps)), jnp.tile(sin, (1, reps))
    cos = jnp.concatenate([cos, jnp.ones((ctx_len, LANE), F32)], axis=0)
    sin = jnp.concatenate([sin, jnp.zeros((ctx_len, LANE), F32)], axis=0)
    return cos, sin


def _ret_kernel(lg_ref, gc_ref, q_ref, k_ref, v_ref, *rest, reverse):
    if reverse:
        yf_ref, gate_ref, o_ref, s_ref = rest
    else:
        o_ref, s_ref = rest
    c = RET_CHUNK

    @pl.when(pl.program_id(0) == 0)
    def _():
        s_ref[...] = jnp.zeros_like(s_ref)

    ii = lax.broadcasted_iota(jnp.int32, (c, c), 0)
    jj = lax.broadcasted_iota(jnp.int32, (c, c), 1)
    rel = ((jj - ii) if reverse else (ii - jj)).astype(F32)
    pos = lax.broadcasted_iota(jnp.int32, (c, 1), 0).astype(F32)
    for h in range(RET_HEADS):
        lg = lg_ref[h]
        dec = jnp.where(rel >= 0, jnp.exp(jnp.maximum(rel, 0.0) * lg), 0.0)
        if reverse:
            q_dec = jnp.exp((c - pos) * lg)
            k_dec = jnp.exp(pos * lg)
        else:
            q_dec = jnp.exp((pos + 1.0) * lg)
            k_dec = jnp.exp((c - 1.0 - pos) * lg)
        q = q_ref[:, h * RET_DK:(h + 1) * RET_DK]
        k = k_ref[:, h * RET_DK:(h + 1) * RET_DK]
        v = v_ref[:, h * RET_DV:(h + 1) * RET_DV].astype(BF16)
        s = lax.dot_general(q.astype(BF16), k.astype(BF16), (((1,), (1,)), ((), ())),
                            preferred_element_type=F32) * dec
        state = s_ref[h]
        y = jnp.dot(s.astype(BF16), v, preferred_element_type=F32)
        y = y + jnp.dot((q * q_dec).astype(BF16), state.astype(BF16), preferred_element_type=F32)
        upd = lax.dot_general((k * k_dec).astype(BF16), v, (((0,), (0,)), ((), ())), preferred_element_type=F32)
        s_ref[h] = gc_ref[h] * state + upd
        if reverse:
            r = y + yf_ref[:, h * RET_DV:(h + 1) * RET_DV]
            mu = jnp.mean(r, axis=-1, keepdims=True)
            rc = r - mu
            var = jnp.mean(rc * rc, axis=-1, keepdims=True)
            g = gate_ref[:, h * RET_DV:(h + 1) * RET_DV]
            o_ref[:, h * RET_DV:(h + 1) * RET_DV] = (rc * lax.rsqrt(var + EPS) * (g * jax.nn.sigmoid(g))).astype(
                o_ref.dtype)
        else:
            o_ref[:, h * RET_DV:(h + 1) * RET_DV] = y


def retention(qkv, p, log_g, g_chunk, *, seq):
    n_rows = qkv.shape[0]
    n_chunks = n_rows // RET_CHUNK
    n_x = seq // RET_CHUNK
    smem = pl.BlockSpec(memory_space=pltpu.SMEM)

    def run(reverse, extra):
        if reverse:
            idx = lambda t: n_chunks - 1 - t
        else:
            idx = lambda t: (t + n_x) % n_chunks
        in_specs = [smem, smem,
                    pl.BlockSpec((RET_CHUNK, RET_QK), lambda t: (idx(t), 0)),
                    pl.BlockSpec((RET_CHUNK, RET_QK), lambda t: (idx(t), 1)),
                    pl.BlockSpec((RET_CHUNK, RET_V), lambda t: (idx(t), 1))]
        args = [log_g[1 if reverse else 0], g_chunk[1 if reverse else 0], qkv, qkv, qkv]
        if reverse:
            in_specs += [pl.BlockSpec((RET_CHUNK, RET_V), lambda t: (idx(t), 0)),
                         pl.BlockSpec((RET_CHUNK, RET_V), lambda t: (idx(t), 2))]
            args += list(extra)
        return pl.pallas_call(
            functools.partial(_ret_kernel, reverse=reverse),
            grid=(n_chunks,),
            in_specs=in_specs,
            out_specs=pl.BlockSpec((RET_CHUNK, RET_V), lambda t: (idx(t), 0)),
            out_shape=jax.ShapeDtypeStruct((n_rows, RET_V), BF16 if reverse else F32),
            scratch_shapes=[pltpu.VMEM((RET_HEADS, RET_DK, RET_DV), F32)],
            compiler_params=_params(("arbitrary",)),
            name="retention_bwd" if reverse else "retention_fwd",
        )(*args)

    y_fwd = run(False, None)
    return run(True, (y_fwd, p))


def _shortconv_kernel(cur_ref, prev_ref, next_ref, w_ref, b_ref, v_ref, x1_ref, x2_ref, *, x_tiles):
    i = pl.program_id(0)
    cur = cur_ref[...]
    rows = cur.shape[0]
    row = lax.broadcasted_iota(jnp.int32, (rows, 1), 0)
    has_prev = jnp.where((i == 0) | (i == x_tiles), 0.0, 1.0)
    has_next = jnp.where((i == x_tiles - 1) | (i == x_tiles), 0.0, 1.0)
    up = jnp.where(row == 0, prev_ref[SUBLANE - 1:SUBLANE, :] * has_prev, pltpu.roll(cur, 1, 0))
    dn = jnp.where(row == rows - 1, next_ref[0:1, :] * has_next, pltpu.roll(cur, rows - 1, 0))
    y = up * w_ref[0:1, :] + cur * w_ref[1:2, :] + dn * w_ref[2:3, :] + b_ref[...]
    v_ref[...] = y[:, :HY_WIDTH]
    x1_ref[...] = y[:, HY_WIDTH:2 * HY_WIDTH]
    x2_ref[...] = y[:, 2 * HY_WIDTH:]


def shortconv(p, w, b, *, seq):
    n_rows = p.shape[0]
    width = 3 * HY_WIDTH
    col = p.shape[1] // width - 1
    per = ROW_TILE // SUBLANE
    last = n_rows // SUBLANE - 1
    out = jax.ShapeDtypeStruct((n_rows, HY_WIDTH), F32)
    ospec = pl.BlockSpec((ROW_TILE, HY_WIDTH), lambda i: (i, 0))
    return pl.pallas_call(
        functools.partial(_shortconv_kernel, x_tiles=seq // ROW_TILE),
        grid=(n_rows // ROW_TILE,),
        in_specs=[pl.BlockSpec((ROW_TILE, width), lambda i: (i, col)),
                  pl.BlockSpec((SUBLANE, width), lambda i: (jnp.maximum(i * per - 1, 0), col)),
                  pl.BlockSpec((SUBLANE, width), lambda i: (jnp.minimum((i + 1) * per, last), col)),
                  pl.BlockSpec((3, width), lambda i: (0, 0)),
                  pl.BlockSpec((1, width), lambda i: (0, 0))],
        out_specs=[ospec, ospec, ospec],
        out_shape=[out, out, out],
        compiler_params=_params(("parallel",)),
        name="shortconv",
    )(p, p, p, w, b.reshape(1, width))


def _filt_kernel(z_ref, w1_ref, b1_ref, f1_ref, w2_ref, b2_ref, f2_ref, w3a_ref, w3b_ref, dl_ref, *o_ref):
    z = z_ref[...]
    h = jnp.sin(f1_ref[...] * (jnp.dot(z, w1_ref[...], precision=HIGHEST, preferred_element_type=F32) + b1_ref[...]))
    h = jnp.sin(f2_ref[...] * (jnp.dot(h, w2_ref[...], precision=HIGHEST, preferred_element_type=F32) + b2_ref[...]))
    window = jnp.exp(-z[:, 0:1] * dl_ref[...]) * z[:, HY_VALID_COL:HY_VALID_COL + 1]
    for o, w3_ref in enumerate((w3a_ref, w3b_ref)):
        o_ref[o][...] = jnp.dot(h, w3_ref[...], precision=HIGHEST, preferred_element_type=F32) * window


def hyena_filter_taps(length, w1, b1, f1, w2, b2, f2, w3):
    z = _filter_positions(length)
    w1p = jnp.zeros((HY_ZCOLS, HY_FFN), F32).at[:HY_EMB].set(w1)
    deltas = jnp.abs(jnp.linspace(math.log(HY_DECAY_TARGET) / HY_SLOW_DECAY,
                                  math.log(HY_DECAY_TARGET) / HY_FAST_DECAY, HY_WIDTH, dtype=F32)).reshape(1, HY_WIDTH)
    tm = min(length, 512)
    half_tiles = length // tm
    vec = lambda a: a.reshape(1, HY_FFN)
    small = lambda shape: pl.BlockSpec(shape, lambda i: (0, 0))
    w3_spec = lambda o: pl.BlockSpec((HY_FFN, HY_WIDTH), lambda i: (0, 2 * o + jnp.where(i >= half_tiles, 1, 0)))
    assert HY_ORDER == 2
    return pl.pallas_call(
        _filt_kernel,
        grid=(2 * half_tiles,),
        in_specs=[pl.BlockSpec((tm, HY_ZCOLS), lambda i: (i, 0)),
                  small((HY_ZCOLS, HY_FFN)), small((1, HY_FFN)), small((1, HY_FFN)),
                  small((HY_FFN, HY_FFN)), small((1, HY_FFN)), small((1, HY_FFN)),
                  w3_spec(0), w3_spec(1), small((1, HY_WIDTH))],
        out_specs=[pl.BlockSpec((tm, HY_WIDTH), lambda i: (i, 0))] * HY_ORDER,
        out_shape=[jax.ShapeDtypeStruct((2 * length, HY_WIDTH), F32)] * HY_ORDER,
        compiler_params=_params(("parallel",)),
        name="hyena_filter",
    )(z, w1p, vec(b1), vec(f1), w2, vec(b2), vec(f2), w3, w3, deltas)


def _filter_positions(length):
    t = jnp.concatenate([jnp.arange(length, dtype=F32), float(length) - jnp.arange(length, dtype=F32)])
    valid = jnp.ones((2 * length,), F32).at[length].set(0.0)
    t_norm = t / max(length - 1, 1)
    bands = jnp.linspace(1e-4, HY_BANDS - 1, HY_BANDS, dtype=F32)
    ang = (2.0 * math.pi / length) * t[:, None] * bands[None, :]
    z = jnp.concatenate([t_norm[:, None], jnp.cos(ang), -jnp.sin(ang), valid[:, None]], axis=-1)
    return jnp.pad(z, ((0, 0), (0, HY_ZCOLS - z.shape[1])))


def _angles(num, den):
    return (2.0 * math.pi / den) * (num % den).astype(F32)


def dft_tables_two_stage(m):
    n2 = FFT_N2
    n1 = m // n2
    half = n1 // 2
    kp = -(-(half + 1) // SUBLANE) * SUBLANE
    k1 = jnp.arange(kp, dtype=jnp.int32)
    live = (k1 <= half)
    a1 = _angles(k1[:, None] * jnp.arange(n1, dtype=jnp.int32)[None, :], n1)
    f1 = jnp.concatenate([jnp.where(live[:, None], jnp.cos(a1), 0.0), jnp.where(live[:, None], -jnp.sin(a1), 0.0)], 0)
    wgt = jnp.where((k1 == 0) | (k1 == half), 1.0, 2.0) * live / m
    a1h = a1[:, :half].T
    cinv = jnp.concatenate([jnp.cos(a1h) * wgt[None, :], -jnp.sin(a1h) * wgt[None, :]], axis=1)
    k = k1[:, None, None] + n1 * jnp.arange(n2, dtype=jnp.int32)[None, :, None]
    th = _angles(k * jnp.arange(n2, dtype=jnp.int32)[None, None, :], m)
    c = jnp.where(live[:, None, None], jnp.cos(th), 0.0)
    s = jnp.where(live[:, None, None], jnp.sin(th), 0.0)
    g_fwd = jnp.concatenate([jnp.concatenate([c, s], 2), jnp.concatenate([-s, c], 2)], 1)
    ct, st = jnp.swapaxes(c, 1, 2), jnp.swapaxes(s, 1, 2)
    g_inv = jnp.concatenate([jnp.concatenate([ct, -st], 2), jnp.concatenate([st, ct], 2)], 1)
    return dict(n1=n1, kp=kp, f1=f1.astype(BF16), f1_half=f1[:, :half].astype(BF16), cinv=cinv.astype(BF16),
                g_fwd=g_fwd.astype(BF16), g_inv=g_inv.astype(BF16))


def dft_tables_one_stage(m):
    half = m // 2
    kp = -(-(half + 1) // SUBLANE) * SUBLANE
    k = jnp.arange(kp, dtype=jnp.int32)
    live = (k <= half)
    a = _angles(k[:, None] * jnp.arange(m, dtype=jnp.int32)[None, :], m)
    f = jnp.concatenate([jnp.where(live[:, None], jnp.cos(a), 0.0), jnp.where(live[:, None], -jnp.sin(a), 0.0)], 0)
    wgt = jnp.where((k == 0) | (k == half), 1.0, 2.0) * live / m
    ah = a[:, :half].T
    cinv = jnp.concatenate([jnp.cos(ah) * wgt[None, :], -jnp.sin(ah) * wgt[None, :]], axis=1)
    return dict(kp=kp, f=f.astype(BF16), f_half=f[:, :half].astype(BF16), cinv=cinv.astype(BF16))


def _bmm_kernel(*refs, kb, in_part_major, out_part_major, has_h):
    if has_h:
        g_ref, a_ref, h_ref, o_ref = refs
    else:
        g_ref, a_ref, o_ref = refs
    n2 = FFT_N2
    for b in range(kb):
        if in_part_major:
            ar, ai = a_ref[0, b], a_ref[1, b]
        else:
            ar, ai = a_ref[b, 0], a_ref[b, 1]
        if has_h:
            hr, hi = h_ref[b, 0], h_ref[b, 1]
            ar, ai = ar * hr - ai * hi, ar * hi + ai * hr
        xin = jnp.concatenate([ar, ai], axis=0).astype(BF16)
        y = jnp.dot(g_ref[b], xin, preferred_element_type=F32)
        if out_part_major:
            o_ref[0, b] = y[:n2]
            o_ref[1, b] = y[n2:]
        else:
            o_ref[b, 0] = y[:n2]
            o_ref[b, 1] = y[n2:]


def bmm_k1(g, a, h=None, *, in_part_major, out_part_major):
    kp = g.shape[0]
    n2 = FFT_N2
    c = a.shape[-1]
    kb, tc = SUBLANE, 256
    pm = lambda: pl.BlockSpec((2, kb, n2, tc), lambda i, j: (0, i, 0, j))
    km = lambda: pl.BlockSpec((kb, 2, n2, tc), lambda i, j: (i, 0, 0, j))
    in_specs = [pl.BlockSpec((kb, 2 * n2, 2 * n2), lambda i, j: (i, 0, 0)), pm() if in_part_major else km()]
    args = [g, a]
    if h is not None:
        in_specs.append(km())
        args.append(h)
    return pl.pallas_call(
        functools.partial(_bmm_kernel, kb=kb, in_part_major=in_part_major, out_part_major=out_part_major,
                          has_h=h is not None),
        grid=(kp // kb, c // tc),
        in_specs=in_specs,
        out_specs=pm() if out_part_major else km(),
        out_shape=jax.ShapeDtypeStruct((2, kp, n2, c) if out_part_major else (kp, 2, n2, c), F32),
        compiler_params=_params(("parallel", "parallel")),
        name="dft_inner",
    )(*args)


def _cmul_kernel(x_ref, h_ref, o_ref):
    xr, xi, hr, hi = x_ref[0], x_ref[1], h_ref[0], h_ref[1]
    o_ref[0] = xr * hr - xi * hi
    o_ref[1] = xr * hi + xi * hr


def cmul(x, h):
    spec = pl.BlockSpec(x.shape, lambda i: (0, 0, 0))
    return pl.pallas_call(_cmul_kernel, grid=(1,), in_specs=[spec, spec], out_specs=spec,
                          out_shape=jax.ShapeDtypeStruct(x.shape, F32), compiler_params=_params(("arbitrary",)),
                          name="spectrum_product")(x, h)


def _dft_outer3_kernel(f_ref, x_ref, o_ref):
    c = x_ref.shape[2]
    f = f_ref[...]
    for j in range(SUBLANE):
        o_ref[:, j * c:(j + 1) * c] = jnp.dot(f, x_ref[:, j, :].astype(BF16), preferred_element_type=F32)


def dft_outer3(f, x3, n_outer):
    rows = f.shape[0]
    c = x3.shape[2]
    return pl.pallas_call(
        _dft_outer3_kernel,
        grid=(FFT_N2 // SUBLANE,),
        in_specs=[pl.BlockSpec((rows, n_outer), lambda j: (0, 0)),
                  pl.BlockSpec((n_outer, SUBLANE, c), lambda j: (0, j, 0))],
        out_specs=pl.BlockSpec((rows, SUBLANE * c), lambda j: (0, j)),
        out_shape=jax.ShapeDtypeStruct((rows, FFT_N2 * c), F32),
        compiler_params=_params(("parallel",)),
        name="dft_outer",
    )(f, x3)


def _idft_gate3_kernel(cinv_ref, b_ref, gate_ref, skip_ref, u_ref, o_ref, *, u_is_3d):
    c = gate_ref.shape[2]
    cinv = cinv_ref[...]
    for j in range(SUBLANE):
        cols = slice(j * c, (j + 1) * c)
        acc = jnp.dot(cinv, b_ref[:, cols].astype(BF16), preferred_element_type=F32)
        u = u_ref[:, j, :] if u_is_3d else u_ref[:, cols]
        o_ref[:, cols] = gate_ref[:, j, :] * (acc + skip_ref[...] * u)


def idft_gate3(cinv, b2d, gate3, skip_row, u):
    n_outer = cinv.shape[0]
    c = gate3.shape[2]
    u_is_3d = u.ndim == 3
    wide = pl.BlockSpec((n_outer, SUBLANE * c), lambda j: (0, j))
    slab = pl.BlockSpec((n_outer, SUBLANE, c), lambda j: (0, j, 0))
    return pl.pallas_call(
        functools.partial(_idft_gate3_kernel, u_is_3d=u_is_3d),
        grid=(FFT_N2 // SUBLANE,),
        in_specs=[pl.BlockSpec(cinv.shape, lambda j: (0, 0)),
                  pl.BlockSpec((b2d.shape[0], SUBLANE * c), lambda j: (0, j)),
                  slab, pl.BlockSpec((1, c), lambda j: (0, 0)), slab if u_is_3d else wide],
        out_specs=wide,
        out_shape=jax.ShapeDtypeStruct((n_outer, FFT_N2 * c), F32),
        compiler_params=_params(("parallel",)),
        name="idft_outer_gate",
    )(cinv, b2d, gate3, skip_row, u)


def long_conv_two_stage(tabs, taps, v, x1, x2, skip, length):
    c = v.shape[1]
    n2, n1, kp = FFT_N2, tabs["n1"], tabs["kp"]
    as3 = lambda a: a.reshape(a.shape[0] // n2, n2, c)
    spectrum = lambda a2d: bmm_k1(tabs["g_fwd"], a2d.reshape(2, kp, n2, c), in_part_major=True, out_part_major=False)

    spectra = [spectrum(dft_outer3(tabs["f1"], as3(taps[o]), n1)) for o in range(HY_ORDER)]
    v3 = as3(v)
    u = v3
    for o, gate in enumerate((x1, x2)):
        if u.ndim == 3:
            a = dft_outer3(tabs["f1_half"], u, n1 // 2)
        else:
            a = mm([(tabs["f1_half"], u)], F32, 2 * kp, 2048, name="dft_outer")
        bt = bmm_k1(tabs["g_inv"], spectrum(a), spectra[o], in_part_major=False, out_part_major=True)
        u = idft_gate3(tabs["cinv"], bt.reshape(2 * kp, n2 * c), as3(gate), skip[o].reshape(1, c), u)
    return u.reshape(length, c)


def long_conv_one_stage(tabs, taps, v, x1, x2, skip):
    length, c = v.shape
    kp = tabs["kp"]
    u = v
    for o, gate in enumerate((x1, x2)):
        hs = mm([(tabs["f"], taps[o])], F32, 2 * kp, c, name="ctx_dft").reshape(2, kp, c)
        xs = mm([(tabs["f_half"], u)], F32, 2 * kp, c, name="ctx_dft").reshape(2, kp, c)
        ys = cmul(xs, hs).reshape(2 * kp, c)
        u = mm([(tabs["cinv"], ys)], F32, length, c, epi=(gate, skip[o].reshape(1, c), u), name="ctx_idft_gate")
    return u


def _flash_kernel(lam_ref, qt_ref, k_ref, vt_ref, sub_ref, o_ref, m_ref, acc_ref, *, kv, seq, ctx_len, out_scale):
    i = pl.program_id(1)
    last_q = pl.num_programs(1) - 1
    tq = qt_ref.shape[1]
    d = DA_HEAD_DIM
    dv = 2 * DA_HEAD_DIM
    n_chunks = k_ref.shape[0] // kv
    acc_ref[...] = jnp.zeros_like(acc_ref)

    def scores(off, rows, c, masked):
        s = jnp.dot(k_ref[pl.ds(off, rows), c * d:(c + 1) * d], qt_ref[c * d:(c + 1) * d, :],
                    preferred_element_type=F32)
        if masked:
            key = off + lax.broadcasted_iota(jnp.int32, (rows, 1), 0)
            lane = lax.broadcasted_iota(jnp.int32, (1, tq), 1)
            s = s + jnp.where(key < seq, NEG_BIG, 0.0) * jnp.where(lane >= tq - ctx_len, 1.0, 0.0)
        return s

    def exact_step(off, c, masked):
        s = scores(off, kv, c, masked)
        m_old = m_ref[c]
        m_new = jnp.maximum(m_old, jnp.max(s, axis=0, keepdims=True))
        pr = jnp.exp2(s - m_new).astype(BF16)
        acc_ref[c] = jnp.exp2(m_old - m_new) * acc_ref[c] + jnp.dot(vt_ref[:, pl.ds(off, kv)], pr,
                                                                   preferred_element_type=F32)
        m_ref[c] = m_new

    def lazy_step(off, c, masked):
        s = scores(off, kv, c, masked)
        m_old = m_ref[c]
        m_chunk = jnp.max(s, axis=0, keepdims=True)
        pv = jnp.dot(vt_ref[:, pl.ds(off, kv)], jnp.exp2(s - m_old).astype(BF16), preferred_element_type=F32)
        safe = jnp.max(m_chunk - m_old) <= FLASH_LAZY_HEADROOM

        @pl.when(safe)
        def _():
            m_new = jnp.maximum(m_old, m_chunk)
            acc_ref[c] = jnp.exp2(m_old - m_new) * (acc_ref[c] + pv)
            m_ref[c] = m_new

        @pl.when(jnp.logical_not(safe))
        def _():
            exact_step(off, c, masked)

    def run(masked):
        for c in range(2):
            m_ref[c] = jnp.max(scores(0, FLASH_INIT_KEYS, c, masked), axis=0, keepdims=True)

        def body(kc, carry):
            off = pl.multiple_of(kc * kv, kv)
            for c in range(2):
                lazy_step(off, c, masked)
            return carry

        lax.fori_loop(0, n_chunks, body, 0)

    @pl.when(i != last_q)
    def _():
        run(False)

    @pl.when(i == last_q)
    def _():
        run(True)

    a0 = acc_ref[0, :dv, :] / acc_ref[0, dv:dv + 1, :]
    a1 = acc_ref[1, :dv, :] / acc_ref[1, dv:dv + 1, :]
    o = (a0 - lam_ref[0] * a1).T
    o = o * lax.rsqrt(jnp.mean(o * o, axis=-1, keepdims=True) + 1e-5) * sub_ref[...]
    o_ref[...] = (o * out_scale).astype(o_ref.dtype)


def _rope_da_kernel(p_ref, cos_ref, sin_ref, qt_ref, k_ref, vt_ref):
    cos = cos_ref[...]
    sin = sin_ref[...]
    hw = 2 * DA_HEAD_DIM
    lane = lax.broadcasted_iota(jnp.int32, cos.shape, 1)
    first_half = (lane % DA_HEAD_DIM) < DA_HEAD_DIM // 2

    def rotated(b):
        x = p_ref[:, b * LANE:(b + 1) * LANE]
        rot = jnp.where(first_half, pltpu.roll(x, LANE - DA_HEAD_DIM // 2, 1), pltpu.roll(x, DA_HEAD_DIM // 2, 1))
        return x * cos + rot * sin

    ones = jnp.ones((FLASH_ONES_ROWS, cos.shape[0]), BF16)
    for h in range(DA_HEADS):
        qt_ref[h * hw:(h + 1) * hw, :] = (rotated(h) * (LOG2_E * DA_HEAD_DIM ** -0.5)).T.astype(BF16)
        k_ref[:, h * hw:(h + 1) * hw] = rotated(DA_HEADS + h).astype(BF16)
        base = h * (hw + FLASH_ONES_ROWS)
        vt_ref[base:base + hw, :] = p_ref[:, (2 * DA_HEADS + h) * LANE:(2 * DA_HEADS + h + 1) * LANE].T.astype(BF16)
        vt_ref[base + hw:base + hw + FLASH_ONES_ROWS, :] = ones


def rope_da(p, cos, sin):
    n_rows = p.shape[0]
    assert 2 * DA_HEAD_DIM == LANE
    vt_rows = DA_HEADS * (LANE + FLASH_ONES_ROWS)
    return pl.pallas_call(
        _rope_da_kernel,
        grid=(n_rows // ROW_TILE,),
        in_specs=[pl.BlockSpec((ROW_TILE, 3 * DA_WIDTH), lambda i: (i, 0)),
                  pl.BlockSpec((ROW_TILE, LANE), lambda i: (i, 0)),
                  pl.BlockSpec((ROW_TILE, LANE), lambda i: (i, 0))],
        out_specs=[pl.BlockSpec((DA_WIDTH, ROW_TILE), lambda i: (0, i)),
                   pl.BlockSpec((ROW_TILE, DA_WIDTH), lambda i: (i, 0)),
                   pl.BlockSpec((vt_rows, ROW_TILE), lambda i: (0, i))],
        out_shape=[jax.ShapeDtypeStruct((DA_WIDTH, n_rows), BF16),
                   jax.ShapeDtypeStruct((n_rows, DA_WIDTH), BF16),
                   jax.ShapeDtypeStruct((vt_rows, n_rows), BF16)],
        compiler_params=_params(("parallel",)),
        name="rope_da",
    )(p, cos, sin)


def diff_attention(qt, k, vt, lam_full, subln, *, seq, ctx_len, lambda_init):
    n_rows = k.shape[0]
    tq = _token_tile(n_rows)
    hw = 2 * DA_HEAD_DIM
    ones_rows = FLASH_ONES_ROWS
    return pl.pallas_call(
        functools.partial(_flash_kernel, kv=tq, seq=seq, ctx_len=ctx_len, out_scale=1.0 - lambda_init),
        grid=(DA_HEADS, n_rows // tq),
        in_specs=[pl.BlockSpec(memory_space=pltpu.SMEM),
                  pl.BlockSpec((hw, tq), lambda h, i: (h, i)),
                  pl.BlockSpec((n_rows, hw), lambda h, i: (0, h)),
                  pl.BlockSpec((hw + ones_rows, n_rows), lambda h, i: (h, 0)),
                  pl.BlockSpec((1, hw), lambda h, i: (0, 0))],
        out_specs=pl.BlockSpec((tq, hw), lambda h, i: (i, h)),
        out_shape=jax.ShapeDtypeStruct((n_rows, DA_WIDTH), BF16),
        scratch_shapes=[pltpu.VMEM((2, 1, tq), F32), pltpu.VMEM((2, hw + ones_rows, tq), F32)],
        compiler_params=pltpu.CompilerParams(dimension_semantics=("parallel", "parallel"),
                                             vmem_limit_bytes=FLASH_VMEM_LIMIT),
        name="diff_attention",
    )(lam_full.reshape(1), qt, k, vt, subln.reshape(1, hw))


def _route_kernel(lg_ref, b_ref, tri_ref, eidx_ref, w_ref, rank_ref, cnt_ref, carry_ref):
    t = lg_ref.shape[1]

    @pl.when(pl.program_id(0) == 0)
    def _():
        carry_ref[...] = jnp.zeros_like(carry_ref)

    scores = jax.nn.sigmoid(lg_ref[...])
    choice = (scores + b_ref[...]).reshape(N_GROUPS, GROUP_SIZE, t)
    s3 = scores.reshape(N_GROUPS, GROUP_SIZE, t)
    member = lax.broadcasted_iota(jnp.int32, choice.shape, 1)
    group = lax.broadcasted_iota(jnp.int32, (N_GROUPS, 1, t), 0)
    expert = lax.broadcasted_iota(jnp.int32, choice.shape, 0) * GROUP_SIZE + member
    neg_inf = -jnp.inf
    m1 = jnp.max(choice, axis=1, keepdims=True)
    first = jnp.min(jnp.where(choice == m1, member, GROUP_SIZE), axis=1, keepdims=True)
    m2 = jnp.max(jnp.where(member == first, neg_inf, choice), axis=1, keepdims=True)
    gscore = m1 + m2
    gsel = jnp.zeros(gscore.shape, F32)
    for _ in range(TOPK_GROUPS):
        m = jnp.max(gscore, axis=0, keepdims=True)
        f = jnp.min(jnp.where(gscore == m, group, N_GROUPS), axis=0, keepdims=True)
        hit = group == f
        gsel = jnp.where(hit, 1.0, gsel)
        gscore = jnp.where(hit, neg_inf, gscore)
    cand = jnp.where(gsel > 0.0, choice, neg_inf)
    esel = jnp.zeros(choice.shape, F32)
    picks = []
    for _ in range(TOP_K):
        m = jnp.max(jnp.max(cand, axis=1, keepdims=True), axis=0, keepdims=True)
        f = jnp.min(jnp.min(jnp.where(cand == m, expert, N_EXPERTS), axis=1, keepdims=True), axis=0, keepdims=True)
        hit = expert == f
        esel = jnp.where(hit, 1.0, esel)
        cand = jnp.where(hit, neg_inf, cand)
        picks.append(f)
    w = s3 * esel
    denom = jnp.sum(jnp.sum(w, axis=1, keepdims=True), axis=0, keepdims=True) + 1e-20
    w = w / denom * ROUTED_SCALE
    sel = esel.reshape(N_EXPERTS, t)
    before = jnp.dot(sel.astype(BF16), tri_ref[...], preferred_element_type=F32) + carry_ref[...]
    before = before.reshape(N_GROUPS, GROUP_SIZE, t)
    pick = lambda a, hit: jnp.sum(jnp.sum(jnp.where(hit, a, 0.0), axis=1, keepdims=True), axis=0).reshape(1, t)
    for k, f in enumerate(picks):
        hit = expert == f
        eidx_ref[k:k + 1, :] = f.reshape(1, t)
        w_ref[k:k + 1, :] = pick(w, hit)
        rank_ref[k:k + 1, :] = pick(before, hit).astype(jnp.int32)
    carry_ref[...] += jnp.sum(sel, axis=1, keepdims=True)
    cnt_ref[...] = carry_ref[...]


def route(logits_t, bias, lo, hi):
    t = _token_tile(logits_t.shape[1])
    n = hi - lo
    tile0 = lo // t
    tri = (jnp.arange(t)[:, None] < jnp.arange(t)[None, :]).astype(BF16)
    tok = lambda dt: jax.ShapeDtypeStruct((TOP_K, n), dt)
    tok_spec = pl.BlockSpec((TOP_K, t), lambda i: (0, i))
    return pl.pallas_call(
        _route_kernel,
        grid=(n // t,),
        in_specs=[pl.BlockSpec((N_EXPERTS, t), lambda i: (0, tile0 + i)),
                  pl.BlockSpec((N_EXPERTS, 1), lambda i: (0, 0)),
                  pl.BlockSpec((t, t), lambda i: (0, 0))],
        out_specs=[tok_spec, tok_spec, tok_spec, pl.BlockSpec((N_EXPERTS, 1), lambda i: (0, 0))],
        out_shape=[tok(jnp.int32), tok(F32), tok(jnp.int32), jax.ShapeDtypeStruct((N_EXPERTS, 1), F32)],
        scratch_shapes=[pltpu.VMEM((N_EXPERTS, 1), F32)],
        compiler_params=_params(("arbitrary",)),
        name="route",
    )(logits_t, bias.reshape(N_EXPERTS, 1), tri)


def _slot_kernel(start_ref, eidx_ref, rank_ref, dest_ref):
    e = eidx_ref[...]
    d = rank_ref[...]
    for x in range(N_EXPERTS):
        d = d + jnp.where(e == x, start_ref[x], 0)
    dest_ref[...] = d


def slot_index(pad_start, eidx, rank):
    n = eidx.shape[1]
    t = _token_tile(n)
    spec = pl.BlockSpec((TOP_K, t), lambda i: (0, i))
    return pl.pallas_call(
        _slot_kernel,
        grid=(n // t,),
        in_specs=[pl.BlockSpec(memory_space=pltpu.SMEM), spec, spec],
        out_specs=spec,
        out_shape=jax.ShapeDtypeStruct((TOP_K, n), jnp.int32),
        compiler_params=_params(("parallel",)),
        name="slot_index",
    )(pad_start, eidx, rank)


def _sc_worker():
    return lax.axis_index("s") * SC_CORES + lax.axis_index("c")


def sc_dispatch(h, dest3, n_slots, row0):
    d = h.shape[1]
    n_win = dest3.shape[0]
    mesh = plsc.VectorSubcoreMesh(core_axis_name="c", subcore_axis_name="s")

    @functools.partial(
        pl.kernel, mesh=mesh, out_type=jax.ShapeDtypeStruct((n_slots, d), h.dtype),
        scratch_types=[pltpu.VMEM((TOP_K, SC_WINDOW), jnp.int32), pltpu.VMEM((SC_WINDOW, d), h.dtype),
                       pltpu.SemaphoreType.DMA])
    def k(h_hbm, dest_hbm, out_hbm, idx_v, rows_v, sem):
        wid = _sc_worker()

        @pl.loop(0, -(-n_win // SC_WORKERS))
        def _(it):
            w = it * SC_WORKERS + wid

            @pl.when(w < n_win)
            def _():
                pltpu.sync_copy(dest_hbm.at[w], idx_v)
                pltpu.sync_copy(h_hbm.at[pl.ds(row0 + w * SC_WINDOW, SC_WINDOW)], rows_v)
                copies = [pltpu.async_copy(rows_v, out_hbm.at[idx_v.at[j]], sem) for j in range(TOP_K)]
                for c in copies:
                    c.wait()

    return k(h, dest3)


def sc_combine_gather(y, dest3):
    d = y.shape[1]
    n_win = dest3.shape[0]
    n = n_win * SC_WINDOW
    mesh = plsc.VectorSubcoreMesh(core_axis_name="c", subcore_axis_name="s")

    @functools.partial(
        pl.kernel, mesh=mesh, out_type=jax.ShapeDtypeStruct((TOP_K, n, d), y.dtype),
        scratch_types=[pltpu.VMEM((TOP_K, SC_WINDOW), jnp.int32), pltpu.VMEM((2, SC_WINDOW, d), y.dtype),
                       pltpu.SemaphoreType.DMA, pltpu.SemaphoreType.DMA,
                       pltpu.SemaphoreType.DMA, pltpu.SemaphoreType.DMA])
    def k(y_hbm, dest_hbm, out_hbm, idx_v, rows_v, gsem0, gsem1, osem0, osem1):
        wid = _sc_worker()
        gsem, osem = (gsem0, gsem1), (osem0, osem1)

        @pl.loop(0, -(-n_win // SC_WORKERS))
        def _(it):
            w = it * SC_WORKERS + wid

            @pl.when(w < n_win)
            def _():
                pltpu.sync_copy(dest_hbm.at[w], idx_v)
                gather = lambda j: pltpu.async_copy(y_hbm.at[idx_v.at[j]], rows_v.at[j % 2], gsem[j % 2])
                g = [None] * TOP_K
                o = [None] * TOP_K
                g[0] = gather(0)
                for j in range(TOP_K):
                    if j + 1 < TOP_K:
                        if j >= 1:
                            o[j - 1].wait()
                        g[j + 1] = gather(j + 1)
                    g[j].wait()
                    o[j] = pltpu.async_copy(rows_v.at[j % 2], out_hbm.at[j, pl.ds(w * SC_WINDOW, SC_WINDOW)],
                                            osem[j % 2])
                o[TOP_K - 2].wait()
                o[TOP_K - 1].wait()

    return k(y, dest3)


def _expert_ffn_kernel(be_ref, bv_ref, x_ref, wg_ref, wu_ref, wd_ref, o_ref, wg_s, wu_s, wd_s):
    b = pl.program_id(0)
    valid = bv_ref[b]
    new_expert = (b == 0) | (be_ref[b] != be_ref[jnp.maximum(b - 1, 0)])

    @pl.when(new_expert)
    def _():
        wg_s[...] = wg_ref[0, 0].astype(BF16)
        wu_s[...] = wu_ref[0, 0].astype(BF16)
        wd_s[...] = wd_ref[0, 0].astype(BF16)

    sub = x_ref.shape[0] // MOE_SUB_BLOCKS

    def sub_block(r):
        row = lax.broadcasted_iota(jnp.int32, (sub, 1), 0) + r * sub
        rows = pl.ds(r * sub, sub)
        lo, hi = _unpack_bf16_pair(jnp.where(row < valid, x_ref[rows, :], 0))
        x = jnp.concatenate([lo.astype(BF16), hi.astype(BF16)], axis=1)
        a = jnp.dot(x, wg_s[...], preferred_element_type=F32)
        a = a * jax.nn.sigmoid(a) * jnp.dot(x, wu_s[...], preferred_element_type=F32)
        y = jnp.dot(a.astype(BF16), wd_s[...], preferred_element_type=F32)
        half = y.shape[1] // 2
        o_ref[rows, :] = _pack_bf16_pair(y[:, :half], y[:, half:])

    for live in range(1, MOE_SUB_BLOCKS + 1):
        upper = valid <= live * sub if live < MOE_SUB_BLOCKS else True

        @pl.when((valid > (live - 1) * sub) & upper)
        def _():
            for r in range(live):
                sub_block(r)


def expert_ffn(xg, block_expert, block_valid, wg, wu, wd, layer):
    n_slots, dp = xg.shape
    d, f = wg.shape[-2:]
    grid_spec = pltpu.PrefetchScalarGridSpec(
        num_scalar_prefetch=2,
        grid=(n_slots // MOE_BLOCK,),
        in_specs=[pl.BlockSpec((MOE_BLOCK, dp), lambda b, be, bv: (b, 0)),
                  pl.BlockSpec((1, 1, d, f), lambda b, be, bv: (layer, be[b], 0, 0)),
                  pl.BlockSpec((1, 1, d, f), lambda b, be, bv: (layer, be[b], 0, 0)),
                  pl.BlockSpec((1, 1, f, d), lambda b, be, bv: (layer, be[b], 0, 0))],
        out_specs=pl.BlockSpec((MOE_BLOCK, dp), lambda b, be, bv: (b, 0)),
        scratch_shapes=[pltpu.VMEM((d, f), BF16), pltpu.VMEM((d, f), BF16), pltpu.VMEM((f, d), BF16)],
    )
    return pl.pallas_call(
        _expert_ffn_kernel,
        grid_spec=grid_spec,
        out_shape=jax.ShapeDtypeStruct((n_slots, dp), jnp.int32),
        compiler_params=_params(("arbitrary",)),
        name="expert_ffn",
    )(block_expert, block_valid, xg, wg, wu, wd)


def _combine_kernel(yg_ref, w_ref, h_ref, swg_ref, swu_ref, swd_ref, *rest):
    o_ref = rest[-1]
    h = h_ref[...]
    a = jnp.dot(h, swg_ref[...], preferred_element_type=F32)
    a = a * jax.nn.sigmoid(a) * jnp.dot(h, swu_ref[...], preferred_element_type=F32)
    acc = jnp.dot(a.astype(BF16), swd_ref[...], preferred_element_type=F32)
    half = acc.shape[1] // 2
    acc_lo, acc_hi = acc[:, :half], acc[:, half:]
    wt = w_ref[...].T
    for k in range(TOP_K):
        lo, hi = _unpack_bf16_pair(yg_ref[k])
        acc_lo = acc_lo + wt[:, k:k + 1] * lo
        acc_hi = acc_hi + wt[:, k:k + 1] * hi
    o_ref[:, :half] = acc_lo
    o_ref[:, half:] = acc_hi


def combine(yg, w, h, swg, swu, swd, lo, prev):
    n_all, d = h.shape
    n = w.shape[1]
    f = swg.shape[-1]
    tm = ROW_TILE
    tile0 = lo // tm
    in_specs = [pl.BlockSpec((TOP_K, tm, d // 2), lambda i: (0, i, 0)),
                pl.BlockSpec((TOP_K, tm), lambda i: (0, i)),
                pl.BlockSpec((tm, d), lambda i: (tile0 + i, 0)),
                pl.BlockSpec((d, f), lambda i: (0, 0)),
                pl.BlockSpec((d, f), lambda i: (0, 0)),
                pl.BlockSpec((f, d), lambda i: (0, 0))]
    args = [yg, w, h, swg, swu, swd]
    aliases = {}
    if prev is not None:
        in_specs.append(pl.BlockSpec(memory_space=pl.ANY))
        args.append(prev)
        aliases = {len(args) - 1: 0}
    return pl.pallas_call(
        _combine_kernel,
        grid=(n // tm,),
        in_specs=in_specs,
        out_specs=pl.BlockSpec((tm, d), lambda i: (tile0 + i, 0)),
        out_shape=jax.ShapeDtypeStruct((n_all, d), F32),
        input_output_aliases=aliases,
        compiler_params=_params(("parallel",)),
        name="moe_combine",
    )(*args)


def moe(h, h_packed, logits_t, bias, wg, wu, wd, layer, swg, swu, swd):
    n = h.shape[0]
    t = _token_tile(n)
    cut = (n // t + 1) // 2 * t
    shared = (swg.astype(BF16), swu.astype(BF16), swd.astype(BF16))
    staged = [_moe_experts(h_packed, logits_t, bias, wg, wu, wd, layer, lo, hi) for lo, hi in ((0, cut), (cut, n))]
    out = None
    for (yg, w), lo in zip(staged, (0, cut)):
        out = combine(yg, w, h, *shared, lo, out)
    return out


def _moe_experts(h_packed, logits_t, bias, wg, wu, wd, layer, lo, hi):
    n = hi - lo
    eidx, w, rank, counts = route(logits_t, bias, lo, hi)
    counts = counts.reshape(N_EXPERTS).astype(jnp.int32)
    padded = (counts + MOE_BLOCK - 1) // MOE_BLOCK * MOE_BLOCK
    pad_end = jnp.cumsum(padded)
    pad_start = pad_end - padded
    n_slots = n * TOP_K + N_EXPERTS * MOE_BLOCK
    starts = jnp.arange(n_slots // MOE_BLOCK, dtype=jnp.int32) * MOE_BLOCK
    owner = jnp.sum((pad_end[None, :] <= starts[:, None]).astype(jnp.int32), axis=1)
    block_expert = jnp.minimum(owner, N_EXPERTS - 1)
    member = (block_expert[:, None] == jnp.arange(N_EXPERTS, dtype=jnp.int32)[None, :]).astype(jnp.int32)
    left = jnp.sum(member * (counts + pad_start)[None, :], axis=1) - starts
    block_valid = jnp.clip(left, 0, MOE_BLOCK).astype(jnp.int32)
    dest = slot_index(pad_start.astype(jnp.int32), eidx, rank)
    dest3 = dest.reshape(TOP_K, n // SC_WINDOW, SC_WINDOW).transpose(1, 0, 2)
    xg = sc_dispatch(h_packed, dest3, n_slots, lo)
    y = expert_ffn(xg, block_expert, block_valid, wg, wu, wd, layer)
    return sc_combine_gather(y, dest3), w


def mixer_ab(h, w_in, w_out, decay_logit, conv_w, conv_b, w1, b1, f1, w2, b2, f2, w3, skip, rope, dft, *, seq,
             ctx_len):
    n_rows = h.shape[0]
    tm = _token_tile(n_rows)
    p = mm([(h, w_in.astype(BF16))], F32, tm, 512, name="ab_in_proj")
    qkv_w = 2 * RET_QK + RET_V
    n_qk = 2 * RET_QK // LANE
    scales = (1.0,) * (RET_QK // LANE) + (RET_DK ** -0.5,) * (RET_QK // LANE) + (1.0,) * (RET_V // LANE)
    qkv = rope_cast(p, rope[0], rope[1], width=qkv_w, n_rot_blocks=n_qk, head_dim=RET_DK, scales=scales,
                    out_dtype=F32)
    log_g = jax.nn.log_sigmoid(decay_logit.astype(F32))
    ret = retention(qkv, p, log_g, jnp.exp(RET_CHUNK * log_g), seq=seq)
    v, x1, x2 = shortconv(p, conv_w, conv_b, seq=seq)
    filt = (w1, b1, f1, w2, b2, f2, w3)
    hy_x = long_conv_two_stage(dft["x"], hyena_filter_taps(seq, *filt), v, x1, x2, skip, seq)
    hy_c = long_conv_one_stage(dft["c"], hyena_filter_taps(ctx_len, *filt), v[seq:], x1[seq:], x2[seq:], skip)
    hy = jnp.concatenate([hy_x, hy_c], axis=0)
    w_out = w_out.astype(BF16)
    return mm([(ret, w_out[:RET_V]), (hy, w_out[RET_V:])], F32, tm, 512, name="ab_out_proj")


def mixer_da(h, w_in, w_out, lam, subln, lambda_init, rope, *, seq, ctx_len):
    n_rows = h.shape[0]
    tm = _token_tile(n_rows)
    p = mm([(h, w_in.astype(BF16))], F32, tm, 512, name="da_in_proj")
    qt, k, vt = rope_da(p, rope[0], rope[1])
    lam_f = lam.astype(F32)
    lam_full = jnp.exp(jnp.sum(lam_f[0] * lam_f[1])) - jnp.exp(jnp.sum(lam_f[2] * lam_f[3])) + lambda_init
    o = diff_attention(qt, k, vt, lam_full, subln, seq=seq, ctx_len=ctx_len, lambda_init=lambda_init)
    return mm([(o, w_out.astype(BF16))], F32, tm, 512, name="da_out_proj")


def kernel(x, c, ctx, c_ctx, w_ada, b_ada, norm_mix, norm_ffn, ab_w_in, ab_w_out, ret_decay_logit, hy_conv_w, hy_conv_b, hy_w1, hy_b1, hy_freq1, hy_w2, hy_b2, hy_freq2, hy_w3, hy_skip, da_w_in, da_w_out, da_lambda, da_subln, router_w, router_b, exp_w_gate, exp_w_up, exp_w_down, sh_w_gate, sh_w_up, sh_w_down, norm_final):
    batch, seq, d = x.shape
    ctx_len = ctx.shape[1]
    assert batch == 1 and seq % ROW_TILE == 0 and ctx_len == ROW_TILE
    depth = w_ada.shape[0]
    n_rows = seq + ctx_len
    ctx_tile = seq // ROW_TILE

    xs = jnp.concatenate([x[0], ctx[0]], axis=0)
    cv = jnp.zeros((SUBLANE, d), F32).at[0].set(c_ctx).at[1].set(c[0])
    mods = adaln(cv, w_ada, b_ada)[:, :2].reshape(depth, 2, 6, d)

    rope_ret = rope_tables(seq, ctx_len, RET_DK)
    rope_da = rope_tables(seq, ctx_len, DA_HEAD_DIM)
    dft = dict(x=dft_tables_two_stage(2 * seq), c=dft_tables_one_stage(2 * ctx_len))
    common = dict(n_rows=n_rows, ctx_tile=ctx_tile)

    delta, gate_mods = None, None
    for i in range(depth):
        j = i // 2
        if delta is None:
            (h,) = norm_mod(xs, norm_mix[i], mods=mods[i], shift_idx=0, scale_idx=1, **common)
        else:
            xs, h = norm_mod(xs, norm_mix[i], delta=delta, gate_mods=gate_mods, gate_idx=5, mods=mods[i], shift_idx=0,
                             scale_idx=1, **common)
        if i % 2 == 0:
            y = mixer_ab(h, ab_w_in[j], ab_w_out[j], ret_decay_logit[j], hy_conv_w[j], hy_conv_b[j], hy_w1[j],
                         hy_b1[j], hy_freq1[j], hy_w2[j], hy_b2[j], hy_freq2[j], hy_w3[j], hy_skip[j], rope_ret, dft,
                         seq=seq, ctx_len=ctx_len)
        else:
            lambda_init = 0.8 - 0.6 * math.exp(-0.3 * i)
            y = mixer_da(h, da_w_in[j], da_w_out[j], da_lambda[j], da_subln[j], lambda_init, rope_da, seq=seq,
                         ctx_len=ctx_len)
        xs, h, logits_t, h_packed = norm_mod(xs, norm_ffn[i], delta=y, gate_mods=mods[i], gate_idx=2, mods=mods[i],
                                             shift_idx=3, scale_idx=4, router_wt=router_w[i].T, **common)
        delta = moe(h, h_packed, logits_t, router_b[i], exp_w_gate, exp_w_up, exp_w_down, i,
                    sh_w_gate[i], sh_w_up[i], sh_w_down[i])
        gate_mods = mods[i]
    _, out = norm_mod(xs, norm_final, n_rows=seq, ctx_tile=ctx_tile, delta=delta, gate_mods=gate_mods, gate_idx=5,
                      out_dtype=F32)
    return out[None]
```

```python
import functools
import math

import jax
import jax.numpy as jnp
from jax import lax
from jax.experimental import pallas as pl
from jax.experimental.pallas import tpu as pltpu
from jax.experimental.pallas import tpu_sc as plsc

F32 = jnp.float32
BF16 = jnp.bfloat16
HIGHEST = lax.Precision.HIGHEST

D_MODEL = 1024
DEPTH = 4
GRID_W = 64
EPS = 1e-6
ROPE_BASE = 10000.0

RET_HEADS = 4
RET_DK = 128
RET_DV = 256
RET_CHUNK = 128
RET_QK = RET_HEADS * RET_DK
RET_V = RET_HEADS * RET_DV

HY_WIDTH = 512
HY_ORDER = 2
HY_BANDS = 16
HY_EMB = 2 * HY_BANDS + 1
HY_FFN = 64
HY_DECAY_TARGET = 1e-2
HY_FAST_DECAY = 0.3
HY_SLOW_DECAY = 1.5
HY_ZCOLS = 64
HY_VALID_COL = HY_EMB
FFT_N2 = 128

AB_IN = 2 * RET_QK + 2 * RET_V + (HY_ORDER + 1) * HY_WIDTH
AB_CAT = RET_V + HY_WIDTH

DA_HEADS = 8
DA_HEAD_DIM = 64
DA_WIDTH = DA_HEADS * 2 * DA_HEAD_DIM

N_EXPERTS = 64
TOP_K = 8
N_GROUPS = 8
TOPK_GROUPS = 4
GROUP_SIZE = N_EXPERTS // N_GROUPS
EXPERT_DIM = 256
ROUTED_SCALE = 2.5
MOE_BLOCK = 512
MOE_SUB_BLOCKS = 2
SC_CORES = 2
SC_SUBCORES = 16
SC_WORKERS = SC_CORES * SC_SUBCORES
SC_WINDOW = 64

LANE = 128
SUBLANE = 8
ROW_TILE = 256
MAX_TOKEN_TILE = 1280
VMEM_LIMIT = 48 * 1024 * 1024
FLASH_VMEM_LIMIT = 56 * 1024 * 1024
NEG_BIG = -1e30
LOG2_E = 1.4426950408889634
FLASH_ONES_ROWS = 16
FLASH_INIT_KEYS = 16
FLASH_LAZY_HEADROOM = 60.0


def _params(sem):
    return pltpu.CompilerParams(dimension_semantics=sem, vmem_limit_bytes=VMEM_LIMIT)


def _token_tile(n):
    best = ROW_TILE
    t = ROW_TILE
    while t <= min(n, MAX_TOKEN_TILE):
        if n % t == 0:
            best = t
        t += ROW_TILE
    return best


def _row_mod(mods_ref, idx, row0, n, seq):
    row = row0 + lax.broadcasted_iota(jnp.int32, (n, 1), 0)
    return jnp.where(row >= seq, mods_ref[0, idx:idx + 1, :], mods_ref[1, idx:idx + 1, :])


def _mm_kernel(*refs, n_pairs, has_epi, gate_idx, seq):
    acc = None
    for p in range(n_pairs):
        a = refs[2 * p][...].astype(BF16)
        b = refs[2 * p + 1][...].astype(BF16)
        d = jnp.dot(a, b, preferred_element_type=F32)
        acc = d if acc is None else acc + d
    idx = 2 * n_pairs
    if has_epi:
        acc = refs[idx][...] * (acc + refs[idx + 1][...] * refs[idx + 2][...])
        idx += 3
    if gate_idx is not None:
        tm = acc.shape[0]
        acc = refs[idx][...] + _row_mod(refs[idx + 1], gate_idx, pl.program_id(0) * tm, tm, seq) * acc
        idx += 2
    o_ref = refs[idx]
    o_ref[...] = acc.astype(o_ref.dtype)


def mm(pairs, out_dtype, tm, tn, epi=None, residual=None, name="mm"):
    m = pairs[0][0].shape[0]
    n = pairs[0][1].shape[1]
    assert m % tm == 0 and n % tn == 0
    in_specs, args = [], []
    for a, b in pairs:
        k = a.shape[1]
        in_specs += [pl.BlockSpec((tm, k), lambda i, j: (i, 0)), pl.BlockSpec((k, tn), lambda i, j: (0, j))]
        args += [a, b]
    if epi is not None:
        in_specs += [pl.BlockSpec((tm, tn), lambda i, j: (i, j)), pl.BlockSpec((1, tn), lambda i, j: (0, j)),
                     pl.BlockSpec((tm, tn), lambda i, j: (i, j))]
        args += list(epi)
    gate_idx = seq = None
    if residual is not None:
        res, mods, gate_idx, seq = residual
        in_specs += [pl.BlockSpec((tm, tn), lambda i, j: (i, j)), pl.BlockSpec((2, 6, tn), lambda i, j: (0, 0, j))]
        args += [res, mods]
    return pl.pallas_call(
        functools.partial(_mm_kernel, n_pairs=len(pairs), has_epi=epi is not None, gate_idx=gate_idx, seq=seq),
        grid=(m // tm, n // tn),
        in_specs=in_specs,
        out_specs=pl.BlockSpec((tm, tn), lambda i, j: (i, j)),
        out_shape=jax.ShapeDtypeStruct((m, n), out_dtype),
        compiler_params=_params(("parallel", "parallel")),
        name=name,
    )(*args)


def _adaln_kernel(cv_ref, w_ref, b_ref, o_ref):
    cv = cv_ref[...]
    s = cv * jax.nn.sigmoid(cv)
    o_ref[0] = jnp.dot(s, w_ref[0], precision=HIGHEST, preferred_element_type=F32) + b_ref[0]


def adaln(cv, w_ada, b_ada):
    depth, d, n = w_ada.shape
    tn = 1536
    return pl.pallas_call(
        _adaln_kernel,
        grid=(depth, n // tn),
        in_specs=[pl.BlockSpec((SUBLANE, d), lambda l, j: (0, 0)),
                  pl.BlockSpec((1, d, tn), lambda l, j: (l, 0, j)),
                  pl.BlockSpec((1, 1, tn), lambda l, j: (l, 0, j))],
        out_specs=pl.BlockSpec((1, SUBLANE, tn), lambda l, j: (l, 0, j)),
        out_shape=jax.ShapeDtypeStruct((depth, SUBLANE, n), F32),
        compiler_params=_params(("parallel", "parallel")),
        name="adaln",
    )(cv, w_ada, b_ada.reshape(depth, 1, n))


def _norm_mod_kernel(*refs, shift_idx, scale_idx, has_router, seq):
    it = iter(refs)
    x = next(it)[...]
    mods_ref = next(it) if shift_idx is not None else None
    g_ref = next(it)
    wr_ref = next(it) if has_router else None
    h_ref = next(it)
    y = x * lax.rsqrt(jnp.mean(x * x, axis=-1, keepdims=True) + EPS) * g_ref[...]
    if shift_idx is not None:
        tm = x.shape[0]
        row0 = pl.program_id(0) * tm
        y = y * (1.0 + _row_mod(mods_ref, scale_idx, row0, tm, seq)) + _row_mod(mods_ref, shift_idx, row0, tm, seq)
    h_ref[...] = y.astype(h_ref.dtype)
    if has_router:
        lg_ref = next(it)
        lg_ref[...] = lax.dot_general(wr_ref[...], y, (((1,), (1,)), ((), ())),
                                      precision=HIGHEST, preferred_element_type=F32)
        half = y.shape[1] // 2
        next(it)[...] = _pack_bf16_pair(y[:, :half], y[:, half:])


def _pack_bf16_pair(a, b):
    def rounded(x):
        u = lax.bitcast_convert_type(x, jnp.int32)
        return u + 0x7FFF + (lax.shift_right_logical(u, 16) & 1)
    return lax.shift_right_logical(rounded(a), 16) | (rounded(b) & -65536)


def _unpack_bf16_pair(w):
    return (lax.bitcast_convert_type(lax.shift_left(w, 16), F32),
            lax.bitcast_convert_type(w & -65536, F32))


def norm_mod(xs, g, *, n_rows, seq, mods=None, shift_idx=None, scale_idx=None, router_wt=None, out_dtype=BF16):
    d = xs.shape[1]
    tm = _token_tile(n_rows)
    row = pl.BlockSpec((tm, d), lambda i: (i, 0))
    in_specs, args = [row], [xs]
    if shift_idx is not None:
        in_specs.append(pl.BlockSpec((2, 6, d), lambda i: (0, 0, 0)))
        args.append(mods)
    in_specs.append(pl.BlockSpec((1, d), lambda i: (0, 0)))
    args.append(g.reshape(1, d))
    has_router = router_wt is not None
    if has_router:
        in_specs.append(pl.BlockSpec(router_wt.shape, lambda i: (0, 0)))
        args.append(router_wt)
    out_specs = [row]
    out_shape = [jax.ShapeDtypeStruct((n_rows, d), out_dtype)]
    if has_router:
        out_specs.append(pl.BlockSpec((N_EXPERTS, tm), lambda i: (0, i)))
        out_shape.append(jax.ShapeDtypeStruct((N_EXPERTS, n_rows), F32))
        out_specs.append(pl.BlockSpec((tm, d // 2), lambda i: (i, 0)))
        out_shape.append(jax.ShapeDtypeStruct((n_rows, d // 2), jnp.int32))
    return pl.pallas_call(
        functools.partial(_norm_mod_kernel, shift_idx=shift_idx, scale_idx=scale_idx, has_router=has_router, seq=seq),
        grid=(n_rows // tm,),
        in_specs=in_specs,
        out_specs=out_specs,
        out_shape=out_shape,
        compiler_params=_params(("parallel",)),
        name="norm_mod",
    )(*args)


def _rope_kernel(p_ref, cos_ref, sin_ref, o_ref, *, n_rot_blocks, head_dim, scales):
    cos = cos_ref[...]
    sin = sin_ref[...]
    for b in range(len(scales)):
        x = p_ref[:, b * LANE:(b + 1) * LANE]
        if b < n_rot_blocks:
            if head_dim == LANE:
                rot = pltpu.roll(x, LANE // 2, 1)
            else:
                lane = lax.broadcasted_iota(jnp.int32, x.shape, 1)
                first_half = (lane % head_dim) < head_dim // 2
                rot = jnp.where(first_half, pltpu.roll(x, LANE - head_dim // 2, 1), pltpu.roll(x, head_dim // 2, 1))
            x = x * cos + rot * sin
        if scales[b] != 1.0:
            x = x * scales[b]
        o_ref[:, b * LANE:(b + 1) * LANE] = x.astype(o_ref.dtype)


def rope_cast(p, cos, sin, *, width, n_rot_blocks, head_dim, scales, out_dtype):
    n_rows = p.shape[0]
    return pl.pallas_call(
        functools.partial(_rope_kernel, n_rot_blocks=n_rot_blocks, head_dim=head_dim, scales=scales),
        grid=(n_rows // ROW_TILE,),
        in_specs=[pl.BlockSpec((ROW_TILE, width), lambda i: (i, 0)),
                  pl.BlockSpec((ROW_TILE, LANE), lambda i: (i, 0)),
                  pl.BlockSpec((ROW_TILE, LANE), lambda i: (i, 0))],
        out_specs=pl.BlockSpec((ROW_TILE, width), lambda i: (i, 0)),
        out_shape=jax.ShapeDtypeStruct((n_rows, width), out_dtype),
        compiler_params=_params(("parallel",)),
        name="rope_cast",
    )(p, cos, sin)


def rope_tables(seq, ctx_len, head_dim):
    n_freq = head_dim // 4
    inv = ROPE_BASE ** (-jnp.arange(n_freq, dtype=F32) / n_freq)
    rows = seq // GRID_W
    row = jnp.repeat(jnp.arange(rows, dtype=F32), GRID_W)
    col = jnp.tile(jnp.arange(GRID_W, dtype=F32), rows)
    ang = jnp.concatenate([row[:, None] * inv, col[:, None] * inv], axis=-1)
    cos, sin = jnp.cos(ang), jnp.sin(ang)
    cos = jnp.concatenate([cos, cos], axis=-1)
    sin = jnp.concatenate([-sin, sin], axis=-1)
    reps = LANE // head_dim
    cos, sin = jnp.tile(cos, (1, reps)), jnp.tile(sin, (1, reps))
    cos = jnp.concatenate([cos, jnp.ones((ctx_len, LANE), F32)], axis=0)
    sin = jnp.concatenate([sin, jnp.zeros((ctx_len, LANE), F32)], axis=0)
    return cos, sin


def _ret_kernel(lg_ref, gc_ref, q_ref, k_ref, v_ref, *rest, reverse):
    if reverse:
        yf_ref, gate_ref, o_ref, s_ref = rest
    else:
        o_ref, s_ref = rest
    c = RET_CHUNK

    @pl.when(pl.program_id(0) == 0)
    def _():
        s_ref[...] = jnp.zeros_like(s_ref)

    ii = lax.broadcasted_iota(jnp.int32, (c, c), 0)
    jj = lax.broadcasted_iota(jnp.int32, (c, c), 1)
    rel = ((jj - ii) if reverse else (ii - jj)).astype(F32)
    pos = lax.broadcasted_iota(jnp.int32, (c, 1), 0).astype(F32)
    for h in range(RET_HEADS):
        lg = lg_ref[h]
        dec = jnp.where(rel >= 0, jnp.exp(jnp.maximum(rel, 0.0) * lg), 0.0)
        if reverse:
            q_dec = jnp.exp((c - pos) * lg)
            k_dec = jnp.exp(pos * lg)
        else:
            q_dec = jnp.exp((pos + 1.0) * lg)
            k_dec = jnp.exp((c - 1.0 - pos) * lg)
        q = q_ref[:, h * RET_DK:(h + 1) * RET_DK]
        k = k_ref[:, h * RET_DK:(h + 1) * RET_DK]
        v = v_ref[:, h * RET_DV:(h + 1) * RET_DV].astype(BF16)
        s = lax.dot_general(q.astype(BF16), k.astype(BF16), (((1,), (1,)), ((), ())),
                            preferred_element_type=F32) * dec
        state = s_ref[h]
        y = jnp.dot(s.astype(BF16), v, preferred_element_type=F32)
        y = y + jnp.dot((q * q_dec).astype(BF16), state.astype(BF16), preferred_element_type=F32)
        upd = lax.dot_general((k * k_dec).astype(BF16), v, (((0,), (0,)), ((), ())), preferred_element_type=F32)
        s_ref[h] = gc_ref[h] * state + upd
        if reverse:
            r = y + yf_ref[:, h * RET_DV:(h + 1) * RET_DV]
            mu = jnp.mean(r, axis=-1, keepdims=True)
            rc = r - mu
            var = jnp.mean(rc * rc, axis=-1, keepdims=True)
            g = gate_ref[:, h * RET_DV:(h + 1) * RET_DV]
            o_ref[:, h * RET_DV:(h + 1) * RET_DV] = (rc * lax.rsqrt(var + EPS) * (g * jax.nn.sigmoid(g))).astype(
                o_ref.dtype)
        else:
            o_ref[:, h * RET_DV:(h + 1) * RET_DV] = y


def retention(qkv, p, log_g, g_chunk, *, seq):
    n_rows = qkv.shape[0]
    n_chunks = n_rows // RET_CHUNK
    n_x = seq // RET_CHUNK
    smem = pl.BlockSpec(memory_space=pltpu.SMEM)

    def run(reverse, extra):
        if reverse:
            idx = lambda t: n_chunks - 1 - t
        else:
            idx = lambda t: (t + n_x) % n_chunks
        in_specs = [smem, smem,
                    pl.BlockSpec((RET_CHUNK, RET_QK), lambda t: (idx(t), 0)),
                    pl.BlockSpec((RET_CHUNK, RET_QK), lambda t: (idx(t), 1)),
                    pl.BlockSpec((RET_CHUNK, RET_V), lambda t: (idx(t), 1))]
        args = [log_g[1 if reverse else 0], g_chunk[1 if reverse else 0], qkv, qkv, qkv]
        if reverse:
            in_specs += [pl.BlockSpec((RET_CHUNK, RET_V), lambda t: (idx(t), 0)),
                         pl.BlockSpec((RET_CHUNK, RET_V), lambda t: (idx(t), 2))]
            args += list(extra)
        return pl.pallas_call(
            functools.partial(_ret_kernel, reverse=reverse),
            grid=(n_chunks,),
            in_specs=in_specs,
            out_specs=pl.BlockSpec((RET_CHUNK, RET_V), lambda t: (idx(t), 0)),
            out_shape=jax.ShapeDtypeStruct((n_rows, RET_V), BF16 if reverse else F32),
            scratch_shapes=[pltpu.VMEM((RET_HEADS, RET_DK, RET_DV), F32)],
            compiler_params=_params(("arbitrary",)),
            name="retention_bwd" if reverse else "retention_fwd",
        )(*args)

    y_fwd = run(False, None)
    return run(True, (y_fwd, p))


def _shortconv_kernel(cur_ref, prev_ref, next_ref, w_ref, b_ref, v_ref, x1_ref, x2_ref, *, x_tiles):
    i = pl.program_id(0)
    cur = cur_ref[...]
    rows = cur.shape[0]
    row = lax.broadcasted_iota(jnp.int32, (rows, 1), 0)
    has_prev = jnp.where((i == 0) | (i == x_tiles), 0.0, 1.0)
    has_next = jnp.where((i == x_tiles - 1) | (i == x_tiles), 0.0, 1.0)
    up = jnp.where(row == 0, prev_ref[SUBLANE - 1:SUBLANE, :] * has_prev, pltpu.roll(cur, 1, 0))
    dn = jnp.where(row == rows - 1, next_ref[0:1, :] * has_next, pltpu.roll(cur, rows - 1, 0))
    y = up * w_ref[0:1, :] + cur * w_ref[1:2, :] + dn * w_ref[2:3, :] + b_ref[...]
    v_ref[...] = y[:, :HY_WIDTH]
    x1_ref[...] = y[:, HY_WIDTH:2 * HY_WIDTH]
    x2_ref[...] = y[:, 2 * HY_WIDTH:]


def shortconv(p, w, b, *, seq):
    n_rows = p.shape[0]
    width = 3 * HY_WIDTH
    col = p.shape[1] // width - 1
    per = ROW_TILE // SUBLANE
    last = n_rows // SUBLANE - 1
    out = jax.ShapeDtypeStruct((n_rows, HY_WIDTH), F32)
    ospec = pl.BlockSpec((ROW_TILE, HY_WIDTH), lambda i: (i, 0))
    return pl.pallas_call(
        functools.partial(_shortconv_kernel, x_tiles=seq // ROW_TILE),
        grid=(n_rows // ROW_TILE,),
        in_specs=[pl.BlockSpec((ROW_TILE, width), lambda i: (i, col)),
                  pl.BlockSpec((SUBLANE, width), lambda i: (jnp.maximum(i * per - 1, 0), col)),
                  pl.BlockSpec((SUBLANE, width), lambda i: (jnp.minimum((i + 1) * per, last), col)),
                  pl.BlockSpec((3, width), lambda i: (0, 0)),
                  pl.BlockSpec((1, width), lambda i: (0, 0))],
        out_specs=[ospec, ospec, ospec],
        out_shape=[out, out, out],
        compiler_params=_params(("parallel",)),
        name="shortconv",
    )(p, p, p, w, b.reshape(1, width))


def _filt_kernel(z_ref, w1_ref, b1_ref, f1_ref, w2_ref, b2_ref, f2_ref, w3a_ref, w3b_ref, dl_ref, *o_ref):
    z = z_ref[...]
    h = jnp.sin(f1_ref[...] * (jnp.dot(z, w1_ref[...], precision=HIGHEST, preferred_element_type=F32) + b1_ref[...]))
    h = jnp.sin(f2_ref[...] * (jnp.dot(h, w2_ref[...], precision=HIGHEST, preferred_element_type=F32) + b2_ref[...]))
    window = jnp.exp(-z[:, 0:1] * dl_ref[...]) * z[:, HY_VALID_COL:HY_VALID_COL + 1]
    for o, w3_ref in enumerate((w3a_ref, w3b_ref)):
        o_ref[o][...] = jnp.dot(h, w3_ref[...], precision=HIGHEST, preferred_element_type=F32) * window


def hyena_filter_taps(length, w1, b1, f1, w2, b2, f2, w3):
    z = _filter_positions(length)
    w1p = jnp.zeros((HY_ZCOLS, HY_FFN), F32).at[:HY_EMB].set(w1)
    deltas = jnp.abs(jnp.linspace(math.log(HY_DECAY_TARGET) / HY_SLOW_DECAY,
                                  math.log(HY_DECAY_TARGET) / HY_FAST_DECAY, HY_WIDTH, dtype=F32)).reshape(1, HY_WIDTH)
    tm = min(length, 512)
    half_tiles = length // tm
    vec = lambda a: a.reshape(1, HY_FFN)
    small = lambda shape: pl.BlockSpec(shape, lambda i: (0, 0))
    w3_spec = lambda o: pl.BlockSpec((HY_FFN, HY_WIDTH), lambda i: (0, 2 * o + jnp.where(i >= half_tiles, 1, 0)))
    assert HY_ORDER == 2
    return pl.pallas_call(
        _filt_kernel,
        grid=(2 * half_tiles,),
        in_specs=[pl.BlockSpec((tm, HY_ZCOLS), lambda i: (i, 0)),
                  small((HY_ZCOLS, HY_FFN)), small((1, HY_FFN)), small((1, HY_FFN)),
                  small((HY_FFN, HY_FFN)), small((1, HY_FFN)), small((1, HY_FFN)),
                  w3_spec(0), w3_spec(1), small((1, HY_WIDTH))],
        out_specs=[pl.BlockSpec((tm, HY_WIDTH), lambda i: (i, 0))] * HY_ORDER,
        out_shape=[jax.ShapeDtypeStruct((2 * length, HY_WIDTH), F32)] * HY_ORDER,
        compiler_params=_params(("parallel",)),
        name="hyena_filter",
    )(z, w1p, vec(b1), vec(f1), w2, vec(b2), vec(f2), w3, w3, deltas)


def _filter_positions(length):
    t = jnp.concatenate([jnp.arange(length, dtype=F32), float(length) - jnp.arange(length, dtype=F32)])
    valid = jnp.ones((2 * length,), F32).at[length].set(0.0)
    t_norm = t / max(length - 1, 1)
    bands = jnp.linspace(1e-4, HY_BANDS - 1, HY_BANDS, dtype=F32)
    ang = (2.0 * math.pi / length) * t[:, None] * bands[None, :]
    z = jnp.concatenate([t_norm[:, None], jnp.cos(ang), -jnp.sin(ang), valid[:, None]], axis=-1)
    return jnp.pad(z, ((0, 0), (0, HY_ZCOLS - z.shape[1])))


def _angles(num, den):
    return (2.0 * math.pi / den) * (num % den).astype(F32)


def dft_tables_two_stage(m):
    n2 = FFT_N2
    n1 = m // n2
    half = n1 // 2
    kp = -(-(half + 1) // SUBLANE) * SUBLANE
    k1 = jnp.arange(kp, dtype=jnp.int32)
    live = (k1 <= half)
    a1 = _angles(k1[:, None] * jnp.arange(n1, dtype=jnp.int32)[None, :], n1)
    f1 = jnp.concatenate([jnp.where(live[:, None], jnp.cos(a1), 0.0), jnp.where(live[:, None], -jnp.sin(a1), 0.0)], 0)
    wgt = jnp.where((k1 == 0) | (k1 == half), 1.0, 2.0) * live / m
    a1h = a1[:, :half].T
    cinv = jnp.concatenate([jnp.cos(a1h) * wgt[None, :], -jnp.sin(a1h) * wgt[None, :]], axis=1)
    k = k1[:, None, None] + n1 * jnp.arange(n2, dtype=jnp.int32)[None, :, None]
    th = _angles(k * jnp.arange(n2, dtype=jnp.int32)[None, None, :], m)
    c = jnp.where(live[:, None, None], jnp.cos(th), 0.0)
    s = jnp.where(live[:, None, None], jnp.sin(th), 0.0)
    g_fwd = jnp.concatenate([jnp.concatenate([c, s], 2), jnp.concatenate([-s, c], 2)], 1)
    ct, st = jnp.swapaxes(c, 1, 2), jnp.swapaxes(s, 1, 2)
    g_inv = jnp.concatenate([jnp.concatenate([ct, -st], 2), jnp.concatenate([st, ct], 2)], 1)
    return dict(n1=n1, kp=kp, f1=f1.astype(BF16), f1_half=f1[:, :half].astype(BF16), cinv=cinv.astype(BF16),
                g_fwd=g_fwd.astype(BF16), g_inv=g_inv.astype(BF16))


def dft_tables_one_stage(m):
    half = m // 2
    kp = -(-(half + 1) // SUBLANE) * SUBLANE
    k = jnp.arange(kp, dtype=jnp.int32)
    live = (k <= half)
    a = _angles(k[:, None] * jnp.arange(m, dtype=jnp.int32)[None, :], m)
    f = jnp.concatenate([jnp.where(live[:, None], jnp.cos(a), 0.0), jnp.where(live[:, None], -jnp.sin(a), 0.0)], 0)
    wgt = jnp.where((k == 0) | (k == half), 1.0, 2.0) * live / m
    ah = a[:, :half].T
    cinv = jnp.concatenate([jnp.cos(ah) * wgt[None, :], -jnp.sin(ah) * wgt[None, :]], axis=1)
    return dict(kp=kp, f=f.astype(BF16), f_half=f[:, :half].astype(BF16), cinv=cinv.astype(BF16))


def _bmm_kernel(*refs, kb, in_part_major, out_part_major, has_h):
    if has_h:
        g_ref, a_ref, h_ref, o_ref = refs
    else:
        g_ref, a_ref, o_ref = refs
    n2 = FFT_N2
    for b in range(kb):
        if in_part_major:
            ar, ai = a_ref[0, b], a_ref[1, b]
        else:
            ar, ai = a_ref[b, 0], a_ref[b, 1]
        if has_h:
            hr, hi = h_ref[b, 0], h_ref[b, 1]
            ar, ai = ar * hr - ai * hi, ar * hi + ai * hr
        xin = jnp.concatenate([ar, ai], axis=0).astype(BF16)
        y = jnp.dot(g_ref[b], xin, preferred_element_type=F32)
        if out_part_major:
            o_ref[0, b] = y[:n2]
            o_ref[1, b] = y[n2:]
        else:
            o_ref[b, 0] = y[:n2]
            o_ref[b, 1] = y[n2:]


def bmm_k1(g, a, h=None, *, in_part_major, out_part_major):
    kp = g.shape[0]
    n2 = FFT_N2
    c = a.shape[-1]
    kb, tc = SUBLANE, 256
    pm = lambda: pl.BlockSpec((2, kb, n2, tc), lambda i, j: (0, i, 0, j))
    km = lambda: pl.BlockSpec((kb, 2, n2, tc), lambda i, j: (i, 0, 0, j))
    in_specs = [pl.BlockSpec((kb, 2 * n2, 2 * n2), lambda i, j: (i, 0, 0)), pm() if in_part_major else km()]
    args = [g, a]
    if h is not None:
        in_specs.append(km())
        args.append(h)
    return pl.pallas_call(
        functools.partial(_bmm_kernel, kb=kb, in_part_major=in_part_major, out_part_major=out_part_major,
                          has_h=h is not None),
        grid=(kp // kb, c // tc),
        in_specs=in_specs,
        out_specs=pm() if out_part_major else km(),
        out_shape=jax.ShapeDtypeStruct((2, kp, n2, c) if out_part_major else (kp, 2, n2, c), F32),
        compiler_params=_params(("parallel", "parallel")),
        name="dft_inner",
    )(*args)


def _cmul_kernel(x_ref, h_ref, o_ref):
    xr, xi, hr, hi = x_ref[0], x_ref[1], h_ref[0], h_ref[1]
    o_ref[0] = xr * hr - xi * hi
    o_ref[1] = xr * hi + xi * hr


def cmul(x, h):
    spec = pl.BlockSpec(x.shape, lambda i: (0, 0, 0))
    return pl.pallas_call(_cmul_kernel, grid=(1,), in_specs=[spec, spec], out_specs=spec,
                          out_shape=jax.ShapeDtypeStruct(x.shape, F32), compiler_params=_params(("arbitrary",)),
                          name="spectrum_product")(x, h)


def _dft_outer3_kernel(f_ref, x_ref, o_ref):
    c = x_ref.shape[2]
    f = f_ref[...]
    for j in range(SUBLANE):
        o_ref[:, j * c:(j + 1) * c] = jnp.dot(f, x_ref[:, j, :].astype(BF16), preferred_element_type=F32)


def dft_outer3(f, x3, n_outer):
    rows = f.shape[0]
    c = x3.shape[2]
    return pl.pallas_call(
        _dft_outer3_kernel,
        grid=(FFT_N2 // SUBLANE,),
        in_specs=[pl.BlockSpec((rows, n_outer), lambda j: (0, 0)),
                  pl.BlockSpec((n_outer, SUBLANE, c), lambda j: (0, j, 0))],
        out_specs=pl.BlockSpec((rows, SUBLANE * c), lambda j: (0, j)),
        out_shape=jax.ShapeDtypeStruct((rows, FFT_N2 * c), F32),
        compiler_params=_params(("parallel",)),
        name="dft_outer",
    )(f, x3)


def _idft_gate3_kernel(cinv_ref, b_ref, gate_ref, skip_ref, u_ref, o_ref, *, u_is_3d):
    c = gate_ref.shape[2]
    cinv = cinv_ref[...]
    for j in range(SUBLANE):
        cols = slice(j * c, (j + 1) * c)
        acc = jnp.dot(cinv, b_ref[:, cols].astype(BF16), preferred_element_type=F32)
        u = u_ref[:, j, :] if u_is_3d else u_ref[:, cols]
        o_ref[:, cols] = gate_ref[:, j, :] * (acc + skip_ref[...] * u)


def idft_gate3(cinv, b2d, gate3, skip_row, u):
    n_outer = cinv.shape[0]
    c = gate3.shape[2]
    u_is_3d = u.ndim == 3
    wide = pl.BlockSpec((n_outer, SUBLANE * c), lambda j: (0, j))
    slab = pl.BlockSpec((n_outer, SUBLANE, c), lambda j: (0, j, 0))
    return pl.pallas_call(
        functools.partial(_idft_gate3_kernel, u_is_3d=u_is_3d),
        grid=(FFT_N2 // SUBLANE,),
        in_specs=[pl.BlockSpec(cinv.shape, lambda j: (0, 0)),
                  pl.BlockSpec((b2d.shape[0], SUBLANE * c), lambda j: (0, j)),
                  slab, pl.BlockSpec((1, c), lambda j: (0, 0)), slab if u_is_3d else wide],
        out_specs=wide,
        out_shape=jax.ShapeDtypeStruct((n_outer, FFT_N2 * c), F32),
        compiler_params=_params(("parallel",)),
        name="idft_outer_gate",
    )(cinv, b2d, gate3, skip_row, u)


def long_conv_two_stage(tabs, taps, v, x1, x2, skip, length):
    c = v.shape[1]
    n2, n1, kp = FFT_N2, tabs["n1"], tabs["kp"]
    as3 = lambda a: a.reshape(a.shape[0] // n2, n2, c)
    spectrum = lambda a2d: bmm_k1(tabs["g_fwd"], a2d.reshape(2, kp, n2, c), in_part_major=True, out_part_major=False)

    spectra = [spectrum(dft_outer3(tabs["f1"], as3(taps[o]), n1)) for o in range(HY_ORDER)]
    v3 = as3(v)
    u = v3
    for o, gate in enumerate((x1, x2)):
        if u.ndim == 3:
            a = dft_outer3(tabs["f1_half"], u, n1 // 2)
        else:
            a = mm([(tabs["f1_half"], u)], F32, 2 * kp, 2048, name="dft_outer")
        bt = bmm_k1(tabs["g_inv"], spectrum(a), spectra[o], in_part_major=False, out_part_major=True)
        u = idft_gate3(tabs["cinv"], bt.reshape(2 * kp, n2 * c), as3(gate), skip[o].reshape(1, c), u)
    return u.reshape(length, c)


def long_conv_one_stage(tabs, taps, v, x1, x2, skip):
    length, c = v.shape
    kp = tabs["kp"]
    u = v
    for o, gate in enumerate((x1, x2)):
        hs = mm([(tabs["f"], taps[o])], F32, 2 * kp, c, name="ctx_dft").reshape(2, kp, c)
        xs = mm([(tabs["f_half"], u)], F32, 2 * kp, c, name="ctx_dft").reshape(2, kp, c)
        ys = cmul(xs, hs).reshape(2 * kp, c)
        u = mm([(tabs["cinv"], ys)], F32, length, c, epi=(gate, skip[o].reshape(1, c), u), name="ctx_idft_gate")
    return u


def _flash_kernel(lam_ref, qt_ref, k_ref, vt_ref, sub_ref, o_ref, m_ref, acc_ref, *, kv, seq, ctx_len, out_scale):
    i = pl.program_id(1)
    last_q = pl.num_programs(1) - 1
    tq = qt_ref.shape[1]
    d = DA_HEAD_DIM
    dv = 2 * DA_HEAD_DIM
    n_chunks = k_ref.shape[0] // kv
    acc_ref[...] = jnp.zeros_like(acc_ref)

    def scores(off, rows, c, masked):
        s = jnp.dot(k_ref[pl.ds(off, rows), c * d:(c + 1) * d], qt_ref[c * d:(c + 1) * d, :],
                    preferred_element_type=F32)
        if masked:
            key = off + lax.broadcasted_iota(jnp.int32, (rows, 1), 0)
            lane = lax.broadcasted_iota(jnp.int32, (1, tq), 1)
            s = s + jnp.where(key < seq, NEG_BIG, 0.0) * jnp.where(lane >= tq - ctx_len, 1.0, 0.0)
        return s

    def exact_step(off, c, masked):
        s = scores(off, kv, c, masked)
        m_old = m_ref[c]
        m_new = jnp.maximum(m_old, jnp.max(s, axis=0, keepdims=True))
        pr = jnp.exp2(s - m_new).astype(BF16)
        acc_ref[c] = jnp.exp2(m_old - m_new) * acc_ref[c] + jnp.dot(vt_ref[:, pl.ds(off, kv)], pr,
                                                                   preferred_element_type=F32)
        m_ref[c] = m_new

    def lazy_step(off, c, masked):
        s = scores(off, kv, c, masked)
        m_old = m_ref[c]
        m_chunk = jnp.max(s, axis=0, keepdims=True)
        pv = jnp.dot(vt_ref[:, pl.ds(off, kv)], jnp.exp2(s - m_old).astype(BF16), preferred_element_type=F32)
        safe = jnp.max(m_chunk - m_old) <= FLASH_LAZY_HEADROOM

        @pl.when(safe)
        def _():
            m_new = jnp.maximum(m_old, m_chunk)
            acc_ref[c] = jnp.exp2(m_old - m_new) * (acc_ref[c] + pv)
            m_ref[c] = m_new

        @pl.when(jnp.logical_not(safe))
        def _():
            exact_step(off, c, masked)

    def run(masked):
        for c in range(2):
            m_ref[c] = jnp.max(scores(0, FLASH_INIT_KEYS, c, masked), axis=0, keepdims=True)

        def body(kc, carry):
            off = pl.multiple_of(kc * kv, kv)
            for c in range(2):
                lazy_step(off, c, masked)
            return carry

        lax.fori_loop(0, n_chunks, body, 0)

    @pl.when(i != last_q)
    def _():
        run(False)

    @pl.when(i == last_q)
    def _():
        run(True)

    a0 = acc_ref[0, :dv, :] / acc_ref[0, dv:dv + 1, :]
    a1 = acc_ref[1, :dv, :] / acc_ref[1, dv:dv + 1, :]
    o = (a0 - lam_ref[0] * a1).T
    o = o * lax.rsqrt(jnp.mean(o * o, axis=-1, keepdims=True) + 1e-5) * sub_ref[...]
    o_ref[...] = (o * out_scale).astype(o_ref.dtype)


def _rope_da_kernel(p_ref, cos_ref, sin_ref, qt_ref, k_ref, vt_ref):
    cos = cos_ref[...]
    sin = sin_ref[...]
    hw = 2 * DA_HEAD_DIM
    lane = lax.broadcasted_iota(jnp.int32, cos.shape, 1)
    first_half = (lane % DA_HEAD_DIM) < DA_HEAD_DIM // 2

    def rotated(b):
        x = p_ref[:, b * LANE:(b + 1) * LANE]
        rot = jnp.where(first_half, pltpu.roll(x, LANE - DA_HEAD_DIM // 2, 1), pltpu.roll(x, DA_HEAD_DIM // 2, 1))
        return x * cos + rot * sin

    ones = jnp.ones((FLASH_ONES_ROWS, cos.shape[0]), BF16)
    for h in range(DA_HEADS):
        qt_ref[h * hw:(h + 1) * hw, :] = (rotated(h) * (LOG2_E * DA_HEAD_DIM ** -0.5)).T.astype(BF16)
        k_ref[:, h * hw:(h + 1) * hw] = rotated(DA_HEADS + h).astype(BF16)
        base = h * (hw + FLASH_ONES_ROWS)
        vt_ref[base:base + hw, :] = p_ref[:, (2 * DA_HEADS + h) * LANE:(2 * DA_HEADS + h + 1) * LANE].T.astype(BF16)
        vt_ref[base + hw:base + hw + FLASH_ONES_ROWS, :] = ones


def rope_da(p, cos, sin):
    n_rows = p.shape[0]
    assert 2 * DA_HEAD_DIM == LANE
    vt_rows = DA_HEADS * (LANE + FLASH_ONES_ROWS)
    return pl.pallas_call(
        _rope_da_kernel,
        grid=(n_rows // ROW_TILE,),
        in_specs=[pl.BlockSpec((ROW_TILE, 3 * DA_WIDTH), lambda i: (i, 0)),
                  pl.BlockSpec((ROW_TILE, LANE), lambda i: (i, 0)),
                  pl.BlockSpec((ROW_TILE, LANE), lambda i: (i, 0))],
        out_specs=[pl.BlockSpec((DA_WIDTH, ROW_TILE), lambda i: (0, i)),
                   pl.BlockSpec((ROW_TILE, DA_WIDTH), lambda i: (i, 0)),
                   pl.BlockSpec((vt_rows, ROW_TILE), lambda i: (0, i))],
        out_shape=[jax.ShapeDtypeStruct((DA_WIDTH, n_rows), BF16),
                   jax.ShapeDtypeStruct((n_rows, DA_WIDTH), BF16),
                   jax.ShapeDtypeStruct((vt_rows, n_rows), BF16)],
        compiler_params=_params(("parallel",)),
        name="rope_da",
    )(p, cos, sin)


def diff_attention(qt, k, vt, lam_full, subln, *, seq, ctx_len, lambda_init):
    n_rows = k.shape[0]
    tq = _token_tile(n_rows)
    hw = 2 * DA_HEAD_DIM
    ones_rows = FLASH_ONES_ROWS
    return pl.pallas_call(
        functools.partial(_flash_kernel, kv=tq, seq=seq, ctx_len=ctx_len, out_scale=1.0 - lambda_init),
        grid=(DA_HEADS, n_rows // tq),
        in_specs=[pl.BlockSpec(memory_space=pltpu.SMEM),
                  pl.BlockSpec((hw, tq), lambda h, i: (h, i)),
                  pl.BlockSpec((n_rows, hw), lambda h, i: (0, h)),
                  pl.BlockSpec((hw + ones_rows, n_rows), lambda h, i: (h, 0)),
                  pl.BlockSpec((1, hw), lambda h, i: (0, 0))],
        out_specs=pl.BlockSpec((tq, hw), lambda h, i: (i, h)),
        out_shape=jax.ShapeDtypeStruct((n_rows, DA_WIDTH), BF16),
        scratch_shapes=[pltpu.VMEM((2, 1, tq), F32), pltpu.VMEM((2, hw + ones_rows, tq), F32)],
        compiler_params=pltpu.CompilerParams(dimension_semantics=("parallel", "parallel"),
                                             vmem_limit_bytes=FLASH_VMEM_LIMIT),
        name="diff_attention",
    )(lam_full.reshape(1), qt, k, vt, subln.reshape(1, hw))


def _route_kernel(lg_ref, b_ref, tri_ref, eidx_ref, w_ref, rank_ref, cnt_ref, carry_ref):
    t = lg_ref.shape[1]

    @pl.when(pl.program_id(0) == 0)
    def _():
        carry_ref[...] = jnp.zeros_like(carry_ref)

    scores = jax.nn.sigmoid(lg_ref[...])
    choice = (scores + b_ref[...]).reshape(N_GROUPS, GROUP_SIZE, t)
    s3 = scores.reshape(N_GROUPS, GROUP_SIZE, t)
    member = lax.broadcasted_iota(jnp.int32, choice.shape, 1)
    group = lax.broadcasted_iota(jnp.int32, (N_GROUPS, 1, t), 0)
    expert = lax.broadcasted_iota(jnp.int32, choice.shape, 0) * GROUP_SIZE + member
    neg_inf = -jnp.inf
    m1 = jnp.max(choice, axis=1, keepdims=True)
    first = jnp.min(jnp.where(choice == m1, member, GROUP_SIZE), axis=1, keepdims=True)
    m2 = jnp.max(jnp.where(member == first, neg_inf, choice), axis=1, keepdims=True)
    gscore = m1 + m2
    gsel = jnp.zeros(gscore.shape, F32)
    for _ in range(TOPK_GROUPS):
        m = jnp.max(gscore, axis=0, keepdims=True)
        f = jnp.min(jnp.where(gscore == m, group, N_GROUPS), axis=0, keepdims=True)
        hit = group == f
        gsel = jnp.where(hit, 1.0, gsel)
        gscore = jnp.where(hit, neg_inf, gscore)
    cand = jnp.where(gsel > 0.0, choice, neg_inf)
    esel = jnp.zeros(choice.shape, F32)
    picks = []
    for _ in range(TOP_K):
        m = jnp.max(jnp.max(cand, axis=1, keepdims=True), axis=0, keepdims=True)
        f = jnp.min(jnp.min(jnp.where(cand == m, expert, N_EXPERTS), axis=1, keepdims=True), axis=0, keepdims=True)
        hit = expert == f
        esel = jnp.where(hit, 1.0, esel)
        cand = jnp.where(hit, neg_inf, cand)
        picks.append(f)
    w = s3 * esel
    denom = jnp.sum(jnp.sum(w, axis=1, keepdims=True), axis=0, keepdims=True) + 1e-20
    w = w / denom * ROUTED_SCALE
    sel = esel.reshape(N_EXPERTS, t)
    before = jnp.dot(sel.astype(BF16), tri_ref[...], preferred_element_type=F32) + carry_ref[...]
    before = before.reshape(N_GROUPS, GROUP_SIZE, t)
    pick = lambda a, hit: jnp.sum(jnp.sum(jnp.where(hit, a, 0.0), axis=1, keepdims=True), axis=0).reshape(1, t)
    for k, f in enumerate(picks):
        hit = expert == f
        eidx_ref[k:k + 1, :] = f.reshape(1, t)
        w_ref[k:k + 1, :] = pick(w, hit)
        rank_ref[k:k + 1, :] = pick(before, hit).astype(jnp.int32)
    carry_ref[...] += jnp.sum(sel, axis=1, keepdims=True)
    cnt_ref[...] = carry_ref[...]


def route(logits_t, bias, lo, hi):
    t = _token_tile(logits_t.shape[1])
    n = hi - lo
    tile0 = lo // t
    tri = (jnp.arange(t)[:, None] < jnp.arange(t)[None, :]).astype(BF16)
    tok = lambda dt: jax.ShapeDtypeStruct((TOP_K, n), dt)
    tok_spec = pl.BlockSpec((TOP_K, t), lambda i: (0, i))
    return pl.pallas_call(
        _route_kernel,
        grid=(n // t,),
        in_specs=[pl.BlockSpec((N_EXPERTS, t), lambda i: (0, tile0 + i)),
                  pl.BlockSpec((N_EXPERTS, 1), lambda i: (0, 0)),
                  pl.BlockSpec((t, t), lambda i: (0, 0))],
        out_specs=[tok_spec, tok_spec, tok_spec, pl.BlockSpec((N_EXPERTS, 1), lambda i: (0, 0))],
        out_shape=[tok(jnp.int32), tok(F32), tok(jnp.int32), jax.ShapeDtypeStruct((N_EXPERTS, 1), F32)],
        scratch_shapes=[pltpu.VMEM((N_EXPERTS, 1), F32)],
        compiler_params=_params(("arbitrary",)),
        name="route",
    )(logits_t, bias.reshape(N_EXPERTS, 1), tri)


def _slot_kernel(start_ref, eidx_ref, rank_ref, dest_ref):
    e = eidx_ref[...]
    d = rank_ref[...]
    for x in range(N_EXPERTS):
        d = d + jnp.where(e == x, start_ref[x], 0)
    dest_ref[...] = d


def slot_index(pad_start, eidx, rank):
    n = eidx.shape[1]
    t = _token_tile(n)
    spec = pl.BlockSpec((TOP_K, t), lambda i: (0, i))
    return pl.pallas_call(
        _slot_kernel,
        grid=(n // t,),
        in_specs=[pl.BlockSpec(memory_space=pltpu.SMEM), spec, spec],
        out_specs=spec,
        out_shape=jax.ShapeDtypeStruct((TOP_K, n), jnp.int32),
        compiler_params=_params(("parallel",)),
        name="slot_index",
    )(pad_start, eidx, rank)


def _sc_worker():
    return lax.axis_index("s") * SC_CORES + lax.axis_index("c")


def sc_dispatch(h, dest3, n_slots, row0):
    d = h.shape[1]
    n_win = dest3.shape[0]
    mesh = plsc.VectorSubcoreMesh(core_axis_name="c", subcore_axis_name="s")

    @functools.partial(
        pl.kernel, mesh=mesh, out_type=jax.ShapeDtypeStruct((n_slots, d), h.dtype),
        scratch_types=[pltpu.VMEM((TOP_K, SC_WINDOW), jnp.int32), pltpu.VMEM((SC_WINDOW, d), h.dtype),
                       pltpu.SemaphoreType.DMA])
    def k(h_hbm, dest_hbm, out_hbm, idx_v, rows_v, sem):
        wid = _sc_worker()

        @pl.loop(0, -(-n_win // SC_WORKERS))
        def _(it):
            w = it * SC_WORKERS + wid

            @pl.when(w < n_win)
            def _():
                pltpu.sync_copy(dest_hbm.at[w], idx_v)
                pltpu.sync_copy(h_hbm.at[pl.ds(row0 + w * SC_WINDOW, SC_WINDOW)], rows_v)
                copies = [pltpu.async_copy(rows_v, out_hbm.at[idx_v.at[j]], sem) for j in range(TOP_K)]
                for c in copies:
                    c.wait()

    return k(h, dest3)


def sc_combine_gather(y, dest3):
    d = y.shape[1]
    n_win = dest3.shape[0]
    n = n_win * SC_WINDOW
    mesh = plsc.VectorSubcoreMesh(core_axis_name="c", subcore_axis_name="s")

    @functools.partial(
        pl.kernel, mesh=mesh, out_type=jax.ShapeDtypeStruct((TOP_K, n, d), y.dtype),
        scratch_types=[pltpu.VMEM((TOP_K, SC_WINDOW), jnp.int32), pltpu.VMEM((2, SC_WINDOW, d), y.dtype),
                       pltpu.SemaphoreType.DMA, pltpu.SemaphoreType.DMA,
                       pltpu.SemaphoreType.DMA, pltpu.SemaphoreType.DMA])
    def k(y_hbm, dest_hbm, out_hbm, idx_v, rows_v, gsem0, gsem1, osem0, osem1):
        wid = _sc_worker()
        gsem, osem = (gsem0, gsem1), (osem0, osem1)

        @pl.loop(0, -(-n_win // SC_WORKERS))
        def _(it):
            w = it * SC_WORKERS + wid

            @pl.when(w < n_win)
            def _():
                pltpu.sync_copy(dest_hbm.at[w], idx_v)
                gather = lambda j: pltpu.async_copy(y_hbm.at[idx_v.at[j]], rows_v.at[j % 2], gsem[j % 2])
                g = [None] * TOP_K
                o = [None] * TOP_K
                g[0] = gather(0)
                for j in range(TOP_K):
                    if j + 1 < TOP_K:
                        if j >= 1:
                            o[j - 1].wait()
                        g[j + 1] = gather(j + 1)
                    g[j].wait()
                    o[j] = pltpu.async_copy(rows_v.at[j % 2], out_hbm.at[j, pl.ds(w * SC_WINDOW, SC_WINDOW)],
                                            osem[j % 2])
                o[TOP_K - 2].wait()
                o[TOP_K - 1].wait()

    return k(y, dest3)


def _expert_ffn_kernel(be_ref, bv_ref, x_ref, wg_ref, wu_ref, wd_ref, o_ref, wg_s, wu_s, wd_s):
    b = pl.program_id(0)
    valid = bv_ref[b]
    new_expert = (b == 0) | (be_ref[b] != be_ref[jnp.maximum(b - 1, 0)])

    @pl.when(new_expert)
    def _():
        wg_s[...] = wg_ref[0, 0].astype(BF16)
        wu_s[...] = wu_ref[0, 0].astype(BF16)
        wd_s[...] = wd_ref[0, 0].astype(BF16)

    sub = x_ref.shape[0] // MOE_SUB_BLOCKS

    def sub_block(r):
        row = lax.broadcasted_iota(jnp.int32, (sub, 1), 0) + r * sub
        rows = pl.ds(r * sub, sub)
        lo, hi = _unpack_bf16_pair(jnp.where(row < valid, x_ref[rows, :], 0))
        x = jnp.concatenate([lo.astype(BF16), hi.astype(BF16)], axis=1)
        a = jnp.dot(x, wg_s[...], preferred_element_type=F32)
        a = a * jax.nn.sigmoid(a) * jnp.dot(x, wu_s[...], preferred_element_type=F32)
        y = jnp.dot(a.astype(BF16), wd_s[...], preferred_element_type=F32)
        half = y.shape[1] // 2
        o_ref[rows, :] = _pack_bf16_pair(y[:, :half], y[:, half:])

    for live in range(1, MOE_SUB_BLOCKS + 1):
        upper = valid <= live * sub if live < MOE_SUB_BLOCKS else True

        @pl.when((valid > (live - 1) * sub) & upper)
        def _():
            for r in range(live):
                sub_block(r)


def expert_ffn(xg, block_expert, block_valid, wg, wu, wd, layer):
    n_slots, dp = xg.shape
    d, f = wg.shape[-2:]
    grid_spec = pltpu.PrefetchScalarGridSpec(
        num_scalar_prefetch=2,
        grid=(n_slots // MOE_BLOCK,),
        in_specs=[pl.BlockSpec((MOE_BLOCK, dp), lambda b, be, bv: (b, 0)),
                  pl.BlockSpec((1, 1, d, f), lambda b, be, bv: (layer, be[b], 0, 0)),
                  pl.BlockSpec((1, 1, d, f), lambda b, be, bv: (layer, be[b], 0, 0)),
                  pl.BlockSpec((1, 1, f, d), lambda b, be, bv: (layer, be[b], 0, 0))],
        out_specs=pl.BlockSpec((MOE_BLOCK, dp), lambda b, be, bv: (b, 0)),
        scratch_shapes=[pltpu.VMEM((d, f), BF16), pltpu.VMEM((d, f), BF16), pltpu.VMEM((f, d), BF16)],
    )
    return pl.pallas_call(
        _expert_ffn_kernel,
        grid_spec=grid_spec,
        out_shape=jax.ShapeDtypeStruct((n_slots, dp), jnp.int32),
        compiler_params=_params(("arbitrary",)),
        name="expert_ffn",
    )(block_expert, block_valid, xg, wg, wu, wd)


def _combine_kernel(yg_ref, w_ref, h_ref, swg_ref, swu_ref, swd_ref, xs_ref, mods_ref, *rest, tile0, gate_idx, seq):
    o_ref = rest[-1]
    h = h_ref[...]
    a = jnp.dot(h, swg_ref[...], preferred_element_type=F32)
    a = a * jax.nn.sigmoid(a) * jnp.dot(h, swu_ref[...], preferred_element_type=F32)
    acc = jnp.dot(a.astype(BF16), swd_ref[...], preferred_element_type=F32)
    half = acc.shape[1] // 2
    acc_lo, acc_hi = acc[:, :half], acc[:, half:]
    wt = w_ref[...].T
    for k in range(TOP_K):
        lo, hi = _unpack_bf16_pair(yg_ref[k])
        acc_lo = acc_lo + wt[:, k:k + 1] * lo
        acc_hi = acc_hi + wt[:, k:k + 1] * hi
    tm = h.shape[0]
    gate = _row_mod(mods_ref, gate_idx, (tile0 + pl.program_id(0)) * tm, tm, seq)
    o_ref[:, :half] = xs_ref[:, :half] + gate[:, :half] * acc_lo
    o_ref[:, half:] = xs_ref[:, half:] + gate[:, half:] * acc_hi


def combine(yg, w, h, swg, swu, swd, residual, lo, prev):
    xs, mods, gate_idx, seq = residual
    n_all, d = h.shape
    n = w.shape[1]
    f = swg.shape[-1]
    tm = ROW_TILE
    tile0 = lo // tm
    in_specs = [pl.BlockSpec((TOP_K, tm, d // 2), lambda i: (0, i, 0)),
                pl.BlockSpec((TOP_K, tm), lambda i: (0, i)),
                pl.BlockSpec((tm, d), lambda i: (tile0 + i, 0)),
                pl.BlockSpec((d, f), lambda i: (0, 0)),
                pl.BlockSpec((d, f), lambda i: (0, 0)),
                pl.BlockSpec((f, d), lambda i: (0, 0)),
                pl.BlockSpec((tm, d), lambda i: (tile0 + i, 0)),
                pl.BlockSpec((2, 6, d), lambda i: (0, 0, 0))]
    args = [yg, w, h, swg, swu, swd, xs, mods]
    aliases = {}
    if prev is not None:
        in_specs.append(pl.BlockSpec(memory_space=pl.ANY))
        args.append(prev)
        aliases = {len(args) - 1: 0}
    return pl.pallas_call(
        functools.partial(_combine_kernel, tile0=tile0, gate_idx=gate_idx, seq=seq),
        grid=(n // tm,),
        in_specs=in_specs,
        out_specs=pl.BlockSpec((tm, d), lambda i: (tile0 + i, 0)),
        out_shape=jax.ShapeDtypeStruct((n_all, d), F32),
        input_output_aliases=aliases,
        compiler_params=_params(("parallel",)),
        name="moe_combine",
    )(*args)


def moe(h, h_packed, logits_t, bias, wg, wu, wd, layer, swg, swu, swd, residual):
    n = h.shape[0]
    t = _token_tile(n)
    cut = (n // t + 1) // 2 * t
    shared = (swg.astype(BF16), swu.astype(BF16), swd.astype(BF16))
    staged = [_moe_experts(h_packed, logits_t, bias, wg, wu, wd, layer, lo, hi) for lo, hi in ((0, cut), (cut, n))]
    out = None
    for (yg, w), lo in zip(staged, (0, cut)):
        out = combine(yg, w, h, *shared, residual, lo, out)
    return out


def _moe_experts(h_packed, logits_t, bias, wg, wu, wd, layer, lo, hi):
    n = hi - lo
    eidx, w, rank, counts = route(logits_t, bias, lo, hi)
    counts = counts.reshape(N_EXPERTS).astype(jnp.int32)
    padded = (counts + MOE_BLOCK - 1) // MOE_BLOCK * MOE_BLOCK
    pad_end = jnp.cumsum(padded)
    pad_start = pad_end - padded
    n_slots = n * TOP_K + N_EXPERTS * MOE_BLOCK
    starts = jnp.arange(n_slots // MOE_BLOCK, dtype=jnp.int32) * MOE_BLOCK
    owner = jnp.sum((pad_end[None, :] <= starts[:, None]).astype(jnp.int32), axis=1)
    block_expert = jnp.minimum(owner, N_EXPERTS - 1)
    member = (block_expert[:, None] == jnp.arange(N_EXPERTS, dtype=jnp.int32)[None, :]).astype(jnp.int32)
    left = jnp.sum(member * (counts + pad_start)[None, :], axis=1) - starts
    block_valid = jnp.clip(left, 0, MOE_BLOCK).astype(jnp.int32)
    dest = slot_index(pad_start.astype(jnp.int32), eidx, rank)
    dest3 = dest.reshape(TOP_K, n // SC_WINDOW, SC_WINDOW).transpose(1, 0, 2)
    xg = sc_dispatch(h_packed, dest3, n_slots, lo)
    y = expert_ffn(xg, block_expert, block_valid, wg, wu, wd, layer)
    return sc_combine_gather(y, dest3), w


def mixer_ab(h, w_in, w_out, decay_logit, conv_w, conv_b, w1, b1, f1, w2, b2, f2, w3, skip, rope, dft, residual, *,
             seq, ctx_len):
    n_rows = h.shape[0]
    tm = _token_tile(n_rows)
    p = mm([(h, w_in.astype(BF16))], F32, tm, 512, name="ab_in_proj")
    qkv_w = 2 * RET_QK + RET_V
    n_qk = 2 * RET_QK // LANE
    scales = (1.0,) * (RET_QK // LANE) + (RET_DK ** -0.5,) * (RET_QK // LANE) + (1.0,) * (RET_V // LANE)
    qkv = rope_cast(p, rope[0], rope[1], width=qkv_w, n_rot_blocks=n_qk, head_dim=RET_DK, scales=scales,
                    out_dtype=F32)
    log_g = jax.nn.log_sigmoid(decay_logit.astype(F32))
    ret = retention(qkv, p, log_g, jnp.exp(RET_CHUNK * log_g), seq=seq)
    v, x1, x2 = shortconv(p, conv_w, conv_b, seq=seq)
    filt = (w1, b1, f1, w2, b2, f2, w3)
    hy_x = long_conv_two_stage(dft["x"], hyena_filter_taps(seq, *filt), v, x1, x2, skip, seq)
    hy_c = long_conv_one_stage(dft["c"], hyena_filter_taps(ctx_len, *filt), v[seq:], x1[seq:], x2[seq:], skip)
    hy = jnp.concatenate([hy_x, hy_c], axis=0)
    w_out = w_out.astype(BF16)
    return mm([(ret, w_out[:RET_V]), (hy, w_out[RET_V:])], F32, tm, 512, residual=residual, name="ab_out_proj")


def mixer_da(h, w_in, w_out, lam, subln, lambda_init, rope, residual, *, seq, ctx_len):
    n_rows = h.shape[0]
    tm = _token_tile(n_rows)
    p = mm([(h, w_in.astype(BF16))], F32, tm, 512, name="da_in_proj")
    qt, k, vt = rope_da(p, rope[0], rope[1])
    lam_f = lam.astype(F32)
    lam_full = jnp.exp(jnp.sum(lam_f[0] * lam_f[1])) - jnp.exp(jnp.sum(lam_f[2] * lam_f[3])) + lambda_init
    o = diff_attention(qt, k, vt, lam_full, subln, seq=seq, ctx_len=ctx_len, lambda_init=lambda_init)
    return mm([(o, w_out.astype(BF16))], F32, tm, 512, residual=residual, name="da_out_proj")


def kernel(x, c, ctx, c_ctx, w_ada, b_ada, norm_mix, norm_ffn, ab_w_in, ab_w_out, ret_decay_logit, hy_conv_w, hy_conv_b, hy_w1, hy_b1, hy_freq1, hy_w2, hy_b2, hy_freq2, hy_w3, hy_skip, da_w_in, da_w_out, da_lambda, da_subln, router_w, router_b, exp_w_gate, exp_w_up, exp_w_down, sh_w_gate, sh_w_up, sh_w_down, norm_final):
    batch, seq, d = x.shape
    ctx_len = ctx.shape[1]
    assert batch == 1 and seq % ROW_TILE == 0 and ctx_len == ROW_TILE
    depth = w_ada.shape[0]
    n_rows = seq + ctx_len

    xs = jnp.concatenate([x[0], ctx[0]], axis=0)
    cv = jnp.zeros((SUBLANE, d), F32).at[0].set(c_ctx).at[1].set(c[0])
    mods = adaln(cv, w_ada, b_ada)[:, :2].reshape(depth, 2, 6, d)

    rope_ret = rope_tables(seq, ctx_len, RET_DK)
    rope_da = rope_tables(seq, ctx_len, DA_HEAD_DIM)
    dft = dict(x=dft_tables_two_stage(2 * seq), c=dft_tables_one_stage(2 * ctx_len))
    common = dict(n_rows=n_rows, seq=seq)

    for i in range(depth):
        j = i // 2
        (h,) = norm_mod(xs, norm_mix[i], mods=mods[i], shift_idx=0, scale_idx=1, **common)
        residual = (xs, mods[i], 2, seq)
        if i % 2 == 0:
            xs = mixer_ab(h, ab_w_in[j], ab_w_out[j], ret_decay_logit[j], hy_conv_w[j], hy_conv_b[j], hy_w1[j],
                          hy_b1[j], hy_freq1[j], hy_w2[j], hy_b2[j], hy_freq2[j], hy_w3[j], hy_skip[j], rope_ret, dft,
                          residual, seq=seq, ctx_len=ctx_len)
        else:
            lambda_init = 0.8 - 0.6 * math.exp(-0.3 * i)
            xs = mixer_da(h, da_w_in[j], da_w_out[j], da_lambda[j], da_subln[j], lambda_init, rope_da, residual,
                          seq=seq, ctx_len=ctx_len)
        h, logits_t, h_packed = norm_mod(xs, norm_ffn[i], mods=mods[i], shift_idx=3, scale_idx=4,
                                         router_wt=router_w[i].T, **common)
        xs = moe(h, h_packed, logits_t, router_b[i], exp_w_gate, exp_w_up, exp_w_down, i,
                 sh_w_gate[i], sh_w_up[i], sh_w_down[i], (xs, mods[i], 5, seq))
    (out,) = norm_mod(xs, norm_final, n_rows=seq, seq=seq, out_dtype=F32)
    return out[None]
```

```python
import functools
import math

import jax
import jax.numpy as jnp
from jax import lax
from jax.experimental import pallas as pl
from jax.experimental.pallas import tpu as pltpu
from jax.experimental.pallas import tpu_sc as plsc

F32 = jnp.float32
BF16 = jnp.bfloat16
HIGHEST = lax.Precision.HIGHEST

D_MODEL = 1024
DEPTH = 4
GRID_W = 64
EPS = 1e-6
ROPE_BASE = 10000.0

RET_HEADS = 4
RET_DK = 128
RET_DV = 256
RET_CHUNK = 128
RET_QK = RET_HEADS * RET_DK
RET_V = RET_HEADS * RET_DV

HY_WIDTH = 512
HY_ORDER = 2
HY_BANDS = 16
HY_EMB = 2 * HY_BANDS + 1
HY_FFN = 64
HY_DECAY_TARGET = 1e-2
HY_FAST_DECAY = 0.3
HY_SLOW_DECAY = 1.5
HY_ZCOLS = 64
HY_VALID_COL = HY_EMB
FFT_N2 = 128

AB_IN = 2 * RET_QK + 2 * RET_V + (HY_ORDER + 1) * HY_WIDTH
AB_CAT = RET_V + HY_WIDTH

DA_HEADS = 8
DA_HEAD_DIM = 64
DA_WIDTH = DA_HEADS * 2 * DA_HEAD_DIM

N_EXPERTS = 64
TOP_K = 8
N_GROUPS = 8
TOPK_GROUPS = 4
GROUP_SIZE = N_EXPERTS // N_GROUPS
EXPERT_DIM = 256
ROUTED_SCALE = 2.5
MOE_BLOCK = 512
MOE_SUB_BLOCKS = 2
SC_CORES = 2
SC_SUBCORES = 16
SC_WORKERS = SC_CORES * SC_SUBCORES
SC_WINDOW = 64

LANE = 128
SUBLANE = 8
ROW_TILE = 256
MAX_TOKEN_TILE = 1280
VMEM_LIMIT = 48 * 1024 * 1024
FLASH_VMEM_LIMIT = 56 * 1024 * 1024
NEG_BIG = -1e30
LOG2_E = 1.4426950408889634
FLASH_ONES_ROWS = 16
FLASH_INIT_KEYS = 16
FLASH_LAZY_HEADROOM = 60.0


def _params(sem):
    return pltpu.CompilerParams(dimension_semantics=sem, vmem_limit_bytes=VMEM_LIMIT)


def _token_tile(n):
    best = ROW_TILE
    t = ROW_TILE
    while t <= min(n, MAX_TOKEN_TILE):
        if n % t == 0:
            best = t
        t += ROW_TILE
    return best


def _row_mod(mods_ref, idx, row0, n, seq):
    row = row0 + lax.broadcasted_iota(jnp.int32, (n, 1), 0)
    return jnp.where(row >= seq, mods_ref[0, idx:idx + 1, :], mods_ref[1, idx:idx + 1, :])


def _mm_kernel(*refs, n_pairs, has_epi, gate_idx, seq):
    acc = None
    for p in range(n_pairs):
        a = refs[2 * p][...].astype(BF16)
        b = refs[2 * p + 1][...].astype(BF16)
        d = jnp.dot(a, b, preferred_element_type=F32)
        acc = d if acc is None else acc + d
    idx = 2 * n_pairs
    if has_epi:
        acc = refs[idx][...] * (acc + refs[idx + 1][...] * refs[idx + 2][...])
        idx += 3
    if gate_idx is not None:
        tm = acc.shape[0]
        acc = refs[idx][...] + _row_mod(refs[idx + 1], gate_idx, pl.program_id(0) * tm, tm, seq) * acc
        idx += 2
    o_ref = refs[idx]
    o_ref[...] = acc.astype(o_ref.dtype)


def mm(pairs, out_dtype, tm, tn, epi=None, residual=None, name="mm"):
    m = pairs[0][0].shape[0]
    n = pairs[0][1].shape[1]
    assert m % tm == 0 and n % tn == 0
    in_specs, args = [], []
    for a, b in pairs:
        k = a.shape[1]
        in_specs += [pl.BlockSpec((tm, k), lambda i, j: (i, 0)), pl.BlockSpec((k, tn), lambda i, j: (0, j))]
        args += [a, b]
    if epi is not None:
        in_specs += [pl.BlockSpec((tm, tn), lambda i, j: (i, j)), pl.BlockSpec((1, tn), lambda i, j: (0, j)),
                     pl.BlockSpec((tm, tn), lambda i, j: (i, j))]
        args += list(epi)
    gate_idx = seq = None
    if residual is not None:
        res, mods, gate_idx, seq = residual
        in_specs += [pl.BlockSpec((tm, tn), lambda i, j: (i, j)), pl.BlockSpec((2, 6, tn), lambda i, j: (0, 0, j))]
        args += [res, mods]
    return pl.pallas_call(
        functools.partial(_mm_kernel, n_pairs=len(pairs), has_epi=epi is not None, gate_idx=gate_idx, seq=seq),
        grid=(m // tm, n // tn),
        in_specs=in_specs,
        out_specs=pl.BlockSpec((tm, tn), lambda i, j: (i, j)),
        out_shape=jax.ShapeDtypeStruct((m, n), out_dtype),
        compiler_params=_params(("parallel", "parallel")),
        name=name,
    )(*args)


def _adaln_kernel(cv_ref, w_ref, b_ref, o_ref):
    cv = cv_ref[...]
    s = cv * jax.nn.sigmoid(cv)
    o_ref[0] = jnp.dot(s, w_ref[0], precision=HIGHEST, preferred_element_type=F32) + b_ref[0]


def adaln(cv, w_ada, b_ada):
    depth, d, n = w_ada.shape
    tn = 1536
    return pl.pallas_call(
        _adaln_kernel,
        grid=(depth, n // tn),
        in_specs=[pl.BlockSpec((SUBLANE, d), lambda l, j: (0, 0)),
                  pl.BlockSpec((1, d, tn), lambda l, j: (l, 0, j)),
                  pl.BlockSpec((1, 1, tn), lambda l, j: (l, 0, j))],
        out_specs=pl.BlockSpec((1, SUBLANE, tn), lambda l, j: (l, 0, j)),
        out_shape=jax.ShapeDtypeStruct((depth, SUBLANE, n), F32),
        compiler_params=_params(("parallel", "parallel")),
        name="adaln",
    )(cv, w_ada, b_ada.reshape(depth, 1, n))


def _norm_mod_kernel(*refs, shift_idx, scale_idx, has_router, seq):
    it = iter(refs)
    x = next(it)[...]
    mods_ref = next(it) if shift_idx is not None else None
    g_ref = next(it)
    wr_ref = next(it) if has_router else None
    h_ref = next(it)
    y = x * lax.rsqrt(jnp.mean(x * x, axis=-1, keepdims=True) + EPS) * g_ref[...]
    if shift_idx is not None:
        tm = x.shape[0]
        row0 = pl.program_id(0) * tm
        y = y * (1.0 + _row_mod(mods_ref, scale_idx, row0, tm, seq)) + _row_mod(mods_ref, shift_idx, row0, tm, seq)
    h_ref[...] = y.astype(h_ref.dtype)
    if has_router:
        lg_ref = next(it)
        lg_ref[...] = lax.dot_general(wr_ref[...], y, (((1,), (1,)), ((), ())),
                                      precision=HIGHEST, preferred_element_type=F32)
        half = y.shape[1] // 2
        next(it)[...] = _pack_bf16_pair(y[:, :half], y[:, half:])


def _pack_bf16_pair(a, b):
    def rounded(x):
        u = lax.bitcast_convert_type(x, jnp.int32)
        return u + 0x7FFF + (lax.shift_right_logical(u, 16) & 1)
    return lax.shift_right_logical(rounded(a), 16) | (rounded(b) & -65536)


def _unpack_bf16_pair(w):
    return (lax.bitcast_convert_type(lax.shift_left(w, 16), F32),
            lax.bitcast_convert_type(w & -65536, F32))


def norm_mod(xs, g, *, n_rows, seq, mods=None, shift_idx=None, scale_idx=None, router_wt=None, out_dtype=BF16):
    d = xs.shape[1]
    tm = _token_tile(n_rows)
    row = pl.BlockSpec((tm, d), lambda i: (i, 0))
    in_specs, args = [row], [xs]
    if shift_idx is not None:
        in_specs.append(pl.BlockSpec((2, 6, d), lambda i: (0, 0, 0)))
        args.append(mods)
    in_specs.append(pl.BlockSpec((1, d), lambda i: (0, 0)))
    args.append(g.reshape(1, d))
    has_router = router_wt is not None
    if has_router:
        in_specs.append(pl.BlockSpec(router_wt.shape, lambda i: (0, 0)))
        args.append(router_wt)
    out_specs = [row]
    out_shape = [jax.ShapeDtypeStruct((n_rows, d), out_dtype)]
    if has_router:
        out_specs.append(pl.BlockSpec((N_EXPERTS, tm), lambda i: (0, i)))
        out_shape.append(jax.ShapeDtypeStruct((N_EXPERTS, n_rows), F32))
        out_specs.append(pl.BlockSpec((tm, d // 2), lambda i: (i, 0)))
        out_shape.append(jax.ShapeDtypeStruct((n_rows, d // 2), jnp.int32))
    return pl.pallas_call(
        functools.partial(_norm_mod_kernel, shift_idx=shift_idx, scale_idx=scale_idx, has_router=has_router, seq=seq),
        grid=(n_rows // tm,),
        in_specs=in_specs,
        out_specs=out_specs,
        out_shape=out_shape,
        compiler_params=_params(("parallel",)),
        name="norm_mod",
    )(*args)


def _rope_kernel(p_ref, cos_ref, sin_ref, o_ref, *, n_rot_blocks, head_dim, scales):
    cos = cos_ref[...]
    sin = sin_ref[...]
    for b in range(len(scales)):
        x = p_ref[:, b * LANE:(b + 1) * LANE]
        if b < n_rot_blocks:
            if head_dim == LANE:
                rot = pltpu.roll(x, LANE // 2, 1)
            else:
                lane = lax.broadcasted_iota(jnp.int32, x.shape, 1)
                first_half = (lane % head_dim) < head_dim // 2
                rot = jnp.where(first_half, pltpu.roll(x, LANE - head_dim // 2, 1), pltpu.roll(x, head_dim // 2, 1))
            x = x * cos + rot * sin
        if scales[b] != 1.0:
            x = x * scales[b]
        o_ref[:, b * LANE:(b + 1) * LANE] = x.astype(o_ref.dtype)


def rope_cast(p, cos, sin, *, width, n_rot_blocks, head_dim, scales, out_dtype):
    n_rows = p.shape[0]
    return pl.pallas_call(
        functools.partial(_rope_kernel, n_rot_blocks=n_rot_blocks, head_dim=head_dim, scales=scales),
        grid=(n_rows // ROW_TILE,),
        in_specs=[pl.BlockSpec((ROW_TILE, width), lambda i: (i, 0)),
                  pl.BlockSpec((ROW_TILE, LANE), lambda i: (i, 0)),
                  pl.BlockSpec((ROW_TILE, LANE), lambda i: (i, 0))],
        out_specs=pl.BlockSpec((ROW_TILE, width), lambda i: (i, 0)),
        out_shape=jax.ShapeDtypeStruct((n_rows, width), out_dtype),
        compiler_params=_params(("parallel",)),
        name="rope_cast",
    )(p, cos, sin)


def rope_tables(seq, ctx_len, head_dim):
    n_freq = head_dim // 4
    inv = ROPE_BASE ** (-jnp.arange(n_freq, dtype=F32) / n_freq)
    rows = seq // GRID_W
    row = jnp.repeat(jnp.arange(rows, dtype=F32), GRID_W)
    col = jnp.tile(jnp.arange(GRID_W, dtype=F32), rows)
    ang = jnp.concatenate([row[:, None] * inv, col[:, None] * inv], axis=-1)
    cos, sin = jnp.cos(ang), jnp.sin(ang)
    cos = jnp.concatenate([cos, cos], axis=-1)
    sin = jnp.concatenate([-sin, sin], axis=-1)
    reps = LANE // head_dim
    cos, sin = jnp.tile(cos, (1, reps)), jnp.tile(sin, (1, reps))
    cos = jnp.concatenate([cos, jnp.ones((ctx_len, LANE), F32)], axis=0)
    sin = jnp.concatenate([sin, jnp.zeros((ctx_len, LANE), F32)], axis=0)
    return cos, sin


def _ret_kernel(lg_ref, gc_ref, q_ref, k_ref, v_ref, *rest, reverse):
    if reverse:
        yf_ref, gate_ref, o_ref, s_ref = rest
    else:
        o_ref, s_ref = rest
    c = RET_CHUNK

    @pl.when(pl.program_id(0) == 0)
    def _():
        s_ref[...] = jnp.zeros_like(s_ref)

    ii = lax.broadcasted_iota(jnp.int32, (c, c), 0)
    jj = lax.broadcasted_iota(jnp.int32, (c, c), 1)
    rel = ((jj - ii) if reverse else (ii - jj)).astype(F32)
    pos = lax.broadcasted_iota(jnp.int32, (c, 1), 0).astype(F32)
    for h in range(RET_HEADS):
        lg = lg_ref[h]
        dec = jnp.where(rel >= 0, jnp.exp(jnp.maximum(rel, 0.0) * lg), 0.0)
        if reverse:
            q_dec = jnp.exp((c - pos) * lg)
            k_dec = jnp.exp(pos * lg)
        else:
            q_dec = jnp.exp((pos + 1.0) * lg)
            k_dec = jnp.exp((c - 1.0 - pos) * lg)
        q = q_ref[:, h * RET_DK:(h + 1) * RET_DK]
        k = k_ref[:, h * RET_DK:(h + 1) * RET_DK]
        v = v_ref[:, h * RET_DV:(h + 1) * RET_DV].astype(BF16)
        s = lax.dot_general(q.astype(BF16), k.astype(BF16), (((1,), (1,)), ((), ())),
                            preferred_element_type=F32) * dec
        state = s_ref[h]
        y = jnp.dot(s.astype(BF16), v, preferred_element_type=F32)
        y = y + jnp.dot((q * q_dec).astype(BF16), state.astype(BF16), preferred_element_type=F32)
        upd = lax.dot_general((k * k_dec).astype(BF16), v, (((0,), (0,)), ((), ())), preferred_element_type=F32)
        s_ref[h] = gc_ref[h] * state + upd
        if reverse:
            r = y + yf_ref[:, h * RET_DV:(h + 1) * RET_DV]
            mu = jnp.mean(r, axis=-1, keepdims=True)
            rc = r - mu
            var = jnp.mean(rc * rc, axis=-1, keepdims=True)
            g = gate_ref[:, h * RET_DV:(h + 1) * RET_DV]
            o_ref[:, h * RET_DV:(h + 1) * RET_DV] = (rc * lax.rsqrt(var + EPS) * (g * jax.nn.sigmoid(g))).astype(
                o_ref.dtype)
        else:
            o_ref[:, h * RET_DV:(h + 1) * RET_DV] = y


def retention(qkv, p, log_g, g_chunk, *, seq):
    n_rows = qkv.shape[0]
    n_chunks = n_rows // RET_CHUNK
    n_x = seq // RET_CHUNK
    smem = pl.BlockSpec(memory_space=pltpu.SMEM)

    def run(reverse, extra):
        if reverse:
            idx = lambda t: n_chunks - 1 - t
        else:
            idx = lambda t: (t + n_x) % n_chunks
        in_specs = [smem, smem,
                    pl.BlockSpec((RET_CHUNK, RET_QK), lambda t: (idx(t), 0)),
                    pl.BlockSpec((RET_CHUNK, RET_QK), lambda t: (idx(t), 1)),
                    pl.BlockSpec((RET_CHUNK, RET_V), lambda t: (idx(t), 1))]
        args = [log_g[1 if reverse else 0], g_chunk[1 if reverse else 0], qkv, qkv, qkv]
        if reverse:
            in_specs += [pl.BlockSpec((RET_CHUNK, RET_V), lambda t: (idx(t), 0)),
                         pl.BlockSpec((RET_CHUNK, RET_V), lambda t: (idx(t), 2))]
            args += list(extra)
        return pl.pallas_call(
            functools.partial(_ret_kernel, reverse=reverse),
            grid=(n_chunks,),
            in_specs=in_specs,
            out_specs=pl.BlockSpec((RET_CHUNK, RET_V), lambda t: (idx(t), 0)),
            out_shape=jax.ShapeDtypeStruct((n_rows, RET_V), BF16 if reverse else F32),
            scratch_shapes=[pltpu.VMEM((RET_HEADS, RET_DK, RET_DV), F32)],
            compiler_params=_params(("arbitrary",)),
            name="retention_bwd" if reverse else "retention_fwd",
        )(*args)

    y_fwd = run(False, None)
    return run(True, (y_fwd, p))


def _shortconv_kernel(cur_ref, prev_ref, next_ref, w_ref, b_ref, v_ref, x1_ref, x2_ref, *, x_tiles):
    i = pl.program_id(0)
    cur = cur_ref[...]
    rows = cur.shape[0]
    row = lax.broadcasted_iota(jnp.int32, (rows, 1), 0)
    has_prev = jnp.where((i == 0) | (i == x_tiles), 0.0, 1.0)
    has_next = jnp.where((i == x_tiles - 1) | (i == x_tiles), 0.0, 1.0)
    up = jnp.where(row == 0, prev_ref[SUBLANE - 1:SUBLANE, :] * has_prev, pltpu.roll(cur, 1, 0))
    dn = jnp.where(row == rows - 1, next_ref[0:1, :] * has_next, pltpu.roll(cur, rows - 1, 0))
    y = up * w_ref[0:1, :] + cur * w_ref[1:2, :] + dn * w_ref[2:3, :] + b_ref[...]
    v_ref[...] = y[:, :HY_WIDTH]
    x1_ref[...] = y[:, HY_WIDTH:2 * HY_WIDTH]
    x2_ref[...] = y[:, 2 * HY_WIDTH:]


def shortconv(p, w, b, *, seq):
    n_rows = p.shape[0]
    width = 3 * HY_WIDTH
    col = p.shape[1] // width - 1
    per = ROW_TILE // SUBLANE
    last = n_rows // SUBLANE - 1
    out = jax.ShapeDtypeStruct((n_rows, HY_WIDTH), F32)
    ospec = pl.BlockSpec((ROW_TILE, HY_WIDTH), lambda i: (i, 0))
    return pl.pallas_call(
        functools.partial(_shortconv_kernel, x_tiles=seq // ROW_TILE),
        grid=(n_rows // ROW_TILE,),
        in_specs=[pl.BlockSpec((ROW_TILE, width), lambda i: (i, col)),
                  pl.BlockSpec((SUBLANE, width), lambda i: (jnp.maximum(i * per - 1, 0), col)),
                  pl.BlockSpec((SUBLANE, width), lambda i: (jnp.minimum((i + 1) * per, last), col)),
                  pl.BlockSpec((3, width), lambda i: (0, 0)),
                  pl.BlockSpec((1, width), lambda i: (0, 0))],
        out_specs=[ospec, ospec, ospec],
        out_shape=[out, out, out],
        compiler_params=_params(("parallel",)),
        name="shortconv",
    )(p, p, p, w, b.reshape(1, width))


def _filt_kernel(z_ref, w1_ref, b1_ref, f1_ref, w2_ref, b2_ref, f2_ref, w3a_ref, w3b_ref, dl_ref, *o_ref):
    z = z_ref[...]
    h = jnp.sin(f1_ref[...] * (jnp.dot(z, w1_ref[...], precision=HIGHEST, preferred_element_type=F32) + b1_ref[...]))
    h = jnp.sin(f2_ref[...] * (jnp.dot(h, w2_ref[...], precision=HIGHEST, preferred_element_type=F32) + b2_ref[...]))
    window = jnp.exp(-z[:, 0:1] * dl_ref[...]) * z[:, HY_VALID_COL:HY_VALID_COL + 1]
    for o, w3_ref in enumerate((w3a_ref, w3b_ref)):
        o_ref[o][...] = jnp.dot(h, w3_ref[...], precision=HIGHEST, preferred_element_type=F32) * window


def hyena_filter_taps(length, w1, b1, f1, w2, b2, f2, w3):
    z = _filter_positions(length)
    w1p = jnp.zeros((HY_ZCOLS, HY_FFN), F32).at[:HY_EMB].set(w1)
    deltas = jnp.abs(jnp.linspace(math.log(HY_DECAY_TARGET) / HY_SLOW_DECAY,
                                  math.log(HY_DECAY_TARGET) / HY_FAST_DECAY, HY_WIDTH, dtype=F32)).reshape(1, HY_WIDTH)
    tm = min(length, 512)
    half_tiles = length // tm
    vec = lambda a: a.reshape(1, HY_FFN)
    small = lambda shape: pl.BlockSpec(shape, lambda i: (0, 0))
    w3_spec = lambda o: pl.BlockSpec((HY_FFN, HY_WIDTH), lambda i: (0, 2 * o + jnp.where(i >= half_tiles, 1, 0)))
    assert HY_ORDER == 2
    return pl.pallas_call(
        _filt_kernel,
        grid=(2 * half_tiles,),
        in_specs=[pl.BlockSpec((tm, HY_ZCOLS), lambda i: (i, 0)),
                  small((HY_ZCOLS, HY_FFN)), small((1, HY_FFN)), small((1, HY_FFN)),
                  small((HY_FFN, HY_FFN)), small((1, HY_FFN)), small((1, HY_FFN)),
                  w3_spec(0), w3_spec(1), small((1, HY_WIDTH))],
        out_specs=[pl.BlockSpec((tm, HY_WIDTH), lambda i: (i, 0))] * HY_ORDER,
        out_shape=[jax.ShapeDtypeStruct((2 * length, HY_WIDTH), F32)] * HY_ORDER,
        compiler_params=_params(("parallel",)),
        name="hyena_filter",
    )(z, w1p, vec(b1), vec(f1), w2, vec(b2), vec(f2), w3, w3, deltas)


def _filter_positions(length):
    t = jnp.concatenate([jnp.arange(length, dtype=F32), float(length) - jnp.arange(length, dtype=F32)])
    valid = jnp.ones((2 * length,), F32).at[length].set(0.0)
    t_norm = t / max(length - 1, 1)
    bands = jnp.linspace(1e-4, HY_BANDS - 1, HY_BANDS, dtype=F32)
    ang = (2.0 * math.pi / length) * t[:, None] * bands[None, :]
    z = jnp.concatenate([t_norm[:, None], jnp.cos(ang), -jnp.sin(ang), valid[:, None]], axis=-1)
    return jnp.pad(z, ((0, 0), (0, HY_ZCOLS - z.shape[1])))


def _angles(num, den):
    return (2.0 * math.pi / den) * (num % den).astype(F32)


def dft_tables_two_stage(m):
    n2 = FFT_N2
    n1 = m // n2
    half = n1 // 2
    kp = -(-(half + 1) // SUBLANE) * SUBLANE
    k1 = jnp.arange(kp, dtype=jnp.int32)
    live = (k1 <= half)
    a1 = _angles(k1[:, None] * jnp.arange(n1, dtype=jnp.int32)[None, :], n1)
    f1 = jnp.concatenate([jnp.where(live[:, None], jnp.cos(a1), 0.0), jnp.where(live[:, None], -jnp.sin(a1), 0.0)], 0)
    wgt = jnp.where((k1 == 0) | (k1 == half), 1.0, 2.0) * live / m
    a1h = a1[:, :half].T
    cinv = jnp.concatenate([jnp.cos(a1h) * wgt[None, :], -jnp.sin(a1h) * wgt[None, :]], axis=1)
    k = k1[:, None, None] + n1 * jnp.arange(n2, dtype=jnp.int32)[None, :, None]
    th = _angles(k * jnp.arange(n2, dtype=jnp.int32)[None, None, :], m)
    c = jnp.where(live[:, None, None], jnp.cos(th), 0.0)
    s = jnp.where(live[:, None, None], jnp.sin(th), 0.0)
    g_fwd = jnp.concatenate([jnp.concatenate([c, s], 2), jnp.concatenate([-s, c], 2)], 1)
    ct, st = jnp.swapaxes(c, 1, 2), jnp.swapaxes(s, 1, 2)
    g_inv = jnp.concatenate([jnp.concatenate([ct, -st], 2), jnp.concatenate([st, ct], 2)], 1)
    return dict(n1=n1, kp=kp, f1=f1.astype(BF16), f1_half=f1[:, :half].astype(BF16), cinv=cinv.astype(BF16),
                g_fwd=g_fwd.astype(BF16), g_inv=g_inv.astype(BF16))


def dft_tables_one_stage(m):
    half = m // 2
    kp = -(-(half + 1) // SUBLANE) * SUBLANE
    k = jnp.arange(kp, dtype=jnp.int32)
    live = (k <= half)
    a = _angles(k[:, None] * jnp.arange(m, dtype=jnp.int32)[None, :], m)
    f = jnp.concatenate([jnp.where(live[:, None], jnp.cos(a), 0.0), jnp.where(live[:, None], -jnp.sin(a), 0.0)], 0)
    wgt = jnp.where((k == 0) | (k == half), 1.0, 2.0) * live / m
    ah = a[:, :half].T
    cinv = jnp.concatenate([jnp.cos(ah) * wgt[None, :], -jnp.sin(ah) * wgt[None, :]], axis=1)
    return dict(kp=kp, f=f.astype(BF16), f_half=f[:, :half].astype(BF16), cinv=cinv.astype(BF16))


def _bmm_kernel(*refs, kb, in_part_major, out_part_major, has_h):
    if has_h:
        g_ref, a_ref, h_ref, o_ref = refs
    else:
        g_ref, a_ref, o_ref = refs
    n2 = FFT_N2
    for b in range(kb):
        if in_part_major:
            ar, ai = a_ref[0, b], a_ref[1, b]
        else:
            ar, ai = a_ref[b, 0], a_ref[b, 1]
        if has_h:
            hr, hi = h_ref[b, 0], h_ref[b, 1]
            ar, ai = ar * hr - ai * hi, ar * hi + ai * hr
        xin = jnp.concatenate([ar, ai], axis=0).astype(BF16)
        y = jnp.dot(g_ref[b], xin, preferred_element_type=F32)
        if out_part_major:
            o_ref[0, b] = y[:n2]
            o_ref[1, b] = y[n2:]
        else:
            o_ref[b, 0] = y[:n2]
            o_ref[b, 1] = y[n2:]


def bmm_k1(g, a, h=None, *, in_part_major, out_part_major):
    kp = g.shape[0]
    n2 = FFT_N2
    c = a.shape[-1]
    kb, tc = SUBLANE, min(c, 512)
    pm = lambda: pl.BlockSpec((2, kb, n2, tc), lambda i, j: (0, i, 0, j))
    km = lambda: pl.BlockSpec((kb, 2, n2, tc), lambda i, j: (i, 0, 0, j))
    in_specs = [pl.BlockSpec((kb, 2 * n2, 2 * n2), lambda i, j: (i, 0, 0)), pm() if in_part_major else km()]
    args = [g, a]
    if h is not None:
        in_specs.append(km())
        args.append(h)
    return pl.pallas_call(
        functools.partial(_bmm_kernel, kb=kb, in_part_major=in_part_major, out_part_major=out_part_major,
                          has_h=h is not None),
        grid=(kp // kb, c // tc),
        in_specs=in_specs,
        out_specs=pm() if out_part_major else km(),
        out_shape=jax.ShapeDtypeStruct((2, kp, n2, c) if out_part_major else (kp, 2, n2, c), F32),
        compiler_params=_params(("parallel", "parallel")),
        name="dft_inner",
    )(*args)


def _cmul_kernel(x_ref, h_ref, o_ref):
    xr, xi, hr, hi = x_ref[0], x_ref[1], h_ref[0], h_ref[1]
    o_ref[0] = xr * hr - xi * hi
    o_ref[1] = xr * hi + xi * hr


def cmul(x, h):
    spec = pl.BlockSpec(x.shape, lambda i: (0, 0, 0))
    return pl.pallas_call(_cmul_kernel, grid=(1,), in_specs=[spec, spec], out_specs=spec,
                          out_shape=jax.ShapeDtypeStruct(x.shape, F32), compiler_params=_params(("arbitrary",)),
                          name="spectrum_product")(x, h)


def _dft_outer3_kernel(f_ref, x_ref, o_ref):
    c = x_ref.shape[2]
    f = f_ref[...]
    for j in range(SUBLANE):
        o_ref[:, j * c:(j + 1) * c] = jnp.dot(f, x_ref[:, j, :].astype(BF16), preferred_element_type=F32)


def dft_outer3(f, x3, n_outer):
    rows = f.shape[0]
    c = x3.shape[2]
    return pl.pallas_call(
        _dft_outer3_kernel,
        grid=(FFT_N2 // SUBLANE,),
        in_specs=[pl.BlockSpec((rows, n_outer), lambda j: (0, 0)),
                  pl.BlockSpec((n_outer, SUBLANE, c), lambda j: (0, j, 0))],
        out_specs=pl.BlockSpec((rows, SUBLANE * c), lambda j: (0, j)),
        out_shape=jax.ShapeDtypeStruct((rows, FFT_N2 * c), F32),
        compiler_params=_params(("parallel",)),
        name="dft_outer",
    )(f, x3)


def _idft_gate3_kernel(cinv_ref, b_ref, gate_ref, skip_ref, u_ref, o_ref, *, u_is_3d):
    c = gate_ref.shape[2]
    cinv = cinv_ref[...]
    for j in range(SUBLANE):
        cols = slice(j * c, (j + 1) * c)
        acc = jnp.dot(cinv, b_ref[:, cols].astype(BF16), preferred_element_type=F32)
        u = u_ref[:, j, :] if u_is_3d else u_ref[:, cols]
        o_ref[:, cols] = gate_ref[:, j, :] * (acc + skip_ref[...] * u)


def idft_gate3(cinv, b2d, gate3, skip_row, u):
    n_outer = cinv.shape[0]
    c = gate3.shape[2]
    u_is_3d = u.ndim == 3
    wide = pl.BlockSpec((n_outer, SUBLANE * c), lambda j: (0, j))
    slab = pl.BlockSpec((n_outer, SUBLANE, c), lambda j: (0, j, 0))
    return pl.pallas_call(
        functools.partial(_idft_gate3_kernel, u_is_3d=u_is_3d),
        grid=(FFT_N2 // SUBLANE,),
        in_specs=[pl.BlockSpec(cinv.shape, lambda j: (0, 0)),
                  pl.BlockSpec((b2d.shape[0], SUBLANE * c), lambda j: (0, j)),
                  slab, pl.BlockSpec((1, c), lambda j: (0, 0)), slab if u_is_3d else wide],
        out_specs=wide,
        out_shape=jax.ShapeDtypeStruct((n_outer, FFT_N2 * c), F32),
        compiler_params=_params(("parallel",)),
        name="idft_outer_gate",
    )(cinv, b2d, gate3, skip_row, u)


def long_conv_two_stage(tabs, taps, v, x1, x2, skip, length):
    c = v.shape[1]
    n2, n1, kp = FFT_N2, tabs["n1"], tabs["kp"]
    as3 = lambda a: a.reshape(a.shape[0] // n2, n2, c)
    spectrum = lambda a2d: bmm_k1(tabs["g_fwd"], a2d.reshape(2, kp, n2, c), in_part_major=True, out_part_major=False)

    spectra = [spectrum(dft_outer3(tabs["f1"], as3(taps[o]), n1)) for o in range(HY_ORDER)]
    v3 = as3(v)
    u = v3
    for o, gate in enumerate((x1, x2)):
        if u.ndim == 3:
            a = dft_outer3(tabs["f1_half"], u, n1 // 2)
        else:
            a = mm([(tabs["f1_half"], u)], F32, 2 * kp, 2048, name="dft_outer")
        bt = bmm_k1(tabs["g_inv"], spectrum(a), spectra[o], in_part_major=False, out_part_major=True)
        u = idft_gate3(tabs["cinv"], bt.reshape(2 * kp, n2 * c), as3(gate), skip[o].reshape(1, c), u)
    return u.reshape(length, c)


def long_conv_one_stage(tabs, taps, v, x1, x2, skip):
    length, c = v.shape
    kp = tabs["kp"]
    u = v
    for o, gate in enumerate((x1, x2)):
        hs = mm([(tabs["f"], taps[o])], F32, 2 * kp, c, name="ctx_dft").reshape(2, kp, c)
        xs = mm([(tabs["f_half"], u)], F32, 2 * kp, c, name="ctx_dft").reshape(2, kp, c)
        ys = cmul(xs, hs).reshape(2 * kp, c)
        u = mm([(tabs["cinv"], ys)], F32, length, c, epi=(gate, skip[o].reshape(1, c), u), name="ctx_idft_gate")
    return u


def _flash_kernel(lam_ref, qt_ref, k_ref, vt_ref, sub_ref, o_ref, m_ref, excess_ref, acc_ref, *, kv, seq, ctx_len,
                  out_scale):
    i = pl.program_id(1)
    last_q = pl.num_programs(1) - 1
    tq = qt_ref.shape[1]
    d = DA_HEAD_DIM
    dv = 2 * DA_HEAD_DIM
    n_chunks = k_ref.shape[0] // kv
    acc_ref[...] = jnp.zeros_like(acc_ref)

    def scores(off, rows, c, masked):
        s = jnp.dot(k_ref[pl.ds(off, rows), c * d:(c + 1) * d], qt_ref[c * d:(c + 1) * d, :],
                    preferred_element_type=F32)
        if masked:
            key = off + lax.broadcasted_iota(jnp.int32, (rows, 1), 0)
            lane = lax.broadcasted_iota(jnp.int32, (1, tq), 1)
            s = s + jnp.where(key < seq, NEG_BIG, 0.0) * jnp.where(lane >= tq - ctx_len, 1.0, 0.0)
        return s

    def exact_step(off, c, masked):
        s = scores(off, kv, c, masked)
        m_old = m_ref[c]
        m_new = jnp.maximum(m_old, jnp.max(s, axis=0, keepdims=True))
        pr = jnp.exp2(s - m_new).astype(BF16)
        acc_ref[c] = jnp.exp2(m_old - m_new) * acc_ref[c] + jnp.dot(vt_ref[:, pl.ds(off, kv)], pr,
                                                                   preferred_element_type=F32)
        m_ref[c] = m_new

    def lazy_step(off, c, masked):
        s = scores(off, kv, c, masked)
        m_old = m_ref[c]
        m_chunk = jnp.max(s, axis=0, keepdims=True)
        pv = jnp.dot(vt_ref[:, pl.ds(off, kv)], jnp.exp2(s - m_old).astype(BF16), preferred_element_type=F32)
        m_new = jnp.maximum(m_old, m_chunk)
        acc_ref[c] = jnp.exp2(m_old - m_new) * (acc_ref[c] + pv)
        m_ref[c] = m_new
        excess_ref[c] = jnp.maximum(excess_ref[c], m_chunk - m_old)

    def all_chunks(step, masked):
        def body(kc, carry):
            off = pl.multiple_of(kc * kv, kv)
            for c in range(2):
                step(off, c, masked)
            return carry

        lax.fori_loop(0, n_chunks, body, 0)

    def run(masked):
        for c in range(2):
            m0 = jnp.max(scores(0, FLASH_INIT_KEYS, c, False), axis=0, keepdims=True)
            if masked:
                lane = lax.broadcasted_iota(jnp.int32, (1, tq), 1)
                m_ctx = jnp.max(scores(seq, FLASH_INIT_KEYS, c, False), axis=0, keepdims=True)
                m0 = jnp.where(lane >= tq - ctx_len, m_ctx, m0)
            m_ref[c] = m0
        excess_ref[...] = jnp.full_like(excess_ref, NEG_BIG)
        all_chunks(lazy_step, masked)

        @pl.when(jnp.max(excess_ref[...]) > FLASH_LAZY_HEADROOM)
        def _():
            m_ref[...] = jnp.full_like(m_ref, NEG_BIG)
            acc_ref[...] = jnp.zeros_like(acc_ref)
            all_chunks(exact_step, masked)

    @pl.when(i != last_q)
    def _():
        run(False)

    @pl.when(i == last_q)
    def _():
        run(True)

    a0 = acc_ref[0, :dv, :] / acc_ref[0, dv:dv + 1, :]
    a1 = acc_ref[1, :dv, :] / acc_ref[1, dv:dv + 1, :]
    o = (a0 - lam_ref[0] * a1).T
    o = o * lax.rsqrt(jnp.mean(o * o, axis=-1, keepdims=True) + 1e-5) * sub_ref[...]
    o_ref[...] = (o * out_scale).astype(o_ref.dtype)


def _rope_da_kernel(p_ref, cos_ref, sin_ref, qt_ref, k_ref, vt_ref):
    cos = cos_ref[...]
    sin = sin_ref[...]
    hw = 2 * DA_HEAD_DIM
    lane = lax.broadcasted_iota(jnp.int32, cos.shape, 1)
    first_half = (lane % DA_HEAD_DIM) < DA_HEAD_DIM // 2

    def rotated(b):
        x = p_ref[:, b * LANE:(b + 1) * LANE]
        rot = jnp.where(first_half, pltpu.roll(x, LANE - DA_HEAD_DIM // 2, 1), pltpu.roll(x, DA_HEAD_DIM // 2, 1))
        return x * cos + rot * sin

    ones = jnp.ones((FLASH_ONES_ROWS, cos.shape[0]), BF16)
    for h in range(DA_HEADS):
        qt_ref[h * hw:(h + 1) * hw, :] = (rotated(h) * (LOG2_E * DA_HEAD_DIM ** -0.5)).T.astype(BF16)
        k_ref[:, h * hw:(h + 1) * hw] = rotated(DA_HEADS + h).astype(BF16)
        base = h * (hw + FLASH_ONES_ROWS)
        vt_ref[base:base + hw, :] = p_ref[:, (2 * DA_HEADS + h) * LANE:(2 * DA_HEADS + h + 1) * LANE].T.astype(BF16)
        vt_ref[base + hw:base + hw + FLASH_ONES_ROWS, :] = ones


def rope_da(p, cos, sin):
    n_rows = p.shape[0]
    assert 2 * DA_HEAD_DIM == LANE
    vt_rows = DA_HEADS * (LANE + FLASH_ONES_ROWS)
    return pl.pallas_call(
        _rope_da_kernel,
        grid=(n_rows // ROW_TILE,),
        in_specs=[pl.BlockSpec((ROW_TILE, 3 * DA_WIDTH), lambda i: (i, 0)),
                  pl.BlockSpec((ROW_TILE, LANE), lambda i: (i, 0)),
                  pl.BlockSpec((ROW_TILE, LANE), lambda i: (i, 0))],
        out_specs=[pl.BlockSpec((DA_WIDTH, ROW_TILE), lambda i: (0, i)),
                   pl.BlockSpec((ROW_TILE, DA_WIDTH), lambda i: (i, 0)),
                   pl.BlockSpec((vt_rows, ROW_TILE), lambda i: (0, i))],
        out_shape=[jax.ShapeDtypeStruct((DA_WIDTH, n_rows), BF16),
                   jax.ShapeDtypeStruct((n_rows, DA_WIDTH), BF16),
                   jax.ShapeDtypeStruct((vt_rows, n_rows), BF16)],
        compiler_params=_params(("parallel",)),
        name="rope_da",
    )(p, cos, sin)


def diff_attention(qt, k, vt, lam_full, subln, *, seq, ctx_len, lambda_init):
    n_rows = k.shape[0]
    tq = _token_tile(n_rows)
    hw = 2 * DA_HEAD_DIM
    ones_rows = FLASH_ONES_ROWS
    return pl.pallas_call(
        functools.partial(_flash_kernel, kv=tq, seq=seq, ctx_len=ctx_len, out_scale=1.0 - lambda_init),
        grid=(DA_HEADS, n_rows // tq),
        in_specs=[pl.BlockSpec(memory_space=pltpu.SMEM),
                  pl.BlockSpec((hw, tq), lambda h, i: (h, i)),
                  pl.BlockSpec((n_rows, hw), lambda h, i: (0, h)),
                  pl.BlockSpec((hw + ones_rows, n_rows), lambda h, i: (h, 0)),
                  pl.BlockSpec((1, hw), lambda h, i: (0, 0))],
        out_specs=pl.BlockSpec((tq, hw), lambda h, i: (i, h)),
        out_shape=jax.ShapeDtypeStruct((n_rows, DA_WIDTH), BF16),
        scratch_shapes=[pltpu.VMEM((2, 1, tq), F32), pltpu.VMEM((2, 1, tq), F32),
                        pltpu.VMEM((2, hw + ones_rows, tq), F32)],
        compiler_params=pltpu.CompilerParams(dimension_semantics=("parallel", "parallel"),
                                             vmem_limit_bytes=FLASH_VMEM_LIMIT),
        name="diff_attention",
    )(lam_full.reshape(1), qt, k, vt, subln.reshape(1, hw))


def _route_kernel(lg_ref, b_ref, tri_ref, eidx_ref, w_ref, rank_ref, cnt_ref, carry_ref):
    t = lg_ref.shape[1]

    @pl.when(pl.program_id(0) == 0)
    def _():
        carry_ref[...] = jnp.zeros_like(carry_ref)

    scores = jax.nn.sigmoid(lg_ref[...])
    choice = (scores + b_ref[...]).reshape(N_GROUPS, GROUP_SIZE, t)
    s3 = scores.reshape(N_GROUPS, GROUP_SIZE, t)
    member = lax.broadcasted_iota(jnp.int32, choice.shape, 1)
    group = lax.broadcasted_iota(jnp.int32, (N_GROUPS, 1, t), 0)
    expert = lax.broadcasted_iota(jnp.int32, choice.shape, 0) * GROUP_SIZE + member
    neg_inf = -jnp.inf
    m1 = jnp.max(choice, axis=1, keepdims=True)
    first = jnp.min(jnp.where(choice == m1, member, GROUP_SIZE), axis=1, keepdims=True)
    m2 = jnp.max(jnp.where(member == first, neg_inf, choice), axis=1, keepdims=True)
    gscore = m1 + m2
    gsel = jnp.zeros(gscore.shape, F32)
    for _ in range(TOPK_GROUPS):
        m = jnp.max(gscore, axis=0, keepdims=True)
        f = jnp.min(jnp.where(gscore == m, group, N_GROUPS), axis=0, keepdims=True)
        hit = group == f
        gsel = jnp.where(hit, 1.0, gsel)
        gscore = jnp.where(hit, neg_inf, gscore)
    cand = jnp.where(gsel > 0.0, choice, neg_inf)
    esel = jnp.zeros(choice.shape, F32)
    picks = []
    for _ in range(TOP_K):
        m = jnp.max(jnp.max(cand, axis=1, keepdims=True), axis=0, keepdims=True)
        f = jnp.min(jnp.min(jnp.where(cand == m, expert, N_EXPERTS), axis=1, keepdims=True), axis=0, keepdims=True)
        hit = expert == f
        esel = jnp.where(hit, 1.0, esel)
        cand = jnp.where(hit, neg_inf, cand)
        picks.append(f)
    w = s3 * esel
    denom = jnp.sum(jnp.sum(w, axis=1, keepdims=True), axis=0, keepdims=True) + 1e-20
    w = w / denom * ROUTED_SCALE
    sel = esel.reshape(N_EXPERTS, t)
    before = jnp.dot(sel.astype(BF16), tri_ref[...], preferred_element_type=F32) + carry_ref[...]
    before = before.reshape(N_GROUPS, GROUP_SIZE, t)
    pick = lambda a, hit: jnp.sum(jnp.sum(jnp.where(hit, a, 0.0), axis=1, keepdims=True), axis=0).reshape(1, t)
    for k, f in enumerate(picks):
        hit = expert == f
        eidx_ref[k:k + 1, :] = f.reshape(1, t)
        w_ref[k:k + 1, :] = pick(w, hit)
        rank_ref[k:k + 1, :] = pick(before, hit).astype(jnp.int32)
    carry_ref[...] += jnp.sum(sel, axis=1, keepdims=True)
    cnt_ref[...] = carry_ref[...]


def route(logits_t, bias, lo, hi):
    t = _token_tile(logits_t.shape[1])
    n = hi - lo
    tile0 = lo // t
    tri = (jnp.arange(t)[:, None] < jnp.arange(t)[None, :]).astype(BF16)
    tok = lambda dt: jax.ShapeDtypeStruct((TOP_K, n), dt)
    tok_spec = pl.BlockSpec((TOP_K, t), lambda i: (0, i))
    return pl.pallas_call(
        _route_kernel,
        grid=(n // t,),
        in_specs=[pl.BlockSpec((N_EXPERTS, t), lambda i: (0, tile0 + i)),
                  pl.BlockSpec((N_EXPERTS, 1), lambda i: (0, 0)),
                  pl.BlockSpec((t, t), lambda i: (0, 0))],
        out_specs=[tok_spec, tok_spec, tok_spec, pl.BlockSpec((N_EXPERTS, 1), lambda i: (0, 0))],
        out_shape=[tok(jnp.int32), tok(F32), tok(jnp.int32), jax.ShapeDtypeStruct((N_EXPERTS, 1), F32)],
        scratch_shapes=[pltpu.VMEM((N_EXPERTS, 1), F32)],
        compiler_params=_params(("arbitrary",)),
        name="route",
    )(logits_t, bias.reshape(N_EXPERTS, 1), tri)


def _slot_kernel(start_ref, eidx_ref, rank_ref, dest_ref):
    e = eidx_ref[...]
    d = rank_ref[...]
    for x in range(N_EXPERTS):
        d = d + jnp.where(e == x, start_ref[x], 0)
    dest_ref[...] = d


def slot_index(pad_start, eidx, rank):
    n = eidx.shape[1]
    t = _token_tile(n)
    spec = pl.BlockSpec((TOP_K, t), lambda i: (0, i))
    return pl.pallas_call(
        _slot_kernel,
        grid=(n // t,),
        in_specs=[pl.BlockSpec(memory_space=pltpu.SMEM), spec, spec],
        out_specs=spec,
        out_shape=jax.ShapeDtypeStruct((TOP_K, n), jnp.int32),
        compiler_params=_params(("parallel",)),
        name="slot_index",
    )(pad_start, eidx, rank)


def _sc_worker():
    return lax.axis_index("s") * SC_CORES + lax.axis_index("c")


def sc_dispatch(h, dest3, n_slots, row0):
    d = h.shape[1]
    n_win = dest3.shape[0]
    mesh = plsc.VectorSubcoreMesh(core_axis_name="c", subcore_axis_name="s")

    @functools.partial(
        pl.kernel, mesh=mesh, out_type=jax.ShapeDtypeStruct((n_slots, d), h.dtype),
        scratch_types=[pltpu.VMEM((TOP_K, SC_WINDOW), jnp.int32), pltpu.VMEM((SC_WINDOW, d), h.dtype),
                       pltpu.SemaphoreType.DMA])
    def k(h_hbm, dest_hbm, out_hbm, idx_v, rows_v, sem):
        wid = _sc_worker()

        @pl.loop(0, -(-n_win // SC_WORKERS))
        def _(it):
            w = it * SC_WORKERS + wid

            @pl.when(w < n_win)
            def _():
                pltpu.sync_copy(dest_hbm.at[w], idx_v)
                pltpu.sync_copy(h_hbm.at[pl.ds(row0 + w * SC_WINDOW, SC_WINDOW)], rows_v)
                copies = [pltpu.async_copy(rows_v, out_hbm.at[idx_v.at[j]], sem) for j in range(TOP_K)]
                for c in copies:
                    c.wait()

    return k(h, dest3)


def sc_combine_gather(y, dest3):
    d = y.shape[1]
    n_win = dest3.shape[0]
    n = n_win * SC_WINDOW
    mesh = plsc.VectorSubcoreMesh(core_axis_name="c", subcore_axis_name="s")

    @functools.partial(
        pl.kernel, mesh=mesh, out_type=jax.ShapeDtypeStruct((TOP_K, n, d), y.dtype),
        scratch_types=[pltpu.VMEM((TOP_K, SC_WINDOW), jnp.int32), pltpu.VMEM((2, SC_WINDOW, d), y.dtype),
                       pltpu.SemaphoreType.DMA, pltpu.SemaphoreType.DMA,
                       pltpu.SemaphoreType.DMA, pltpu.SemaphoreType.DMA])
    def k(y_hbm, dest_hbm, out_hbm, idx_v, rows_v, gsem0, gsem1, osem0, osem1):
        wid = _sc_worker()
        gsem, osem = (gsem0, gsem1), (osem0, osem1)

        @pl.loop(0, -(-n_win // SC_WORKERS))
        def _(it):
            w = it * SC_WORKERS + wid

            @pl.when(w < n_win)
            def _():
                pltpu.sync_copy(dest_hbm.at[w], idx_v)
                gather = lambda j: pltpu.async_copy(y_hbm.at[idx_v.at[j]], rows_v.at[j % 2], gsem[j % 2])
                g = [None] * TOP_K
                o = [None] * TOP_K
                g[0] = gather(0)
                for j in range(TOP_K):
                    if j + 1 < TOP_K:
                        if j >= 1:
                            o[j - 1].wait()
                        g[j + 1] = gather(j + 1)
                    g[j].wait()
                    o[j] = pltpu.async_copy(rows_v.at[j % 2], out_hbm.at[j, pl.ds(w * SC_WINDOW, SC_WINDOW)],
                                            osem[j % 2])
                o[TOP_K - 2].wait()
                o[TOP_K - 1].wait()

    return k(y, dest3)


def _expert_ffn_kernel(be_ref, bv_ref, x_ref, wg_ref, wu_ref, wd_ref, o_ref, wg_s, wu_s, wd_s):
    b = pl.program_id(0)
    valid = bv_ref[b]
    new_expert = (b == 0) | (be_ref[b] != be_ref[jnp.maximum(b - 1, 0)])

    @pl.when(new_expert)
    def _():
        wg_s[...] = wg_ref[0, 0].astype(BF16)
        wu_s[...] = wu_ref[0, 0].astype(BF16)
        wd_s[...] = wd_ref[0, 0].astype(BF16)

    sub = x_ref.shape[0] // MOE_SUB_BLOCKS

    def sub_block(r):
        row = lax.broadcasted_iota(jnp.int32, (sub, 1), 0) + r * sub
        rows = pl.ds(r * sub, sub)
        lo, hi = _unpack_bf16_pair(jnp.where(row < valid, x_ref[rows, :], 0))
        x = jnp.concatenate([lo.astype(BF16), hi.astype(BF16)], axis=1)
        a = jnp.dot(x, wg_s[...], preferred_element_type=F32)
        a = a * jax.nn.sigmoid(a) * jnp.dot(x, wu_s[...], preferred_element_type=F32)
        y = jnp.dot(a.astype(BF16), wd_s[...], preferred_element_type=F32)
        half = y.shape[1] // 2
        o_ref[rows, :] = _pack_bf16_pair(y[:, :half], y[:, half:])

    for live in range(1, MOE_SUB_BLOCKS + 1):
        upper = valid <= live * sub if live < MOE_SUB_BLOCKS else True

        @pl.when((valid > (live - 1) * sub) & upper)
        def _():
            for r in range(live):
                sub_block(r)


def expert_ffn(xg, block_expert, block_valid, wg, wu, wd, layer):
    n_slots, dp = xg.shape
    d, f = wg.shape[-2:]
    grid_spec = pltpu.PrefetchScalarGridSpec(
        num_scalar_prefetch=2,
        grid=(n_slots // MOE_BLOCK,),
        in_specs=[pl.BlockSpec((MOE_BLOCK, dp), lambda b, be, bv: (b, 0)),
                  pl.BlockSpec((1, 1, d, f), lambda b, be, bv: (layer, be[b], 0, 0)),
                  pl.BlockSpec((1, 1, d, f), lambda b, be, bv: (layer, be[b], 0, 0)),
                  pl.BlockSpec((1, 1, f, d), lambda b, be, bv: (layer, be[b], 0, 0))],
        out_specs=pl.BlockSpec((MOE_BLOCK, dp), lambda b, be, bv: (b, 0)),
        scratch_shapes=[pltpu.VMEM((d, f), BF16), pltpu.VMEM((d, f), BF16), pltpu.VMEM((f, d), BF16)],
    )
    return pl.pallas_call(
        _expert_ffn_kernel,
        grid_spec=grid_spec,
        out_shape=jax.ShapeDtypeStruct((n_slots, dp), jnp.int32),
        compiler_params=_params(("arbitrary",)),
        name="expert_ffn",
    )(block_expert, block_valid, xg, wg, wu, wd)


def _combine_kernel(yg_ref, w_ref, h_ref, swg_ref, swu_ref, swd_ref, xs_ref, mods_ref, *rest, tile0, gate_idx, seq):
    o_ref = rest[-1]
    h = h_ref[...]
    a = jnp.dot(h, swg_ref[...], preferred_element_type=F32)
    a = a * jax.nn.sigmoid(a) * jnp.dot(h, swu_ref[...], preferred_element_type=F32)
    acc = jnp.dot(a.astype(BF16), swd_ref[...], preferred_element_type=F32)
    half = acc.shape[1] // 2
    acc_lo, acc_hi = acc[:, :half], acc[:, half:]
    wt = w_ref[...].T
    for k in range(TOP_K):
        lo, hi = _unpack_bf16_pair(yg_ref[k])
        acc_lo = acc_lo + wt[:, k:k + 1] * lo
        acc_hi = acc_hi + wt[:, k:k + 1] * hi
    tm = h.shape[0]
    gate = _row_mod(mods_ref, gate_idx, (tile0 + pl.program_id(0)) * tm, tm, seq)
    o_ref[:, :half] = xs_ref[:, :half] + gate[:, :half] * acc_lo
    o_ref[:, half:] = xs_ref[:, half:] + gate[:, half:] * acc_hi


def combine(yg, w, h, swg, swu, swd, residual, lo, prev):
    xs, mods, gate_idx, seq = residual
    n_all, d = h.shape
    n = w.shape[1]
    f = swg.shape[-1]
    tm = ROW_TILE
    tile0 = lo // tm
    in_specs = [pl.BlockSpec((TOP_K, tm, d // 2), lambda i: (0, i, 0)),
                pl.BlockSpec((TOP_K, tm), lambda i: (0, i)),
                pl.BlockSpec((tm, d), lambda i: (tile0 + i, 0)),
                pl.BlockSpec((d, f), lambda i: (0, 0)),
                pl.BlockSpec((d, f), lambda i: (0, 0)),
                pl.BlockSpec((f, d), lambda i: (0, 0)),
                pl.BlockSpec((tm, d), lambda i: (tile0 + i, 0)),
                pl.BlockSpec((2, 6, d), lambda i: (0, 0, 0))]
    args = [yg, w, h, swg, swu, swd, xs, mods]
    aliases = {}
    if prev is not None:
        in_specs.append(pl.BlockSpec(memory_space=pl.ANY))
        args.append(prev)
        aliases = {len(args) - 1: 0}
    return pl.pallas_call(
        functools.partial(_combine_kernel, tile0=tile0, gate_idx=gate_idx, seq=seq),
        grid=(n // tm,),
        in_specs=in_specs,
        out_specs=pl.BlockSpec((tm, d), lambda i: (tile0 + i, 0)),
        out_shape=jax.ShapeDtypeStruct((n_all, d), F32),
        input_output_aliases=aliases,
        compiler_params=_params(("parallel",)),
        name="moe_combine",
    )(*args)


def moe(h, h_packed, logits_t, bias, wg, wu, wd, layer, swg, swu, swd, residual):
    n = h.shape[0]
    t = _token_tile(n)
    cut = (n // t + 1) // 2 * t
    shared = (swg.astype(BF16), swu.astype(BF16), swd.astype(BF16))
    staged = [_moe_experts(h_packed, logits_t, bias, wg, wu, wd, layer, lo, hi) for lo, hi in ((0, cut), (cut, n))]
    out = None
    for (yg, w), lo in zip(staged, (0, cut)):
        out = combine(yg, w, h, *shared, residual, lo, out)
    return out


def _moe_experts(h_packed, logits_t, bias, wg, wu, wd, layer, lo, hi):
    n = hi - lo
    eidx, w, rank, counts = route(logits_t, bias, lo, hi)
    counts = counts.reshape(N_EXPERTS).astype(jnp.int32)
    padded = (counts + MOE_BLOCK - 1) // MOE_BLOCK * MOE_BLOCK
    pad_end = jnp.cumsum(padded)
    pad_start = pad_end - padded
    n_slots = n * TOP_K + N_EXPERTS * MOE_BLOCK
    starts = jnp.arange(n_slots // MOE_BLOCK, dtype=jnp.int32) * MOE_BLOCK
    owner = jnp.sum((pad_end[None, :] <= starts[:, None]).astype(jnp.int32), axis=1)
    block_expert = jnp.minimum(owner, N_EXPERTS - 1)
    member = (block_expert[:, None] == jnp.arange(N_EXPERTS, dtype=jnp.int32)[None, :]).astype(jnp.int32)
    left = jnp.sum(member * (counts + pad_start)[None, :], axis=1) - starts
    block_valid = jnp.clip(left, 0, MOE_BLOCK).astype(jnp.int32)
    dest = slot_index(pad_start.astype(jnp.int32), eidx, rank)
    dest3 = dest.reshape(TOP_K, n // SC_WINDOW, SC_WINDOW).transpose(1, 0, 2)
    xg = sc_dispatch(h_packed, dest3, n_slots, lo)
    y = expert_ffn(xg, block_expert, block_valid, wg, wu, wd, layer)
    return sc_combine_gather(y, dest3), w


def mixer_ab(h, w_in, w_out, decay_logit, conv_w, conv_b, w1, b1, f1, w2, b2, f2, w3, skip, rope, dft, residual, *,
             seq, ctx_len):
    n_rows = h.shape[0]
    tm = _token_tile(n_rows)
    p = mm([(h, w_in.astype(BF16))], F32, tm, 512, name="ab_in_proj")
    qkv_w = 2 * RET_QK + RET_V
    n_qk = 2 * RET_QK // LANE
    scales = (1.0,) * (RET_QK // LANE) + (RET_DK ** -0.5,) * (RET_QK // LANE) + (1.0,) * (RET_V // LANE)
    qkv = rope_cast(p, rope[0], rope[1], width=qkv_w, n_rot_blocks=n_qk, head_dim=RET_DK, scales=scales,
                    out_dtype=F32)
    log_g = jax.nn.log_sigmoid(decay_logit.astype(F32))
    ret = retention(qkv, p, log_g, jnp.exp(RET_CHUNK * log_g), seq=seq)
    v, x1, x2 = shortconv(p, conv_w, conv_b, seq=seq)
    filt = (w1, b1, f1, w2, b2, f2, w3)
    hy_x = long_conv_two_stage(dft["x"], hyena_filter_taps(seq, *filt), v, x1, x2, skip, seq)
    hy_c = long_conv_one_stage(dft["c"], hyena_filter_taps(ctx_len, *filt), v[seq:], x1[seq:], x2[seq:], skip)
    hy = jnp.concatenate([hy_x, hy_c], axis=0)
    w_out = w_out.astype(BF16)
    return mm([(ret, w_out[:RET_V]), (hy, w_out[RET_V:])], F32, tm, 512, residual=residual, name="ab_out_proj")


def mixer_da(h, w_in, w_out, lam, subln, lambda_init, rope, residual, *, seq, ctx_len):
    n_rows = h.shape[0]
    tm = _token_tile(n_rows)
    p = mm([(h, w_in.astype(BF16))], F32, tm, 512, name="da_in_proj")
    qt, k, vt = rope_da(p, rope[0], rope[1])
    lam_f = lam.astype(F32)
    lam_full = jnp.exp(jnp.sum(lam_f[0] * lam_f[1])) - jnp.exp(jnp.sum(lam_f[2] * lam_f[3])) + lambda_init
    o = diff_attention(qt, k, vt, lam_full, subln, seq=seq, ctx_len=ctx_len, lambda_init=lambda_init)
    return mm([(o, w_out.astype(BF16))], F32, tm, 512, residual=residual, name="da_out_proj")


def kernel(x, c, ctx, c_ctx, w_ada, b_ada, norm_mix, norm_ffn, ab_w_in, ab_w_out, ret_decay_logit, hy_conv_w, hy_conv_b, hy_w1, hy_b1, hy_freq1, hy_w2, hy_b2, hy_freq2, hy_w3, hy_skip, da_w_in, da_w_out, da_lambda, da_subln, router_w, router_b, exp_w_gate, exp_w_up, exp_w_down, sh_w_gate, sh_w_up, sh_w_down, norm_final):
    batch, seq, d = x.shape
    ctx_len = ctx.shape[1]
    assert batch == 1 and seq % ROW_TILE == 0 and ctx_len == ROW_TILE
    depth = w_ada.shape[0]
    n_rows = seq + ctx_len

    xs = jnp.concatenate([x[0], ctx[0]], axis=0)
    cv = jnp.zeros((SUBLANE, d), F32).at[0].set(c_ctx).at[1].set(c[0])
    mods = adaln(cv, w_ada, b_ada)[:, :2].reshape(depth, 2, 6, d)

    rope_ret = rope_tables(seq, ctx_len, RET_DK)
    rope_da = rope_tables(seq, ctx_len, DA_HEAD_DIM)
    dft = dict(x=dft_tables_two_stage(2 * seq), c=dft_tables_one_stage(2 * ctx_len))
    common = dict(n_rows=n_rows, seq=seq)

    for i in range(depth):
        j = i // 2
        (h,) = norm_mod(xs, norm_mix[i], mods=mods[i], shift_idx=0, scale_idx=1, **common)
        residual = (xs, mods[i], 2, seq)
        if i % 2 == 0:
            xs = mixer_ab(h, ab_w_in[j], ab_w_out[j], ret_decay_logit[j], hy_conv_w[j], hy_conv_b[j], hy_w1[j],
                          hy_b1[j], hy_freq1[j], hy_w2[j], hy_b2[j], hy_freq2[j], hy_w3[j], hy_skip[j], rope_ret, dft,
                          residual, seq=seq, ctx_len=ctx_len)
        else:
            lambda_init = 0.8 - 0.6 * math.exp(-0.3 * i)
            xs = mixer_da(h, da_w_in[j], da_w_out[j], da_lambda[j], da_subln[j], lambda_init, rope_da, residual,
                          seq=seq, ctx_len=ctx_len)
        h, logits_t, h_packed = norm_mod(xs, norm_ffn[i], mods=mods[i], shift_idx=3, scale_idx=4,
                                         router_wt=router_w[i].T, **common)
        xs = moe(h, h_packed, logits_t, router_b[i], exp_w_gate, exp_w_up, exp_w_down, i,
                 sh_w_gate[i], sh_w_up[i], sh_w_down[i], (xs, mods[i], 5, seq))
    (out,) = norm_mod(xs, norm_final, n_rows=seq, seq=seq, out_dtype=F32)
    return out[None]
```

```python
import functools
import math

import jax
import jax.numpy as jnp
from jax import lax
from jax.experimental import pallas as pl
from jax.experimental.pallas import tpu as pltpu
from jax.experimental.pallas import tpu_sc as plsc

F32 = jnp.float32
BF16 = jnp.bfloat16
HIGHEST = lax.Precision.HIGHEST

D_MODEL = 1024
DEPTH = 4
GRID_W = 64
EPS = 1e-6
ROPE_BASE = 10000.0

RET_HEADS = 4
RET_DK = 128
RET_DV = 256
RET_CHUNK = 128
RET_QK = RET_HEADS * RET_DK
RET_V = RET_HEADS * RET_DV

HY_WIDTH = 512
HY_ORDER = 2
HY_BANDS = 16
HY_EMB = 2 * HY_BANDS + 1
HY_FFN = 64
HY_DECAY_TARGET = 1e-2
HY_FAST_DECAY = 0.3
HY_SLOW_DECAY = 1.5
HY_ZCOLS = 64
HY_VALID_COL = HY_EMB
FFT_N2 = 128

AB_IN = 2 * RET_QK + 2 * RET_V + (HY_ORDER + 1) * HY_WIDTH
AB_CAT = RET_V + HY_WIDTH

DA_HEADS = 8
DA_HEAD_DIM = 64
DA_WIDTH = DA_HEADS * 2 * DA_HEAD_DIM

N_EXPERTS = 64
TOP_K = 8
N_GROUPS = 8
TOPK_GROUPS = 4
GROUP_SIZE = N_EXPERTS // N_GROUPS
EXPERT_DIM = 256
ROUTED_SCALE = 2.5
MOE_BLOCK = 512
MOE_SUB_BLOCKS = 2
SC_CORES = 2
SC_SUBCORES = 16
SC_WORKERS = SC_CORES * SC_SUBCORES
SC_WINDOW = 64

LANE = 128
SUBLANE = 8
ROW_TILE = 256
MAX_TOKEN_TILE = 1280
VMEM_LIMIT = 48 * 1024 * 1024
FLASH_VMEM_LIMIT = 56 * 1024 * 1024
NEG_BIG = -1e30
LOG2_E = 1.4426950408889634
FLASH_ONES_ROWS = 16
FLASH_INIT_KEYS = 16
FLASH_LAZY_HEADROOM = 60.0


def _params(sem):
    return pltpu.CompilerParams(dimension_semantics=sem, vmem_limit_bytes=VMEM_LIMIT)


def _token_tile(n):
    best = ROW_TILE
    t = ROW_TILE
    while t <= min(n, MAX_TOKEN_TILE):
        if n % t == 0:
            best = t
        t += ROW_TILE
    return best


def _row_mod(mods_ref, idx, row0, n, seq):
    row = row0 + lax.broadcasted_iota(jnp.int32, (n, 1), 0)
    return jnp.where(row >= seq, mods_ref[0, idx:idx + 1, :], mods_ref[1, idx:idx + 1, :])


def _mm_kernel(*refs, n_pairs, has_epi, gate_idx, seq):
    acc = None
    for p in range(n_pairs):
        a = refs[2 * p][...].astype(BF16)
        b = refs[2 * p + 1][...].astype(BF16)
        d = jnp.dot(a, b, preferred_element_type=F32)
        acc = d if acc is None else acc + d
    idx = 2 * n_pairs
    if has_epi:
        acc = refs[idx][...] * (acc + refs[idx + 1][...] * refs[idx + 2][...])
        idx += 3
    if gate_idx is not None:
        tm = acc.shape[0]
        acc = refs[idx][...] + _row_mod(refs[idx + 1], gate_idx, pl.program_id(0) * tm, tm, seq) * acc
        idx += 2
    o_ref = refs[idx]
    o_ref[...] = acc.astype(o_ref.dtype)


def mm(pairs, out_dtype, tm, tn, epi=None, residual=None, name="mm"):
    m = pairs[0][0].shape[0]
    n = pairs[0][1].shape[1]
    assert m % tm == 0 and n % tn == 0
    in_specs, args = [], []
    for a, b in pairs:
        k = a.shape[1]
        in_specs += [pl.BlockSpec((tm, k), lambda i, j: (i, 0)), pl.BlockSpec((k, tn), lambda i, j: (0, j))]
        args += [a, b]
    if epi is not None:
        in_specs += [pl.BlockSpec((tm, tn), lambda i, j: (i, j)), pl.BlockSpec((1, tn), lambda i, j: (0, j)),
                     pl.BlockSpec((tm, tn), lambda i, j: (i, j))]
        args += list(epi)
    gate_idx = seq = None
    if residual is not None:
        res, mods, gate_idx, seq = residual
        in_specs += [pl.BlockSpec((tm, tn), lambda i, j: (i, j)), pl.BlockSpec((2, 6, tn), lambda i, j: (0, 0, j))]
        args += [res, mods]
    return pl.pallas_call(
        functools.partial(_mm_kernel, n_pairs=len(pairs), has_epi=epi is not None, gate_idx=gate_idx, seq=seq),
        grid=(m // tm, n // tn),
        in_specs=in_specs,
        out_specs=pl.BlockSpec((tm, tn), lambda i, j: (i, j)),
        out_shape=jax.ShapeDtypeStruct((m, n), out_dtype),
        compiler_params=_params(("parallel", "parallel")),
        name=name,
    )(*args)


def _adaln_kernel(cv_ref, w_ref, b_ref, o_ref):
    cv = cv_ref[...]
    s = cv * jax.nn.sigmoid(cv)
    o_ref[0] = jnp.dot(s, w_ref[0], precision=HIGHEST, preferred_element_type=F32) + b_ref[0]


def adaln(cv, w_ada, b_ada):
    depth, d, n = w_ada.shape
    tn = 1536
    return pl.pallas_call(
        _adaln_kernel,
        grid=(depth, n // tn),
        in_specs=[pl.BlockSpec((SUBLANE, d), lambda l, j: (0, 0)),
                  pl.BlockSpec((1, d, tn), lambda l, j: (l, 0, j)),
                  pl.BlockSpec((1, 1, tn), lambda l, j: (l, 0, j))],
        out_specs=pl.BlockSpec((1, SUBLANE, tn), lambda l, j: (l, 0, j)),
        out_shape=jax.ShapeDtypeStruct((depth, SUBLANE, n), F32),
        compiler_params=_params(("parallel", "parallel")),
        name="adaln",
    )(cv, w_ada, b_ada.reshape(depth, 1, n))


def _norm_mod_kernel(*refs, shift_idx, scale_idx, has_router, seq):
    it = iter(refs)
    x = next(it)[...]
    mods_ref = next(it) if shift_idx is not None else None
    g_ref = next(it)
    wr_ref = next(it) if has_router else None
    h_ref = next(it)
    y = x * lax.rsqrt(jnp.mean(x * x, axis=-1, keepdims=True) + EPS) * g_ref[...]
    if shift_idx is not None:
        tm = x.shape[0]
        row0 = pl.program_id(0) * tm
        y = y * (1.0 + _row_mod(mods_ref, scale_idx, row0, tm, seq)) + _row_mod(mods_ref, shift_idx, row0, tm, seq)
    h_ref[...] = y.astype(h_ref.dtype)
    if has_router:
        lg_ref = next(it)
        lg_ref[...] = lax.dot_general(wr_ref[...], y, (((1,), (1,)), ((), ())),
                                      precision=HIGHEST, preferred_element_type=F32)
        half = y.shape[1] // 2
        next(it)[...] = _pack_bf16_pair(y[:, :half], y[:, half:])


def _pack_bf16_pair(a, b):
    def rounded(x):
        u = lax.bitcast_convert_type(x, jnp.int32)
        return u + 0x7FFF + (lax.shift_right_logical(u, 16) & 1)
    return lax.shift_right_logical(rounded(a), 16) | (rounded(b) & -65536)


def _unpack_bf16_pair(w):
    return (lax.bitcast_convert_type(lax.shift_left(w, 16), F32),
            lax.bitcast_convert_type(w & -65536, F32))


def norm_mod(xs, g, *, n_rows, seq, mods=None, shift_idx=None, scale_idx=None, router_wt=None, out_dtype=BF16):
    d = xs.shape[1]
    tm = _token_tile(n_rows)
    row = pl.BlockSpec((tm, d), lambda i: (i, 0))
    in_specs, args = [row], [xs]
    if shift_idx is not None:
        in_specs.append(pl.BlockSpec((2, 6, d), lambda i: (0, 0, 0)))
        args.append(mods)
    in_specs.append(pl.BlockSpec((1, d), lambda i: (0, 0)))
    args.append(g.reshape(1, d))
    has_router = router_wt is not None
    if has_router:
        in_specs.append(pl.BlockSpec(router_wt.shape, lambda i: (0, 0)))
        args.append(router_wt)
    out_specs = [row]
    out_shape = [jax.ShapeDtypeStruct((n_rows, d), out_dtype)]
    if has_router:
        out_specs.append(pl.BlockSpec((N_EXPERTS, tm), lambda i: (0, i)))
        out_shape.append(jax.ShapeDtypeStruct((N_EXPERTS, n_rows), F32))
        out_specs.append(pl.BlockSpec((tm, d // 2), lambda i: (i, 0)))
        out_shape.append(jax.ShapeDtypeStruct((n_rows, d // 2), jnp.int32))
    return pl.pallas_call(
        functools.partial(_norm_mod_kernel, shift_idx=shift_idx, scale_idx=scale_idx, has_router=has_router, seq=seq),
        grid=(n_rows // tm,),
        in_specs=in_specs,
        out_specs=out_specs,
        out_shape=out_shape,
        compiler_params=_params(("parallel",)),
        name="norm_mod",
    )(*args)


def _rope_kernel(p_ref, cos_ref, sin_ref, o_ref, *, n_rot_blocks, head_dim, scales):
    cos = cos_ref[...]
    sin = sin_ref[...]
    for b in range(len(scales)):
        x = p_ref[:, b * LANE:(b + 1) * LANE]
        if b < n_rot_blocks:
            if head_dim == LANE:
                rot = pltpu.roll(x, LANE // 2, 1)
            else:
                lane = lax.broadcasted_iota(jnp.int32, x.shape, 1)
                first_half = (lane % head_dim) < head_dim // 2
                rot = jnp.where(first_half, pltpu.roll(x, LANE - head_dim // 2, 1), pltpu.roll(x, head_dim // 2, 1))
            x = x * cos + rot * sin
        if scales[b] != 1.0:
            x = x * scales[b]
        o_ref[:, b * LANE:(b + 1) * LANE] = x.astype(o_ref.dtype)


def rope_cast(p, cos, sin, *, width, n_rot_blocks, head_dim, scales, out_dtype):
    n_rows = p.shape[0]
    return pl.pallas_call(
        functools.partial(_rope_kernel, n_rot_blocks=n_rot_blocks, head_dim=head_dim, scales=scales),
        grid=(n_rows // ROW_TILE,),
        in_specs=[pl.BlockSpec((ROW_TILE, width), lambda i: (i, 0)),
                  pl.BlockSpec((ROW_TILE, LANE), lambda i: (i, 0)),
                  pl.BlockSpec((ROW_TILE, LANE), lambda i: (i, 0))],
        out_specs=pl.BlockSpec((ROW_TILE, width), lambda i: (i, 0)),
        out_shape=jax.ShapeDtypeStruct((n_rows, width), out_dtype),
        compiler_params=_params(("parallel",)),
        name="rope_cast",
    )(p, cos, sin)


def rope_tables(seq, ctx_len, head_dim):
    n_freq = head_dim // 4
    inv = ROPE_BASE ** (-jnp.arange(n_freq, dtype=F32) / n_freq)
    rows = seq // GRID_W
    row = jnp.repeat(jnp.arange(rows, dtype=F32), GRID_W)
    col = jnp.tile(jnp.arange(GRID_W, dtype=F32), rows)
    ang = jnp.concatenate([row[:, None] * inv, col[:, None] * inv], axis=-1)
    cos, sin = jnp.cos(ang), jnp.sin(ang)
    cos = jnp.concatenate([cos, cos], axis=-1)
    sin = jnp.concatenate([-sin, sin], axis=-1)
    reps = LANE // head_dim
    cos, sin = jnp.tile(cos, (1, reps)), jnp.tile(sin, (1, reps))
    cos = jnp.concatenate([cos, jnp.ones((ctx_len, LANE), F32)], axis=0)
    sin = jnp.concatenate([sin, jnp.zeros((ctx_len, LANE), F32)], axis=0)
    return cos, sin


def _ret_kernel(lg_ref, gc_ref, q_ref, k_ref, v_ref, *rest, reverse):
    if reverse:
        yf_ref, gate_ref, o_ref, s_ref = rest
    else:
        o_ref, s_ref = rest
    c = RET_CHUNK

    @pl.when(pl.program_id(0) == 0)
    def _():
        s_ref[...] = jnp.zeros_like(s_ref)

    ii = lax.broadcasted_iota(jnp.int32, (c, c), 0)
    jj = lax.broadcasted_iota(jnp.int32, (c, c), 1)
    rel = ((jj - ii) if reverse else (ii - jj)).astype(F32)
    pos = lax.broadcasted_iota(jnp.int32, (c, 1), 0).astype(F32)
    for h in range(RET_HEADS):
        lg = lg_ref[h]
        dec = jnp.where(rel >= 0, jnp.exp(jnp.maximum(rel, 0.0) * lg), 0.0)
        if reverse:
            q_dec = jnp.exp((c - pos) * lg)
            k_dec = jnp.exp(pos * lg)
        else:
            q_dec = jnp.exp((pos + 1.0) * lg)
            k_dec = jnp.exp((c - 1.0 - pos) * lg)
        q = q_ref[:, h * RET_DK:(h + 1) * RET_DK]
        k = k_ref[:, h * RET_DK:(h + 1) * RET_DK]
        v = v_ref[:, h * RET_DV:(h + 1) * RET_DV].astype(BF16)
        s = lax.dot_general(q.astype(BF16), k.astype(BF16), (((1,), (1,)), ((), ())),
                            preferred_element_type=F32) * dec
        state = s_ref[h]
        y = jnp.dot(s.astype(BF16), v, preferred_element_type=F32)
        y = y + jnp.dot((q * q_dec).astype(BF16), state.astype(BF16), preferred_element_type=F32)
        upd = lax.dot_general((k * k_dec).astype(BF16), v, (((0,), (0,)), ((), ())), preferred_element_type=F32)
        s_ref[h] = gc_ref[h] * state + upd
        if reverse:
            r = y + yf_ref[:, h * RET_DV:(h + 1) * RET_DV]
            mu = jnp.mean(r, axis=-1, keepdims=True)
            rc = r - mu
            var = jnp.mean(rc * rc, axis=-1, keepdims=True)
            g = gate_ref[:, h * RET_DV:(h + 1) * RET_DV]
            o_ref[:, h * RET_DV:(h + 1) * RET_DV] = (rc * lax.rsqrt(var + EPS) * (g * jax.nn.sigmoid(g))).astype(
                o_ref.dtype)
        else:
            o_ref[:, h * RET_DV:(h + 1) * RET_DV] = y


def retention(qkv, p, log_g, g_chunk, *, seq):
    n_rows = qkv.shape[0]
    n_chunks = n_rows // RET_CHUNK
    n_x = seq // RET_CHUNK
    smem = pl.BlockSpec(memory_space=pltpu.SMEM)

    def run(reverse, extra):
        if reverse:
            idx = lambda t: n_chunks - 1 - t
        else:
            idx = lambda t: (t + n_x) % n_chunks
        in_specs = [smem, smem,
                    pl.BlockSpec((RET_CHUNK, RET_QK), lambda t: (idx(t), 0)),
                    pl.BlockSpec((RET_CHUNK, RET_QK), lambda t: (idx(t), 1)),
                    pl.BlockSpec((RET_CHUNK, RET_V), lambda t: (idx(t), 1))]
        args = [log_g[1 if reverse else 0], g_chunk[1 if reverse else 0], qkv, qkv, qkv]
        if reverse:
            in_specs += [pl.BlockSpec((RET_CHUNK, RET_V), lambda t: (idx(t), 0)),
                         pl.BlockSpec((RET_CHUNK, RET_V), lambda t: (idx(t), 2))]
            args += list(extra)
        return pl.pallas_call(
            functools.partial(_ret_kernel, reverse=reverse),
            grid=(n_chunks,),
            in_specs=in_specs,
            out_specs=pl.BlockSpec((RET_CHUNK, RET_V), lambda t: (idx(t), 0)),
            out_shape=jax.ShapeDtypeStruct((n_rows, RET_V), BF16 if reverse else F32),
            scratch_shapes=[pltpu.VMEM((RET_HEADS, RET_DK, RET_DV), F32)],
            compiler_params=_params(("arbitrary",)),
            name="retention_bwd" if reverse else "retention_fwd",
        )(*args)

    y_fwd = run(False, None)
    return run(True, (y_fwd, p))


def _shortconv_kernel(cur_ref, prev_ref, next_ref, w_ref, b_ref, v_ref, x1_ref, x2_ref, *, x_tiles):
    i = pl.program_id(0)
    cur = cur_ref[...]
    rows = cur.shape[0]
    row = lax.broadcasted_iota(jnp.int32, (rows, 1), 0)
    has_prev = jnp.where((i == 0) | (i == x_tiles), 0.0, 1.0)
    has_next = jnp.where((i == x_tiles - 1) | (i == x_tiles), 0.0, 1.0)
    up = jnp.where(row == 0, prev_ref[SUBLANE - 1:SUBLANE, :] * has_prev, pltpu.roll(cur, 1, 0))
    dn = jnp.where(row == rows - 1, next_ref[0:1, :] * has_next, pltpu.roll(cur, rows - 1, 0))
    y = up * w_ref[0:1, :] + cur * w_ref[1:2, :] + dn * w_ref[2:3, :] + b_ref[...]
    v_ref[...] = y[:, :HY_WIDTH]
    x1_ref[...] = y[:, HY_WIDTH:2 * HY_WIDTH]
    x2_ref[...] = y[:, 2 * HY_WIDTH:]


def shortconv(p, w, b, *, seq):
    n_rows = p.shape[0]
    width = 3 * HY_WIDTH
    col = p.shape[1] // width - 1
    per = ROW_TILE // SUBLANE
    last = n_rows // SUBLANE - 1
    out = jax.ShapeDtypeStruct((n_rows, HY_WIDTH), F32)
    ospec = pl.BlockSpec((ROW_TILE, HY_WIDTH), lambda i: (i, 0))
    return pl.pallas_call(
        functools.partial(_shortconv_kernel, x_tiles=seq // ROW_TILE),
        grid=(n_rows // ROW_TILE,),
        in_specs=[pl.BlockSpec((ROW_TILE, width), lambda i: (i, col)),
                  pl.BlockSpec((SUBLANE, width), lambda i: (jnp.maximum(i * per - 1, 0), col)),
                  pl.BlockSpec((SUBLANE, width), lambda i: (jnp.minimum((i + 1) * per, last), col)),
                  pl.BlockSpec((3, width), lambda i: (0, 0)),
                  pl.BlockSpec((1, width), lambda i: (0, 0))],
        out_specs=[ospec, ospec, ospec],
        out_shape=[out, out, out],
        compiler_params=_params(("parallel",)),
        name="shortconv",
    )(p, p, p, w, b.reshape(1, width))


def _filt_kernel(z_ref, w1_ref, b1_ref, f1_ref, w2_ref, b2_ref, f2_ref, w3a_ref, w3b_ref, dl_ref, *o_ref):
    z = z_ref[...]
    h = jnp.sin(f1_ref[...] * (jnp.dot(z, w1_ref[...], precision=HIGHEST, preferred_element_type=F32) + b1_ref[...]))
    h = jnp.sin(f2_ref[...] * (jnp.dot(h, w2_ref[...], precision=HIGHEST, preferred_element_type=F32) + b2_ref[...]))
    window = jnp.exp(-z[:, 0:1] * dl_ref[...]) * z[:, HY_VALID_COL:HY_VALID_COL + 1]
    for o, w3_ref in enumerate((w3a_ref, w3b_ref)):
        o_ref[o][...] = jnp.dot(h, w3_ref[...], precision=HIGHEST, preferred_element_type=F32) * window


def hyena_filter_taps(length, w1, b1, f1, w2, b2, f2, w3):
    z = _filter_positions(length)
    w1p = jnp.zeros((HY_ZCOLS, HY_FFN), F32).at[:HY_EMB].set(w1)
    deltas = jnp.abs(jnp.linspace(math.log(HY_DECAY_TARGET) / HY_SLOW_DECAY,
                                  math.log(HY_DECAY_TARGET) / HY_FAST_DECAY, HY_WIDTH, dtype=F32)).reshape(1, HY_WIDTH)
    tm = min(length, 512)
    half_tiles = length // tm
    vec = lambda a: a.reshape(1, HY_FFN)
    small = lambda shape: pl.BlockSpec(shape, lambda i: (0, 0))
    w3_spec = lambda o: pl.BlockSpec((HY_FFN, HY_WIDTH), lambda i: (0, 2 * o + jnp.where(i >= half_tiles, 1, 0)))
    assert HY_ORDER == 2
    return pl.pallas_call(
        _filt_kernel,
        grid=(2 * half_tiles,),
        in_specs=[pl.BlockSpec((tm, HY_ZCOLS), lambda i: (i, 0)),
                  small((HY_ZCOLS, HY_FFN)), small((1, HY_FFN)), small((1, HY_FFN)),
                  small((HY_FFN, HY_FFN)), small((1, HY_FFN)), small((1, HY_FFN)),
                  w3_spec(0), w3_spec(1), small((1, HY_WIDTH))],
        out_specs=[pl.BlockSpec((tm, HY_WIDTH), lambda i: (i, 0))] * HY_ORDER,
        out_shape=[jax.ShapeDtypeStruct((2 * length, HY_WIDTH), F32)] * HY_ORDER,
        compiler_params=_params(("parallel",)),
        name="hyena_filter",
    )(z, w1p, vec(b1), vec(f1), w2, vec(b2), vec(f2), w3, w3, deltas)


def _filter_positions(length):
    t = jnp.concatenate([jnp.arange(length, dtype=F32), float(length) - jnp.arange(length, dtype=F32)])
    valid = jnp.ones((2 * length,), F32).at[length].set(0.0)
    t_norm = t / max(length - 1, 1)
    bands = jnp.linspace(1e-4, HY_BANDS - 1, HY_BANDS, dtype=F32)
    ang = (2.0 * math.pi / length) * t[:, None] * bands[None, :]
    z = jnp.concatenate([t_norm[:, None], jnp.cos(ang), -jnp.sin(ang), valid[:, None]], axis=-1)
    return jnp.pad(z, ((0, 0), (0, HY_ZCOLS - z.shape[1])))


def _angles(num, den):
    return (2.0 * math.pi / den) * (num % den).astype(F32)


def dft_tables_two_stage(m):
    n2 = FFT_N2
    n1 = m // n2
    half = n1 // 2
    kp = -(-(half + 1) // SUBLANE) * SUBLANE
    k1 = jnp.arange(kp, dtype=jnp.int32)
    live = (k1 <= half)
    a1 = _angles(k1[:, None] * jnp.arange(n1, dtype=jnp.int32)[None, :], n1)
    f1 = jnp.concatenate([jnp.where(live[:, None], jnp.cos(a1), 0.0), jnp.where(live[:, None], -jnp.sin(a1), 0.0)], 0)
    wgt = jnp.where((k1 == 0) | (k1 == half), 1.0, 2.0) * live / m
    a1h = a1[:, :half].T
    cinv = jnp.concatenate([jnp.cos(a1h) * wgt[None, :], -jnp.sin(a1h) * wgt[None, :]], axis=1)
    k = k1[:, None, None] + n1 * jnp.arange(n2, dtype=jnp.int32)[None, :, None]
    th = _angles(k * jnp.arange(n2, dtype=jnp.int32)[None, None, :], m)
    c = jnp.where(live[:, None, None], jnp.cos(th), 0.0)
    s = jnp.where(live[:, None, None], jnp.sin(th), 0.0)
    g_fwd = jnp.concatenate([jnp.concatenate([c, s], 2), jnp.concatenate([-s, c], 2)], 1)
    ct, st = jnp.swapaxes(c, 1, 2), jnp.swapaxes(s, 1, 2)
    g_inv = jnp.concatenate([jnp.concatenate([ct, -st], 2), jnp.concatenate([st, ct], 2)], 1)
    return dict(n1=n1, kp=kp, f1=f1.astype(BF16), f1_half=f1[:, :half].astype(BF16), cinv=cinv.astype(BF16),
                g_fwd=g_fwd.astype(BF16), g_inv=g_inv.astype(BF16))


def dft_tables_one_stage(m):
    half = m // 2
    kp = -(-(half + 1) // SUBLANE) * SUBLANE
    k = jnp.arange(kp, dtype=jnp.int32)
    live = (k <= half)
    a = _angles(k[:, None] * jnp.arange(m, dtype=jnp.int32)[None, :], m)
    f = jnp.concatenate([jnp.where(live[:, None], jnp.cos(a), 0.0), jnp.where(live[:, None], -jnp.sin(a), 0.0)], 0)
    wgt = jnp.where((k == 0) | (k == half), 1.0, 2.0) * live / m
    ah = a[:, :half].T
    cinv = jnp.concatenate([jnp.cos(ah) * wgt[None, :], -jnp.sin(ah) * wgt[None, :]], axis=1)
    return dict(kp=kp, f=f.astype(BF16), f_half=f[:, :half].astype(BF16), cinv=cinv.astype(BF16))


def _bmm_kernel(*refs, kb, in_part_major, out_part_major, has_h):
    if has_h:
        g_ref, a_ref, h_ref, o_ref = refs
    else:
        g_ref, a_ref, o_ref = refs
    n2 = FFT_N2
    for b in range(kb):
        if in_part_major:
            ar, ai = a_ref[0, b], a_ref[1, b]
        else:
            ar, ai = a_ref[b, 0], a_ref[b, 1]
        if has_h:
            hr, hi = h_ref[b, 0], h_ref[b, 1]
            ar, ai = ar * hr - ai * hi, ar * hi + ai * hr
        xin = jnp.concatenate([ar, ai], axis=0).astype(BF16)
        y = jnp.dot(g_ref[b], xin, preferred_element_type=F32)
        if out_part_major:
            o_ref[0, b] = y[:n2].astype(o_ref.dtype)
            o_ref[1, b] = y[n2:].astype(o_ref.dtype)
        else:
            o_ref[b, 0] = y[:n2].astype(o_ref.dtype)
            o_ref[b, 1] = y[n2:].astype(o_ref.dtype)


def bmm_k1(g, a, h=None, *, in_part_major, out_part_major):
    kp = g.shape[0]
    n2 = FFT_N2
    c = a.shape[-1]
    kb, tc = SUBLANE, min(c, 512)
    pm = lambda: pl.BlockSpec((2, kb, n2, tc), lambda i, j: (0, i, 0, j))
    km = lambda: pl.BlockSpec((kb, 2, n2, tc), lambda i, j: (i, 0, 0, j))
    in_specs = [pl.BlockSpec((kb, 2 * n2, 2 * n2), lambda i, j: (i, 0, 0)), pm() if in_part_major else km()]
    args = [g, a]
    if h is not None:
        in_specs.append(km())
        args.append(h)
    return pl.pallas_call(
        functools.partial(_bmm_kernel, kb=kb, in_part_major=in_part_major, out_part_major=out_part_major,
                          has_h=h is not None),
        grid=(kp // kb, c // tc),
        in_specs=in_specs,
        out_specs=pm() if out_part_major else km(),
        out_shape=jax.ShapeDtypeStruct((2, kp, n2, c), BF16) if out_part_major else
        jax.ShapeDtypeStruct((kp, 2, n2, c), F32),
        compiler_params=_params(("parallel", "parallel")),
        name="dft_inner",
    )(*args)


def _cmul_kernel(x_ref, h_ref, o_ref):
    xr, xi, hr, hi = x_ref[0], x_ref[1], h_ref[0], h_ref[1]
    o_ref[0] = xr * hr - xi * hi
    o_ref[1] = xr * hi + xi * hr


def cmul(x, h):
    spec = pl.BlockSpec(x.shape, lambda i: (0, 0, 0))
    return pl.pallas_call(_cmul_kernel, grid=(1,), in_specs=[spec, spec], out_specs=spec,
                          out_shape=jax.ShapeDtypeStruct(x.shape, F32), compiler_params=_params(("arbitrary",)),
                          name="spectrum_product")(x, h)


def _dft_outer3_kernel(f_ref, x_ref, o_ref):
    c = x_ref.shape[2]
    f = f_ref[...]
    for j in range(SUBLANE):
        o_ref[:, j * c:(j + 1) * c] = jnp.dot(f, x_ref[:, j, :].astype(BF16),
                                              preferred_element_type=F32).astype(o_ref.dtype)


def dft_outer3(f, x3, n_outer):
    rows = f.shape[0]
    c = x3.shape[2]
    return pl.pallas_call(
        _dft_outer3_kernel,
        grid=(FFT_N2 // SUBLANE,),
        in_specs=[pl.BlockSpec((rows, n_outer), lambda j: (0, 0)),
                  pl.BlockSpec((n_outer, SUBLANE, c), lambda j: (0, j, 0))],
        out_specs=pl.BlockSpec((rows, SUBLANE * c), lambda j: (0, j)),
        out_shape=jax.ShapeDtypeStruct((rows, FFT_N2 * c), BF16),
        compiler_params=_params(("parallel",)),
        name="dft_outer",
    )(f, x3)


def _idft_gate3_kernel(cinv_ref, b_ref, gate_ref, skip_ref, u_ref, o_ref, *, u_is_3d):
    c = gate_ref.shape[2]
    cinv = cinv_ref[...]
    for j in range(SUBLANE):
        cols = slice(j * c, (j + 1) * c)
        acc = jnp.dot(cinv, b_ref[:, cols].astype(BF16), preferred_element_type=F32)
        u = u_ref[:, j, :] if u_is_3d else u_ref[:, cols]
        o_ref[:, cols] = gate_ref[:, j, :] * (acc + skip_ref[...] * u)


def idft_gate3(cinv, b2d, gate3, skip_row, u):
    n_outer = cinv.shape[0]
    c = gate3.shape[2]
    u_is_3d = u.ndim == 3
    wide = pl.BlockSpec((n_outer, SUBLANE * c), lambda j: (0, j))
    slab = pl.BlockSpec((n_outer, SUBLANE, c), lambda j: (0, j, 0))
    return pl.pallas_call(
        functools.partial(_idft_gate3_kernel, u_is_3d=u_is_3d),
        grid=(FFT_N2 // SUBLANE,),
        in_specs=[pl.BlockSpec(cinv.shape, lambda j: (0, 0)),
                  pl.BlockSpec((b2d.shape[0], SUBLANE * c), lambda j: (0, j)),
                  slab, pl.BlockSpec((1, c), lambda j: (0, 0)), slab if u_is_3d else wide],
        out_specs=wide,
        out_shape=jax.ShapeDtypeStruct((n_outer, FFT_N2 * c), F32),
        compiler_params=_params(("parallel",)),
        name="idft_outer_gate",
    )(cinv, b2d, gate3, skip_row, u)


def long_conv_two_stage(tabs, taps, v, x1, x2, skip, length):
    c = v.shape[1]
    n2, n1, kp = FFT_N2, tabs["n1"], tabs["kp"]
    as3 = lambda a: a.reshape(a.shape[0] // n2, n2, c)
    spectrum = lambda a2d: bmm_k1(tabs["g_fwd"], a2d.reshape(2, kp, n2, c), in_part_major=True, out_part_major=False)

    spectra = [spectrum(dft_outer3(tabs["f1"], as3(taps[o]), n1)) for o in range(HY_ORDER)]
    v3 = as3(v)
    u = v3
    for o, gate in enumerate((x1, x2)):
        if u.ndim == 3:
            a = dft_outer3(tabs["f1_half"], u, n1 // 2)
        else:
            a = mm([(tabs["f1_half"], u)], BF16, 2 * kp, 2048, name="dft_outer")
        bt = bmm_k1(tabs["g_inv"], spectrum(a), spectra[o], in_part_major=False, out_part_major=True)
        u = idft_gate3(tabs["cinv"], bt.reshape(2 * kp, n2 * c), as3(gate), skip[o].reshape(1, c), u)
    return u.reshape(length, c)


def long_conv_one_stage(tabs, taps, v, x1, x2, skip):
    length, c = v.shape
    kp = tabs["kp"]
    u = v
    for o, gate in enumerate((x1, x2)):
        hs = mm([(tabs["f"], taps[o])], F32, 2 * kp, c, name="ctx_dft").reshape(2, kp, c)
        xs = mm([(tabs["f_half"], u)], F32, 2 * kp, c, name="ctx_dft").reshape(2, kp, c)
        ys = cmul(xs, hs).reshape(2 * kp, c)
        u = mm([(tabs["cinv"], ys)], F32, length, c, epi=(gate, skip[o].reshape(1, c), u), name="ctx_idft_gate")
    return u


def _flash_kernel(lam_ref, qt_ref, k_ref, vt_ref, sub_ref, o_ref, m_ref, excess_ref, acc_ref, *, kv, seq, ctx_len,
                  out_scale):
    i = pl.program_id(1)
    last_q = pl.num_programs(1) - 1
    tq = qt_ref.shape[1]
    d = DA_HEAD_DIM
    dv = 2 * DA_HEAD_DIM
    n_chunks = k_ref.shape[0] // kv
    acc_ref[...] = jnp.zeros_like(acc_ref)

    def scores(off, rows, c, masked):
        s = jnp.dot(k_ref[pl.ds(off, rows), c * d:(c + 1) * d], qt_ref[c * d:(c + 1) * d, :],
                    preferred_element_type=F32)
        if masked:
            key = off + lax.broadcasted_iota(jnp.int32, (rows, 1), 0)
            lane = lax.broadcasted_iota(jnp.int32, (1, tq), 1)
            s = s + jnp.where(key < seq, NEG_BIG, 0.0) * jnp.where(lane >= tq - ctx_len, 1.0, 0.0)
        return s

    def exact_step(off, c, masked):
        s = scores(off, kv, c, masked)
        m_old = m_ref[c]
        m_new = jnp.maximum(m_old, jnp.max(s, axis=0, keepdims=True))
        pr = jnp.exp2(s - m_new).astype(BF16)
        acc_ref[c] = jnp.exp2(m_old - m_new) * acc_ref[c] + jnp.dot(vt_ref[:, pl.ds(off, kv)], pr,
                                                                   preferred_element_type=F32)
        m_ref[c] = m_new

    def lazy_step(off, c, masked):
        s = scores(off, kv, c, masked)
        m_old = m_ref[c]
        m_chunk = jnp.max(s, axis=0, keepdims=True)
        pv = jnp.dot(vt_ref[:, pl.ds(off, kv)], jnp.exp2(s - m_old).astype(BF16), preferred_element_type=F32)
        m_new = jnp.maximum(m_old, m_chunk)
        acc_ref[c] = jnp.exp2(m_old - m_new) * (acc_ref[c] + pv)
        m_ref[c] = m_new
        excess_ref[c] = jnp.maximum(excess_ref[c], m_chunk - m_old)

    def all_chunks(step, masked):
        def body(kc, carry):
            off = pl.multiple_of(kc * kv, kv)
            for c in range(2):
                step(off, c, masked)
            return carry

        lax.fori_loop(0, n_chunks, body, 0)

    def run(masked):
        for c in range(2):
            m0 = jnp.max(scores(0, FLASH_INIT_KEYS, c, False), axis=0, keepdims=True)
            if masked:
                lane = lax.broadcasted_iota(jnp.int32, (1, tq), 1)
                m_ctx = jnp.max(scores(seq, FLASH_INIT_KEYS, c, False), axis=0, keepdims=True)
                m0 = jnp.where(lane >= tq - ctx_len, m_ctx, m0)
            m_ref[c] = m0
        excess_ref[...] = jnp.full_like(excess_ref, NEG_BIG)
        all_chunks(lazy_step, masked)

        @pl.when(jnp.max(excess_ref[...]) > FLASH_LAZY_HEADROOM)
        def _():
            m_ref[...] = jnp.full_like(m_ref, NEG_BIG)
            acc_ref[...] = jnp.zeros_like(acc_ref)
            all_chunks(exact_step, masked)

    @pl.when(i != last_q)
    def _():
        run(False)

    @pl.when(i == last_q)
    def _():
        run(True)

    a0 = acc_ref[0, :dv, :] / acc_ref[0, dv:dv + 1, :]
    a1 = acc_ref[1, :dv, :] / acc_ref[1, dv:dv + 1, :]
    o = (a0 - lam_ref[0] * a1).T
    o = o * lax.rsqrt(jnp.mean(o * o, axis=-1, keepdims=True) + 1e-5) * sub_ref[...]
    o_ref[...] = (o * out_scale).astype(o_ref.dtype)


def _rope_da_kernel(p_ref, cos_ref, sin_ref, qt_ref, k_ref, vt_ref):
    cos = cos_ref[...]
    sin = sin_ref[...]
    hw = 2 * DA_HEAD_DIM
    lane = lax.broadcasted_iota(jnp.int32, cos.shape, 1)
    first_half = (lane % DA_HEAD_DIM) < DA_HEAD_DIM // 2

    def rotated(b):
        x = p_ref[:, b * LANE:(b + 1) * LANE]
        rot = jnp.where(first_half, pltpu.roll(x, LANE - DA_HEAD_DIM // 2, 1), pltpu.roll(x, DA_HEAD_DIM // 2, 1))
        return x * cos + rot * sin

    ones = jnp.ones((FLASH_ONES_ROWS, cos.shape[0]), BF16)
    for h in range(DA_HEADS):
        qt_ref[h * hw:(h + 1) * hw, :] = (rotated(h) * (LOG2_E * DA_HEAD_DIM ** -0.5)).T.astype(BF16)
        k_ref[:, h * hw:(h + 1) * hw] = rotated(DA_HEADS + h).astype(BF16)
        base = h * (hw + FLASH_ONES_ROWS)
        vt_ref[base:base + hw, :] = p_ref[:, (2 * DA_HEADS + h) * LANE:(2 * DA_HEADS + h + 1) * LANE].T.astype(BF16)
        vt_ref[base + hw:base + hw + FLASH_ONES_ROWS, :] = ones


def rope_da(p, cos, sin):
    n_rows = p.shape[0]
    assert 2 * DA_HEAD_DIM == LANE
    vt_rows = DA_HEADS * (LANE + FLASH_ONES_ROWS)
    return pl.pallas_call(
        _rope_da_kernel,
        grid=(n_rows // ROW_TILE,),
        in_specs=[pl.BlockSpec((ROW_TILE, 3 * DA_WIDTH), lambda i: (i, 0)),
                  pl.BlockSpec((ROW_TILE, LANE), lambda i: (i, 0)),
                  pl.BlockSpec((ROW_TILE, LANE), lambda i: (i, 0))],
        out_specs=[pl.BlockSpec((DA_WIDTH, ROW_TILE), lambda i: (0, i)),
                   pl.BlockSpec((ROW_TILE, DA_WIDTH), lambda i: (i, 0)),
                   pl.BlockSpec((vt_rows, ROW_TILE), lambda i: (0, i))],
        out_shape=[jax.ShapeDtypeStruct((DA_WIDTH, n_rows), BF16),
                   jax.ShapeDtypeStruct((n_rows, DA_WIDTH), BF16),
                   jax.ShapeDtypeStruct((vt_rows, n_rows), BF16)],
        compiler_params=_params(("parallel",)),
        name="rope_da",
    )(p, cos, sin)


def diff_attention(qt, k, vt, lam_full, subln, *, seq, ctx_len, lambda_init):
    n_rows = k.shape[0]
    tq = _token_tile(n_rows)
    hw = 2 * DA_HEAD_DIM
    ones_rows = FLASH_ONES_ROWS
    return pl.pallas_call(
        functools.partial(_flash_kernel, kv=tq, seq=seq, ctx_len=ctx_len, out_scale=1.0 - lambda_init),
        grid=(DA_HEADS, n_rows // tq),
        in_specs=[pl.BlockSpec(memory_space=pltpu.SMEM),
                  pl.BlockSpec((hw, tq), lambda h, i: (h, i)),
                  pl.BlockSpec((n_rows, hw), lambda h, i: (0, h)),
                  pl.BlockSpec((hw + ones_rows, n_rows), lambda h, i: (h, 0)),
                  pl.BlockSpec((1, hw), lambda h, i: (0, 0))],
        out_specs=pl.BlockSpec((tq, hw), lambda h, i: (i, h)),
        out_shape=jax.ShapeDtypeStruct((n_rows, DA_WIDTH), BF16),
        scratch_shapes=[pltpu.VMEM((2, 1, tq), F32), pltpu.VMEM((2, 1, tq), F32),
                        pltpu.VMEM((2, hw + ones_rows, tq), F32)],
        compiler_params=pltpu.CompilerParams(dimension_semantics=("parallel", "parallel"),
                                             vmem_limit_bytes=FLASH_VMEM_LIMIT),
        name="diff_attention",
    )(lam_full.reshape(1), qt, k, vt, subln.reshape(1, hw))


def _route_kernel(lg_ref, b_ref, tri_ref, eidx_ref, w_ref, rank_ref, cnt_ref, carry_ref):
    t = lg_ref.shape[1]

    @pl.when(pl.program_id(0) == 0)
    def _():
        carry_ref[...] = jnp.zeros_like(carry_ref)

    scores = jax.nn.sigmoid(lg_ref[...])
    choice = (scores + b_ref[...]).reshape(N_GROUPS, GROUP_SIZE, t)
    s3 = scores.reshape(N_GROUPS, GROUP_SIZE, t)
    member = lax.broadcasted_iota(jnp.int32, choice.shape, 1)
    group = lax.broadcasted_iota(jnp.int32, (N_GROUPS, 1, t), 0)
    expert = lax.broadcasted_iota(jnp.int32, choice.shape, 0) * GROUP_SIZE + member
    neg_inf = -jnp.inf
    m1 = jnp.max(choice, axis=1, keepdims=True)
    first = jnp.min(jnp.where(choice == m1, member, GROUP_SIZE), axis=1, keepdims=True)
    m2 = jnp.max(jnp.where(member == first, neg_inf, choice), axis=1, keepdims=True)
    gscore = m1 + m2
    gsel = jnp.zeros(gscore.shape, F32)
    for _ in range(TOPK_GROUPS):
        m = jnp.max(gscore, axis=0, keepdims=True)
        f = jnp.min(jnp.where(gscore == m, group, N_GROUPS), axis=0, keepdims=True)
        hit = group == f
        gsel = jnp.where(hit, 1.0, gsel)
        gscore = jnp.where(hit, neg_inf, gscore)
    cand = jnp.where(gsel > 0.0, choice, neg_inf)
    esel = jnp.zeros(choice.shape, F32)
    picks = []
    for _ in range(TOP_K):
        m = jnp.max(jnp.max(cand, axis=1, keepdims=True), axis=0, keepdims=True)
        f = jnp.min(jnp.min(jnp.where(cand == m, expert, N_EXPERTS), axis=1, keepdims=True), axis=0, keepdims=True)
        hit = expert == f
        esel = jnp.where(hit, 1.0, esel)
        cand = jnp.where(hit, neg_inf, cand)
        picks.append(f)
    w = s3 * esel
    denom = jnp.sum(jnp.sum(w, axis=1, keepdims=True), axis=0, keepdims=True) + 1e-20
    w = w / denom * ROUTED_SCALE
    sel = esel.reshape(N_EXPERTS, t)
    before = jnp.dot(sel.astype(BF16), tri_ref[...], preferred_element_type=F32) + carry_ref[...]
    before = before.reshape(N_GROUPS, GROUP_SIZE, t)
    pick = lambda a, hit: jnp.sum(jnp.sum(jnp.where(hit, a, 0.0), axis=1, keepdims=True), axis=0).reshape(1, t)
    for k, f in enumerate(picks):
        hit = expert == f
        eidx_ref[k:k + 1, :] = f.reshape(1, t)
        w_ref[k:k + 1, :] = pick(w, hit)
        rank_ref[k:k + 1, :] = pick(before, hit).astype(jnp.int32)
    carry_ref[...] += jnp.sum(sel, axis=1, keepdims=True)
    cnt_ref[...] = carry_ref[...]


def route(logits_t, bias, lo, hi):
    t = _token_tile(logits_t.shape[1])
    n = hi - lo
    tile0 = lo // t
    tri = (jnp.arange(t)[:, None] < jnp.arange(t)[None, :]).astype(BF16)
    tok = lambda dt: jax.ShapeDtypeStruct((TOP_K, n), dt)
    tok_spec = pl.BlockSpec((TOP_K, t), lambda i: (0, i))
    return pl.pallas_call(
        _route_kernel,
        grid=(n // t,),
        in_specs=[pl.BlockSpec((N_EXPERTS, t), lambda i: (0, tile0 + i)),
                  pl.BlockSpec((N_EXPERTS, 1), lambda i: (0, 0)),
                  pl.BlockSpec((t, t), lambda i: (0, 0))],
        out_specs=[tok_spec, tok_spec, tok_spec, pl.BlockSpec((N_EXPERTS, 1), lambda i: (0, 0))],
        out_shape=[tok(jnp.int32), tok(F32), tok(jnp.int32), jax.ShapeDtypeStruct((N_EXPERTS, 1), F32)],
        scratch_shapes=[pltpu.VMEM((N_EXPERTS, 1), F32)],
        compiler_params=_params(("arbitrary",)),
        name="route",
    )(logits_t, bias.reshape(N_EXPERTS, 1), tri)


def _slot_kernel(start_ref, eidx_ref, rank_ref, dest_ref):
    e = eidx_ref[...]
    d = rank_ref[...]
    for x in range(N_EXPERTS):
        d = d + jnp.where(e == x, start_ref[x], 0)
    dest_ref[...] = d


def slot_index(pad_start, eidx, rank):
    n = eidx.shape[1]
    t = _token_tile(n)
    spec = pl.BlockSpec((TOP_K, t), lambda i: (0, i))
    return pl.pallas_call(
        _slot_kernel,
        grid=(n // t,),
        in_specs=[pl.BlockSpec(memory_space=pltpu.SMEM), spec, spec],
        out_specs=spec,
        out_shape=jax.ShapeDtypeStruct((TOP_K, n), jnp.int32),
        compiler_params=_params(("parallel",)),
        name="slot_index",
    )(pad_start, eidx, rank)


def _sc_worker():
    return lax.axis_index("s") * SC_CORES + lax.axis_index("c")


def sc_dispatch(h, dest3, n_slots, row0):
    d = h.shape[1]
    n_win = dest3.shape[0]
    mesh = plsc.VectorSubcoreMesh(core_axis_name="c", subcore_axis_name="s")

    @functools.partial(
        pl.kernel, mesh=mesh, out_type=jax.ShapeDtypeStruct((n_slots, d), h.dtype),
        scratch_types=[pltpu.VMEM((TOP_K, SC_WINDOW), jnp.int32), pltpu.VMEM((SC_WINDOW, d), h.dtype),
                       pltpu.SemaphoreType.DMA])
    def k(h_hbm, dest_hbm, out_hbm, idx_v, rows_v, sem):
        wid = _sc_worker()

        @pl.loop(0, -(-n_win // SC_WORKERS))
        def _(it):
            w = it * SC_WORKERS + wid

            @pl.when(w < n_win)
            def _():
                pltpu.sync_copy(dest_hbm.at[w], idx_v)
                pltpu.sync_copy(h_hbm.at[pl.ds(row0 + w * SC_WINDOW, SC_WINDOW)], rows_v)
                copies = [pltpu.async_copy(rows_v, out_hbm.at[idx_v.at[j]], sem) for j in range(TOP_K)]
                for c in copies:
                    c.wait()

    return k(h, dest3)


def sc_combine_gather(y, dest3):
    d = y.shape[1]
    n_win = dest3.shape[0]
    n = n_win * SC_WINDOW
    mesh = plsc.VectorSubcoreMesh(core_axis_name="c", subcore_axis_name="s")

    @functools.partial(
        pl.kernel, mesh=mesh, out_type=jax.ShapeDtypeStruct((TOP_K, n, d), y.dtype),
        scratch_types=[pltpu.VMEM((TOP_K, SC_WINDOW), jnp.int32), pltpu.VMEM((2, SC_WINDOW, d), y.dtype),
                       pltpu.SemaphoreType.DMA, pltpu.SemaphoreType.DMA,
                       pltpu.SemaphoreType.DMA, pltpu.SemaphoreType.DMA])
    def k(y_hbm, dest_hbm, out_hbm, idx_v, rows_v, gsem0, gsem1, osem0, osem1):
        wid = _sc_worker()
        gsem, osem = (gsem0, gsem1), (osem0, osem1)

        @pl.loop(0, -(-n_win // SC_WORKERS))
        def _(it):
            w = it * SC_WORKERS + wid

            @pl.when(w < n_win)
            def _():
                pltpu.sync_copy(dest_hbm.at[w], idx_v)
                gather = lambda j: pltpu.async_copy(y_hbm.at[idx_v.at[j]], rows_v.at[j % 2], gsem[j % 2])
                g = [None] * TOP_K
                o = [None] * TOP_K
                g[0] = gather(0)
                for j in range(TOP_K):
                    if j + 1 < TOP_K:
                        if j >= 1:
                            o[j - 1].wait()
                        g[j + 1] = gather(j + 1)
                    g[j].wait()
                    o[j] = pltpu.async_copy(rows_v.at[j % 2], out_hbm.at[j, pl.ds(w * SC_WINDOW, SC_WINDOW)],
                                            osem[j % 2])
                o[TOP_K - 2].wait()
                o[TOP_K - 1].wait()

    return k(y, dest3)


def _expert_ffn_kernel(be_ref, bv_ref, x_ref, wg_ref, wu_ref, wd_ref, o_ref, wg_s, wu_s, wd_s):
    b = pl.program_id(0)
    valid = bv_ref[b]
    new_expert = (b == 0) | (be_ref[b] != be_ref[jnp.maximum(b - 1, 0)])

    @pl.when(new_expert)
    def _():
        wg_s[...] = wg_ref[0, 0].astype(BF16)
        wu_s[...] = wu_ref[0, 0].astype(BF16)
        wd_s[...] = wd_ref[0, 0].astype(BF16)

    sub = x_ref.shape[0] // MOE_SUB_BLOCKS

    def sub_block(r):
        row = lax.broadcasted_iota(jnp.int32, (sub, 1), 0) + r * sub
        rows = pl.ds(r * sub, sub)
        lo, hi = _unpack_bf16_pair(jnp.where(row < valid, x_ref[rows, :], 0))
        x = jnp.concatenate([lo.astype(BF16), hi.astype(BF16)], axis=1)
        a = jnp.dot(x, wg_s[...], preferred_element_type=F32)
        a = a * jax.nn.sigmoid(a) * jnp.dot(x, wu_s[...], preferred_element_type=F32)
        y = jnp.dot(a.astype(BF16), wd_s[...], preferred_element_type=F32)
        half = y.shape[1] // 2
        o_ref[rows, :] = _pack_bf16_pair(y[:, :half], y[:, half:])

    for live in range(1, MOE_SUB_BLOCKS + 1):
        upper = valid <= live * sub if live < MOE_SUB_BLOCKS else True

        @pl.when((valid > (live - 1) * sub) & upper)
        def _():
            for r in range(live):
                sub_block(r)


def expert_ffn(xg, block_expert, block_valid, wg, wu, wd, layer):
    n_slots, dp = xg.shape
    d, f = wg.shape[-2:]
    grid_spec = pltpu.PrefetchScalarGridSpec(
        num_scalar_prefetch=2,
        grid=(n_slots // MOE_BLOCK,),
        in_specs=[pl.BlockSpec((MOE_BLOCK, dp), lambda b, be, bv: (b, 0)),
                  pl.BlockSpec((1, 1, d, f), lambda b, be, bv: (layer, be[b], 0, 0)),
                  pl.BlockSpec((1, 1, d, f), lambda b, be, bv: (layer, be[b], 0, 0)),
                  pl.BlockSpec((1, 1, f, d), lambda b, be, bv: (layer, be[b], 0, 0))],
        out_specs=pl.BlockSpec((MOE_BLOCK, dp), lambda b, be, bv: (b, 0)),
        scratch_shapes=[pltpu.VMEM((d, f), BF16), pltpu.VMEM((d, f), BF16), pltpu.VMEM((f, d), BF16)],
    )
    return pl.pallas_call(
        _expert_ffn_kernel,
        grid_spec=grid_spec,
        out_shape=jax.ShapeDtypeStruct((n_slots, dp), jnp.int32),
        compiler_params=_params(("arbitrary",)),
        name="expert_ffn",
    )(block_expert, block_valid, xg, wg, wu, wd)


def _combine_kernel(yg_ref, w_ref, h_ref, swg_ref, swu_ref, swd_ref, xs_ref, mods_ref, *rest, tile0, gate_idx, seq):
    o_ref = rest[-1]
    h = h_ref[...]
    a = jnp.dot(h, swg_ref[...], preferred_element_type=F32)
    a = a * jax.nn.sigmoid(a) * jnp.dot(h, swu_ref[...], preferred_element_type=F32)
    acc = jnp.dot(a.astype(BF16), swd_ref[...], preferred_element_type=F32)
    half = acc.shape[1] // 2
    acc_lo, acc_hi = acc[:, :half], acc[:, half:]
    wt = w_ref[...].T
    for k in range(TOP_K):
        lo, hi = _unpack_bf16_pair(yg_ref[k])
        acc_lo = acc_lo + wt[:, k:k + 1] * lo
        acc_hi = acc_hi + wt[:, k:k + 1] * hi
    tm = h.shape[0]
    gate = _row_mod(mods_ref, gate_idx, (tile0 + pl.program_id(0)) * tm, tm, seq)
    o_ref[:, :half] = xs_ref[:, :half] + gate[:, :half] * acc_lo
    o_ref[:, half:] = xs_ref[:, half:] + gate[:, half:] * acc_hi


def combine(yg, w, h, swg, swu, swd, residual, lo, prev):
    xs, mods, gate_idx, seq = residual
    n_all, d = h.shape
    n = w.shape[1]
    f = swg.shape[-1]
    tm = ROW_TILE
    tile0 = lo // tm
    in_specs = [pl.BlockSpec((TOP_K, tm, d // 2), lambda i: (0, i, 0)),
                pl.BlockSpec((TOP_K, tm), lambda i: (0, i)),
                pl.BlockSpec((tm, d), lambda i: (tile0 + i, 0)),
                pl.BlockSpec((d, f), lambda i: (0, 0)),
                pl.BlockSpec((d, f), lambda i: (0, 0)),
                pl.BlockSpec((f, d), lambda i: (0, 0)),
                pl.BlockSpec((tm, d), lambda i: (tile0 + i, 0)),
                pl.BlockSpec((2, 6, d), lambda i: (0, 0, 0))]
    args = [yg, w, h, swg, swu, swd, xs, mods]
    aliases = {}
    if prev is not None:
        in_specs.append(pl.BlockSpec(memory_space=pl.ANY))
        args.append(prev)
        aliases = {len(args) - 1: 0}
    return pl.pallas_call(
        functools.partial(_combine_kernel, tile0=tile0, gate_idx=gate_idx, seq=seq),
        grid=(n // tm,),
        in_specs=in_specs,
        out_specs=pl.BlockSpec((tm, d), lambda i: (tile0 + i, 0)),
        out_shape=jax.ShapeDtypeStruct((n_all, d), F32),
        input_output_aliases=aliases,
        compiler_params=_params(("parallel",)),
        name="moe_combine",
    )(*args)


def moe(h, h_packed, logits_t, bias, wg, wu, wd, layer, swg, swu, swd, residual):
    n = h.shape[0]
    t = _token_tile(n)
    cut = (n // t + 1) // 2 * t
    shared = (swg.astype(BF16), swu.astype(BF16), swd.astype(BF16))
    staged = [_moe_experts(h_packed, logits_t, bias, wg, wu, wd, layer, lo, hi) for lo, hi in ((0, cut), (cut, n))]
    out = None
    for (yg, w), lo in zip(staged, (0, cut)):
        out = combine(yg, w, h, *shared, residual, lo, out)
    return out


def _moe_experts(h_packed, logits_t, bias, wg, wu, wd, layer, lo, hi):
    n = hi - lo
    eidx, w, rank, counts = route(logits_t, bias, lo, hi)
    counts = counts.reshape(N_EXPERTS).astype(jnp.int32)
    padded = (counts + MOE_BLOCK - 1) // MOE_BLOCK * MOE_BLOCK
    pad_end = jnp.cumsum(padded)
    pad_start = pad_end - padded
    n_slots = n * TOP_K + N_EXPERTS * MOE_BLOCK
    starts = jnp.arange(n_slots // MOE_BLOCK, dtype=jnp.int32) * MOE_BLOCK
    owner = jnp.sum((pad_end[None, :] <= starts[:, None]).astype(jnp.int32), axis=1)
    block_expert = jnp.minimum(owner, N_EXPERTS - 1)
    member = (block_expert[:, None] == jnp.arange(N_EXPERTS, dtype=jnp.int32)[None, :]).astype(jnp.int32)
    left = jnp.sum(member * (counts + pad_start)[None, :], axis=1) - starts
    block_valid = jnp.clip(left, 0, MOE_BLOCK).astype(jnp.int32)
    dest = slot_index(pad_start.astype(jnp.int32), eidx, rank)
    dest3 = dest.reshape(TOP_K, n // SC_WINDOW, SC_WINDOW).transpose(1, 0, 2)
    xg = sc_dispatch(h_packed, dest3, n_slots, lo)
    y = expert_ffn(xg, block_expert, block_valid, wg, wu, wd, layer)
    return sc_combine_gather(y, dest3), w


def mixer_ab(h, w_in, w_out, decay_logit, conv_w, conv_b, w1, b1, f1, w2, b2, f2, w3, skip, rope, dft, residual, *,
             seq, ctx_len):
    n_rows = h.shape[0]
    tm = _token_tile(n_rows)
    p = mm([(h, w_in.astype(BF16))], F32, tm, 512, name="ab_in_proj")
    qkv_w = 2 * RET_QK + RET_V
    n_qk = 2 * RET_QK // LANE
    scales = (1.0,) * (RET_QK // LANE) + (RET_DK ** -0.5,) * (RET_QK // LANE) + (1.0,) * (RET_V // LANE)
    qkv = rope_cast(p, rope[0], rope[1], width=qkv_w, n_rot_blocks=n_qk, head_dim=RET_DK, scales=scales,
                    out_dtype=F32)
    log_g = jax.nn.log_sigmoid(decay_logit.astype(F32))
    ret = retention(qkv, p, log_g, jnp.exp(RET_CHUNK * log_g), seq=seq)
    v, x1, x2 = shortconv(p, conv_w, conv_b, seq=seq)
    filt = (w1, b1, f1, w2, b2, f2, w3)
    hy_x = long_conv_two_stage(dft["x"], hyena_filter_taps(seq, *filt), v, x1, x2, skip, seq)
    hy_c = long_conv_one_stage(dft["c"], hyena_filter_taps(ctx_len, *filt), v[seq:], x1[seq:], x2[seq:], skip)
    hy = jnp.concatenate([hy_x, hy_c], axis=0)
    w_out = w_out.astype(BF16)
    return mm([(ret, w_out[:RET_V]), (hy, w_out[RET_V:])], F32, tm, 512, residual=residual, name="ab_out_proj")


def mixer_da(h, w_in, w_out, lam, subln, lambda_init, rope, residual, *, seq, ctx_len):
    n_rows = h.shape[0]
    tm = _token_tile(n_rows)
    p = mm([(h, w_in.astype(BF16))], F32, tm, 512, name="da_in_proj")
    qt, k, vt = rope_da(p, rope[0], rope[1])
    lam_f = lam.astype(F32)
    lam_full = jnp.exp(jnp.sum(lam_f[0] * lam_f[1])) - jnp.exp(jnp.sum(lam_f[2] * lam_f[3])) + lambda_init
    o = diff_attention(qt, k, vt, lam_full, subln, seq=seq, ctx_len=ctx_len, lambda_init=lambda_init)
    return mm([(o, w_out.astype(BF16))], F32, tm, 512, residual=residual, name="da_out_proj")


def kernel(x, c, ctx, c_ctx, w_ada, b_ada, norm_mix, norm_ffn, ab_w_in, ab_w_out, ret_decay_logit, hy_conv_w, hy_conv_b, hy_w1, hy_b1, hy_freq1, hy_w2, hy_b2, hy_freq2, hy_w3, hy_skip, da_w_in, da_w_out, da_lambda, da_subln, router_w, router_b, exp_w_gate, exp_w_up, exp_w_down, sh_w_gate, sh_w_up, sh_w_down, norm_final):
    batch, seq, d = x.shape
    ctx_len = ctx.shape[1]
    assert batch == 1 and seq % ROW_TILE == 0 and ctx_len == ROW_TILE
    depth = w_ada.shape[0]
    n_rows = seq + ctx_len

    xs = jnp.concatenate([x[0], ctx[0]], axis=0)
    cv = jnp.zeros((SUBLANE, d), F32).at[0].set(c_ctx).at[1].set(c[0])
    mods = adaln(cv, w_ada, b_ada)[:, :2].reshape(depth, 2, 6, d)

    rope_ret = rope_tables(seq, ctx_len, RET_DK)
    rope_da = rope_tables(seq, ctx_len, DA_HEAD_DIM)
    dft = dict(x=dft_tables_two_stage(2 * seq), c=dft_tables_one_stage(2 * ctx_len))
    common = dict(n_rows=n_rows, seq=seq)

    for i in range(depth):
        j = i // 2
        (h,) = norm_mod(xs, norm_mix[i], mods=mods[i], shift_idx=0, scale_idx=1, **common)
        residual = (xs, mods[i], 2, seq)
        if i % 2 == 0:
            xs = mixer_ab(h, ab_w_in[j], ab_w_out[j], ret_decay_logit[j], hy_conv_w[j], hy_conv_b[j], hy_w1[j],
                          hy_b1[j], hy_freq1[j], hy_w2[j], hy_b2[j], hy_freq2[j], hy_w3[j], hy_skip[j], rope_ret, dft,
                          residual, seq=seq, ctx_len=ctx_len)
        else:
            lambda_init = 0.8 - 0.6 * math.exp(-0.3 * i)
            xs = mixer_da(h, da_w_in[j], da_w_out[j], da_lambda[j], da_subln[j], lambda_init, rope_da, residual,
                          seq=seq, ctx_len=ctx_len)
        h, logits_t, h_packed = norm_mod(xs, norm_ffn[i], mods=mods[i], shift_idx=3, scale_idx=4,
                                         router_wt=router_w[i].T, **common)
        xs = moe(h, h_packed, logits_t, router_b[i], exp_w_gate, exp_w_up, exp_w_down, i,
                 sh_w_gate[i], sh_w_up[i], sh_w_down[i], (xs, mods[i], 5, seq))
    (out,) = norm_mod(xs, norm_final, n_rows=seq, seq=seq, out_dtype=F32)
    return out[None]
```

```python
import functools
import math

import jax
import jax.numpy as jnp
from jax import lax
from jax.experimental import pallas as pl
from jax.experimental.pallas import tpu as pltpu
from jax.experimental.pallas import tpu_sc as plsc

F32 = jnp.float32
BF16 = jnp.bfloat16
HIGHEST = lax.Precision.HIGHEST

D_MODEL = 1024
DEPTH = 4
GRID_W = 64
EPS = 1e-6
ROPE_BASE = 10000.0

RET_HEADS = 4
RET_DK = 128
RET_DV = 256
RET_CHUNK = 128
RET_QK = RET_HEADS * RET_DK
RET_V = RET_HEADS * RET_DV

HY_WIDTH = 512
HY_ORDER = 2
HY_BANDS = 16
HY_EMB = 2 * HY_BANDS + 1
HY_FFN = 64
HY_DECAY_TARGET = 1e-2
HY_FAST_DECAY = 0.3
HY_SLOW_DECAY = 1.5
HY_ZCOLS = 64
HY_VALID_COL = HY_EMB
FFT_N2 = 128

AB_IN = 2 * RET_QK + 2 * RET_V + (HY_ORDER + 1) * HY_WIDTH
AB_CAT = RET_V + HY_WIDTH

DA_HEADS = 8
DA_HEAD_DIM = 64
DA_WIDTH = DA_HEADS * 2 * DA_HEAD_DIM

N_EXPERTS = 64
TOP_K = 8
N_GROUPS = 8
TOPK_GROUPS = 4
GROUP_SIZE = N_EXPERTS // N_GROUPS
EXPERT_DIM = 256
ROUTED_SCALE = 2.5
MOE_BLOCK = 512
MOE_SUB_BLOCKS = 2
SC_CORES = 2
SC_SUBCORES = 16
SC_WORKERS = SC_CORES * SC_SUBCORES
SC_WINDOW = 64

LANE = 128
SUBLANE = 8
ROW_TILE = 256
MAX_TOKEN_TILE = 1280
VMEM_LIMIT = 48 * 1024 * 1024
FLASH_VMEM_LIMIT = 56 * 1024 * 1024
NEG_BIG = -1e30
LOG2_E = 1.4426950408889634
FLASH_ONES_ROWS = 16
FLASH_INIT_KEYS = 16
FLASH_LAZY_HEADROOM = 60.0


def _params(sem):
    return pltpu.CompilerParams(dimension_semantics=sem, vmem_limit_bytes=VMEM_LIMIT)


def _token_tile(n):
    best = ROW_TILE
    t = ROW_TILE
    while t <= min(n, MAX_TOKEN_TILE):
        if n % t == 0:
            best = t
        t += ROW_TILE
    return best


def _row_mod(mods_ref, idx, row0, n, seq):
    row = row0 + lax.broadcasted_iota(jnp.int32, (n, 1), 0)
    return jnp.where(row >= seq, mods_ref[0, idx:idx + 1, :], mods_ref[1, idx:idx + 1, :])


def _mm_kernel(*refs, n_pairs, has_epi, gate_idx, seq):
    acc = None
    for p in range(n_pairs):
        a = refs[2 * p][...].astype(BF16)
        b = refs[2 * p + 1][...].astype(BF16)
        d = jnp.dot(a, b, preferred_element_type=F32)
        acc = d if acc is None else acc + d
    idx = 2 * n_pairs
    if has_epi:
        acc = refs[idx][...] * (acc + refs[idx + 1][...] * refs[idx + 2][...])
        idx += 3
    if gate_idx is not None:
        tm = acc.shape[0]
        acc = refs[idx][...] + _row_mod(refs[idx + 1], gate_idx, pl.program_id(0) * tm, tm, seq) * acc
        idx += 2
    o_ref = refs[idx]
    o_ref[...] = acc.astype(o_ref.dtype)


def mm(pairs, out_dtype, tm, tn, epi=None, residual=None, name="mm"):
    m = pairs[0][0].shape[0]
    n = pairs[0][1].shape[1]
    assert m % tm == 0 and n % tn == 0
    in_specs, args = [], []
    for a, b in pairs:
        k = a.shape[1]
        in_specs += [pl.BlockSpec((tm, k), lambda i, j: (i, 0)), pl.BlockSpec((k, tn), lambda i, j: (0, j))]
        args += [a, b]
    if epi is not None:
        in_specs += [pl.BlockSpec((tm, tn), lambda i, j: (i, j)), pl.BlockSpec((1, tn), lambda i, j: (0, j)),
                     pl.BlockSpec((tm, tn), lambda i, j: (i, j))]
        args += list(epi)
    gate_idx = seq = None
    if residual is not None:
        res, mods, gate_idx, seq = residual
        in_specs += [pl.BlockSpec((tm, tn), lambda i, j: (i, j)), pl.BlockSpec((2, 6, tn), lambda i, j: (0, 0, j))]
        args += [res, mods]
    return pl.pallas_call(
        functools.partial(_mm_kernel, n_pairs=len(pairs), has_epi=epi is not None, gate_idx=gate_idx, seq=seq),
        grid=(m // tm, n // tn),
        in_specs=in_specs,
        out_specs=pl.BlockSpec((tm, tn), lambda i, j: (i, j)),
        out_shape=jax.ShapeDtypeStruct((m, n), out_dtype),
        compiler_params=_params(("parallel", "parallel")),
        name=name,
    )(*args)


def _adaln_kernel(cv_ref, w_ref, b_ref, o_ref):
    cv = cv_ref[...]
    s = cv * jax.nn.sigmoid(cv)
    o_ref[0] = jnp.dot(s, w_ref[0], precision=HIGHEST, preferred_element_type=F32) + b_ref[0]


def adaln(cv, w_ada, b_ada):
    depth, d, n = w_ada.shape
    tn = 1536
    return pl.pallas_call(
        _adaln_kernel,
        grid=(depth, n // tn),
        in_specs=[pl.BlockSpec((SUBLANE, d), lambda l, j: (0, 0)),
                  pl.BlockSpec((1, d, tn), lambda l, j: (l, 0, j)),
                  pl.BlockSpec((1, 1, tn), lambda l, j: (l, 0, j))],
        out_specs=pl.BlockSpec((1, SUBLANE, tn), lambda l, j: (l, 0, j)),
        out_shape=jax.ShapeDtypeStruct((depth, SUBLANE, n), F32),
        compiler_params=_params(("parallel", "parallel")),
        name="adaln",
    )(cv, w_ada, b_ada.reshape(depth, 1, n))


def _norm_mod_kernel(*refs, shift_idx, scale_idx, has_router, seq):
    it = iter(refs)
    x = next(it)[...]
    mods_ref = next(it) if shift_idx is not None else None
    g_ref = next(it)
    wr_ref = next(it) if has_router else None
    h_ref = next(it)
    y = x * lax.rsqrt(jnp.mean(x * x, axis=-1, keepdims=True) + EPS) * g_ref[...]
    if shift_idx is not None:
        tm = x.shape[0]
        row0 = pl.program_id(0) * tm
        y = y * (1.0 + _row_mod(mods_ref, scale_idx, row0, tm, seq)) + _row_mod(mods_ref, shift_idx, row0, tm, seq)
    h_ref[...] = y.astype(h_ref.dtype)
    if has_router:
        lg_ref = next(it)
        lg_ref[...] = lax.dot_general(wr_ref[...], y, (((1,), (1,)), ((), ())),
                                      precision=HIGHEST, preferred_element_type=F32)
        half = y.shape[1] // 2
        next(it)[...] = _pack_bf16_pair(y[:, :half], y[:, half:])


def _pack_bf16_pair(a, b):
    def rounded(x):
        u = lax.bitcast_convert_type(x, jnp.int32)
        return u + 0x7FFF + (lax.shift_right_logical(u, 16) & 1)
    return lax.shift_right_logical(rounded(a), 16) | (rounded(b) & -65536)


def _unpack_bf16_pair(w):
    return (lax.bitcast_convert_type(lax.shift_left(w, 16), F32),
            lax.bitcast_convert_type(w & -65536, F32))


def norm_mod(xs, g, *, n_rows, seq, mods=None, shift_idx=None, scale_idx=None, router_wt=None, out_dtype=BF16):
    d = xs.shape[1]
    tm = _token_tile(n_rows)
    row = pl.BlockSpec((tm, d), lambda i: (i, 0))
    in_specs, args = [row], [xs]
    if shift_idx is not None:
        in_specs.append(pl.BlockSpec((2, 6, d), lambda i: (0, 0, 0)))
        args.append(mods)
    in_specs.append(pl.BlockSpec((1, d), lambda i: (0, 0)))
    args.append(g.reshape(1, d))
    has_router = router_wt is not None
    if has_router:
        in_specs.append(pl.BlockSpec(router_wt.shape, lambda i: (0, 0)))
        args.append(router_wt)
    out_specs = [row]
    out_shape = [jax.ShapeDtypeStruct((n_rows, d), out_dtype)]
    if has_router:
        out_specs.append(pl.BlockSpec((N_EXPERTS, tm), lambda i: (0, i)))
        out_shape.append(jax.ShapeDtypeStruct((N_EXPERTS, n_rows), F32))
        out_specs.append(pl.BlockSpec((tm, d // 2), lambda i: (i, 0)))
        out_shape.append(jax.ShapeDtypeStruct((n_rows, d // 2), jnp.int32))
    return pl.pallas_call(
        functools.partial(_norm_mod_kernel, shift_idx=shift_idx, scale_idx=scale_idx, has_router=has_router, seq=seq),
        grid=(n_rows // tm,),
        in_specs=in_specs,
        out_specs=out_specs,
        out_shape=out_shape,
        compiler_params=_params(("parallel",)),
        name="norm_mod",
    )(*args)


def _rope_kernel(p_ref, cos_ref, sin_ref, o_ref, *, n_rot_blocks, head_dim, scales):
    cos = cos_ref[...]
    sin = sin_ref[...]
    for b in range(len(scales)):
        x = p_ref[:, b * LANE:(b + 1) * LANE]
        if b < n_rot_blocks:
            if head_dim == LANE:
                rot = pltpu.roll(x, LANE // 2, 1)
            else:
                lane = lax.broadcasted_iota(jnp.int32, x.shape, 1)
                first_half = (lane % head_dim) < head_dim // 2
                rot = jnp.where(first_half, pltpu.roll(x, LANE - head_dim // 2, 1), pltpu.roll(x, head_dim // 2, 1))
            x = x * cos + rot * sin
        if scales[b] != 1.0:
            x = x * scales[b]
        o_ref[:, b * LANE:(b + 1) * LANE] = x.astype(o_ref.dtype)


def rope_cast(p, cos, sin, *, width, n_rot_blocks, head_dim, scales, out_dtype):
    n_rows = p.shape[0]
    return pl.pallas_call(
        functools.partial(_rope_kernel, n_rot_blocks=n_rot_blocks, head_dim=head_dim, scales=scales),
        grid=(n_rows // ROW_TILE,),
        in_specs=[pl.BlockSpec((ROW_TILE, width), lambda i: (i, 0)),
                  pl.BlockSpec((ROW_TILE, LANE), lambda i: (i, 0)),
                  pl.BlockSpec((ROW_TILE, LANE), lambda i: (i, 0))],
        out_specs=pl.BlockSpec((ROW_TILE, width), lambda i: (i, 0)),
        out_shape=jax.ShapeDtypeStruct((n_rows, width), out_dtype),
        compiler_params=_params(("parallel",)),
        name="rope_cast",
    )(p, cos, sin)


def rope_tables(seq, ctx_len, head_dim):
    n_freq = head_dim // 4
    inv = ROPE_BASE ** (-jnp.arange(n_freq, dtype=F32) / n_freq)
    rows = seq // GRID_W
    row = jnp.repeat(jnp.arange(rows, dtype=F32), GRID_W)
    col = jnp.tile(jnp.arange(GRID_W, dtype=F32), rows)
    ang = jnp.concatenate([row[:, None] * inv, col[:, None] * inv], axis=-1)
    cos, sin = jnp.cos(ang), jnp.sin(ang)
    cos = jnp.concatenate([cos, cos], axis=-1)
    sin = jnp.concatenate([-sin, sin], axis=-1)
    reps = LANE // head_dim
    cos, sin = jnp.tile(cos, (1, reps)), jnp.tile(sin, (1, reps))
    cos = jnp.concatenate([cos, jnp.ones((ctx_len, LANE), F32)], axis=0)
    sin = jnp.concatenate([sin, jnp.zeros((ctx_len, LANE), F32)], axis=0)
    return cos, sin


def _ret_kernel(lg_ref, gc_ref, q_ref, k_ref, v_ref, *rest, reverse):
    if reverse:
        yf_ref, gate_ref, o_ref, s_ref = rest
    else:
        o_ref, s_ref = rest
    c = RET_CHUNK

    @pl.when(pl.program_id(0) == 0)
    def _():
        s_ref[...] = jnp.zeros_like(s_ref)

    ii = lax.broadcasted_iota(jnp.int32, (c, c), 0)
    jj = lax.broadcasted_iota(jnp.int32, (c, c), 1)
    rel = ((jj - ii) if reverse else (ii - jj)).astype(F32)
    pos = lax.broadcasted_iota(jnp.int32, (c, 1), 0).astype(F32)
    for h in range(RET_HEADS):
        lg = lg_ref[h]
        dec = jnp.where(rel >= 0, jnp.exp(jnp.maximum(rel, 0.0) * lg), 0.0)
        if reverse:
            q_dec = jnp.exp((c - pos) * lg)
            k_dec = jnp.exp(pos * lg)
        else:
            q_dec = jnp.exp((pos + 1.0) * lg)
            k_dec = jnp.exp((c - 1.0 - pos) * lg)
        q = q_ref[:, h * RET_DK:(h + 1) * RET_DK]
        k = k_ref[:, h * RET_DK:(h + 1) * RET_DK]
        v = v_ref[:, h * RET_DV:(h + 1) * RET_DV].astype(BF16)
        s = lax.dot_general(q.astype(BF16), k.astype(BF16), (((1,), (1,)), ((), ())),
                            preferred_element_type=F32) * dec
        state = s_ref[h]
        y = jnp.dot(s.astype(BF16), v, preferred_element_type=F32)
        y = y + jnp.dot((q * q_dec).astype(BF16), state.astype(BF16), preferred_element_type=F32)
        upd = lax.dot_general((k * k_dec).astype(BF16), v, (((0,), (0,)), ((), ())), preferred_element_type=F32)
        s_ref[h] = gc_ref[h] * state + upd
        if reverse:
            r = y + yf_ref[:, h * RET_DV:(h + 1) * RET_DV]
            mu = jnp.mean(r, axis=-1, keepdims=True)
            rc = r - mu
            var = jnp.mean(rc * rc, axis=-1, keepdims=True)
            g = gate_ref[:, h * RET_DV:(h + 1) * RET_DV]
            o_ref[:, h * RET_DV:(h + 1) * RET_DV] = (rc * lax.rsqrt(var + EPS) * (g * jax.nn.sigmoid(g))).astype(
                o_ref.dtype)
        else:
            o_ref[:, h * RET_DV:(h + 1) * RET_DV] = y


def retention(qkv, p, log_g, g_chunk, *, seq):
    n_rows = qkv.shape[0]
    n_chunks = n_rows // RET_CHUNK
    n_x = seq // RET_CHUNK
    smem = pl.BlockSpec(memory_space=pltpu.SMEM)

    def run(reverse, extra):
        if reverse:
            idx = lambda t: n_chunks - 1 - t
        else:
            idx = lambda t: (t + n_x) % n_chunks
        in_specs = [smem, smem,
                    pl.BlockSpec((RET_CHUNK, RET_QK), lambda t: (idx(t), 0)),
                    pl.BlockSpec((RET_CHUNK, RET_QK), lambda t: (idx(t), 1)),
                    pl.BlockSpec((RET_CHUNK, RET_V), lambda t: (idx(t), 1))]
        args = [log_g[1 if reverse else 0], g_chunk[1 if reverse else 0], qkv, qkv, qkv]
        if reverse:
            in_specs += [pl.BlockSpec((RET_CHUNK, RET_V), lambda t: (idx(t), 0)),
                         pl.BlockSpec((RET_CHUNK, RET_V), lambda t: (idx(t), 2))]
            args += list(extra)
        return pl.pallas_call(
            functools.partial(_ret_kernel, reverse=reverse),
            grid=(n_chunks,),
            in_specs=in_specs,
            out_specs=pl.BlockSpec((RET_CHUNK, RET_V), lambda t: (idx(t), 0)),
            out_shape=jax.ShapeDtypeStruct((n_rows, RET_V), BF16 if reverse else F32),
            scratch_shapes=[pltpu.VMEM((RET_HEADS, RET_DK, RET_DV), F32)],
            compiler_params=_params(("arbitrary",)),
            name="retention_bwd" if reverse else "retention_fwd",
        )(*args)

    y_fwd = run(False, None)
    return run(True, (y_fwd, p))


def _shortconv_kernel(cur_ref, prev_ref, next_ref, w_ref, b_ref, v_ref, x1_ref, x2_ref, *, x_tiles):
    i = pl.program_id(0)
    cur = cur_ref[...]
    rows = cur.shape[0]
    row = lax.broadcasted_iota(jnp.int32, (rows, 1), 0)
    has_prev = jnp.where((i == 0) | (i == x_tiles), 0.0, 1.0)
    has_next = jnp.where((i == x_tiles - 1) | (i == x_tiles), 0.0, 1.0)
    up = jnp.where(row == 0, prev_ref[SUBLANE - 1:SUBLANE, :] * has_prev, pltpu.roll(cur, 1, 0))
    dn = jnp.where(row == rows - 1, next_ref[0:1, :] * has_next, pltpu.roll(cur, rows - 1, 0))
    y = up * w_ref[0:1, :] + cur * w_ref[1:2, :] + dn * w_ref[2:3, :] + b_ref[...]
    v_ref[...] = y[:, :HY_WIDTH]
    x1_ref[...] = y[:, HY_WIDTH:2 * HY_WIDTH]
    x2_ref[...] = y[:, 2 * HY_WIDTH:]


def shortconv(p, w, b, *, seq):
    n_rows = p.shape[0]
    width = 3 * HY_WIDTH
    col = p.shape[1] // width - 1
    per = ROW_TILE // SUBLANE
    last = n_rows // SUBLANE - 1
    out = jax.ShapeDtypeStruct((n_rows, HY_WIDTH), F32)
    ospec = pl.BlockSpec((ROW_TILE, HY_WIDTH), lambda i: (i, 0))
    return pl.pallas_call(
        functools.partial(_shortconv_kernel, x_tiles=seq // ROW_TILE),
        grid=(n_rows // ROW_TILE,),
        in_specs=[pl.BlockSpec((ROW_TILE, width), lambda i: (i, col)),
                  pl.BlockSpec((SUBLANE, width), lambda i: (jnp.maximum(i * per - 1, 0), col)),
                  pl.BlockSpec((SUBLANE, width), lambda i: (jnp.minimum((i + 1) * per, last), col)),
                  pl.BlockSpec((3, width), lambda i: (0, 0)),
                  pl.BlockSpec((1, width), lambda i: (0, 0))],
        out_specs=[ospec, ospec, ospec],
        out_shape=[out, out, out],
        compiler_params=_params(("parallel",)),
        name="shortconv",
    )(p, p, p, w, b.reshape(1, width))


def _filt_kernel(z_ref, w1_ref, b1_ref, f1_ref, w2_ref, b2_ref, f2_ref, w3a_ref, w3b_ref, dl_ref, *o_ref):
    z = z_ref[...]
    h = jnp.sin(f1_ref[...] * (jnp.dot(z, w1_ref[...], precision=HIGHEST, preferred_element_type=F32) + b1_ref[...]))
    h = jnp.sin(f2_ref[...] * (jnp.dot(h, w2_ref[...], precision=HIGHEST, preferred_element_type=F32) + b2_ref[...]))
    window = jnp.exp(-z[:, 0:1] * dl_ref[...]) * z[:, HY_VALID_COL:HY_VALID_COL + 1]
    for o, w3_ref in enumerate((w3a_ref, w3b_ref)):
        o_ref[o][...] = jnp.dot(h, w3_ref[...], precision=HIGHEST, preferred_element_type=F32) * window


def hyena_filter_taps(length, w1, b1, f1, w2, b2, f2, w3):
    z = _filter_positions(length)
    w1p = jnp.zeros((HY_ZCOLS, HY_FFN), F32).at[:HY_EMB].set(w1)
    deltas = jnp.abs(jnp.linspace(math.log(HY_DECAY_TARGET) / HY_SLOW_DECAY,
                                  math.log(HY_DECAY_TARGET) / HY_FAST_DECAY, HY_WIDTH, dtype=F32)).reshape(1, HY_WIDTH)
    tm = min(length, 512)
    half_tiles = length // tm
    vec = lambda a: a.reshape(1, HY_FFN)
    small = lambda shape: pl.BlockSpec(shape, lambda i: (0, 0))
    w3_spec = lambda o: pl.BlockSpec((HY_FFN, HY_WIDTH), lambda i: (0, 2 * o + jnp.where(i >= half_tiles, 1, 0)))
    assert HY_ORDER == 2
    return pl.pallas_call(
        _filt_kernel,
        grid=(2 * half_tiles,),
        in_specs=[pl.BlockSpec((tm, HY_ZCOLS), lambda i: (i, 0)),
                  small((HY_ZCOLS, HY_FFN)), small((1, HY_FFN)), small((1, HY_FFN)),
                  small((HY_FFN, HY_FFN)), small((1, HY_FFN)), small((1, HY_FFN)),
                  w3_spec(0), w3_spec(1), small((1, HY_WIDTH))],
        out_specs=[pl.BlockSpec((tm, HY_WIDTH), lambda i: (i, 0))] * HY_ORDER,
        out_shape=[jax.ShapeDtypeStruct((2 * length, HY_WIDTH), F32)] * HY_ORDER,
        compiler_params=_params(("parallel",)),
        name="hyena_filter",
    )(z, w1p, vec(b1), vec(f1), w2, vec(b2), vec(f2), w3, w3, deltas)


def _filt2d_kernel(z_ref, w1_ref, b1_ref, f1_ref, w2_ref, b2_ref, f2_ref, w3fa_ref, w3ba_ref, w3fb_ref, w3bb_ref,
                   dl_ref, oa_ref, ob_ref):
    n1 = oa_ref.shape[0]
    c = dl_ref.shape[1]
    z = z_ref[...]
    h = jnp.sin(f1_ref[...] * (jnp.dot(z, w1_ref[...], precision=HIGHEST, preferred_element_type=F32) + b1_ref[...]))
    h = jnp.sin(f2_ref[...] * (jnp.dot(h, w2_ref[...], precision=HIGHEST, preferred_element_type=F32) + b2_ref[...]))
    window = jnp.exp(-z[:, 0:1] * dl_ref[...]) * z[:, HY_VALID_COL:HY_VALID_COL + 1]
    for o_ref, wf_ref, wb_ref in ((oa_ref, w3fa_ref, w3ba_ref), (ob_ref, w3fb_ref, w3bb_ref)):
        for j in range(SUBLANE):
            rows = slice(j * n1, (j + 1) * n1)
            hj = h[rows]
            taps = jnp.concatenate(
                [jnp.dot(hj[:n1 // 2], wf_ref[...], precision=HIGHEST, preferred_element_type=F32),
                 jnp.dot(hj[n1 // 2:], wb_ref[...], precision=HIGHEST, preferred_element_type=F32)], axis=0)
            o_ref[:, j * c:(j + 1) * c] = taps * window[rows]


def hyena_filter_taps_2d(length, w1, b1, f1, w2, b2, f2, w3):
    n1 = 2 * length // FFT_N2
    groups = FFT_N2 // SUBLANE
    z = _filter_positions(length).reshape(n1, groups, SUBLANE, HY_ZCOLS).transpose(1, 2, 0, 3)
    z = z.reshape(2 * length, HY_ZCOLS)
    w1p = jnp.zeros((HY_ZCOLS, HY_FFN), F32).at[:HY_EMB].set(w1)
    deltas = jnp.abs(jnp.linspace(math.log(HY_DECAY_TARGET) / HY_SLOW_DECAY,
                                  math.log(HY_DECAY_TARGET) / HY_FAST_DECAY, HY_WIDTH, dtype=F32)).reshape(1, HY_WIDTH)
    vec = lambda a: a.reshape(1, HY_FFN)
    small = lambda shape: pl.BlockSpec(shape, lambda i: (0, 0))
    w3_spec = lambda col: pl.BlockSpec((HY_FFN, HY_WIDTH), lambda i: (0, col))
    assert HY_ORDER == 2
    out = jax.ShapeDtypeStruct((n1, FFT_N2 * HY_WIDTH), F32)
    return pl.pallas_call(
        _filt2d_kernel,
        grid=(groups,),
        in_specs=[pl.BlockSpec((SUBLANE * n1, HY_ZCOLS), lambda i: (i, 0)),
                  small((HY_ZCOLS, HY_FFN)), small((1, HY_FFN)), small((1, HY_FFN)),
                  small((HY_FFN, HY_FFN)), small((1, HY_FFN)), small((1, HY_FFN)),
                  w3_spec(0), w3_spec(1), w3_spec(2), w3_spec(3), small((1, HY_WIDTH))],
        out_specs=[pl.BlockSpec((n1, SUBLANE * HY_WIDTH), lambda i: (0, i))] * HY_ORDER,
        out_shape=[out] * HY_ORDER,
        compiler_params=_params(("parallel",)),
        name="hyena_filter",
    )(z, w1p, vec(b1), vec(f1), w2, vec(b2), vec(f2), w3, w3, w3, w3, deltas)


def _filter_positions(length):
    t = jnp.concatenate([jnp.arange(length, dtype=F32), float(length) - jnp.arange(length, dtype=F32)])
    valid = jnp.ones((2 * length,), F32).at[length].set(0.0)
    t_norm = t / max(length - 1, 1)
    bands = jnp.linspace(1e-4, HY_BANDS - 1, HY_BANDS, dtype=F32)
    ang = (2.0 * math.pi / length) * t[:, None] * bands[None, :]
    z = jnp.concatenate([t_norm[:, None], jnp.cos(ang), -jnp.sin(ang), valid[:, None]], axis=-1)
    return jnp.pad(z, ((0, 0), (0, HY_ZCOLS - z.shape[1])))


def _angles(num, den):
    return (2.0 * math.pi / den) * (num % den).astype(F32)


def dft_tables_two_stage(m):
    n2 = FFT_N2
    n1 = m // n2
    half = n1 // 2
    kp = -(-(half + 1) // SUBLANE) * SUBLANE
    k1 = jnp.arange(kp, dtype=jnp.int32)
    live = (k1 <= half)
    a1 = _angles(k1[:, None] * jnp.arange(n1, dtype=jnp.int32)[None, :], n1)
    f1 = jnp.concatenate([jnp.where(live[:, None], jnp.cos(a1), 0.0), jnp.where(live[:, None], -jnp.sin(a1), 0.0)], 0)
    wgt = jnp.where((k1 == 0) | (k1 == half), 1.0, 2.0) * live / m
    a1h = a1[:, :half].T
    cinv = jnp.concatenate([jnp.cos(a1h) * wgt[None, :], -jnp.sin(a1h) * wgt[None, :]], axis=1)
    k = k1[:, None, None] + n1 * jnp.arange(n2, dtype=jnp.int32)[None, :, None]
    th = _angles(k * jnp.arange(n2, dtype=jnp.int32)[None, None, :], m)
    c = jnp.where(live[:, None, None], jnp.cos(th), 0.0)
    s = jnp.where(live[:, None, None], jnp.sin(th), 0.0)
    g_fwd = jnp.concatenate([jnp.concatenate([c, s], 2), jnp.concatenate([-s, c], 2)], 1)
    ct, st = jnp.swapaxes(c, 1, 2), jnp.swapaxes(s, 1, 2)
    g_inv = jnp.concatenate([jnp.concatenate([ct, -st], 2), jnp.concatenate([st, ct], 2)], 1)
    return dict(n1=n1, kp=kp, f1=f1.astype(BF16), f1_half=f1[:, :half].astype(BF16), cinv=cinv.astype(BF16),
                g_fwd=g_fwd.astype(BF16), g_inv=g_inv.astype(BF16))


def dft_tables_one_stage(m):
    half = m // 2
    kp = -(-(half + 1) // SUBLANE) * SUBLANE
    k = jnp.arange(kp, dtype=jnp.int32)
    live = (k <= half)
    a = _angles(k[:, None] * jnp.arange(m, dtype=jnp.int32)[None, :], m)
    f = jnp.concatenate([jnp.where(live[:, None], jnp.cos(a), 0.0), jnp.where(live[:, None], -jnp.sin(a), 0.0)], 0)
    wgt = jnp.where((k == 0) | (k == half), 1.0, 2.0) * live / m
    ah = a[:, :half].T
    cinv = jnp.concatenate([jnp.cos(ah) * wgt[None, :], -jnp.sin(ah) * wgt[None, :]], axis=1)
    return dict(kp=kp, f=f.astype(BF16), f_half=f[:, :half].astype(BF16), cinv=cinv.astype(BF16))


def _bmm_kernel(*refs, kb, in_part_major, out_part_major, has_h):
    if has_h:
        g_ref, a_ref, h_ref, o_ref = refs
    else:
        g_ref, a_ref, o_ref = refs
    n2 = FFT_N2
    for b in range(kb):
        if in_part_major:
            ar, ai = a_ref[0, b], a_ref[1, b]
        else:
            ar, ai = a_ref[b, 0], a_ref[b, 1]
        if has_h:
            hr, hi = h_ref[b, 0], h_ref[b, 1]
            ar, ai = ar * hr - ai * hi, ar * hi + ai * hr
        xin = jnp.concatenate([ar, ai], axis=0).astype(BF16)
        y = jnp.dot(g_ref[b], xin, preferred_element_type=F32)
        if out_part_major:
            o_ref[0, b] = y[:n2].astype(o_ref.dtype)
            o_ref[1, b] = y[n2:].astype(o_ref.dtype)
        else:
            o_ref[b, 0] = y[:n2].astype(o_ref.dtype)
            o_ref[b, 1] = y[n2:].astype(o_ref.dtype)


def bmm_k1(g, a, h=None, *, in_part_major, out_part_major):
    kp = g.shape[0]
    n2 = FFT_N2
    c = a.shape[-1]
    kb, tc = SUBLANE, min(c, 512)
    pm = lambda: pl.BlockSpec((2, kb, n2, tc), lambda i, j: (0, i, 0, j))
    km = lambda: pl.BlockSpec((kb, 2, n2, tc), lambda i, j: (i, 0, 0, j))
    in_specs = [pl.BlockSpec((kb, 2 * n2, 2 * n2), lambda i, j: (i, 0, 0)), pm() if in_part_major else km()]
    args = [g, a]
    if h is not None:
        in_specs.append(km())
        args.append(h)
    return pl.pallas_call(
        functools.partial(_bmm_kernel, kb=kb, in_part_major=in_part_major, out_part_major=out_part_major,
                          has_h=h is not None),
        grid=(kp // kb, c // tc),
        in_specs=in_specs,
        out_specs=pm() if out_part_major else km(),
        out_shape=jax.ShapeDtypeStruct((2, kp, n2, c), BF16) if out_part_major else
        jax.ShapeDtypeStruct((kp, 2, n2, c), F32),
        compiler_params=_params(("parallel", "parallel")),
        name="dft_inner",
    )(*args)


def _cmul_kernel(x_ref, h_ref, o_ref):
    xr, xi, hr, hi = x_ref[0], x_ref[1], h_ref[0], h_ref[1]
    o_ref[0] = xr * hr - xi * hi
    o_ref[1] = xr * hi + xi * hr


def cmul(x, h):
    spec = pl.BlockSpec(x.shape, lambda i: (0, 0, 0))
    return pl.pallas_call(_cmul_kernel, grid=(1,), in_specs=[spec, spec], out_specs=spec,
                          out_shape=jax.ShapeDtypeStruct(x.shape, F32), compiler_params=_params(("arbitrary",)),
                          name="spectrum_product")(x, h)


def _dft_outer3_kernel(f_ref, x_ref, o_ref):
    c = x_ref.shape[2]
    f = f_ref[...]
    for j in range(SUBLANE):
        o_ref[:, j * c:(j + 1) * c] = jnp.dot(f, x_ref[:, j, :].astype(BF16),
                                              preferred_element_type=F32).astype(o_ref.dtype)


def dft_outer3(f, x3, n_outer):
    rows = f.shape[0]
    c = x3.shape[2]
    return pl.pallas_call(
        _dft_outer3_kernel,
        grid=(FFT_N2 // SUBLANE,),
        in_specs=[pl.BlockSpec((rows, n_outer), lambda j: (0, 0)),
                  pl.BlockSpec((n_outer, SUBLANE, c), lambda j: (0, j, 0))],
        out_specs=pl.BlockSpec((rows, SUBLANE * c), lambda j: (0, j)),
        out_shape=jax.ShapeDtypeStruct((rows, FFT_N2 * c), BF16),
        compiler_params=_params(("parallel",)),
        name="dft_outer",
    )(f, x3)


def _idft_gate3_kernel(cinv_ref, b_ref, gate_ref, skip_ref, u_ref, o_ref, *, u_is_3d):
    c = gate_ref.shape[2]
    cinv = cinv_ref[...]
    for j in range(SUBLANE):
        cols = slice(j * c, (j + 1) * c)
        acc = jnp.dot(cinv, b_ref[:, cols].astype(BF16), preferred_element_type=F32)
        u = u_ref[:, j, :] if u_is_3d else u_ref[:, cols]
        o_ref[:, cols] = gate_ref[:, j, :] * (acc + skip_ref[...] * u)


def idft_gate3(cinv, b2d, gate3, skip_row, u):
    n_outer = cinv.shape[0]
    c = gate3.shape[2]
    u_is_3d = u.ndim == 3
    wide = pl.BlockSpec((n_outer, SUBLANE * c), lambda j: (0, j))
    slab = pl.BlockSpec((n_outer, SUBLANE, c), lambda j: (0, j, 0))
    return pl.pallas_call(
        functools.partial(_idft_gate3_kernel, u_is_3d=u_is_3d),
        grid=(FFT_N2 // SUBLANE,),
        in_specs=[pl.BlockSpec(cinv.shape, lambda j: (0, 0)),
                  pl.BlockSpec((b2d.shape[0], SUBLANE * c), lambda j: (0, j)),
                  slab, pl.BlockSpec((1, c), lambda j: (0, 0)), slab if u_is_3d else wide],
        out_specs=wide,
        out_shape=jax.ShapeDtypeStruct((n_outer, FFT_N2 * c), F32),
        compiler_params=_params(("parallel",)),
        name="idft_outer_gate",
    )(cinv, b2d, gate3, skip_row, u)


def long_conv_two_stage(tabs, taps, v, x1, x2, skip, length):
    c = v.shape[1]
    n2, n1, kp = FFT_N2, tabs["n1"], tabs["kp"]
    as3 = lambda a: a.reshape(a.shape[0] // n2, n2, c)
    spectrum = lambda a2d: bmm_k1(tabs["g_fwd"], a2d.reshape(2, kp, n2, c), in_part_major=True, out_part_major=False)

    spectra = [spectrum(mm([(tabs["f1"], taps[o])], BF16, 2 * kp, 2048, name="dft_outer")) for o in range(HY_ORDER)]
    v3 = as3(v)
    u = v3
    for o, gate in enumerate((x1, x2)):
        if u.ndim == 3:
            a = dft_outer3(tabs["f1_half"], u, n1 // 2)
        else:
            a = mm([(tabs["f1_half"], u)], BF16, 2 * kp, 2048, name="dft_outer")
        bt = bmm_k1(tabs["g_inv"], spectrum(a), spectra[o], in_part_major=False, out_part_major=True)
        u = idft_gate3(tabs["cinv"], bt.reshape(2 * kp, n2 * c), as3(gate), skip[o].reshape(1, c), u)
    return u.reshape(length, c)


def long_conv_one_stage(tabs, taps, v, x1, x2, skip):
    length, c = v.shape
    kp = tabs["kp"]
    u = v
    for o, gate in enumerate((x1, x2)):
        hs = mm([(tabs["f"], taps[o])], F32, 2 * kp, c, name="ctx_dft").reshape(2, kp, c)
        xs = mm([(tabs["f_half"], u)], F32, 2 * kp, c, name="ctx_dft").reshape(2, kp, c)
        ys = cmul(xs, hs).reshape(2 * kp, c)
        u = mm([(tabs["cinv"], ys)], F32, length, c, epi=(gate, skip[o].reshape(1, c), u), name="ctx_idft_gate")
    return u


def _flash_kernel(lam_ref, qt_ref, k_ref, vt_ref, sub_ref, o_ref, m_ref, excess_ref, acc_ref, *, kv, seq, ctx_len,
                  out_scale):
    i = pl.program_id(1)
    last_q = pl.num_programs(1) - 1
    tq = qt_ref.shape[1]
    d = DA_HEAD_DIM
    dv = 2 * DA_HEAD_DIM
    n_chunks = k_ref.shape[0] // kv
    acc_ref[...] = jnp.zeros_like(acc_ref)

    def scores(off, rows, c, masked):
        s = jnp.dot(k_ref[pl.ds(off, rows), c * d:(c + 1) * d], qt_ref[c * d:(c + 1) * d, :],
                    preferred_element_type=F32)
        if masked:
            key = off + lax.broadcasted_iota(jnp.int32, (rows, 1), 0)
            lane = lax.broadcasted_iota(jnp.int32, (1, tq), 1)
            s = s + jnp.where(key < seq, NEG_BIG, 0.0) * jnp.where(lane >= tq - ctx_len, 1.0, 0.0)
        return s

    def exact_step(off, c, masked):
        s = scores(off, kv, c, masked)
        m_old = m_ref[c]
        m_new = jnp.maximum(m_old, jnp.max(s, axis=0, keepdims=True))
        pr = jnp.exp2(s - m_new).astype(BF16)
        acc_ref[c] = jnp.exp2(m_old - m_new) * acc_ref[c] + jnp.dot(vt_ref[:, pl.ds(off, kv)], pr,
                                                                   preferred_element_type=F32)
        m_ref[c] = m_new

    def lazy_step(off, c, masked):
        s = scores(off, kv, c, masked)
        m_old = m_ref[c]
        m_chunk = jnp.max(s, axis=0, keepdims=True)
        pv = jnp.dot(vt_ref[:, pl.ds(off, kv)], jnp.exp2(s - m_old).astype(BF16), preferred_element_type=F32)
        m_new = jnp.maximum(m_old, m_chunk)
        acc_ref[c] = jnp.exp2(m_old - m_new) * (acc_ref[c] + pv)
        m_ref[c] = m_new
        excess_ref[c] = jnp.maximum(excess_ref[c], m_chunk - m_old)

    def all_chunks(step, masked):
        def body(kc, carry):
            off = pl.multiple_of(kc * kv, kv)
            for c in range(2):
                step(off, c, masked)
            return carry

        lax.fori_loop(0, n_chunks, body, 0)

    def run(masked):
        for c in range(2):
            m0 = jnp.max(scores(0, FLASH_INIT_KEYS, c, False), axis=0, keepdims=True)
            if masked:
                lane = lax.broadcasted_iota(jnp.int32, (1, tq), 1)
                m_ctx = jnp.max(scores(seq, FLASH_INIT_KEYS, c, False), axis=0, keepdims=True)
                m0 = jnp.where(lane >= tq - ctx_len, m_ctx, m0)
            m_ref[c] = m0
        excess_ref[...] = jnp.full_like(excess_ref, NEG_BIG)
        all_chunks(lazy_step, masked)

        @pl.when(jnp.max(excess_ref[...]) > FLASH_LAZY_HEADROOM)
        def _():
            m_ref[...] = jnp.full_like(m_ref, NEG_BIG)
            acc_ref[...] = jnp.zeros_like(acc_ref)
            all_chunks(exact_step, masked)

    @pl.when(i != last_q)
    def _():
        run(False)

    @pl.when(i == last_q)
    def _():
        run(True)

    a0 = acc_ref[0, :dv, :] / acc_ref[0, dv:dv + 1, :]
    a1 = acc_ref[1, :dv, :] / acc_ref[1, dv:dv + 1, :]
    o = (a0 - lam_ref[0] * a1).T
    o = o * lax.rsqrt(jnp.mean(o * o, axis=-1, keepdims=True) + 1e-5) * sub_ref[...]
    o_ref[...] = (o * out_scale).astype(o_ref.dtype)


def _rope_da_kernel(p_ref, cos_ref, sin_ref, qt_ref, k_ref, vt_ref):
    cos = cos_ref[...]
    sin = sin_ref[...]
    hw = 2 * DA_HEAD_DIM
    lane = lax.broadcasted_iota(jnp.int32, cos.shape, 1)
    first_half = (lane % DA_HEAD_DIM) < DA_HEAD_DIM // 2

    def rotated(b):
        x = p_ref[:, b * LANE:(b + 1) * LANE]
        rot = jnp.where(first_half, pltpu.roll(x, LANE - DA_HEAD_DIM // 2, 1), pltpu.roll(x, DA_HEAD_DIM // 2, 1))
        return x * cos + rot * sin

    ones = jnp.ones((FLASH_ONES_ROWS, cos.shape[0]), BF16)
    for h in range(DA_HEADS):
        qt_ref[h * hw:(h + 1) * hw, :] = (rotated(h) * (LOG2_E * DA_HEAD_DIM ** -0.5)).T.astype(BF16)
        k_ref[:, h * hw:(h + 1) * hw] = rotated(DA_HEADS + h).astype(BF16)
        base = h * (hw + FLASH_ONES_ROWS)
        vt_ref[base:base + hw, :] = p_ref[:, (2 * DA_HEADS + h) * LANE:(2 * DA_HEADS + h + 1) * LANE].T.astype(BF16)
        vt_ref[base + hw:base + hw + FLASH_ONES_ROWS, :] = ones


def rope_da(p, cos, sin):
    n_rows = p.shape[0]
    assert 2 * DA_HEAD_DIM == LANE
    vt_rows = DA_HEADS * (LANE + FLASH_ONES_ROWS)
    return pl.pallas_call(
        _rope_da_kernel,
        grid=(n_rows // ROW_TILE,),
        in_specs=[pl.BlockSpec((ROW_TILE, 3 * DA_WIDTH), lambda i: (i, 0)),
                  pl.BlockSpec((ROW_TILE, LANE), lambda i: (i, 0)),
                  pl.BlockSpec((ROW_TILE, LANE), lambda i: (i, 0))],
        out_specs=[pl.BlockSpec((DA_WIDTH, ROW_TILE), lambda i: (0, i)),
                   pl.BlockSpec((ROW_TILE, DA_WIDTH), lambda i: (i, 0)),
                   pl.BlockSpec((vt_rows, ROW_TILE), lambda i: (0, i))],
        out_shape=[jax.ShapeDtypeStruct((DA_WIDTH, n_rows), BF16),
                   jax.ShapeDtypeStruct((n_rows, DA_WIDTH), BF16),
                   jax.ShapeDtypeStruct((vt_rows, n_rows), BF16)],
        compiler_params=_params(("parallel",)),
        name="rope_da",
    )(p, cos, sin)


def diff_attention(qt, k, vt, lam_full, subln, *, seq, ctx_len, lambda_init):
    n_rows = k.shape[0]
    tq = _token_tile(n_rows)
    hw = 2 * DA_HEAD_DIM
    ones_rows = FLASH_ONES_ROWS
    return pl.pallas_call(
        functools.partial(_flash_kernel, kv=tq, seq=seq, ctx_len=ctx_len, out_scale=1.0 - lambda_init),
        grid=(DA_HEADS, n_rows // tq),
        in_specs=[pl.BlockSpec(memory_space=pltpu.SMEM),
                  pl.BlockSpec((hw, tq), lambda h, i: (h, i)),
                  pl.BlockSpec((n_rows, hw), lambda h, i: (0, h)),
                  pl.BlockSpec((hw + ones_rows, n_rows), lambda h, i: (h, 0)),
                  pl.BlockSpec((1, hw), lambda h, i: (0, 0))],
        out_specs=pl.BlockSpec((tq, hw), lambda h, i: (i, h)),
        out_shape=jax.ShapeDtypeStruct((n_rows, DA_WIDTH), BF16),
        scratch_shapes=[pltpu.VMEM((2, 1, tq), F32), pltpu.VMEM((2, 1, tq), F32),
                        pltpu.VMEM((2, hw + ones_rows, tq), F32)],
        compiler_params=pltpu.CompilerParams(dimension_semantics=("parallel", "parallel"),
                                             vmem_limit_bytes=FLASH_VMEM_LIMIT),
        name="diff_attention",
    )(lam_full.reshape(1), qt, k, vt, subln.reshape(1, hw))


def _route_kernel(lg_ref, b_ref, tri_ref, eidx_ref, w_ref, rank_ref, cnt_ref, carry_ref):
    t = lg_ref.shape[1]

    @pl.when(pl.program_id(0) == 0)
    def _():
        carry_ref[...] = jnp.zeros_like(carry_ref)

    scores = jax.nn.sigmoid(lg_ref[...])
    choice = (scores + b_ref[...]).reshape(N_GROUPS, GROUP_SIZE, t)
    s3 = scores.reshape(N_GROUPS, GROUP_SIZE, t)
    member = lax.broadcasted_iota(jnp.int32, choice.shape, 1)
    group = lax.broadcasted_iota(jnp.int32, (N_GROUPS, 1, t), 0)
    expert = lax.broadcasted_iota(jnp.int32, choice.shape, 0) * GROUP_SIZE + member
    neg_inf = -jnp.inf
    m1 = jnp.max(choice, axis=1, keepdims=True)
    first = jnp.min(jnp.where(choice == m1, member, GROUP_SIZE), axis=1, keepdims=True)
    m2 = jnp.max(jnp.where(member == first, neg_inf, choice), axis=1, keepdims=True)
    gscore = m1 + m2
    gsel = jnp.zeros(gscore.shape, F32)
    for _ in range(TOPK_GROUPS):
        m = jnp.max(gscore, axis=0, keepdims=True)
        f = jnp.min(jnp.where(gscore == m, group, N_GROUPS), axis=0, keepdims=True)
        hit = group == f
        gsel = jnp.where(hit, 1.0, gsel)
        gscore = jnp.where(hit, neg_inf, gscore)
    cand = jnp.where(gsel > 0.0, choice, neg_inf)
    esel = jnp.zeros(choice.shape, F32)
    picks = []
    for _ in range(TOP_K):
        m = jnp.max(jnp.max(cand, axis=1, keepdims=True), axis=0, keepdims=True)
        f = jnp.min(jnp.min(jnp.where(cand == m, expert, N_EXPERTS), axis=1, keepdims=True), axis=0, keepdims=True)
        hit = expert == f
        esel = jnp.where(hit, 1.0, esel)
        cand = jnp.where(hit, neg_inf, cand)
        picks.append(f)
    w = s3 * esel
    denom = jnp.sum(jnp.sum(w, axis=1, keepdims=True), axis=0, keepdims=True) + 1e-20
    w = w / denom * ROUTED_SCALE
    sel = esel.reshape(N_EXPERTS, t)
    before = jnp.dot(sel.astype(BF16), tri_ref[...], preferred_element_type=F32) + carry_ref[...]
    before = before.reshape(N_GROUPS, GROUP_SIZE, t)
    pick = lambda a, hit: jnp.sum(jnp.sum(jnp.where(hit, a, 0.0), axis=1, keepdims=True), axis=0).reshape(1, t)
    for k, f in enumerate(picks):
        hit = expert == f
        eidx_ref[k:k + 1, :] = f.reshape(1, t)
        w_ref[k:k + 1, :] = pick(w, hit)
        rank_ref[k:k + 1, :] = pick(before, hit).astype(jnp.int32)
    carry_ref[...] += jnp.sum(sel, axis=1, keepdims=True)
    cnt_ref[...] = carry_ref[...]


def route(logits_t, bias, lo, hi):
    t = _token_tile(logits_t.shape[1])
    n = hi - lo
    tile0 = lo // t
    tri = (jnp.arange(t)[:, None] < jnp.arange(t)[None, :]).astype(BF16)
    tok = lambda dt: jax.ShapeDtypeStruct((TOP_K, n), dt)
    tok_spec = pl.BlockSpec((TOP_K, t), lambda i: (0, i))
    return pl.pallas_call(
        _route_kernel,
        grid=(n // t,),
        in_specs=[pl.BlockSpec((N_EXPERTS, t), lambda i: (0, tile0 + i)),
                  pl.BlockSpec((N_EXPERTS, 1), lambda i: (0, 0)),
                  pl.BlockSpec((t, t), lambda i: (0, 0))],
        out_specs=[tok_spec, tok_spec, tok_spec, pl.BlockSpec((N_EXPERTS, 1), lambda i: (0, 0))],
        out_shape=[tok(jnp.int32), tok(F32), tok(jnp.int32), jax.ShapeDtypeStruct((N_EXPERTS, 1), F32)],
        scratch_shapes=[pltpu.VMEM((N_EXPERTS, 1), F32)],
        compiler_params=_params(("arbitrary",)),
        name="route",
    )(logits_t, bias.reshape(N_EXPERTS, 1), tri)


def _slot_kernel(start_ref, eidx_ref, rank_ref, dest_ref):
    e = eidx_ref[...]
    d = rank_ref[...]
    for x in range(N_EXPERTS):
        d = d + jnp.where(e == x, start_ref[x], 0)
    dest_ref[...] = d


def slot_index(pad_start, eidx, rank):
    n = eidx.shape[1]
    t = _token_tile(n)
    spec = pl.BlockSpec((TOP_K, t), lambda i: (0, i))
    return pl.pallas_call(
        _slot_kernel,
        grid=(n // t,),
        in_specs=[pl.BlockSpec(memory_space=pltpu.SMEM), spec, spec],
        out_specs=spec,
        out_shape=jax.ShapeDtypeStruct((TOP_K, n), jnp.int32),
        compiler_params=_params(("parallel",)),
        name="slot_index",
    )(pad_start, eidx, rank)


def _sc_worker():
    return lax.axis_index("s") * SC_CORES + lax.axis_index("c")


def sc_dispatch(h, dest3, n_slots, row0):
    d = h.shape[1]
    n_win = dest3.shape[0]
    mesh = plsc.VectorSubcoreMesh(core_axis_name="c", subcore_axis_name="s")

    @functools.partial(
        pl.kernel, mesh=mesh, out_type=jax.ShapeDtypeStruct((n_slots, d), h.dtype),
        scratch_types=[pltpu.VMEM((TOP_K, SC_WINDOW), jnp.int32), pltpu.VMEM((SC_WINDOW, d), h.dtype),
                       pltpu.SemaphoreType.DMA])
    def k(h_hbm, dest_hbm, out_hbm, idx_v, rows_v, sem):
        wid = _sc_worker()

        @pl.loop(0, -(-n_win // SC_WORKERS))
        def _(it):
            w = it * SC_WORKERS + wid

            @pl.when(w < n_win)
            def _():
                pltpu.sync_copy(dest_hbm.at[w], idx_v)
                pltpu.sync_copy(h_hbm.at[pl.ds(row0 + w * SC_WINDOW, SC_WINDOW)], rows_v)
                copies = [pltpu.async_copy(rows_v, out_hbm.at[idx_v.at[j]], sem) for j in range(TOP_K)]
                for c in copies:
                    c.wait()

    return k(h, dest3)


def sc_combine_gather(y, dest3):
    d = y.shape[1]
    n_win = dest3.shape[0]
    n = n_win * SC_WINDOW
    mesh = plsc.VectorSubcoreMesh(core_axis_name="c", subcore_axis_name="s")

    @functools.partial(
        pl.kernel, mesh=mesh, out_type=jax.ShapeDtypeStruct((TOP_K, n, d), y.dtype),
        scratch_types=[pltpu.VMEM((TOP_K, SC_WINDOW), jnp.int32), pltpu.VMEM((2, SC_WINDOW, d), y.dtype),
                       pltpu.SemaphoreType.DMA, pltpu.SemaphoreType.DMA,
                       pltpu.SemaphoreType.DMA, pltpu.SemaphoreType.DMA])
    def k(y_hbm, dest_hbm, out_hbm, idx_v, rows_v, gsem0, gsem1, osem0, osem1):
        wid = _sc_worker()
        gsem, osem = (gsem0, gsem1), (osem0, osem1)

        @pl.loop(0, -(-n_win // SC_WORKERS))
        def _(it):
            w = it * SC_WORKERS + wid

            @pl.when(w < n_win)
            def _():
                pltpu.sync_copy(dest_hbm.at[w], idx_v)
                gather = lambda j: pltpu.async_copy(y_hbm.at[idx_v.at[j]], rows_v.at[j % 2], gsem[j % 2])
                g = [None] * TOP_K
                o = [None] * TOP_K
                g[0] = gather(0)
                for j in range(TOP_K):
                    if j + 1 < TOP_K:
                        if j >= 1:
                            o[j - 1].wait()
                        g[j + 1] = gather(j + 1)
                    g[j].wait()
                    o[j] = pltpu.async_copy(rows_v.at[j % 2], out_hbm.at[j, pl.ds(w * SC_WINDOW, SC_WINDOW)],
                                            osem[j % 2])
                o[TOP_K - 2].wait()
                o[TOP_K - 1].wait()

    return k(y, dest3)


def _expert_ffn_kernel(be_ref, bv_ref, x_ref, wg_ref, wu_ref, wd_ref, o_ref, wg_s, wu_s, wd_s):
    b = pl.program_id(0)
    valid = bv_ref[b]
    new_expert = (b == 0) | (be_ref[b] != be_ref[jnp.maximum(b - 1, 0)])

    @pl.when(new_expert)
    def _():
        wg_s[...] = wg_ref[0, 0].astype(BF16)
        wu_s[...] = wu_ref[0, 0].astype(BF16)
        wd_s[...] = wd_ref[0, 0].astype(BF16)

    sub = x_ref.shape[0] // MOE_SUB_BLOCKS

    def sub_block(r):
        row = lax.broadcasted_iota(jnp.int32, (sub, 1), 0) + r * sub
        rows = pl.ds(r * sub, sub)
        lo, hi = _unpack_bf16_pair(jnp.where(row < valid, x_ref[rows, :], 0))
        x = jnp.concatenate([lo.astype(BF16), hi.astype(BF16)], axis=1)
        a = jnp.dot(x, wg_s[...], preferred_element_type=F32)
        a = a * jax.nn.sigmoid(a) * jnp.dot(x, wu_s[...], preferred_element_type=F32)
        y = jnp.dot(a.astype(BF16), wd_s[...], preferred_element_type=F32)
        half = y.shape[1] // 2
        o_ref[rows, :] = _pack_bf16_pair(y[:, :half], y[:, half:])

    for live in range(1, MOE_SUB_BLOCKS + 1):
        upper = valid <= live * sub if live < MOE_SUB_BLOCKS else True

        @pl.when((valid > (live - 1) * sub) & upper)
        def _():
            for r in range(live):
                sub_block(r)


def expert_ffn(xg, block_expert, block_valid, wg, wu, wd, layer):
    n_slots, dp = xg.shape
    d, f = wg.shape[-2:]
    grid_spec = pltpu.PrefetchScalarGridSpec(
        num_scalar_prefetch=2,
        grid=(n_slots // MOE_BLOCK,),
        in_specs=[pl.BlockSpec((MOE_BLOCK, dp), lambda b, be, bv: (b, 0)),
                  pl.BlockSpec((1, 1, d, f), lambda b, be, bv: (layer, be[b], 0, 0)),
                  pl.BlockSpec((1, 1, d, f), lambda b, be, bv: (layer, be[b], 0, 0)),
                  pl.BlockSpec((1, 1, f, d), lambda b, be, bv: (layer, be[b], 0, 0))],
        out_specs=pl.BlockSpec((MOE_BLOCK, dp), lambda b, be, bv: (b, 0)),
        scratch_shapes=[pltpu.VMEM((d, f), BF16), pltpu.VMEM((d, f), BF16), pltpu.VMEM((f, d), BF16)],
    )
    return pl.pallas_call(
        _expert_ffn_kernel,
        grid_spec=grid_spec,
        out_shape=jax.ShapeDtypeStruct((n_slots, dp), jnp.int32),
        compiler_params=_params(("arbitrary",)),
        name="expert_ffn",
    )(block_expert, block_valid, xg, wg, wu, wd)


def _combine_kernel(yg_ref, w_ref, h_ref, swg_ref, swu_ref, swd_ref, xs_ref, mods_ref, *rest, tile0, gate_idx, seq):
    o_ref = rest[-1]
    h = h_ref[...]
    a = jnp.dot(h, swg_ref[...], preferred_element_type=F32)
    a = a * jax.nn.sigmoid(a) * jnp.dot(h, swu_ref[...], preferred_element_type=F32)
    acc = jnp.dot(a.astype(BF16), swd_ref[...], preferred_element_type=F32)
    half = acc.shape[1] // 2
    acc_lo, acc_hi = acc[:, :half], acc[:, half:]
    wt = w_ref[...].T
    for k in range(TOP_K):
        lo, hi = _unpack_bf16_pair(yg_ref[k])
        acc_lo = acc_lo + wt[:, k:k + 1] * lo
        acc_hi = acc_hi + wt[:, k:k + 1] * hi
    tm = h.shape[0]
    gate = _row_mod(mods_ref, gate_idx, (tile0 + pl.program_id(0)) * tm, tm, seq)
    o_ref[:, :half] = xs_ref[:, :half] + gate[:, :half] * acc_lo
    o_ref[:, half:] = xs_ref[:, half:] + gate[:, half:] * acc_hi


def combine(yg, w, h, swg, swu, swd, residual, lo, prev):
    xs, mods, gate_idx, seq = residual
    n_all, d = h.shape
    n = w.shape[1]
    f = swg.shape[-1]
    tm = ROW_TILE
    tile0 = lo // tm
    in_specs = [pl.BlockSpec((TOP_K, tm, d // 2), lambda i: (0, i, 0)),
                pl.BlockSpec((TOP_K, tm), lambda i: (0, i)),
                pl.BlockSpec((tm, d), lambda i: (tile0 + i, 0)),
                pl.BlockSpec((d, f), lambda i: (0, 0)),
                pl.BlockSpec((d, f), lambda i: (0, 0)),
                pl.BlockSpec((f, d), lambda i: (0, 0)),
                pl.BlockSpec((tm, d), lambda i: (tile0 + i, 0)),
                pl.BlockSpec((2, 6, d), lambda i: (0, 0, 0))]
    args = [yg, w, h, swg, swu, swd, xs, mods]
    aliases = {}
    if prev is not None:
        in_specs.append(pl.BlockSpec(memory_space=pl.ANY))
        args.append(prev)
        aliases = {len(args) - 1: 0}
    return pl.pallas_call(
        functools.partial(_combine_kernel, tile0=tile0, gate_idx=gate_idx, seq=seq),
        grid=(n // tm,),
        in_specs=in_specs,
        out_specs=pl.BlockSpec((tm, d), lambda i: (tile0 + i, 0)),
        out_shape=jax.ShapeDtypeStruct((n_all, d), F32),
        input_output_aliases=aliases,
        compiler_params=_params(("parallel",)),
        name="moe_combine",
    )(*args)


def moe(h, h_packed, logits_t, bias, wg, wu, wd, layer, swg, swu, swd, residual):
    n = h.shape[0]
    t = _token_tile(n)
    cut = (n // t + 1) // 2 * t
    shared = (swg.astype(BF16), swu.astype(BF16), swd.astype(BF16))
    staged = [_moe_experts(h_packed, logits_t, bias, wg, wu, wd, layer, lo, hi) for lo, hi in ((0, cut), (cut, n))]
    out = None
    for (yg, w), lo in zip(staged, (0, cut)):
        out = combine(yg, w, h, *shared, residual, lo, out)
    return out


def _moe_experts(h_packed, logits_t, bias, wg, wu, wd, layer, lo, hi):
    n = hi - lo
    eidx, w, rank, counts = route(logits_t, bias, lo, hi)
    counts = counts.reshape(N_EXPERTS).astype(jnp.int32)
    padded = (counts + MOE_BLOCK - 1) // MOE_BLOCK * MOE_BLOCK
    pad_end = jnp.cumsum(padded)
    pad_start = pad_end - padded
    n_slots = n * TOP_K + N_EXPERTS * MOE_BLOCK
    starts = jnp.arange(n_slots // MOE_BLOCK, dtype=jnp.int32) * MOE_BLOCK
    owner = jnp.sum((pad_end[None, :] <= starts[:, None]).astype(jnp.int32), axis=1)
    block_expert = jnp.minimum(owner, N_EXPERTS - 1)
    member = (block_expert[:, None] == jnp.arange(N_EXPERTS, dtype=jnp.int32)[None, :]).astype(jnp.int32)
    left = jnp.sum(member * (counts + pad_start)[None, :], axis=1) - starts
    block_valid = jnp.clip(left, 0, MOE_BLOCK).astype(jnp.int32)
    dest = slot_index(pad_start.astype(jnp.int32), eidx, rank)
    dest3 = dest.reshape(TOP_K, n // SC_WINDOW, SC_WINDOW).transpose(1, 0, 2)
    xg = sc_dispatch(h_packed, dest3, n_slots, lo)
    y = expert_ffn(xg, block_expert, block_valid, wg, wu, wd, layer)
    return sc_combine_gather(y, dest3), w


def mixer_ab(h, w_in, w_out, decay_logit, conv_w, conv_b, w1, b1, f1, w2, b2, f2, w3, skip, rope, dft, residual, *,
             seq, ctx_len):
    n_rows = h.shape[0]
    tm = _token_tile(n_rows)
    p = mm([(h, w_in.astype(BF16))], F32, tm, 512, name="ab_in_proj")
    qkv_w = 2 * RET_QK + RET_V
    n_qk = 2 * RET_QK // LANE
    scales = (1.0,) * (RET_QK // LANE) + (RET_DK ** -0.5,) * (RET_QK // LANE) + (1.0,) * (RET_V // LANE)
    qkv = rope_cast(p, rope[0], rope[1], width=qkv_w, n_rot_blocks=n_qk, head_dim=RET_DK, scales=scales,
                    out_dtype=F32)
    log_g = jax.nn.log_sigmoid(decay_logit.astype(F32))
    ret = retention(qkv, p, log_g, jnp.exp(RET_CHUNK * log_g), seq=seq)
    v, x1, x2 = shortconv(p, conv_w, conv_b, seq=seq)
    filt = (w1, b1, f1, w2, b2, f2, w3)
    hy_x = long_conv_two_stage(dft["x"], hyena_filter_taps_2d(seq, *filt), v, x1, x2, skip, seq)
    hy_c = long_conv_one_stage(dft["c"], hyena_filter_taps(ctx_len, *filt), v[seq:], x1[seq:], x2[seq:], skip)
    hy = jnp.concatenate([hy_x, hy_c], axis=0)
    w_out = w_out.astype(BF16)
    return mm([(ret, w_out[:RET_V]), (hy, w_out[RET_V:])], F32, tm, 512, residual=residual, name="ab_out_proj")


def mixer_da(h, w_in, w_out, lam, subln, lambda_init, rope, residual, *, seq, ctx_len):
    n_rows = h.shape[0]
    tm = _token_tile(n_rows)
    p = mm([(h, w_in.astype(BF16))], F32, tm, 512, name="da_in_proj")
    qt, k, vt = rope_da(p, rope[0], rope[1])
    lam_f = lam.astype(F32)
    lam_full = jnp.exp(jnp.sum(lam_f[0] * lam_f[1])) - jnp.exp(jnp.sum(lam_f[2] * lam_f[3])) + lambda_init
    o = diff_attention(qt, k, vt, lam_full, subln, seq=seq, ctx_len=ctx_len, lambda_init=lambda_init)
    return mm([(o, w_out.astype(BF16))], F32, tm, 512, residual=residual, name="da_out_proj")


def kernel(x, c, ctx, c_ctx, w_ada, b_ada, norm_mix, norm_ffn, ab_w_in, ab_w_out, ret_decay_logit, hy_conv_w, hy_conv_b, hy_w1, hy_b1, hy_freq1, hy_w2, hy_b2, hy_freq2, hy_w3, hy_skip, da_w_in, da_w_out, da_lambda, da_subln, router_w, router_b, exp_w_gate, exp_w_up, exp_w_down, sh_w_gate, sh_w_up, sh_w_down, norm_final):
    batch, seq, d = x.shape
    ctx_len = ctx.shape[1]
    assert batch == 1 and seq % ROW_TILE == 0 and ctx_len == ROW_TILE
    depth = w_ada.shape[0]
    n_rows = seq + ctx_len

    xs = jnp.concatenate([x[0], ctx[0]], axis=0)
    cv = jnp.zeros((SUBLANE, d), F32).at[0].set(c_ctx).at[1].set(c[0])
    mods = adaln(cv, w_ada, b_ada)[:, :2].reshape(depth, 2, 6, d)

    rope_ret = rope_tables(seq, ctx_len, RET_DK)
    rope_da = rope_tables(seq, ctx_len, DA_HEAD_DIM)
    dft = dict(x=dft_tables_two_stage(2 * seq), c=dft_tables_one_stage(2 * ctx_len))
    common = dict(n_rows=n_rows, seq=seq)

    for i in range(depth):
        j = i // 2
        (h,) = norm_mod(xs, norm_mix[i], mods=mods[i], shift_idx=0, scale_idx=1, **common)
        residual = (xs, mods[i], 2, seq)
        if i % 2 == 0:
            xs = mixer_ab(h, ab_w_in[j], ab_w_out[j], ret_decay_logit[j], hy_conv_w[j], hy_conv_b[j], hy_w1[j],
                          hy_b1[j], hy_freq1[j], hy_w2[j], hy_b2[j], hy_freq2[j], hy_w3[j], hy_skip[j], rope_ret, dft,
                          residual, seq=seq, ctx_len=ctx_len)
        else:
            lambda_init = 0.8 - 0.6 * math.exp(-0.3 * i)
            xs = mixer_da(h, da_w_in[j], da_w_out[j], da_lambda[j], da_subln[j], lambda_init, rope_da, residual,
                          seq=seq, ctx_len=ctx_len)
        h, logits_t, h_packed = norm_mod(xs, norm_ffn[i], mods=mods[i], shift_idx=3, scale_idx=4,
                                         router_wt=router_w[i].T, **common)
        xs = moe(h, h_packed, logits_t, router_b[i], exp_w_gate, exp_w_up, exp_w_down, i,
                 sh_w_gate[i], sh_w_up[i], sh_w_down[i], (xs, mods[i], 5, seq))
    (out,) = norm_mod(xs, norm_final, n_rows=seq, seq=seq, out_dtype=F32)
    return out[None]
```

```python
import functools
import math

import jax
import jax.numpy as jnp
from jax import lax
from jax.experimental import pallas as pl
from jax.experimental.pallas import tpu as pltpu
from jax.experimental.pallas import tpu_sc as plsc

F32 = jnp.float32
BF16 = jnp.bfloat16
HIGHEST = lax.Precision.HIGHEST

D_MODEL = 1024
DEPTH = 4
GRID_W = 64
EPS = 1e-6
ROPE_BASE = 10000.0

RET_HEADS = 4
RET_DK = 128
RET_DV = 256
RET_CHUNK = 128
RET_STEP_CHUNKS = 2
RET_QK = RET_HEADS * RET_DK
RET_V = RET_HEADS * RET_DV

HY_WIDTH = 512
HY_ORDER = 2
HY_BANDS = 16
HY_EMB = 2 * HY_BANDS + 1
HY_FFN = 64
HY_DECAY_TARGET = 1e-2
HY_FAST_DECAY = 0.3
HY_SLOW_DECAY = 1.5
HY_ZCOLS = 64
HY_VALID_COL = HY_EMB
FFT_N2 = 128

AB_IN = 2 * RET_QK + 2 * RET_V + (HY_ORDER + 1) * HY_WIDTH
AB_CAT = RET_V + HY_WIDTH

DA_HEADS = 8
DA_HEAD_DIM = 64
DA_WIDTH = DA_HEADS * 2 * DA_HEAD_DIM

N_EXPERTS = 64
TOP_K = 8
N_GROUPS = 8
TOPK_GROUPS = 4
GROUP_SIZE = N_EXPERTS // N_GROUPS
EXPERT_DIM = 256
ROUTED_SCALE = 2.5
MOE_BLOCK = 512
MOE_SUB_BLOCKS = 2
SC_CORES = 2
SC_SUBCORES = 16
SC_WORKERS = SC_CORES * SC_SUBCORES
SC_WINDOW = 64

LANE = 128
SUBLANE = 8
ROW_TILE = 256
MAX_TOKEN_TILE = 1280
VMEM_LIMIT = 48 * 1024 * 1024
FLASH_VMEM_LIMIT = 56 * 1024 * 1024
NEG_BIG = -1e30
LOG2_E = 1.4426950408889634
FLASH_ONES_ROWS = 16
FLASH_INIT_KEYS = 16
FLASH_LAZY_HEADROOM = 60.0


def _params(sem):
    return pltpu.CompilerParams(dimension_semantics=sem, vmem_limit_bytes=VMEM_LIMIT)


def _token_tile(n):
    best = ROW_TILE
    t = ROW_TILE
    while t <= min(n, MAX_TOKEN_TILE):
        if n % t == 0:
            best = t
        t += ROW_TILE
    return best


def _row_mod(mods_ref, idx, row0, n, seq):
    row = row0 + lax.broadcasted_iota(jnp.int32, (n, 1), 0)
    return jnp.where(row >= seq, mods_ref[0, idx:idx + 1, :], mods_ref[1, idx:idx + 1, :])


def _mm_kernel(*refs, n_pairs, has_epi, gate_idx, seq):
    acc = None
    for p in range(n_pairs):
        a = refs[2 * p][...].astype(BF16)
        b = refs[2 * p + 1][...].astype(BF16)
        d = jnp.dot(a, b, preferred_element_type=F32)
        acc = d if acc is None else acc + d
    idx = 2 * n_pairs
    if has_epi:
        acc = refs[idx][...] * (acc + refs[idx + 1][...] * refs[idx + 2][...])
        idx += 3
    if gate_idx is not None:
        tm = acc.shape[0]
        acc = refs[idx][...] + _row_mod(refs[idx + 1], gate_idx, pl.program_id(0) * tm, tm, seq) * acc
        idx += 2
    o_ref = refs[idx]
    o_ref[...] = acc.astype(o_ref.dtype)


def mm(pairs, out_dtype, tm, tn, epi=None, residual=None, name="mm"):
    m = pairs[0][0].shape[0]
    n = pairs[0][1].shape[1]
    assert m % tm == 0 and n % tn == 0
    in_specs, args = [], []
    for a, b in pairs:
        k = a.shape[1]
        in_specs += [pl.BlockSpec((tm, k), lambda i, j: (i, 0)), pl.BlockSpec((k, tn), lambda i, j: (0, j))]
        args += [a, b]
    if epi is not None:
        in_specs += [pl.BlockSpec((tm, tn), lambda i, j: (i, j)), pl.BlockSpec((1, tn), lambda i, j: (0, j)),
                     pl.BlockSpec((tm, tn), lambda i, j: (i, j))]
        args += list(epi)
    gate_idx = seq = None
    if residual is not None:
        res, mods, gate_idx, seq = residual
        in_specs += [pl.BlockSpec((tm, tn), lambda i, j: (i, j)), pl.BlockSpec((2, 6, tn), lambda i, j: (0, 0, j))]
        args += [res, mods]
    return pl.pallas_call(
        functools.partial(_mm_kernel, n_pairs=len(pairs), has_epi=epi is not None, gate_idx=gate_idx, seq=seq),
        grid=(m // tm, n // tn),
        in_specs=in_specs,
        out_specs=pl.BlockSpec((tm, tn), lambda i, j: (i, j)),
        out_shape=jax.ShapeDtypeStruct((m, n), out_dtype),
        compiler_params=_params(("parallel", "parallel")),
        name=name,
    )(*args)


def _adaln_kernel(cv_ref, w_ref, b_ref, o_ref):
    cv = cv_ref[...]
    s = cv * jax.nn.sigmoid(cv)
    o_ref[0] = jnp.dot(s, w_ref[0], precision=HIGHEST, preferred_element_type=F32) + b_ref[0]


def adaln(cv, w_ada, b_ada):
    depth, d, n = w_ada.shape
    tn = 1536
    return pl.pallas_call(
        _adaln_kernel,
        grid=(depth, n // tn),
        in_specs=[pl.BlockSpec((SUBLANE, d), lambda l, j: (0, 0)),
                  pl.BlockSpec((1, d, tn), lambda l, j: (l, 0, j)),
                  pl.BlockSpec((1, 1, tn), lambda l, j: (l, 0, j))],
        out_specs=pl.BlockSpec((1, SUBLANE, tn), lambda l, j: (l, 0, j)),
        out_shape=jax.ShapeDtypeStruct((depth, SUBLANE, n), F32),
        compiler_params=_params(("parallel", "parallel")),
        name="adaln",
    )(cv, w_ada, b_ada.reshape(depth, 1, n))


def _norm_mod_kernel(*refs, shift_idx, scale_idx, has_router, seq):
    it = iter(refs)
    x = next(it)[...]
    mods_ref = next(it) if shift_idx is not None else None
    g_ref = next(it)
    wr_ref = next(it) if has_router else None
    h_ref = next(it)
    y = x * lax.rsqrt(jnp.mean(x * x, axis=-1, keepdims=True) + EPS) * g_ref[...]
    if shift_idx is not None:
        tm = x.shape[0]
        row0 = pl.program_id(0) * tm
        y = y * (1.0 + _row_mod(mods_ref, scale_idx, row0, tm, seq)) + _row_mod(mods_ref, shift_idx, row0, tm, seq)
    h_ref[...] = y.astype(h_ref.dtype)
    if has_router:
        lg_ref = next(it)
        lg_ref[...] = lax.dot_general(wr_ref[...], y, (((1,), (1,)), ((), ())),
                                      precision=HIGHEST, preferred_element_type=F32)
        half = y.shape[1] // 2
        next(it)[...] = _pack_bf16_pair(y[:, :half], y[:, half:])


def _pack_bf16_pair(a, b):
    bits = lambda x: lax.bitcast_convert_type(x.astype(BF16), jnp.uint16).astype(jnp.int32)
    return bits(a) | lax.shift_left(bits(b), 16)


def _unpack_bf16_pair(w):
    return (lax.bitcast_convert_type(lax.shift_left(w, 16), F32),
            lax.bitcast_convert_type(w & -65536, F32))


def norm_mod(xs, g, *, n_rows, seq, mods=None, shift_idx=None, scale_idx=None, router_wt=None, out_dtype=BF16):
    d = xs.shape[1]
    tm = _token_tile(n_rows)
    row = pl.BlockSpec((tm, d), lambda i: (i, 0))
    in_specs, args = [row], [xs]
    if shift_idx is not None:
        in_specs.append(pl.BlockSpec((2, 6, d), lambda i: (0, 0, 0)))
        args.append(mods)
    in_specs.append(pl.BlockSpec((1, d), lambda i: (0, 0)))
    args.append(g.reshape(1, d))
    has_router = router_wt is not None
    if has_router:
        in_specs.append(pl.BlockSpec(router_wt.shape, lambda i: (0, 0)))
        args.append(router_wt)
    out_specs = [row]
    out_shape = [jax.ShapeDtypeStruct((n_rows, d), out_dtype)]
    if has_router:
        out_specs.append(pl.BlockSpec((N_EXPERTS, tm), lambda i: (0, i)))
        out_shape.append(jax.ShapeDtypeStruct((N_EXPERTS, n_rows), F32))
        out_specs.append(pl.BlockSpec((tm, d // 2), lambda i: (i, 0)))
        out_shape.append(jax.ShapeDtypeStruct((n_rows, d // 2), jnp.int32))
    return pl.pallas_call(
        functools.partial(_norm_mod_kernel, shift_idx=shift_idx, scale_idx=scale_idx, has_router=has_router, seq=seq),
        grid=(n_rows // tm,),
        in_specs=in_specs,
        out_specs=out_specs,
        out_shape=out_shape,
        compiler_params=_params(("parallel",)),
        name="norm_mod",
    )(*args)


def _rope_kernel(p_ref, cos_ref, sin_ref, o_ref, *, n_rot_blocks, head_dim, scales):
    cos = cos_ref[...]
    sin = sin_ref[...]
    for b in range(len(scales)):
        x = p_ref[:, b * LANE:(b + 1) * LANE]
        if b < n_rot_blocks:
            if head_dim == LANE:
                rot = pltpu.roll(x, LANE // 2, 1)
            else:
                lane = lax.broadcasted_iota(jnp.int32, x.shape, 1)
                first_half = (lane % head_dim) < head_dim // 2
                rot = jnp.where(first_half, pltpu.roll(x, LANE - head_dim // 2, 1), pltpu.roll(x, head_dim // 2, 1))
            x = x * cos + rot * sin
        if scales[b] != 1.0:
            x = x * scales[b]
        o_ref[:, b * LANE:(b + 1) * LANE] = x.astype(o_ref.dtype)


def rope_cast(p, cos, sin, *, width, n_rot_blocks, head_dim, scales, out_dtype):
    n_rows = p.shape[0]
    return pl.pallas_call(
        functools.partial(_rope_kernel, n_rot_blocks=n_rot_blocks, head_dim=head_dim, scales=scales),
        grid=(n_rows // ROW_TILE,),
        in_specs=[pl.BlockSpec((ROW_TILE, width), lambda i: (i, 0)),
                  pl.BlockSpec((ROW_TILE, LANE), lambda i: (i, 0)),
                  pl.BlockSpec((ROW_TILE, LANE), lambda i: (i, 0))],
        out_specs=pl.BlockSpec((ROW_TILE, width), lambda i: (i, 0)),
        out_shape=jax.ShapeDtypeStruct((n_rows, width), out_dtype),
        compiler_params=_params(("parallel",)),
        name="rope_cast",
    )(p, cos, sin)


def rope_tables(seq, ctx_len, head_dim):
    n_freq = head_dim // 4
    inv = ROPE_BASE ** (-jnp.arange(n_freq, dtype=F32) / n_freq)
    rows = seq // GRID_W
    row = jnp.repeat(jnp.arange(rows, dtype=F32), GRID_W)
    col = jnp.tile(jnp.arange(GRID_W, dtype=F32), rows)
    ang = jnp.concatenate([row[:, None] * inv, col[:, None] * inv], axis=-1)
    cos, sin = jnp.cos(ang), jnp.sin(ang)
    cos = jnp.concatenate([cos, cos], axis=-1)
    sin = jnp.concatenate([-sin, sin], axis=-1)
    reps = LANE // head_dim
    cos, sin = jnp.tile(cos, (1, reps)), jnp.tile(sin, (1, reps))
    cos = jnp.concatenate([cos, jnp.ones((ctx_len, LANE), F32)], axis=0)
    sin = jnp.concatenate([sin, jnp.zeros((ctx_len, LANE), F32)], axis=0)
    return cos, sin


def _ret_kernel(lg_ref, gc_ref, q_ref, k_ref, v_ref, *rest, reverse):
    if reverse:
        yf_ref, gate_ref, o_ref, s_ref = rest
    else:
        o_ref, s_ref = rest
    c = RET_CHUNK

    @pl.when(pl.program_id(0) == 0)
    def _():
        s_ref[...] = jnp.zeros_like(s_ref)

    ii = lax.broadcasted_iota(jnp.int32, (c, c), 0)
    jj = lax.broadcasted_iota(jnp.int32, (c, c), 1)
    rel = ((jj - ii) if reverse else (ii - jj)).astype(F32)
    pos = lax.broadcasted_iota(jnp.int32, (c, 1), 0).astype(F32)
    for h in range(RET_HEADS):
        lg = lg_ref[h]
        dec = jnp.where(rel >= 0, jnp.exp(jnp.maximum(rel, 0.0) * lg), 0.0)
        if reverse:
            q_dec = jnp.exp((c - pos) * lg)
            k_dec = jnp.exp(pos * lg)
        else:
            q_dec = jnp.exp((pos + 1.0) * lg)
            k_dec = jnp.exp((c - 1.0 - pos) * lg)
        state = s_ref[h]
        qk_cols = slice(h * RET_DK, (h + 1) * RET_DK)
        v_cols = slice(h * RET_DV, (h + 1) * RET_DV)
        for sub in (range(RET_STEP_CHUNKS - 1, -1, -1) if reverse else range(RET_STEP_CHUNKS)):
            rows = slice(sub * c, (sub + 1) * c)
            q = q_ref[rows, qk_cols]
            k = k_ref[rows, qk_cols]
            v = v_ref[rows, v_cols].astype(BF16)
            s = lax.dot_general(q.astype(BF16), k.astype(BF16), (((1,), (1,)), ((), ())),
                                preferred_element_type=F32) * dec
            y = jnp.dot(s.astype(BF16), v, preferred_element_type=F32)
            y = y + jnp.dot((q * q_dec).astype(BF16), state.astype(BF16), preferred_element_type=F32)
            upd = lax.dot_general((k * k_dec).astype(BF16), v, (((0,), (0,)), ((), ())), preferred_element_type=F32)
            state = gc_ref[h] * state + upd
            if reverse:
                r = y + yf_ref[rows, v_cols]
                mu = jnp.mean(r, axis=-1, keepdims=True)
                rc = r - mu
                var = jnp.mean(rc * rc, axis=-1, keepdims=True)
                g = gate_ref[rows, v_cols]
                o_ref[rows, v_cols] = (rc * lax.rsqrt(var + EPS) * (g * jax.nn.sigmoid(g))).astype(o_ref.dtype)
            else:
                o_ref[rows, v_cols] = y
        s_ref[h] = state


def retention(qkv, p, log_g, g_chunk, *, seq):
    n_rows = qkv.shape[0]
    step = RET_STEP_CHUNKS * RET_CHUNK
    assert seq % step == 0 and n_rows % step == 0
    n_steps = n_rows // step
    n_x = seq // step
    smem = pl.BlockSpec(memory_space=pltpu.SMEM)

    def run(reverse, extra):
        if reverse:
            idx = lambda t: n_steps - 1 - t
        else:
            idx = lambda t: (t + n_x) % n_steps
        in_specs = [smem, smem,
                    pl.BlockSpec((step, RET_QK), lambda t: (idx(t), 0)),
                    pl.BlockSpec((step, RET_QK), lambda t: (idx(t), 1)),
                    pl.BlockSpec((step, RET_V), lambda t: (idx(t), 1))]
        args = [log_g[1 if reverse else 0], g_chunk[1 if reverse else 0], qkv, qkv, qkv]
        if reverse:
            in_specs += [pl.BlockSpec((step, RET_V), lambda t: (idx(t), 0)),
                         pl.BlockSpec((step, RET_V), lambda t: (idx(t), 2))]
            args += list(extra)
        return pl.pallas_call(
            functools.partial(_ret_kernel, reverse=reverse),
            grid=(n_steps,),
            in_specs=in_specs,
            out_specs=pl.BlockSpec((step, RET_V), lambda t: (idx(t), 0)),
            out_shape=jax.ShapeDtypeStruct((n_rows, RET_V), BF16 if reverse else F32),
            scratch_shapes=[pltpu.VMEM((RET_HEADS, RET_DK, RET_DV), F32)],
            compiler_params=_params(("arbitrary",)),
            name="retention_bwd" if reverse else "retention_fwd",
        )(*args)

    y_fwd = run(False, None)
    return run(True, (y_fwd, p))


def _shortconv_kernel(cur_ref, prev_ref, next_ref, w_ref, b_ref, v_ref, x1_ref, x2_ref, *, x_tiles):
    i = pl.program_id(0)
    cur = cur_ref[...]
    rows = cur.shape[0]
    row = lax.broadcasted_iota(jnp.int32, (rows, 1), 0)
    has_prev = jnp.where((i == 0) | (i == x_tiles), 0.0, 1.0)
    has_next = jnp.where((i == x_tiles - 1) | (i == x_tiles), 0.0, 1.0)
    up = jnp.where(row == 0, prev_ref[SUBLANE - 1:SUBLANE, :] * has_prev, pltpu.roll(cur, 1, 0))
    dn = jnp.where(row == rows - 1, next_ref[0:1, :] * has_next, pltpu.roll(cur, rows - 1, 0))
    y = up * w_ref[0:1, :] + cur * w_ref[1:2, :] + dn * w_ref[2:3, :] + b_ref[...]
    v_ref[...] = y[:, :HY_WIDTH]
    x1_ref[...] = y[:, HY_WIDTH:2 * HY_WIDTH]
    x2_ref[...] = y[:, 2 * HY_WIDTH:]


def shortconv(p, w, b, *, seq):
    n_rows = p.shape[0]
    width = 3 * HY_WIDTH
    col = p.shape[1] // width - 1
    per = ROW_TILE // SUBLANE
    last = n_rows // SUBLANE - 1
    out = jax.ShapeDtypeStruct((n_rows, HY_WIDTH), F32)
    ospec = pl.BlockSpec((ROW_TILE, HY_WIDTH), lambda i: (i, 0))
    return pl.pallas_call(
        functools.partial(_shortconv_kernel, x_tiles=seq // ROW_TILE),
        grid=(n_rows // ROW_TILE,),
        in_specs=[pl.BlockSpec((ROW_TILE, width), lambda i: (i, col)),
                  pl.BlockSpec((SUBLANE, width), lambda i: (jnp.maximum(i * per - 1, 0), col)),
                  pl.BlockSpec((SUBLANE, width), lambda i: (jnp.minimum((i + 1) * per, last), col)),
                  pl.BlockSpec((3, width), lambda i: (0, 0)),
                  pl.BlockSpec((1, width), lambda i: (0, 0))],
        out_specs=[ospec, ospec, ospec],
        out_shape=[out, out, out],
        compiler_params=_params(("parallel",)),
        name="shortconv",
    )(p, p, p, w, b.reshape(1, width))


def _filt_kernel(z_ref, w1_ref, b1_ref, f1_ref, w2_ref, b2_ref, f2_ref, w3a_ref, w3b_ref, dl_ref, *o_ref):
    z = z_ref[...]
    h = jnp.sin(f1_ref[...] * (jnp.dot(z, w1_ref[...], precision=HIGHEST, preferred_element_type=F32) + b1_ref[...]))
    h = jnp.sin(f2_ref[...] * (jnp.dot(h, w2_ref[...], precision=HIGHEST, preferred_element_type=F32) + b2_ref[...]))
    window = jnp.exp(-z[:, 0:1] * dl_ref[...]) * z[:, HY_VALID_COL:HY_VALID_COL + 1]
    for o, w3_ref in enumerate((w3a_ref, w3b_ref)):
        o_ref[o][...] = jnp.dot(h, w3_ref[...], precision=HIGHEST, preferred_element_type=F32) * window


def hyena_filter_taps(length, w1, b1, f1, w2, b2, f2, w3):
    z = _filter_positions(length)
    w1p = jnp.zeros((HY_ZCOLS, HY_FFN), F32).at[:HY_EMB].set(w1)
    deltas = jnp.abs(jnp.linspace(math.log(HY_DECAY_TARGET) / HY_SLOW_DECAY,
                                  math.log(HY_DECAY_TARGET) / HY_FAST_DECAY, HY_WIDTH, dtype=F32)).reshape(1, HY_WIDTH)
    tm = min(length, 512)
    half_tiles = length // tm
    vec = lambda a: a.reshape(1, HY_FFN)
    small = lambda shape: pl.BlockSpec(shape, lambda i: (0, 0))
    w3_spec = lambda o: pl.BlockSpec((HY_FFN, HY_WIDTH), lambda i: (0, 2 * o + jnp.where(i >= half_tiles, 1, 0)))
    assert HY_ORDER == 2
    return pl.pallas_call(
        _filt_kernel,
        grid=(2 * half_tiles,),
        in_specs=[pl.BlockSpec((tm, HY_ZCOLS), lambda i: (i, 0)),
                  small((HY_ZCOLS, HY_FFN)), small((1, HY_FFN)), small((1, HY_FFN)),
                  small((HY_FFN, HY_FFN)), small((1, HY_FFN)), small((1, HY_FFN)),
                  w3_spec(0), w3_spec(1), small((1, HY_WIDTH))],
        out_specs=[pl.BlockSpec((tm, HY_WIDTH), lambda i: (i, 0))] * HY_ORDER,
        out_shape=[jax.ShapeDtypeStruct((2 * length, HY_WIDTH), F32)] * HY_ORDER,
        compiler_params=_params(("parallel",)),
        name="hyena_filter",
    )(z, w1p, vec(b1), vec(f1), w2, vec(b2), vec(f2), w3, w3, deltas)


def _filt2d_kernel(z_ref, w1_ref, b1_ref, f1_ref, w2_ref, b2_ref, f2_ref, w3fa_ref, w3ba_ref, w3fb_ref, w3bb_ref,
                   dl_ref, oa_ref, ob_ref):
    n1 = oa_ref.shape[0]
    c = dl_ref.shape[1]
    z = z_ref[...]
    h = jnp.sin(f1_ref[...] * (jnp.dot(z, w1_ref[...], precision=HIGHEST, preferred_element_type=F32) + b1_ref[...]))
    h = jnp.sin(f2_ref[...] * (jnp.dot(h, w2_ref[...], precision=HIGHEST, preferred_element_type=F32) + b2_ref[...]))
    window = jnp.exp(-z[:, 0:1] * dl_ref[...]) * z[:, HY_VALID_COL:HY_VALID_COL + 1]
    for o_ref, wf_ref, wb_ref in ((oa_ref, w3fa_ref, w3ba_ref), (ob_ref, w3fb_ref, w3bb_ref)):
        for j in range(SUBLANE):
            rows = slice(j * n1, (j + 1) * n1)
            hj = h[rows]
            taps = jnp.concatenate(
                [jnp.dot(hj[:n1 // 2], wf_ref[...], precision=HIGHEST, preferred_element_type=F32),
                 jnp.dot(hj[n1 // 2:], wb_ref[...], precision=HIGHEST, preferred_element_type=F32)], axis=0)
            o_ref[:, j * c:(j + 1) * c] = taps * window[rows]


def hyena_filter_taps_2d(length, w1, b1, f1, w2, b2, f2, w3):
    n1 = 2 * length // FFT_N2
    groups = FFT_N2 // SUBLANE
    z = _filter_positions(length).reshape(n1, groups, SUBLANE, HY_ZCOLS).transpose(1, 2, 0, 3)
    z = z.reshape(2 * length, HY_ZCOLS)
    w1p = jnp.zeros((HY_ZCOLS, HY_FFN), F32).at[:HY_EMB].set(w1)
    deltas = jnp.abs(jnp.linspace(math.log(HY_DECAY_TARGET) / HY_SLOW_DECAY,
                                  math.log(HY_DECAY_TARGET) / HY_FAST_DECAY, HY_WIDTH, dtype=F32)).reshape(1, HY_WIDTH)
    vec = lambda a: a.reshape(1, HY_FFN)
    small = lambda shape: pl.BlockSpec(shape, lambda i: (0, 0))
    w3_spec = lambda col: pl.BlockSpec((HY_FFN, HY_WIDTH), lambda i: (0, col))
    assert HY_ORDER == 2
    out = jax.ShapeDtypeStruct((n1, FFT_N2 * HY_WIDTH), F32)
    return pl.pallas_call(
        _filt2d_kernel,
        grid=(groups,),
        in_specs=[pl.BlockSpec((SUBLANE * n1, HY_ZCOLS), lambda i: (i, 0)),
                  small((HY_ZCOLS, HY_FFN)), small((1, HY_FFN)), small((1, HY_FFN)),
                  small((HY_FFN, HY_FFN)), small((1, HY_FFN)), small((1, HY_FFN)),
                  w3_spec(0), w3_spec(1), w3_spec(2), w3_spec(3), small((1, HY_WIDTH))],
        out_specs=[pl.BlockSpec((n1, SUBLANE * HY_WIDTH), lambda i: (0, i))] * HY_ORDER,
        out_shape=[out] * HY_ORDER,
        compiler_params=_params(("parallel",)),
        name="hyena_filter",
    )(z, w1p, vec(b1), vec(f1), w2, vec(b2), vec(f2), w3, w3, w3, w3, deltas)


def _filter_positions(length):
    t = jnp.concatenate([jnp.arange(length, dtype=F32), float(length) - jnp.arange(length, dtype=F32)])
    valid = jnp.ones((2 * length,), F32).at[length].set(0.0)
    t_norm = t / max(length - 1, 1)
    bands = jnp.linspace(1e-4, HY_BANDS - 1, HY_BANDS, dtype=F32)
    ang = (2.0 * math.pi / length) * t[:, None] * bands[None, :]
    z = jnp.concatenate([t_norm[:, None], jnp.cos(ang), -jnp.sin(ang), valid[:, None]], axis=-1)
    return jnp.pad(z, ((0, 0), (0, HY_ZCOLS - z.shape[1])))


def _angles(num, den):
    return (2.0 * math.pi / den) * (num % den).astype(F32)


def dft_tables_two_stage(m):
    n2 = FFT_N2
    n1 = m // n2
    half = n1 // 2
    kp = -(-(half + 1) // SUBLANE) * SUBLANE
    k1 = jnp.arange(kp, dtype=jnp.int32)
    live = (k1 <= half)
    a1 = _angles(k1[:, None] * jnp.arange(n1, dtype=jnp.int32)[None, :], n1)
    f1 = jnp.concatenate([jnp.where(live[:, None], jnp.cos(a1), 0.0), jnp.where(live[:, None], -jnp.sin(a1), 0.0)], 0)
    wgt = jnp.where((k1 == 0) | (k1 == half), 1.0, 2.0) * live / m
    a1h = a1[:, :half].T
    cinv = jnp.concatenate([jnp.cos(a1h) * wgt[None, :], -jnp.sin(a1h) * wgt[None, :]], axis=1)
    k = k1[:, None, None] + n1 * jnp.arange(n2, dtype=jnp.int32)[None, :, None]
    th = _angles(k * jnp.arange(n2, dtype=jnp.int32)[None, None, :], m)
    c = jnp.where(live[:, None, None], jnp.cos(th), 0.0)
    s = jnp.where(live[:, None, None], jnp.sin(th), 0.0)
    g_fwd = jnp.concatenate([jnp.concatenate([c, s], 2), jnp.concatenate([-s, c], 2)], 1)
    ct, st = jnp.swapaxes(c, 1, 2), jnp.swapaxes(s, 1, 2)
    g_inv = jnp.concatenate([jnp.concatenate([ct, -st], 2), jnp.concatenate([st, ct], 2)], 1)
    return dict(n1=n1, kp=kp, f1=f1.astype(BF16), f1_half=f1[:, :half].astype(BF16), cinv=cinv.astype(BF16),
                g_fwd=g_fwd.astype(BF16), g_inv=g_inv.astype(BF16))


def dft_tables_one_stage(m):
    half = m // 2
    kp = -(-(half + 1) // SUBLANE) * SUBLANE
    k = jnp.arange(kp, dtype=jnp.int32)
    live = (k <= half)
    a = _angles(k[:, None] * jnp.arange(m, dtype=jnp.int32)[None, :], m)
    f = jnp.concatenate([jnp.where(live[:, None], jnp.cos(a), 0.0), jnp.where(live[:, None], -jnp.sin(a), 0.0)], 0)
    wgt = jnp.where((k == 0) | (k == half), 1.0, 2.0) * live / m
    ah = a[:, :half].T
    cinv = jnp.concatenate([jnp.cos(ah) * wgt[None, :], -jnp.sin(ah) * wgt[None, :]], axis=1)
    return dict(kp=kp, f=f.astype(BF16), f_half=f[:, :half].astype(BF16), cinv=cinv.astype(BF16))


def _bmm_kernel(*refs, kb, in_part_major, out_part_major, has_h):
    if has_h:
        g_ref, a_ref, h_ref, o_ref = refs
    else:
        g_ref, a_ref, o_ref = refs
    n2 = FFT_N2
    for b in range(kb):
        if in_part_major:
            ar, ai = a_ref[0, b], a_ref[1, b]
        else:
            ar, ai = a_ref[b, 0], a_ref[b, 1]
        if has_h:
            hr, hi = h_ref[b, 0], h_ref[b, 1]
            ar, ai = ar * hr - ai * hi, ar * hi + ai * hr
        xin = jnp.concatenate([ar, ai], axis=0).astype(BF16)
        y = jnp.dot(g_ref[b], xin, preferred_element_type=F32)
        if out_part_major:
            o_ref[0, b] = y[:n2].astype(o_ref.dtype)
            o_ref[1, b] = y[n2:].astype(o_ref.dtype)
        else:
            o_ref[b, 0] = y[:n2].astype(o_ref.dtype)
            o_ref[b, 1] = y[n2:].astype(o_ref.dtype)


def bmm_k1(g, a, h=None, *, in_part_major, out_part_major):
    kp = g.shape[0]
    n2 = FFT_N2
    c = a.shape[-1]
    kb, tc = SUBLANE, min(c, 512)
    pm = lambda: pl.BlockSpec((2, kb, n2, tc), lambda i, j: (0, i, 0, j))
    km = lambda: pl.BlockSpec((kb, 2, n2, tc), lambda i, j: (i, 0, 0, j))
    in_specs = [pl.BlockSpec((kb, 2 * n2, 2 * n2), lambda i, j: (i, 0, 0)), pm() if in_part_major else km()]
    args = [g, a]
    if h is not None:
        in_specs.append(km())
        args.append(h)
    return pl.pallas_call(
        functools.partial(_bmm_kernel, kb=kb, in_part_major=in_part_major, out_part_major=out_part_major,
                          has_h=h is not None),
        grid=(kp // kb, c // tc),
        in_specs=in_specs,
        out_specs=pm() if out_part_major else km(),
        out_shape=jax.ShapeDtypeStruct((2, kp, n2, c), BF16) if out_part_major else
        jax.ShapeDtypeStruct((kp, 2, n2, c), F32),
        compiler_params=_params(("parallel", "parallel")),
        name="dft_inner",
    )(*args)


def _cmul_kernel(x_ref, h_ref, o_ref):
    xr, xi, hr, hi = x_ref[0], x_ref[1], h_ref[0], h_ref[1]
    o_ref[0] = xr * hr - xi * hi
    o_ref[1] = xr * hi + xi * hr


def cmul(x, h):
    spec = pl.BlockSpec(x.shape, lambda i: (0, 0, 0))
    return pl.pallas_call(_cmul_kernel, grid=(1,), in_specs=[spec, spec], out_specs=spec,
                          out_shape=jax.ShapeDtypeStruct(x.shape, F32), compiler_params=_params(("arbitrary",)),
                          name="spectrum_product")(x, h)


def _dft_outer3_kernel(f_ref, x_ref, o_ref):
    c = x_ref.shape[2]
    f = f_ref[...]
    for j in range(SUBLANE):
        o_ref[:, j * c:(j + 1) * c] = jnp.dot(f, x_ref[:, j, :].astype(BF16),
                                              preferred_element_type=F32).astype(o_ref.dtype)


def dft_outer3(f, x3, n_outer):
    rows = f.shape[0]
    c = x3.shape[2]
    return pl.pallas_call(
        _dft_outer3_kernel,
        grid=(FFT_N2 // SUBLANE,),
        in_specs=[pl.BlockSpec((rows, n_outer), lambda j: (0, 0)),
                  pl.BlockSpec((n_outer, SUBLANE, c), lambda j: (0, j, 0))],
        out_specs=pl.BlockSpec((rows, SUBLANE * c), lambda j: (0, j)),
        out_shape=jax.ShapeDtypeStruct((rows, FFT_N2 * c), BF16),
        compiler_params=_params(("parallel",)),
        name="dft_outer",
    )(f, x3)


def _idft_gate3_kernel(cinv_ref, b_ref, gate_ref, skip_ref, u_ref, o_ref, *, u_is_3d):
    c = gate_ref.shape[2]
    cinv = cinv_ref[...]
    for j in range(SUBLANE):
        cols = slice(j * c, (j + 1) * c)
        acc = jnp.dot(cinv, b_ref[:, cols].astype(BF16), preferred_element_type=F32)
        u = u_ref[:, j, :] if u_is_3d else u_ref[:, cols]
        o_ref[:, cols] = gate_ref[:, j, :] * (acc + skip_ref[...] * u)


def idft_gate3(cinv, b2d, gate3, skip_row, u):
    n_outer = cinv.shape[0]
    c = gate3.shape[2]
    u_is_3d = u.ndim == 3
    wide = pl.BlockSpec((n_outer, SUBLANE * c), lambda j: (0, j))
    slab = pl.BlockSpec((n_outer, SUBLANE, c), lambda j: (0, j, 0))
    return pl.pallas_call(
        functools.partial(_idft_gate3_kernel, u_is_3d=u_is_3d),
        grid=(FFT_N2 // SUBLANE,),
        in_specs=[pl.BlockSpec(cinv.shape, lambda j: (0, 0)),
                  pl.BlockSpec((b2d.shape[0], SUBLANE * c), lambda j: (0, j)),
                  slab, pl.BlockSpec((1, c), lambda j: (0, 0)), slab if u_is_3d else wide],
        out_specs=wide,
        out_shape=jax.ShapeDtypeStruct((n_outer, FFT_N2 * c), F32),
        compiler_params=_params(("parallel",)),
        name="idft_outer_gate",
    )(cinv, b2d, gate3, skip_row, u)


def long_conv_two_stage(tabs, taps, v, x1, x2, skip, length):
    c = v.shape[1]
    n2, n1, kp = FFT_N2, tabs["n1"], tabs["kp"]
    as3 = lambda a: a.reshape(a.shape[0] // n2, n2, c)
    spectrum = lambda a2d: bmm_k1(tabs["g_fwd"], a2d.reshape(2, kp, n2, c), in_part_major=True, out_part_major=False)

    spectra = [spectrum(mm([(tabs["f1"], taps[o])], BF16, 2 * kp, 2048, name="dft_outer")) for o in range(HY_ORDER)]
    v3 = as3(v)
    u = v3
    for o, gate in enumerate((x1, x2)):
        if u.ndim == 3:
            a = dft_outer3(tabs["f1_half"], u, n1 // 2)
        else:
            a = mm([(tabs["f1_half"], u)], BF16, 2 * kp, 2048, name="dft_outer")
        bt = bmm_k1(tabs["g_inv"], spectrum(a), spectra[o], in_part_major=False, out_part_major=True)
        u = idft_gate3(tabs["cinv"], bt.reshape(2 * kp, n2 * c), as3(gate), skip[o].reshape(1, c), u)
    return u.reshape(length, c)


def long_conv_one_stage(tabs, taps, v, x1, x2, skip):
    length, c = v.shape
    kp = tabs["kp"]
    u = v
    for o, gate in enumerate((x1, x2)):
        hs = mm([(tabs["f"], taps[o])], F32, 2 * kp, c, name="ctx_dft").reshape(2, kp, c)
        xs = mm([(tabs["f_half"], u)], F32, 2 * kp, c, name="ctx_dft").reshape(2, kp, c)
        ys = cmul(xs, hs).reshape(2 * kp, c)
        u = mm([(tabs["cinv"], ys)], F32, length, c, epi=(gate, skip[o].reshape(1, c), u), name="ctx_idft_gate")
    return u


def _flash_kernel(lam_ref, qt_ref, k_ref, vt_ref, sub_ref, o_ref, m_ref, excess_ref, acc_ref, *, kv, seq, ctx_len,
                  out_scale):
    i = pl.program_id(1)
    last_q = pl.num_programs(1) - 1
    tq = qt_ref.shape[1]
    d = DA_HEAD_DIM
    dv = 2 * DA_HEAD_DIM
    n_chunks = k_ref.shape[0] // kv
    acc_ref[...] = jnp.zeros_like(acc_ref)

    def scores(off, rows, c, masked):
        s = jnp.dot(k_ref[pl.ds(off, rows), c * d:(c + 1) * d], qt_ref[c * d:(c + 1) * d, :],
                    preferred_element_type=F32)
        if masked:
            key = off + lax.broadcasted_iota(jnp.int32, (rows, 1), 0)
            lane = lax.broadcasted_iota(jnp.int32, (1, tq), 1)
            s = s + jnp.where(key < seq, NEG_BIG, 0.0) * jnp.where(lane >= tq - ctx_len, 1.0, 0.0)
        return s

    def exact_step(off, c, masked):
        s = scores(off, kv, c, masked)
        m_old = m_ref[c]
        m_new = jnp.maximum(m_old, jnp.max(s, axis=0, keepdims=True))
        pr = jnp.exp2(s - m_new).astype(BF16)
        acc_ref[c] = jnp.exp2(m_old - m_new) * acc_ref[c] + jnp.dot(vt_ref[:, pl.ds(off, kv)], pr,
                                                                   preferred_element_type=F32)
        m_ref[c] = m_new

    def lazy_step(off, c, masked):
        s = scores(off, kv, c, masked)
        m_old = m_ref[c]
        m_chunk = jnp.max(s, axis=0, keepdims=True)
        pv = jnp.dot(vt_ref[:, pl.ds(off, kv)], jnp.exp2(s - m_old).astype(BF16), preferred_element_type=F32)
        m_new = jnp.maximum(m_old, m_chunk)
        acc_ref[c] = jnp.exp2(m_old - m_new) * (acc_ref[c] + pv)
        m_ref[c] = m_new
        excess_ref[c] = jnp.maximum(excess_ref[c], m_chunk - m_old)

    def all_chunks(step, masked):
        def body(kc, carry):
            off = pl.multiple_of(kc * kv, kv)
            for c in range(2):
                step(off, c, masked)
            return carry

        lax.fori_loop(0, n_chunks, body, 0)

    def run(masked):
        for c in range(2):
            m0 = jnp.max(scores(0, FLASH_INIT_KEYS, c, False), axis=0, keepdims=True)
            if masked:
                lane = lax.broadcasted_iota(jnp.int32, (1, tq), 1)
                m_ctx = jnp.max(scores(seq, FLASH_INIT_KEYS, c, False), axis=0, keepdims=True)
                m0 = jnp.where(lane >= tq - ctx_len, m_ctx, m0)
            m_ref[c] = m0
        excess_ref[...] = jnp.full_like(excess_ref, NEG_BIG)
        all_chunks(lazy_step, masked)

        @pl.when(jnp.max(excess_ref[...]) > FLASH_LAZY_HEADROOM)
        def _():
            m_ref[...] = jnp.full_like(m_ref, NEG_BIG)
            acc_ref[...] = jnp.zeros_like(acc_ref)
            all_chunks(exact_step, masked)

    @pl.when(i != last_q)
    def _():
        run(False)

    @pl.when(i == last_q)
    def _():
        run(True)

    a0 = acc_ref[0, :dv, :] / acc_ref[0, dv:dv + 1, :]
    a1 = acc_ref[1, :dv, :] / acc_ref[1, dv:dv + 1, :]
    o = (a0 - lam_ref[0] * a1).T
    o = o * lax.rsqrt(jnp.mean(o * o, axis=-1, keepdims=True) + 1e-5) * sub_ref[...]
    o_ref[...] = (o * out_scale).astype(o_ref.dtype)


def _rope_da_kernel(p_ref, cos_ref, sin_ref, qt_ref, k_ref, vt_ref):
    cos = cos_ref[...]
    sin = sin_ref[...]
    hw = 2 * DA_HEAD_DIM
    lane = lax.broadcasted_iota(jnp.int32, cos.shape, 1)
    first_half = (lane % DA_HEAD_DIM) < DA_HEAD_DIM // 2

    def rotated(b):
        x = p_ref[:, b * LANE:(b + 1) * LANE]
        rot = jnp.where(first_half, pltpu.roll(x, LANE - DA_HEAD_DIM // 2, 1), pltpu.roll(x, DA_HEAD_DIM // 2, 1))
        return x * cos + rot * sin

    ones = jnp.ones((FLASH_ONES_ROWS, cos.shape[0]), BF16)
    for h in range(DA_HEADS):
        qt_ref[h * hw:(h + 1) * hw, :] = (rotated(h) * (LOG2_E * DA_HEAD_DIM ** -0.5)).T.astype(BF16)
        k_ref[:, h * hw:(h + 1) * hw] = rotated(DA_HEADS + h).astype(BF16)
        base = h * (hw + FLASH_ONES_ROWS)
        vt_ref[base:base + hw, :] = p_ref[:, (2 * DA_HEADS + h) * LANE:(2 * DA_HEADS + h + 1) * LANE].T.astype(BF16)
        vt_ref[base + hw:base + hw + FLASH_ONES_ROWS, :] = ones


def rope_da(p, cos, sin):
    n_rows = p.shape[0]
    assert 2 * DA_HEAD_DIM == LANE
    vt_rows = DA_HEADS * (LANE + FLASH_ONES_ROWS)
    return pl.pallas_call(
        _rope_da_kernel,
        grid=(n_rows // ROW_TILE,),
        in_specs=[pl.BlockSpec((ROW_TILE, 3 * DA_WIDTH), lambda i: (i, 0)),
                  pl.BlockSpec((ROW_TILE, LANE), lambda i: (i, 0)),
                  pl.BlockSpec((ROW_TILE, LANE), lambda i: (i, 0))],
        out_specs=[pl.BlockSpec((DA_WIDTH, ROW_TILE), lambda i: (0, i)),
                   pl.BlockSpec((ROW_TILE, DA_WIDTH), lambda i: (i, 0)),
                   pl.BlockSpec((vt_rows, ROW_TILE), lambda i: (0, i))],
        out_shape=[jax.ShapeDtypeStruct((DA_WIDTH, n_rows), BF16),
                   jax.ShapeDtypeStruct((n_rows, DA_WIDTH), BF16),
                   jax.ShapeDtypeStruct((vt_rows, n_rows), BF16)],
        compiler_params=_params(("parallel",)),
        name="rope_da",
    )(p, cos, sin)


def diff_attention(qt, k, vt, lam_full, subln, *, seq, ctx_len, lambda_init):
    n_rows = k.shape[0]
    tq = _token_tile(n_rows)
    hw = 2 * DA_HEAD_DIM
    ones_rows = FLASH_ONES_ROWS
    return pl.pallas_call(
        functools.partial(_flash_kernel, kv=tq, seq=seq, ctx_len=ctx_len, out_scale=1.0 - lambda_init),
        grid=(DA_HEADS, n_rows // tq),
        in_specs=[pl.BlockSpec(memory_space=pltpu.SMEM),
                  pl.BlockSpec((hw, tq), lambda h, i: (h, i)),
                  pl.BlockSpec((n_rows, hw), lambda h, i: (0, h)),
                  pl.BlockSpec((hw + ones_rows, n_rows), lambda h, i: (h, 0)),
                  pl.BlockSpec((1, hw), lambda h, i: (0, 0))],
        out_specs=pl.BlockSpec((tq, hw), lambda h, i: (i, h)),
        out_shape=jax.ShapeDtypeStruct((n_rows, DA_WIDTH), BF16),
        scratch_shapes=[pltpu.VMEM((2, 1, tq), F32), pltpu.VMEM((2, 1, tq), F32),
                        pltpu.VMEM((2, hw + ones_rows, tq), F32)],
        compiler_params=pltpu.CompilerParams(dimension_semantics=("parallel", "parallel"),
                                             vmem_limit_bytes=FLASH_VMEM_LIMIT),
        name="diff_attention",
    )(lam_full.reshape(1), qt, k, vt, subln.reshape(1, hw))


def _route_kernel(lg_ref, b_ref, tri_ref, eidx_ref, w_ref, rank_ref, cnt_ref, carry_ref):
    t = lg_ref.shape[1]

    @pl.when(pl.program_id(0) == 0)
    def _():
        carry_ref[...] = jnp.zeros_like(carry_ref)

    scores = jax.nn.sigmoid(lg_ref[...])
    choice = (scores + b_ref[...]).reshape(N_GROUPS, GROUP_SIZE, t)
    s3 = scores.reshape(N_GROUPS, GROUP_SIZE, t)
    member = lax.broadcasted_iota(jnp.int32, choice.shape, 1)
    group = lax.broadcasted_iota(jnp.int32, (N_GROUPS, 1, t), 0)
    expert = lax.broadcasted_iota(jnp.int32, choice.shape, 0) * GROUP_SIZE + member
    neg_inf = -jnp.inf
    m1 = jnp.max(choice, axis=1, keepdims=True)
    first = jnp.min(jnp.where(choice == m1, member, GROUP_SIZE), axis=1, keepdims=True)
    m2 = jnp.max(jnp.where(member == first, neg_inf, choice), axis=1, keepdims=True)
    gscore = m1 + m2
    gsel = jnp.zeros(gscore.shape, F32)
    for _ in range(TOPK_GROUPS):
        m = jnp.max(gscore, axis=0, keepdims=True)
        f = jnp.min(jnp.where(gscore == m, group, N_GROUPS), axis=0, keepdims=True)
        hit = group == f
        gsel = jnp.where(hit, 1.0, gsel)
        gscore = jnp.where(hit, neg_inf, gscore)
    cand = jnp.where(gsel > 0.0, choice, neg_inf)
    esel = jnp.zeros(choice.shape, F32)
    picks = []
    for _ in range(TOP_K):
        m = jnp.max(jnp.max(cand, axis=1, keepdims=True), axis=0, keepdims=True)
        f = jnp.min(jnp.min(jnp.where(cand == m, expert, N_EXPERTS), axis=1, keepdims=True), axis=0, keepdims=True)
        hit = expert == f
        esel = jnp.where(hit, 1.0, esel)
        cand = jnp.where(hit, neg_inf, cand)
        picks.append(f)
    w = s3 * esel
    denom = jnp.sum(jnp.sum(w, axis=1, keepdims=True), axis=0, keepdims=True) + 1e-20
    w = w / denom * ROUTED_SCALE
    sel = esel.reshape(N_EXPERTS, t)
    before = jnp.dot(sel.astype(BF16), tri_ref[...], preferred_element_type=F32) + carry_ref[...]
    before = before.reshape(N_GROUPS, GROUP_SIZE, t)
    pick = lambda a, hit: jnp.sum(jnp.sum(jnp.where(hit, a, 0.0), axis=1, keepdims=True), axis=0).reshape(1, t)
    for k, f in enumerate(picks):
        hit = expert == f
        eidx_ref[k:k + 1, :] = f.reshape(1, t)
        w_ref[k:k + 1, :] = pick(w, hit)
        rank_ref[k:k + 1, :] = pick(before, hit).astype(jnp.int32)
    carry_ref[...] += jnp.sum(sel, axis=1, keepdims=True)
    cnt_ref[...] = carry_ref[...]


def route(logits_t, bias, lo, hi):
    t = _token_tile(logits_t.shape[1])
    n = hi - lo
    tile0 = lo // t
    tri = (jnp.arange(t)[:, None] < jnp.arange(t)[None, :]).astype(BF16)
    tok = lambda dt: jax.ShapeDtypeStruct((TOP_K, n), dt)
    tok_spec = pl.BlockSpec((TOP_K, t), lambda i: (0, i))
    return pl.pallas_call(
        _route_kernel,
        grid=(n // t,),
        in_specs=[pl.BlockSpec((N_EXPERTS, t), lambda i: (0, tile0 + i)),
                  pl.BlockSpec((N_EXPERTS, 1), lambda i: (0, 0)),
                  pl.BlockSpec((t, t), lambda i: (0, 0))],
        out_specs=[tok_spec, tok_spec, tok_spec, pl.BlockSpec((N_EXPERTS, 1), lambda i: (0, 0))],
        out_shape=[tok(jnp.int32), tok(F32), tok(jnp.int32), jax.ShapeDtypeStruct((N_EXPERTS, 1), F32)],
        scratch_shapes=[pltpu.VMEM((N_EXPERTS, 1), F32)],
        compiler_params=_params(("arbitrary",)),
        name="route",
    )(logits_t, bias.reshape(N_EXPERTS, 1), tri)


def _slot_kernel(start_ref, eidx_ref, rank_ref, dest_ref):
    e = eidx_ref[...]
    d = rank_ref[...]
    for x in range(N_EXPERTS):
        d = d + jnp.where(e == x, start_ref[x], 0)
    dest_ref[...] = d


def slot_index(pad_start, eidx, rank):
    n = eidx.shape[1]
    t = _token_tile(n)
    spec = pl.BlockSpec((TOP_K, t), lambda i: (0, i))
    return pl.pallas_call(
        _slot_kernel,
        grid=(n // t,),
        in_specs=[pl.BlockSpec(memory_space=pltpu.SMEM), spec, spec],
        out_specs=spec,
        out_shape=jax.ShapeDtypeStruct((TOP_K, n), jnp.int32),
        compiler_params=_params(("parallel",)),
        name="slot_index",
    )(pad_start, eidx, rank)


def _sc_worker():
    return lax.axis_index("s") * SC_CORES + lax.axis_index("c")


def sc_dispatch(h, dest3, n_slots, row0):
    d = h.shape[1]
    n_win = dest3.shape[0]
    mesh = plsc.VectorSubcoreMesh(core_axis_name="c", subcore_axis_name="s")

    @functools.partial(
        pl.kernel, mesh=mesh, out_type=jax.ShapeDtypeStruct((n_slots, d), h.dtype),
        scratch_types=[pltpu.VMEM((TOP_K, SC_WINDOW), jnp.int32), pltpu.VMEM((SC_WINDOW, d), h.dtype),
                       pltpu.SemaphoreType.DMA])
    def k(h_hbm, dest_hbm, out_hbm, idx_v, rows_v, sem):
        wid = _sc_worker()

        @pl.loop(0, -(-n_win // SC_WORKERS))
        def _(it):
            w = it * SC_WORKERS + wid

            @pl.when(w < n_win)
            def _():
                pltpu.sync_copy(dest_hbm.at[w], idx_v)
                pltpu.sync_copy(h_hbm.at[pl.ds(row0 + w * SC_WINDOW, SC_WINDOW)], rows_v)
                copies = [pltpu.async_copy(rows_v, out_hbm.at[idx_v.at[j]], sem) for j in range(TOP_K)]
                for c in copies:
                    c.wait()

    return k(h, dest3)


def sc_combine_gather(y, dest3):
    d = y.shape[1]
    n_win = dest3.shape[0]
    n = n_win * SC_WINDOW
    mesh = plsc.VectorSubcoreMesh(core_axis_name="c", subcore_axis_name="s")

    @functools.partial(
        pl.kernel, mesh=mesh, out_type=jax.ShapeDtypeStruct((TOP_K, n, d), y.dtype),
        scratch_types=[pltpu.VMEM((TOP_K, SC_WINDOW), jnp.int32), pltpu.VMEM((2, SC_WINDOW, d), y.dtype),
                       pltpu.SemaphoreType.DMA, pltpu.SemaphoreType.DMA,
                       pltpu.SemaphoreType.DMA, pltpu.SemaphoreType.DMA])
    def k(y_hbm, dest_hbm, out_hbm, idx_v, rows_v, gsem0, gsem1, osem0, osem1):
        wid = _sc_worker()
        gsem, osem = (gsem0, gsem1), (osem0, osem1)

        @pl.loop(0, -(-n_win // SC_WORKERS))
        def _(it):
            w = it * SC_WORKERS + wid

            @pl.when(w < n_win)
            def _():
                pltpu.sync_copy(dest_hbm.at[w], idx_v)
                gather = lambda j: pltpu.async_copy(y_hbm.at[idx_v.at[j]], rows_v.at[j % 2], gsem[j % 2])
                g = [None] * TOP_K
                o = [None] * TOP_K
                g[0] = gather(0)
                for j in range(TOP_K):
                    if j + 1 < TOP_K:
                        if j >= 1:
                            o[j - 1].wait()
                        g[j + 1] = gather(j + 1)
                    g[j].wait()
                    o[j] = pltpu.async_copy(rows_v.at[j % 2], out_hbm.at[j, pl.ds(w * SC_WINDOW, SC_WINDOW)],
                                            osem[j % 2])
                o[TOP_K - 2].wait()
                o[TOP_K - 1].wait()

    return k(y, dest3)


def _expert_ffn_kernel(be_ref, bv_ref, x_ref, wg_ref, wu_ref, wd_ref, o_ref, wg_s, wu_s, wd_s):
    b = pl.program_id(0)
    valid = bv_ref[b]
    new_expert = (b == 0) | (be_ref[b] != be_ref[jnp.maximum(b - 1, 0)])

    @pl.when(new_expert)
    def _():
        wg_s[...] = wg_ref[0, 0].astype(BF16)
        wu_s[...] = wu_ref[0, 0].astype(BF16)
        wd_s[...] = wd_ref[0, 0].astype(BF16)

    sub = x_ref.shape[0] // MOE_SUB_BLOCKS

    def sub_block(r):
        row = lax.broadcasted_iota(jnp.int32, (sub, 1), 0) + r * sub
        rows = pl.ds(r * sub, sub)
        lo, hi = _unpack_bf16_pair(jnp.where(row < valid, x_ref[rows, :], 0))
        x = jnp.concatenate([lo.astype(BF16), hi.astype(BF16)], axis=1)
        a = jnp.dot(x, wg_s[...], preferred_element_type=F32)
        a = a * jax.nn.sigmoid(a) * jnp.dot(x, wu_s[...], preferred_element_type=F32)
        y = jnp.dot(a.astype(BF16), wd_s[...], preferred_element_type=F32)
        half = y.shape[1] // 2
        o_ref[rows, :] = _pack_bf16_pair(y[:, :half], y[:, half:])

    for live in range(1, MOE_SUB_BLOCKS + 1):
        upper = valid <= live * sub if live < MOE_SUB_BLOCKS else True

        @pl.when((valid > (live - 1) * sub) & upper)
        def _():
            for r in range(live):
                sub_block(r)


def expert_ffn(xg, block_expert, block_valid, wg, wu, wd, layer):
    n_slots, dp = xg.shape
    d, f = wg.shape[-2:]
    grid_spec = pltpu.PrefetchScalarGridSpec(
        num_scalar_prefetch=2,
        grid=(n_slots // MOE_BLOCK,),
        in_specs=[pl.BlockSpec((MOE_BLOCK, dp), lambda b, be, bv: (b, 0)),
                  pl.BlockSpec((1, 1, d, f), lambda b, be, bv: (layer, be[b], 0, 0)),
                  pl.BlockSpec((1, 1, d, f), lambda b, be, bv: (layer, be[b], 0, 0)),
                  pl.BlockSpec((1, 1, f, d), lambda b, be, bv: (layer, be[b], 0, 0))],
        out_specs=pl.BlockSpec((MOE_BLOCK, dp), lambda b, be, bv: (b, 0)),
        scratch_shapes=[pltpu.VMEM((d, f), BF16), pltpu.VMEM((d, f), BF16), pltpu.VMEM((f, d), BF16)],
    )
    return pl.pallas_call(
        _expert_ffn_kernel,
        grid_spec=grid_spec,
        out_shape=jax.ShapeDtypeStruct((n_slots, dp), jnp.int32),
        compiler_params=_params(("arbitrary",)),
        name="expert_ffn",
    )(block_expert, block_valid, xg, wg, wu, wd)


def _combine_kernel(yg_ref, w_ref, h_ref, swg_ref, swu_ref, swd_ref, xs_ref, mods_ref, *rest, tile0, gate_idx, seq):
    o_ref = rest[-1]
    h = h_ref[...]
    a = jnp.dot(h, swg_ref[...], preferred_element_type=F32)
    a = a * jax.nn.sigmoid(a) * jnp.dot(h, swu_ref[...], preferred_element_type=F32)
    acc = jnp.dot(a.astype(BF16), swd_ref[...], preferred_element_type=F32)
    half = acc.shape[1] // 2
    acc_lo, acc_hi = acc[:, :half], acc[:, half:]
    wt = w_ref[...].T
    for k in range(TOP_K):
        lo, hi = _unpack_bf16_pair(yg_ref[k])
        acc_lo = acc_lo + wt[:, k:k + 1] * lo
        acc_hi = acc_hi + wt[:, k:k + 1] * hi
    tm = h.shape[0]
    gate = _row_mod(mods_ref, gate_idx, (tile0 + pl.program_id(0)) * tm, tm, seq)
    o_ref[:, :half] = xs_ref[:, :half] + gate[:, :half] * acc_lo
    o_ref[:, half:] = xs_ref[:, half:] + gate[:, half:] * acc_hi


def combine(yg, w, h, swg, swu, swd, residual, lo, prev):
    xs, mods, gate_idx, seq = residual
    n_all, d = h.shape
    n = w.shape[1]
    f = swg.shape[-1]
    tm = ROW_TILE
    tile0 = lo // tm
    in_specs = [pl.BlockSpec((TOP_K, tm, d // 2), lambda i: (0, i, 0)),
                pl.BlockSpec((TOP_K, tm), lambda i: (0, i)),
                pl.BlockSpec((tm, d), lambda i: (tile0 + i, 0)),
                pl.BlockSpec((d, f), lambda i: (0, 0)),
                pl.BlockSpec((d, f), lambda i: (0, 0)),
                pl.BlockSpec((f, d), lambda i: (0, 0)),
                pl.BlockSpec((tm, d), lambda i: (tile0 + i, 0)),
                pl.BlockSpec((2, 6, d), lambda i: (0, 0, 0))]
    args = [yg, w, h, swg, swu, swd, xs, mods]
    aliases = {}
    if prev is not None:
        in_specs.append(pl.BlockSpec(memory_space=pl.ANY))
        args.append(prev)
        aliases = {len(args) - 1: 0}
    return pl.pallas_call(
        functools.partial(_combine_kernel, tile0=tile0, gate_idx=gate_idx, seq=seq),
        grid=(n // tm,),
        in_specs=in_specs,
        out_specs=pl.BlockSpec((tm, d), lambda i: (tile0 + i, 0)),
        out_shape=jax.ShapeDtypeStruct((n_all, d), F32),
        input_output_aliases=aliases,
        compiler_params=_params(("parallel",)),
        name="moe_combine",
    )(*args)


def moe(h, h_packed, logits_t, bias, wg, wu, wd, layer, swg, swu, swd, residual):
    n = h.shape[0]
    t = _token_tile(n)
    cut = (n // t + 1) // 2 * t
    shared = (swg.astype(BF16), swu.astype(BF16), swd.astype(BF16))
    staged = [_moe_experts(h_packed, logits_t, bias, wg, wu, wd, layer, lo, hi) for lo, hi in ((0, cut), (cut, n))]
    out = None
    for (yg, w), lo in zip(staged, (0, cut)):
        out = combine(yg, w, h, *shared, residual, lo, out)
    return out


def _moe_experts(h_packed, logits_t, bias, wg, wu, wd, layer, lo, hi):
    n = hi - lo
    eidx, w, rank, counts = route(logits_t, bias, lo, hi)
    counts = counts.reshape(N_EXPERTS).astype(jnp.int32)
    padded = (counts + MOE_BLOCK - 1) // MOE_BLOCK * MOE_BLOCK
    pad_end = jnp.cumsum(padded)
    pad_start = pad_end - padded
    n_slots = n * TOP_K + N_EXPERTS * MOE_BLOCK
    starts = jnp.arange(n_slots // MOE_BLOCK, dtype=jnp.int32) * MOE_BLOCK
    owner = jnp.sum((pad_end[None, :] <= starts[:, None]).astype(jnp.int32), axis=1)
    block_expert = jnp.minimum(owner, N_EXPERTS - 1)
    member = (block_expert[:, None] == jnp.arange(N_EXPERTS, dtype=jnp.int32)[None, :]).astype(jnp.int32)
    left = jnp.sum(member * (counts + pad_start)[None, :], axis=1) - starts
    block_valid = jnp.clip(left, 0, MOE_BLOCK).astype(jnp.int32)
    dest = slot_index(pad_start.astype(jnp.int32), eidx, rank)
    dest3 = dest.reshape(TOP_K, n // SC_WINDOW, SC_WINDOW).transpose(1, 0, 2)
    xg = sc_dispatch(h_packed, dest3, n_slots, lo)
    y = expert_ffn(xg, block_expert, block_valid, wg, wu, wd, layer)
    return sc_combine_gather(y, dest3), w


def mixer_ab(h, w_in, w_out, decay_logit, conv_w, conv_b, w1, b1, f1, w2, b2, f2, w3, skip, rope, dft, residual, *,
             seq, ctx_len):
    n_rows = h.shape[0]
    tm = _token_tile(n_rows)
    p = mm([(h, w_in.astype(BF16))], F32, tm, 512, name="ab_in_proj")
    qkv_w = 2 * RET_QK + RET_V
    n_qk = 2 * RET_QK // LANE
    scales = (1.0,) * (RET_QK // LANE) + (RET_DK ** -0.5,) * (RET_QK // LANE) + (1.0,) * (RET_V // LANE)
    qkv = rope_cast(p, rope[0], rope[1], width=qkv_w, n_rot_blocks=n_qk, head_dim=RET_DK, scales=scales,
                    out_dtype=F32)
    log_g = jax.nn.log_sigmoid(decay_logit.astype(F32))
    ret = retention(qkv, p, log_g, jnp.exp(RET_CHUNK * log_g), seq=seq)
    v, x1, x2 = shortconv(p, conv_w, conv_b, seq=seq)
    filt = (w1, b1, f1, w2, b2, f2, w3)
    hy_x = long_conv_two_stage(dft["x"], hyena_filter_taps_2d(seq, *filt), v, x1, x2, skip, seq)
    hy_c = long_conv_one_stage(dft["c"], hyena_filter_taps(ctx_len, *filt), v[seq:], x1[seq:], x2[seq:], skip)
    hy = jnp.concatenate([hy_x, hy_c], axis=0)
    w_out = w_out.astype(BF16)
    return mm([(ret, w_out[:RET_V]), (hy, w_out[RET_V:])], F32, tm, 512, residual=residual, name="ab_out_proj")


def mixer_da(h, w_in, w_out, lam, subln, lambda_init, rope, residual, *, seq, ctx_len):
    n_rows = h.shape[0]
    tm = _token_tile(n_rows)
    p = mm([(h, w_in.astype(BF16))], F32, tm, 512, name="da_in_proj")
    qt, k, vt = rope_da(p, rope[0], rope[1])
    lam_f = lam.astype(F32)
    lam_full = jnp.exp(jnp.sum(lam_f[0] * lam_f[1])) - jnp.exp(jnp.sum(lam_f[2] * lam_f[3])) + lambda_init
    o = diff_attention(qt, k, vt, lam_full, subln, seq=seq, ctx_len=ctx_len, lambda_init=lambda_init)
    return mm([(o, w_out.astype(BF16))], F32, tm, 512, residual=residual, name="da_out_proj")


def kernel(x, c, ctx, c_ctx, w_ada, b_ada, norm_mix, norm_ffn, ab_w_in, ab_w_out, ret_decay_logit, hy_conv_w, hy_conv_b, hy_w1, hy_b1, hy_freq1, hy_w2, hy_b2, hy_freq2, hy_w3, hy_skip, da_w_in, da_w_out, da_lambda, da_subln, router_w, router_b, exp_w_gate, exp_w_up, exp_w_down, sh_w_gate, sh_w_up, sh_w_down, norm_final):
    batch, seq, d = x.shape
    ctx_len = ctx.shape[1]
    assert batch == 1 and seq % ROW_TILE == 0 and ctx_len == ROW_TILE
    depth = w_ada.shape[0]
    n_rows = seq + ctx_len

    xs = jnp.concatenate([x[0], ctx[0]], axis=0)
    cv = jnp.zeros((SUBLANE, d), F32).at[0].set(c_ctx).at[1].set(c[0])
    mods = adaln(cv, w_ada, b_ada)[:, :2].reshape(depth, 2, 6, d)

    rope_ret = rope_tables(seq, ctx_len, RET_DK)
    rope_da = rope_tables(seq, ctx_len, DA_HEAD_DIM)
    dft = dict(x=dft_tables_two_stage(2 * seq), c=dft_tables_one_stage(2 * ctx_len))
    common = dict(n_rows=n_rows, seq=seq)

    for i in range(depth):
        j = i // 2
        (h,) = norm_mod(xs, norm_mix[i], mods=mods[i], shift_idx=0, scale_idx=1, **common)
        residual = (xs, mods[i], 2, seq)
        if i % 2 == 0:
            xs = mixer_ab(h, ab_w_in[j], ab_w_out[j], ret_decay_logit[j], hy_conv_w[j], hy_conv_b[j], hy_w1[j],
                          hy_b1[j], hy_freq1[j], hy_w2[j], hy_b2[j], hy_freq2[j], hy_w3[j], hy_skip[j], rope_ret, dft,
                          residual, seq=seq, ctx_len=ctx_len)
        else:
            lambda_init = 0.8 - 0.6 * math.exp(-0.3 * i)
            xs = mixer_da(h, da_w_in[j], da_w_out[j], da_lambda[j], da_subln[j], lambda_init, rope_da, residual,
                          seq=seq, ctx_len=ctx_len)
        h, logits_t, h_packed = norm_mod(xs, norm_ffn[i], mods=mods[i], shift_idx=3, scale_idx=4,
                                         router_wt=router_w[i].T, **common)
        xs = moe(h, h_packed, logits_t, router_b[i], exp_w_gate, exp_w_up, exp_w_down, i,
                 sh_w_gate[i], sh_w_up[i], sh_w_down[i], (xs, mods[i], 5, seq))
    (out,) = norm_mod(xs, norm_final, n_rows=seq, seq=seq, out_dtype=F32)
    return out[None]
```

```python
import functools
import math

import jax
import jax.numpy as jnp
from jax import lax
from jax.experimental import pallas as pl
from jax.experimental.pallas import tpu as pltpu
from jax.experimental.pallas import tpu_sc as plsc

F32 = jnp.float32
BF16 = jnp.bfloat16
HIGHEST = lax.Precision.HIGHEST

GRID_W = 64
EPS = 1e-6
ROPE_BASE = 10000.0

RET_HEADS = 4
RET_DK = 128
RET_DV = 256
RET_CHUNK = 128
RET_STEP_CHUNKS = 2
RET_QK = RET_HEADS * RET_DK
RET_V = RET_HEADS * RET_DV

HY_WIDTH = 512
HY_ORDER = 2
HY_BANDS = 16
HY_EMB = 2 * HY_BANDS + 1
HY_FFN = 64
HY_DECAY_TARGET = 1e-2
HY_FAST_DECAY = 0.3
HY_SLOW_DECAY = 1.5
HY_ZCOLS = 64
HY_VALID_COL = HY_EMB
FFT_N2 = 128

DA_HEADS = 8
DA_HEAD_DIM = 64
DA_WIDTH = DA_HEADS * 2 * DA_HEAD_DIM

N_EXPERTS = 64
TOP_K = 8
N_GROUPS = 8
TOPK_GROUPS = 4
GROUP_SIZE = N_EXPERTS // N_GROUPS
ROUTED_SCALE = 2.5
MOE_BLOCK = 512
MOE_SUB_BLOCKS = 2
SC_CORES = 2
SC_SUBCORES = 16
SC_WORKERS = SC_CORES * SC_SUBCORES
SC_WINDOW = 64

LANE = 128
SUBLANE = 8
ROW_TILE = 256
MAX_TOKEN_TILE = 1280
VMEM_LIMIT = 48 * 1024 * 1024
FLASH_VMEM_LIMIT = 56 * 1024 * 1024
NEG_BIG = -1e30
LOG2_E = 1.4426950408889634
FLASH_ONES_ROWS = 16
FLASH_INIT_KEYS = 16
FLASH_LAZY_HEADROOM = 60.0


def _params(sem):
    return pltpu.CompilerParams(dimension_semantics=sem, vmem_limit_bytes=VMEM_LIMIT)


def _token_tile(n):
    best = ROW_TILE
    t = ROW_TILE
    while t <= min(n, MAX_TOKEN_TILE):
        if n % t == 0:
            best = t
        t += ROW_TILE
    return best


def _row_mod(mods_ref, idx, row0, n, seq):
    row = row0 + lax.broadcasted_iota(jnp.int32, (n, 1), 0)
    return jnp.where(row >= seq, mods_ref[0, idx:idx + 1, :], mods_ref[1, idx:idx + 1, :])


def _mm_kernel(*refs, n_pairs, has_epi, gate_idx, seq, rope_scales):
    acc = None
    for p in range(n_pairs):
        a = refs[2 * p][...].astype(BF16)
        b = refs[2 * p + 1][...].astype(BF16)
        d = jnp.dot(a, b, preferred_element_type=F32)
        acc = d if acc is None else acc + d
    idx = 2 * n_pairs
    if has_epi:
        acc = refs[idx][...] * (acc + refs[idx + 1][...] * refs[idx + 2][...])
        idx += 3
    if gate_idx is not None:
        tm = acc.shape[0]
        acc = refs[idx][...] + _row_mod(refs[idx + 1], gate_idx, pl.program_id(0) * tm, tm, seq) * acc
        idx += 2
    if rope_scales is None:
        o_ref = refs[idx]
        o_ref[...] = acc.astype(o_ref.dtype)
        return
    cos, sin = refs[idx][...], refs[idx + 1][...]
    o_ref = refs[idx + 2]
    j = pl.program_id(1)

    @pl.when(j >= len(rope_scales))
    def _():
        o_ref[...] = acc.astype(o_ref.dtype)

    for t, scale in enumerate(rope_scales):
        @pl.when(j == t)
        def _():
            for b in range(acc.shape[1] // LANE):
                x = acc[:, b * LANE:(b + 1) * LANE]
                x = x * cos + pltpu.roll(x, LANE // 2, 1) * sin
                o_ref[:, b * LANE:(b + 1) * LANE] = (x * scale if scale != 1.0 else x).astype(o_ref.dtype)


def mm(pairs, out_dtype, tm, tn, epi=None, residual=None, rope=None, name="mm"):
    m = pairs[0][0].shape[0]
    n = pairs[0][1].shape[1]
    assert m % tm == 0 and n % tn == 0
    in_specs, args = [], []
    for a, b in pairs:
        k = a.shape[1]
        in_specs += [pl.BlockSpec((tm, k), lambda i, j: (i, 0)), pl.BlockSpec((k, tn), lambda i, j: (0, j))]
        args += [a, b]
    if epi is not None:
        in_specs += [pl.BlockSpec((tm, tn), lambda i, j: (i, j)), pl.BlockSpec((1, tn), lambda i, j: (0, j)),
                     pl.BlockSpec((tm, tn), lambda i, j: (i, j))]
        args += list(epi)
    gate_idx = seq = None
    if residual is not None:
        res, mods, gate_idx, seq = residual
        in_specs += [pl.BlockSpec((tm, tn), lambda i, j: (i, j)), pl.BlockSpec((2, 6, tn), lambda i, j: (0, 0, j))]
        args += [res, mods]
    rope_scales = None
    if rope is not None:
        cos, sin, rope_scales = rope
        in_specs += [pl.BlockSpec((tm, LANE), lambda i, j: (i, 0)), pl.BlockSpec((tm, LANE), lambda i, j: (i, 0))]
        args += [cos, sin]
    return pl.pallas_call(
        functools.partial(_mm_kernel, n_pairs=len(pairs), has_epi=epi is not None, gate_idx=gate_idx, seq=seq,
                          rope_scales=rope_scales),
        grid=(m // tm, n // tn),
        in_specs=in_specs,
        out_specs=pl.BlockSpec((tm, tn), lambda i, j: (i, j)),
        out_shape=jax.ShapeDtypeStruct((m, n), out_dtype),
        compiler_params=_params(("parallel", "parallel")),
        name=name,
    )(*args)


def _adaln_kernel(cv_ref, w_ref, b_ref, o_ref):
    cv = cv_ref[...]
    s = cv * jax.nn.sigmoid(cv)
    o_ref[0] = jnp.dot(s, w_ref[0], precision=HIGHEST, preferred_element_type=F32) + b_ref[0]


def adaln(cv, w_ada, b_ada):
    depth, d, n = w_ada.shape
    tn = 1536
    return pl.pallas_call(
        _adaln_kernel,
        grid=(depth, n // tn),
        in_specs=[pl.BlockSpec((SUBLANE, d), lambda l, j: (0, 0)),
                  pl.BlockSpec((1, d, tn), lambda l, j: (l, 0, j)),
                  pl.BlockSpec((1, 1, tn), lambda l, j: (l, 0, j))],
        out_specs=pl.BlockSpec((1, SUBLANE, tn), lambda l, j: (l, 0, j)),
        out_shape=jax.ShapeDtypeStruct((depth, SUBLANE, n), F32),
        compiler_params=_params(("parallel", "parallel")),
        name="adaln",
    )(cv, w_ada, b_ada.reshape(depth, 1, n))


def _norm_mod_kernel(*refs, shift_idx, scale_idx, has_router, seq):
    it = iter(refs)
    x = next(it)[...]
    mods_ref = next(it) if shift_idx is not None else None
    g_ref = next(it)
    wr_ref = next(it) if has_router else None
    h_ref = next(it)
    y = x * lax.rsqrt(jnp.mean(x * x, axis=-1, keepdims=True) + EPS) * g_ref[...]
    if shift_idx is not None:
        tm = x.shape[0]
        row0 = pl.program_id(0) * tm
        y = y * (1.0 + _row_mod(mods_ref, scale_idx, row0, tm, seq)) + _row_mod(mods_ref, shift_idx, row0, tm, seq)
    h_ref[...] = y.astype(h_ref.dtype)
    if has_router:
        lg_ref = next(it)
        lg_ref[...] = lax.dot_general(wr_ref[...], y, (((1,), (1,)), ((), ())),
                                      precision=HIGHEST, preferred_element_type=F32)
        half = y.shape[1] // 2
        next(it)[...] = _pack_bf16_pair(y[:, :half], y[:, half:])


def _pack_bf16_pair(a, b):
    bits = lambda x: lax.bitcast_convert_type(x.astype(BF16), jnp.uint16).astype(jnp.int32)
    return bits(a) | lax.shift_left(bits(b), 16)


def _unpack_bf16_pair(w):
    return (lax.bitcast_convert_type(lax.shift_left(w, 16), F32),
            lax.bitcast_convert_type(w & -65536, F32))


def norm_mod(xs, g, *, n_rows, seq, mods=None, shift_idx=None, scale_idx=None, router_wt=None, out_dtype=BF16):
    d = xs.shape[1]
    tm = _token_tile(n_rows)
    row = pl.BlockSpec((tm, d), lambda i: (i, 0))
    in_specs, args = [row], [xs]
    if shift_idx is not None:
        in_specs.append(pl.BlockSpec((2, 6, d), lambda i: (0, 0, 0)))
        args.append(mods)
    in_specs.append(pl.BlockSpec((1, d), lambda i: (0, 0)))
    args.append(g.reshape(1, d))
    has_router = router_wt is not None
    if has_router:
        in_specs.append(pl.BlockSpec(router_wt.shape, lambda i: (0, 0)))
        args.append(router_wt)
    out_specs = [row]
    out_shape = [jax.ShapeDtypeStruct((n_rows, d), out_dtype)]
    if has_router:
        out_specs.append(pl.BlockSpec((N_EXPERTS, tm), lambda i: (0, i)))
        out_shape.append(jax.ShapeDtypeStruct((N_EXPERTS, n_rows), F32))
        out_specs.append(pl.BlockSpec((tm, d // 2), lambda i: (i, 0)))
        out_shape.append(jax.ShapeDtypeStruct((n_rows, d // 2), jnp.int32))
    return pl.pallas_call(
        functools.partial(_norm_mod_kernel, shift_idx=shift_idx, scale_idx=scale_idx, has_router=has_router, seq=seq),
        grid=(n_rows // tm,),
        in_specs=in_specs,
        out_specs=out_specs,
        out_shape=out_shape,
        compiler_params=_params(("parallel",)),
        name="norm_mod",
    )(*args)


def rope_tables(seq, ctx_len, head_dim):
    n_freq = head_dim // 4
    inv = ROPE_BASE ** (-jnp.arange(n_freq, dtype=F32) / n_freq)
    rows = seq // GRID_W
    row = jnp.repeat(jnp.arange(rows, dtype=F32), GRID_W)
    col = jnp.tile(jnp.arange(GRID_W, dtype=F32), rows)
    ang = jnp.concatenate([row[:, None] * inv, col[:, None] * inv], axis=-1)
    cos, sin = jnp.cos(ang), jnp.sin(ang)
    cos = jnp.concatenate([cos, cos], axis=-1)
    sin = jnp.concatenate([-sin, sin], axis=-1)
    reps = LANE // head_dim
    cos, sin = jnp.tile(cos, (1, reps)), jnp.tile(sin, (1, reps))
    cos = jnp.concatenate([cos, jnp.ones((ctx_len, LANE), F32)], axis=0)
    sin = jnp.concatenate([sin, jnp.zeros((ctx_len, LANE), F32)], axis=0)
    return cos, sin


def _ret_kernel(lg_ref, gc_ref, q_ref, k_ref, v_ref, *rest, reverse):
    if reverse:
        yf_ref, gate_ref, o_ref, s_ref = rest
    else:
        o_ref, s_ref = rest
    c = RET_CHUNK

    @pl.when(pl.program_id(0) == 0)
    def _():
        s_ref[...] = jnp.zeros_like(s_ref)

    ii = lax.broadcasted_iota(jnp.int32, (c, c), 0)
    jj = lax.broadcasted_iota(jnp.int32, (c, c), 1)
    rel = ((jj - ii) if reverse else (ii - jj)).astype(F32)
    pos = lax.broadcasted_iota(jnp.int32, (c, 1), 0).astype(F32)
    for h in range(RET_HEADS):
        lg = lg_ref[h]
        dec = jnp.where(rel >= 0, jnp.exp(jnp.maximum(rel, 0.0) * lg), 0.0)
        if reverse:
            q_dec = jnp.exp((c - pos) * lg)
            k_dec = jnp.exp(pos * lg)
        else:
            q_dec = jnp.exp((pos + 1.0) * lg)
            k_dec = jnp.exp((c - 1.0 - pos) * lg)
        state = s_ref[h]
        qk_cols = slice(h * RET_DK, (h + 1) * RET_DK)
        v_cols = slice(h * RET_DV, (h + 1) * RET_DV)
        for sub in (range(RET_STEP_CHUNKS - 1, -1, -1) if reverse else range(RET_STEP_CHUNKS)):
            rows = slice(sub * c, (sub + 1) * c)
            q = q_ref[rows, qk_cols]
            k = k_ref[rows, qk_cols]
            v = v_ref[rows, v_cols].astype(BF16)
            s = lax.dot_general(q.astype(BF16), k.astype(BF16), (((1,), (1,)), ((), ())),
                                preferred_element_type=F32) * dec
            y = jnp.dot(s.astype(BF16), v, preferred_element_type=F32)
            y = y + jnp.dot((q * q_dec).astype(BF16), state.astype(BF16), preferred_element_type=F32)
            upd = lax.dot_general((k * k_dec).astype(BF16), v, (((0,), (0,)), ((), ())), preferred_element_type=F32)
            state = gc_ref[h] * state + upd
            if reverse:
                r = y + yf_ref[rows, v_cols]
                mu = jnp.mean(r, axis=-1, keepdims=True)
                rc = r - mu
                var = jnp.mean(rc * rc, axis=-1, keepdims=True)
                g = gate_ref[rows, v_cols]
                o_ref[rows, v_cols] = (rc * lax.rsqrt(var + EPS) * (g * jax.nn.sigmoid(g))).astype(o_ref.dtype)
            else:
                o_ref[rows, v_cols] = y
        s_ref[h] = state


def retention(p, log_g, g_chunk, *, seq):
    n_rows = p.shape[0]
    step = RET_STEP_CHUNKS * RET_CHUNK
    assert seq % step == 0 and n_rows % step == 0
    n_steps = n_rows // step
    n_x = seq // step
    smem = pl.BlockSpec(memory_space=pltpu.SMEM)

    def run(reverse, extra):
        if reverse:
            idx = lambda t: n_steps - 1 - t
        else:
            idx = lambda t: (t + n_x) % n_steps
        in_specs = [smem, smem,
                    pl.BlockSpec((step, RET_QK), lambda t: (idx(t), 0)),
                    pl.BlockSpec((step, RET_QK), lambda t: (idx(t), 1)),
                    pl.BlockSpec((step, RET_V), lambda t: (idx(t), 1))]
        args = [log_g[1 if reverse else 0], g_chunk[1 if reverse else 0], p, p, p]
        if reverse:
            in_specs += [pl.BlockSpec((step, RET_V), lambda t: (idx(t), 0)),
                         pl.BlockSpec((step, RET_V), lambda t: (idx(t), 2))]
            args += list(extra)
        return pl.pallas_call(
            functools.partial(_ret_kernel, reverse=reverse),
            grid=(n_steps,),
            in_specs=in_specs,
            out_specs=pl.BlockSpec((step, RET_V), lambda t: (idx(t), 0)),
            out_shape=jax.ShapeDtypeStruct((n_rows, RET_V), BF16 if reverse else F32),
            scratch_shapes=[pltpu.VMEM((RET_HEADS, RET_DK, RET_DV), F32)],
            compiler_params=_params(("arbitrary",)),
            name="retention_bwd" if reverse else "retention_fwd",
        )(*args)

    y_fwd = run(False, None)
    return run(True, (y_fwd, p))


def _shortconv_kernel(cur_ref, prev_ref, next_ref, w_ref, b_ref, v_ref, x1_ref, x2_ref, *, x_tiles):
    i = pl.program_id(0)
    cur = cur_ref[...]
    rows = cur.shape[0]
    row = lax.broadcasted_iota(jnp.int32, (rows, 1), 0)
    has_prev = jnp.where((i == 0) | (i == x_tiles), 0.0, 1.0)
    has_next = jnp.where((i == x_tiles - 1) | (i == x_tiles), 0.0, 1.0)
    up = jnp.where(row == 0, prev_ref[SUBLANE - 1:SUBLANE, :] * has_prev, pltpu.roll(cur, 1, 0))
    dn = jnp.where(row == rows - 1, next_ref[0:1, :] * has_next, pltpu.roll(cur, rows - 1, 0))
    y = up * w_ref[0:1, :] + cur * w_ref[1:2, :] + dn * w_ref[2:3, :] + b_ref[...]
    v_ref[...] = y[:, :HY_WIDTH]
    x1_ref[...] = y[:, HY_WIDTH:2 * HY_WIDTH]
    x2_ref[...] = y[:, 2 * HY_WIDTH:]


def shortconv(p, w, b, *, seq):
    n_rows = p.shape[0]
    width = 3 * HY_WIDTH
    col = p.shape[1] // width - 1
    per = ROW_TILE // SUBLANE
    last = n_rows // SUBLANE - 1
    out = jax.ShapeDtypeStruct((n_rows, HY_WIDTH), F32)
    ospec = pl.BlockSpec((ROW_TILE, HY_WIDTH), lambda i: (i, 0))
    return pl.pallas_call(
        functools.partial(_shortconv_kernel, x_tiles=seq // ROW_TILE),
        grid=(n_rows // ROW_TILE,),
        in_specs=[pl.BlockSpec((ROW_TILE, width), lambda i: (i, col)),
                  pl.BlockSpec((SUBLANE, width), lambda i: (jnp.maximum(i * per - 1, 0), col)),
                  pl.BlockSpec((SUBLANE, width), lambda i: (jnp.minimum((i + 1) * per, last), col)),
                  pl.BlockSpec((3, width), lambda i: (0, 0)),
                  pl.BlockSpec((1, width), lambda i: (0, 0))],
        out_specs=[ospec, ospec, ospec],
        out_shape=[out, out, out],
        compiler_params=_params(("parallel",)),
        name="shortconv",
    )(p, p, p, w, b.reshape(1, width))


def _filt_kernel(z_ref, w1_ref, b1_ref, f1_ref, w2_ref, b2_ref, f2_ref, w3a_ref, w3b_ref, dl_ref, *o_ref):
    z = z_ref[...]
    h = jnp.sin(f1_ref[...] * (jnp.dot(z, w1_ref[...], precision=HIGHEST, preferred_element_type=F32) + b1_ref[...]))
    h = jnp.sin(f2_ref[...] * (jnp.dot(h, w2_ref[...], precision=HIGHEST, preferred_element_type=F32) + b2_ref[...]))
    window = jnp.exp(-z[:, 0:1] * dl_ref[...]) * z[:, HY_VALID_COL:HY_VALID_COL + 1]
    for o, w3_ref in enumerate((w3a_ref, w3b_ref)):
        o_ref[o][...] = jnp.dot(h, w3_ref[...], precision=HIGHEST, preferred_element_type=F32) * window


def hyena_filter_taps(length, w1, b1, f1, w2, b2, f2, w3):
    z = _filter_positions(length)
    w1p = jnp.zeros((HY_ZCOLS, HY_FFN), F32).at[:HY_EMB].set(w1)
    deltas = jnp.abs(jnp.linspace(math.log(HY_DECAY_TARGET) / HY_SLOW_DECAY,
                                  math.log(HY_DECAY_TARGET) / HY_FAST_DECAY, HY_WIDTH, dtype=F32)).reshape(1, HY_WIDTH)
    tm = min(length, 512)
    half_tiles = length // tm
    vec = lambda a: a.reshape(1, HY_FFN)
    small = lambda shape: pl.BlockSpec(shape, lambda i: (0, 0))
    w3_spec = lambda o: pl.BlockSpec((HY_FFN, HY_WIDTH), lambda i: (0, 2 * o + jnp.where(i >= half_tiles, 1, 0)))
    assert HY_ORDER == 2
    return pl.pallas_call(
        _filt_kernel,
        grid=(2 * half_tiles,),
        in_specs=[pl.BlockSpec((tm, HY_ZCOLS), lambda i: (i, 0)),
                  small((HY_ZCOLS, HY_FFN)), small((1, HY_FFN)), small((1, HY_FFN)),
                  small((HY_FFN, HY_FFN)), small((1, HY_FFN)), small((1, HY_FFN)),
                  w3_spec(0), w3_spec(1), small((1, HY_WIDTH))],
        out_specs=[pl.BlockSpec((tm, HY_WIDTH), lambda i: (i, 0))] * HY_ORDER,
        out_shape=[jax.ShapeDtypeStruct((2 * length, HY_WIDTH), F32)] * HY_ORDER,
        compiler_params=_params(("parallel",)),
        name="hyena_filter",
    )(z, w1p, vec(b1), vec(f1), w2, vec(b2), vec(f2), w3, w3, deltas)


def _filt2d_kernel(z_ref, w1_ref, b1_ref, f1_ref, w2_ref, b2_ref, f2_ref, w3fa_ref, w3ba_ref, w3fb_ref, w3bb_ref,
                   dl_ref, oa_ref, ob_ref):
    n1 = oa_ref.shape[0]
    c = dl_ref.shape[1]
    z = z_ref[...]
    h = jnp.sin(f1_ref[...] * (jnp.dot(z, w1_ref[...], precision=HIGHEST, preferred_element_type=F32) + b1_ref[...]))
    h = jnp.sin(f2_ref[...] * (jnp.dot(h, w2_ref[...], precision=HIGHEST, preferred_element_type=F32) + b2_ref[...]))
    window = jnp.exp(-z[:, 0:1] * dl_ref[...]) * z[:, HY_VALID_COL:HY_VALID_COL + 1]
    for o_ref, wf_ref, wb_ref in ((oa_ref, w3fa_ref, w3ba_ref), (ob_ref, w3fb_ref, w3bb_ref)):
        for j in range(SUBLANE):
            rows = slice(j * n1, (j + 1) * n1)
            hj = h[rows]
            taps = jnp.concatenate(
                [jnp.dot(hj[:n1 // 2], wf_ref[...], precision=HIGHEST, preferred_element_type=F32),
                 jnp.dot(hj[n1 // 2:], wb_ref[...], precision=HIGHEST, preferred_element_type=F32)], axis=0)
            o_ref[:, j * c:(j + 1) * c] = taps * window[rows]


def hyena_filter_taps_2d(length, w1, b1, f1, w2, b2, f2, w3):
    n1 = 2 * length // FFT_N2
    groups = FFT_N2 // SUBLANE
    z = _filter_positions(length).reshape(n1, groups, SUBLANE, HY_ZCOLS).transpose(1, 2, 0, 3)
    z = z.reshape(2 * length, HY_ZCOLS)
    w1p = jnp.zeros((HY_ZCOLS, HY_FFN), F32).at[:HY_EMB].set(w1)
    deltas = jnp.abs(jnp.linspace(math.log(HY_DECAY_TARGET) / HY_SLOW_DECAY,
                                  math.log(HY_DECAY_TARGET) / HY_FAST_DECAY, HY_WIDTH, dtype=F32)).reshape(1, HY_WIDTH)
    vec = lambda a: a.reshape(1, HY_FFN)
    small = lambda shape: pl.BlockSpec(shape, lambda i: (0, 0))
    w3_spec = lambda col: pl.BlockSpec((HY_FFN, HY_WIDTH), lambda i: (0, col))
    assert HY_ORDER == 2
    out = jax.ShapeDtypeStruct((n1, FFT_N2 * HY_WIDTH), F32)
    return pl.pallas_call(
        _filt2d_kernel,
        grid=(groups,),
        in_specs=[pl.BlockSpec((SUBLANE * n1, HY_ZCOLS), lambda i: (i, 0)),
                  small((HY_ZCOLS, HY_FFN)), small((1, HY_FFN)), small((1, HY_FFN)),
                  small((HY_FFN, HY_FFN)), small((1, HY_FFN)), small((1, HY_FFN)),
                  w3_spec(0), w3_spec(1), w3_spec(2), w3_spec(3), small((1, HY_WIDTH))],
        out_specs=[pl.BlockSpec((n1, SUBLANE * HY_WIDTH), lambda i: (0, i))] * HY_ORDER,
        out_shape=[out] * HY_ORDER,
        compiler_params=_params(("parallel",)),
        name="hyena_filter",
    )(z, w1p, vec(b1), vec(f1), w2, vec(b2), vec(f2), w3, w3, w3, w3, deltas)


def _filter_positions(length):
    t = jnp.concatenate([jnp.arange(length, dtype=F32), float(length) - jnp.arange(length, dtype=F32)])
    valid = jnp.ones((2 * length,), F32).at[length].set(0.0)
    t_norm = t / max(length - 1, 1)
    bands = jnp.linspace(1e-4, HY_BANDS - 1, HY_BANDS, dtype=F32)
    ang = (2.0 * math.pi / length) * t[:, None] * bands[None, :]
    z = jnp.concatenate([t_norm[:, None], jnp.cos(ang), -jnp.sin(ang), valid[:, None]], axis=-1)
    return jnp.pad(z, ((0, 0), (0, HY_ZCOLS - z.shape[1])))


def _angles(num, den):
    return (2.0 * math.pi / den) * (num % den).astype(F32)


def dft_tables_two_stage(m):
    n2 = FFT_N2
    n1 = m // n2
    half = n1 // 2
    kp = -(-(half + 1) // SUBLANE) * SUBLANE
    k1 = jnp.arange(kp, dtype=jnp.int32)
    live = (k1 <= half)
    a1 = _angles(k1[:, None] * jnp.arange(n1, dtype=jnp.int32)[None, :], n1)
    f1 = jnp.concatenate([jnp.where(live[:, None], jnp.cos(a1), 0.0), jnp.where(live[:, None], -jnp.sin(a1), 0.0)], 0)
    wgt = jnp.where((k1 == 0) | (k1 == half), 1.0, 2.0) * live / m
    a1h = a1[:, :half].T
    cinv = jnp.concatenate([jnp.cos(a1h) * wgt[None, :], -jnp.sin(a1h) * wgt[None, :]], axis=1)
    k = k1[:, None, None] + n1 * jnp.arange(n2, dtype=jnp.int32)[None, :, None]
    th = _angles(k * jnp.arange(n2, dtype=jnp.int32)[None, None, :], m)
    c = jnp.where(live[:, None, None], jnp.cos(th), 0.0)
    s = jnp.where(live[:, None, None], jnp.sin(th), 0.0)
    g_fwd = jnp.concatenate([jnp.concatenate([c, s], 2), jnp.concatenate([-s, c], 2)], 1)
    ct, st = jnp.swapaxes(c, 1, 2), jnp.swapaxes(s, 1, 2)
    g_inv = jnp.concatenate([jnp.concatenate([ct, -st], 2), jnp.concatenate([st, ct], 2)], 1)
    return dict(n1=n1, kp=kp, f1=f1.astype(BF16), f1_half=f1[:, :half].astype(BF16), cinv=cinv.astype(BF16),
                g_fwd=g_fwd.astype(BF16), g_inv=g_inv.astype(BF16))


def dft_tables_one_stage(m):
    half = m // 2
    kp = -(-(half + 1) // SUBLANE) * SUBLANE
    k = jnp.arange(kp, dtype=jnp.int32)
    live = (k <= half)
    a = _angles(k[:, None] * jnp.arange(m, dtype=jnp.int32)[None, :], m)
    f = jnp.concatenate([jnp.where(live[:, None], jnp.cos(a), 0.0), jnp.where(live[:, None], -jnp.sin(a), 0.0)], 0)
    wgt = jnp.where((k == 0) | (k == half), 1.0, 2.0) * live / m
    ah = a[:, :half].T
    cinv = jnp.concatenate([jnp.cos(ah) * wgt[None, :], -jnp.sin(ah) * wgt[None, :]], axis=1)
    return dict(kp=kp, f=f.astype(BF16), f_half=f[:, :half].astype(BF16), cinv=cinv.astype(BF16))


def _bmm_kernel(*refs, kb, in_part_major, out_part_major, has_h):
    if has_h:
        g_ref, a_ref, h_ref, o_ref = refs
    else:
        g_ref, a_ref, o_ref = refs
    n2 = FFT_N2
    for b in range(kb):
        if in_part_major:
            ar, ai = a_ref[0, b], a_ref[1, b]
        else:
            ar, ai = a_ref[b, 0], a_ref[b, 1]
        if has_h:
            hr, hi = h_ref[b, 0], h_ref[b, 1]
            ar, ai = ar * hr - ai * hi, ar * hi + ai * hr
        xin = jnp.concatenate([ar, ai], axis=0).astype(BF16)
        y = jnp.dot(g_ref[b], xin, preferred_element_type=F32)
        if out_part_major:
            o_ref[0, b] = y[:n2].astype(o_ref.dtype)
            o_ref[1, b] = y[n2:].astype(o_ref.dtype)
        else:
            o_ref[b, 0] = y[:n2].astype(o_ref.dtype)
            o_ref[b, 1] = y[n2:].astype(o_ref.dtype)


def bmm_k1(g, a, h=None, *, in_part_major, out_part_major):
    kp = g.shape[0]
    n2 = FFT_N2
    c = a.shape[-1]
    kb, tc = SUBLANE, min(c, 512)
    pm = lambda: pl.BlockSpec((2, kb, n2, tc), lambda i, j: (0, i, 0, j))
    km = lambda: pl.BlockSpec((kb, 2, n2, tc), lambda i, j: (i, 0, 0, j))
    in_specs = [pl.BlockSpec((kb, 2 * n2, 2 * n2), lambda i, j: (i, 0, 0)), pm() if in_part_major else km()]
    args = [g, a]
    if h is not None:
        in_specs.append(km())
        args.append(h)
    return pl.pallas_call(
        functools.partial(_bmm_kernel, kb=kb, in_part_major=in_part_major, out_part_major=out_part_major,
                          has_h=h is not None),
        grid=(kp // kb, c // tc),
        in_specs=in_specs,
        out_specs=pm() if out_part_major else km(),
        out_shape=jax.ShapeDtypeStruct((2, kp, n2, c), BF16) if out_part_major else
        jax.ShapeDtypeStruct((kp, 2, n2, c), F32),
        compiler_params=_params(("parallel", "parallel")),
        name="dft_inner",
    )(*args)


def _cmul_kernel(x_ref, h_ref, o_ref):
    xr, xi, hr, hi = x_ref[0], x_ref[1], h_ref[0], h_ref[1]
    o_ref[0] = xr * hr - xi * hi
    o_ref[1] = xr * hi + xi * hr


def cmul(x, h):
    spec = pl.BlockSpec(x.shape, lambda i: (0, 0, 0))
    return pl.pallas_call(_cmul_kernel, grid=(1,), in_specs=[spec, spec], out_specs=spec,
                          out_shape=jax.ShapeDtypeStruct(x.shape, F32), compiler_params=_params(("arbitrary",)),
                          name="spectrum_product")(x, h)


def _dft_outer3_kernel(f_ref, x_ref, o_ref):
    c = x_ref.shape[2]
    f = f_ref[...]
    for j in range(SUBLANE):
        o_ref[:, j * c:(j + 1) * c] = jnp.dot(f, x_ref[:, j, :].astype(BF16),
                                              preferred_element_type=F32).astype(o_ref.dtype)


def dft_outer3(f, x3, n_outer):
    rows = f.shape[0]
    c = x3.shape[2]
    return pl.pallas_call(
        _dft_outer3_kernel,
        grid=(FFT_N2 // SUBLANE,),
        in_specs=[pl.BlockSpec((rows, n_outer), lambda j: (0, 0)),
                  pl.BlockSpec((n_outer, SUBLANE, c), lambda j: (0, j, 0))],
        out_specs=pl.BlockSpec((rows, SUBLANE * c), lambda j: (0, j)),
        out_shape=jax.ShapeDtypeStruct((rows, FFT_N2 * c), BF16),
        compiler_params=_params(("parallel",)),
        name="dft_outer",
    )(f, x3)


def _idft_gate3_kernel(cinv_ref, b_ref, gate_ref, skip_ref, u_ref, o_ref, *, u_is_3d):
    c = gate_ref.shape[2]
    cinv = cinv_ref[...]
    for j in range(SUBLANE):
        cols = slice(j * c, (j + 1) * c)
        acc = jnp.dot(cinv, b_ref[:, cols].astype(BF16), preferred_element_type=F32)
        u = u_ref[:, j, :] if u_is_3d else u_ref[:, cols]
        o_ref[:, cols] = gate_ref[:, j, :] * (acc + skip_ref[...] * u)


def idft_gate3(cinv, b2d, gate3, skip_row, u):
    n_outer = cinv.shape[0]
    c = gate3.shape[2]
    u_is_3d = u.ndim == 3
    wide = pl.BlockSpec((n_outer, SUBLANE * c), lambda j: (0, j))
    slab = pl.BlockSpec((n_outer, SUBLANE, c), lambda j: (0, j, 0))
    return pl.pallas_call(
        functools.partial(_idft_gate3_kernel, u_is_3d=u_is_3d),
        grid=(FFT_N2 // SUBLANE,),
        in_specs=[pl.BlockSpec(cinv.shape, lambda j: (0, 0)),
                  pl.BlockSpec((b2d.shape[0], SUBLANE * c), lambda j: (0, j)),
                  slab, pl.BlockSpec((1, c), lambda j: (0, 0)), slab if u_is_3d else wide],
        out_specs=wide,
        out_shape=jax.ShapeDtypeStruct((n_outer, FFT_N2 * c), F32),
        compiler_params=_params(("parallel",)),
        name="idft_outer_gate",
    )(cinv, b2d, gate3, skip_row, u)


def long_conv_two_stage(tabs, taps, v, x1, x2, skip, length):
    c = v.shape[1]
    n2, n1, kp = FFT_N2, tabs["n1"], tabs["kp"]
    as3 = lambda a: a.reshape(a.shape[0] // n2, n2, c)
    spectrum = lambda a2d: bmm_k1(tabs["g_fwd"], a2d.reshape(2, kp, n2, c), in_part_major=True, out_part_major=False)

    spectra = [spectrum(mm([(tabs["f1"], taps[o])], BF16, 2 * kp, 2048, name="dft_outer")) for o in range(HY_ORDER)]
    v3 = as3(v)
    u = v3
    for o, gate in enumerate((x1, x2)):
        if u.ndim == 3:
            a = dft_outer3(tabs["f1_half"], u, n1 // 2)
        else:
            a = mm([(tabs["f1_half"], u)], BF16, 2 * kp, 2048, name="dft_outer")
        bt = bmm_k1(tabs["g_inv"], spectrum(a), spectra[o], in_part_major=False, out_part_major=True)
        u = idft_gate3(tabs["cinv"], bt.reshape(2 * kp, n2 * c), as3(gate), skip[o].reshape(1, c), u)
    return u.reshape(length, c)


def long_conv_one_stage(tabs, taps, v, x1, x2, skip):
    length, c = v.shape
    kp = tabs["kp"]
    u = v
    for o, gate in enumerate((x1, x2)):
        hs = mm([(tabs["f"], taps[o])], F32, 2 * kp, c, name="ctx_dft").reshape(2, kp, c)
        xs = mm([(tabs["f_half"], u)], F32, 2 * kp, c, name="ctx_dft").reshape(2, kp, c)
        ys = cmul(xs, hs).reshape(2 * kp, c)
        u = mm([(tabs["cinv"], ys)], F32, length, c, epi=(gate, skip[o].reshape(1, c), u), name="ctx_idft_gate")
    return u


def _flash_kernel(lam_ref, qt_ref, k_ref, vt_ref, sub_ref, o_ref, m_ref, excess_ref, acc_ref, *, kv, seq, ctx_len,
                  out_scale):
    i = pl.program_id(1)
    last_q = pl.num_programs(1) - 1
    tq = qt_ref.shape[1]
    d = DA_HEAD_DIM
    dv = 2 * DA_HEAD_DIM
    n_chunks = k_ref.shape[0] // kv
    acc_ref[...] = jnp.zeros_like(acc_ref)

    def scores(off, rows, c, masked):
        s = jnp.dot(k_ref[pl.ds(off, rows), c * d:(c + 1) * d], qt_ref[c * d:(c + 1) * d, :],
                    preferred_element_type=F32)
        if masked:
            key = off + lax.broadcasted_iota(jnp.int32, (rows, 1), 0)
            lane = lax.broadcasted_iota(jnp.int32, (1, tq), 1)
            s = s + jnp.where(key < seq, NEG_BIG, 0.0) * jnp.where(lane >= tq - ctx_len, 1.0, 0.0)
        return s

    def exact_step(off, c, masked):
        s = scores(off, kv, c, masked)
        m_old = m_ref[c]
        m_new = jnp.maximum(m_old, jnp.max(s, axis=0, keepdims=True))
        pr = jnp.exp2(s - m_new).astype(BF16)
        acc_ref[c] = jnp.exp2(m_old - m_new) * acc_ref[c] + jnp.dot(vt_ref[:, pl.ds(off, kv)], pr,
                                                                   preferred_element_type=F32)
        m_ref[c] = m_new

    def lazy_step(off, c, masked):
        s = scores(off, kv, c, masked)
        m_old = m_ref[c]
        m_chunk = jnp.max(s, axis=0, keepdims=True)
        pv = jnp.dot(vt_ref[:, pl.ds(off, kv)], jnp.exp2(s - m_old).astype(BF16), preferred_element_type=F32)
        m_new = jnp.maximum(m_old, m_chunk)
        acc_ref[c] = jnp.exp2(m_old - m_new) * (acc_ref[c] + pv)
        m_ref[c] = m_new
        excess_ref[c] = jnp.maximum(excess_ref[c], m_chunk - m_old)

    def all_chunks(step, masked):
        def body(kc, carry):
            off = pl.multiple_of(kc * kv, kv)
            for c in range(2):
                step(off, c, masked)
            return carry

        lax.fori_loop(0, n_chunks, body, 0)

    def run(masked):
        for c in range(2):
            m0 = jnp.max(scores(0, FLASH_INIT_KEYS, c, False), axis=0, keepdims=True)
            if masked:
                lane = lax.broadcasted_iota(jnp.int32, (1, tq), 1)
                m_ctx = jnp.max(scores(seq, FLASH_INIT_KEYS, c, False), axis=0, keepdims=True)
                m0 = jnp.where(lane >= tq - ctx_len, m_ctx, m0)
            m_ref[c] = m0
        excess_ref[...] = jnp.full_like(excess_ref, NEG_BIG)
        all_chunks(lazy_step, masked)

        @pl.when(jnp.max(excess_ref[...]) > FLASH_LAZY_HEADROOM)
        def _():
            m_ref[...] = jnp.full_like(m_ref, NEG_BIG)
            acc_ref[...] = jnp.zeros_like(acc_ref)
            all_chunks(exact_step, masked)

    @pl.when(i != last_q)
    def _():
        run(False)

    @pl.when(i == last_q)
    def _():
        run(True)

    a0 = acc_ref[0, :dv, :] / acc_ref[0, dv:dv + 1, :]
    a1 = acc_ref[1, :dv, :] / acc_ref[1, dv:dv + 1, :]
    o = (a0 - lam_ref[0] * a1).T
    o = o * lax.rsqrt(jnp.mean(o * o, axis=-1, keepdims=True) + 1e-5) * sub_ref[...]
    o_ref[...] = (o * out_scale).astype(o_ref.dtype)


def _rope_da_kernel(p_ref, cos_ref, sin_ref, qt_ref, k_ref, vt_ref):
    cos = cos_ref[...]
    sin = sin_ref[...]
    hw = 2 * DA_HEAD_DIM
    lane = lax.broadcasted_iota(jnp.int32, cos.shape, 1)
    first_half = (lane % DA_HEAD_DIM) < DA_HEAD_DIM // 2

    def rotated(b):
        x = p_ref[:, b * LANE:(b + 1) * LANE]
        rot = jnp.where(first_half, pltpu.roll(x, LANE - DA_HEAD_DIM // 2, 1), pltpu.roll(x, DA_HEAD_DIM // 2, 1))
        return x * cos + rot * sin

    ones = jnp.ones((FLASH_ONES_ROWS, cos.shape[0]), BF16)
    for h in range(DA_HEADS):
        qt_ref[h * hw:(h + 1) * hw, :] = (rotated(h) * (LOG2_E * DA_HEAD_DIM ** -0.5)).T.astype(BF16)
        k_ref[:, h * hw:(h + 1) * hw] = rotated(DA_HEADS + h).astype(BF16)
        base = h * (hw + FLASH_ONES_ROWS)
        vt_ref[base:base + hw, :] = p_ref[:, (2 * DA_HEADS + h) * LANE:(2 * DA_HEADS + h + 1) * LANE].T.astype(BF16)
        vt_ref[base + hw:base + hw + FLASH_ONES_ROWS, :] = ones


def rope_da(p, cos, sin):
    n_rows = p.shape[0]
    assert 2 * DA_HEAD_DIM == LANE
    vt_rows = DA_HEADS * (LANE + FLASH_ONES_ROWS)
    return pl.pallas_call(
        _rope_da_kernel,
        grid=(n_rows // ROW_TILE,),
        in_specs=[pl.BlockSpec((ROW_TILE, 3 * DA_WIDTH), lambda i: (i, 0)),
                  pl.BlockSpec((ROW_TILE, LANE), lambda i: (i, 0)),
                  pl.BlockSpec((ROW_TILE, LANE), lambda i: (i, 0))],
        out_specs=[pl.BlockSpec((DA_WIDTH, ROW_TILE), lambda i: (0, i)),
                   pl.BlockSpec((ROW_TILE, DA_WIDTH), lambda i: (i, 0)),
                   pl.BlockSpec((vt_rows, ROW_TILE), lambda i: (0, i))],
        out_shape=[jax.ShapeDtypeStruct((DA_WIDTH, n_rows), BF16),
                   jax.ShapeDtypeStruct((n_rows, DA_WIDTH), BF16),
                   jax.ShapeDtypeStruct((vt_rows, n_rows), BF16)],
        compiler_params=_params(("parallel",)),
        name="rope_da",
    )(p, cos, sin)


def diff_attention(qt, k, vt, lam_full, subln, *, seq, ctx_len, lambda_init):
    n_rows = k.shape[0]
    tq = _token_tile(n_rows)
    hw = 2 * DA_HEAD_DIM
    ones_rows = FLASH_ONES_ROWS
    return pl.pallas_call(
        functools.partial(_flash_kernel, kv=tq, seq=seq, ctx_len=ctx_len, out_scale=1.0 - lambda_init),
        grid=(DA_HEADS, n_rows // tq),
        in_specs=[pl.BlockSpec(memory_space=pltpu.SMEM),
                  pl.BlockSpec((hw, tq), lambda h, i: (h, i)),
                  pl.BlockSpec((n_rows, hw), lambda h, i: (0, h)),
                  pl.BlockSpec((hw + ones_rows, n_rows), lambda h, i: (h, 0)),
                  pl.BlockSpec((1, hw), lambda h, i: (0, 0))],
        out_specs=pl.BlockSpec((tq, hw), lambda h, i: (i, h)),
        out_shape=jax.ShapeDtypeStruct((n_rows, DA_WIDTH), BF16),
        scratch_shapes=[pltpu.VMEM((2, 1, tq), F32), pltpu.VMEM((2, 1, tq), F32),
                        pltpu.VMEM((2, hw + ones_rows, tq), F32)],
        compiler_params=pltpu.CompilerParams(dimension_semantics=("parallel", "parallel"),
                                             vmem_limit_bytes=FLASH_VMEM_LIMIT),
        name="diff_attention",
    )(lam_full.reshape(1), qt, k, vt, subln.reshape(1, hw))


def _route_kernel(lg_ref, b_ref, tri_ref, eidx_ref, w_ref, rank_ref, cnt_ref, carry_ref):
    t = lg_ref.shape[1]

    @pl.when(pl.program_id(0) == 0)
    def _():
        carry_ref[...] = jnp.zeros_like(carry_ref)

    scores = jax.nn.sigmoid(lg_ref[...])
    choice = (scores + b_ref[...]).reshape(N_GROUPS, GROUP_SIZE, t)
    s3 = scores.reshape(N_GROUPS, GROUP_SIZE, t)
    member = lax.broadcasted_iota(jnp.int32, choice.shape, 1)
    group = lax.broadcasted_iota(jnp.int32, (N_GROUPS, 1, t), 0)
    expert = lax.broadcasted_iota(jnp.int32, choice.shape, 0) * GROUP_SIZE + member
    neg_inf = -jnp.inf
    m1 = jnp.max(choice, axis=1, keepdims=True)
    first = jnp.min(jnp.where(choice == m1, member, GROUP_SIZE), axis=1, keepdims=True)
    m2 = jnp.max(jnp.where(member == first, neg_inf, choice), axis=1, keepdims=True)
    gscore = m1 + m2
    gsel = jnp.zeros(gscore.shape, F32)
    for _ in range(TOPK_GROUPS):
        m = jnp.max(gscore, axis=0, keepdims=True)
        f = jnp.min(jnp.where(gscore == m, group, N_GROUPS), axis=0, keepdims=True)
        hit = group == f
        gsel = jnp.where(hit, 1.0, gsel)
        gscore = jnp.where(hit, neg_inf, gscore)
    cand = jnp.where(gsel > 0.0, choice, neg_inf)
    esel = jnp.zeros(choice.shape, F32)
    picks = []
    for _ in range(TOP_K):
        m = jnp.max(jnp.max(cand, axis=1, keepdims=True), axis=0, keepdims=True)
        f = jnp.min(jnp.min(jnp.where(cand == m, expert, N_EXPERTS), axis=1, keepdims=True), axis=0, keepdims=True)
        hit = expert == f
        esel = jnp.where(hit, 1.0, esel)
        cand = jnp.where(hit, neg_inf, cand)
        picks.append(f)
    w = s3 * esel
    denom = jnp.sum(jnp.sum(w, axis=1, keepdims=True), axis=0, keepdims=True) + 1e-20
    w = w / denom * ROUTED_SCALE
    sel = esel.reshape(N_EXPERTS, t)
    before = jnp.dot(sel.astype(BF16), tri_ref[...], preferred_element_type=F32) + carry_ref[...]
    before = before.reshape(N_GROUPS, GROUP_SIZE, t)
    pick = lambda a, hit: jnp.sum(jnp.sum(jnp.where(hit, a, 0.0), axis=1, keepdims=True), axis=0).reshape(1, t)
    for k, f in enumerate(picks):
        hit = expert == f
        eidx_ref[k:k + 1, :] = f.reshape(1, t)
        w_ref[k:k + 1, :] = pick(w, hit)
        rank_ref[k:k + 1, :] = pick(before, hit).astype(jnp.int32)
    carry_ref[...] += jnp.sum(sel, axis=1, keepdims=True)
    cnt_ref[...] = carry_ref[...]


def route(logits_t, bias, lo, hi):
    t = _token_tile(logits_t.shape[1])
    n = hi - lo
    tile0 = lo // t
    tri = (jnp.arange(t)[:, None] < jnp.arange(t)[None, :]).astype(BF16)
    tok = lambda dt: jax.ShapeDtypeStruct((TOP_K, n), dt)
    tok_spec = pl.BlockSpec((TOP_K, t), lambda i: (0, i))
    return pl.pallas_call(
        _route_kernel,
        grid=(n // t,),
        in_specs=[pl.BlockSpec((N_EXPERTS, t), lambda i: (0, tile0 + i)),
                  pl.BlockSpec((N_EXPERTS, 1), lambda i: (0, 0)),
                  pl.BlockSpec((t, t), lambda i: (0, 0))],
        out_specs=[tok_spec, tok_spec, tok_spec, pl.BlockSpec((N_EXPERTS, 1), lambda i: (0, 0))],
        out_shape=[tok(jnp.int32), tok(F32), tok(jnp.int32), jax.ShapeDtypeStruct((N_EXPERTS, 1), F32)],
        scratch_shapes=[pltpu.VMEM((N_EXPERTS, 1), F32)],
        compiler_params=_params(("arbitrary",)),
        name="route",
    )(logits_t, bias.reshape(N_EXPERTS, 1), tri)


def _slot_kernel(start_ref, eidx_ref, rank_ref, dest_ref):
    e = eidx_ref[...]
    d = rank_ref[...]
    for x in range(N_EXPERTS):
        d = d + jnp.where(e == x, start_ref[x], 0)
    dest_ref[...] = d


def slot_index(pad_start, eidx, rank):
    n = eidx.shape[1]
    t = _token_tile(n)
    spec = pl.BlockSpec((TOP_K, t), lambda i: (0, i))
    return pl.pallas_call(
        _slot_kernel,
        grid=(n // t,),
        in_specs=[pl.BlockSpec(memory_space=pltpu.SMEM), spec, spec],
        out_specs=spec,
        out_shape=jax.ShapeDtypeStruct((TOP_K, n), jnp.int32),
        compiler_params=_params(("parallel",)),
        name="slot_index",
    )(pad_start, eidx, rank)


def _sc_worker():
    return lax.axis_index("s") * SC_CORES + lax.axis_index("c")


def sc_dispatch(h, dest3, n_slots, row0):
    d = h.shape[1]
    n_win = dest3.shape[0]
    mesh = plsc.VectorSubcoreMesh(core_axis_name="c", subcore_axis_name="s")

    @functools.partial(
        pl.kernel, mesh=mesh, out_type=jax.ShapeDtypeStruct((n_slots, d), h.dtype),
        scratch_types=[pltpu.VMEM((TOP_K, SC_WINDOW), jnp.int32), pltpu.VMEM((SC_WINDOW, d), h.dtype),
                       pltpu.SemaphoreType.DMA])
    def k(h_hbm, dest_hbm, out_hbm, idx_v, rows_v, sem):
        wid = _sc_worker()

        @pl.loop(0, -(-n_win // SC_WORKERS))
        def _(it):
            w = it * SC_WORKERS + wid

            @pl.when(w < n_win)
            def _():
                pltpu.sync_copy(dest_hbm.at[w], idx_v)
                pltpu.sync_copy(h_hbm.at[pl.ds(row0 + w * SC_WINDOW, SC_WINDOW)], rows_v)
                copies = [pltpu.async_copy(rows_v, out_hbm.at[idx_v.at[j]], sem) for j in range(TOP_K)]
                for c in copies:
                    c.wait()

    return k(h, dest3)


def sc_combine_gather(y, dest3):
    d = y.shape[1]
    n_win = dest3.shape[0]
    n = n_win * SC_WINDOW
    mesh = plsc.VectorSubcoreMesh(core_axis_name="c", subcore_axis_name="s")

    @functools.partial(
        pl.kernel, mesh=mesh, out_type=jax.ShapeDtypeStruct((TOP_K, n, d), y.dtype),
        scratch_types=[pltpu.VMEM((TOP_K, SC_WINDOW), jnp.int32), pltpu.VMEM((2, SC_WINDOW, d), y.dtype),
                       pltpu.SemaphoreType.DMA, pltpu.SemaphoreType.DMA,
                       pltpu.SemaphoreType.DMA, pltpu.SemaphoreType.DMA])
    def k(y_hbm, dest_hbm, out_hbm, idx_v, rows_v, gsem0, gsem1, osem0, osem1):
        wid = _sc_worker()
        gsem, osem = (gsem0, gsem1), (osem0, osem1)

        @pl.loop(0, -(-n_win // SC_WORKERS))
        def _(it):
            w = it * SC_WORKERS + wid

            @pl.when(w < n_win)
            def _():
                pltpu.sync_copy(dest_hbm.at[w], idx_v)
                gather = lambda j: pltpu.async_copy(y_hbm.at[idx_v.at[j]], rows_v.at[j % 2], gsem[j % 2])
                g = [None] * TOP_K
                o = [None] * TOP_K
                g[0] = gather(0)
                for j in range(TOP_K):
                    if j + 1 < TOP_K:
                        if j >= 1:
                            o[j - 1].wait()
                        g[j + 1] = gather(j + 1)
                    g[j].wait()
                    o[j] = pltpu.async_copy(rows_v.at[j % 2], out_hbm.at[j, pl.ds(w * SC_WINDOW, SC_WINDOW)],
                                            osem[j % 2])
                o[TOP_K - 2].wait()
                o[TOP_K - 1].wait()

    return k(y, dest3)


def _expert_ffn_kernel(be_ref, bv_ref, x_ref, wg_ref, wu_ref, wd_ref, o_ref, wg_s, wu_s, wd_s):
    b = pl.program_id(0)
    valid = bv_ref[b]
    new_expert = (b == 0) | (be_ref[b] != be_ref[jnp.maximum(b - 1, 0)])

    @pl.when(new_expert)
    def _():
        wg_s[...] = wg_ref[0, 0].astype(BF16)
        wu_s[...] = wu_ref[0, 0].astype(BF16)
        wd_s[...] = wd_ref[0, 0].astype(BF16)

    sub = x_ref.shape[0] // MOE_SUB_BLOCKS

    def sub_block(r):
        row = lax.broadcasted_iota(jnp.int32, (sub, 1), 0) + r * sub
        rows = pl.ds(r * sub, sub)
        lo, hi = _unpack_bf16_pair(jnp.where(row < valid, x_ref[rows, :], 0))
        x = jnp.concatenate([lo.astype(BF16), hi.astype(BF16)], axis=1)
        a = jnp.dot(x, wg_s[...], preferred_element_type=F32)
        a = a * jax.nn.sigmoid(a) * jnp.dot(x, wu_s[...], preferred_element_type=F32)
        y = jnp.dot(a.astype(BF16), wd_s[...], preferred_element_type=F32)
        half = y.shape[1] // 2
        o_ref[rows, :] = _pack_bf16_pair(y[:, :half], y[:, half:])

    for live in range(1, MOE_SUB_BLOCKS + 1):
        upper = valid <= live * sub if live < MOE_SUB_BLOCKS else True

        @pl.when((valid > (live - 1) * sub) & upper)
        def _():
            for r in range(live):
                sub_block(r)


def expert_ffn(xg, block_expert, block_valid, wg, wu, wd, layer):
    n_slots, dp = xg.shape
    d, f = wg.shape[-2:]
    grid_spec = pltpu.PrefetchScalarGridSpec(
        num_scalar_prefetch=2,
        grid=(n_slots // MOE_BLOCK,),
        in_specs=[pl.BlockSpec((MOE_BLOCK, dp), lambda b, be, bv: (b, 0)),
                  pl.BlockSpec((1, 1, d, f), lambda b, be, bv: (layer, be[b], 0, 0)),
                  pl.BlockSpec((1, 1, d, f), lambda b, be, bv: (layer, be[b], 0, 0)),
                  pl.BlockSpec((1, 1, f, d), lambda b, be, bv: (layer, be[b], 0, 0))],
        out_specs=pl.BlockSpec((MOE_BLOCK, dp), lambda b, be, bv: (b, 0)),
        scratch_shapes=[pltpu.VMEM((d, f), BF16), pltpu.VMEM((d, f), BF16), pltpu.VMEM((f, d), BF16)],
    )
    return pl.pallas_call(
        _expert_ffn_kernel,
        grid_spec=grid_spec,
        out_shape=jax.ShapeDtypeStruct((n_slots, dp), jnp.int32),
        compiler_params=_params(("arbitrary",)),
        name="expert_ffn",
    )(block_expert, block_valid, xg, wg, wu, wd)


def _combine_kernel(yg_ref, w_ref, h_ref, swg_ref, swu_ref, swd_ref, xs_ref, mods_ref, *rest, tile0, gate_idx, seq):
    o_ref = rest[-1]
    h = h_ref[...]
    a = jnp.dot(h, swg_ref[...], preferred_element_type=F32)
    a = a * jax.nn.sigmoid(a) * jnp.dot(h, swu_ref[...], preferred_element_type=F32)
    acc = jnp.dot(a.astype(BF16), swd_ref[...], preferred_element_type=F32)
    half = acc.shape[1] // 2
    acc_lo, acc_hi = acc[:, :half], acc[:, half:]
    wt = w_ref[...].T
    for k in range(TOP_K):
        lo, hi = _unpack_bf16_pair(yg_ref[k])
        acc_lo = acc_lo + wt[:, k:k + 1] * lo
        acc_hi = acc_hi + wt[:, k:k + 1] * hi
    tm = h.shape[0]
    gate = _row_mod(mods_ref, gate_idx, (tile0 + pl.program_id(0)) * tm, tm, seq)
    o_ref[:, :half] = xs_ref[:, :half] + gate[:, :half] * acc_lo
    o_ref[:, half:] = xs_ref[:, half:] + gate[:, half:] * acc_hi


def combine(yg, w, h, swg, swu, swd, residual, lo, prev):
    xs, mods, gate_idx, seq = residual
    n_all, d = h.shape
    n = w.shape[1]
    f = swg.shape[-1]
    tm = ROW_TILE
    tile0 = lo // tm
    in_specs = [pl.BlockSpec((TOP_K, tm, d // 2), lambda i: (0, i, 0)),
                pl.BlockSpec((TOP_K, tm), lambda i: (0, i)),
                pl.BlockSpec((tm, d), lambda i: (tile0 + i, 0)),
                pl.BlockSpec((d, f), lambda i: (0, 0)),
                pl.BlockSpec((d, f), lambda i: (0, 0)),
                pl.BlockSpec((f, d), lambda i: (0, 0)),
                pl.BlockSpec((tm, d), lambda i: (tile0 + i, 0)),
                pl.BlockSpec((2, 6, d), lambda i: (0, 0, 0))]
    args = [yg, w, h, swg, swu, swd, xs, mods]
    aliases = {}
    if prev is not None:
        in_specs.append(pl.BlockSpec(memory_space=pl.ANY))
        args.append(prev)
        aliases = {len(args) - 1: 0}
    return pl.pallas_call(
        functools.partial(_combine_kernel, tile0=tile0, gate_idx=gate_idx, seq=seq),
        grid=(n // tm,),
        in_specs=in_specs,
        out_specs=pl.BlockSpec((tm, d), lambda i: (tile0 + i, 0)),
        out_shape=jax.ShapeDtypeStruct((n_all, d), F32),
        input_output_aliases=aliases,
        compiler_params=_params(("parallel",)),
        name="moe_combine",
    )(*args)


def moe(h, h_packed, logits_t, bias, wg, wu, wd, layer, swg, swu, swd, residual):
    n = h.shape[0]
    t = _token_tile(n)
    cut = (n // t + 1) // 2 * t
    shared = (swg.astype(BF16), swu.astype(BF16), swd.astype(BF16))
    staged = [_moe_experts(h_packed, logits_t, bias, wg, wu, wd, layer, lo, hi) for lo, hi in ((0, cut), (cut, n))]
    out = None
    for (yg, w), lo in zip(staged, (0, cut)):
        out = combine(yg, w, h, *shared, residual, lo, out)
    return out


def _moe_experts(h_packed, logits_t, bias, wg, wu, wd, layer, lo, hi):
    n = hi - lo
    eidx, w, rank, counts = route(logits_t, bias, lo, hi)
    counts = counts.reshape(N_EXPERTS).astype(jnp.int32)
    padded = (counts + MOE_BLOCK - 1) // MOE_BLOCK * MOE_BLOCK
    pad_end = jnp.cumsum(padded)
    pad_start = pad_end - padded
    n_slots = n * TOP_K + N_EXPERTS * MOE_BLOCK
    starts = jnp.arange(n_slots // MOE_BLOCK, dtype=jnp.int32) * MOE_BLOCK
    owner = jnp.sum((pad_end[None, :] <= starts[:, None]).astype(jnp.int32), axis=1)
    block_expert = jnp.minimum(owner, N_EXPERTS - 1)
    member = (block_expert[:, None] == jnp.arange(N_EXPERTS, dtype=jnp.int32)[None, :]).astype(jnp.int32)
    left = jnp.sum(member * (counts + pad_start)[None, :], axis=1) - starts
    block_valid = jnp.clip(left, 0, MOE_BLOCK).astype(jnp.int32)
    dest = slot_index(pad_start.astype(jnp.int32), eidx, rank)
    dest3 = dest.reshape(TOP_K, n // SC_WINDOW, SC_WINDOW).transpose(1, 0, 2)
    xg = sc_dispatch(h_packed, dest3, n_slots, lo)
    y = expert_ffn(xg, block_expert, block_valid, wg, wu, wd, layer)
    return sc_combine_gather(y, dest3), w


def mixer_ab(h, w_in, w_out, decay_logit, conv_w, conv_b, w1, b1, f1, w2, b2, f2, w3, skip, rope, dft, residual, *,
             seq, ctx_len):
    n_rows = h.shape[0]
    tm = _token_tile(n_rows)
    assert RET_DK == LANE
    p = mm([(h, w_in.astype(BF16))], F32, tm, RET_QK, rope=(rope[0], rope[1], (1.0, RET_DK ** -0.5)),
           name="ab_in_proj")
    log_g = jax.nn.log_sigmoid(decay_logit.astype(F32))
    ret = retention(p, log_g, jnp.exp(RET_CHUNK * log_g), seq=seq)
    v, x1, x2 = shortconv(p, conv_w, conv_b, seq=seq)
    filt = (w1, b1, f1, w2, b2, f2, w3)
    hy_x = long_conv_two_stage(dft["x"], hyena_filter_taps_2d(seq, *filt), v, x1, x2, skip, seq)
    hy_c = long_conv_one_stage(dft["c"], hyena_filter_taps(ctx_len, *filt), v[seq:], x1[seq:], x2[seq:], skip)
    hy = jnp.concatenate([hy_x, hy_c], axis=0)
    w_out = w_out.astype(BF16)
    return mm([(ret, w_out[:RET_V]), (hy, w_out[RET_V:])], F32, tm, 512, residual=residual, name="ab_out_proj")


def mixer_da(h, w_in, w_out, lam, subln, lambda_init, rope, residual, *, seq, ctx_len):
    n_rows = h.shape[0]
    tm = _token_tile(n_rows)
    p = mm([(h, w_in.astype(BF16))], F32, tm, 512, name="da_in_proj")
    qt, k, vt = rope_da(p, rope[0], rope[1])
    lam_f = lam.astype(F32)
    lam_full = jnp.exp(jnp.sum(lam_f[0] * lam_f[1])) - jnp.exp(jnp.sum(lam_f[2] * lam_f[3])) + lambda_init
    o = diff_attention(qt, k, vt, lam_full, subln, seq=seq, ctx_len=ctx_len, lambda_init=lambda_init)
    return mm([(o, w_out.astype(BF16))], F32, tm, 512, residual=residual, name="da_out_proj")


def kernel(x, c, ctx, c_ctx, w_ada, b_ada, norm_mix, norm_ffn, ab_w_in, ab_w_out, ret_decay_logit, hy_conv_w, hy_conv_b, hy_w1, hy_b1, hy_freq1, hy_w2, hy_b2, hy_freq2, hy_w3, hy_skip, da_w_in, da_w_out, da_lambda, da_subln, router_w, router_b, exp_w_gate, exp_w_up, exp_w_down, sh_w_gate, sh_w_up, sh_w_down, norm_final):
    batch, seq, d = x.shape
    ctx_len = ctx.shape[1]
    assert batch == 1 and seq % ROW_TILE == 0 and ctx_len == ROW_TILE
    depth = w_ada.shape[0]
    n_rows = seq + ctx_len

    xs = jnp.concatenate([x[0], ctx[0]], axis=0)
    cv = jnp.zeros((SUBLANE, d), F32).at[0].set(c_ctx).at[1].set(c[0])
    mods = adaln(cv, w_ada, b_ada)[:, :2].reshape(depth, 2, 6, d)

    rope_ret = rope_tables(seq, ctx_len, RET_DK)
    rope_da = rope_tables(seq, ctx_len, DA_HEAD_DIM)
    dft = dict(x=dft_tables_two_stage(2 * seq), c=dft_tables_one_stage(2 * ctx_len))
    common = dict(n_rows=n_rows, seq=seq)

    for i in range(depth):
        j = i // 2
        (h,) = norm_mod(xs, norm_mix[i], mods=mods[i], shift_idx=0, scale_idx=1, **common)
        residual = (xs, mods[i], 2, seq)
        if i % 2 == 0:
            xs = mixer_ab(h, ab_w_in[j], ab_w_out[j], ret_decay_logit[j], hy_conv_w[j], hy_conv_b[j], hy_w1[j],
                          hy_b1[j], hy_freq1[j], hy_w2[j], hy_b2[j], hy_freq2[j], hy_w3[j], hy_skip[j], rope_ret, dft,
                          residual, seq=seq, ctx_len=ctx_len)
        else:
            lambda_init = 0.8 - 0.6 * math.exp(-0.3 * i)
            xs = mixer_da(h, da_w_in[j], da_w_out[j], da_lambda[j], da_subln[j], lambda_init, rope_da, residual,
                          seq=seq, ctx_len=ctx_len)
        h, logits_t, h_packed = norm_mod(xs, norm_ffn[i], mods=mods[i], shift_idx=3, scale_idx=4,
                                         router_wt=router_w[i].T, **common)
        xs = moe(h, h_packed, logits_t, router_b[i], exp_w_gate, exp_w_up, exp_w_down, i,
                 sh_w_gate[i], sh_w_up[i], sh_w_down[i], (xs, mods[i], 5, seq))
    (out,) = norm_mod(xs, norm_final, n_rows=seq, seq=seq, out_dtype=F32)
    return out[None]
```

```python
import functools
import math

import jax
import jax.numpy as jnp
from jax import lax
from jax.experimental import pallas as pl
from jax.experimental.pallas import tpu as pltpu
from jax.experimental.pallas import tpu_sc as plsc

F32 = jnp.float32
BF16 = jnp.bfloat16
HIGHEST = lax.Precision.HIGHEST

GRID_W = 64
EPS = 1e-6
ROPE_BASE = 10000.0

RET_HEADS = 4
RET_DK = 128
RET_DV = 256
RET_CHUNK = 128
RET_STEP_CHUNKS = 2
RET_QK = RET_HEADS * RET_DK
RET_V = RET_HEADS * RET_DV

HY_WIDTH = 512
HY_ORDER = 2
HY_BANDS = 16
HY_EMB = 2 * HY_BANDS + 1
HY_FFN = 64
HY_DECAY_TARGET = 1e-2
HY_FAST_DECAY = 0.3
HY_SLOW_DECAY = 1.5
HY_ZCOLS = 64
HY_VALID_COL = HY_EMB
FFT_N2 = 128

DA_HEADS = 8
DA_HEAD_DIM = 64
DA_WIDTH = DA_HEADS * 2 * DA_HEAD_DIM

N_EXPERTS = 64
TOP_K = 8
N_GROUPS = 8
TOPK_GROUPS = 4
GROUP_SIZE = N_EXPERTS // N_GROUPS
ROUTED_SCALE = 2.5
MOE_BLOCK = 512
MOE_SUB_BLOCKS = 2
SC_CORES = 2
SC_SUBCORES = 16
SC_WORKERS = SC_CORES * SC_SUBCORES
SC_WINDOW = 64

LANE = 128
SUBLANE = 8
ROW_TILE = 256
MAX_TOKEN_TILE = 1280
VMEM_LIMIT = 48 * 1024 * 1024
FLASH_VMEM_LIMIT = 56 * 1024 * 1024
NEG_BIG = -1e30
LOG2_E = 1.4426950408889634
FLASH_ONES_ROWS = 16
FLASH_INIT_KEYS = 16
FLASH_LAZY_HEADROOM = 60.0


def _params(sem):
    return pltpu.CompilerParams(dimension_semantics=sem, vmem_limit_bytes=VMEM_LIMIT)


def _token_tile(n):
    best = ROW_TILE
    t = ROW_TILE
    while t <= min(n, MAX_TOKEN_TILE):
        if n % t == 0:
            best = t
        t += ROW_TILE
    return best


def _row_mod(mods_ref, idx, row0, n, seq):
    row = row0 + lax.broadcasted_iota(jnp.int32, (n, 1), 0)
    return jnp.where(row >= seq, mods_ref[0, idx:idx + 1, :], mods_ref[1, idx:idx + 1, :])


def _mm_kernel(*refs, n_pairs, has_epi, gate_idx, seq, rope_scales):
    acc = None
    for p in range(n_pairs):
        a = refs[2 * p][...].astype(BF16)
        b = refs[2 * p + 1][...].astype(BF16)
        d = jnp.dot(a, b, preferred_element_type=F32)
        acc = d if acc is None else acc + d
    idx = 2 * n_pairs
    if has_epi:
        acc = refs[idx][...] * (acc + refs[idx + 1][...] * refs[idx + 2][...])
        idx += 3
    if gate_idx is not None:
        tm = acc.shape[0]
        acc = refs[idx][...] + _row_mod(refs[idx + 1], gate_idx, pl.program_id(0) * tm, tm, seq) * acc
        idx += 2
    if rope_scales is None:
        o_ref = refs[idx]
        o_ref[...] = acc.astype(o_ref.dtype)
        return
    cos, sin = refs[idx][...], refs[idx + 1][...]
    o_ref = refs[idx + 2]
    j = pl.program_id(1)

    @pl.when(j >= len(rope_scales))
    def _():
        o_ref[...] = acc.astype(o_ref.dtype)

    for t, scale in enumerate(rope_scales):
        @pl.when(j == t)
        def _():
            for b in range(acc.shape[1] // LANE):
                x = acc[:, b * LANE:(b + 1) * LANE]
                x = x * cos + pltpu.roll(x, LANE // 2, 1) * sin
                o_ref[:, b * LANE:(b + 1) * LANE] = (x * scale if scale != 1.0 else x).astype(o_ref.dtype)


def mm(pairs, out_dtype, tm, tn, epi=None, residual=None, rope=None, name="mm"):
    m = pairs[0][0].shape[0]
    n = pairs[0][1].shape[1]
    assert m % tm == 0 and n % tn == 0
    in_specs, args = [], []
    for a, b in pairs:
        k = a.shape[1]
        in_specs += [pl.BlockSpec((tm, k), lambda i, j: (i, 0)), pl.BlockSpec((k, tn), lambda i, j: (0, j))]
        args += [a, b]
    if epi is not None:
        in_specs += [pl.BlockSpec((tm, tn), lambda i, j: (i, j)), pl.BlockSpec((1, tn), lambda i, j: (0, j)),
                     pl.BlockSpec((tm, tn), lambda i, j: (i, j))]
        args += list(epi)
    gate_idx = seq = None
    if residual is not None:
        res, mods, gate_idx, seq = residual
        in_specs += [pl.BlockSpec((tm, tn), lambda i, j: (i, j)), pl.BlockSpec((2, 6, tn), lambda i, j: (0, 0, j))]
        args += [res, mods]
    rope_scales = None
    if rope is not None:
        cos, sin, rope_scales = rope
        in_specs += [pl.BlockSpec((tm, LANE), lambda i, j: (i, 0)), pl.BlockSpec((tm, LANE), lambda i, j: (i, 0))]
        args += [cos, sin]
    return pl.pallas_call(
        functools.partial(_mm_kernel, n_pairs=len(pairs), has_epi=epi is not None, gate_idx=gate_idx, seq=seq,
                          rope_scales=rope_scales),
        grid=(m // tm, n // tn),
        in_specs=in_specs,
        out_specs=pl.BlockSpec((tm, tn), lambda i, j: (i, j)),
        out_shape=jax.ShapeDtypeStruct((m, n), out_dtype),
        compiler_params=_params(("parallel", "parallel")),
        name=name,
    )(*args)


def _adaln_kernel(cv_ref, w_ref, b_ref, o_ref):
    cv = cv_ref[...]
    s = cv * jax.nn.sigmoid(cv)
    o_ref[0] = jnp.dot(s, w_ref[0], precision=HIGHEST, preferred_element_type=F32) + b_ref[0]


def adaln(cv, w_ada, b_ada):
    depth, d, n = w_ada.shape
    tn = 1536
    return pl.pallas_call(
        _adaln_kernel,
        grid=(depth, n // tn),
        in_specs=[pl.BlockSpec((SUBLANE, d), lambda l, j: (0, 0)),
                  pl.BlockSpec((1, d, tn), lambda l, j: (l, 0, j)),
                  pl.BlockSpec((1, 1, tn), lambda l, j: (l, 0, j))],
        out_specs=pl.BlockSpec((1, SUBLANE, tn), lambda l, j: (l, 0, j)),
        out_shape=jax.ShapeDtypeStruct((depth, SUBLANE, n), F32),
        compiler_params=_params(("parallel", "parallel")),
        name="adaln",
    )(cv, w_ada, b_ada.reshape(depth, 1, n))


def _norm_mod_kernel(*refs, shift_idx, scale_idx, has_router, seq):
    it = iter(refs)
    x = next(it)[...]
    mods_ref = next(it) if shift_idx is not None else None
    g_ref = next(it)
    wr_ref = next(it) if has_router else None
    h_ref = next(it)
    y = x * lax.rsqrt(jnp.mean(x * x, axis=-1, keepdims=True) + EPS) * g_ref[...]
    if shift_idx is not None:
        tm = x.shape[0]
        row0 = pl.program_id(0) * tm
        y = y * (1.0 + _row_mod(mods_ref, scale_idx, row0, tm, seq)) + _row_mod(mods_ref, shift_idx, row0, tm, seq)
    h_ref[...] = y.astype(h_ref.dtype)
    if has_router:
        lg_ref = next(it)
        lg_ref[...] = lax.dot_general(wr_ref[...], y, (((1,), (1,)), ((), ())),
                                      precision=HIGHEST, preferred_element_type=F32)
        half = y.shape[1] // 2
        next(it)[...] = _pack_bf16_pair(y[:, :half], y[:, half:])


def _pack_bf16_pair(a, b):
    bits = lambda x: lax.bitcast_convert_type(x.astype(BF16), jnp.uint16).astype(jnp.int32)
    return bits(a) | lax.shift_left(bits(b), 16)


def _unpack_bf16_pair(w):
    return (lax.bitcast_convert_type(lax.shift_left(w, 16), F32),
            lax.bitcast_convert_type(w & -65536, F32))


def norm_mod(xs, g, *, n_rows, seq, mods=None, shift_idx=None, scale_idx=None, router_wt=None, out_dtype=BF16):
    d = xs.shape[1]
    tm = _token_tile(n_rows)
    row = pl.BlockSpec((tm, d), lambda i: (i, 0))
    in_specs, args = [row], [xs]
    if shift_idx is not None:
        in_specs.append(pl.BlockSpec((2, 6, d), lambda i: (0, 0, 0)))
        args.append(mods)
    in_specs.append(pl.BlockSpec((1, d), lambda i: (0, 0)))
    args.append(g.reshape(1, d))
    has_router = router_wt is not None
    if has_router:
        in_specs.append(pl.BlockSpec(router_wt.shape, lambda i: (0, 0)))
        args.append(router_wt)
    out_specs = [row]
    out_shape = [jax.ShapeDtypeStruct((n_rows, d), out_dtype)]
    if has_router:
        out_specs.append(pl.BlockSpec((N_EXPERTS, tm), lambda i: (0, i)))
        out_shape.append(jax.ShapeDtypeStruct((N_EXPERTS, n_rows), F32))
        out_specs.append(pl.BlockSpec((tm, d // 2), lambda i: (i, 0)))
        out_shape.append(jax.ShapeDtypeStruct((n_rows, d // 2), jnp.int32))
    return pl.pallas_call(
        functools.partial(_norm_mod_kernel, shift_idx=shift_idx, scale_idx=scale_idx, has_router=has_router, seq=seq),
        grid=(n_rows // tm,),
        in_specs=in_specs,
        out_specs=out_specs,
        out_shape=out_shape,
        compiler_params=_params(("parallel",)),
        name="norm_mod",
    )(*args)


def rope_tables(seq, ctx_len, head_dim):
    n_freq = head_dim // 4
    inv = ROPE_BASE ** (-jnp.arange(n_freq, dtype=F32) / n_freq)
    rows = seq // GRID_W
    row = jnp.repeat(jnp.arange(rows, dtype=F32), GRID_W)
    col = jnp.tile(jnp.arange(GRID_W, dtype=F32), rows)
    ang = jnp.concatenate([row[:, None] * inv, col[:, None] * inv], axis=-1)
    cos, sin = jnp.cos(ang), jnp.sin(ang)
    cos = jnp.concatenate([cos, cos], axis=-1)
    sin = jnp.concatenate([-sin, sin], axis=-1)
    reps = LANE // head_dim
    cos, sin = jnp.tile(cos, (1, reps)), jnp.tile(sin, (1, reps))
    cos = jnp.concatenate([cos, jnp.ones((ctx_len, LANE), F32)], axis=0)
    sin = jnp.concatenate([sin, jnp.zeros((ctx_len, LANE), F32)], axis=0)
    return cos, sin


def _ret_kernel(lg_ref, gc_ref, q_ref, k_ref, v_ref, *rest, reverse):
    if reverse:
        yf_ref, gate_ref, o_ref, s_ref = rest
    else:
        o_ref, s_ref = rest
    c = RET_CHUNK

    @pl.when(pl.program_id(0) == 0)
    def _():
        s_ref[...] = jnp.zeros_like(s_ref)

    ii = lax.broadcasted_iota(jnp.int32, (c, c), 0)
    jj = lax.broadcasted_iota(jnp.int32, (c, c), 1)
    rel = ((jj - ii) if reverse else (ii - jj)).astype(F32)
    pos = lax.broadcasted_iota(jnp.int32, (c, 1), 0).astype(F32)
    for h in range(RET_HEADS):
        lg = lg_ref[h]
        dec = jnp.where(rel >= 0, jnp.exp(jnp.maximum(rel, 0.0) * lg), 0.0)
        if reverse:
            q_dec = jnp.exp((c - pos) * lg)
            k_dec = jnp.exp(pos * lg)
        else:
            q_dec = jnp.exp((pos + 1.0) * lg)
            k_dec = jnp.exp((c - 1.0 - pos) * lg)
        state = s_ref[h]
        qk_cols = slice(h * RET_DK, (h + 1) * RET_DK)
        v_cols = slice(h * RET_DV, (h + 1) * RET_DV)
        for sub in (range(RET_STEP_CHUNKS - 1, -1, -1) if reverse else range(RET_STEP_CHUNKS)):
            rows = slice(sub * c, (sub + 1) * c)
            q = q_ref[rows, qk_cols]
            k = k_ref[rows, qk_cols]
            v = v_ref[rows, v_cols].astype(BF16)
            s = lax.dot_general(q.astype(BF16), k.astype(BF16), (((1,), (1,)), ((), ())),
                                preferred_element_type=F32) * dec
            y = jnp.dot(s.astype(BF16), v, preferred_element_type=F32)
            y = y + jnp.dot((q * q_dec).astype(BF16), state.astype(BF16), preferred_element_type=F32)
            upd = lax.dot_general((k * k_dec).astype(BF16), v, (((0,), (0,)), ((), ())), preferred_element_type=F32)
            state = gc_ref[h] * state + upd
            if reverse:
                r = y + yf_ref[rows, v_cols]
                mu = jnp.mean(r, axis=-1, keepdims=True)
                rc = r - mu
                var = jnp.mean(rc * rc, axis=-1, keepdims=True)
                g = gate_ref[rows, v_cols]
                o_ref[rows, v_cols] = (rc * lax.rsqrt(var + EPS) * (g * jax.nn.sigmoid(g))).astype(o_ref.dtype)
            else:
                o_ref[rows, v_cols] = y
        s_ref[h] = state


def retention(p, log_g, g_chunk, *, seq):
    n_rows = p.shape[0]
    step = RET_STEP_CHUNKS * RET_CHUNK
    assert seq % step == 0 and n_rows % step == 0
    n_steps = n_rows // step
    n_x = seq // step
    smem = pl.BlockSpec(memory_space=pltpu.SMEM)

    def run(reverse, extra):
        if reverse:
            idx = lambda t: n_steps - 1 - t
        else:
            idx = lambda t: (t + n_x) % n_steps
        in_specs = [smem, smem,
                    pl.BlockSpec((step, RET_QK), lambda t: (idx(t), 0)),
                    pl.BlockSpec((step, RET_QK), lambda t: (idx(t), 1)),
                    pl.BlockSpec((step, RET_V), lambda t: (idx(t), 1))]
        args = [log_g[1 if reverse else 0], g_chunk[1 if reverse else 0], p, p, p]
        if reverse:
            in_specs += [pl.BlockSpec((step, RET_V), lambda t: (idx(t), 0)),
                         pl.BlockSpec((step, RET_V), lambda t: (idx(t), 2))]
            args += list(extra)
        return pl.pallas_call(
            functools.partial(_ret_kernel, reverse=reverse),
            grid=(n_steps,),
            in_specs=in_specs,
            out_specs=pl.BlockSpec((step, RET_V), lambda t: (idx(t), 0)),
            out_shape=jax.ShapeDtypeStruct((n_rows, RET_V), BF16 if reverse else F32),
            scratch_shapes=[pltpu.VMEM((RET_HEADS, RET_DK, RET_DV), F32)],
            compiler_params=_params(("arbitrary",)),
            name="retention_bwd" if reverse else "retention_fwd",
        )(*args)

    y_fwd = run(False, None)
    return run(True, (y_fwd, p))


def _shortconv_kernel(cur_ref, prev_ref, next_ref, w_ref, b_ref, v_ref, x1_ref, x2_ref, *, x_tiles):
    i = pl.program_id(0)
    cur = cur_ref[...]
    rows = cur.shape[0]
    row = lax.broadcasted_iota(jnp.int32, (rows, 1), 0)
    has_prev = jnp.where((i == 0) | (i == x_tiles), 0.0, 1.0)
    has_next = jnp.where((i == x_tiles - 1) | (i == x_tiles), 0.0, 1.0)
    up = jnp.where(row == 0, prev_ref[SUBLANE - 1:SUBLANE, :] * has_prev, pltpu.roll(cur, 1, 0))
    dn = jnp.where(row == rows - 1, next_ref[0:1, :] * has_next, pltpu.roll(cur, rows - 1, 0))
    y = up * w_ref[0:1, :] + cur * w_ref[1:2, :] + dn * w_ref[2:3, :] + b_ref[...]
    v_ref[...] = y[:, :HY_WIDTH]
    x1_ref[...] = y[:, HY_WIDTH:2 * HY_WIDTH]
    x2_ref[...] = y[:, 2 * HY_WIDTH:]


def shortconv(p, w, b, *, seq):
    n_rows = p.shape[0]
    width = 3 * HY_WIDTH
    col = p.shape[1] // width - 1
    per = ROW_TILE // SUBLANE
    last = n_rows // SUBLANE - 1
    out = jax.ShapeDtypeStruct((n_rows, HY_WIDTH), F32)
    ospec = pl.BlockSpec((ROW_TILE, HY_WIDTH), lambda i: (i, 0))
    return pl.pallas_call(
        functools.partial(_shortconv_kernel, x_tiles=seq // ROW_TILE),
        grid=(n_rows // ROW_TILE,),
        in_specs=[pl.BlockSpec((ROW_TILE, width), lambda i: (i, col)),
                  pl.BlockSpec((SUBLANE, width), lambda i: (jnp.maximum(i * per - 1, 0), col)),
                  pl.BlockSpec((SUBLANE, width), lambda i: (jnp.minimum((i + 1) * per, last), col)),
                  pl.BlockSpec((3, width), lambda i: (0, 0)),
                  pl.BlockSpec((1, width), lambda i: (0, 0))],
        out_specs=[ospec, ospec, ospec],
        out_shape=[out, out, out],
        compiler_params=_params(("parallel",)),
        name="shortconv",
    )(p, p, p, w, b.reshape(1, width))


def _filt_kernel(z_ref, w1_ref, b1_ref, f1_ref, w2_ref, b2_ref, f2_ref, w3a_ref, w3b_ref, dl_ref, *o_ref):
    z = z_ref[...]
    h = jnp.sin(f1_ref[...] * (jnp.dot(z, w1_ref[...], precision=HIGHEST, preferred_element_type=F32) + b1_ref[...]))
    h = jnp.sin(f2_ref[...] * (jnp.dot(h, w2_ref[...], precision=HIGHEST, preferred_element_type=F32) + b2_ref[...]))
    window = jnp.exp(-z[:, 0:1] * dl_ref[...]) * z[:, HY_VALID_COL:HY_VALID_COL + 1]
    for o, w3_ref in enumerate((w3a_ref, w3b_ref)):
        o_ref[o][...] = jnp.dot(h, w3_ref[...], precision=HIGHEST, preferred_element_type=F32) * window


def hyena_filter_taps(length, w1, b1, f1, w2, b2, f2, w3):
    z = _filter_positions(length)
    w1p = jnp.zeros((HY_ZCOLS, HY_FFN), F32).at[:HY_EMB].set(w1)
    deltas = jnp.abs(jnp.linspace(math.log(HY_DECAY_TARGET) / HY_SLOW_DECAY,
                                  math.log(HY_DECAY_TARGET) / HY_FAST_DECAY, HY_WIDTH, dtype=F32)).reshape(1, HY_WIDTH)
    tm = min(length, 512)
    half_tiles = length // tm
    vec = lambda a: a.reshape(1, HY_FFN)
    small = lambda shape: pl.BlockSpec(shape, lambda i: (0, 0))
    w3_spec = lambda o: pl.BlockSpec((HY_FFN, HY_WIDTH), lambda i: (0, 2 * o + jnp.where(i >= half_tiles, 1, 0)))
    assert HY_ORDER == 2
    return pl.pallas_call(
        _filt_kernel,
        grid=(2 * half_tiles,),
        in_specs=[pl.BlockSpec((tm, HY_ZCOLS), lambda i: (i, 0)),
                  small((HY_ZCOLS, HY_FFN)), small((1, HY_FFN)), small((1, HY_FFN)),
                  small((HY_FFN, HY_FFN)), small((1, HY_FFN)), small((1, HY_FFN)),
                  w3_spec(0), w3_spec(1), small((1, HY_WIDTH))],
        out_specs=[pl.BlockSpec((tm, HY_WIDTH), lambda i: (i, 0))] * HY_ORDER,
        out_shape=[jax.ShapeDtypeStruct((2 * length, HY_WIDTH), F32)] * HY_ORDER,
        compiler_params=_params(("parallel",)),
        name="hyena_filter",
    )(z, w1p, vec(b1), vec(f1), w2, vec(b2), vec(f2), w3, w3, deltas)


def _filt2d_kernel(z_ref, w1_ref, b1_ref, f1_ref, w2_ref, b2_ref, f2_ref, w3fa_ref, w3ba_ref, w3fb_ref, w3bb_ref,
                   dl_ref, oa_ref, ob_ref):
    n1 = oa_ref.shape[0]
    c = dl_ref.shape[1]
    z = z_ref[...]
    h = jnp.sin(f1_ref[...] * (jnp.dot(z, w1_ref[...], precision=HIGHEST, preferred_element_type=F32) + b1_ref[...]))
    h = jnp.sin(f2_ref[...] * (jnp.dot(h, w2_ref[...], precision=HIGHEST, preferred_element_type=F32) + b2_ref[...]))
    window = jnp.exp(-z[:, 0:1] * dl_ref[...]) * z[:, HY_VALID_COL:HY_VALID_COL + 1]
    for o_ref, wf_ref, wb_ref in ((oa_ref, w3fa_ref, w3ba_ref), (ob_ref, w3fb_ref, w3bb_ref)):
        for j in range(SUBLANE):
            rows = slice(j * n1, (j + 1) * n1)
            hj = h[rows]
            taps = jnp.concatenate(
                [jnp.dot(hj[:n1 // 2], wf_ref[...], precision=HIGHEST, preferred_element_type=F32),
                 jnp.dot(hj[n1 // 2:], wb_ref[...], precision=HIGHEST, preferred_element_type=F32)], axis=0)
            o_ref[:, j * c:(j + 1) * c] = taps * window[rows]


def hyena_filter_taps_2d(length, w1, b1, f1, w2, b2, f2, w3):
    n1 = 2 * length // FFT_N2
    groups = FFT_N2 // SUBLANE
    z = _filter_positions(length).reshape(n1, groups, SUBLANE, HY_ZCOLS).transpose(1, 2, 0, 3)
    z = z.reshape(2 * length, HY_ZCOLS)
    w1p = jnp.zeros((HY_ZCOLS, HY_FFN), F32).at[:HY_EMB].set(w1)
    deltas = jnp.abs(jnp.linspace(math.log(HY_DECAY_TARGET) / HY_SLOW_DECAY,
                                  math.log(HY_DECAY_TARGET) / HY_FAST_DECAY, HY_WIDTH, dtype=F32)).reshape(1, HY_WIDTH)
    vec = lambda a: a.reshape(1, HY_FFN)
    small = lambda shape: pl.BlockSpec(shape, lambda i: (0, 0))
    w3_spec = lambda col: pl.BlockSpec((HY_FFN, HY_WIDTH), lambda i: (0, col))
    assert HY_ORDER == 2
    out = jax.ShapeDtypeStruct((n1, FFT_N2 * HY_WIDTH), F32)
    return pl.pallas_call(
        _filt2d_kernel,
        grid=(groups,),
        in_specs=[pl.BlockSpec((SUBLANE * n1, HY_ZCOLS), lambda i: (i, 0)),
                  small((HY_ZCOLS, HY_FFN)), small((1, HY_FFN)), small((1, HY_FFN)),
                  small((HY_FFN, HY_FFN)), small((1, HY_FFN)), small((1, HY_FFN)),
                  w3_spec(0), w3_spec(1), w3_spec(2), w3_spec(3), small((1, HY_WIDTH))],
        out_specs=[pl.BlockSpec((n1, SUBLANE * HY_WIDTH), lambda i: (0, i))] * HY_ORDER,
        out_shape=[out] * HY_ORDER,
        compiler_params=_params(("parallel",)),
        name="hyena_filter",
    )(z, w1p, vec(b1), vec(f1), w2, vec(b2), vec(f2), w3, w3, w3, w3, deltas)


def _filter_positions(length):
    t = jnp.concatenate([jnp.arange(length, dtype=F32), float(length) - jnp.arange(length, dtype=F32)])
    valid = jnp.ones((2 * length,), F32).at[length].set(0.0)
    t_norm = t / max(length - 1, 1)
    bands = jnp.linspace(1e-4, HY_BANDS - 1, HY_BANDS, dtype=F32)
    ang = (2.0 * math.pi / length) * t[:, None] * bands[None, :]
    z = jnp.concatenate([t_norm[:, None], jnp.cos(ang), -jnp.sin(ang), valid[:, None]], axis=-1)
    return jnp.pad(z, ((0, 0), (0, HY_ZCOLS - z.shape[1])))


def _angles(num, den):
    return (2.0 * math.pi / den) * (num % den).astype(F32)


def dft_tables_two_stage(m):
    n2 = FFT_N2
    n1 = m // n2
    half = n1 // 2
    kp = -(-(half + 1) // SUBLANE) * SUBLANE
    k1 = jnp.arange(kp, dtype=jnp.int32)
    live = (k1 <= half)
    a1 = _angles(k1[:, None] * jnp.arange(n1, dtype=jnp.int32)[None, :], n1)
    f1 = jnp.concatenate([jnp.where(live[:, None], jnp.cos(a1), 0.0), jnp.where(live[:, None], -jnp.sin(a1), 0.0)], 0)
    wgt = jnp.where((k1 == 0) | (k1 == half), 1.0, 2.0) * live / m
    a1h = a1[:, :half].T
    cinv = jnp.concatenate([jnp.cos(a1h) * wgt[None, :], -jnp.sin(a1h) * wgt[None, :]], axis=1)
    k = k1[:, None, None] + n1 * jnp.arange(n2, dtype=jnp.int32)[None, :, None]
    th = _angles(k * jnp.arange(n2, dtype=jnp.int32)[None, None, :], m)
    c = jnp.where(live[:, None, None], jnp.cos(th), 0.0)
    s = jnp.where(live[:, None, None], jnp.sin(th), 0.0)
    g_fwd = jnp.concatenate([jnp.concatenate([c, s], 2), jnp.concatenate([-s, c], 2)], 1)
    ct, st = jnp.swapaxes(c, 1, 2), jnp.swapaxes(s, 1, 2)
    g_inv = jnp.concatenate([jnp.concatenate([ct, -st], 2), jnp.concatenate([st, ct], 2)], 1)
    return dict(n1=n1, kp=kp, f1=f1.astype(BF16), f1_half=f1[:, :half].astype(BF16), cinv=cinv.astype(BF16),
                g_fwd=g_fwd.astype(BF16), g_inv=g_inv.astype(BF16))


def dft_tables_one_stage(m):
    half = m // 2
    kp = -(-(half + 1) // SUBLANE) * SUBLANE
    k = jnp.arange(kp, dtype=jnp.int32)
    live = (k <= half)
    a = _angles(k[:, None] * jnp.arange(m, dtype=jnp.int32)[None, :], m)
    f = jnp.concatenate([jnp.where(live[:, None], jnp.cos(a), 0.0), jnp.where(live[:, None], -jnp.sin(a), 0.0)], 0)
    wgt = jnp.where((k == 0) | (k == half), 1.0, 2.0) * live / m
    ah = a[:, :half].T
    cinv = jnp.concatenate([jnp.cos(ah) * wgt[None, :], -jnp.sin(ah) * wgt[None, :]], axis=1)
    return dict(kp=kp, f=f.astype(BF16), f_half=f[:, :half].astype(BF16), cinv=cinv.astype(BF16))


def _bmm_kernel(*refs, kb, in_part_major, out_part_major, has_h):
    if has_h:
        g_ref, a_ref, h_ref, o_ref = refs
    else:
        g_ref, a_ref, o_ref = refs
    n2 = FFT_N2
    for b in range(kb):
        if in_part_major:
            ar, ai = a_ref[0, b], a_ref[1, b]
        else:
            ar, ai = a_ref[b, 0], a_ref[b, 1]
        if has_h:
            hr, hi = h_ref[b, 0], h_ref[b, 1]
            ar, ai = ar * hr - ai * hi, ar * hi + ai * hr
        xin = jnp.concatenate([ar, ai], axis=0).astype(BF16)
        y = jnp.dot(g_ref[b], xin, preferred_element_type=F32)
        if out_part_major:
            o_ref[0, b] = y[:n2].astype(o_ref.dtype)
            o_ref[1, b] = y[n2:].astype(o_ref.dtype)
        else:
            o_ref[b, 0] = y[:n2].astype(o_ref.dtype)
            o_ref[b, 1] = y[n2:].astype(o_ref.dtype)


def bmm_k1(g, a, h=None, *, in_part_major, out_part_major):
    kp = g.shape[0]
    n2 = FFT_N2
    c = a.shape[-1]
    kb, tc = SUBLANE, min(c, 512)
    pm = lambda: pl.BlockSpec((2, kb, n2, tc), lambda i, j: (0, i, 0, j))
    km = lambda: pl.BlockSpec((kb, 2, n2, tc), lambda i, j: (i, 0, 0, j))
    in_specs = [pl.BlockSpec((kb, 2 * n2, 2 * n2), lambda i, j: (i, 0, 0)), pm() if in_part_major else km()]
    args = [g, a]
    if h is not None:
        in_specs.append(km())
        args.append(h)
    return pl.pallas_call(
        functools.partial(_bmm_kernel, kb=kb, in_part_major=in_part_major, out_part_major=out_part_major,
                          has_h=h is not None),
        grid=(kp // kb, c // tc),
        in_specs=in_specs,
        out_specs=pm() if out_part_major else km(),
        out_shape=jax.ShapeDtypeStruct((2, kp, n2, c), BF16) if out_part_major else
        jax.ShapeDtypeStruct((kp, 2, n2, c), F32),
        compiler_params=_params(("parallel", "parallel")),
        name="dft_inner",
    )(*args)


def _cmul_kernel(x_ref, h_ref, o_ref):
    xr, xi, hr, hi = x_ref[0], x_ref[1], h_ref[0], h_ref[1]
    o_ref[0] = xr * hr - xi * hi
    o_ref[1] = xr * hi + xi * hr


def cmul(x, h):
    spec = pl.BlockSpec(x.shape, lambda i: (0, 0, 0))
    return pl.pallas_call(_cmul_kernel, grid=(1,), in_specs=[spec, spec], out_specs=spec,
                          out_shape=jax.ShapeDtypeStruct(x.shape, F32), compiler_params=_params(("arbitrary",)),
                          name="spectrum_product")(x, h)


def _dft_outer3_kernel(f_ref, x_ref, o_ref):
    c = x_ref.shape[2]
    f = f_ref[...]
    for j in range(SUBLANE):
        o_ref[:, j * c:(j + 1) * c] = jnp.dot(f, x_ref[:, j, :].astype(BF16),
                                              preferred_element_type=F32).astype(o_ref.dtype)


def dft_outer3(f, x3, n_outer):
    rows = f.shape[0]
    c = x3.shape[2]
    return pl.pallas_call(
        _dft_outer3_kernel,
        grid=(FFT_N2 // SUBLANE,),
        in_specs=[pl.BlockSpec((rows, n_outer), lambda j: (0, 0)),
                  pl.BlockSpec((n_outer, SUBLANE, c), lambda j: (0, j, 0))],
        out_specs=pl.BlockSpec((rows, SUBLANE * c), lambda j: (0, j)),
        out_shape=jax.ShapeDtypeStruct((rows, FFT_N2 * c), BF16),
        compiler_params=_params(("parallel",)),
        name="dft_outer",
    )(f, x3)


def _idft_gate3_kernel(cinv_ref, b_ref, gate_ref, skip_ref, u_ref, o_ref, *, u_is_3d):
    c = gate_ref.shape[2]
    cinv = cinv_ref[...]
    for j in range(SUBLANE):
        cols = slice(j * c, (j + 1) * c)
        acc = jnp.dot(cinv, b_ref[:, cols].astype(BF16), preferred_element_type=F32)
        u = u_ref[:, j, :] if u_is_3d else u_ref[:, cols]
        o_ref[:, cols] = gate_ref[:, j, :] * (acc + skip_ref[...] * u)


def idft_gate3(cinv, b2d, gate3, skip_row, u):
    n_outer = cinv.shape[0]
    c = gate3.shape[2]
    u_is_3d = u.ndim == 3
    wide = pl.BlockSpec((n_outer, SUBLANE * c), lambda j: (0, j))
    slab = pl.BlockSpec((n_outer, SUBLANE, c), lambda j: (0, j, 0))
    return pl.pallas_call(
        functools.partial(_idft_gate3_kernel, u_is_3d=u_is_3d),
        grid=(FFT_N2 // SUBLANE,),
        in_specs=[pl.BlockSpec(cinv.shape, lambda j: (0, 0)),
                  pl.BlockSpec((b2d.shape[0], SUBLANE * c), lambda j: (0, j)),
                  slab, pl.BlockSpec((1, c), lambda j: (0, 0)), slab if u_is_3d else wide],
        out_specs=wide,
        out_shape=jax.ShapeDtypeStruct((n_outer, FFT_N2 * c), F32),
        compiler_params=_params(("parallel",)),
        name="idft_outer_gate",
    )(cinv, b2d, gate3, skip_row, u)


def long_conv_two_stage(tabs, taps, v, x1, x2, skip, length):
    c = v.shape[1]
    n2, n1, kp = FFT_N2, tabs["n1"], tabs["kp"]
    as3 = lambda a: a.reshape(a.shape[0] // n2, n2, c)
    spectrum = lambda a2d: bmm_k1(tabs["g_fwd"], a2d.reshape(2, kp, n2, c), in_part_major=True, out_part_major=False)

    spectra = [spectrum(mm([(tabs["f1"], taps[o])], BF16, 2 * kp, 2048, name="dft_outer")) for o in range(HY_ORDER)]
    v3 = as3(v)
    u = v3
    for o, gate in enumerate((x1, x2)):
        if u.ndim == 3:
            a = dft_outer3(tabs["f1_half"], u, n1 // 2)
        else:
            a = mm([(tabs["f1_half"], u)], BF16, 2 * kp, 2048, name="dft_outer")
        bt = bmm_k1(tabs["g_inv"], spectrum(a), spectra[o], in_part_major=False, out_part_major=True)
        u = idft_gate3(tabs["cinv"], bt.reshape(2 * kp, n2 * c), as3(gate), skip[o].reshape(1, c), u)
    return u.reshape(length, c)


def long_conv_one_stage(tabs, taps, v, x1, x2, skip):
    length, c = v.shape
    kp = tabs["kp"]
    u = v
    for o, gate in enumerate((x1, x2)):
        hs = mm([(tabs["f"], taps[o])], F32, 2 * kp, c, name="ctx_dft").reshape(2, kp, c)
        xs = mm([(tabs["f_half"], u)], F32, 2 * kp, c, name="ctx_dft").reshape(2, kp, c)
        ys = cmul(xs, hs).reshape(2 * kp, c)
        u = mm([(tabs["cinv"], ys)], F32, length, c, epi=(gate, skip[o].reshape(1, c), u), name="ctx_idft_gate")
    return u


def _flash_kernel(lam_ref, qt_ref, k_ref, vt_ref, sub_ref, o_ref, m_ref, excess_ref, acc_ref, *, kv, seq, ctx_len,
                  out_scale):
    i = pl.program_id(1)
    last_q = pl.num_programs(1) - 1
    tq = qt_ref.shape[1]
    d = DA_HEAD_DIM
    dv = 2 * DA_HEAD_DIM
    n_chunks = k_ref.shape[0] // kv
    acc_ref[...] = jnp.zeros_like(acc_ref)

    def scores(off, rows, c, masked):
        s = jnp.dot(k_ref[pl.ds(off, rows), c * d:(c + 1) * d], qt_ref[c * d:(c + 1) * d, :],
                    preferred_element_type=F32)
        if masked:
            key = off + lax.broadcasted_iota(jnp.int32, (rows, 1), 0)
            lane = lax.broadcasted_iota(jnp.int32, (1, tq), 1)
            s = s + jnp.where(key < seq, NEG_BIG, 0.0) * jnp.where(lane >= tq - ctx_len, 1.0, 0.0)
        return s

    def exact_step(off, c, masked):
        s = scores(off, kv, c, masked)
        m_old = m_ref[c]
        m_new = jnp.maximum(m_old, jnp.max(s, axis=0, keepdims=True))
        pr = jnp.exp2(s - m_new).astype(BF16)
        acc_ref[c] = jnp.exp2(m_old - m_new) * acc_ref[c] + jnp.dot(vt_ref[:, pl.ds(off, kv)], pr,
                                                                   preferred_element_type=F32)
        m_ref[c] = m_new

    def lazy_step(off, c, masked):
        s = scores(off, kv, c, masked)
        m_old = m_ref[c]
        m_chunk = jnp.max(s, axis=0, keepdims=True)
        pv = jnp.dot(vt_ref[:, pl.ds(off, kv)], jnp.exp2(s - m_old).astype(BF16), preferred_element_type=F32)
        m_new = jnp.maximum(m_old, m_chunk)
        acc_ref[c] = jnp.exp2(m_old - m_new) * (acc_ref[c] + pv)
        m_ref[c] = m_new
        excess_ref[c] = jnp.maximum(excess_ref[c], m_chunk - m_old)

    def all_chunks(step, masked):
        def body(kc, carry):
            off = pl.multiple_of(kc * kv, kv)
            for c in range(2):
                step(off, c, masked)
            return carry

        lax.fori_loop(0, n_chunks, body, 0)

    def run(masked):
        for c in range(2):
            m0 = jnp.max(scores(0, FLASH_INIT_KEYS, c, False), axis=0, keepdims=True)
            if masked:
                lane = lax.broadcasted_iota(jnp.int32, (1, tq), 1)
                m_ctx = jnp.max(scores(seq, FLASH_INIT_KEYS, c, False), axis=0, keepdims=True)
                m0 = jnp.where(lane >= tq - ctx_len, m_ctx, m0)
            m_ref[c] = m0
        excess_ref[...] = jnp.full_like(excess_ref, NEG_BIG)
        all_chunks(lazy_step, masked)

        @pl.when(jnp.max(excess_ref[...]) > FLASH_LAZY_HEADROOM)
        def _():
            m_ref[...] = jnp.full_like(m_ref, NEG_BIG)
            acc_ref[...] = jnp.zeros_like(acc_ref)
            all_chunks(exact_step, masked)

    @pl.when(i != last_q)
    def _():
        run(False)

    @pl.when(i == last_q)
    def _():
        run(True)

    a0 = acc_ref[0, :dv, :] / acc_ref[0, dv:dv + 1, :]
    a1 = acc_ref[1, :dv, :] / acc_ref[1, dv:dv + 1, :]
    o = (a0 - lam_ref[0] * a1).T
    o = o * lax.rsqrt(jnp.mean(o * o, axis=-1, keepdims=True) + 1e-5) * sub_ref[...]
    o_ref[...] = (o * out_scale).astype(o_ref.dtype)


def _rope_da_kernel(p_ref, cos_ref, sin_ref, qt_ref, k_ref, vt_ref):
    cos = cos_ref[...]
    sin = sin_ref[...]
    hw = 2 * DA_HEAD_DIM
    lane = lax.broadcasted_iota(jnp.int32, cos.shape, 1)
    first_half = (lane % DA_HEAD_DIM) < DA_HEAD_DIM // 2

    def rotated(b):
        x = p_ref[:, b * LANE:(b + 1) * LANE]
        rot = jnp.where(first_half, pltpu.roll(x, LANE - DA_HEAD_DIM // 2, 1), pltpu.roll(x, DA_HEAD_DIM // 2, 1))
        return x * cos + rot * sin

    ones = jnp.ones((FLASH_ONES_ROWS, cos.shape[0]), BF16)
    for h in range(DA_HEADS):
        qt_ref[h * hw:(h + 1) * hw, :] = (rotated(h) * (LOG2_E * DA_HEAD_DIM ** -0.5)).T.astype(BF16)
        k_ref[:, h * hw:(h + 1) * hw] = rotated(DA_HEADS + h).astype(BF16)
        base = h * (hw + FLASH_ONES_ROWS)
        vt_ref[base:base + hw, :] = p_ref[:, (2 * DA_HEADS + h) * LANE:(2 * DA_HEADS + h + 1) * LANE].T.astype(BF16)
        vt_ref[base + hw:base + hw + FLASH_ONES_ROWS, :] = ones


def rope_da(p, cos, sin):
    n_rows = p.shape[0]
    assert 2 * DA_HEAD_DIM == LANE
    vt_rows = DA_HEADS * (LANE + FLASH_ONES_ROWS)
    return pl.pallas_call(
        _rope_da_kernel,
        grid=(n_rows // ROW_TILE,),
        in_specs=[pl.BlockSpec((ROW_TILE, 3 * DA_WIDTH), lambda i: (i, 0)),
                  pl.BlockSpec((ROW_TILE, LANE), lambda i: (i, 0)),
                  pl.BlockSpec((ROW_TILE, LANE), lambda i: (i, 0))],
        out_specs=[pl.BlockSpec((DA_WIDTH, ROW_TILE), lambda i: (0, i)),
                   pl.BlockSpec((ROW_TILE, DA_WIDTH), lambda i: (i, 0)),
                   pl.BlockSpec((vt_rows, ROW_TILE), lambda i: (0, i))],
        out_shape=[jax.ShapeDtypeStruct((DA_WIDTH, n_rows), BF16),
                   jax.ShapeDtypeStruct((n_rows, DA_WIDTH), BF16),
                   jax.ShapeDtypeStruct((vt_rows, n_rows), BF16)],
        compiler_params=_params(("parallel",)),
        name="rope_da",
    )(p, cos, sin)


def diff_attention(qt, k, vt, lam_full, subln, *, seq, ctx_len, lambda_init):
    n_rows = k.shape[0]
    tq = _token_tile(n_rows)
    hw = 2 * DA_HEAD_DIM
    ones_rows = FLASH_ONES_ROWS
    return pl.pallas_call(
        functools.partial(_flash_kernel, kv=tq, seq=seq, ctx_len=ctx_len, out_scale=1.0 - lambda_init),
        grid=(DA_HEADS, n_rows // tq),
        in_specs=[pl.BlockSpec(memory_space=pltpu.SMEM),
                  pl.BlockSpec((hw, tq), lambda h, i: (h, i)),
                  pl.BlockSpec((n_rows, hw), lambda h, i: (0, h)),
                  pl.BlockSpec((hw + ones_rows, n_rows), lambda h, i: (h, 0)),
                  pl.BlockSpec((1, hw), lambda h, i: (0, 0))],
        out_specs=pl.BlockSpec((tq, hw), lambda h, i: (i, h)),
        out_shape=jax.ShapeDtypeStruct((n_rows, DA_WIDTH), BF16),
        scratch_shapes=[pltpu.VMEM((2, 1, tq), F32), pltpu.VMEM((2, 1, tq), F32),
                        pltpu.VMEM((2, hw + ones_rows, tq), F32)],
        compiler_params=pltpu.CompilerParams(dimension_semantics=("parallel", "parallel"),
                                             vmem_limit_bytes=FLASH_VMEM_LIMIT),
        name="diff_attention",
    )(lam_full.reshape(1), qt, k, vt, subln.reshape(1, hw))


def _route_kernel(lg_ref, b_ref, tri_ref, eidx_ref, w_ref, rank_ref, cnt_ref, carry_ref):
    t = lg_ref.shape[1]

    @pl.when(pl.program_id(0) == 0)
    def _():
        carry_ref[...] = jnp.zeros_like(carry_ref)

    scores = jax.nn.sigmoid(lg_ref[...])
    choice = (scores + b_ref[...]).reshape(N_GROUPS, GROUP_SIZE, t)
    s3 = scores.reshape(N_GROUPS, GROUP_SIZE, t)
    member = lax.broadcasted_iota(jnp.int32, choice.shape, 1)
    group = lax.broadcasted_iota(jnp.int32, (N_GROUPS, 1, t), 0)
    expert = lax.broadcasted_iota(jnp.int32, choice.shape, 0) * GROUP_SIZE + member
    neg_inf = -jnp.inf
    m1 = jnp.max(choice, axis=1, keepdims=True)
    first = jnp.min(jnp.where(choice == m1, member, GROUP_SIZE), axis=1, keepdims=True)
    m2 = jnp.max(jnp.where(member == first, neg_inf, choice), axis=1, keepdims=True)
    gscore = m1 + m2
    gsel = jnp.zeros(gscore.shape, F32)
    for _ in range(TOPK_GROUPS):
        m = jnp.max(gscore, axis=0, keepdims=True)
        f = jnp.min(jnp.where(gscore == m, group, N_GROUPS), axis=0, keepdims=True)
        hit = group == f
        gsel = jnp.where(hit, 1.0, gsel)
        gscore = jnp.where(hit, neg_inf, gscore)
    cand = jnp.where(gsel > 0.0, choice, neg_inf)
    esel = jnp.zeros(choice.shape, F32)
    picks = []
    for _ in range(TOP_K):
        m = jnp.max(jnp.max(cand, axis=1, keepdims=True), axis=0, keepdims=True)
        f = jnp.min(jnp.min(jnp.where(cand == m, expert, N_EXPERTS), axis=1, keepdims=True), axis=0, keepdims=True)
        hit = expert == f
        esel = jnp.where(hit, 1.0, esel)
        cand = jnp.where(hit, neg_inf, cand)
        picks.append(f)
    w = s3 * esel
    denom = jnp.sum(jnp.sum(w, axis=1, keepdims=True), axis=0, keepdims=True) + 1e-20
    w = w / denom * ROUTED_SCALE
    sel = esel.reshape(N_EXPERTS, t)
    before = jnp.dot(sel.astype(BF16), tri_ref[...], preferred_element_type=F32) + carry_ref[...]
    before = before.reshape(N_GROUPS, GROUP_SIZE, t)
    pick = lambda a, hit: jnp.sum(jnp.sum(jnp.where(hit, a, 0.0), axis=1, keepdims=True), axis=0).reshape(1, t)
    for k, f in enumerate(picks):
        hit = expert == f
        eidx_ref[k:k + 1, :] = f.reshape(1, t)
        w_ref[k:k + 1, :] = pick(w, hit)
        rank_ref[k:k + 1, :] = pick(before, hit).astype(jnp.int32)
    carry_ref[...] += jnp.sum(sel, axis=1, keepdims=True)
    cnt_ref[...] = carry_ref[...]


def route(logits_t, bias, lo, hi):
    t = _token_tile(logits_t.shape[1])
    n = hi - lo
    tile0 = lo // t
    tri = (jnp.arange(t)[:, None] < jnp.arange(t)[None, :]).astype(BF16)
    tok = lambda dt: jax.ShapeDtypeStruct((TOP_K, n), dt)
    tok_spec = pl.BlockSpec((TOP_K, t), lambda i: (0, i))
    return pl.pallas_call(
        _route_kernel,
        grid=(n // t,),
        in_specs=[pl.BlockSpec((N_EXPERTS, t), lambda i: (0, tile0 + i)),
                  pl.BlockSpec((N_EXPERTS, 1), lambda i: (0, 0)),
                  pl.BlockSpec((t, t), lambda i: (0, 0))],
        out_specs=[tok_spec, tok_spec, tok_spec, pl.BlockSpec((N_EXPERTS, 1), lambda i: (0, 0))],
        out_shape=[tok(jnp.int32), tok(F32), tok(jnp.int32), jax.ShapeDtypeStruct((N_EXPERTS, 1), F32)],
        scratch_shapes=[pltpu.VMEM((N_EXPERTS, 1), F32)],
        compiler_params=_params(("arbitrary",)),
        name="route",
    )(logits_t, bias.reshape(N_EXPERTS, 1), tri)


def _slot_kernel(start_ref, eidx_ref, rank_ref, dest_ref):
    e = eidx_ref[...]
    d = rank_ref[...]
    for x in range(N_EXPERTS):
        d = d + jnp.where(e == x, start_ref[x], 0)
    dest_ref[...] = d


def slot_index(pad_start, eidx, rank):
    n = eidx.shape[1]
    t = _token_tile(n)
    spec = pl.BlockSpec((TOP_K, t), lambda i: (0, i))
    return pl.pallas_call(
        _slot_kernel,
        grid=(n // t,),
        in_specs=[pl.BlockSpec(memory_space=pltpu.SMEM), spec, spec],
        out_specs=spec,
        out_shape=jax.ShapeDtypeStruct((TOP_K, n), jnp.int32),
        compiler_params=_params(("parallel",)),
        name="slot_index",
    )(pad_start, eidx, rank)


def _sc_worker():
    return lax.axis_index("s") * SC_CORES + lax.axis_index("c")


def sc_dispatch(h, dest3, n_slots, row0):
    d = h.shape[1]
    n_win = dest3.shape[0]
    mesh = plsc.VectorSubcoreMesh(core_axis_name="c", subcore_axis_name="s")

    @functools.partial(
        pl.kernel, mesh=mesh, out_type=jax.ShapeDtypeStruct((n_slots, d), h.dtype),
        scratch_types=[pltpu.VMEM((TOP_K, SC_WINDOW), jnp.int32), pltpu.VMEM((SC_WINDOW, d), h.dtype),
                       pltpu.SemaphoreType.DMA])
    def k(h_hbm, dest_hbm, out_hbm, idx_v, rows_v, sem):
        wid = _sc_worker()

        @pl.loop(0, -(-n_win // SC_WORKERS))
        def _(it):
            w = it * SC_WORKERS + wid

            @pl.when(w < n_win)
            def _():
                pltpu.sync_copy(dest_hbm.at[w], idx_v)
                pltpu.sync_copy(h_hbm.at[pl.ds(row0 + w * SC_WINDOW, SC_WINDOW)], rows_v)
                copies = [pltpu.async_copy(rows_v, out_hbm.at[idx_v.at[j]], sem) for j in range(TOP_K)]
                for c in copies:
                    c.wait()

    return k(h, dest3)


def sc_combine_gather(y, dest3):
    d = y.shape[1]
    n_win = dest3.shape[0]
    n = n_win * SC_WINDOW
    mesh = plsc.VectorSubcoreMesh(core_axis_name="c", subcore_axis_name="s")

    @functools.partial(
        pl.kernel, mesh=mesh, out_type=jax.ShapeDtypeStruct((TOP_K, n, d), y.dtype),
        scratch_types=[pltpu.VMEM((TOP_K, SC_WINDOW), jnp.int32), pltpu.VMEM((2, SC_WINDOW, d), y.dtype),
                       pltpu.SemaphoreType.DMA, pltpu.SemaphoreType.DMA,
                       pltpu.SemaphoreType.DMA, pltpu.SemaphoreType.DMA])
    def k(y_hbm, dest_hbm, out_hbm, idx_v, rows_v, gsem0, gsem1, osem0, osem1):
        wid = _sc_worker()
        gsem, osem = (gsem0, gsem1), (osem0, osem1)

        @pl.loop(0, -(-n_win // SC_WORKERS))
        def _(it):
            w = it * SC_WORKERS + wid

            @pl.when(w < n_win)
            def _():
                pltpu.sync_copy(dest_hbm.at[w], idx_v)
                gather = lambda j: pltpu.async_copy(y_hbm.at[idx_v.at[j]], rows_v.at[j % 2], gsem[j % 2])
                g = [None] * TOP_K
                o = [None] * TOP_K
                g[0] = gather(0)
                for j in range(TOP_K):
                    if j + 1 < TOP_K:
                        if j >= 1:
                            o[j - 1].wait()
                        g[j + 1] = gather(j + 1)
                    g[j].wait()
                    o[j] = pltpu.async_copy(rows_v.at[j % 2], out_hbm.at[j, pl.ds(w * SC_WINDOW, SC_WINDOW)],
                                            osem[j % 2])
                o[TOP_K - 2].wait()
                o[TOP_K - 1].wait()

    return k(y, dest3)


def _expert_ffn_kernel(be_ref, bv_ref, bs_ref, x_ref, wg_ref, wu_ref, wd_ref, o_ref, wg_s, wu_s, wd_s):
    b = pl.program_id(0)
    valid = bv_ref[b]
    new_expert = (b == 0) | (be_ref[b] != be_ref[jnp.maximum(b - 1, 0)])

    @pl.when(new_expert)
    def _():
        wg_s[...] = wg_ref[0, 0].astype(BF16)
        wu_s[...] = wu_ref[0, 0].astype(BF16)
        wd_s[...] = wd_ref[0, 0].astype(BF16)

    sub = x_ref.shape[0] // MOE_SUB_BLOCKS

    def sub_block(r):
        row = lax.broadcasted_iota(jnp.int32, (sub, 1), 0) + r * sub
        rows = pl.ds(r * sub, sub)
        lo, hi = _unpack_bf16_pair(jnp.where(row < valid, x_ref[rows, :], 0))
        x = jnp.concatenate([lo.astype(BF16), hi.astype(BF16)], axis=1)
        a = jnp.dot(x, wg_s[...], preferred_element_type=F32)
        a = a * jax.nn.sigmoid(a) * jnp.dot(x, wu_s[...], preferred_element_type=F32)
        y = jnp.dot(a.astype(BF16), wd_s[...], preferred_element_type=F32)
        half = y.shape[1] // 2
        o_ref[rows, :] = _pack_bf16_pair(y[:, :half], y[:, half:])

    for live in range(1, MOE_SUB_BLOCKS + 1):
        upper = valid <= live * sub if live < MOE_SUB_BLOCKS else True

        @pl.when((valid > (live - 1) * sub) & upper)
        def _():
            for r in range(live):
                sub_block(r)


def expert_ffn(xg, block_expert, block_valid, block_src, wg, wu, wd, layer):
    n_slots, dp = xg.shape
    d, f = wg.shape[-2:]
    grid_spec = pltpu.PrefetchScalarGridSpec(
        num_scalar_prefetch=3,
        grid=(n_slots // MOE_BLOCK,),
        in_specs=[pl.BlockSpec((MOE_BLOCK, dp), lambda b, be, bv, bs: (bs[b], 0)),
                  pl.BlockSpec((1, 1, d, f), lambda b, be, bv, bs: (layer, be[b], 0, 0)),
                  pl.BlockSpec((1, 1, d, f), lambda b, be, bv, bs: (layer, be[b], 0, 0)),
                  pl.BlockSpec((1, 1, f, d), lambda b, be, bv, bs: (layer, be[b], 0, 0))],
        out_specs=pl.BlockSpec((MOE_BLOCK, dp), lambda b, be, bv, bs: (bs[b], 0)),
        scratch_shapes=[pltpu.VMEM((d, f), BF16), pltpu.VMEM((d, f), BF16), pltpu.VMEM((f, d), BF16)],
    )
    return pl.pallas_call(
        _expert_ffn_kernel,
        grid_spec=grid_spec,
        out_shape=jax.ShapeDtypeStruct((n_slots, dp), jnp.int32),
        compiler_params=_params(("arbitrary",)),
        name="expert_ffn",
    )(block_expert, block_valid, block_src, xg, wg, wu, wd)


def _combine_kernel(yg_ref, w_ref, h_ref, swg_ref, swu_ref, swd_ref, xs_ref, mods_ref, *rest, tile0, gate_idx, seq):
    o_ref = rest[-1]
    h = h_ref[...]
    a = jnp.dot(h, swg_ref[...], preferred_element_type=F32)
    a = a * jax.nn.sigmoid(a) * jnp.dot(h, swu_ref[...], preferred_element_type=F32)
    acc = jnp.dot(a.astype(BF16), swd_ref[...], preferred_element_type=F32)
    half = acc.shape[1] // 2
    acc_lo, acc_hi = acc[:, :half], acc[:, half:]
    wt = w_ref[...].T
    for k in range(TOP_K):
        lo, hi = _unpack_bf16_pair(yg_ref[k])
        acc_lo = acc_lo + wt[:, k:k + 1] * lo
        acc_hi = acc_hi + wt[:, k:k + 1] * hi
    tm = h.shape[0]
    gate = _row_mod(mods_ref, gate_idx, (tile0 + pl.program_id(0)) * tm, tm, seq)
    o_ref[:, :half] = xs_ref[:, :half] + gate[:, :half] * acc_lo
    o_ref[:, half:] = xs_ref[:, half:] + gate[:, half:] * acc_hi


def combine(yg, w, h, swg, swu, swd, residual, lo, prev):
    xs, mods, gate_idx, seq = residual
    n_all, d = h.shape
    n = w.shape[1]
    f = swg.shape[-1]
    tm = ROW_TILE
    tile0 = lo // tm
    in_specs = [pl.BlockSpec((TOP_K, tm, d // 2), lambda i: (0, i, 0)),
                pl.BlockSpec((TOP_K, tm), lambda i: (0, i)),
                pl.BlockSpec((tm, d), lambda i: (tile0 + i, 0)),
                pl.BlockSpec((d, f), lambda i: (0, 0)),
                pl.BlockSpec((d, f), lambda i: (0, 0)),
                pl.BlockSpec((f, d), lambda i: (0, 0)),
                pl.BlockSpec((tm, d), lambda i: (tile0 + i, 0)),
                pl.BlockSpec((2, 6, d), lambda i: (0, 0, 0))]
    args = [yg, w, h, swg, swu, swd, xs, mods]
    aliases = {}
    if prev is not None:
        in_specs.append(pl.BlockSpec(memory_space=pl.ANY))
        args.append(prev)
        aliases = {len(args) - 1: 0}
    return pl.pallas_call(
        functools.partial(_combine_kernel, tile0=tile0, gate_idx=gate_idx, seq=seq),
        grid=(n // tm,),
        in_specs=in_specs,
        out_specs=pl.BlockSpec((tm, d), lambda i: (tile0 + i, 0)),
        out_shape=jax.ShapeDtypeStruct((n_all, d), F32),
        input_output_aliases=aliases,
        compiler_params=_params(("parallel",)),
        name="moe_combine",
    )(*args)


def moe(h, h_packed, logits_t, bias, wg, wu, wd, layer, swg, swu, swd, residual):
    n = h.shape[0]
    t = _token_tile(n)
    cut = (n // t + 1) // 2 * t
    shared = (swg.astype(BF16), swu.astype(BF16), swd.astype(BF16))
    staged = [_moe_experts(h_packed, logits_t, bias, wg, wu, wd, layer, lo, hi) for lo, hi in ((0, cut), (cut, n))]
    out = None
    for (yg, w), lo in zip(staged, (0, cut)):
        out = combine(yg, w, h, *shared, residual, lo, out)
    return out


def _moe_experts(h_packed, logits_t, bias, wg, wu, wd, layer, lo, hi):
    n = hi - lo
    eidx, w, rank, counts = route(logits_t, bias, lo, hi)
    counts = counts.reshape(N_EXPERTS).astype(jnp.int32)
    padded = (counts + MOE_BLOCK - 1) // MOE_BLOCK * MOE_BLOCK
    pad_end = jnp.cumsum(padded)
    pad_start = pad_end - padded
    n_slots = n * TOP_K + N_EXPERTS * MOE_BLOCK
    starts = jnp.arange(n_slots // MOE_BLOCK, dtype=jnp.int32) * MOE_BLOCK
    owner = jnp.sum((pad_end[None, :] <= starts[:, None]).astype(jnp.int32), axis=1)
    block_src = jnp.minimum(jnp.arange(starts.shape[0], dtype=jnp.int32), jnp.maximum(pad_end[-1] // MOE_BLOCK - 1, 0))
    owner = owner[block_src]
    block_expert = jnp.minimum(owner, N_EXPERTS - 1)
    member = (block_expert[:, None] == jnp.arange(N_EXPERTS, dtype=jnp.int32)[None, :]).astype(jnp.int32)
    left = jnp.sum(member * (counts + pad_start)[None, :], axis=1) - starts
    block_valid = jnp.clip(left, 0, MOE_BLOCK).astype(jnp.int32)
    dest = slot_index(pad_start.astype(jnp.int32), eidx, rank)
    dest3 = dest.reshape(TOP_K, n // SC_WINDOW, SC_WINDOW).transpose(1, 0, 2)
    xg = sc_dispatch(h_packed, dest3, n_slots, lo)
    y = expert_ffn(xg, block_expert, block_valid, block_src.astype(jnp.int32), wg, wu, wd, layer)
    return sc_combine_gather(y, dest3), w


def mixer_ab(h, w_in, w_out, decay_logit, conv_w, conv_b, w1, b1, f1, w2, b2, f2, w3, skip, rope, dft, residual, *,
             seq, ctx_len):
    n_rows = h.shape[0]
    tm = _token_tile(n_rows)
    assert RET_DK == LANE
    p = mm([(h, w_in.astype(BF16))], F32, tm, RET_QK, rope=(rope[0], rope[1], (1.0, RET_DK ** -0.5)),
           name="ab_in_proj")
    log_g = jax.nn.log_sigmoid(decay_logit.astype(F32))
    ret = retention(p, log_g, jnp.exp(RET_CHUNK * log_g), seq=seq)
    v, x1, x2 = shortconv(p, conv_w, conv_b, seq=seq)
    filt = (w1, b1, f1, w2, b2, f2, w3)
    hy_x = long_conv_two_stage(dft["x"], hyena_filter_taps_2d(seq, *filt), v, x1, x2, skip, seq)
    hy_c = long_conv_one_stage(dft["c"], hyena_filter_taps(ctx_len, *filt), v[seq:], x1[seq:], x2[seq:], skip)
    hy = jnp.concatenate([hy_x, hy_c], axis=0)
    w_out = w_out.astype(BF16)
    return mm([(ret, w_out[:RET_V]), (hy, w_out[RET_V:])], F32, tm, 512, residual=residual, name="ab_out_proj")


def mixer_da(h, w_in, w_out, lam, subln, lambda_init, rope, residual, *, seq, ctx_len):
    n_rows = h.shape[0]
    tm = _token_tile(n_rows)
    p = mm([(h, w_in.astype(BF16))], F32, tm, 512, name="da_in_proj")
    qt, k, vt = rope_da(p, rope[0], rope[1])
    lam_f = lam.astype(F32)
    lam_full = jnp.exp(jnp.sum(lam_f[0] * lam_f[1])) - jnp.exp(jnp.sum(lam_f[2] * lam_f[3])) + lambda_init
    o = diff_attention(qt, k, vt, lam_full, subln, seq=seq, ctx_len=ctx_len, lambda_init=lambda_init)
    return mm([(o, w_out.astype(BF16))], F32, tm, 512, residual=residual, name="da_out_proj")


def kernel(x, c, ctx, c_ctx, w_ada, b_ada, norm_mix, norm_ffn, ab_w_in, ab_w_out, ret_decay_logit, hy_conv_w, hy_conv_b, hy_w1, hy_b1, hy_freq1, hy_w2, hy_b2, hy_freq2, hy_w3, hy_skip, da_w_in, da_w_out, da_lambda, da_subln, router_w, router_b, exp_w_gate, exp_w_up, exp_w_down, sh_w_gate, sh_w_up, sh_w_down, norm_final):
    batch, seq, d = x.shape
    ctx_len = ctx.shape[1]
    assert batch == 1 and seq % ROW_TILE == 0 and ctx_len == ROW_TILE
    depth = w_ada.shape[0]
    n_rows = seq + ctx_len

    xs = jnp.concatenate([x[0], ctx[0]], axis=0)
    cv = jnp.zeros((SUBLANE, d), F32).at[0].set(c_ctx).at[1].set(c[0])
    mods = adaln(cv, w_ada, b_ada)[:, :2].reshape(depth, 2, 6, d)

    rope_ret = rope_tables(seq, ctx_len, RET_DK)
    rope_da = rope_tables(seq, ctx_len, DA_HEAD_DIM)
    dft = dict(x=dft_tables_two_stage(2 * seq), c=dft_tables_one_stage(2 * ctx_len))
    common = dict(n_rows=n_rows, seq=seq)

    for i in range(depth):
        j = i // 2
        (h,) = norm_mod(xs, norm_mix[i], mods=mods[i], shift_idx=0, scale_idx=1, **common)
        residual = (xs, mods[i], 2, seq)
        if i % 2 == 0:
            xs = mixer_ab(h, ab_w_in[j], ab_w_out[j], ret_decay_logit[j], hy_conv_w[j], hy_conv_b[j], hy_w1[j],
                          hy_b1[j], hy_freq1[j], hy_w2[j], hy_b2[j], hy_freq2[j], hy_w3[j], hy_skip[j], rope_ret, dft,
                          residual, seq=seq, ctx_len=ctx_len)
        else:
            lambda_init = 0.8 - 0.6 * math.exp(-0.3 * i)
            xs = mixer_da(h, da_w_in[j], da_w_out[j], da_lambda[j], da_subln[j], lambda_init, rope_da, residual,
                          seq=seq, ctx_len=ctx_len)
        h, logits_t, h_packed = norm_mod(xs, norm_ffn[i], mods=mods[i], shift_idx=3, scale_idx=4,
                                         router_wt=router_w[i].T, **common)
        xs = moe(h, h_packed, logits_t, router_b[i], exp_w_gate, exp_w_up, exp_w_down, i,
                 sh_w_gate[i], sh_w_up[i], sh_w_down[i], (xs, mods[i], 5, seq))
    (out,) = norm_mod(xs, norm_final, n_rows=seq, seq=seq, out_dtype=F32)
    return out[None]
```

```python
import functools
import math

import jax
import jax.numpy as jnp
from jax import lax
from jax.experimental import pallas as pl
from jax.experimental.pallas import tpu as pltpu
from jax.experimental.pallas import tpu_sc as plsc

F32 = jnp.float32
BF16 = jnp.bfloat16
HIGHEST = lax.Precision.HIGHEST

GRID_W = 64
EPS = 1e-6
ROPE_BASE = 10000.0

RET_HEADS = 4
RET_DK = 128
RET_DV = 256
RET_CHUNK = 128
RET_STEP_CHUNKS = 2
RET_QK = RET_HEADS * RET_DK
RET_V = RET_HEADS * RET_DV

HY_WIDTH = 512
HY_ORDER = 2
HY_BANDS = 16
HY_EMB = 2 * HY_BANDS + 1
HY_FFN = 64
HY_DECAY_TARGET = 1e-2
HY_FAST_DECAY = 0.3
HY_SLOW_DECAY = 1.5
HY_ZCOLS = 64
HY_VALID_COL = HY_EMB
FFT_N2 = 128

DA_HEADS = 8
DA_HEAD_DIM = 64
DA_WIDTH = DA_HEADS * 2 * DA_HEAD_DIM

N_EXPERTS = 64
TOP_K = 8
N_GROUPS = 8
TOPK_GROUPS = 4
GROUP_SIZE = N_EXPERTS // N_GROUPS
ROUTED_SCALE = 2.5
MOE_BLOCK = 512
MOE_SUB_BLOCKS = 2
SC_CORES = 2
SC_SUBCORES = 16
SC_WORKERS = SC_CORES * SC_SUBCORES
SC_WINDOW = 64

LANE = 128
SUBLANE = 8
ROW_TILE = 256
MAX_TOKEN_TILE = 1280
VMEM_LIMIT = 48 * 1024 * 1024
FLASH_VMEM_LIMIT = 56 * 1024 * 1024
NEG_BIG = -1e30
LOG2_E = 1.4426950408889634
FLASH_ONES_ROWS = 16
FLASH_INIT_KEYS = 16
FLASH_LAZY_HEADROOM = 60.0


def _params(sem):
    return pltpu.CompilerParams(dimension_semantics=sem, vmem_limit_bytes=VMEM_LIMIT)


def _token_tile(n):
    best = ROW_TILE
    t = ROW_TILE
    while t <= min(n, MAX_TOKEN_TILE):
        if n % t == 0:
            best = t
        t += ROW_TILE
    return best


def _row_mod(mods_ref, idx, row0, n, seq):
    row = row0 + lax.broadcasted_iota(jnp.int32, (n, 1), 0)
    return jnp.where(row >= seq, mods_ref[0, idx:idx + 1, :], mods_ref[1, idx:idx + 1, :])


def _mm_kernel(*refs, n_pairs, has_epi, gate_idx, seq, rope_scales):
    acc = None
    for p in range(n_pairs):
        a = refs[2 * p][...].astype(BF16)
        b = refs[2 * p + 1][...].astype(BF16)
        d = jnp.dot(a, b, preferred_element_type=F32)
        acc = d if acc is None else acc + d
    idx = 2 * n_pairs
    if has_epi:
        acc = refs[idx][...] * (acc + refs[idx + 1][...] * refs[idx + 2][...])
        idx += 3
    if gate_idx is not None:
        tm = acc.shape[0]
        acc = refs[idx][...] + _row_mod(refs[idx + 1], gate_idx, pl.program_id(0) * tm, tm, seq) * acc
        idx += 2
    if rope_scales is None:
        o_ref = refs[idx]
        o_ref[...] = acc.astype(o_ref.dtype)
        return
    cos, sin = refs[idx][...], refs[idx + 1][...]
    o_ref = refs[idx + 2]
    j = pl.program_id(1)

    @pl.when(j >= len(rope_scales))
    def _():
        o_ref[...] = acc.astype(o_ref.dtype)

    for t, scale in enumerate(rope_scales):
        @pl.when(j == t)
        def _():
            for b in range(acc.shape[1] // LANE):
                x = acc[:, b * LANE:(b + 1) * LANE]
                x = x * cos + pltpu.roll(x, LANE // 2, 1) * sin
                o_ref[:, b * LANE:(b + 1) * LANE] = (x * scale if scale != 1.0 else x).astype(o_ref.dtype)


def mm(pairs, out_dtype, tm, tn, epi=None, residual=None, rope=None, name="mm"):
    m = pairs[0][0].shape[0]
    n = pairs[0][1].shape[1]
    assert m % tm == 0 and n % tn == 0
    in_specs, args = [], []
    for a, b in pairs:
        k = a.shape[1]
        in_specs += [pl.BlockSpec((tm, k), lambda i, j: (i, 0)), pl.BlockSpec((k, tn), lambda i, j: (0, j))]
        args += [a, b]
    if epi is not None:
        in_specs += [pl.BlockSpec((tm, tn), lambda i, j: (i, j)), pl.BlockSpec((1, tn), lambda i, j: (0, j)),
                     pl.BlockSpec((tm, tn), lambda i, j: (i, j))]
        args += list(epi)
    gate_idx = seq = None
    if residual is not None:
        res, mods, gate_idx, seq = residual
        in_specs += [pl.BlockSpec((tm, tn), lambda i, j: (i, j)), pl.BlockSpec((2, 6, tn), lambda i, j: (0, 0, j))]
        args += [res, mods]
    rope_scales = None
    if rope is not None:
        cos, sin, rope_scales = rope
        in_specs += [pl.BlockSpec((tm, LANE), lambda i, j: (i, 0)), pl.BlockSpec((tm, LANE), lambda i, j: (i, 0))]
        args += [cos, sin]
    return pl.pallas_call(
        functools.partial(_mm_kernel, n_pairs=len(pairs), has_epi=epi is not None, gate_idx=gate_idx, seq=seq,
                          rope_scales=rope_scales),
        grid=(m // tm, n // tn),
        in_specs=in_specs,
        out_specs=pl.BlockSpec((tm, tn), lambda i, j: (i, j)),
        out_shape=jax.ShapeDtypeStruct((m, n), out_dtype),
        compiler_params=_params(("parallel", "parallel")),
        name=name,
    )(*args)


def _adaln_kernel(cv_ref, w_ref, b_ref, o_ref):
    cv = cv_ref[...]
    s = cv * jax.nn.sigmoid(cv)
    o_ref[0] = jnp.dot(s, w_ref[0], precision=HIGHEST, preferred_element_type=F32) + b_ref[0]


def adaln(cv, w_ada, b_ada):
    depth, d, n = w_ada.shape
    tn = 1536
    return pl.pallas_call(
        _adaln_kernel,
        grid=(depth, n // tn),
        in_specs=[pl.BlockSpec((SUBLANE, d), lambda l, j: (0, 0)),
                  pl.BlockSpec((1, d, tn), lambda l, j: (l, 0, j)),
                  pl.BlockSpec((1, 1, tn), lambda l, j: (l, 0, j))],
        out_specs=pl.BlockSpec((1, SUBLANE, tn), lambda l, j: (l, 0, j)),
        out_shape=jax.ShapeDtypeStruct((depth, SUBLANE, n), F32),
        compiler_params=_params(("parallel", "parallel")),
        name="adaln",
    )(cv, w_ada, b_ada.reshape(depth, 1, n))


def _norm_mod_kernel(*refs, shift_idx, scale_idx, has_router, seq):
    it = iter(refs)
    x = next(it)[...]
    mods_ref = next(it) if shift_idx is not None else None
    g_ref = next(it)
    wr_ref = next(it) if has_router else None
    h_ref = next(it)
    y = x * lax.rsqrt(jnp.mean(x * x, axis=-1, keepdims=True) + EPS) * g_ref[...]
    if shift_idx is not None:
        tm = x.shape[0]
        row0 = pl.program_id(0) * tm
        y = y * (1.0 + _row_mod(mods_ref, scale_idx, row0, tm, seq)) + _row_mod(mods_ref, shift_idx, row0, tm, seq)
    h_ref[...] = y.astype(h_ref.dtype)
    if has_router:
        lg_ref = next(it)
        lg_ref[...] = lax.dot_general(wr_ref[...], y, (((1,), (1,)), ((), ())),
                                      precision=HIGHEST, preferred_element_type=F32)
        half = y.shape[1] // 2
        next(it)[...] = _pack_bf16_pair(y[:, :half], y[:, half:])


def _pack_bf16_pair(a, b):
    bits = lambda x: lax.bitcast_convert_type(x.astype(BF16), jnp.uint16).astype(jnp.int32)
    return bits(a) | lax.shift_left(bits(b), 16)


def _unpack_bf16_pair(w):
    return (lax.bitcast_convert_type(lax.shift_left(w, 16), F32),
            lax.bitcast_convert_type(w & -65536, F32))


def norm_mod(xs, g, *, n_rows, seq, mods=None, shift_idx=None, scale_idx=None, router_wt=None, out_dtype=BF16):
    d = xs.shape[1]
    tm = _token_tile(n_rows)
    row = pl.BlockSpec((tm, d), lambda i: (i, 0))
    in_specs, args = [row], [xs]
    if shift_idx is not None:
        in_specs.append(pl.BlockSpec((2, 6, d), lambda i: (0, 0, 0)))
        args.append(mods)
    in_specs.append(pl.BlockSpec((1, d), lambda i: (0, 0)))
    args.append(g.reshape(1, d))
    has_router = router_wt is not None
    if has_router:
        in_specs.append(pl.BlockSpec(router_wt.shape, lambda i: (0, 0)))
        args.append(router_wt)
    out_specs = [row]
    out_shape = [jax.ShapeDtypeStruct((n_rows, d), out_dtype)]
    if has_router:
        out_specs.append(pl.BlockSpec((N_EXPERTS, tm), lambda i: (0, i)))
        out_shape.append(jax.ShapeDtypeStruct((N_EXPERTS, n_rows), F32))
        out_specs.append(pl.BlockSpec((tm, d // 2), lambda i: (i, 0)))
        out_shape.append(jax.ShapeDtypeStruct((n_rows, d // 2), jnp.int32))
    return pl.pallas_call(
        functools.partial(_norm_mod_kernel, shift_idx=shift_idx, scale_idx=scale_idx, has_router=has_router, seq=seq),
        grid=(n_rows // tm,),
        in_specs=in_specs,
        out_specs=out_specs,
        out_shape=out_shape,
        compiler_params=_params(("parallel",)),
        name="norm_mod",
    )(*args)


def rope_tables(seq, ctx_len, head_dim):
    n_freq = head_dim // 4
    inv = ROPE_BASE ** (-jnp.arange(n_freq, dtype=F32) / n_freq)
    rows = seq // GRID_W
    row = jnp.repeat(jnp.arange(rows, dtype=F32), GRID_W)
    col = jnp.tile(jnp.arange(GRID_W, dtype=F32), rows)
    ang = jnp.concatenate([row[:, None] * inv, col[:, None] * inv], axis=-1)
    cos, sin = jnp.cos(ang), jnp.sin(ang)
    cos = jnp.concatenate([cos, cos], axis=-1)
    sin = jnp.concatenate([-sin, sin], axis=-1)
    reps = LANE // head_dim
    cos, sin = jnp.tile(cos, (1, reps)), jnp.tile(sin, (1, reps))
    cos = jnp.concatenate([cos, jnp.ones((ctx_len, LANE), F32)], axis=0)
    sin = jnp.concatenate([sin, jnp.zeros((ctx_len, LANE), F32)], axis=0)
    return cos, sin


def _ret_kernel(lg_ref, gc_ref, q_ref, k_ref, v_ref, *rest, reverse):
    if reverse:
        yf_ref, gate_ref, o_ref, s_ref = rest
    else:
        o_ref, s_ref = rest
    c = RET_CHUNK

    @pl.when(pl.program_id(0) == 0)
    def _():
        s_ref[...] = jnp.zeros_like(s_ref)

    ii = lax.broadcasted_iota(jnp.int32, (c, c), 0)
    jj = lax.broadcasted_iota(jnp.int32, (c, c), 1)
    rel = ((jj - ii) if reverse else (ii - jj)).astype(F32)
    pos = lax.broadcasted_iota(jnp.int32, (c, 1), 0).astype(F32)
    for h in range(RET_HEADS):
        lg = lg_ref[h]
        dec = jnp.where(rel >= 0, jnp.exp(jnp.maximum(rel, 0.0) * lg), 0.0)
        if reverse:
            q_dec = jnp.exp((c - pos) * lg)
            k_dec = jnp.exp(pos * lg)
        else:
            q_dec = jnp.exp((pos + 1.0) * lg)
            k_dec = jnp.exp((c - 1.0 - pos) * lg)
        state = s_ref[h]
        qk_cols = slice(h * RET_DK, (h + 1) * RET_DK)
        v_cols = slice(h * RET_DV, (h + 1) * RET_DV)
        for sub in (range(RET_STEP_CHUNKS - 1, -1, -1) if reverse else range(RET_STEP_CHUNKS)):
            rows = slice(sub * c, (sub + 1) * c)
            q = q_ref[rows, qk_cols]
            k = k_ref[rows, qk_cols]
            v = v_ref[rows, v_cols].astype(BF16)
            s = lax.dot_general(q.astype(BF16), k.astype(BF16), (((1,), (1,)), ((), ())),
                                preferred_element_type=F32) * dec
            y = jnp.dot(s.astype(BF16), v, preferred_element_type=F32)
            y = y + jnp.dot((q * q_dec).astype(BF16), state.astype(BF16), preferred_element_type=F32)
            upd = lax.dot_general((k * k_dec).astype(BF16), v, (((0,), (0,)), ((), ())), preferred_element_type=F32)
            state = gc_ref[h] * state + upd
            if reverse:
                r = y + yf_ref[rows, v_cols]
                mu = jnp.mean(r, axis=-1, keepdims=True)
                rc = r - mu
                var = jnp.mean(rc * rc, axis=-1, keepdims=True)
                g = gate_ref[rows, v_cols]
                o_ref[rows, v_cols] = (rc * lax.rsqrt(var + EPS) * (g * jax.nn.sigmoid(g))).astype(o_ref.dtype)
            else:
                o_ref[rows, v_cols] = y
        s_ref[h] = state


def retention(p, log_g, g_chunk, *, seq):
    n_rows = p.shape[0]
    step = RET_STEP_CHUNKS * RET_CHUNK
    assert seq % step == 0 and n_rows % step == 0
    n_steps = n_rows // step
    n_x = seq // step
    smem = pl.BlockSpec(memory_space=pltpu.SMEM)

    def run(reverse, extra):
        if reverse:
            idx = lambda t: n_steps - 1 - t
        else:
            idx = lambda t: (t + n_x) % n_steps
        in_specs = [smem, smem,
                    pl.BlockSpec((step, RET_QK), lambda t: (idx(t), 0)),
                    pl.BlockSpec((step, RET_QK), lambda t: (idx(t), 1)),
                    pl.BlockSpec((step, RET_V), lambda t: (idx(t), 1))]
        args = [log_g[1 if reverse else 0], g_chunk[1 if reverse else 0], p, p, p]
        if reverse:
            in_specs += [pl.BlockSpec((step, RET_V), lambda t: (idx(t), 0)),
                         pl.BlockSpec((step, RET_V), lambda t: (idx(t), 2))]
            args += list(extra)
        return pl.pallas_call(
            functools.partial(_ret_kernel, reverse=reverse),
            grid=(n_steps,),
            in_specs=in_specs,
            out_specs=pl.BlockSpec((step, RET_V), lambda t: (idx(t), 0)),
            out_shape=jax.ShapeDtypeStruct((n_rows, RET_V), BF16 if reverse else F32),
            scratch_shapes=[pltpu.VMEM((RET_HEADS, RET_DK, RET_DV), F32)],
            compiler_params=_params(("arbitrary",)),
            name="retention_bwd" if reverse else "retention_fwd",
        )(*args)

    y_fwd = run(False, None)
    return run(True, (y_fwd, p))


def _shortconv_kernel(cur_ref, prev_ref, next_ref, w_ref, b_ref, v_ref, x1_ref, x2_ref, *, x_tiles):
    i = pl.program_id(0)
    cur = cur_ref[...]
    rows = cur.shape[0]
    row = lax.broadcasted_iota(jnp.int32, (rows, 1), 0)
    has_prev = jnp.where((i == 0) | (i == x_tiles), 0.0, 1.0)
    has_next = jnp.where((i == x_tiles - 1) | (i == x_tiles), 0.0, 1.0)
    up = jnp.where(row == 0, prev_ref[SUBLANE - 1:SUBLANE, :] * has_prev, pltpu.roll(cur, 1, 0))
    dn = jnp.where(row == rows - 1, next_ref[0:1, :] * has_next, pltpu.roll(cur, rows - 1, 0))
    y = up * w_ref[0:1, :] + cur * w_ref[1:2, :] + dn * w_ref[2:3, :] + b_ref[...]
    v_ref[...] = y[:, :HY_WIDTH]
    x1_ref[...] = y[:, HY_WIDTH:2 * HY_WIDTH]
    x2_ref[...] = y[:, 2 * HY_WIDTH:]


def shortconv(p, w, b, *, seq):
    n_rows = p.shape[0]
    width = 3 * HY_WIDTH
    col = p.shape[1] // width - 1
    per = ROW_TILE // SUBLANE
    last = n_rows // SUBLANE - 1
    out = jax.ShapeDtypeStruct((n_rows, HY_WIDTH), F32)
    ospec = pl.BlockSpec((ROW_TILE, HY_WIDTH), lambda i: (i, 0))
    return pl.pallas_call(
        functools.partial(_shortconv_kernel, x_tiles=seq // ROW_TILE),
        grid=(n_rows // ROW_TILE,),
        in_specs=[pl.BlockSpec((ROW_TILE, width), lambda i: (i, col)),
                  pl.BlockSpec((SUBLANE, width), lambda i: (jnp.maximum(i * per - 1, 0), col)),
                  pl.BlockSpec((SUBLANE, width), lambda i: (jnp.minimum((i + 1) * per, last), col)),
                  pl.BlockSpec((3, width), lambda i: (0, 0)),
                  pl.BlockSpec((1, width), lambda i: (0, 0))],
        out_specs=[ospec, ospec, ospec],
        out_shape=[out, out, out],
        compiler_params=_params(("parallel",)),
        name="shortconv",
    )(p, p, p, w, b.reshape(1, width))


def _filt_kernel(z_ref, w1_ref, b1_ref, f1_ref, w2_ref, b2_ref, f2_ref, w3a_ref, w3b_ref, dl_ref, *o_ref):
    z = z_ref[...]
    h = jnp.sin(f1_ref[...] * (jnp.dot(z, w1_ref[...], precision=HIGHEST, preferred_element_type=F32) + b1_ref[...]))
    h = jnp.sin(f2_ref[...] * (jnp.dot(h, w2_ref[...], precision=HIGHEST, preferred_element_type=F32) + b2_ref[...]))
    window = jnp.exp(-z[:, 0:1] * dl_ref[...]) * z[:, HY_VALID_COL:HY_VALID_COL + 1]
    for o, w3_ref in enumerate((w3a_ref, w3b_ref)):
        o_ref[o][...] = jnp.dot(h, w3_ref[...], precision=HIGHEST, preferred_element_type=F32) * window


def hyena_filter_taps(length, w1, b1, f1, w2, b2, f2, w3):
    z = _filter_positions(length)
    w1p = jnp.zeros((HY_ZCOLS, HY_FFN), F32).at[:HY_EMB].set(w1)
    deltas = jnp.abs(jnp.linspace(math.log(HY_DECAY_TARGET) / HY_SLOW_DECAY,
                                  math.log(HY_DECAY_TARGET) / HY_FAST_DECAY, HY_WIDTH, dtype=F32)).reshape(1, HY_WIDTH)
    tm = min(length, 512)
    half_tiles = length // tm
    vec = lambda a: a.reshape(1, HY_FFN)
    small = lambda shape: pl.BlockSpec(shape, lambda i: (0, 0))
    w3_spec = lambda o: pl.BlockSpec((HY_FFN, HY_WIDTH), lambda i: (0, 2 * o + jnp.where(i >= half_tiles, 1, 0)))
    assert HY_ORDER == 2
    return pl.pallas_call(
        _filt_kernel,
        grid=(2 * half_tiles,),
        in_specs=[pl.BlockSpec((tm, HY_ZCOLS), lambda i: (i, 0)),
                  small((HY_ZCOLS, HY_FFN)), small((1, HY_FFN)), small((1, HY_FFN)),
                  small((HY_FFN, HY_FFN)), small((1, HY_FFN)), small((1, HY_FFN)),
                  w3_spec(0), w3_spec(1), small((1, HY_WIDTH))],
        out_specs=[pl.BlockSpec((tm, HY_WIDTH), lambda i: (i, 0))] * HY_ORDER,
        out_shape=[jax.ShapeDtypeStruct((2 * length, HY_WIDTH), F32)] * HY_ORDER,
        compiler_params=_params(("parallel",)),
        name="hyena_filter",
    )(z, w1p, vec(b1), vec(f1), w2, vec(b2), vec(f2), w3, w3, deltas)


def _filt2d_kernel(z_ref, w1_ref, b1_ref, f1_ref, w2_ref, b2_ref, f2_ref, w3fa_ref, w3ba_ref, w3fb_ref, w3bb_ref,
                   dl_ref, oa_ref, ob_ref):
    n1 = oa_ref.shape[0]
    c = dl_ref.shape[1]
    z = z_ref[...]
    h = jnp.sin(f1_ref[...] * (_dot_split(_split_bf16(z), _split_bf16(w1_ref[...])) + b1_ref[...]))
    h = jnp.sin(f2_ref[...] * (_dot_split(_split_bf16(h), _split_bf16(w2_ref[...])) + b2_ref[...]))
    window = jnp.exp(-z[:, 0:1] * dl_ref[...]) * z[:, HY_VALID_COL:HY_VALID_COL + 1]
    h_split = _split_bf16(h)
    for o_ref, wf_ref, wb_ref in ((oa_ref, w3fa_ref, w3ba_ref), (ob_ref, w3fb_ref, w3bb_ref)):
        wf_split, wb_split = _split_bf16(wf_ref[...]), _split_bf16(wb_ref[...])
        for j in range(SUBLANE):
            fwd_rows = slice(j * n1, j * n1 + n1 // 2)
            bwd_rows = slice(j * n1 + n1 // 2, (j + 1) * n1)
            taps = jnp.concatenate([_dot_split(tuple(p[fwd_rows] for p in h_split), wf_split),
                                    _dot_split(tuple(p[bwd_rows] for p in h_split), wb_split)], axis=0)
            o_ref[:, j * c:(j + 1) * c] = taps * window[j * n1:(j + 1) * n1]


def _split_bf16(a):
    hi = a.astype(BF16)
    return hi, (a - hi.astype(F32)).astype(BF16)


def _dot_split(a, b):
    d = lambda x, y: jnp.dot(x, y, preferred_element_type=F32)
    return d(a[0], b[0]) + d(a[0], b[1]) + d(a[1], b[0])


def hyena_filter_taps_2d(length, w1, b1, f1, w2, b2, f2, w3):
    n1 = 2 * length // FFT_N2
    groups = FFT_N2 // SUBLANE
    z = _filter_positions(length).reshape(n1, groups, SUBLANE, HY_ZCOLS).transpose(1, 2, 0, 3)
    z = z.reshape(2 * length, HY_ZCOLS)
    w1p = jnp.zeros((HY_ZCOLS, HY_FFN), F32).at[:HY_EMB].set(w1)
    deltas = jnp.abs(jnp.linspace(math.log(HY_DECAY_TARGET) / HY_SLOW_DECAY,
                                  math.log(HY_DECAY_TARGET) / HY_FAST_DECAY, HY_WIDTH, dtype=F32)).reshape(1, HY_WIDTH)
    vec = lambda a: a.reshape(1, HY_FFN)
    small = lambda shape: pl.BlockSpec(shape, lambda i: (0, 0))
    w3_spec = lambda col: pl.BlockSpec((HY_FFN, HY_WIDTH), lambda i: (0, col))
    assert HY_ORDER == 2
    out = jax.ShapeDtypeStruct((n1, FFT_N2 * HY_WIDTH), F32)
    return pl.pallas_call(
        _filt2d_kernel,
        grid=(groups,),
        in_specs=[pl.BlockSpec((SUBLANE * n1, HY_ZCOLS), lambda i: (i, 0)),
                  small((HY_ZCOLS, HY_FFN)), small((1, HY_FFN)), small((1, HY_FFN)),
                  small((HY_FFN, HY_FFN)), small((1, HY_FFN)), small((1, HY_FFN)),
                  w3_spec(0), w3_spec(1), w3_spec(2), w3_spec(3), small((1, HY_WIDTH))],
        out_specs=[pl.BlockSpec((n1, SUBLANE * HY_WIDTH), lambda i: (0, i))] * HY_ORDER,
        out_shape=[out] * HY_ORDER,
        compiler_params=_params(("parallel",)),
        name="hyena_filter",
    )(z, w1p, vec(b1), vec(f1), w2, vec(b2), vec(f2), w3, w3, w3, w3, deltas)


def _filter_positions(length):
    t = jnp.concatenate([jnp.arange(length, dtype=F32), float(length) - jnp.arange(length, dtype=F32)])
    valid = jnp.ones((2 * length,), F32).at[length].set(0.0)
    t_norm = t / max(length - 1, 1)
    bands = jnp.linspace(1e-4, HY_BANDS - 1, HY_BANDS, dtype=F32)
    ang = (2.0 * math.pi / length) * t[:, None] * bands[None, :]
    z = jnp.concatenate([t_norm[:, None], jnp.cos(ang), -jnp.sin(ang), valid[:, None]], axis=-1)
    return jnp.pad(z, ((0, 0), (0, HY_ZCOLS - z.shape[1])))


def _angles(num, den):
    return (2.0 * math.pi / den) * (num % den).astype(F32)


def dft_tables_two_stage(m):
    n2 = FFT_N2
    n1 = m // n2
    half = n1 // 2
    kp = -(-(half + 1) // SUBLANE) * SUBLANE
    k1 = jnp.arange(kp, dtype=jnp.int32)
    live = (k1 <= half)
    a1 = _angles(k1[:, None] * jnp.arange(n1, dtype=jnp.int32)[None, :], n1)
    f1 = jnp.concatenate([jnp.where(live[:, None], jnp.cos(a1), 0.0), jnp.where(live[:, None], -jnp.sin(a1), 0.0)], 0)
    wgt = jnp.where((k1 == 0) | (k1 == half), 1.0, 2.0) * live / m
    a1h = a1[:, :half].T
    cinv = jnp.concatenate([jnp.cos(a1h) * wgt[None, :], -jnp.sin(a1h) * wgt[None, :]], axis=1)
    k = k1[:, None, None] + n1 * jnp.arange(n2, dtype=jnp.int32)[None, :, None]
    th = _angles(k * jnp.arange(n2, dtype=jnp.int32)[None, None, :], m)
    c = jnp.where(live[:, None, None], jnp.cos(th), 0.0)
    s = jnp.where(live[:, None, None], jnp.sin(th), 0.0)
    g_fwd = jnp.concatenate([jnp.concatenate([c, s], 2), jnp.concatenate([-s, c], 2)], 1)
    ct, st = jnp.swapaxes(c, 1, 2), jnp.swapaxes(s, 1, 2)
    g_inv = jnp.concatenate([jnp.concatenate([ct, -st], 2), jnp.concatenate([st, ct], 2)], 1)
    return dict(n1=n1, kp=kp, f1=f1.astype(BF16), f1_half=f1[:, :half].astype(BF16), cinv=cinv.astype(BF16),
                g_fwd=g_fwd.astype(BF16), g_inv=g_inv.astype(BF16))


def dft_tables_one_stage(m):
    half = m // 2
    kp = -(-(half + 1) // SUBLANE) * SUBLANE
    k = jnp.arange(kp, dtype=jnp.int32)
    live = (k <= half)
    a = _angles(k[:, None] * jnp.arange(m, dtype=jnp.int32)[None, :], m)
    f = jnp.concatenate([jnp.where(live[:, None], jnp.cos(a), 0.0), jnp.where(live[:, None], -jnp.sin(a), 0.0)], 0)
    wgt = jnp.where((k == 0) | (k == half), 1.0, 2.0) * live / m
    ah = a[:, :half].T
    cinv = jnp.concatenate([jnp.cos(ah) * wgt[None, :], -jnp.sin(ah) * wgt[None, :]], axis=1)
    return dict(kp=kp, f=f.astype(BF16), f_half=f[:, :half].astype(BF16), cinv=cinv.astype(BF16))


def _bmm_kernel(*refs, kb, in_part_major, out_part_major, has_h):
    if has_h:
        g_ref, a_ref, h_ref, o_ref = refs
    else:
        g_ref, a_ref, o_ref = refs
    n2 = FFT_N2
    for b in range(kb):
        if in_part_major:
            ar, ai = a_ref[0, b], a_ref[1, b]
        else:
            ar, ai = a_ref[b, 0], a_ref[b, 1]
        if has_h:
            hr, hi = h_ref[b, 0], h_ref[b, 1]
            ar, ai = ar * hr - ai * hi, ar * hi + ai * hr
        xin = jnp.concatenate([ar, ai], axis=0).astype(BF16)
        y = jnp.dot(g_ref[b], xin, preferred_element_type=F32)
        if out_part_major:
            o_ref[0, b] = y[:n2].astype(o_ref.dtype)
            o_ref[1, b] = y[n2:].astype(o_ref.dtype)
        else:
            o_ref[b, 0] = y[:n2].astype(o_ref.dtype)
            o_ref[b, 1] = y[n2:].astype(o_ref.dtype)


def bmm_k1(g, a, h=None, *, in_part_major, out_part_major):
    kp = g.shape[0]
    n2 = FFT_N2
    c = a.shape[-1]
    kb, tc = SUBLANE, min(c, 512)
    pm = lambda: pl.BlockSpec((2, kb, n2, tc), lambda i, j: (0, i, 0, j))
    km = lambda: pl.BlockSpec((kb, 2, n2, tc), lambda i, j: (i, 0, 0, j))
    in_specs = [pl.BlockSpec((kb, 2 * n2, 2 * n2), lambda i, j: (i, 0, 0)), pm() if in_part_major else km()]
    args = [g, a]
    if h is not None:
        in_specs.append(km())
        args.append(h)
    return pl.pallas_call(
        functools.partial(_bmm_kernel, kb=kb, in_part_major=in_part_major, out_part_major=out_part_major,
                          has_h=h is not None),
        grid=(kp // kb, c // tc),
        in_specs=in_specs,
        out_specs=pm() if out_part_major else km(),
        out_shape=jax.ShapeDtypeStruct((2, kp, n2, c), BF16) if out_part_major else
        jax.ShapeDtypeStruct((kp, 2, n2, c), F32),
        compiler_params=_params(("parallel", "parallel")),
        name="dft_inner",
    )(*args)


def _cmul_kernel(x_ref, h_ref, o_ref):
    xr, xi, hr, hi = x_ref[0], x_ref[1], h_ref[0], h_ref[1]
    o_ref[0] = xr * hr - xi * hi
    o_ref[1] = xr * hi + xi * hr


def cmul(x, h):
    spec = pl.BlockSpec(x.shape, lambda i: (0, 0, 0))
    return pl.pallas_call(_cmul_kernel, grid=(1,), in_specs=[spec, spec], out_specs=spec,
                          out_shape=jax.ShapeDtypeStruct(x.shape, F32), compiler_params=_params(("arbitrary",)),
                          name="spectrum_product")(x, h)


def _dft_outer3_kernel(f_ref, x_ref, o_ref):
    c = x_ref.shape[2]
    f = f_ref[...]
    for j in range(SUBLANE):
        o_ref[:, j * c:(j + 1) * c] = jnp.dot(f, x_ref[:, j, :].astype(BF16),
                                              preferred_element_type=F32).astype(o_ref.dtype)


def dft_outer3(f, x3, n_outer):
    rows = f.shape[0]
    c = x3.shape[2]
    return pl.pallas_call(
        _dft_outer3_kernel,
        grid=(FFT_N2 // SUBLANE,),
        in_specs=[pl.BlockSpec((rows, n_outer), lambda j: (0, 0)),
                  pl.BlockSpec((n_outer, SUBLANE, c), lambda j: (0, j, 0))],
        out_specs=pl.BlockSpec((rows, SUBLANE * c), lambda j: (0, j)),
        out_shape=jax.ShapeDtypeStruct((rows, FFT_N2 * c), BF16),
        compiler_params=_params(("parallel",)),
        name="dft_outer",
    )(f, x3)


def _idft_gate3_kernel(cinv_ref, b_ref, gate_ref, skip_ref, u_ref, o_ref, *, u_is_3d):
    c = gate_ref.shape[2]
    cinv = cinv_ref[...]
    for j in range(SUBLANE):
        cols = slice(j * c, (j + 1) * c)
        acc = jnp.dot(cinv, b_ref[:, cols].astype(BF16), preferred_element_type=F32)
        u = u_ref[:, j, :] if u_is_3d else u_ref[:, cols]
        o_ref[:, cols] = gate_ref[:, j, :] * (acc + skip_ref[...] * u)


def idft_gate3(cinv, b2d, gate3, skip_row, u):
    n_outer = cinv.shape[0]
    c = gate3.shape[2]
    u_is_3d = u.ndim == 3
    wide = pl.BlockSpec((n_outer, SUBLANE * c), lambda j: (0, j))
    slab = pl.BlockSpec((n_outer, SUBLANE, c), lambda j: (0, j, 0))
    return pl.pallas_call(
        functools.partial(_idft_gate3_kernel, u_is_3d=u_is_3d),
        grid=(FFT_N2 // SUBLANE,),
        in_specs=[pl.BlockSpec(cinv.shape, lambda j: (0, 0)),
                  pl.BlockSpec((b2d.shape[0], SUBLANE * c), lambda j: (0, j)),
                  slab, pl.BlockSpec((1, c), lambda j: (0, 0)), slab if u_is_3d else wide],
        out_specs=wide,
        out_shape=jax.ShapeDtypeStruct((n_outer, FFT_N2 * c), F32),
        compiler_params=_params(("parallel",)),
        name="idft_outer_gate",
    )(cinv, b2d, gate3, skip_row, u)


def long_conv_two_stage(tabs, taps, v, x1, x2, skip, length):
    c = v.shape[1]
    n2, n1, kp = FFT_N2, tabs["n1"], tabs["kp"]
    as3 = lambda a: a.reshape(a.shape[0] // n2, n2, c)
    spectrum = lambda a2d: bmm_k1(tabs["g_fwd"], a2d.reshape(2, kp, n2, c), in_part_major=True, out_part_major=False)

    spectra = [spectrum(mm([(tabs["f1"], taps[o])], BF16, 2 * kp, 2048, name="dft_outer")) for o in range(HY_ORDER)]
    v3 = as3(v)
    u = v3
    for o, gate in enumerate((x1, x2)):
        if u.ndim == 3:
            a = dft_outer3(tabs["f1_half"], u, n1 // 2)
        else:
            a = mm([(tabs["f1_half"], u)], BF16, 2 * kp, 2048, name="dft_outer")
        bt = bmm_k1(tabs["g_inv"], spectrum(a), spectra[o], in_part_major=False, out_part_major=True)
        u = idft_gate3(tabs["cinv"], bt.reshape(2 * kp, n2 * c), as3(gate), skip[o].reshape(1, c), u)
    return u.reshape(length, c)


def long_conv_one_stage(tabs, taps, v, x1, x2, skip):
    length, c = v.shape
    kp = tabs["kp"]
    u = v
    for o, gate in enumerate((x1, x2)):
        hs = mm([(tabs["f"], taps[o])], F32, 2 * kp, c, name="ctx_dft").reshape(2, kp, c)
        xs = mm([(tabs["f_half"], u)], F32, 2 * kp, c, name="ctx_dft").reshape(2, kp, c)
        ys = cmul(xs, hs).reshape(2 * kp, c)
        u = mm([(tabs["cinv"], ys)], F32, length, c, epi=(gate, skip[o].reshape(1, c), u), name="ctx_idft_gate")
    return u


def _flash_kernel(lam_ref, qt_ref, k_ref, vt_ref, sub_ref, o_ref, m_ref, excess_ref, acc_ref, *, kv, seq, ctx_len,
                  out_scale):
    i = pl.program_id(1)
    last_q = pl.num_programs(1) - 1
    tq = qt_ref.shape[1]
    d = DA_HEAD_DIM
    dv = 2 * DA_HEAD_DIM
    n_chunks = k_ref.shape[0] // kv
    acc_ref[...] = jnp.zeros_like(acc_ref)

    def scores(off, rows, c, masked):
        s = jnp.dot(k_ref[pl.ds(off, rows), c * d:(c + 1) * d], qt_ref[c * d:(c + 1) * d, :],
                    preferred_element_type=F32)
        if masked:
            key = off + lax.broadcasted_iota(jnp.int32, (rows, 1), 0)
            lane = lax.broadcasted_iota(jnp.int32, (1, tq), 1)
            s = s + jnp.where(key < seq, NEG_BIG, 0.0) * jnp.where(lane >= tq - ctx_len, 1.0, 0.0)
        return s

    def exact_step(off, c, masked):
        s = scores(off, kv, c, masked)
        m_old = m_ref[c]
        m_new = jnp.maximum(m_old, jnp.max(s, axis=0, keepdims=True))
        pr = jnp.exp2(s - m_new).astype(BF16)
        acc_ref[c] = jnp.exp2(m_old - m_new) * acc_ref[c] + jnp.dot(vt_ref[:, pl.ds(off, kv)], pr,
                                                                   preferred_element_type=F32)
        m_ref[c] = m_new

    def lazy_step(off, c, masked):
        s = scores(off, kv, c, masked)
        m_old = m_ref[c]
        m_chunk = jnp.max(s, axis=0, keepdims=True)
        pv = jnp.dot(vt_ref[:, pl.ds(off, kv)], jnp.exp2(s - m_old).astype(BF16), preferred_element_type=F32)
        m_new = jnp.maximum(m_old, m_chunk)
        acc_ref[c] = jnp.exp2(m_old - m_new) * (acc_ref[c] + pv)
        m_ref[c] = m_new
        excess_ref[c] = jnp.maximum(excess_ref[c], m_chunk - m_old)

    def all_chunks(step, masked):
        def body(kc, carry):
            off = pl.multiple_of(kc * kv, kv)
            for c in range(2):
                step(off, c, masked)
            return carry

        lax.fori_loop(0, n_chunks, body, 0)

    def run(masked):
        for c in range(2):
            m0 = jnp.max(scores(0, FLASH_INIT_KEYS, c, False), axis=0, keepdims=True)
            if masked:
                lane = lax.broadcasted_iota(jnp.int32, (1, tq), 1)
                m_ctx = jnp.max(scores(seq, FLASH_INIT_KEYS, c, False), axis=0, keepdims=True)
                m0 = jnp.where(lane >= tq - ctx_len, m_ctx, m0)
            m_ref[c] = m0
        excess_ref[...] = jnp.full_like(excess_ref, NEG_BIG)
        all_chunks(lazy_step, masked)

        @pl.when(jnp.max(excess_ref[...]) > FLASH_LAZY_HEADROOM)
        def _():
            m_ref[...] = jnp.full_like(m_ref, NEG_BIG)
            acc_ref[...] = jnp.zeros_like(acc_ref)
            all_chunks(exact_step, masked)

    @pl.when(i != last_q)
    def _():
        run(False)

    @pl.when(i == last_q)
    def _():
        run(True)

    a0 = acc_ref[0, :dv, :] / acc_ref[0, dv:dv + 1, :]
    a1 = acc_ref[1, :dv, :] / acc_ref[1, dv:dv + 1, :]
    o = (a0 - lam_ref[0] * a1).T
    o = o * lax.rsqrt(jnp.mean(o * o, axis=-1, keepdims=True) + 1e-5) * sub_ref[...]
    o_ref[...] = (o * out_scale).astype(o_ref.dtype)


def _rope_da_kernel(p_ref, cos_ref, sin_ref, qt_ref, k_ref, vt_ref):
    cos = cos_ref[...]
    sin = sin_ref[...]
    hw = 2 * DA_HEAD_DIM
    lane = lax.broadcasted_iota(jnp.int32, cos.shape, 1)
    first_half = (lane % DA_HEAD_DIM) < DA_HEAD_DIM // 2

    def rotated(b):
        x = p_ref[:, b * LANE:(b + 1) * LANE]
        rot = jnp.where(first_half, pltpu.roll(x, LANE - DA_HEAD_DIM // 2, 1), pltpu.roll(x, DA_HEAD_DIM // 2, 1))
        return x * cos + rot * sin

    ones = jnp.ones((FLASH_ONES_ROWS, cos.shape[0]), BF16)
    for h in range(DA_HEADS):
        qt_ref[h * hw:(h + 1) * hw, :] = (rotated(h) * (LOG2_E * DA_HEAD_DIM ** -0.5)).T.astype(BF16)
        k_ref[:, h * hw:(h + 1) * hw] = rotated(DA_HEADS + h).astype(BF16)
        base = h * (hw + FLASH_ONES_ROWS)
        vt_ref[base:base + hw, :] = p_ref[:, (2 * DA_HEADS + h) * LANE:(2 * DA_HEADS + h + 1) * LANE].T.astype(BF16)
        vt_ref[base + hw:base + hw + FLASH_ONES_ROWS, :] = ones


def rope_da(p, cos, sin):
    n_rows = p.shape[0]
    assert 2 * DA_HEAD_DIM == LANE
    vt_rows = DA_HEADS * (LANE + FLASH_ONES_ROWS)
    return pl.pallas_call(
        _rope_da_kernel,
        grid=(n_rows // ROW_TILE,),
        in_specs=[pl.BlockSpec((ROW_TILE, 3 * DA_WIDTH), lambda i: (i, 0)),
                  pl.BlockSpec((ROW_TILE, LANE), lambda i: (i, 0)),
                  pl.BlockSpec((ROW_TILE, LANE), lambda i: (i, 0))],
        out_specs=[pl.BlockSpec((DA_WIDTH, ROW_TILE), lambda i: (0, i)),
                   pl.BlockSpec((ROW_TILE, DA_WIDTH), lambda i: (i, 0)),
                   pl.BlockSpec((vt_rows, ROW_TILE), lambda i: (0, i))],
        out_shape=[jax.ShapeDtypeStruct((DA_WIDTH, n_rows), BF16),
                   jax.ShapeDtypeStruct((n_rows, DA_WIDTH), BF16),
                   jax.ShapeDtypeStruct((vt_rows, n_rows), BF16)],
        compiler_params=_params(("parallel",)),
        name="rope_da",
    )(p, cos, sin)


def diff_attention(qt, k, vt, lam_full, subln, *, seq, ctx_len, lambda_init):
    n_rows = k.shape[0]
    tq = _token_tile(n_rows)
    hw = 2 * DA_HEAD_DIM
    ones_rows = FLASH_ONES_ROWS
    return pl.pallas_call(
        functools.partial(_flash_kernel, kv=tq, seq=seq, ctx_len=ctx_len, out_scale=1.0 - lambda_init),
        grid=(DA_HEADS, n_rows // tq),
        in_specs=[pl.BlockSpec(memory_space=pltpu.SMEM),
                  pl.BlockSpec((hw, tq), lambda h, i: (h, i)),
                  pl.BlockSpec((n_rows, hw), lambda h, i: (0, h)),
                  pl.BlockSpec((hw + ones_rows, n_rows), lambda h, i: (h, 0)),
                  pl.BlockSpec((1, hw), lambda h, i: (0, 0))],
        out_specs=pl.BlockSpec((tq, hw), lambda h, i: (i, h)),
        out_shape=jax.ShapeDtypeStruct((n_rows, DA_WIDTH), BF16),
        scratch_shapes=[pltpu.VMEM((2, 1, tq), F32), pltpu.VMEM((2, 1, tq), F32),
                        pltpu.VMEM((2, hw + ones_rows, tq), F32)],
        compiler_params=pltpu.CompilerParams(dimension_semantics=("parallel", "parallel"),
                                             vmem_limit_bytes=FLASH_VMEM_LIMIT),
        name="diff_attention",
    )(lam_full.reshape(1), qt, k, vt, subln.reshape(1, hw))


def _route_kernel(lg_ref, b_ref, tri_ref, eidx_ref, w_ref, rank_ref, cnt_ref, carry_ref):
    t = lg_ref.shape[1]

    @pl.when(pl.program_id(0) == 0)
    def _():
        carry_ref[...] = jnp.zeros_like(carry_ref)

    scores = jax.nn.sigmoid(lg_ref[...])
    choice = (scores + b_ref[...]).reshape(N_GROUPS, GROUP_SIZE, t)
    s3 = scores.reshape(N_GROUPS, GROUP_SIZE, t)
    member = lax.broadcasted_iota(jnp.int32, choice.shape, 1)
    group = lax.broadcasted_iota(jnp.int32, (N_GROUPS, 1, t), 0)
    expert = lax.broadcasted_iota(jnp.int32, choice.shape, 0) * GROUP_SIZE + member
    neg_inf = -jnp.inf
    m1 = jnp.max(choice, axis=1, keepdims=True)
    first = jnp.min(jnp.where(choice == m1, member, GROUP_SIZE), axis=1, keepdims=True)
    m2 = jnp.max(jnp.where(member == first, neg_inf, choice), axis=1, keepdims=True)
    gscore = m1 + m2
    gsel = jnp.zeros(gscore.shape, F32)
    for _ in range(TOPK_GROUPS):
        m = jnp.max(gscore, axis=0, keepdims=True)
        f = jnp.min(jnp.where(gscore == m, group, N_GROUPS), axis=0, keepdims=True)
        hit = group == f
        gsel = jnp.where(hit, 1.0, gsel)
        gscore = jnp.where(hit, neg_inf, gscore)
    cand = jnp.where(gsel > 0.0, choice, neg_inf)
    esel = jnp.zeros(choice.shape, F32)
    picks = []
    for _ in range(TOP_K):
        m = jnp.max(jnp.max(cand, axis=1, keepdims=True), axis=0, keepdims=True)
        f = jnp.min(jnp.min(jnp.where(cand == m, expert, N_EXPERTS), axis=1, keepdims=True), axis=0, keepdims=True)
        hit = expert == f
        esel = jnp.where(hit, 1.0, esel)
        cand = jnp.where(hit, neg_inf, cand)
        picks.append(f)
    w = s3 * esel
    denom = jnp.sum(jnp.sum(w, axis=1, keepdims=True), axis=0, keepdims=True) + 1e-20
    w = w / denom * ROUTED_SCALE
    sel = esel.reshape(N_EXPERTS, t)
    before = jnp.dot(sel.astype(BF16), tri_ref[...], preferred_element_type=F32) + carry_ref[...]
    before = before.reshape(N_GROUPS, GROUP_SIZE, t)
    pick = lambda a, hit: jnp.sum(jnp.sum(jnp.where(hit, a, 0.0), axis=1, keepdims=True), axis=0).reshape(1, t)
    for k, f in enumerate(picks):
        hit = expert == f
        eidx_ref[k:k + 1, :] = f.reshape(1, t)
        w_ref[k:k + 1, :] = pick(w, hit)
        rank_ref[k:k + 1, :] = pick(before, hit).astype(jnp.int32)
    carry_ref[...] += jnp.sum(sel, axis=1, keepdims=True)
    cnt_ref[...] = carry_ref[...]


def route(logits_t, bias, lo, hi):
    t = _token_tile(logits_t.shape[1])
    n = hi - lo
    tile0 = lo // t
    tri = (jnp.arange(t)[:, None] < jnp.arange(t)[None, :]).astype(BF16)
    tok = lambda dt: jax.ShapeDtypeStruct((TOP_K, n), dt)
    tok_spec = pl.BlockSpec((TOP_K, t), lambda i: (0, i))
    return pl.pallas_call(
        _route_kernel,
        grid=(n // t,),
        in_specs=[pl.BlockSpec((N_EXPERTS, t), lambda i: (0, tile0 + i)),
                  pl.BlockSpec((N_EXPERTS, 1), lambda i: (0, 0)),
                  pl.BlockSpec((t, t), lambda i: (0, 0))],
        out_specs=[tok_spec, tok_spec, tok_spec, pl.BlockSpec((N_EXPERTS, 1), lambda i: (0, 0))],
        out_shape=[tok(jnp.int32), tok(F32), tok(jnp.int32), jax.ShapeDtypeStruct((N_EXPERTS, 1), F32)],
        scratch_shapes=[pltpu.VMEM((N_EXPERTS, 1), F32)],
        compiler_params=_params(("arbitrary",)),
        name="route",
    )(logits_t, bias.reshape(N_EXPERTS, 1), tri)


def _slot_kernel(start_ref, eidx_ref, rank_ref, dest_ref):
    e = eidx_ref[...]
    d = rank_ref[...]
    for x in range(N_EXPERTS):
        d = d + jnp.where(e == x, start_ref[x], 0)
    dest_ref[...] = d


def slot_index(pad_start, eidx, rank):
    n = eidx.shape[1]
    t = _token_tile(n)
    spec = pl.BlockSpec((TOP_K, t), lambda i: (0, i))
    return pl.pallas_call(
        _slot_kernel,
        grid=(n // t,),
        in_specs=[pl.BlockSpec(memory_space=pltpu.SMEM), spec, spec],
        out_specs=spec,
        out_shape=jax.ShapeDtypeStruct((TOP_K, n), jnp.int32),
        compiler_params=_params(("parallel",)),
        name="slot_index",
    )(pad_start, eidx, rank)


def _sc_worker():
    return lax.axis_index("s") * SC_CORES + lax.axis_index("c")


def sc_dispatch(h, dest3, n_slots, row0):
    d = h.shape[1]
    n_win = dest3.shape[0]
    mesh = plsc.VectorSubcoreMesh(core_axis_name="c", subcore_axis_name="s")

    @functools.partial(
        pl.kernel, mesh=mesh, out_type=jax.ShapeDtypeStruct((n_slots, d), h.dtype),
        scratch_types=[pltpu.VMEM((TOP_K, SC_WINDOW), jnp.int32), pltpu.VMEM((SC_WINDOW, d), h.dtype),
                       pltpu.SemaphoreType.DMA])
    def k(h_hbm, dest_hbm, out_hbm, idx_v, rows_v, sem):
        wid = _sc_worker()

        @pl.loop(0, -(-n_win // SC_WORKERS))
        def _(it):
            w = it * SC_WORKERS + wid

            @pl.when(w < n_win)
            def _():
                pltpu.sync_copy(dest_hbm.at[w], idx_v)
                pltpu.sync_copy(h_hbm.at[pl.ds(row0 + w * SC_WINDOW, SC_WINDOW)], rows_v)
                copies = [pltpu.async_copy(rows_v, out_hbm.at[idx_v.at[j]], sem) for j in range(TOP_K)]
                for c in copies:
                    c.wait()

    return k(h, dest3)


def sc_combine_gather(y, dest3):
    d = y.shape[1]
    n_win = dest3.shape[0]
    n = n_win * SC_WINDOW
    mesh = plsc.VectorSubcoreMesh(core_axis_name="c", subcore_axis_name="s")

    @functools.partial(
        pl.kernel, mesh=mesh, out_type=jax.ShapeDtypeStruct((TOP_K, n, d), y.dtype),
        scratch_types=[pltpu.VMEM((TOP_K, SC_WINDOW), jnp.int32), pltpu.VMEM((2, SC_WINDOW, d), y.dtype),
                       pltpu.SemaphoreType.DMA, pltpu.SemaphoreType.DMA,
                       pltpu.SemaphoreType.DMA, pltpu.SemaphoreType.DMA])
    def k(y_hbm, dest_hbm, out_hbm, idx_v, rows_v, gsem0, gsem1, osem0, osem1):
        wid = _sc_worker()
        gsem, osem = (gsem0, gsem1), (osem0, osem1)

        @pl.loop(0, -(-n_win // SC_WORKERS))
        def _(it):
            w = it * SC_WORKERS + wid

            @pl.when(w < n_win)
            def _():
                pltpu.sync_copy(dest_hbm.at[w], idx_v)
                gather = lambda j: pltpu.async_copy(y_hbm.at[idx_v.at[j]], rows_v.at[j % 2], gsem[j % 2])
                g = [None] * TOP_K
                o = [None] * TOP_K
                g[0] = gather(0)
                for j in range(TOP_K):
                    if j + 1 < TOP_K:
                        if j >= 1:
                            o[j - 1].wait()
                        g[j + 1] = gather(j + 1)
                    g[j].wait()
                    o[j] = pltpu.async_copy(rows_v.at[j % 2], out_hbm.at[j, pl.ds(w * SC_WINDOW, SC_WINDOW)],
                                            osem[j % 2])
                o[TOP_K - 2].wait()
                o[TOP_K - 1].wait()

    return k(y, dest3)


def _expert_ffn_kernel(be_ref, bv_ref, bs_ref, x_ref, wg_ref, wu_ref, wd_ref, o_ref, wg_s, wu_s, wd_s):
    b = pl.program_id(0)
    valid = bv_ref[b]
    new_expert = (b == 0) | (be_ref[b] != be_ref[jnp.maximum(b - 1, 0)])

    @pl.when(new_expert)
    def _():
        wg_s[...] = wg_ref[0, 0].astype(BF16)
        wu_s[...] = wu_ref[0, 0].astype(BF16)
        wd_s[...] = wd_ref[0, 0].astype(BF16)

    sub = x_ref.shape[0] // MOE_SUB_BLOCKS

    def sub_block(r):
        row = lax.broadcasted_iota(jnp.int32, (sub, 1), 0) + r * sub
        rows = pl.ds(r * sub, sub)
        lo, hi = _unpack_bf16_pair(jnp.where(row < valid, x_ref[rows, :], 0))
        x = jnp.concatenate([lo.astype(BF16), hi.astype(BF16)], axis=1)
        a = jnp.dot(x, wg_s[...], preferred_element_type=F32)
        a = a * jax.nn.sigmoid(a) * jnp.dot(x, wu_s[...], preferred_element_type=F32)
        y = jnp.dot(a.astype(BF16), wd_s[...], preferred_element_type=F32)
        half = y.shape[1] // 2
        o_ref[rows, :] = _pack_bf16_pair(y[:, :half], y[:, half:])

    for live in range(1, MOE_SUB_BLOCKS + 1):
        upper = valid <= live * sub if live < MOE_SUB_BLOCKS else True

        @pl.when((valid > (live - 1) * sub) & upper)
        def _():
            for r in range(live):
                sub_block(r)


def expert_ffn(xg, block_expert, block_valid, block_src, wg, wu, wd, layer):
    n_slots, dp = xg.shape
    d, f = wg.shape[-2:]
    grid_spec = pltpu.PrefetchScalarGridSpec(
        num_scalar_prefetch=3,
        grid=(n_slots // MOE_BLOCK,),
        in_specs=[pl.BlockSpec((MOE_BLOCK, dp), lambda b, be, bv, bs: (bs[b], 0)),
                  pl.BlockSpec((1, 1, d, f), lambda b, be, bv, bs: (layer, be[b], 0, 0)),
                  pl.BlockSpec((1, 1, d, f), lambda b, be, bv, bs: (layer, be[b], 0, 0)),
                  pl.BlockSpec((1, 1, f, d), lambda b, be, bv, bs: (layer, be[b], 0, 0))],
        out_specs=pl.BlockSpec((MOE_BLOCK, dp), lambda b, be, bv, bs: (bs[b], 0)),
        scratch_shapes=[pltpu.VMEM((d, f), BF16), pltpu.VMEM((d, f), BF16), pltpu.VMEM((f, d), BF16)],
    )
    return pl.pallas_call(
        _expert_ffn_kernel,
        grid_spec=grid_spec,
        out_shape=jax.ShapeDtypeStruct((n_slots, dp), jnp.int32),
        compiler_params=_params(("arbitrary",)),
        name="expert_ffn",
    )(block_expert, block_valid, block_src, xg, wg, wu, wd)


def _combine_kernel(yg_ref, w_ref, h_ref, swg_ref, swu_ref, swd_ref, xs_ref, mods_ref, *rest, tile0, gate_idx, seq):
    o_ref = rest[-1]
    h = h_ref[...]
    a = jnp.dot(h, swg_ref[...], preferred_element_type=F32)
    a = a * jax.nn.sigmoid(a) * jnp.dot(h, swu_ref[...], preferred_element_type=F32)
    acc = jnp.dot(a.astype(BF16), swd_ref[...], preferred_element_type=F32)
    half = acc.shape[1] // 2
    acc_lo, acc_hi = acc[:, :half], acc[:, half:]
    wt = w_ref[...].T
    for k in range(TOP_K):
        lo, hi = _unpack_bf16_pair(yg_ref[k])
        acc_lo = acc_lo + wt[:, k:k + 1] * lo
        acc_hi = acc_hi + wt[:, k:k + 1] * hi
    tm = h.shape[0]
    gate = _row_mod(mods_ref, gate_idx, (tile0 + pl.program_id(0)) * tm, tm, seq)
    o_ref[:, :half] = xs_ref[:, :half] + gate[:, :half] * acc_lo
    o_ref[:, half:] = xs_ref[:, half:] + gate[:, half:] * acc_hi


def combine(yg, w, h, swg, swu, swd, residual, lo, prev):
    xs, mods, gate_idx, seq = residual
    n_all, d = h.shape
    n = w.shape[1]
    f = swg.shape[-1]
    tm = ROW_TILE
    tile0 = lo // tm
    in_specs = [pl.BlockSpec((TOP_K, tm, d // 2), lambda i: (0, i, 0)),
                pl.BlockSpec((TOP_K, tm), lambda i: (0, i)),
                pl.BlockSpec((tm, d), lambda i: (tile0 + i, 0)),
                pl.BlockSpec((d, f), lambda i: (0, 0)),
                pl.BlockSpec((d, f), lambda i: (0, 0)),
                pl.BlockSpec((f, d), lambda i: (0, 0)),
                pl.BlockSpec((tm, d), lambda i: (tile0 + i, 0)),
                pl.BlockSpec((2, 6, d), lambda i: (0, 0, 0))]
    args = [yg, w, h, swg, swu, swd, xs, mods]
    aliases = {}
    if prev is not None:
        in_specs.append(pl.BlockSpec(memory_space=pl.ANY))
        args.append(prev)
        aliases = {len(args) - 1: 0}
    return pl.pallas_call(
        functools.partial(_combine_kernel, tile0=tile0, gate_idx=gate_idx, seq=seq),
        grid=(n // tm,),
        in_specs=in_specs,
        out_specs=pl.BlockSpec((tm, d), lambda i: (tile0 + i, 0)),
        out_shape=jax.ShapeDtypeStruct((n_all, d), F32),
        input_output_aliases=aliases,
        compiler_params=_params(("parallel",)),
        name="moe_combine",
    )(*args)


def moe(h, h_packed, logits_t, bias, wg, wu, wd, layer, swg, swu, swd, residual):
    n = h.shape[0]
    t = _token_tile(n)
    cut = (n // t + 1) // 2 * t
    shared = (swg.astype(BF16), swu.astype(BF16), swd.astype(BF16))
    staged = [_moe_experts(h_packed, logits_t, bias, wg, wu, wd, layer, lo, hi) for lo, hi in ((0, cut), (cut, n))]
    out = None
    for (yg, w), lo in zip(staged, (0, cut)):
        out = combine(yg, w, h, *shared, residual, lo, out)
    return out


def _moe_experts(h_packed, logits_t, bias, wg, wu, wd, layer, lo, hi):
    n = hi - lo
    eidx, w, rank, counts = route(logits_t, bias, lo, hi)
    counts = counts.reshape(N_EXPERTS).astype(jnp.int32)
    padded = (counts + MOE_BLOCK - 1) // MOE_BLOCK * MOE_BLOCK
    pad_end = jnp.cumsum(padded)
    pad_start = pad_end - padded
    n_slots = n * TOP_K + N_EXPERTS * MOE_BLOCK
    starts = jnp.arange(n_slots // MOE_BLOCK, dtype=jnp.int32) * MOE_BLOCK
    owner = jnp.sum((pad_end[None, :] <= starts[:, None]).astype(jnp.int32), axis=1)
    block_src = jnp.minimum(jnp.arange(starts.shape[0], dtype=jnp.int32), jnp.maximum(pad_end[-1] // MOE_BLOCK - 1, 0))
    owner = owner[block_src]
    block_expert = jnp.minimum(owner, N_EXPERTS - 1)
    member = (block_expert[:, None] == jnp.arange(N_EXPERTS, dtype=jnp.int32)[None, :]).astype(jnp.int32)
    left = jnp.sum(member * (counts + pad_start)[None, :], axis=1) - starts
    block_valid = jnp.clip(left, 0, MOE_BLOCK).astype(jnp.int32)
    dest = slot_index(pad_start.astype(jnp.int32), eidx, rank)
    dest3 = dest.reshape(TOP_K, n // SC_WINDOW, SC_WINDOW).transpose(1, 0, 2)
    xg = sc_dispatch(h_packed, dest3, n_slots, lo)
    y = expert_ffn(xg, block_expert, block_valid, block_src.astype(jnp.int32), wg, wu, wd, layer)
    return sc_combine_gather(y, dest3), w


def mixer_ab(h, w_in, w_out, decay_logit, conv_w, conv_b, w1, b1, f1, w2, b2, f2, w3, skip, rope, dft, residual, *,
             seq, ctx_len):
    n_rows = h.shape[0]
    tm = _token_tile(n_rows)
    assert RET_DK == LANE
    p = mm([(h, w_in.astype(BF16))], F32, tm, RET_QK, rope=(rope[0], rope[1], (1.0, RET_DK ** -0.5)),
           name="ab_in_proj")
    log_g = jax.nn.log_sigmoid(decay_logit.astype(F32))
    ret = retention(p, log_g, jnp.exp(RET_CHUNK * log_g), seq=seq)
    v, x1, x2 = shortconv(p, conv_w, conv_b, seq=seq)
    filt = (w1, b1, f1, w2, b2, f2, w3)
    hy_x = long_conv_two_stage(dft["x"], hyena_filter_taps_2d(seq, *filt), v, x1, x2, skip, seq)
    hy_c = long_conv_one_stage(dft["c"], hyena_filter_taps(ctx_len, *filt), v[seq:], x1[seq:], x2[seq:], skip)
    hy = jnp.concatenate([hy_x, hy_c], axis=0)
    w_out = w_out.astype(BF16)
    return mm([(ret, w_out[:RET_V]), (hy, w_out[RET_V:])], F32, tm, 512, residual=residual, name="ab_out_proj")


def mixer_da(h, w_in, w_out, lam, subln, lambda_init, rope, residual, *, seq, ctx_len):
    n_rows = h.shape[0]
    tm = _token_tile(n_rows)
    p = mm([(h, w_in.astype(BF16))], F32, tm, 512, name="da_in_proj")
    qt, k, vt = rope_da(p, rope[0], rope[1])
    lam_f = lam.astype(F32)
    lam_full = jnp.exp(jnp.sum(lam_f[0] * lam_f[1])) - jnp.exp(jnp.sum(lam_f[2] * lam_f[3])) + lambda_init
    o = diff_attention(qt, k, vt, lam_full, subln, seq=seq, ctx_len=ctx_len, lambda_init=lambda_init)
    return mm([(o, w_out.astype(BF16))], F32, tm, 512, residual=residual, name="da_out_proj")


def kernel(x, c, ctx, c_ctx, w_ada, b_ada, norm_mix, norm_ffn, ab_w_in, ab_w_out, ret_decay_logit, hy_conv_w, hy_conv_b, hy_w1, hy_b1, hy_freq1, hy_w2, hy_b2, hy_freq2, hy_w3, hy_skip, da_w_in, da_w_out, da_lambda, da_subln, router_w, router_b, exp_w_gate, exp_w_up, exp_w_down, sh_w_gate, sh_w_up, sh_w_down, norm_final):
    batch, seq, d = x.shape
    ctx_len = ctx.shape[1]
    assert batch == 1 and seq % ROW_TILE == 0 and ctx_len == ROW_TILE
    depth = w_ada.shape[0]
    n_rows = seq + ctx_len

    xs = jnp.concatenate([x[0], ctx[0]], axis=0)
    cv = jnp.zeros((SUBLANE, d), F32).at[0].set(c_ctx).at[1].set(c[0])
    mods = adaln(cv, w_ada, b_ada)[:, :2].reshape(depth, 2, 6, d)

    rope_ret = rope_tables(seq, ctx_len, RET_DK)
    rope_da = rope_tables(seq, ctx_len, DA_HEAD_DIM)
    dft = dict(x=dft_tables_two_stage(2 * seq), c=dft_tables_one_stage(2 * ctx_len))
    common = dict(n_rows=n_rows, seq=seq)

    for i in range(depth):
        j = i // 2
        (h,) = norm_mod(xs, norm_mix[i], mods=mods[i], shift_idx=0, scale_idx=1, **common)
        residual = (xs, mods[i], 2, seq)
        if i % 2 == 0:
            xs = mixer_ab(h, ab_w_in[j], ab_w_out[j], ret_decay_logit[j], hy_conv_w[j], hy_conv_b[j], hy_w1[j],
                          hy_b1[j], hy_freq1[j], hy_w2[j], hy_b2[j], hy_freq2[j], hy_w3[j], hy_skip[j], rope_ret, dft,
                          residual, seq=seq, ctx_len=ctx_len)
        else:
            lambda_init = 0.8 - 0.6 * math.exp(-0.3 * i)
            xs = mixer_da(h, da_w_in[j], da_w_out[j], da_lambda[j], da_subln[j], lambda_init, rope_da, residual,
                          seq=seq, ctx_len=ctx_len)
        h, logits_t, h_packed = norm_mod(xs, norm_ffn[i], mods=mods[i], shift_idx=3, scale_idx=4,
                                         router_wt=router_w[i].T, **common)
        xs = moe(h, h_packed, logits_t, router_b[i], exp_w_gate, exp_w_up, exp_w_down, i,
                 sh_w_gate[i], sh_w_up[i], sh_w_down[i], (xs, mods[i], 5, seq))
    (out,) = norm_mod(xs, norm_final, n_rows=seq, seq=seq, out_dtype=F32)
    return out[None]
```

```python
import functools
import math

import jax
import jax.numpy as jnp
from jax import lax
from jax.experimental import pallas as pl
from jax.experimental.pallas import tpu as pltpu
from jax.experimental.pallas import tpu_sc as plsc

F32 = jnp.float32
BF16 = jnp.bfloat16
HIGHEST = lax.Precision.HIGHEST

GRID_W = 64
EPS = 1e-6
ROPE_BASE = 10000.0

RET_HEADS = 4
RET_DK = 128
RET_DV = 256
RET_CHUNK = 128
RET_STEP_CHUNKS = 2
RET_QK = RET_HEADS * RET_DK
RET_V = RET_HEADS * RET_DV

HY_WIDTH = 512
HY_ORDER = 2
HY_BANDS = 16
HY_EMB = 2 * HY_BANDS + 1
HY_FFN = 64
HY_DECAY_TARGET = 1e-2
HY_FAST_DECAY = 0.3
HY_SLOW_DECAY = 1.5
HY_ZCOLS = 64
HY_VALID_COL = HY_EMB
FFT_N2 = 128

DA_HEADS = 8
DA_HEAD_DIM = 64
DA_WIDTH = DA_HEADS * 2 * DA_HEAD_DIM

N_EXPERTS = 64
TOP_K = 8
N_GROUPS = 8
TOPK_GROUPS = 4
GROUP_SIZE = N_EXPERTS // N_GROUPS
ROUTED_SCALE = 2.5
MOE_BLOCK = 512
MOE_SUB_BLOCKS = 2
SC_CORES = 2
SC_SUBCORES = 16
SC_WORKERS = SC_CORES * SC_SUBCORES
SC_WINDOW = 64

LANE = 128
SUBLANE = 8
ROW_TILE = 256
MAX_TOKEN_TILE = 1280
VMEM_LIMIT = 48 * 1024 * 1024
FLASH_VMEM_LIMIT = 56 * 1024 * 1024
NEG_BIG = -1e30
LOG2_E = 1.4426950408889634
FLASH_ONES_ROWS = 16
FLASH_INIT_KEYS = 16
FLASH_LAZY_HEADROOM = 60.0


def _params(sem):
    return pltpu.CompilerParams(dimension_semantics=sem, vmem_limit_bytes=VMEM_LIMIT)


def _token_tile(n):
    best = ROW_TILE
    t = ROW_TILE
    while t <= min(n, MAX_TOKEN_TILE):
        if n % t == 0:
            best = t
        t += ROW_TILE
    return best


def _row_mod(mods_ref, idx, row0, n, seq):
    row = row0 + lax.broadcasted_iota(jnp.int32, (n, 1), 0)
    return jnp.where(row >= seq, mods_ref[0, idx:idx + 1, :], mods_ref[1, idx:idx + 1, :])


def _mm_kernel(*refs, n_pairs, has_epi, gate_idx, seq, rope_scales):
    acc = None
    for p in range(n_pairs):
        a = refs[2 * p][...].astype(BF16)
        b = refs[2 * p + 1][...].astype(BF16)
        d = jnp.dot(a, b, preferred_element_type=F32)
        acc = d if acc is None else acc + d
    idx = 2 * n_pairs
    if has_epi:
        acc = refs[idx][...] * (acc + refs[idx + 1][...] * refs[idx + 2][...])
        idx += 3
    if gate_idx is not None:
        tm = acc.shape[0]
        acc = refs[idx][...] + _row_mod(refs[idx + 1], gate_idx, pl.program_id(0) * tm, tm, seq) * acc
        idx += 2
    if rope_scales is None:
        o_ref = refs[idx]
        o_ref[...] = acc.astype(o_ref.dtype)
        return
    cos, sin = refs[idx][...], refs[idx + 1][...]
    o_ref = refs[idx + 2]
    j = pl.program_id(1)

    @pl.when(j >= len(rope_scales))
    def _():
        o_ref[...] = acc.astype(o_ref.dtype)

    for t, scale in enumerate(rope_scales):
        @pl.when(j == t)
        def _():
            for b in range(acc.shape[1] // LANE):
                x = acc[:, b * LANE:(b + 1) * LANE]
                x = x * cos + pltpu.roll(x, LANE // 2, 1) * sin
                o_ref[:, b * LANE:(b + 1) * LANE] = (x * scale if scale != 1.0 else x).astype(o_ref.dtype)


def mm(pairs, out_dtype, tm, tn, epi=None, residual=None, rope=None, name="mm"):
    m = pairs[0][0].shape[0]
    n = pairs[0][1].shape[1]
    assert m % tm == 0 and n % tn == 0
    in_specs, args = [], []
    for a, b in pairs:
        k = a.shape[1]
        in_specs += [pl.BlockSpec((tm, k), lambda i, j: (i, 0)), pl.BlockSpec((k, tn), lambda i, j: (0, j))]
        args += [a, b]
    if epi is not None:
        in_specs += [pl.BlockSpec((tm, tn), lambda i, j: (i, j)), pl.BlockSpec((1, tn), lambda i, j: (0, j)),
                     pl.BlockSpec((tm, tn), lambda i, j: (i, j))]
        args += list(epi)
    gate_idx = seq = None
    if residual is not None:
        res, mods, gate_idx, seq = residual
        in_specs += [pl.BlockSpec((tm, tn), lambda i, j: (i, j)), pl.BlockSpec((2, 6, tn), lambda i, j: (0, 0, j))]
        args += [res, mods]
    rope_scales = None
    if rope is not None:
        cos, sin, rope_scales = rope
        in_specs += [pl.BlockSpec((tm, LANE), lambda i, j: (i, 0)), pl.BlockSpec((tm, LANE), lambda i, j: (i, 0))]
        args += [cos, sin]
    return pl.pallas_call(
        functools.partial(_mm_kernel, n_pairs=len(pairs), has_epi=epi is not None, gate_idx=gate_idx, seq=seq,
                          rope_scales=rope_scales),
        grid=(m // tm, n // tn),
        in_specs=in_specs,
        out_specs=pl.BlockSpec((tm, tn), lambda i, j: (i, j)),
        out_shape=jax.ShapeDtypeStruct((m, n), out_dtype),
        compiler_params=_params(("parallel", "parallel")),
        name=name,
    )(*args)


def _adaln_kernel(cv_ref, w_ref, b_ref, o_ref):
    cv = cv_ref[...]
    s = cv * jax.nn.sigmoid(cv)
    o_ref[0] = jnp.dot(s, w_ref[0], precision=HIGHEST, preferred_element_type=F32) + b_ref[0]


def adaln(cv, w_ada, b_ada):
    depth, d, n = w_ada.shape
    tn = 1536
    return pl.pallas_call(
        _adaln_kernel,
        grid=(depth, n // tn),
        in_specs=[pl.BlockSpec((SUBLANE, d), lambda l, j: (0, 0)),
                  pl.BlockSpec((1, d, tn), lambda l, j: (l, 0, j)),
                  pl.BlockSpec((1, 1, tn), lambda l, j: (l, 0, j))],
        out_specs=pl.BlockSpec((1, SUBLANE, tn), lambda l, j: (l, 0, j)),
        out_shape=jax.ShapeDtypeStruct((depth, SUBLANE, n), F32),
        compiler_params=_params(("parallel", "parallel")),
        name="adaln",
    )(cv, w_ada, b_ada.reshape(depth, 1, n))


def _norm_mod_kernel(*refs, shift_idx, scale_idx, has_router, seq):
    it = iter(refs)
    x = next(it)[...]
    mods_ref = next(it) if shift_idx is not None else None
    g_ref = next(it)
    wr_ref = next(it) if has_router else None
    h_ref = next(it)
    y = x * lax.rsqrt(jnp.mean(x * x, axis=-1, keepdims=True) + EPS) * g_ref[...]
    if shift_idx is not None:
        tm = x.shape[0]
        row0 = pl.program_id(0) * tm
        y = y * (1.0 + _row_mod(mods_ref, scale_idx, row0, tm, seq)) + _row_mod(mods_ref, shift_idx, row0, tm, seq)
    h_ref[...] = y.astype(h_ref.dtype)
    if has_router:
        lg_ref = next(it)
        lg_ref[...] = lax.dot_general(wr_ref[...], y, (((1,), (1,)), ((), ())),
                                      precision=HIGHEST, preferred_element_type=F32)
        half = y.shape[1] // 2
        next(it)[...] = _pack_bf16_pair(y[:, :half], y[:, half:])


def _pack_bf16_pair(a, b):
    bits = lambda x: lax.bitcast_convert_type(x.astype(BF16), jnp.uint16).astype(jnp.int32)
    return bits(a) | lax.shift_left(bits(b), 16)


def _unpack_bf16_pair(w):
    return (lax.bitcast_convert_type(lax.shift_left(w, 16), F32),
            lax.bitcast_convert_type(w & -65536, F32))


def norm_mod(xs, g, *, n_rows, seq, mods=None, shift_idx=None, scale_idx=None, router_wt=None, out_dtype=BF16):
    d = xs.shape[1]
    tm = _token_tile(n_rows)
    row = pl.BlockSpec((tm, d), lambda i: (i, 0))
    in_specs, args = [row], [xs]
    if shift_idx is not None:
        in_specs.append(pl.BlockSpec((2, 6, d), lambda i: (0, 0, 0)))
        args.append(mods)
    in_specs.append(pl.BlockSpec((1, d), lambda i: (0, 0)))
    args.append(g.reshape(1, d))
    has_router = router_wt is not None
    if has_router:
        in_specs.append(pl.BlockSpec(router_wt.shape, lambda i: (0, 0)))
        args.append(router_wt)
    out_specs = [row]
    out_shape = [jax.ShapeDtypeStruct((n_rows, d), out_dtype)]
    if has_router:
        out_specs.append(pl.BlockSpec((N_EXPERTS, tm), lambda i: (0, i)))
        out_shape.append(jax.ShapeDtypeStruct((N_EXPERTS, n_rows), F32))
        out_specs.append(pl.BlockSpec((tm, d // 2), lambda i: (i, 0)))
        out_shape.append(jax.ShapeDtypeStruct((n_rows, d // 2), jnp.int32))
    return pl.pallas_call(
        functools.partial(_norm_mod_kernel, shift_idx=shift_idx, scale_idx=scale_idx, has_router=has_router, seq=seq),
        grid=(n_rows // tm,),
        in_specs=in_specs,
        out_specs=out_specs,
        out_shape=out_shape,
        compiler_params=_params(("parallel",)),
        name="norm_mod",
    )(*args)


def rope_tables(seq, ctx_len, head_dim):
    n_freq = head_dim // 4
    inv = ROPE_BASE ** (-jnp.arange(n_freq, dtype=F32) / n_freq)
    rows = seq // GRID_W
    row = jnp.repeat(jnp.arange(rows, dtype=F32), GRID_W)
    col = jnp.tile(jnp.arange(GRID_W, dtype=F32), rows)
    ang = jnp.concatenate([row[:, None] * inv, col[:, None] * inv], axis=-1)
    cos, sin = jnp.cos(ang), jnp.sin(ang)
    cos = jnp.concatenate([cos, cos], axis=-1)
    sin = jnp.concatenate([-sin, sin], axis=-1)
    reps = LANE // head_dim
    cos, sin = jnp.tile(cos, (1, reps)), jnp.tile(sin, (1, reps))
    cos = jnp.concatenate([cos, jnp.ones((ctx_len, LANE), F32)], axis=0)
    sin = jnp.concatenate([sin, jnp.zeros((ctx_len, LANE), F32)], axis=0)
    return cos, sin


def _ret_kernel(lg_ref, gc_ref, q_ref, k_ref, v_ref, *rest, reverse):
    if reverse:
        yf_ref, gate_ref, o_ref, s_ref = rest
    else:
        o_ref, s_ref = rest
    c = RET_CHUNK

    @pl.when(pl.program_id(0) == 0)
    def _():
        s_ref[...] = jnp.zeros_like(s_ref)

    ii = lax.broadcasted_iota(jnp.int32, (c, c), 0)
    jj = lax.broadcasted_iota(jnp.int32, (c, c), 1)
    rel = ((jj - ii) if reverse else (ii - jj)).astype(F32)
    pos = lax.broadcasted_iota(jnp.int32, (c, 1), 0).astype(F32)
    for h in range(RET_HEADS):
        lg = lg_ref[h]
        dec = jnp.where(rel >= 0, jnp.exp(jnp.maximum(rel, 0.0) * lg), 0.0)
        if reverse:
            q_dec = jnp.exp((c - pos) * lg)
            k_dec = jnp.exp(pos * lg)
        else:
            q_dec = jnp.exp((pos + 1.0) * lg)
            k_dec = jnp.exp((c - 1.0 - pos) * lg)
        state = s_ref[h]
        qk_cols = slice(h * RET_DK, (h + 1) * RET_DK)
        v_cols = slice(h * RET_DV, (h + 1) * RET_DV)
        for sub in (range(RET_STEP_CHUNKS - 1, -1, -1) if reverse else range(RET_STEP_CHUNKS)):
            rows = slice(sub * c, (sub + 1) * c)
            q = q_ref[rows, qk_cols]
            k = k_ref[rows, qk_cols]
            v = v_ref[rows, v_cols].astype(BF16)
            s = lax.dot_general(q.astype(BF16), k.astype(BF16), (((1,), (1,)), ((), ())),
                                preferred_element_type=F32) * dec
            y = jnp.dot(s.astype(BF16), v, preferred_element_type=F32)
            y = y + jnp.dot((q * q_dec).astype(BF16), state.astype(BF16), preferred_element_type=F32)
            upd = lax.dot_general((k * k_dec).astype(BF16), v, (((0,), (0,)), ((), ())), preferred_element_type=F32)
            state = gc_ref[h] * state + upd
            if reverse:
                r = y + yf_ref[rows, v_cols]
                mu = jnp.mean(r, axis=-1, keepdims=True)
                rc = r - mu
                var = jnp.mean(rc * rc, axis=-1, keepdims=True)
                g = gate_ref[rows, v_cols]
                o_ref[rows, v_cols] = (rc * lax.rsqrt(var + EPS) * (g * jax.nn.sigmoid(g))).astype(o_ref.dtype)
            else:
                o_ref[rows, v_cols] = y
        s_ref[h] = state


def retention(p, log_g, g_chunk, *, seq):
    n_rows = p.shape[0]
    step = RET_STEP_CHUNKS * RET_CHUNK
    assert seq % step == 0 and n_rows % step == 0
    n_steps = n_rows // step
    n_x = seq // step
    smem = pl.BlockSpec(memory_space=pltpu.SMEM)

    def run(reverse, extra):
        if reverse:
            idx = lambda t: n_steps - 1 - t
        else:
            idx = lambda t: (t + n_x) % n_steps
        in_specs = [smem, smem,
                    pl.BlockSpec((step, RET_QK), lambda t: (idx(t), 0)),
                    pl.BlockSpec((step, RET_QK), lambda t: (idx(t), 1)),
                    pl.BlockSpec((step, RET_V), lambda t: (idx(t), 1))]
        args = [log_g[1 if reverse else 0], g_chunk[1 if reverse else 0], p, p, p]
        if reverse:
            in_specs += [pl.BlockSpec((step, RET_V), lambda t: (idx(t), 0)),
                         pl.BlockSpec((step, RET_V), lambda t: (idx(t), 2))]
            args += list(extra)
        return pl.pallas_call(
            functools.partial(_ret_kernel, reverse=reverse),
            grid=(n_steps,),
            in_specs=in_specs,
            out_specs=pl.BlockSpec((step, RET_V), lambda t: (idx(t), 0)),
            out_shape=jax.ShapeDtypeStruct((n_rows, RET_V), BF16 if reverse else F32),
            scratch_shapes=[pltpu.VMEM((RET_HEADS, RET_DK, RET_DV), F32)],
            compiler_params=_params(("arbitrary",)),
            name="retention_bwd" if reverse else "retention_fwd",
        )(*args)

    y_fwd = run(False, None)
    return run(True, (y_fwd, p))


def _shortconv_kernel(cur_ref, prev_ref, next_ref, w_ref, b_ref, v_ref, x1_ref, x2_ref, *, x_tiles):
    i = pl.program_id(0)
    cur = cur_ref[...]
    rows = cur.shape[0]
    row = lax.broadcasted_iota(jnp.int32, (rows, 1), 0)
    has_prev = jnp.where((i == 0) | (i == x_tiles), 0.0, 1.0)
    has_next = jnp.where((i == x_tiles - 1) | (i == x_tiles), 0.0, 1.0)
    up = jnp.where(row == 0, prev_ref[SUBLANE - 1:SUBLANE, :] * has_prev, pltpu.roll(cur, 1, 0))
    dn = jnp.where(row == rows - 1, next_ref[0:1, :] * has_next, pltpu.roll(cur, rows - 1, 0))
    y = up * w_ref[0:1, :] + cur * w_ref[1:2, :] + dn * w_ref[2:3, :] + b_ref[...]
    v_ref[...] = y[:, :HY_WIDTH]
    x1_ref[...] = y[:, HY_WIDTH:2 * HY_WIDTH]
    x2_ref[...] = y[:, 2 * HY_WIDTH:]


def shortconv(p, w, b, *, seq):
    n_rows = p.shape[0]
    width = 3 * HY_WIDTH
    col = p.shape[1] // width - 1
    per = ROW_TILE // SUBLANE
    last = n_rows // SUBLANE - 1
    out = jax.ShapeDtypeStruct((n_rows, HY_WIDTH), F32)
    ospec = pl.BlockSpec((ROW_TILE, HY_WIDTH), lambda i: (i, 0))
    return pl.pallas_call(
        functools.partial(_shortconv_kernel, x_tiles=seq // ROW_TILE),
        grid=(n_rows // ROW_TILE,),
        in_specs=[pl.BlockSpec((ROW_TILE, width), lambda i: (i, col)),
                  pl.BlockSpec((SUBLANE, width), lambda i: (jnp.maximum(i * per - 1, 0), col)),
                  pl.BlockSpec((SUBLANE, width), lambda i: (jnp.minimum((i + 1) * per, last), col)),
                  pl.BlockSpec((3, width), lambda i: (0, 0)),
                  pl.BlockSpec((1, width), lambda i: (0, 0))],
        out_specs=[ospec, ospec, ospec],
        out_shape=[out, out, out],
        compiler_params=_params(("parallel",)),
        name="shortconv",
    )(p, p, p, w, b.reshape(1, width))


def _filt_kernel(z_ref, w1_ref, b1_ref, f1_ref, w2_ref, b2_ref, f2_ref, w3a_ref, w3b_ref, dl_ref, *o_ref):
    z = z_ref[...]
    h = jnp.sin(f1_ref[...] * (jnp.dot(z, w1_ref[...], precision=HIGHEST, preferred_element_type=F32) + b1_ref[...]))
    h = jnp.sin(f2_ref[...] * (jnp.dot(h, w2_ref[...], precision=HIGHEST, preferred_element_type=F32) + b2_ref[...]))
    window = jnp.exp(-z[:, 0:1] * dl_ref[...]) * z[:, HY_VALID_COL:HY_VALID_COL + 1]
    for o, w3_ref in enumerate((w3a_ref, w3b_ref)):
        o_ref[o][...] = jnp.dot(h, w3_ref[...], precision=HIGHEST, preferred_element_type=F32) * window


def hyena_filter_taps(length, w1, b1, f1, w2, b2, f2, w3):
    z = _filter_positions(length)
    w1p = jnp.zeros((HY_ZCOLS, HY_FFN), F32).at[:HY_EMB].set(w1)
    deltas = jnp.abs(jnp.linspace(math.log(HY_DECAY_TARGET) / HY_SLOW_DECAY,
                                  math.log(HY_DECAY_TARGET) / HY_FAST_DECAY, HY_WIDTH, dtype=F32)).reshape(1, HY_WIDTH)
    tm = min(length, 512)
    half_tiles = length // tm
    vec = lambda a: a.reshape(1, HY_FFN)
    small = lambda shape: pl.BlockSpec(shape, lambda i: (0, 0))
    w3_spec = lambda o: pl.BlockSpec((HY_FFN, HY_WIDTH), lambda i: (0, 2 * o + jnp.where(i >= half_tiles, 1, 0)))
    assert HY_ORDER == 2
    return pl.pallas_call(
        _filt_kernel,
        grid=(2 * half_tiles,),
        in_specs=[pl.BlockSpec((tm, HY_ZCOLS), lambda i: (i, 0)),
                  small((HY_ZCOLS, HY_FFN)), small((1, HY_FFN)), small((1, HY_FFN)),
                  small((HY_FFN, HY_FFN)), small((1, HY_FFN)), small((1, HY_FFN)),
                  w3_spec(0), w3_spec(1), small((1, HY_WIDTH))],
        out_specs=[pl.BlockSpec((tm, HY_WIDTH), lambda i: (i, 0))] * HY_ORDER,
        out_shape=[jax.ShapeDtypeStruct((2 * length, HY_WIDTH), F32)] * HY_ORDER,
        compiler_params=_params(("parallel",)),
        name="hyena_filter",
    )(z, w1p, vec(b1), vec(f1), w2, vec(b2), vec(f2), w3, w3, deltas)


def _filt2d_kernel(z_ref, w1_ref, b1_ref, f1_ref, w2_ref, b2_ref, f2_ref, w3fa_ref, w3ba_ref, w3fb_ref, w3bb_ref,
                   dl_ref, oa_ref, ob_ref):
    n1 = oa_ref.shape[0]
    c = dl_ref.shape[1]
    z = z_ref[...]
    h = jnp.sin(f1_ref[...] * (_dot_split(_split_bf16(z), _split_bf16(w1_ref[...])) + b1_ref[...]))
    h = jnp.sin(f2_ref[...] * (_dot_split(_split_bf16(h), _split_bf16(w2_ref[...])) + b2_ref[...]))
    window = jnp.exp(-z[:, 0:1] * dl_ref[...]) * z[:, HY_VALID_COL:HY_VALID_COL + 1]
    h_split = _split_bf16(h)
    for o_ref, wf_ref, wb_ref in ((oa_ref, w3fa_ref, w3ba_ref), (ob_ref, w3fb_ref, w3bb_ref)):
        wf_split, wb_split = _split_bf16(wf_ref[...]), _split_bf16(wb_ref[...])
        for j in range(SUBLANE):
            fwd_rows = slice(j * n1, j * n1 + n1 // 2)
            bwd_rows = slice(j * n1 + n1 // 2, (j + 1) * n1)
            taps = jnp.concatenate([_dot_split(tuple(p[fwd_rows] for p in h_split), wf_split),
                                    _dot_split(tuple(p[bwd_rows] for p in h_split), wb_split)], axis=0)
            o_ref[:, j * c:(j + 1) * c] = taps * window[j * n1:(j + 1) * n1]


def _split_bf16(a):
    hi = a.astype(BF16)
    return hi, (a - hi.astype(F32)).astype(BF16)


def _dot_split(a, b):
    d = lambda x, y: jnp.dot(x, y, preferred_element_type=F32)
    return d(a[0], b[0]) + d(a[0], b[1]) + d(a[1], b[0])


def hyena_filter_taps_2d(length, w1, b1, f1, w2, b2, f2, w3):
    n1 = 2 * length // FFT_N2
    groups = FFT_N2 // SUBLANE
    z = _filter_positions(length).reshape(n1, groups, SUBLANE, HY_ZCOLS).transpose(1, 2, 0, 3)
    z = z.reshape(2 * length, HY_ZCOLS)
    w1p = jnp.zeros((HY_ZCOLS, HY_FFN), F32).at[:HY_EMB].set(w1)
    deltas = jnp.abs(jnp.linspace(math.log(HY_DECAY_TARGET) / HY_SLOW_DECAY,
                                  math.log(HY_DECAY_TARGET) / HY_FAST_DECAY, HY_WIDTH, dtype=F32)).reshape(1, HY_WIDTH)
    vec = lambda a: a.reshape(1, HY_FFN)
    small = lambda shape: pl.BlockSpec(shape, lambda i: (0, 0))
    w3_spec = lambda col: pl.BlockSpec((HY_FFN, HY_WIDTH), lambda i: (0, col))
    assert HY_ORDER == 2
    out = jax.ShapeDtypeStruct((n1, FFT_N2 * HY_WIDTH), F32)
    return pl.pallas_call(
        _filt2d_kernel,
        grid=(groups,),
        in_specs=[pl.BlockSpec((SUBLANE * n1, HY_ZCOLS), lambda i: (i, 0)),
                  small((HY_ZCOLS, HY_FFN)), small((1, HY_FFN)), small((1, HY_FFN)),
                  small((HY_FFN, HY_FFN)), small((1, HY_FFN)), small((1, HY_FFN)),
                  w3_spec(0), w3_spec(1), w3_spec(2), w3_spec(3), small((1, HY_WIDTH))],
        out_specs=[pl.BlockSpec((n1, SUBLANE * HY_WIDTH), lambda i: (0, i))] * HY_ORDER,
        out_shape=[out] * HY_ORDER,
        compiler_params=_params(("parallel",)),
        name="hyena_filter",
    )(z, w1p, vec(b1), vec(f1), w2, vec(b2), vec(f2), w3, w3, w3, w3, deltas)


def _filter_positions(length):
    t = jnp.concatenate([jnp.arange(length, dtype=F32), float(length) - jnp.arange(length, dtype=F32)])
    valid = jnp.ones((2 * length,), F32).at[length].set(0.0)
    t_norm = t / max(length - 1, 1)
    bands = jnp.linspace(1e-4, HY_BANDS - 1, HY_BANDS, dtype=F32)
    ang = (2.0 * math.pi / length) * t[:, None] * bands[None, :]
    z = jnp.concatenate([t_norm[:, None], jnp.cos(ang), -jnp.sin(ang), valid[:, None]], axis=-1)
    return jnp.pad(z, ((0, 0), (0, HY_ZCOLS - z.shape[1])))


def _angles(num, den):
    return (2.0 * math.pi / den) * (num % den).astype(F32)


def dft_tables_two_stage(m):
    n2 = FFT_N2
    n1 = m // n2
    half = n1 // 2
    kp = -(-(half + 1) // SUBLANE) * SUBLANE
    k1 = jnp.arange(kp, dtype=jnp.int32)
    live = (k1 <= half)
    a1 = _angles(k1[:, None] * jnp.arange(n1, dtype=jnp.int32)[None, :], n1)
    f1 = jnp.concatenate([jnp.where(live[:, None], jnp.cos(a1), 0.0), jnp.where(live[:, None], -jnp.sin(a1), 0.0)], 0)
    wgt = jnp.where((k1 == 0) | (k1 == half), 1.0, 2.0) * live / m
    a1h = a1[:, :half].T
    cinv = jnp.concatenate([jnp.cos(a1h) * wgt[None, :], -jnp.sin(a1h) * wgt[None, :]], axis=1)
    k = k1[:, None, None] + n1 * jnp.arange(n2, dtype=jnp.int32)[None, :, None]
    th = _angles(k * jnp.arange(n2, dtype=jnp.int32)[None, None, :], m)
    c = jnp.where(live[:, None, None], jnp.cos(th), 0.0)
    s = jnp.where(live[:, None, None], jnp.sin(th), 0.0)
    g_fwd = jnp.concatenate([jnp.concatenate([c, s], 2), jnp.concatenate([-s, c], 2)], 1)
    ct, st = jnp.swapaxes(c, 1, 2), jnp.swapaxes(s, 1, 2)
    g_inv = jnp.concatenate([jnp.concatenate([ct, -st], 2), jnp.concatenate([st, ct], 2)], 1)
    return dict(n1=n1, kp=kp, f1=f1.astype(BF16), f1_half=f1[:, :half].astype(BF16), cinv=cinv.astype(BF16),
                g_fwd=g_fwd.astype(BF16), g_inv=g_inv.astype(BF16))


def dft_tables_one_stage(m):
    half = m // 2
    kp = -(-(half + 1) // SUBLANE) * SUBLANE
    k = jnp.arange(kp, dtype=jnp.int32)
    live = (k <= half)
    a = _angles(k[:, None] * jnp.arange(m, dtype=jnp.int32)[None, :], m)
    f = jnp.concatenate([jnp.where(live[:, None], jnp.cos(a), 0.0), jnp.where(live[:, None], -jnp.sin(a), 0.0)], 0)
    wgt = jnp.where((k == 0) | (k == half), 1.0, 2.0) * live / m
    ah = a[:, :half].T
    cinv = jnp.concatenate([jnp.cos(ah) * wgt[None, :], -jnp.sin(ah) * wgt[None, :]], axis=1)
    return dict(kp=kp, f=f.astype(BF16), f_half=f[:, :half].astype(BF16), cinv=cinv.astype(BF16))


def _bmm_kernel(*refs, kb, in_part_major, out_part_major, has_h):
    if has_h:
        g_ref, a_ref, h_ref, o_ref = refs
    else:
        g_ref, a_ref, o_ref = refs
    n2 = FFT_N2
    for b in range(kb):
        if in_part_major:
            ar, ai = a_ref[0, b], a_ref[1, b]
        else:
            ar, ai = a_ref[b, 0], a_ref[b, 1]
        if has_h:
            hr, hi = h_ref[b, 0], h_ref[b, 1]
            ar, ai = ar * hr - ai * hi, ar * hi + ai * hr
        xin = jnp.concatenate([ar, ai], axis=0).astype(BF16)
        y = jnp.dot(g_ref[b], xin, preferred_element_type=F32)
        if out_part_major:
            o_ref[0, b] = y[:n2].astype(o_ref.dtype)
            o_ref[1, b] = y[n2:].astype(o_ref.dtype)
        else:
            o_ref[b, 0] = y[:n2].astype(o_ref.dtype)
            o_ref[b, 1] = y[n2:].astype(o_ref.dtype)


def bmm_k1(g, a, h=None, *, in_part_major, out_part_major):
    kp = g.shape[0]
    n2 = FFT_N2
    c = a.shape[-1]
    kb, tc = SUBLANE, min(c, 512)
    pm = lambda: pl.BlockSpec((2, kb, n2, tc), lambda i, j: (0, i, 0, j))
    km = lambda: pl.BlockSpec((kb, 2, n2, tc), lambda i, j: (i, 0, 0, j))
    in_specs = [pl.BlockSpec((kb, 2 * n2, 2 * n2), lambda i, j: (i, 0, 0)), pm() if in_part_major else km()]
    args = [g, a]
    if h is not None:
        in_specs.append(km())
        args.append(h)
    return pl.pallas_call(
        functools.partial(_bmm_kernel, kb=kb, in_part_major=in_part_major, out_part_major=out_part_major,
                          has_h=h is not None),
        grid=(kp // kb, c // tc),
        in_specs=in_specs,
        out_specs=pm() if out_part_major else km(),
        out_shape=jax.ShapeDtypeStruct((2, kp, n2, c), BF16) if out_part_major else
        jax.ShapeDtypeStruct((kp, 2, n2, c), F32),
        compiler_params=_params(("parallel", "parallel")),
        name="dft_inner",
    )(*args)


def _cmul_kernel(x_ref, h_ref, o_ref):
    xr, xi, hr, hi = x_ref[0], x_ref[1], h_ref[0], h_ref[1]
    o_ref[0] = xr * hr - xi * hi
    o_ref[1] = xr * hi + xi * hr


def cmul(x, h):
    spec = pl.BlockSpec(x.shape, lambda i: (0, 0, 0))
    return pl.pallas_call(_cmul_kernel, grid=(1,), in_specs=[spec, spec], out_specs=spec,
                          out_shape=jax.ShapeDtypeStruct(x.shape, F32), compiler_params=_params(("arbitrary",)),
                          name="spectrum_product")(x, h)


def _dft_outer3_kernel(f_ref, x_ref, o_ref):
    c = x_ref.shape[2]
    f = f_ref[...]
    for j in range(SUBLANE):
        o_ref[:, j * c:(j + 1) * c] = jnp.dot(f, x_ref[:, j, :].astype(BF16),
                                              preferred_element_type=F32).astype(o_ref.dtype)


def dft_outer3(f, x3, n_outer):
    rows = f.shape[0]
    c = x3.shape[2]
    return pl.pallas_call(
        _dft_outer3_kernel,
        grid=(FFT_N2 // SUBLANE,),
        in_specs=[pl.BlockSpec((rows, n_outer), lambda j: (0, 0)),
                  pl.BlockSpec((n_outer, SUBLANE, c), lambda j: (0, j, 0))],
        out_specs=pl.BlockSpec((rows, SUBLANE * c), lambda j: (0, j)),
        out_shape=jax.ShapeDtypeStruct((rows, FFT_N2 * c), BF16),
        compiler_params=_params(("parallel",)),
        name="dft_outer",
    )(f, x3)


def _idft_gate3_kernel(cinv_ref, b_ref, gate_ref, skip_ref, u_ref, o_ref, *, u_is_3d):
    c = gate_ref.shape[2]
    cinv = cinv_ref[...]
    for j in range(SUBLANE):
        cols = slice(j * c, (j + 1) * c)
        acc = jnp.dot(cinv, b_ref[:, cols].astype(BF16), preferred_element_type=F32)
        u = u_ref[:, j, :] if u_is_3d else u_ref[:, cols]
        o_ref[:, cols] = gate_ref[:, j, :] * (acc + skip_ref[...] * u)


def idft_gate3(cinv, b2d, gate3, skip_row, u):
    n_outer = cinv.shape[0]
    c = gate3.shape[2]
    u_is_3d = u.ndim == 3
    wide = pl.BlockSpec((n_outer, SUBLANE * c), lambda j: (0, j))
    slab = pl.BlockSpec((n_outer, SUBLANE, c), lambda j: (0, j, 0))
    return pl.pallas_call(
        functools.partial(_idft_gate3_kernel, u_is_3d=u_is_3d),
        grid=(FFT_N2 // SUBLANE,),
        in_specs=[pl.BlockSpec(cinv.shape, lambda j: (0, 0)),
                  pl.BlockSpec((b2d.shape[0], SUBLANE * c), lambda j: (0, j)),
                  slab, pl.BlockSpec((1, c), lambda j: (0, 0)), slab if u_is_3d else wide],
        out_specs=wide,
        out_shape=jax.ShapeDtypeStruct((n_outer, FFT_N2 * c), F32),
        compiler_params=_params(("parallel",)),
        name="idft_outer_gate",
    )(cinv, b2d, gate3, skip_row, u)


def long_conv_two_stage(tabs, taps, v, x1, x2, skip, length):
    c = v.shape[1]
    n2, n1, kp = FFT_N2, tabs["n1"], tabs["kp"]
    as3 = lambda a: a.reshape(a.shape[0] // n2, n2, c)
    spectrum = lambda a2d: bmm_k1(tabs["g_fwd"], a2d.reshape(2, kp, n2, c), in_part_major=True, out_part_major=False)

    spectra = [spectrum(mm([(tabs["f1"], taps[o])], BF16, 2 * kp, 2048, name="dft_outer")) for o in range(HY_ORDER)]
    v3 = as3(v)
    u = v3
    for o, gate in enumerate((x1, x2)):
        if u.ndim == 3:
            a = dft_outer3(tabs["f1_half"], u, n1 // 2)
        else:
            a = mm([(tabs["f1_half"], u)], BF16, 2 * kp, 2048, name="dft_outer")
        bt = bmm_k1(tabs["g_inv"], spectrum(a), spectra[o], in_part_major=False, out_part_major=True)
        u = idft_gate3(tabs["cinv"], bt.reshape(2 * kp, n2 * c), as3(gate), skip[o].reshape(1, c), u)
    return u.reshape(length, c)


def long_conv_one_stage(tabs, taps, v, x1, x2, skip):
    length, c = v.shape
    kp = tabs["kp"]
    u = v
    for o, gate in enumerate((x1, x2)):
        hs = mm([(tabs["f"], taps[o])], F32, 2 * kp, c, name="ctx_dft").reshape(2, kp, c)
        xs = mm([(tabs["f_half"], u)], F32, 2 * kp, c, name="ctx_dft").reshape(2, kp, c)
        ys = cmul(xs, hs).reshape(2 * kp, c)
        u = mm([(tabs["cinv"], ys)], F32, length, c, epi=(gate, skip[o].reshape(1, c), u), name="ctx_idft_gate")
    return u


def _flash_kernel(lam_ref, qt_ref, k_ref, vt_ref, sub_ref, o_ref, m_ref, excess_ref, acc_ref, *, kv, seq, ctx_len,
                  out_scale):
    i = pl.program_id(1)
    last_q = pl.num_programs(1) - 1
    tq = qt_ref.shape[1]
    d = DA_HEAD_DIM
    dv = 2 * DA_HEAD_DIM
    n_chunks = k_ref.shape[0] // kv
    acc_ref[...] = jnp.zeros_like(acc_ref)

    def scores(off, rows, c, masked):
        s = jnp.dot(k_ref[pl.ds(off, rows), c * d:(c + 1) * d], qt_ref[c * d:(c + 1) * d, :],
                    preferred_element_type=F32)
        if masked:
            key = off + lax.broadcasted_iota(jnp.int32, (rows, 1), 0)
            lane = lax.broadcasted_iota(jnp.int32, (1, tq), 1)
            s = s + jnp.where(key < seq, NEG_BIG, 0.0) * jnp.where(lane >= tq - ctx_len, 1.0, 0.0)
        return s

    def exact_step(off, c, masked):
        s = scores(off, kv, c, masked)
        m_old = m_ref[c]
        m_new = jnp.maximum(m_old, jnp.max(s, axis=0, keepdims=True))
        pr = jnp.exp2(s - m_new).astype(BF16)
        acc_ref[c] = jnp.exp2(m_old - m_new) * acc_ref[c] + jnp.dot(vt_ref[:, pl.ds(off, kv)], pr,
                                                                   preferred_element_type=F32)
        m_ref[c] = m_new

    def lazy_step(off, c, masked):
        s = scores(off, kv, c, masked)
        m_old = m_ref[c]
        m_chunk = jnp.max(s, axis=0, keepdims=True)
        pv = jnp.dot(vt_ref[:, pl.ds(off, kv)], jnp.exp2(s - m_old).astype(BF16), preferred_element_type=F32)
        m_new = jnp.maximum(m_old, m_chunk)
        acc_ref[c] = jnp.exp2(m_old - m_new) * (acc_ref[c] + pv)
        m_ref[c] = m_new
        excess_ref[c] = jnp.maximum(excess_ref[c], m_chunk - m_old)

    def all_chunks(step, masked):
        def body(kc, carry):
            off = pl.multiple_of(kc * kv, kv)
            for c in range(2):
                step(off, c, masked)
            return carry

        lax.fori_loop(0, n_chunks, body, 0)

    def run(masked):
        for c in range(2):
            m0 = jnp.max(scores(0, FLASH_INIT_KEYS, c, False), axis=0, keepdims=True)
            if masked:
                lane = lax.broadcasted_iota(jnp.int32, (1, tq), 1)
                m_ctx = jnp.max(scores(seq, FLASH_INIT_KEYS, c, False), axis=0, keepdims=True)
                m0 = jnp.where(lane >= tq - ctx_len, m_ctx, m0)
            m_ref[c] = m0
        excess_ref[...] = jnp.full_like(excess_ref, NEG_BIG)
        all_chunks(lazy_step, masked)

        @pl.when(jnp.max(excess_ref[...]) > FLASH_LAZY_HEADROOM)
        def _():
            m_ref[...] = jnp.full_like(m_ref, NEG_BIG)
            acc_ref[...] = jnp.zeros_like(acc_ref)
            all_chunks(exact_step, masked)

    @pl.when(i != last_q)
    def _():
        run(False)

    @pl.when(i == last_q)
    def _():
        run(True)

    a0 = acc_ref[0, :dv, :] / acc_ref[0, dv:dv + 1, :]
    a1 = acc_ref[1, :dv, :] / acc_ref[1, dv:dv + 1, :]
    o = (a0 - lam_ref[0] * a1).T
    o = o * lax.rsqrt(jnp.mean(o * o, axis=-1, keepdims=True) + 1e-5) * sub_ref[...]
    o_ref[...] = (o * out_scale).astype(o_ref.dtype)


def _rope_da_kernel(p_ref, cos_ref, sin_ref, qt_ref, k_ref, vt_ref):
    cos = cos_ref[...]
    sin = sin_ref[...]
    hw = 2 * DA_HEAD_DIM
    lane = lax.broadcasted_iota(jnp.int32, cos.shape, 1)
    first_half = (lane % DA_HEAD_DIM) < DA_HEAD_DIM // 2

    def rotated(b):
        x = p_ref[:, b * LANE:(b + 1) * LANE]
        rot = jnp.where(first_half, pltpu.roll(x, LANE - DA_HEAD_DIM // 2, 1), pltpu.roll(x, DA_HEAD_DIM // 2, 1))
        return x * cos + rot * sin

    ones = jnp.ones((FLASH_ONES_ROWS, cos.shape[0]), BF16)
    for h in range(DA_HEADS):
        qt_ref[h * hw:(h + 1) * hw, :] = (rotated(h) * (LOG2_E * DA_HEAD_DIM ** -0.5)).T.astype(BF16)
        k_ref[:, h * hw:(h + 1) * hw] = rotated(DA_HEADS + h).astype(BF16)
        base = h * (hw + FLASH_ONES_ROWS)
        vt_ref[base:base + hw, :] = p_ref[:, (2 * DA_HEADS + h) * LANE:(2 * DA_HEADS + h + 1) * LANE].T.astype(BF16)
        vt_ref[base + hw:base + hw + FLASH_ONES_ROWS, :] = ones


def rope_da(p, cos, sin):
    n_rows = p.shape[0]
    assert 2 * DA_HEAD_DIM == LANE
    vt_rows = DA_HEADS * (LANE + FLASH_ONES_ROWS)
    return pl.pallas_call(
        _rope_da_kernel,
        grid=(n_rows // ROW_TILE,),
        in_specs=[pl.BlockSpec((ROW_TILE, 3 * DA_WIDTH), lambda i: (i, 0)),
                  pl.BlockSpec((ROW_TILE, LANE), lambda i: (i, 0)),
                  pl.BlockSpec((ROW_TILE, LANE), lambda i: (i, 0))],
        out_specs=[pl.BlockSpec((DA_WIDTH, ROW_TILE), lambda i: (0, i)),
                   pl.BlockSpec((ROW_TILE, DA_WIDTH), lambda i: (i, 0)),
                   pl.BlockSpec((vt_rows, ROW_TILE), lambda i: (0, i))],
        out_shape=[jax.ShapeDtypeStruct((DA_WIDTH, n_rows), BF16),
                   jax.ShapeDtypeStruct((n_rows, DA_WIDTH), BF16),
                   jax.ShapeDtypeStruct((vt_rows, n_rows), BF16)],
        compiler_params=_params(("parallel",)),
        name="rope_da",
    )(p, cos, sin)


def diff_attention(qt, k, vt, lam_full, subln, *, seq, ctx_len, lambda_init):
    n_rows = k.shape[0]
    tq = _token_tile(n_rows)
    hw = 2 * DA_HEAD_DIM
    ones_rows = FLASH_ONES_ROWS
    return pl.pallas_call(
        functools.partial(_flash_kernel, kv=tq, seq=seq, ctx_len=ctx_len, out_scale=1.0 - lambda_init),
        grid=(DA_HEADS, n_rows // tq),
        in_specs=[pl.BlockSpec(memory_space=pltpu.SMEM),
                  pl.BlockSpec((hw, tq), lambda h, i: (h, i)),
                  pl.BlockSpec((n_rows, hw), lambda h, i: (0, h)),
                  pl.BlockSpec((hw + ones_rows, n_rows), lambda h, i: (h, 0)),
                  pl.BlockSpec((1, hw), lambda h, i: (0, 0))],
        out_specs=pl.BlockSpec((tq, hw), lambda h, i: (i, h)),
        out_shape=jax.ShapeDtypeStruct((n_rows, DA_WIDTH), BF16),
        scratch_shapes=[pltpu.VMEM((2, 1, tq), F32), pltpu.VMEM((2, 1, tq), F32),
                        pltpu.VMEM((2, hw + ones_rows, tq), F32)],
        compiler_params=pltpu.CompilerParams(dimension_semantics=("parallel", "parallel"),
                                             vmem_limit_bytes=FLASH_VMEM_LIMIT),
        name="diff_attention",
    )(lam_full.reshape(1), qt, k, vt, subln.reshape(1, hw))


def _route_kernel(lg_ref, b_ref, tri_ref, eidx_ref, w_ref, rank_ref, cnt_ref, carry_ref):
    t = lg_ref.shape[1]

    @pl.when(pl.program_id(0) == 0)
    def _():
        carry_ref[...] = jnp.zeros_like(carry_ref)

    scores = jax.nn.sigmoid(lg_ref[...])
    choice = (scores + b_ref[...]).reshape(N_GROUPS, GROUP_SIZE, t)
    s3 = scores.reshape(N_GROUPS, GROUP_SIZE, t)
    member = lax.broadcasted_iota(jnp.int32, choice.shape, 1)
    group = lax.broadcasted_iota(jnp.int32, (N_GROUPS, 1, t), 0)
    expert = lax.broadcasted_iota(jnp.int32, choice.shape, 0) * GROUP_SIZE + member
    neg_inf = -jnp.inf
    m1 = jnp.max(choice, axis=1, keepdims=True)
    first = jnp.min(jnp.where(choice == m1, member, GROUP_SIZE), axis=1, keepdims=True)
    m2 = jnp.max(jnp.where(member == first, neg_inf, choice), axis=1, keepdims=True)
    gscore = m1 + m2
    gsel = jnp.zeros(gscore.shape, F32)
    for _ in range(TOPK_GROUPS):
        m = jnp.max(gscore, axis=0, keepdims=True)
        f = jnp.min(jnp.where(gscore == m, group, N_GROUPS), axis=0, keepdims=True)
        hit = group == f
        gsel = jnp.where(hit, 1.0, gsel)
        gscore = jnp.where(hit, neg_inf, gscore)
    cand = jnp.where(gsel > 0.0, choice, neg_inf)
    esel = jnp.zeros(choice.shape, F32)
    picks = []
    for _ in range(TOP_K):
        m = jnp.max(jnp.max(cand, axis=1, keepdims=True), axis=0, keepdims=True)
        f = jnp.min(jnp.min(jnp.where(cand == m, expert, N_EXPERTS), axis=1, keepdims=True), axis=0, keepdims=True)
        hit = expert == f
        esel = jnp.where(hit, 1.0, esel)
        cand = jnp.where(hit, neg_inf, cand)
        picks.append(f)
    w = s3 * esel
    denom = jnp.sum(jnp.sum(w, axis=1, keepdims=True), axis=0, keepdims=True) + 1e-20
    w = w / denom * ROUTED_SCALE
    sel = esel.reshape(N_EXPERTS, t)
    before = jnp.dot(sel.astype(BF16), tri_ref[...], preferred_element_type=F32) + carry_ref[...]
    before = before.reshape(N_GROUPS, GROUP_SIZE, t)
    pick = lambda a, hit: jnp.sum(jnp.sum(jnp.where(hit, a, 0.0), axis=1, keepdims=True), axis=0).reshape(1, t)
    for k, f in enumerate(picks):
        hit = expert == f
        eidx_ref[k:k + 1, :] = f.reshape(1, t)
        w_ref[k:k + 1, :] = pick(w, hit)
        rank_ref[k:k + 1, :] = pick(before, hit).astype(jnp.int32)
    carry_ref[...] += jnp.sum(sel, axis=1, keepdims=True)
    cnt_ref[...] = carry_ref[...]


def route(logits_t, bias, lo, hi):
    t = _token_tile(logits_t.shape[1])
    n = hi - lo
    tile0 = lo // t
    tri = (jnp.arange(t)[:, None] < jnp.arange(t)[None, :]).astype(BF16)
    tok = lambda dt: jax.ShapeDtypeStruct((TOP_K, n), dt)
    tok_spec = pl.BlockSpec((TOP_K, t), lambda i: (0, i))
    return pl.pallas_call(
        _route_kernel,
        grid=(n // t,),
        in_specs=[pl.BlockSpec((N_EXPERTS, t), lambda i: (0, tile0 + i)),
                  pl.BlockSpec((N_EXPERTS, 1), lambda i: (0, 0)),
                  pl.BlockSpec((t, t), lambda i: (0, 0))],
        out_specs=[tok_spec, tok_spec, tok_spec, pl.BlockSpec((N_EXPERTS, 1), lambda i: (0, 0))],
        out_shape=[tok(jnp.int32), tok(F32), tok(jnp.int32), jax.ShapeDtypeStruct((N_EXPERTS, 1), F32)],
        scratch_shapes=[pltpu.VMEM((N_EXPERTS, 1), F32)],
        compiler_params=_params(("arbitrary",)),
        name="route",
    )(logits_t, bias.reshape(N_EXPERTS, 1), tri)


def _slot_kernel(start_ref, eidx_ref, rank_ref, dest_ref):
    e = eidx_ref[...]
    d = rank_ref[...]
    for x in range(N_EXPERTS):
        d = d + jnp.where(e == x, start_ref[x], 0)
    dest_ref[...] = d


def slot_index(pad_start, eidx, rank):
    n = eidx.shape[1]
    t = _token_tile(n)
    spec = pl.BlockSpec((TOP_K, t), lambda i: (0, i))
    return pl.pallas_call(
        _slot_kernel,
        grid=(n // t,),
        in_specs=[pl.BlockSpec(memory_space=pltpu.SMEM), spec, spec],
        out_specs=spec,
        out_shape=jax.ShapeDtypeStruct((TOP_K, n), jnp.int32),
        compiler_params=_params(("parallel",)),
        name="slot_index",
    )(pad_start, eidx, rank)


def _sc_worker():
    return lax.axis_index("s") * SC_CORES + lax.axis_index("c")


def sc_dispatch(h, dest3, n_slots, row0):
    d = h.shape[1]
    n_win = dest3.shape[0]
    mesh = plsc.VectorSubcoreMesh(core_axis_name="c", subcore_axis_name="s")

    @functools.partial(
        pl.kernel, mesh=mesh, out_type=jax.ShapeDtypeStruct((n_slots, d), h.dtype),
        scratch_types=[pltpu.VMEM((TOP_K, SC_WINDOW), jnp.int32), pltpu.VMEM((SC_WINDOW, d), h.dtype),
                       pltpu.SemaphoreType.DMA])
    def k(h_hbm, dest_hbm, out_hbm, idx_v, rows_v, sem):
        wid = _sc_worker()

        @pl.loop(0, -(-n_win // SC_WORKERS))
        def _(it):
            w = it * SC_WORKERS + wid

            @pl.when(w < n_win)
            def _():
                pltpu.sync_copy(dest_hbm.at[w], idx_v)
                pltpu.sync_copy(h_hbm.at[pl.ds(row0 + w * SC_WINDOW, SC_WINDOW)], rows_v)
                copies = [pltpu.async_copy(rows_v, out_hbm.at[idx_v.at[j]], sem) for j in range(TOP_K)]
                for c in copies:
                    c.wait()

    return k(h, dest3)


def sc_combine_gather(y, dest3):
    d = y.shape[1]
    n_win = dest3.shape[0]
    n = n_win * SC_WINDOW
    mesh = plsc.VectorSubcoreMesh(core_axis_name="c", subcore_axis_name="s")

    @functools.partial(
        pl.kernel, mesh=mesh, out_type=jax.ShapeDtypeStruct((TOP_K, n, d), y.dtype),
        scratch_types=[pltpu.VMEM((TOP_K, SC_WINDOW), jnp.int32), pltpu.VMEM((2, SC_WINDOW, d), y.dtype),
                       pltpu.SemaphoreType.DMA, pltpu.SemaphoreType.DMA,
                       pltpu.SemaphoreType.DMA, pltpu.SemaphoreType.DMA])
    def k(y_hbm, dest_hbm, out_hbm, idx_v, rows_v, gsem0, gsem1, osem0, osem1):
        wid = _sc_worker()
        gsem, osem = (gsem0, gsem1), (osem0, osem1)

        @pl.loop(0, -(-n_win // SC_WORKERS))
        def _(it):
            w = it * SC_WORKERS + wid

            @pl.when(w < n_win)
            def _():
                pltpu.sync_copy(dest_hbm.at[w], idx_v)
                gather = lambda j: pltpu.async_copy(y_hbm.at[idx_v.at[j]], rows_v.at[j % 2], gsem[j % 2])
                g = [None] * TOP_K
                o = [None] * TOP_K
                g[0] = gather(0)
                for j in range(TOP_K):
                    if j + 1 < TOP_K:
                        if j >= 1:
                            o[j - 1].wait()
                        g[j + 1] = gather(j + 1)
                    g[j].wait()
                    o[j] = pltpu.async_copy(rows_v.at[j % 2], out_hbm.at[j, pl.ds(w * SC_WINDOW, SC_WINDOW)],
                                            osem[j % 2])
                o[TOP_K - 2].wait()
                o[TOP_K - 1].wait()

    return k(y, dest3)


def _expert_ffn_kernel(be_ref, bv_ref, bs_ref, x_ref, wg_ref, wu_ref, wd_ref, o_ref, wg_s, wu_s, wd_s):
    b = pl.program_id(0)
    valid = bv_ref[b]
    new_expert = (b == 0) | (be_ref[b] != be_ref[jnp.maximum(b - 1, 0)])

    @pl.when(new_expert)
    def _():
        wg_s[...] = wg_ref[0, 0].astype(BF16)
        wu_s[...] = wu_ref[0, 0].astype(BF16)
        wd_s[...] = wd_ref[0, 0].astype(BF16)

    sub = x_ref.shape[0] // MOE_SUB_BLOCKS

    def sub_block(r):
        row = lax.broadcasted_iota(jnp.int32, (sub, 1), 0) + r * sub
        rows = pl.ds(r * sub, sub)
        lo, hi = _unpack_bf16_pair(jnp.where(row < valid, x_ref[rows, :], 0))
        x = jnp.concatenate([lo.astype(BF16), hi.astype(BF16)], axis=1)
        a = jnp.dot(x, wg_s[...], preferred_element_type=F32)
        a = a * jax.nn.sigmoid(a) * jnp.dot(x, wu_s[...], preferred_element_type=F32)
        y = jnp.dot(a.astype(BF16), wd_s[...], preferred_element_type=F32)
        half = y.shape[1] // 2
        o_ref[rows, :] = _pack_bf16_pair(y[:, :half], y[:, half:])

    for live in range(1, MOE_SUB_BLOCKS + 1):
        upper = valid <= live * sub if live < MOE_SUB_BLOCKS else True

        @pl.when((valid > (live - 1) * sub) & upper)
        def _():
            for r in range(live):
                sub_block(r)


def expert_ffn(xg, block_expert, block_valid, block_src, wg, wu, wd, layer):
    n_slots, dp = xg.shape
    d, f = wg.shape[-2:]
    grid_spec = pltpu.PrefetchScalarGridSpec(
        num_scalar_prefetch=3,
        grid=(n_slots // MOE_BLOCK,),
        in_specs=[pl.BlockSpec((MOE_BLOCK, dp), lambda b, be, bv, bs: (bs[b], 0)),
                  pl.BlockSpec((1, 1, d, f), lambda b, be, bv, bs: (layer, be[b], 0, 0)),
                  pl.BlockSpec((1, 1, d, f), lambda b, be, bv, bs: (layer, be[b], 0, 0)),
                  pl.BlockSpec((1, 1, f, d), lambda b, be, bv, bs: (layer, be[b], 0, 0))],
        out_specs=pl.BlockSpec((MOE_BLOCK, dp), lambda b, be, bv, bs: (bs[b], 0)),
        scratch_shapes=[pltpu.VMEM((d, f), BF16), pltpu.VMEM((d, f), BF16), pltpu.VMEM((f, d), BF16)],
    )
    return pl.pallas_call(
        _expert_ffn_kernel,
        grid_spec=grid_spec,
        out_shape=jax.ShapeDtypeStruct((n_slots, dp), jnp.int32),
        compiler_params=_params(("arbitrary",)),
        name="expert_ffn",
    )(block_expert, block_valid, block_src, xg, wg, wu, wd)


def _shared_kernel(h_ref, swg_ref, swu_ref, swd_ref, xs_ref, mods_ref, o_ref, *, gate_idx, seq):
    h = h_ref[...]
    a = jnp.dot(h, swg_ref[...], preferred_element_type=F32)
    a = a * jax.nn.sigmoid(a) * jnp.dot(h, swu_ref[...], preferred_element_type=F32)
    y = jnp.dot(a.astype(BF16), swd_ref[...], preferred_element_type=F32)
    tm = h.shape[0]
    o_ref[...] = xs_ref[...] + _row_mod(mods_ref, gate_idx, pl.program_id(0) * tm, tm, seq) * y


def shared_expert(h, swg, swu, swd, residual):
    xs, mods, gate_idx, seq = residual
    n, d = h.shape
    f = swg.shape[-1]
    tm = _token_tile(n)
    row = pl.BlockSpec((tm, d), lambda i: (i, 0))
    return pl.pallas_call(
        functools.partial(_shared_kernel, gate_idx=gate_idx, seq=seq),
        grid=(n // tm,),
        in_specs=[row, pl.BlockSpec((d, f), lambda i: (0, 0)), pl.BlockSpec((d, f), lambda i: (0, 0)),
                  pl.BlockSpec((f, d), lambda i: (0, 0)), row, pl.BlockSpec((2, 6, d), lambda i: (0, 0, 0))],
        out_specs=row,
        out_shape=jax.ShapeDtypeStruct((n, d), F32),
        compiler_params=_params(("parallel",)),
        name="shared_expert",
    )(h, swg, swu, swd, xs, mods)


def _combine_kernel(yg_ref, w_ref, base_ref, mods_ref, *rest, tile0, gate_idx, seq):
    o_ref = rest[-1]
    tm, d = base_ref.shape
    half = d // 2
    acc_lo = jnp.zeros((tm, half), F32)
    acc_hi = jnp.zeros((tm, half), F32)
    wt = w_ref[...].T
    for k in range(TOP_K):
        lo, hi = _unpack_bf16_pair(yg_ref[k])
        acc_lo = acc_lo + wt[:, k:k + 1] * lo
        acc_hi = acc_hi + wt[:, k:k + 1] * hi
    gate = _row_mod(mods_ref, gate_idx, (tile0 + pl.program_id(0)) * tm, tm, seq)
    o_ref[:, :half] = base_ref[:, :half] + gate[:, :half] * acc_lo
    o_ref[:, half:] = base_ref[:, half:] + gate[:, half:] * acc_hi


def combine(yg, w, base, residual, lo, prev):
    _, mods, gate_idx, seq = residual
    n_all, d = base.shape
    n = w.shape[1]
    tm = ROW_TILE
    tile0 = lo // tm
    in_specs = [pl.BlockSpec((TOP_K, tm, d // 2), lambda i: (0, i, 0)),
                pl.BlockSpec((TOP_K, tm), lambda i: (0, i)),
                pl.BlockSpec((tm, d), lambda i: (tile0 + i, 0)),
                pl.BlockSpec((2, 6, d), lambda i: (0, 0, 0))]
    args = [yg, w, base, mods]
    aliases = {}
    if prev is not None:
        in_specs.append(pl.BlockSpec(memory_space=pl.ANY))
        args.append(prev)
        aliases = {len(args) - 1: 0}
    return pl.pallas_call(
        functools.partial(_combine_kernel, tile0=tile0, gate_idx=gate_idx, seq=seq),
        grid=(n // tm,),
        in_specs=in_specs,
        out_specs=pl.BlockSpec((tm, d), lambda i: (tile0 + i, 0)),
        out_shape=jax.ShapeDtypeStruct((n_all, d), F32),
        input_output_aliases=aliases,
        compiler_params=_params(("parallel",)),
        name="moe_combine",
    )(*args)


def moe(h, h_packed, logits_t, bias, wg, wu, wd, layer, swg, swu, swd, residual):
    n = h.shape[0]
    t = _token_tile(n)
    cut = (n // t + 1) // 2 * t
    base = shared_expert(h, swg.astype(BF16), swu.astype(BF16), swd.astype(BF16), residual)
    staged = [_moe_experts(h_packed, logits_t, bias, wg, wu, wd, layer, lo, hi) for lo, hi in ((0, cut), (cut, n))]
    out = None
    for (yg, w), lo in zip(staged, (0, cut)):
        out = combine(yg, w, base, residual, lo, out)
    return out


def _moe_experts(h_packed, logits_t, bias, wg, wu, wd, layer, lo, hi):
    n = hi - lo
    eidx, w, rank, counts = route(logits_t, bias, lo, hi)
    counts = counts.reshape(N_EXPERTS).astype(jnp.int32)
    padded = (counts + MOE_BLOCK - 1) // MOE_BLOCK * MOE_BLOCK
    pad_end = jnp.cumsum(padded)
    pad_start = pad_end - padded
    n_slots = n * TOP_K + N_EXPERTS * MOE_BLOCK
    starts = jnp.arange(n_slots // MOE_BLOCK, dtype=jnp.int32) * MOE_BLOCK
    owner = jnp.sum((pad_end[None, :] <= starts[:, None]).astype(jnp.int32), axis=1)
    block_src = jnp.minimum(jnp.arange(starts.shape[0], dtype=jnp.int32), jnp.maximum(pad_end[-1] // MOE_BLOCK - 1, 0))
    owner = owner[block_src]
    block_expert = jnp.minimum(owner, N_EXPERTS - 1)
    member = (block_expert[:, None] == jnp.arange(N_EXPERTS, dtype=jnp.int32)[None, :]).astype(jnp.int32)
    left = jnp.sum(member * (counts + pad_start)[None, :], axis=1) - starts
    block_valid = jnp.clip(left, 0, MOE_BLOCK).astype(jnp.int32)
    dest = slot_index(pad_start.astype(jnp.int32), eidx, rank)
    dest3 = dest.reshape(TOP_K, n // SC_WINDOW, SC_WINDOW).transpose(1, 0, 2)
    xg = sc_dispatch(h_packed, dest3, n_slots, lo)
    y = expert_ffn(xg, block_expert, block_valid, block_src.astype(jnp.int32), wg, wu, wd, layer)
    return sc_combine_gather(y, dest3), w


def mixer_ab(h, w_in, w_out, decay_logit, conv_w, conv_b, w1, b1, f1, w2, b2, f2, w3, skip, rope, dft, residual, *,
             seq, ctx_len):
    n_rows = h.shape[0]
    tm = _token_tile(n_rows)
    assert RET_DK == LANE
    p = mm([(h, w_in.astype(BF16))], F32, tm, RET_QK, rope=(rope[0], rope[1], (1.0, RET_DK ** -0.5)),
           name="ab_in_proj")
    log_g = jax.nn.log_sigmoid(decay_logit.astype(F32))
    ret = retention(p, log_g, jnp.exp(RET_CHUNK * log_g), seq=seq)
    v, x1, x2 = shortconv(p, conv_w, conv_b, seq=seq)
    filt = (w1, b1, f1, w2, b2, f2, w3)
    hy_x = long_conv_two_stage(dft["x"], hyena_filter_taps_2d(seq, *filt), v, x1, x2, skip, seq)
    hy_c = long_conv_one_stage(dft["c"], hyena_filter_taps(ctx_len, *filt), v[seq:], x1[seq:], x2[seq:], skip)
    hy = jnp.concatenate([hy_x, hy_c], axis=0)
    w_out = w_out.astype(BF16)
    return mm([(ret, w_out[:RET_V]), (hy, w_out[RET_V:])], F32, tm, 512, residual=residual, name="ab_out_proj")


def mixer_da(h, w_in, w_out, lam, subln, lambda_init, rope, residual, *, seq, ctx_len):
    n_rows = h.shape[0]
    tm = _token_tile(n_rows)
    p = mm([(h, w_in.astype(BF16))], F32, tm, 512, name="da_in_proj")
    qt, k, vt = rope_da(p, rope[0], rope[1])
    lam_f = lam.astype(F32)
    lam_full = jnp.exp(jnp.sum(lam_f[0] * lam_f[1])) - jnp.exp(jnp.sum(lam_f[2] * lam_f[3])) + lambda_init
    o = diff_attention(qt, k, vt, lam_full, subln, seq=seq, ctx_len=ctx_len, lambda_init=lambda_init)
    return mm([(o, w_out.astype(BF16))], F32, tm, 512, residual=residual, name="da_out_proj")


def kernel(x, c, ctx, c_ctx, w_ada, b_ada, norm_mix, norm_ffn, ab_w_in, ab_w_out, ret_decay_logit, hy_conv_w, hy_conv_b, hy_w1, hy_b1, hy_freq1, hy_w2, hy_b2, hy_freq2, hy_w3, hy_skip, da_w_in, da_w_out, da_lambda, da_subln, router_w, router_b, exp_w_gate, exp_w_up, exp_w_down, sh_w_gate, sh_w_up, sh_w_down, norm_final):
    batch, seq, d = x.shape
    ctx_len = ctx.shape[1]
    assert batch == 1 and seq % ROW_TILE == 0 and ctx_len == ROW_TILE
    depth = w_ada.shape[0]
    n_rows = seq + ctx_len

    xs = jnp.concatenate([x[0], ctx[0]], axis=0)
    cv = jnp.zeros((SUBLANE, d), F32).at[0].set(c_ctx).at[1].set(c[0])
    mods = adaln(cv, w_ada, b_ada)[:, :2].reshape(depth, 2, 6, d)

    rope_ret = rope_tables(seq, ctx_len, RET_DK)
    rope_da = rope_tables(seq, ctx_len, DA_HEAD_DIM)
    dft = dict(x=dft_tables_two_stage(2 * seq), c=dft_tables_one_stage(2 * ctx_len))
    common = dict(n_rows=n_rows, seq=seq)

    for i in range(depth):
        j = i // 2
        (h,) = norm_mod(xs, norm_mix[i], mods=mods[i], shift_idx=0, scale_idx=1, **common)
        residual = (xs, mods[i], 2, seq)
        if i % 2 == 0:
            xs = mixer_ab(h, ab_w_in[j], ab_w_out[j], ret_decay_logit[j], hy_conv_w[j], hy_conv_b[j], hy_w1[j],
                          hy_b1[j], hy_freq1[j], hy_w2[j], hy_b2[j], hy_freq2[j], hy_w3[j], hy_skip[j], rope_ret, dft,
                          residual, seq=seq, ctx_len=ctx_len)
        else:
            lambda_init = 0.8 - 0.6 * math.exp(-0.3 * i)
            xs = mixer_da(h, da_w_in[j], da_w_out[j], da_lambda[j], da_subln[j], lambda_init, rope_da, residual,
                          seq=seq, ctx_len=ctx_len)
        h, logits_t, h_packed = norm_mod(xs, norm_ffn[i], mods=mods[i], shift_idx=3, scale_idx=4,
                                         router_wt=router_w[i].T, **common)
        xs = moe(h, h_packed, logits_t, router_b[i], exp_w_gate, exp_w_up, exp_w_down, i,
                 sh_w_gate[i], sh_w_up[i], sh_w_down[i], (xs, mods[i], 5, seq))
    (out,) = norm_mod(xs, norm_final, n_rows=seq, seq=seq, out_dtype=F32)
    return out[None]
```
